```python
import math
import jax, jax.numpy as jnp
from jax import lax
import numpy as np

D_MODEL = 1024
BATCH = 16
SEQ = 2048
DEPTH = 1
DEC_BATCH = 128
DEC_SEQ = 1
PAST_LEN = 16384
PAGE_SIZE = 128

MIX_WIDTH = D_MODEL
HEAD_DIM = 64
ATTN_WIDTH = MIX_WIDTH // 2
N_HEADS = ATTN_WIDTH // HEAD_DIM
KV_HEADS = 2
REP = N_HEADS // KV_HEADS
WINDOW = 128
ATTN_BLOCK = WINDOW
ATTN_SCALE = HEAD_DIM ** -0.5
ROPE_THETA = 10000.0
SSM_WIDTH = MIX_WIDTH - ATTN_WIDTH
SSM_HEAD_DIM = 64
SSM_HEADS = SSM_WIDTH // SSM_HEAD_DIM
SSM_GROUPS = 2
HEADS_PER_GROUP = SSM_HEADS // SSM_GROUPS
SSM_STATE = 128
CONV_K = 4
CONV_DIM = SSM_WIDTH + 2 * SSM_GROUPS * SSM_STATE
SSD_CHUNK = 128
N_EGROUPS = 4
EXP_PER_GROUP = 8
N_EXPERTS = N_EGROUPS * EXP_PER_GROUP
TOP_K = 2
EXPERT_FF = D_MODEL // 8
EPS = 1e-6
Q_END = N_HEADS * HEAD_DIM
K_END = Q_END + KV_HEADS * HEAD_DIM
V_END = K_END + KV_HEADS * HEAD_DIM
Z_END = V_END + SSM_WIDTH
XBC_END = Z_END + CONV_DIM
IN_DIM = XBC_END + SSM_HEADS

kernel_name = "hymba_swa_sink_ssd_hiermoe_step"


def rms_norm(x, g):
    xf = x.astype(jnp.float32)
    xf = xf * lax.rsqrt(jnp.mean(xf * xf, axis=-1, keepdims=True) + EPS)
    return (xf * g.astype(jnp.float32)).astype(x.dtype)


def rope(x, pos):
    inv = 1.0 / (ROPE_THETA ** (jnp.arange(0, HEAD_DIM, 2, dtype=jnp.float32) / HEAD_DIM))
    ang = pos.astype(jnp.float32)[:, None] * inv[None, :]
    cos = jnp.cos(ang)[:, None, :]
    sin = jnp.sin(ang)[:, None, :]
    xf = x.astype(jnp.float32)
    x1, x2 = xf[..., :HEAD_DIM // 2], xf[..., HEAD_DIM // 2:]
    return jnp.concatenate([x1 * cos - x2 * sin, x2 * cos + x1 * sin], axis=-1).astype(x.dtype)


def sink_attend(q, k, v, mask, sinks):
    s = jnp.einsum('...qkrd,...skd->...krqs', q, k).astype(jnp.float32) * ATTN_SCALE
    s = jnp.where(mask, s, -jnp.inf)
    sink = sinks.astype(jnp.float32).reshape(KV_HEADS, REP, 1, 1)
    m = jnp.maximum(jnp.max(s, axis=-1, keepdims=True), sink)
    p = jnp.exp(s - m)
    p = p / (jnp.sum(p, axis=-1, keepdims=True) + jnp.exp(sink - m))
    return jnp.einsum('...krqs,...skd->...qkrd', p.astype(v.dtype), v)


def swa_banded(q, k, v, sinks):
    bt, L = q.shape[:2]
    nb = L // ATTN_BLOCK
    qb = q.reshape(bt, nb, ATTN_BLOCK, KV_HEADS, REP, HEAD_DIM)

    def with_prev(t):
        tb = t.reshape(bt, nb, ATTN_BLOCK, KV_HEADS, HEAD_DIM)
        prev = jnp.concatenate([jnp.zeros_like(tb[:, :1]), tb[:, :-1]], axis=1)
        return jnp.concatenate([prev, tb], axis=2)

    kb, vb = with_prev(k), with_prev(v)
    q_off = jnp.arange(ATTN_BLOCK)[:, None]
    k_off = jnp.arange(2 * ATTN_BLOCK)[None, :] - ATTN_BLOCK
    rel = q_off - k_off
    band = (rel >= 0) & (rel < WINDOW)
    in_seq = (jnp.arange(nb)[:, None, None] > 0) | (k_off[None] >= 0)
    mask = (band[None] & in_seq)[:, None, None]
    o = sink_attend(qb, kb, vb, mask, sinks)
    return o.reshape(bt, L, N_HEADS * HEAD_DIM)


def swa_cached(q, k, v, past_k, past_v, pos, sinks):
    bt, L = q.shape[:2]
    wc = past_k.shape[1]
    k_all = jnp.concatenate([past_k.astype(k.dtype), k], axis=1)
    v_all = jnp.concatenate([past_v.astype(v.dtype), v], axis=1)
    k_pos = jnp.concatenate([PAST_LEN - wc + jnp.arange(wc, dtype=jnp.int32), pos])
    rel = pos[:, None] - k_pos[None, :]
    mask = (rel >= 0) & (rel < WINDOW)
    o = sink_attend(q, k_all, v_all, mask, sinks)
    return o.reshape(bt, L, N_HEADS * HEAD_DIM), k_all[:, -wc:], v_all[:, -wc:]


def ssd_scan(x, dt, a, b_in, c_in, d_skip, h0):
    bt, L = x.shape[:2]
    cl = SSD_CHUNK if L % SSD_CHUNK == 0 else L
    nc = L // cl
    xr = x.astype(jnp.float32).reshape(bt, nc, cl, SSM_GROUPS, HEADS_PER_GROUP, SSM_HEAD_DIM)
    dtr = dt.reshape(bt, nc, cl, SSM_GROUPS, HEADS_PER_GROUP)
    xdt = xr * dtr[..., None]
    bc = b_in.astype(jnp.float32).reshape(bt, nc, cl, SSM_GROUPS, SSM_STATE)
    cc = c_in.astype(jnp.float32).reshape(bt, nc, cl, SSM_GROUPS, SSM_STATE)
    a_cum = jnp.cumsum(dtr * a, axis=2)
    seg = a_cum[:, :, :, None] - a_cum[:, :, None, :]
    causal = jnp.tril(jnp.ones((cl, cl), dtype=bool))[:, :, None, None]
    decay = jnp.exp(jnp.where(causal, seg, -jnp.inf))
    cb = jnp.einsum('bclgn,bcsgn->bclsg', cc, bc)
    y_diag = jnp.einsum('bclsge,bcsgep->bclgep', cb[..., None] * decay, xdt)
    decay_to_end = jnp.exp(a_cum[:, :, -1:] - a_cum)
    chunk_states = jnp.einsum('bclgn,bclge,bclgep->bcgepn', bc, decay_to_end, xdt)
    chunk_decay = jnp.exp(a_cum[:, :, -1])

    def step(h, inp):
        s, dcy = inp
        return dcy[..., None, None] * h + s, h

    h_last, h_in = lax.scan(step, h0, (jnp.moveaxis(chunk_states, 1, 0), jnp.moveaxis(chunk_decay, 1, 0)))
    h_in = jnp.moveaxis(h_in, 0, 1)
    y_off = jnp.einsum('bclgn,bcgepn,bclge->bclgep', cc, h_in, jnp.exp(a_cum))
    y = y_diag + y_off + d_skip.astype(jnp.float32)[..., None] * xr
    return y.reshape(bt, L, SSM_WIDTH), h_last


def ssd_mixer(z, xbc, dt_raw, conv_prev, h0, conv_w, conv_b, dt_bias, a_log, d_skip, norm_w):
    bt, L, _ = xbc.shape
    full = jnp.concatenate([conv_prev.astype(xbc.dtype), xbc], axis=1)
    new_conv = full[:, full.shape[1] - (CONV_K - 1):]
    conv = lax.conv_general_dilated(full, conv_w[:, None, :].astype(full.dtype), (1,), 'VALID',
                                    dimension_numbers=('NWC', 'WIO', 'NWC'),
                                    feature_group_count=CONV_DIM)
    xbc = jax.nn.silu(conv + conv_b.astype(conv.dtype))
    xs, bs, cs = jnp.split(xbc, [SSM_WIDTH, SSM_WIDTH + SSM_GROUPS * SSM_STATE], axis=-1)
    xs = xs.reshape(bt, L, SSM_GROUPS, HEADS_PER_GROUP, SSM_HEAD_DIM)
    bs = bs.reshape(bt, L, SSM_GROUPS, SSM_STATE)
    cs = cs.reshape(bt, L, SSM_GROUPS, SSM_STATE)
    dt = jax.nn.softplus(dt_raw.astype(jnp.float32) + dt_bias.astype(jnp.float32))
    dt = dt.reshape(bt, L, SSM_GROUPS, HEADS_PER_GROUP)
    a = -jnp.exp(a_log.astype(jnp.float32)).reshape(SSM_GROUPS, HEADS_PER_GROUP)
    h0 = h0.astype(jnp.float32).reshape(bt, SSM_GROUPS, HEADS_PER_GROUP, SSM_HEAD_DIM, SSM_STATE)
    y, h_last = ssd_scan(xs, dt, a, bs, cs, d_skip.reshape(SSM_GROUPS, HEADS_PER_GROUP), h0)
    g = (y * jax.nn.silu(z.astype(jnp.float32))).reshape(bt, L, SSM_GROUPS, SSM_WIDTH // SSM_GROUPS)
    g = g * lax.rsqrt(jnp.mean(g * g, axis=-1, keepdims=True) + EPS)
    out = (g.reshape(bt, L, SSM_WIDTH) * norm_w.astype(jnp.float32)).astype(z.dtype)
    return out, new_conv, h_last.reshape(bt, SSM_HEADS, SSM_HEAD_DIM, SSM_STATE)


def hier_moe(x, w_grp, b_grp, w_exp, b_exp, w_gate, w_up, w_down):
    bt, L, D = x.shape
    t = x.reshape(bt * L, D)
    g_prob = jax.nn.softmax((t @ w_grp).astype(jnp.float32) + b_grp.astype(jnp.float32), axis=-1)
    g_top, g_idx = lax.top_k(g_prob, 1)
    e_logits = ((t @ w_exp).astype(jnp.float32) + b_exp.astype(jnp.float32)).reshape(-1, N_EGROUPS, EXP_PER_GROUP)
    e_logits = jnp.take_along_axis(e_logits, g_idx[:, :, None], axis=1)[:, 0]
    e_top, e_idx = lax.top_k(jax.nn.softmax(e_logits, axis=-1), TOP_K)
    gate = g_top * (e_top / jnp.sum(e_top, axis=-1, keepdims=True))
    expert = g_idx * EXP_PER_GROUP + e_idx
    combine = jnp.einsum('tk,tke->te', gate, jax.nn.one_hot(expert, N_EXPERTS, dtype=jnp.float32))
    hid = jax.nn.silu(jnp.einsum('td,edf->tef', t, w_gate)) * jnp.einsum('td,edf->tef', t, w_up)
    hid = hid * combine[:, :, None].astype(hid.dtype)
    y = jnp.einsum('tef,efd->td', hid, w_down)
    return y.reshape(bt, L, D).astype(x.dtype)


def trunk_layer(x, pos, past_k, past_v, conv_prev, h0, w):
    bt, L, _ = x.shape
    xn = rms_norm(x, w['norm1'])
    proj = xn @ w['w_in']
    q, k, v, z, xbc, dt_raw = jnp.split(proj, [Q_END, K_END, V_END, Z_END, XBC_END], axis=-1)
    q = rope(rms_norm(q.reshape(bt, L, N_HEADS, HEAD_DIM), w['q_norm']), pos)
    k = rope(rms_norm(k.reshape(bt, L, KV_HEADS, HEAD_DIM), w['k_norm']), pos)
    v = v.reshape(bt, L, KV_HEADS, HEAD_DIM)
    qg = q.reshape(bt, L, KV_HEADS, REP, HEAD_DIM)
    if past_k is None:
        attn = swa_banded(qg, k, v, w['sinks'])
        keep = min(WINDOW, L)
        new_k, new_v = k[:, L - keep:], v[:, L - keep:]
    else:
        attn, new_k, new_v = swa_cached(qg, k, v, past_k, past_v, pos, w['sinks'])
    ssm, new_conv, new_h = ssd_mixer(z, xbc, dt_raw, conv_prev, h0, w['conv_w'], w['conv_b'],
                                     w['dt_bias'], w['a_log'], w['d_skip'], w['ssm_norm'])
    h = x + jnp.concatenate([attn.astype(x.dtype), ssm], axis=-1) @ w['w_out']
    out = h + hier_moe(rms_norm(h, w['norm2']), w['w_grp'], w['b_grp'], w['w_exp'], w['b_exp'],
                       w['w_gate'], w['w_up'], w['w_down'])
    return out, new_k, new_v, new_conv, new_h


def setup_inputs(seed: int = 0) -> dict:
    key = jax.random.key(seed)
    ks = jax.random.split(key, 26)
    f32 = jnp.float32
    nrm = lambda k, s, sc: jax.random.normal(k, s, f32) * sc
    win = min(WINDOW, PAST_LEN)
    dt0 = jnp.exp(jax.random.uniform(ks[14], (DEPTH, SSM_HEADS), f32) * (math.log(0.1) - math.log(0.001)) + math.log(0.001))
    return {
        "x_prompt": nrm(ks[0], (BATCH, SEQ, D_MODEL), 1.0),
        "x_sample": nrm(ks[1], (DEC_BATCH, DEC_SEQ, D_MODEL), 1.0),
        "cache_win_k": nrm(ks[2], (DEPTH, DEC_BATCH, win, KV_HEADS, HEAD_DIM), 1.0),
        "cache_win_v": nrm(ks[3], (DEPTH, DEC_BATCH, win, KV_HEADS, HEAD_DIM), 1.0),
        "state_conv": nrm(ks[4], (DEPTH, DEC_BATCH, CONV_K - 1, CONV_DIM), 1.0),
        "state_ssm": nrm(ks[5], (DEPTH, DEC_BATCH, SSM_HEADS, SSM_HEAD_DIM, SSM_STATE), 0.1),
        "norm1": 1.0 + nrm(ks[6], (DEPTH, D_MODEL), 0.02),
        "w_in": nrm(ks[7], (DEPTH, D_MODEL, IN_DIM), D_MODEL ** -0.5),
        "q_norm": 1.0 + nrm(ks[8], (DEPTH, HEAD_DIM), 0.02),
        "k_norm": 1.0 + nrm(ks[9], (DEPTH, HEAD_DIM), 0.02),
        "sinks": nrm(ks[10], (DEPTH, N_HEADS), 0.5),
        "conv_w": nrm(ks[11], (DEPTH, CONV_K, CONV_DIM), CONV_K ** -0.5),
        "conv_b": nrm(ks[12], (DEPTH, CONV_DIM), 0.02),
        "dt_bias": dt0 + jnp.log(-jnp.expm1(-dt0)),
        "a_log": jnp.log(jax.random.uniform(ks[13], (DEPTH, SSM_HEADS), f32, 1.0, 16.0)),
        "d_skip": 1.0 + nrm(ks[15], (DEPTH, SSM_HEADS), 0.02),
        "ssm_norm": 1.0 + nrm(ks[16], (DEPTH, SSM_WIDTH), 0.02),
        "w_out": nrm(ks[17], (DEPTH, MIX_WIDTH, D_MODEL), MIX_WIDTH ** -0.5),
        "norm2": 1.0 + nrm(ks[18], (DEPTH, D_MODEL), 0.02),
        "w_grp": nrm(ks[19], (DEPTH, D_MODEL, N_EGROUPS), D_MODEL ** -0.5),
        "b_grp": nrm(ks[20], (DEPTH, N_EGROUPS), 0.01),
        "w_exp": nrm(ks[21], (DEPTH, D_MODEL, N_EXPERTS), D_MODEL ** -0.5),
        "b_exp": nrm(ks[22], (DEPTH, N_EXPERTS), 0.01),
        "w_gate": nrm(ks[23], (DEPTH, N_EXPERTS, D_MODEL, EXPERT_FF), D_MODEL ** -0.5),
        "w_up": nrm(ks[24], (DEPTH, N_EXPERTS, D_MODEL, EXPERT_FF), D_MODEL ** -0.5),
        "w_down": nrm(ks[25], (DEPTH, N_EXPERTS, EXPERT_FF, D_MODEL), EXPERT_FF ** -0.5),
    }


def reference(x_prompt, x_sample, cache_win_k, cache_win_v, state_conv, state_ssm, norm1, w_in,
              q_norm, k_norm, sinks, conv_w, conv_b, dt_bias, a_log, d_skip, ssm_norm, w_out,
              norm2, w_grp, b_grp, w_exp, b_exp, w_gate, w_up, w_down):
    bp, lp = x_prompt.shape[:2]
    pos_p = jnp.arange(lp, dtype=jnp.int32)
    pos_s = PAST_LEN + jnp.arange(x_sample.shape[1], dtype=jnp.int32)
    hp, hs = x_prompt, x_sample
    kp_l, vp_l, cp_l, sp_l, ks_l, vs_l, cs_l, ss_l = [], [], [], [], [], [], [], []
    for l in range(DEPTH):
        w = dict(norm1=norm1[l], w_in=w_in[l], q_norm=q_norm[l], k_norm=k_norm[l], sinks=sinks[l],
                 conv_w=conv_w[l], conv_b=conv_b[l], dt_bias=dt_bias[l], a_log=a_log[l],
                 d_skip=d_skip[l], ssm_norm=ssm_norm[l], w_out=w_out[l], norm2=norm2[l],
                 w_grp=w_grp[l], b_grp=b_grp[l], w_exp=w_exp[l], b_exp=b_exp[l],
                 w_gate=w_gate[l], w_up=w_up[l], w_down=w_down[l])
        conv0 = jnp.zeros((bp, CONV_K - 1, CONV_DIM), x_prompt.dtype)
        h0 = jnp.zeros((bp, SSM_HEADS, SSM_HEAD_DIM, SSM_STATE), jnp.float32)
        hp, kp, vp, cp, sp = trunk_layer(hp, pos_p, None, None, conv0, h0, w)
        hs, ks, vs, cs, ss = trunk_layer(hs, pos_s, cache_win_k[l], cache_win_v[l],
                                         state_conv[l], state_ssm[l], w)
        kp_l.append(kp); vp_l.append(vp); cp_l.append(cp); sp_l.append(sp)
        ks_l.append(ks); vs_l.append(vs); cs_l.append(cs); ss_l.append(ss)
    return (hp, hs, jnp.stack(kp_l), jnp.stack(vp_l), jnp.stack(cp_l), jnp.stack(sp_l),
            jnp.stack(ks_l), jnp.stack(vs_l), jnp.stack(cs_l), jnp.stack(ss_l))
```

```python
import functools
import math

import jax
import jax.numpy as jnp
from jax import lax
from jax.experimental import pallas as pl
from jax.experimental.pallas import tpu as pltpu

F32 = jnp.float32
BF16 = jnp.bfloat16

D_MODEL = 1024
HEAD_DIM = 64
N_HEADS = 8
KV_HEADS = 2
WINDOW = 128
ATTN_WIDTH = N_HEADS * HEAD_DIM
QK_WIDTH = ATTN_WIDTH + KV_HEADS * HEAD_DIM
KV_WIDTH = KV_HEADS * HEAD_DIM
ATTN_SCALE = HEAD_DIM ** -0.5
ROPE_THETA = 10000.0
SSM_WIDTH = 512
SSM_HEADS = 8
SSM_HEAD_DIM = 64
SSM_GROUPS = 2
SSM_STATE = 128
CONV_K = 4
CONV_DIM = SSM_WIDTH + 2 * SSM_GROUPS * SSM_STATE
SSD_CHUNK = 128
N_EGROUPS = 4
EXP_PER_GROUP = 8
N_EXPERTS = 32
EXPERT_FF = 128
EPS = 1e-6
PAST_LEN = 16384

LANES = 128
BF16_ROWS = 16
HEAD_PAIRS = SSM_HEADS // 2
EXPERT_LANE0 = 32
VMEM_LIMIT = 56 * 1024 * 1024

Q_END = ATTN_WIDTH
K_END = Q_END + KV_WIDTH
V_END = K_END + KV_WIDTH
Z_END = V_END + SSM_WIDTH
XBC_END = Z_END + CONV_DIM


def _dot(a, b):
    return jnp.dot(a, b, preferred_element_type=F32)


def _dot_nt(a, b):
    return lax.dot_general(a, b, (((1,), (1,)), ((), ())), preferred_element_type=F32)


def _split2(v):
    hi = v.astype(BF16)
    lo = (v - hi.astype(F32)).astype(BF16)
    return hi, lo


def _split3(v):
    hi = v.astype(BF16)
    r = v - hi.astype(F32)
    mid = r.astype(BF16)
    lo = (r - mid.astype(F32)).astype(BF16)
    return hi, mid, lo


def _silu(x):
    return x * jax.nn.sigmoid(x)


def _softplus(x):
    return jnp.maximum(x, 0.0) + jnp.log1p(jnp.exp(-jnp.abs(x)))


def _lane_bcast_pairs(v, n_pairs):
    r = v.shape[0]
    lo = lax.broadcasted_iota(jnp.int32, (r, LANES), 1) < HEAD_DIM
    slabs = []
    for j in range(n_pairs):
        a = jnp.broadcast_to(v[:, 2 * j:2 * j + 1], (r, LANES))
        b = jnp.broadcast_to(v[:, 2 * j + 1:2 * j + 2], (r, LANES))
        slabs.append(jnp.where(lo, a, b))
    return jnp.concatenate(slabs, axis=1)


def _inproj_kernel(x_ref, n1_ref, wqk_ref, wv_ref, wz_ref, wxbc_ref, wdt_ref, qkn_ref,
                   cos_ref, sin_ref, red_ref, exp_ref,
                   q_ref, k_ref, v_ref, z_ref, xbc_ref, dt_ref):
    x = x_ref[...]
    ms = jnp.mean(x * x, axis=-1, keepdims=True)
    xn = (x * lax.rsqrt(ms + EPS) * n1_ref[...]).astype(BF16)
    v_ref[...] = _dot(xn, wv_ref[...])
    z_ref[...] = _dot(xn, wz_ref[...])
    xbc_ref[...] = _dot(xn, wxbc_ref[...])
    dt_ref[...] = _dot(xn, wdt_ref[...])
    qk = _dot(xn, wqk_ref[...])
    sq_hi, sq_lo = _split2(qk * qk)
    ss = _dot(sq_hi, red_ref[...]) + _dot(sq_lo, red_ref[...])
    inv = lax.rsqrt(ss * (1.0 / HEAD_DIM) + EPS)
    inv_hi, inv_lo = _split2(inv)
    inv_x = _dot(inv_hi, exp_ref[...]) + _dot(inv_lo, exp_ref[...])
    qkn = qk * inv_x * qkn_ref[...]
    cos = cos_ref[...]
    sin = sin_ref[...]
    lane = lax.broadcasted_iota(jnp.int32, (x.shape[0], LANES), 1)
    first_half = (lane % HEAD_DIM) < (HEAD_DIM // 2)
    for c in range(QK_WIDTH // LANES):
        xc = qkn[:, c * LANES:(c + 1) * LANES]
        partner = jnp.where(first_half,
                            pltpu.roll(xc, LANES - HEAD_DIM // 2, axis=1),
                            pltpu.roll(xc, HEAD_DIM // 2, axis=1))
        rot = xc * cos + partner * sin
        if c < ATTN_WIDTH // LANES:
            q_ref[:, c * LANES:(c + 1) * LANES] = (rot * ATTN_SCALE).astype(BF16)
        else:
            k_ref[...] = rot


def _inproj(x2d, w, cos_tab, sin_tab, tm, n_pos_blocks):
    t = x2d.shape[0]
    grid = (t // tm,)
    tok = lambda i: (i, 0)
    const = lambda i: (0, 0)
    pos = lambda i: (i % n_pos_blocks, 0)
    full = lambda a: pl.BlockSpec(a.shape, const)
    out_shapes = (
        jax.ShapeDtypeStruct((t, ATTN_WIDTH), BF16),
        jax.ShapeDtypeStruct((t, KV_WIDTH), F32),
        jax.ShapeDtypeStruct((t, KV_WIDTH), F32),
        jax.ShapeDtypeStruct((t, SSM_WIDTH), F32),
        jax.ShapeDtypeStruct((t, CONV_DIM), F32),
        jax.ShapeDtypeStruct((t, LANES), F32),
    )
    return pl.pallas_call(
        _inproj_kernel,
        out_shape=out_shapes,
        grid=grid,
        in_specs=[
            pl.BlockSpec((tm, D_MODEL), tok),
            full(w["norm1"]), full(w["wqk"]), full(w["wv"]), full(w["wz"]), full(w["wxbc"]),
            full(w["wdt"]), full(w["qkn"]),
            pl.BlockSpec((tm, LANES), pos), pl.BlockSpec((tm, LANES), pos),
            full(w["red"]), full(w["exp"]),
        ],
        out_specs=(
            pl.BlockSpec((tm, ATTN_WIDTH), tok), pl.BlockSpec((tm, KV_WIDTH), tok),
            pl.BlockSpec((tm, KV_WIDTH), tok), pl.BlockSpec((tm, SSM_WIDTH), tok),
            pl.BlockSpec((tm, CONV_DIM), tok), pl.BlockSpec((tm, LANES), tok),
        ),
        compiler_params=pltpu.CompilerParams(dimension_semantics=("parallel",),
                                             vmem_limit_bytes=VMEM_LIMIT),
        name="inproj",
    )(x2d, w["norm1"], w["wqk"], w["wv"], w["wz"], w["wxbc"], w["wdt"], w["qkn"],
      cos_tab, sin_tab, w["red"], w["exp"])


def _pair_rhs(blk, g):
    lo = lax.broadcasted_iota(jnp.int32, blk.shape, 1) < HEAD_DIM
    swapped = pltpu.roll(blk, HEAD_DIM, axis=1)
    if g == 0:
        top = jnp.where(lo, blk, 0.0)
        bot = jnp.where(lo, 0.0, swapped)
    else:
        top = jnp.where(lo, swapped, 0.0)
        bot = jnp.where(lo, 0.0, blk)
    return jnp.concatenate([top, bot], axis=0).astype(BF16)


def _attn_kernel(sink_ref, q_ref, kc_ref, kp_ref, vc_ref, vp_ref, o_ref):
    has_prev = pl.program_id(1) > 0
    blk = q_ref.shape[0]
    qi = lax.broadcasted_iota(jnp.int32, (blk, 2 * blk), 0)
    kj = lax.broadcasted_iota(jnp.int32, (blk, 2 * blk), 1) % blk
    cur_ok = kj <= qi
    prev_ok = jnp.logical_and(jnp.logical_not(cur_ok), has_prev)
    lo = lax.broadcasted_iota(jnp.int32, (blk, LANES), 1) < HEAD_DIM
    kc = kc_ref[...]
    kp = kp_ref[...]
    vc = vc_ref[...]
    vp = vp_ref[...]
    for g in range(KV_HEADS):
        kc2, kp2 = _pair_rhs(kc, g), _pair_rhs(kp, g)
        vc2, vp2 = _pair_rhs(vc, g), _pair_rhs(vp, g)
        for r in range(N_HEADS // KV_HEADS // 2):
            pair = g * 2 + r
            q2 = q_ref[:, pair * LANES:(pair + 1) * LANES]
            s_cur = _dot_nt(q2, kc2)
            s_prev = _dot_nt(q2, kp2)
            s = jnp.where(cur_ok, s_cur, jnp.where(prev_ok, s_prev, -jnp.inf))
            outs = []
            ps = []
            dens = []
            for hh in range(2):
                sink = sink_ref[2 * pair + hh]
                sh = s[:, hh * blk:(hh + 1) * blk]
                m = jnp.maximum(jnp.max(sh, axis=-1, keepdims=True), sink)
                p = jnp.exp(sh - m)
                dens.append(jnp.sum(p, axis=-1, keepdims=True) + jnp.exp(sink - m))
                ps.append(p)
            p2 = jnp.concatenate(ps, axis=1)
            p_cur = jnp.where(cur_ok, p2, 0.0).astype(BF16)
            p_prev = jnp.where(cur_ok, 0.0, p2).astype(BF16)
            o2 = _dot(p_cur, vc2) + _dot(p_prev, vp2)
            den = jnp.where(lo, dens[0], dens[1])
            o_ref[:, pair * LANES:(pair + 1) * LANES] = (o2 / den).astype(BF16)


def _attn_prompt(q, k, v, sinks, batch, seq):
    nb = seq // WINDOW
    cur = lambda b, j, s: (b * nb + j, 0)
    prev = lambda b, j, s: (b * nb + jnp.maximum(j - 1, 0), 0)
    return pl.pallas_call(
        _attn_kernel,
        out_shape=jax.ShapeDtypeStruct((batch * seq, ATTN_WIDTH), BF16),
        grid_spec=pltpu.PrefetchScalarGridSpec(
            num_scalar_prefetch=1,
            grid=(batch, nb),
            in_specs=[
                pl.BlockSpec((WINDOW, ATTN_WIDTH), cur),
                pl.BlockSpec((WINDOW, KV_WIDTH), cur), pl.BlockSpec((WINDOW, KV_WIDTH), prev),
                pl.BlockSpec((WINDOW, KV_WIDTH), cur), pl.BlockSpec((WINDOW, KV_WIDTH), prev),
            ],
            out_specs=pl.BlockSpec((WINDOW, ATTN_WIDTH), cur),
        ),
        compiler_params=pltpu.CompilerParams(dimension_semantics=("parallel", "parallel"),
                                             vmem_limit_bytes=VMEM_LIMIT),
        name="attn_prompt",
    )(sinks, q, k, k, v, v)


def _ssd_kernel(xbc_ref, z_ref, dt_ref, convw_ref, convb_ref, dtb_ref, alog_ref, dskip_ref, nw_ref,
                tri_ref, y_ref, st_ref, buf_ref, state_ref):
    c = pl.program_id(1)
    cl = xbc_ref.shape[0]
    halo = buf_ref.shape[0] - cl

    @pl.when(c == 0)
    def _():
        buf_ref[0:halo, :] = jnp.zeros((halo, CONV_DIM), F32)
        state_ref[...] = jnp.zeros(state_ref.shape, F32)

    x_raw = xbc_ref[...]
    buf_ref[halo:halo + cl, :] = x_raw
    conv = convb_ref[...]
    for j in range(CONV_K):
        off = halo - (CONV_K - 1) + j
        conv = conv + buf_ref[off:off + cl, :] * convw_ref[j:j + 1, :]
    buf_ref[0:halo, :] = x_raw[cl - halo:cl, :]
    act = _silu(conv)
    xs = act[:, :SSM_WIDTH]
    bm = act[:, SSM_WIDTH:SSM_WIDTH + SSM_GROUPS * SSM_STATE].astype(BF16)
    cm = act[:, SSM_WIDTH + SSM_GROUPS * SSM_STATE:].astype(BF16)

    lane = lax.broadcasted_iota(jnp.int32, (1, LANES), 1)
    a_neg = jnp.where(lane < SSM_HEADS, -jnp.exp(alog_ref[...]), 0.0)
    dt = _softplus(dt_ref[...] + dtb_ref[...])
    dta = dt * a_neg
    tri = tri_ref[...]
    p_hi, p_mid, p_lo = _split3(dta)
    a_col = _dot(tri, p_hi) + _dot(tri, p_mid) + _dot(tri, p_lo)
    a_last = a_col[cl - 1:cl, :]
    a_row = a_col.T
    dt_x = _lane_bcast_pairs(dt, HEAD_PAIRS)
    ecol_x = _lane_bcast_pairs(jnp.exp(a_col), HEAD_PAIRS)
    dte_x = _lane_bcast_pairs(jnp.exp(a_last - a_col), HEAD_PAIRS)
    e_last = jnp.exp(a_last)
    xdt = xs * dt_x

    li = lax.broadcasted_iota(jnp.int32, (cl, cl), 0)
    si = lax.broadcasted_iota(jnp.int32, (cl, cl), 1)
    causal = si <= li
    lo = lax.broadcasted_iota(jnp.int32, (cl, LANES), 1) < SSM_HEAD_DIM
    row_lo = lax.broadcasted_iota(jnp.int32, (LANES, SSM_STATE), 0) < SSM_HEAD_DIM

    ys = []
    for g in range(SSM_GROUPS):
        b_g = bm[:, g * SSM_STATE:(g + 1) * SSM_STATE]
        c_g = cm[:, g * SSM_STATE:(g + 1) * SSM_STATE]
        cb = _dot_nt(c_g, b_g)
        for r in range(HEAD_PAIRS // SSM_GROUPS):
            j = g * (HEAD_PAIRS // SSM_GROUPS) + r
            sl = slice(j * LANES, (j + 1) * LANES)
            xdt_p = xdt[:, sl]
            ms = []
            for hh in range(2):
                h = 2 * j + hh
                seg = a_col[:, h:h + 1] - a_row[h:h + 1, :]
                ms.append(cb * jnp.exp(jnp.where(causal, seg, -jnp.inf)))
            m2 = jnp.concatenate(ms, axis=1).astype(BF16)
            rhs = jnp.concatenate([jnp.where(lo, xdt_p, 0.0), jnp.where(lo, 0.0, xdt_p)],
                                  axis=0).astype(BF16)
            y_diag = _dot(m2, rhs)
            st = state_ref[j]
            y_off = _dot_nt(c_g, st.astype(BF16)) * ecol_x[:, sl]
            xdt_e = (xdt_p * dte_x[:, sl]).T.astype(BF16)
            d_a = e_last[:, 2 * j:2 * j + 1]
            d_b = e_last[:, 2 * j + 1:2 * j + 2]
            decay = jnp.where(row_lo, jnp.broadcast_to(d_a, row_lo.shape), jnp.broadcast_to(d_b, row_lo.shape))
            state_ref[j] = decay * st + _dot(xdt_e, b_g)
            ys.append(y_diag + y_off + dskip_ref[:, sl] * xs[:, sl])
    y = jnp.concatenate(ys, axis=1)
    gated = y * _silu(z_ref[...])
    gw = SSM_WIDTH // SSM_GROUPS
    outs = []
    for g in range(SSM_GROUPS):
        gg = gated[:, g * gw:(g + 1) * gw]
        outs.append(gg * lax.rsqrt(jnp.mean(gg * gg, axis=-1, keepdims=True) + EPS))
    y_ref[...] = (jnp.concatenate(outs, axis=1) * nw_ref[...]).astype(BF16)

    @pl.when(c == pl.num_programs(1) - 1)
    def _():
        st_ref[0] = state_ref[...]


def _ssd_prompt(xbc, z, dt, w, batch, seq):
    nc = seq // SSD_CHUNK
    tok = lambda b, c: (b * nc + c, 0)
    const = lambda b, c: (0, 0)
    full = lambda a: pl.BlockSpec(a.shape, const)
    return pl.pallas_call(
        _ssd_kernel,
        out_shape=(jax.ShapeDtypeStruct((batch * seq, SSM_WIDTH), BF16),
                   jax.ShapeDtypeStruct((batch, HEAD_PAIRS, LANES, SSM_STATE), F32)),
        grid=(batch, nc),
        in_specs=[
            pl.BlockSpec((SSD_CHUNK, CONV_DIM), tok), pl.BlockSpec((SSD_CHUNK, SSM_WIDTH), tok),
            pl.BlockSpec((SSD_CHUNK, LANES), tok),
            full(w["conv_w"]), full(w["conv_b"]), full(w["dt_bias"]), full(w["a_log"]),
            full(w["d_skip"]), full(w["ssm_norm"]), full(w["tri"]),
        ],
        out_specs=(pl.BlockSpec((SSD_CHUNK, SSM_WIDTH), tok),
                   pl.BlockSpec((1, HEAD_PAIRS, LANES, SSM_STATE), lambda b, c: (b, 0, 0, 0))),
        scratch_shapes=[pltpu.VMEM((SSD_CHUNK + 8, CONV_DIM), F32),
                        pltpu.VMEM((HEAD_PAIRS, LANES, SSM_STATE), F32)],
        compiler_params=pltpu.CompilerParams(dimension_semantics=("parallel", "arbitrary"),
                                             vmem_limit_bytes=VMEM_LIMIT),
        name="ssd_prompt",
    )(xbc, z, dt, w["conv_w"], w["conv_b"], w["dt_bias"], w["a_log"], w["d_skip"], w["ssm_norm"], w["tri"])


def _attn_sample_kernel(qx_ref, kc_ref, kn_ref, vc_ref, vn_ref, sink_ref, ko_ref, vo_ref, o_ref):
    bs = qx_ref.shape[0]
    w = kc_ref.shape[1]
    sink = sink_ref[...]
    lo = lax.broadcasted_iota(jnp.int32, (1, LANES), 1) < HEAD_DIM
    for i in range(bs):
        ko_ref[i, 0:w - 1, :] = kc_ref[i, 1:w, :]
        ko_ref[i, w - 1:w, :] = kn_ref[i]
        vo_ref[i, 0:w - 1, :] = vc_ref[i, 1:w, :]
        vo_ref[i, w - 1:w, :] = vn_ref[i]
        s = _dot_nt(qx_ref[i], ko_ref[i].astype(BF16))
        m = jnp.maximum(jnp.max(s, axis=-1, keepdims=True), sink)
        p = jnp.exp(s - m)
        den = jnp.sum(p, axis=-1, keepdims=True) + jnp.exp(sink - m)
        o = _dot(p.astype(BF16), vo_ref[i].astype(BF16)) / den
        o_sw = pltpu.roll(o, HEAD_DIM, axis=1)
        for j in range(N_HEADS // 2):
            a, b = (o, o_sw) if j < N_HEADS // 4 else (o_sw, o)
            o_ref[i:i + 1, j * LANES:(j + 1) * LANES] = jnp.where(lo, a[2 * j:2 * j + 1], b[2 * j + 1:2 * j + 2])


def _attn_sample(qx, kc, kn, vc, vn, sink_x, bs):
    n, w = kc.shape[0], kc.shape[1]
    blk3 = lambda i: (i, 0, 0)
    return pl.pallas_call(
        _attn_sample_kernel,
        out_shape=(jax.ShapeDtypeStruct((n, w, KV_WIDTH), F32),
                   jax.ShapeDtypeStruct((n, w, KV_WIDTH), F32),
                   jax.ShapeDtypeStruct((n, ATTN_WIDTH), F32)),
        grid=(n // bs,),
        in_specs=[
            pl.BlockSpec((bs, BF16_ROWS, LANES), blk3),
            pl.BlockSpec((bs, w, KV_WIDTH), blk3), pl.BlockSpec((bs, 1, KV_WIDTH), blk3),
            pl.BlockSpec((bs, w, KV_WIDTH), blk3), pl.BlockSpec((bs, 1, KV_WIDTH), blk3),
            pl.BlockSpec(sink_x.shape, lambda i: (0, 0)),
        ],
        out_specs=(pl.BlockSpec((bs, w, KV_WIDTH), blk3), pl.BlockSpec((bs, w, KV_WIDTH), blk3),
                   pl.BlockSpec((bs, ATTN_WIDTH), lambda i: (i, 0))),
        compiler_params=pltpu.CompilerParams(dimension_semantics=("parallel",),
                                             vmem_limit_bytes=VMEM_LIMIT),
        name="attn_sample",
    )(qx, kc, kn, vc, vn, sink_x)


def _ssd_sample_kernel(xbc_ref, z_ref, dt_ref, cprev_ref, h0_ref, convw_ref, convb_ref, dtb_ref, alog_ref,
                       dskip_ref, nw_ref, y_ref, cnew_ref, h1_ref):
    bs = xbc_ref.shape[0]
    x_raw = xbc_ref[...]
    conv = convb_ref[...] + x_raw * convw_ref[CONV_K - 1:CONV_K, :]
    for j in range(CONV_K - 1):
        conv = conv + cprev_ref[j] * convw_ref[j:j + 1, :]
    for j in range(CONV_K - 2):
        cnew_ref[j] = cprev_ref[j + 1]
    cnew_ref[CONV_K - 2] = x_raw
    act = _silu(conv)
    xs = act[:, :SSM_WIDTH]
    bm = act[:, SSM_WIDTH:SSM_WIDTH + SSM_GROUPS * SSM_STATE].astype(BF16)
    cm = act[:, SSM_WIDTH + SSM_GROUPS * SSM_STATE:].astype(BF16)
    lane = lax.broadcasted_iota(jnp.int32, (1, LANES), 1)
    a_neg = jnp.where(lane < SSM_HEADS, -jnp.exp(alog_ref[...]), 0.0)
    dt = _softplus(dt_ref[...] + dtb_ref[...])
    dec = jnp.exp(dt * a_neg)
    xdt = xs * _lane_bcast_pairs(dt, HEAD_PAIRS)
    rowid = lax.broadcasted_iota(jnp.int32, (bs, LANES), 0)
    row_lo = lax.broadcasted_iota(jnp.int32, (LANES, SSM_STATE), 0) < SSM_HEAD_DIM
    ys = []
    for j in range(HEAD_PAIRS):
        g = j // (HEAD_PAIRS // SSM_GROUPS)
        sl = slice(j * LANES, (j + 1) * LANES)
        b_g = bm[:, g * SSM_STATE:(g + 1) * SSM_STATE]
        c_g = cm[:, g * SSM_STATE:(g + 1) * SSM_STATE]
        xdt_p = xdt[:, sl]
        y_p = jnp.zeros((bs, LANES), F32)
        for i in range(bs):
            xi = jnp.where(rowid == i, xdt_p, 0.0).T.astype(BF16)
            d_a = dec[i:i + 1, 2 * j:2 * j + 1]
            d_b = dec[i:i + 1, 2 * j + 1:2 * j + 2]
            decay = jnp.where(row_lo, jnp.broadcast_to(d_a, row_lo.shape), jnp.broadcast_to(d_b, row_lo.shape))
            new = decay * h0_ref[i, j] + _dot(xi, b_g)
            h1_ref[i, j] = new
            y_p = y_p + jnp.where(rowid == i, _dot_nt(c_g, new.astype(BF16)), 0.0)
        ys.append(y_p + dskip_ref[:, sl] * xs[:, sl])
    y = jnp.concatenate(ys, axis=1)
    gated = y * _silu(z_ref[...])
    gw = SSM_WIDTH // SSM_GROUPS
    outs = []
    for g in range(SSM_GROUPS):
        gg = gated[:, g * gw:(g + 1) * gw]
        outs.append(gg * lax.rsqrt(jnp.mean(gg * gg, axis=-1, keepdims=True) + EPS))
    y_ref[...] = (jnp.concatenate(outs, axis=1) * nw_ref[...]).astype(BF16)


def _ssd_sample(xbc, z, dt, cprev_t, h0, w, bs):
    n = xbc.shape[0]
    tok = lambda i: (i, 0)
    const = lambda i: (0, 0)
    full = lambda a: pl.BlockSpec(a.shape, const)
    return pl.pallas_call(
        _ssd_sample_kernel,
        out_shape=(jax.ShapeDtypeStruct((n, SSM_WIDTH), BF16),
                   jax.ShapeDtypeStruct((CONV_K - 1, n, CONV_DIM), F32),
                   jax.ShapeDtypeStruct((n, HEAD_PAIRS, LANES, SSM_STATE), F32)),
        grid=(n // bs,),
        in_specs=[
            pl.BlockSpec((bs, CONV_DIM), tok), pl.BlockSpec((bs, SSM_WIDTH), tok),
            pl.BlockSpec((bs, LANES), tok),
            pl.BlockSpec((CONV_K - 1, bs, CONV_DIM), lambda i: (0, i, 0)),
            pl.BlockSpec((bs, HEAD_PAIRS, LANES, SSM_STATE), lambda i: (i, 0, 0, 0)),
            full(w["conv_w"]), full(w["conv_b"]), full(w["dt_bias"]), full(w["a_log"]),
            full(w["d_skip"]), full(w["ssm_norm"]),
        ],
        out_specs=(pl.BlockSpec((bs, SSM_WIDTH), tok),
                   pl.BlockSpec((CONV_K - 1, bs, CONV_DIM), lambda i: (0, i, 0)),
                   pl.BlockSpec((bs, HEAD_PAIRS, LANES, SSM_STATE), lambda i: (i, 0, 0, 0))),
        compiler_params=pltpu.CompilerParams(dimension_semantics=("parallel",),
                                             vmem_limit_bytes=VMEM_LIMIT),
        name="ssd_sample",
    )(xbc, z, dt, cprev_t, h0, w["conv_w"], w["conv_b"], w["dt_bias"], w["a_log"], w["d_skip"], w["ssm_norm"])


def _outproj_router_kernel(x_ref, a_ref, s_ref, wo_a_ref, wo_s_ref, n2_ref, wr_hi_ref, wr_lo_ref, br_ref,
                           h_ref, t_ref, comb_ref):
    h = x_ref[...] + _dot(a_ref[...].astype(BF16), wo_a_ref[...]) + _dot(s_ref[...].astype(BF16), wo_s_ref[...])
    h_ref[...] = h
    ms = jnp.mean(h * h, axis=-1, keepdims=True)
    t = h * lax.rsqrt(ms + EPS) * n2_ref[...]
    t_hi, t_lo = _split2(t)
    t_ref[...] = t_hi
    logits = (_dot(t_hi, wr_hi_ref[...]) + _dot(t_lo, wr_hi_ref[...]) + _dot(t_hi, wr_lo_ref[...])
              + br_ref[...])
    lane = lax.broadcasted_iota(jnp.int32, logits.shape, 1)
    lane_f = lane.astype(F32)
    big = float(LANES)
    ninf = -jnp.inf
    glog = jnp.where(lane < N_EGROUPS, logits, ninf)
    gmax = jnp.max(glog, axis=-1, keepdims=True)
    g_top = 1.0 / jnp.sum(jnp.exp(glog - gmax), axis=-1, keepdims=True)
    g_idx = jnp.min(jnp.where(glog == gmax, lane_f, big), axis=-1, keepdims=True)
    e_of_lane = lane - EXPERT_LANE0
    in_grp = jnp.logical_and(jnp.logical_and(e_of_lane >= 0, e_of_lane < N_EXPERTS),
                             (e_of_lane // EXP_PER_GROUP).astype(F32) == g_idx)
    ml = jnp.where(in_grp, logits, ninf)
    m1 = jnp.max(ml, axis=-1, keepdims=True)
    i1 = jnp.min(jnp.where(ml == m1, lane_f, big), axis=-1, keepdims=True)
    ml2 = jnp.where(lane_f == i1, ninf, ml)
    m2 = jnp.max(ml2, axis=-1, keepdims=True)
    i2 = jnp.min(jnp.where(ml2 == m2, lane_f, big), axis=-1, keepdims=True)
    r = jnp.exp(m2 - m1)
    w1 = g_top / (1.0 + r)
    w2 = g_top * r / (1.0 + r)
    comb_ref[...] = jnp.where(lane_f == i1, w1, 0.0) + jnp.where(lane_f == i2, w2, 0.0)


def _outproj_router(x2d, attn, ssm, w, tm):
    t = x2d.shape[0]
    tok = lambda i: (i, 0)
    const = lambda i: (0, 0)
    full = lambda a: pl.BlockSpec(a.shape, const)
    return pl.pallas_call(
        _outproj_router_kernel,
        out_shape=(jax.ShapeDtypeStruct((t, D_MODEL), F32), jax.ShapeDtypeStruct((t, D_MODEL), BF16),
                   jax.ShapeDtypeStruct((t, LANES), F32)),
        grid=(t // tm,),
        in_specs=[
            pl.BlockSpec((tm, D_MODEL), tok), pl.BlockSpec((tm, ATTN_WIDTH), tok),
            pl.BlockSpec((tm, SSM_WIDTH), tok),
            full(w["wo_a"]), full(w["wo_s"]), full(w["norm2"]), full(w["wr_hi"]), full(w["wr_lo"]),
            full(w["br"]),
        ],
        out_specs=(pl.BlockSpec((tm, D_MODEL), tok), pl.BlockSpec((tm, D_MODEL), tok),
                   pl.BlockSpec((tm, LANES), tok)),
        compiler_params=pltpu.CompilerParams(dimension_semantics=("parallel",),
                                             vmem_limit_bytes=VMEM_LIMIT),
        name="outproj_router",
    )(x2d, attn, ssm, w["wo_a"], w["wo_s"], w["norm2"], w["wr_hi"], w["wr_lo"], w["br"])


def _moe_dense_kernel(h_ref, t_ref, comb_ref, w1_ref, w2_ref, y_ref, acc_ref):
    t = t_ref[...]
    tm = t.shape[0]
    lo = lax.broadcasted_iota(jnp.int32, (tm, 2 * EXPERT_FF), 1) < EXPERT_FF
    acc_ref[...] = h_ref[...]
    for p in range(N_EXPERTS // 2):
        gu = _dot(t, w1_ref[p])
        c_a = comb_ref[:, EXPERT_LANE0 + 2 * p:EXPERT_LANE0 + 2 * p + 1]
        c_b = comb_ref[:, EXPERT_LANE0 + 2 * p + 1:EXPERT_LANE0 + 2 * p + 2]
        cw = jnp.where(lo, jnp.broadcast_to(c_a, lo.shape), jnp.broadcast_to(c_b, lo.shape))
        hid = _silu(gu[:, :2 * EXPERT_FF]) * gu[:, 2 * EXPERT_FF:] * cw
        acc_ref[...] += _dot(hid.astype(BF16), w2_ref[p])
    y_ref[...] = acc_ref[...]


def _moe_dense(h, t, comb, w, tm):
    n = h.shape[0]
    tok = lambda i: (i, 0)
    resident = lambda a: pl.BlockSpec(a.shape, lambda i: (0, 0, 0), pipeline_mode=pl.Buffered(1))
    return pl.pallas_call(
        _moe_dense_kernel,
        out_shape=jax.ShapeDtypeStruct((n, D_MODEL), F32),
        grid=(n // tm,),
        in_specs=[pl.BlockSpec((tm, D_MODEL), tok), pl.BlockSpec((tm, D_MODEL), tok),
                  pl.BlockSpec((tm, LANES), tok), resident(w["w1"]), resident(w["w2"])],
        out_specs=pl.BlockSpec((tm, D_MODEL), tok),
        scratch_shapes=[pltpu.VMEM((tm, D_MODEL), F32)],
        compiler_params=pltpu.CompilerParams(dimension_semantics=("parallel",),
                                             vmem_limit_bytes=VMEM_LIMIT),
        name="moe_dense",
    )(h, t, comb, w["w1"], w["w2"])


def _pad_lanes(a, width=LANES):
    return jnp.pad(a, ((0, 0), (0, width - a.shape[1])))


def _prep_weights(norm1, w_in, q_norm, k_norm, conv_w, conv_b, dt_bias, a_log, d_skip, ssm_norm, w_out,
                  norm2, w_grp, b_grp, w_exp, b_exp, w_gate, w_up, w_down):
    w = {}
    w["norm1"] = norm1.reshape(1, D_MODEL)
    w["wqk"] = w_in[:, :K_END].astype(BF16)
    w["wv"] = w_in[:, K_END:V_END].astype(BF16)
    w["wz"] = w_in[:, V_END:Z_END].astype(BF16)
    w["wxbc"] = w_in[:, Z_END:XBC_END].astype(BF16)
    w["wdt"] = _pad_lanes(w_in[:, XBC_END:]).astype(BF16)
    w["qkn"] = jnp.concatenate([jnp.tile(q_norm, N_HEADS), jnp.tile(k_norm, KV_HEADS)]).reshape(1, QK_WIDTH)
    head_of_col = jnp.arange(QK_WIDTH) // HEAD_DIM
    red = (head_of_col[:, None] == jnp.arange(LANES)[None, :])
    w["red"] = red.astype(BF16)
    w["exp"] = red.T.astype(BF16)
    w["conv_w"] = conv_w
    w["conv_b"] = conv_b.reshape(1, CONV_DIM)
    w["dt_bias"] = _pad_lanes(dt_bias.reshape(1, SSM_HEADS))
    w["a_log"] = _pad_lanes(a_log.reshape(1, SSM_HEADS))
    w["d_skip"] = jnp.repeat(d_skip, SSM_HEAD_DIM).reshape(1, SSM_WIDTH)
    w["ssm_norm"] = ssm_norm.reshape(1, SSM_WIDTH)
    idx = jnp.arange(SSD_CHUNK)
    w["tri"] = (idx[None, :] <= idx[:, None]).astype(BF16)
    w["wo_a"] = w_out[:ATTN_WIDTH].astype(BF16)
    w["wo_s"] = w_out[ATTN_WIDTH:].astype(BF16)
    w["norm2"] = norm2.reshape(1, D_MODEL)
    wr = jnp.zeros((D_MODEL, LANES), F32)
    wr = wr.at[:, :N_EGROUPS].set(w_grp).at[:, EXPERT_LANE0:EXPERT_LANE0 + N_EXPERTS].set(w_exp)
    w["wr_hi"] = wr.astype(BF16)
    w["wr_lo"] = (wr - w["wr_hi"].astype(F32)).astype(BF16)
    br = jnp.zeros((1, LANES), F32)
    w["br"] = br.at[0, :N_EGROUPS].set(b_grp).at[0, EXPERT_LANE0:EXPERT_LANE0 + N_EXPERTS].set(b_exp)
    wg = w_gate.reshape(N_EXPERTS // 2, 2, D_MODEL, EXPERT_FF)
    wu = w_up.reshape(N_EXPERTS // 2, 2, D_MODEL, EXPERT_FF)
    w["w1"] = jnp.concatenate([wg[:, 0], wg[:, 1], wu[:, 0], wu[:, 1]], axis=-1).astype(BF16)
    w["w2"] = w_down.reshape(N_EXPERTS // 2, 2 * EXPERT_FF, D_MODEL).astype(BF16)
    return w


def _rope_tables(pos):
    inv = 1.0 / (ROPE_THETA ** (jnp.arange(0, HEAD_DIM, 2, dtype=F32) / HEAD_DIM))
    ang = pos.astype(F32)[:, None] * inv[None, :]
    cos, sin = jnp.cos(ang), jnp.sin(ang)
    reps = LANES // HEAD_DIM
    return (jnp.tile(jnp.concatenate([cos, cos], axis=-1), (1, reps)),
            jnp.tile(jnp.concatenate([-sin, sin], axis=-1), (1, reps)))


def _token_tile(t):
    for tm in (512, 256, 128, 64, 32, 16):
        if t % tm == 0:
            return tm
    raise ValueError(f"token count {t} is not a multiple of 16")


def kernel(x_prompt, x_sample, cache_win_k, cache_win_v, state_conv, state_ssm, norm1, w_in, q_norm, k_norm,
           sinks, conv_w, conv_b, dt_bias, a_log, d_skip, ssm_norm, w_out, norm2, w_grp, b_grp, w_exp, b_exp,
           w_gate, w_up, w_down):
    depth = norm1.shape[0]
    assert depth == 1, "single-layer stack"
    bp, lp, _ = x_prompt.shape
    bsn, ls, _ = x_sample.shape
    assert ls == 1 and lp % WINDOW == 0 and cache_win_k.shape[2] == WINDOW
    l = 0
    w = _prep_weights(norm1[l], w_in[l], q_norm[l], k_norm[l], conv_w[l], conv_b[l], dt_bias[l], a_log[l],
                      d_skip[l], ssm_norm[l], w_out[l], norm2[l], w_grp[l], b_grp[l], w_exp[l], b_exp[l],
                      w_gate[l], w_up[l], w_down[l])
    sink = sinks[l]

    tp = bp * lp
    xp = x_prompt.reshape(tp, D_MODEL)
    tm_p = _token_tile(lp)
    cos_p, sin_p = _rope_tables(jnp.arange(lp, dtype=jnp.int32))
    q, k, v, z, xbc, dt = _inproj(xp, w, cos_p, sin_p, tm_p, lp // tm_p)
    attn = _attn_prompt(q, k, v, sink, bp, lp)
    ssm, st_p = _ssd_prompt(xbc, z, dt, w, bp, lp)
    h, t, comb = _outproj_router(xp, attn, ssm, w, tm_p)
    y_prompt = _moe_dense(h, t, comb, w, min(tm_p, 256)).reshape(bp, lp, D_MODEL)
    k3 = k.reshape(bp, lp, KV_HEADS, HEAD_DIM)
    v3 = v.reshape(bp, lp, KV_HEADS, HEAD_DIM)
    win_k_p = k3[:, lp - WINDOW:][None]
    win_v_p = v3[:, lp - WINDOW:][None]
    conv_p = xbc.reshape(bp, lp, CONV_DIM)[:, lp - (CONV_K - 1):][None]
    ssm_p = st_p.reshape(1, bp, SSM_HEADS, SSM_HEAD_DIM, SSM_STATE)

    xs2 = x_sample.reshape(bsn, D_MODEL)
    tm_s = _token_tile(bsn)
    cos_s, sin_s = _rope_tables(jnp.full((tm_s,), PAST_LEN, jnp.int32))
    q_s, k_s, v_s, z_s, xbc_s, dt_s = _inproj(xs2, w, cos_s, sin_s, tm_s, 1)
    q4 = q_s.reshape(bsn, KV_HEADS, N_HEADS // KV_HEADS, HEAD_DIM)
    zq = jnp.zeros_like(q4[:, 0])
    qx = jnp.concatenate([jnp.concatenate([q4[:, 0], zq], axis=-1),
                          jnp.concatenate([zq, q4[:, 1]], axis=-1)], axis=1)
    qx = jnp.pad(qx, ((0, 0), (0, BF16_ROWS - N_HEADS), (0, 0)))
    sink_x = jnp.pad(jnp.broadcast_to(sink[:, None], (N_HEADS, LANES)), ((0, BF16_ROWS - N_HEADS), (0, 0)))
    kc = cache_win_k[l].reshape(bsn, WINDOW, KV_WIDTH)
    vc = cache_win_v[l].reshape(bsn, WINDOW, KV_WIDTH)
    ko, vo, attn_s = _attn_sample(qx, kc, k_s.reshape(bsn, 1, KV_WIDTH), vc, v_s.reshape(bsn, 1, KV_WIDTH),
                               sink_x, 8)
    cprev_t = jnp.transpose(state_conv[l], (1, 0, 2))
    h0 = state_ssm[l].reshape(bsn, HEAD_PAIRS, LANES, SSM_STATE)
    ssm_s, cnew_t, h1 = _ssd_sample(xbc_s, z_s, dt_s, cprev_t, h0, w, 16)
    h_s, t_s, comb_s = _outproj_router(xs2, attn_s, ssm_s, w, tm_s)
    y_sample = _moe_dense(h_s, t_s, comb_s, w, tm_s).reshape(bsn, 1, D_MODEL)
    win_k_s = ko.reshape(1, bsn, WINDOW, KV_HEADS, HEAD_DIM)
    win_v_s = vo.reshape(1, bsn, WINDOW, KV_HEADS, HEAD_DIM)
    conv_s = jnp.transpose(cnew_t, (1, 0, 2))[None]
    ssm_s_state = h1.reshape(1, bsn, SSM_HEADS, SSM_HEAD_DIM, SSM_STATE)

    return (y_prompt, y_sample, win_k_p, win_v_p, conv_p, ssm_p, win_k_s, win_v_s, conv_s, ssm_s_state)
```

```python
import functools
import math

import jax
import jax.numpy as jnp
from jax import lax
from jax.experimental import pallas as pl
from jax.experimental.pallas import tpu as pltpu
from jax.experimental.pallas import tpu_sc as plsc

F32 = jnp.float32
BF16 = jnp.bfloat16

D_MODEL = 1024
HEAD_DIM = 64
N_HEADS = 8
KV_HEADS = 2
WINDOW = 128
ATTN_WIDTH = N_HEADS * HEAD_DIM
QK_WIDTH = ATTN_WIDTH + KV_HEADS * HEAD_DIM
KV_WIDTH = KV_HEADS * HEAD_DIM
ATTN_SCALE = HEAD_DIM ** -0.5
ROPE_THETA = 10000.0
SSM_WIDTH = 512
SSM_HEADS = 8
SSM_HEAD_DIM = 64
SSM_GROUPS = 2
SSM_STATE = 128
CONV_K = 4
CONV_DIM = SSM_WIDTH + 2 * SSM_GROUPS * SSM_STATE
SSD_CHUNK = 128
N_EGROUPS = 4
EXP_PER_GROUP = 8
N_EXPERTS = 32
EXPERT_FF = 128
EPS = 1e-6
PAST_LEN = 16384

LANES = 128
BF16_ROWS = 16
HEAD_PAIRS = SSM_HEADS // 2
EXPERT_LANE0 = 32
VMEM_LIMIT = 56 * 1024 * 1024
MOE_TILE = 256
SC_CORES = 2
SC_SUBCORES = 16
SC_WORKERS = SC_CORES * SC_SUBCORES
SC_CHUNK = 64

Q_END = ATTN_WIDTH
K_END = Q_END + KV_WIDTH
V_END = K_END + KV_WIDTH
Z_END = V_END + SSM_WIDTH
XBC_END = Z_END + CONV_DIM


def _dot(a, b):
    return jnp.dot(a, b, preferred_element_type=F32)


def _dot_nt(a, b):
    return lax.dot_general(a, b, (((1,), (1,)), ((), ())), preferred_element_type=F32)


def _split2(v):
    hi = v.astype(BF16)
    lo = (v - hi.astype(F32)).astype(BF16)
    return hi, lo


def _split3(v):
    hi = v.astype(BF16)
    r = v - hi.astype(F32)
    mid = r.astype(BF16)
    lo = (r - mid.astype(F32)).astype(BF16)
    return hi, mid, lo


def _silu(x):
    return x * jax.nn.sigmoid(x)


def _softplus(x):
    return jnp.maximum(x, 0.0) + jnp.log1p(jnp.exp(-jnp.abs(x)))


def _lane_bcast_pairs(v, n_pairs):
    r = v.shape[0]
    lo = lax.broadcasted_iota(jnp.int32, (r, LANES), 1) < HEAD_DIM
    slabs = []
    for j in range(n_pairs):
        a = jnp.broadcast_to(v[:, 2 * j:2 * j + 1], (r, LANES))
        b = jnp.broadcast_to(v[:, 2 * j + 1:2 * j + 2], (r, LANES))
        slabs.append(jnp.where(lo, a, b))
    return jnp.concatenate(slabs, axis=1)


def _inproj_kernel(x_ref, n1_ref, wqk_ref, wv_ref, wz_ref, wxbc_ref, wdt_ref, qkn_ref,
                   cos_ref, sin_ref, red_ref, exp_ref,
                   q_ref, k_ref, v_ref, z_ref, xbc_ref, dt_ref):
    x = x_ref[...]
    ms = jnp.mean(x * x, axis=-1, keepdims=True)
    xn = (x * lax.rsqrt(ms + EPS) * n1_ref[...]).astype(BF16)
    v_ref[...] = _dot(xn, wv_ref[...])
    z_ref[...] = _dot(xn, wz_ref[...])
    xbc_ref[...] = _dot(xn, wxbc_ref[...])
    dt_ref[...] = _dot(xn, wdt_ref[...])
    qk = _dot(xn, wqk_ref[...])
    sq_hi, sq_lo = _split2(qk * qk)
    ss = _dot(sq_hi, red_ref[...]) + _dot(sq_lo, red_ref[...])
    inv = lax.rsqrt(ss * (1.0 / HEAD_DIM) + EPS)
    inv_hi, inv_lo = _split2(inv)
    inv_x = _dot(inv_hi, exp_ref[...]) + _dot(inv_lo, exp_ref[...])
    qkn = qk * inv_x * qkn_ref[...]
    cos = cos_ref[...]
    sin = sin_ref[...]
    lane = lax.broadcasted_iota(jnp.int32, (x.shape[0], LANES), 1)
    first_half = (lane % HEAD_DIM) < (HEAD_DIM // 2)
    for c in range(QK_WIDTH // LANES):
        xc = qkn[:, c * LANES:(c + 1) * LANES]
        partner = jnp.where(first_half,
                            pltpu.roll(xc, LANES - HEAD_DIM // 2, axis=1),
                            pltpu.roll(xc, HEAD_DIM // 2, axis=1))
        rot = xc * cos + partner * sin
        if c < ATTN_WIDTH // LANES:
            q_ref[:, c * LANES:(c + 1) * LANES] = (rot * ATTN_SCALE).astype(BF16)
        else:
            k_ref[...] = rot


def _inproj(x2d, w, cos_tab, sin_tab, tm, n_pos_blocks):
    t = x2d.shape[0]
    grid = (t // tm,)
    tok = lambda i: (i, 0)
    const = lambda i: (0, 0)
    pos = lambda i: (i % n_pos_blocks, 0)
    full = lambda a: pl.BlockSpec(a.shape, const)
    out_shapes = (
        jax.ShapeDtypeStruct((t, ATTN_WIDTH), BF16),
        jax.ShapeDtypeStruct((t, KV_WIDTH), F32),
        jax.ShapeDtypeStruct((t, KV_WIDTH), F32),
        jax.ShapeDtypeStruct((t, SSM_WIDTH), F32),
        jax.ShapeDtypeStruct((t, CONV_DIM), F32),
        jax.ShapeDtypeStruct((t, LANES), F32),
    )
    return pl.pallas_call(
        _inproj_kernel,
        out_shape=out_shapes,
        grid=grid,
        in_specs=[
            pl.BlockSpec((tm, D_MODEL), tok),
            full(w["norm1"]), full(w["wqk"]), full(w["wv"]), full(w["wz"]), full(w["wxbc"]),
            full(w["wdt"]), full(w["qkn"]),
            pl.BlockSpec((tm, LANES), pos), pl.BlockSpec((tm, LANES), pos),
            full(w["red"]), full(w["exp"]),
        ],
        out_specs=(
            pl.BlockSpec((tm, ATTN_WIDTH), tok), pl.BlockSpec((tm, KV_WIDTH), tok),
            pl.BlockSpec((tm, KV_WIDTH), tok), pl.BlockSpec((tm, SSM_WIDTH), tok),
            pl.BlockSpec((tm, CONV_DIM), tok), pl.BlockSpec((tm, LANES), tok),
        ),
        compiler_params=pltpu.CompilerParams(dimension_semantics=("parallel",),
                                             vmem_limit_bytes=VMEM_LIMIT),
        name="inproj",
    )(x2d, w["norm1"], w["wqk"], w["wv"], w["wz"], w["wxbc"], w["wdt"], w["qkn"],
      cos_tab, sin_tab, w["red"], w["exp"])


def _pair_rhs(blk, g):
    lo = lax.broadcasted_iota(jnp.int32, blk.shape, 1) < HEAD_DIM
    swapped = pltpu.roll(blk, HEAD_DIM, axis=1)
    if g == 0:
        top = jnp.where(lo, blk, 0.0)
        bot = jnp.where(lo, 0.0, swapped)
    else:
        top = jnp.where(lo, swapped, 0.0)
        bot = jnp.where(lo, 0.0, blk)
    return jnp.concatenate([top, bot], axis=0).astype(BF16)


def _attn_kernel(sink_ref, q_ref, kc_ref, kp_ref, vc_ref, vp_ref, o_ref):
    has_prev = pl.program_id(1) > 0
    blk = q_ref.shape[0]
    qi = lax.broadcasted_iota(jnp.int32, (blk, 2 * blk), 0)
    kj = lax.broadcasted_iota(jnp.int32, (blk, 2 * blk), 1) % blk
    cur_ok = kj <= qi
    prev_ok = jnp.logical_and(jnp.logical_not(cur_ok), has_prev)
    lo = lax.broadcasted_iota(jnp.int32, (blk, LANES), 1) < HEAD_DIM
    kc = kc_ref[...]
    kp = kp_ref[...]
    vc = vc_ref[...]
    vp = vp_ref[...]
    for g in range(KV_HEADS):
        kc2, kp2 = _pair_rhs(kc, g), _pair_rhs(kp, g)
        vc2, vp2 = _pair_rhs(vc, g), _pair_rhs(vp, g)
        for r in range(N_HEADS // KV_HEADS // 2):
            pair = g * 2 + r
            q2 = q_ref[:, pair * LANES:(pair + 1) * LANES]
            s_cur = _dot_nt(q2, kc2)
            s_prev = _dot_nt(q2, kp2)
            s = jnp.where(cur_ok, s_cur, jnp.where(prev_ok, s_prev, -jnp.inf))
            outs = []
            ps = []
            dens = []
            for hh in range(2):
                sink = sink_ref[2 * pair + hh]
                sh = s[:, hh * blk:(hh + 1) * blk]
                m = jnp.maximum(jnp.max(sh, axis=-1, keepdims=True), sink)
                p = jnp.exp(sh - m)
                dens.append(jnp.sum(p, axis=-1, keepdims=True) + jnp.exp(sink - m))
                ps.append(p)
            p2 = jnp.concatenate(ps, axis=1)
            p_cur = jnp.where(cur_ok, p2, 0.0).astype(BF16)
            p_prev = jnp.where(cur_ok, 0.0, p2).astype(BF16)
            o2 = _dot(p_cur, vc2) + _dot(p_prev, vp2)
            den = jnp.where(lo, dens[0], dens[1])
            o_ref[:, pair * LANES:(pair + 1) * LANES] = (o2 / den).astype(BF16)


def _attn_prompt(q, k, v, sinks, batch, seq):
    nb = seq // WINDOW
    cur = lambda b, j, s: (b * nb + j, 0)
    prev = lambda b, j, s: (b * nb + jnp.maximum(j - 1, 0), 0)
    return pl.pallas_call(
        _attn_kernel,
        out_shape=jax.ShapeDtypeStruct((batch * seq, ATTN_WIDTH), BF16),
        grid_spec=pltpu.PrefetchScalarGridSpec(
            num_scalar_prefetch=1,
            grid=(batch, nb),
            in_specs=[
                pl.BlockSpec((WINDOW, ATTN_WIDTH), cur),
                pl.BlockSpec((WINDOW, KV_WIDTH), cur), pl.BlockSpec((WINDOW, KV_WIDTH), prev),
                pl.BlockSpec((WINDOW, KV_WIDTH), cur), pl.BlockSpec((WINDOW, KV_WIDTH), prev),
            ],
            out_specs=pl.BlockSpec((WINDOW, ATTN_WIDTH), cur),
        ),
        compiler_params=pltpu.CompilerParams(dimension_semantics=("parallel", "parallel"),
                                             vmem_limit_bytes=VMEM_LIMIT),
        name="attn_prompt",
    )(sinks, q, k, k, v, v)


def _ssd_kernel(xbc_ref, z_ref, dt_ref, convw_ref, convb_ref, dtb_ref, alog_ref, dskip_ref, nw_ref,
                tri_ref, y_ref, st_ref, buf_ref, state_ref):
    c = pl.program_id(1)
    cl = xbc_ref.shape[0]
    halo = buf_ref.shape[0] - cl

    @pl.when(c == 0)
    def _():
        buf_ref[0:halo, :] = jnp.zeros((halo, CONV_DIM), F32)
        state_ref[...] = jnp.zeros(state_ref.shape, F32)

    x_raw = xbc_ref[...]
    buf_ref[halo:halo + cl, :] = x_raw
    conv = convb_ref[...]
    for j in range(CONV_K):
        off = halo - (CONV_K - 1) + j
        conv = conv + buf_ref[off:off + cl, :] * convw_ref[j:j + 1, :]
    buf_ref[0:halo, :] = x_raw[cl - halo:cl, :]
    act = _silu(conv)
    xs = act[:, :SSM_WIDTH]
    bm = act[:, SSM_WIDTH:SSM_WIDTH + SSM_GROUPS * SSM_STATE].astype(BF16)
    cm = act[:, SSM_WIDTH + SSM_GROUPS * SSM_STATE:].astype(BF16)

    lane = lax.broadcasted_iota(jnp.int32, (1, LANES), 1)
    a_neg = jnp.where(lane < SSM_HEADS, -jnp.exp(alog_ref[...]), 0.0)
    dt = _softplus(dt_ref[...] + dtb_ref[...])
    dta = dt * a_neg
    tri = tri_ref[...]
    p_hi, p_mid, p_lo = _split3(dta)
    a_col = _dot(tri, p_hi) + _dot(tri, p_mid) + _dot(tri, p_lo)
    a_last = a_col[cl - 1:cl, :]
    a_row = a_col.T
    dt_x = _lane_bcast_pairs(dt, HEAD_PAIRS)
    ecol_x = _lane_bcast_pairs(jnp.exp(a_col), HEAD_PAIRS)
    dte_x = _lane_bcast_pairs(jnp.exp(a_last - a_col), HEAD_PAIRS)
    e_last = jnp.exp(a_last)
    xdt = xs * dt_x

    li = lax.broadcasted_iota(jnp.int32, (cl, cl), 0)
    si = lax.broadcasted_iota(jnp.int32, (cl, cl), 1)
    causal = si <= li
    lo = lax.broadcasted_iota(jnp.int32, (cl, LANES), 1) < SSM_HEAD_DIM
    row_lo = lax.broadcasted_iota(jnp.int32, (LANES, SSM_STATE), 0) < SSM_HEAD_DIM

    ys = []
    for g in range(SSM_GROUPS):
        b_g = bm[:, g * SSM_STATE:(g + 1) * SSM_STATE]
        c_g = cm[:, g * SSM_STATE:(g + 1) * SSM_STATE]
        cb = _dot_nt(c_g, b_g)
        for r in range(HEAD_PAIRS // SSM_GROUPS):
            j = g * (HEAD_PAIRS // SSM_GROUPS) + r
            sl = slice(j * LANES, (j + 1) * LANES)
            xdt_p = xdt[:, sl]
            ms = []
            for hh in range(2):
                h = 2 * j + hh
                seg = a_col[:, h:h + 1] - a_row[h:h + 1, :]
                ms.append(cb * jnp.exp(jnp.where(causal, seg, -jnp.inf)))
            m2 = jnp.concatenate(ms, axis=1).astype(BF16)
            rhs = jnp.concatenate([jnp.where(lo, xdt_p, 0.0), jnp.where(lo, 0.0, xdt_p)],
                                  axis=0).astype(BF16)
            y_diag = _dot(m2, rhs)
            st = state_ref[j]
            y_off = _dot_nt(c_g, st.astype(BF16)) * ecol_x[:, sl]
            xdt_e = (xdt_p * dte_x[:, sl]).T.astype(BF16)
            d_a = e_last[:, 2 * j:2 * j + 1]
            d_b = e_last[:, 2 * j + 1:2 * j + 2]
            decay = jnp.where(row_lo, jnp.broadcast_to(d_a, row_lo.shape), jnp.broadcast_to(d_b, row_lo.shape))
            state_ref[j] = decay * st + _dot(xdt_e, b_g)
            ys.append(y_diag + y_off + dskip_ref[:, sl] * xs[:, sl])
    y = jnp.concatenate(ys, axis=1)
    gated = y * _silu(z_ref[...])
    gw = SSM_WIDTH // SSM_GROUPS
    outs = []
    for g in range(SSM_GROUPS):
        gg = gated[:, g * gw:(g + 1) * gw]
        outs.append(gg * lax.rsqrt(jnp.mean(gg * gg, axis=-1, keepdims=True) + EPS))
    y_ref[...] = (jnp.concatenate(outs, axis=1) * nw_ref[...]).astype(BF16)

    @pl.when(c == pl.num_programs(1) - 1)
    def _():
        st_ref[0] = state_ref[...]


def _ssd_prompt(xbc, z, dt, w, batch, seq):
    nc = seq // SSD_CHUNK
    tok = lambda b, c: (b * nc + c, 0)
    const = lambda b, c: (0, 0)
    full = lambda a: pl.BlockSpec(a.shape, const)
    return pl.pallas_call(
        _ssd_kernel,
        out_shape=(jax.ShapeDtypeStruct((batch * seq, SSM_WIDTH), BF16),
                   jax.ShapeDtypeStruct((batch, HEAD_PAIRS, LANES, SSM_STATE), F32)),
        grid=(batch, nc),
        in_specs=[
            pl.BlockSpec((SSD_CHUNK, CONV_DIM), tok), pl.BlockSpec((SSD_CHUNK, SSM_WIDTH), tok),
            pl.BlockSpec((SSD_CHUNK, LANES), tok),
            full(w["conv_w"]), full(w["conv_b"]), full(w["dt_bias"]), full(w["a_log"]),
            full(w["d_skip"]), full(w["ssm_norm"]), full(w["tri"]),
        ],
        out_specs=(pl.BlockSpec((SSD_CHUNK, SSM_WIDTH), tok),
                   pl.BlockSpec((1, HEAD_PAIRS, LANES, SSM_STATE), lambda b, c: (b, 0, 0, 0))),
        scratch_shapes=[pltpu.VMEM((SSD_CHUNK + 8, CONV_DIM), F32),
                        pltpu.VMEM((HEAD_PAIRS, LANES, SSM_STATE), F32)],
        compiler_params=pltpu.CompilerParams(dimension_semantics=("parallel", "arbitrary"),
                                             vmem_limit_bytes=VMEM_LIMIT),
        name="ssd_prompt",
    )(xbc, z, dt, w["conv_w"], w["conv_b"], w["dt_bias"], w["a_log"], w["d_skip"], w["ssm_norm"], w["tri"])


def _attn_sample_kernel(qx_ref, kc_ref, kn_ref, vc_ref, vn_ref, sink_ref, ko_ref, vo_ref, o_ref):
    bs = qx_ref.shape[0]
    w = kc_ref.shape[1]
    sink = sink_ref[...]
    lo = lax.broadcasted_iota(jnp.int32, (1, LANES), 1) < HEAD_DIM
    for i in range(bs):
        ko_ref[i, 0:w - 1, :] = kc_ref[i, 1:w, :]
        ko_ref[i, w - 1:w, :] = kn_ref[i]
        vo_ref[i, 0:w - 1, :] = vc_ref[i, 1:w, :]
        vo_ref[i, w - 1:w, :] = vn_ref[i]
        s = _dot_nt(qx_ref[i], ko_ref[i].astype(BF16))
        m = jnp.maximum(jnp.max(s, axis=-1, keepdims=True), sink)
        p = jnp.exp(s - m)
        den = jnp.sum(p, axis=-1, keepdims=True) + jnp.exp(sink - m)
        o = _dot(p.astype(BF16), vo_ref[i].astype(BF16)) / den
        o_sw = pltpu.roll(o, HEAD_DIM, axis=1)
        for j in range(N_HEADS // 2):
            a, b = (o, o_sw) if j < N_HEADS // 4 else (o_sw, o)
            o_ref[i:i + 1, j * LANES:(j + 1) * LANES] = jnp.where(lo, a[2 * j:2 * j + 1], b[2 * j + 1:2 * j + 2])


def _attn_sample(qx, kc, kn, vc, vn, sink_x, bs):
    n, w = kc.shape[0], kc.shape[1]
    blk3 = lambda i: (i, 0, 0)
    return pl.pallas_call(
        _attn_sample_kernel,
        out_shape=(jax.ShapeDtypeStruct((n, w, KV_WIDTH), F32),
                   jax.ShapeDtypeStruct((n, w, KV_WIDTH), F32),
                   jax.ShapeDtypeStruct((n, ATTN_WIDTH), F32)),
        grid=(n // bs,),
        in_specs=[
            pl.BlockSpec((bs, BF16_ROWS, LANES), blk3),
            pl.BlockSpec((bs, w, KV_WIDTH), blk3), pl.BlockSpec((bs, 1, KV_WIDTH), blk3),
            pl.BlockSpec((bs, w, KV_WIDTH), blk3), pl.BlockSpec((bs, 1, KV_WIDTH), blk3),
            pl.BlockSpec(sink_x.shape, lambda i: (0, 0)),
        ],
        out_specs=(pl.BlockSpec((bs, w, KV_WIDTH), blk3), pl.BlockSpec((bs, w, KV_WIDTH), blk3),
                   pl.BlockSpec((bs, ATTN_WIDTH), lambda i: (i, 0))),
        compiler_params=pltpu.CompilerParams(dimension_semantics=("parallel",),
                                             vmem_limit_bytes=VMEM_LIMIT),
        name="attn_sample",
    )(qx, kc, kn, vc, vn, sink_x)


def _ssd_sample_kernel(xbc_ref, z_ref, dt_ref, cprev_ref, h0_ref, convw_ref, convb_ref, dtb_ref, alog_ref,
                       dskip_ref, nw_ref, y_ref, cnew_ref, h1_ref):
    bs = xbc_ref.shape[0]
    x_raw = xbc_ref[...]
    conv = convb_ref[...] + x_raw * convw_ref[CONV_K - 1:CONV_K, :]
    for j in range(CONV_K - 1):
        conv = conv + cprev_ref[j] * convw_ref[j:j + 1, :]
    for j in range(CONV_K - 2):
        cnew_ref[j] = cprev_ref[j + 1]
    cnew_ref[CONV_K - 2] = x_raw
    act = _silu(conv)
    xs = act[:, :SSM_WIDTH]
    bm = act[:, SSM_WIDTH:SSM_WIDTH + SSM_GROUPS * SSM_STATE].astype(BF16)
    cm = act[:, SSM_WIDTH + SSM_GROUPS * SSM_STATE:].astype(BF16)
    lane = lax.broadcasted_iota(jnp.int32, (1, LANES), 1)
    a_neg = jnp.where(lane < SSM_HEADS, -jnp.exp(alog_ref[...]), 0.0)
    dt = _softplus(dt_ref[...] + dtb_ref[...])
    dec = jnp.exp(dt * a_neg)
    xdt = xs * _lane_bcast_pairs(dt, HEAD_PAIRS)
    rowid = lax.broadcasted_iota(jnp.int32, (bs, LANES), 0)
    row_lo = lax.broadcasted_iota(jnp.int32, (LANES, SSM_STATE), 0) < SSM_HEAD_DIM
    ys = []
    for j in range(HEAD_PAIRS):
        g = j // (HEAD_PAIRS // SSM_GROUPS)
        sl = slice(j * LANES, (j + 1) * LANES)
        b_g = bm[:, g * SSM_STATE:(g + 1) * SSM_STATE]
        c_g = cm[:, g * SSM_STATE:(g + 1) * SSM_STATE]
        xdt_p = xdt[:, sl]
        y_p = jnp.zeros((bs, LANES), F32)
        for i in range(bs):
            xi = jnp.where(rowid == i, xdt_p, 0.0).T.astype(BF16)
            d_a = dec[i:i + 1, 2 * j:2 * j + 1]
            d_b = dec[i:i + 1, 2 * j + 1:2 * j + 2]
            decay = jnp.where(row_lo, jnp.broadcast_to(d_a, row_lo.shape), jnp.broadcast_to(d_b, row_lo.shape))
            new = decay * h0_ref[i, j] + _dot(xi, b_g)
            h1_ref[i, j] = new
            y_p = y_p + jnp.where(rowid == i, _dot_nt(c_g, new.astype(BF16)), 0.0)
        ys.append(y_p + dskip_ref[:, sl] * xs[:, sl])
    y = jnp.concatenate(ys, axis=1)
    gated = y * _silu(z_ref[...])
    gw = SSM_WIDTH // SSM_GROUPS
    outs = []
    for g in range(SSM_GROUPS):
        gg = gated[:, g * gw:(g + 1) * gw]
        outs.append(gg * lax.rsqrt(jnp.mean(gg * gg, axis=-1, keepdims=True) + EPS))
    y_ref[...] = (jnp.concatenate(outs, axis=1) * nw_ref[...]).astype(BF16)


def _ssd_sample(xbc, z, dt, cprev_t, h0, w, bs):
    n = xbc.shape[0]
    tok = lambda i: (i, 0)
    const = lambda i: (0, 0)
    full = lambda a: pl.BlockSpec(a.shape, const)
    return pl.pallas_call(
        _ssd_sample_kernel,
        out_shape=(jax.ShapeDtypeStruct((n, SSM_WIDTH), BF16),
                   jax.ShapeDtypeStruct((CONV_K - 1, n, CONV_DIM), F32),
                   jax.ShapeDtypeStruct((n, HEAD_PAIRS, LANES, SSM_STATE), F32)),
        grid=(n // bs,),
        in_specs=[
            pl.BlockSpec((bs, CONV_DIM), tok), pl.BlockSpec((bs, SSM_WIDTH), tok),
            pl.BlockSpec((bs, LANES), tok),
            pl.BlockSpec((CONV_K - 1, bs, CONV_DIM), lambda i: (0, i, 0)),
            pl.BlockSpec((bs, HEAD_PAIRS, LANES, SSM_STATE), lambda i: (i, 0, 0, 0)),
            full(w["conv_w"]), full(w["conv_b"]), full(w["dt_bias"]), full(w["a_log"]),
            full(w["d_skip"]), full(w["ssm_norm"]),
        ],
        out_specs=(pl.BlockSpec((bs, SSM_WIDTH), tok),
                   pl.BlockSpec((CONV_K - 1, bs, CONV_DIM), lambda i: (0, i, 0)),
                   pl.BlockSpec((bs, HEAD_PAIRS, LANES, SSM_STATE), lambda i: (i, 0, 0, 0))),
        compiler_params=pltpu.CompilerParams(dimension_semantics=("parallel",),
                                             vmem_limit_bytes=VMEM_LIMIT),
        name="ssd_sample",
    )(xbc, z, dt, cprev_t, h0, w["conv_w"], w["conv_b"], w["dt_bias"], w["a_log"], w["d_skip"], w["ssm_norm"])


def _pack_bf16_pair(v):
    c = v.shape[1] // 2
    hi = lax.bitcast_convert_type(v[:, :c].astype(BF16).astype(F32), jnp.uint32)
    lo = lax.bitcast_convert_type(v[:, c:].astype(BF16).astype(F32), jnp.uint32)
    return hi | (lo >> 16)


def _unpack_bf16_pair(word):
    a = lax.bitcast_convert_type(word & jnp.uint32(0xFFFF0000), F32)
    b = lax.bitcast_convert_type(word << 16, F32)
    return a, b


def _outproj_router_kernel(x_ref, a_ref, s_ref, wo_a_ref, wo_s_ref, n2_ref, wr_hi_ref, wr_lo_ref, br_ref, tri_ref,
                           h_ref, t_ref, route_ref, cnt_ref, carry_ref):
    @pl.when(pl.program_id(0) == 0)
    def _():
        carry_ref[...] = jnp.zeros(carry_ref.shape, F32)

    h = x_ref[...] + _dot(a_ref[...].astype(BF16), wo_a_ref[...]) + _dot(s_ref[...].astype(BF16), wo_s_ref[...])
    h_ref[...] = h
    ms = jnp.mean(h * h, axis=-1, keepdims=True)
    t = h * lax.rsqrt(ms + EPS) * n2_ref[...]
    t_hi, t_lo = _split2(t)
    t_ref[...] = _pack_bf16_pair(t)
    logits = (_dot(t_hi, wr_hi_ref[...]) + _dot(t_lo, wr_hi_ref[...]) + _dot(t_hi, wr_lo_ref[...])
              + br_ref[...])
    lane = lax.broadcasted_iota(jnp.int32, logits.shape, 1)
    lane_f = lane.astype(F32)
    big = float(LANES)
    ninf = -jnp.inf
    glog = jnp.where(lane < N_EGROUPS, logits, ninf)
    gmax = jnp.max(glog, axis=-1, keepdims=True)
    g_top = 1.0 / jnp.sum(jnp.exp(glog - gmax), axis=-1, keepdims=True)
    g_idx = jnp.min(jnp.where(glog == gmax, lane_f, big), axis=-1, keepdims=True)
    e_of_lane = lane - EXPERT_LANE0
    in_grp = jnp.logical_and(jnp.logical_and(e_of_lane >= 0, e_of_lane < N_EXPERTS),
                             (e_of_lane // EXP_PER_GROUP).astype(F32) == g_idx)
    ml = jnp.where(in_grp, logits, ninf)
    m1 = jnp.max(ml, axis=-1, keepdims=True)
    i1 = jnp.min(jnp.where(ml == m1, lane_f, big), axis=-1, keepdims=True)
    ml2 = jnp.where(lane_f == i1, ninf, ml)
    m2 = jnp.max(ml2, axis=-1, keepdims=True)
    i2 = jnp.min(jnp.where(ml2 == m2, lane_f, big), axis=-1, keepdims=True)
    r = jnp.exp(m2 - m1)
    w1 = g_top / (1.0 + r)
    w2 = g_top * r / (1.0 + r)
    is1 = lane_f == i1
    is2 = lane_f == i2
    onehot = jnp.where(jnp.logical_or(is1, is2), 1.0, 0.0)
    cum = _dot(tri_ref[...], onehot.astype(BF16)) + carry_ref[0:1, :]
    before = cum - onehot
    rank1 = jnp.sum(jnp.where(is1, before, 0.0), axis=-1, keepdims=True)
    rank2 = jnp.sum(jnp.where(is2, before, 0.0), axis=-1, keepdims=True)
    tm = cum.shape[0]
    carry_ref[0:1, :] = cum[tm - 1:tm, :]
    cnt_ref[...] = jnp.broadcast_to(cum[tm - 1:tm, :], cnt_ref.shape)
    fields = (i1 - EXPERT_LANE0, i2 - EXPERT_LANE0, w1, w2, rank1, rank2)
    route = jnp.zeros(logits.shape, F32)
    for pos, val in enumerate(fields):
        route = jnp.where(lane == pos, val, route)
    route_ref[...] = route


def _outproj_router(x2d, attn, ssm, w, tm):
    t = x2d.shape[0]
    tok = lambda i: (i, 0)
    const = lambda i: (0, 0)
    full = lambda a: pl.BlockSpec(a.shape, const)
    idx = jnp.arange(tm)
    tri = (idx[None, :] <= idx[:, None]).astype(BF16)
    return pl.pallas_call(
        _outproj_router_kernel,
        out_shape=(jax.ShapeDtypeStruct((t, D_MODEL), F32), jax.ShapeDtypeStruct((t, D_MODEL // 2), jnp.uint32),
                   jax.ShapeDtypeStruct((t, LANES), F32), jax.ShapeDtypeStruct((8, LANES), F32)),
        grid=(t // tm,),
        in_specs=[
            pl.BlockSpec((tm, D_MODEL), tok), pl.BlockSpec((tm, ATTN_WIDTH), tok),
            pl.BlockSpec((tm, SSM_WIDTH), tok),
            full(w["wo_a"]), full(w["wo_s"]), full(w["norm2"]), full(w["wr_hi"]), full(w["wr_lo"]),
            full(w["br"]), full(tri),
        ],
        out_specs=(pl.BlockSpec((tm, D_MODEL), tok), pl.BlockSpec((tm, D_MODEL // 2), tok),
                   pl.BlockSpec((tm, LANES), tok), pl.BlockSpec((8, LANES), const)),
        scratch_shapes=[pltpu.VMEM((8, LANES), F32)],
        compiler_params=pltpu.CompilerParams(dimension_semantics=("arbitrary",),
                                             vmem_limit_bytes=VMEM_LIMIT),
        name="outproj_router",
    )(x2d, attn, ssm, w["wo_a"], w["wo_s"], w["norm2"], w["wr_hi"], w["wr_lo"], w["br"], tri)


def _expert_hidden(t_a, t_b, w1):
    half = D_MODEL // 2
    gu = _dot(t_a, w1[:half]) + _dot(t_b, w1[half:])
    return _silu(gu[:, :EXPERT_FF]) * gu[:, EXPERT_FF:]


def _moe_dense_kernel(h_ref, t_ref, route_ref, w1_ref, w2_ref, y_ref, acc_ref):
    t_a, t_b = _unpack_bf16_pair(t_ref[...])
    t_a, t_b = t_a.astype(BF16), t_b.astype(BF16)
    route = route_ref[...]
    e1, e2, g1, g2 = route[:, 0:1], route[:, 1:2], route[:, 2:3], route[:, 3:4]
    acc_ref[...] = h_ref[...]
    for e in range(N_EXPERTS):
        hid = _expert_hidden(t_a, t_b, w1_ref[e])
        c_e = jnp.where(e1 == float(e), g1, 0.0) + jnp.where(e2 == float(e), g2, 0.0)
        acc_ref[...] += _dot((hid * c_e).astype(BF16), w2_ref[e])
    y_ref[...] = acc_ref[...]


def _moe_dense(h, t, route, w, tm):
    n = h.shape[0]
    tok = lambda i: (i, 0)
    resident = lambda a: pl.BlockSpec(a.shape, lambda i: (0, 0, 0), pipeline_mode=pl.Buffered(1))
    return pl.pallas_call(
        _moe_dense_kernel,
        out_shape=jax.ShapeDtypeStruct((n, D_MODEL), F32),
        grid=(n // tm,),
        in_specs=[pl.BlockSpec((tm, D_MODEL), tok), pl.BlockSpec((tm, D_MODEL // 2), tok),
                  pl.BlockSpec((tm, LANES), tok), resident(w["w1e"]), resident(w["w2e"])],
        out_specs=pl.BlockSpec((tm, D_MODEL), tok),
        scratch_shapes=[pltpu.VMEM((tm, D_MODEL), F32)],
        compiler_params=pltpu.CompilerParams(dimension_semantics=("parallel",),
                                             vmem_limit_bytes=VMEM_LIMIT),
        name="moe_dense",
    )(h, t, route, w["w1e"], w["w2e"])


def _sc_scatter_rows(src, pos1, pos2, n_out):
    t, width = src.shape
    rows_per_worker = t // SC_WORKERS
    n_chunks = rows_per_worker // SC_CHUNK
    assert t == SC_WORKERS * SC_CHUNK * n_chunks
    mesh = plsc.VectorSubcoreMesh(core_axis_name="c", subcore_axis_name="s")

    @functools.partial(
        pl.kernel, mesh=mesh,
        out_type=jax.ShapeDtypeStruct((n_out, width), src.dtype),
        scratch_types=[pltpu.VMEM((SC_CHUNK,), jnp.int32), pltpu.VMEM((SC_CHUNK,), jnp.int32),
                       pltpu.VMEM((SC_CHUNK, width), src.dtype), pltpu.SemaphoreType.DMA],
    )
    def scatter_kernel(src_hbm, p1_hbm, p2_hbm, out_hbm, i1_v, i2_v, rows_v, sem):
        wid = lax.axis_index("s") * SC_CORES + lax.axis_index("c")
        base = wid * rows_per_worker

        @pl.loop(0, n_chunks)
        def _(c):
            off = pl.multiple_of(base + c * SC_CHUNK, 8)
            pltpu.sync_copy(p1_hbm.at[pl.ds(off, SC_CHUNK)], i1_v)
            pltpu.sync_copy(p2_hbm.at[pl.ds(off, SC_CHUNK)], i2_v)
            pltpu.sync_copy(src_hbm.at[pl.ds(off, SC_CHUNK)], rows_v)
            pltpu.async_copy(rows_v, out_hbm.at[i1_v], sem).wait()
            pltpu.async_copy(rows_v, out_hbm.at[i2_v], sem).wait()

    return scatter_kernel(src, pos1, pos2)


def _sc_gather_rows(table, idx):
    n, width = idx.shape[0], table.shape[1]
    rows_per_worker = n // SC_WORKERS
    n_chunks = rows_per_worker // SC_CHUNK
    assert n == SC_WORKERS * SC_CHUNK * n_chunks and n_chunks % 2 == 0
    mesh = plsc.VectorSubcoreMesh(core_axis_name="c", subcore_axis_name="s")

    @functools.partial(
        pl.kernel, mesh=mesh,
        out_type=jax.ShapeDtypeStruct((n, width), table.dtype),
        scratch_types=[pltpu.VMEM((2, SC_CHUNK), jnp.int32), pltpu.VMEM((2, SC_CHUNK, width), table.dtype),
                       pltpu.SemaphoreType.DMA, pltpu.SemaphoreType.DMA,
                       pltpu.SemaphoreType.DMA, pltpu.SemaphoreType.DMA],
    )
    def gather_kernel(table_hbm, idx_hbm, out_hbm, idx_v, rows_v, g0, g1, w0, w1):
        wid = lax.axis_index("s") * SC_CORES + lax.axis_index("c")
        base = wid * rows_per_worker
        gsem = (g0, g1)
        wsem = (w0, w1)

        def gather_copy(slot):
            return pltpu.make_async_copy(table_hbm.at[idx_v.at[slot]], rows_v.at[slot], gsem[slot])

        def write_copy(c, slot):
            off = pl.multiple_of(base + c * SC_CHUNK, 8)
            return pltpu.make_async_copy(rows_v.at[slot], out_hbm.at[pl.ds(off, SC_CHUNK)], wsem[slot])

        def start_gather(c, slot):
            off = pl.multiple_of(base + c * SC_CHUNK, 8)
            pltpu.sync_copy(idx_hbm.at[pl.ds(off, SC_CHUNK)], idx_v.at[slot])
            gather_copy(slot).start()

        start_gather(0, 0)

        @pl.loop(0, n_chunks, step=2)
        def _(c):
            @pl.when(c > 0)
            def _():
                write_copy(c - 1, 1).wait()

            start_gather(c + 1, 1)
            gather_copy(0).wait()
            write_copy(c, 0).start()
            gather_copy(1).wait()
            write_copy(c + 1, 1).start()
            write_copy(c, 0).wait()

            @pl.when(c + 2 < n_chunks)
            def _():
                start_gather(c + 2, 0)

        write_copy(n_chunks - 1, 1).wait()

    return gather_kernel(table, idx)


def _moe_grouped_kernel(te_ref, nt_ref, x_ref, w1_ref, w2_ref, o_ref):
    @pl.when(pl.program_id(0) < nt_ref[0])
    def _():
        t_a, t_b = _unpack_bf16_pair(x_ref[...])
        hid = _expert_hidden(t_a.astype(BF16), t_b.astype(BF16), w1_ref[0])
        o_ref[...] = _pack_bf16_pair(_dot(hid.astype(BF16), w2_ref[0]))


def _moe_grouped(xs, tile_expert, n_tiles, w):
    rows = xs.shape[0]
    row = lambda i, te, nt: (i, 0)
    by_expert = lambda i, te, nt: (te[i], 0, 0)
    return pl.pallas_call(
        _moe_grouped_kernel,
        out_shape=jax.ShapeDtypeStruct((rows, D_MODEL // 2), jnp.uint32),
        grid_spec=pltpu.PrefetchScalarGridSpec(
            num_scalar_prefetch=2,
            grid=(rows // MOE_TILE,),
            in_specs=[pl.BlockSpec((MOE_TILE, D_MODEL // 2), row),
                      pl.BlockSpec((1, D_MODEL, 2 * EXPERT_FF), by_expert),
                      pl.BlockSpec((1, EXPERT_FF, D_MODEL), by_expert)],
            out_specs=pl.BlockSpec((MOE_TILE, D_MODEL // 2), row),
        ),
        compiler_params=pltpu.CompilerParams(dimension_semantics=("arbitrary",),
                                             vmem_limit_bytes=VMEM_LIMIT),
        name="moe_grouped",
    )(tile_expert, n_tiles, xs, w["w1e"], w["w2e"])


def _moe_combine_kernel(h_ref, z1_ref, z2_ref, route_ref, y_ref):
    route = route_ref[...]
    g1, g2 = route[:, 2:3], route[:, 3:4]
    half = D_MODEL // 2
    a1, b1 = _unpack_bf16_pair(z1_ref[...])
    a2, b2 = _unpack_bf16_pair(z2_ref[...])
    y_ref[:, :half] = h_ref[:, :half] + g1 * a1 + g2 * a2
    y_ref[:, half:] = h_ref[:, half:] + g1 * b1 + g2 * b2


def _moe_combine(h, z, route, tm):
    t = h.shape[0]
    nb = t // tm
    tok = lambda i: (i, 0)
    return pl.pallas_call(
        _moe_combine_kernel,
        out_shape=jax.ShapeDtypeStruct((t, D_MODEL), F32),
        grid=(nb,),
        in_specs=[pl.BlockSpec((tm, D_MODEL), tok), pl.BlockSpec((tm, D_MODEL // 2), tok),
                  pl.BlockSpec((tm, D_MODEL // 2), lambda i: (i + nb, 0)), pl.BlockSpec((tm, LANES), tok)],
        out_specs=pl.BlockSpec((tm, D_MODEL), tok),
        compiler_params=pltpu.CompilerParams(dimension_semantics=("parallel",),
                                             vmem_limit_bytes=VMEM_LIMIT),
        name="moe_combine",
    )(h, z, z, route)


def _moe_routed(h, t_packed, route, counts, w, tm):
    t = h.shape[0]
    e1 = route[:, 0].astype(jnp.int32)
    e2 = route[:, 1].astype(jnp.int32)
    r1 = route[:, 4].astype(jnp.int32)
    r2 = route[:, 5].astype(jnp.int32)
    cnt = counts[0, EXPERT_LANE0:EXPERT_LANE0 + N_EXPERTS].astype(jnp.int32)
    padded = (cnt + MOE_TILE - 1) // MOE_TILE * MOE_TILE
    ends = jnp.cumsum(padded)
    starts = ends - padded
    pos1 = starts[e1] + r1
    pos2 = starts[e2] + r2
    n_rows = 2 * t + N_EXPERTS * MOE_TILE
    n_tiles_max = n_rows // MOE_TILE
    n_tiles = (ends[N_EXPERTS - 1] // MOE_TILE).astype(jnp.int32)
    tile_start = jnp.arange(n_tiles_max, dtype=jnp.int32) * MOE_TILE
    tile_expert = jnp.searchsorted(ends, jnp.minimum(tile_start, ends[N_EXPERTS - 1] - 1), side="right")
    tile_expert = jnp.minimum(tile_expert, N_EXPERTS - 1).astype(jnp.int32)
    xs = _sc_scatter_rows(t_packed, pos1, pos2, n_rows)
    out = _moe_grouped(xs, tile_expert, n_tiles.reshape(1), w)
    z = _sc_gather_rows(out, jnp.concatenate([pos1, pos2]))
    return _moe_combine(h, z, route, tm)


def _pad_lanes(a, width=LANES):
    return jnp.pad(a, ((0, 0), (0, width - a.shape[1])))


def _prep_weights(norm1, w_in, q_norm, k_norm, conv_w, conv_b, dt_bias, a_log, d_skip, ssm_norm, w_out,
                  norm2, w_grp, b_grp, w_exp, b_exp, w_gate, w_up, w_down):
    w = {}
    w["norm1"] = norm1.reshape(1, D_MODEL)
    w["wqk"] = w_in[:, :K_END].astype(BF16)
    w["wv"] = w_in[:, K_END:V_END].astype(BF16)
    w["wz"] = w_in[:, V_END:Z_END].astype(BF16)
    w["wxbc"] = w_in[:, Z_END:XBC_END].astype(BF16)
    w["wdt"] = _pad_lanes(w_in[:, XBC_END:]).astype(BF16)
    w["qkn"] = jnp.concatenate([jnp.tile(q_norm, N_HEADS), jnp.tile(k_norm, KV_HEADS)]).reshape(1, QK_WIDTH)
    head_of_col = jnp.arange(QK_WIDTH) // HEAD_DIM
    red = (head_of_col[:, None] == jnp.arange(LANES)[None, :])
    w["red"] = red.astype(BF16)
    w["exp"] = red.T.astype(BF16)
    w["conv_w"] = conv_w
    w["conv_b"] = conv_b.reshape(1, CONV_DIM)
    w["dt_bias"] = _pad_lanes(dt_bias.reshape(1, SSM_HEADS))
    w["a_log"] = _pad_lanes(a_log.reshape(1, SSM_HEADS))
    w["d_skip"] = jnp.repeat(d_skip, SSM_HEAD_DIM).reshape(1, SSM_WIDTH)
    w["ssm_norm"] = ssm_norm.reshape(1, SSM_WIDTH)
    idx = jnp.arange(SSD_CHUNK)
    w["tri"] = (idx[None, :] <= idx[:, None]).astype(BF16)
    w["wo_a"] = w_out[:ATTN_WIDTH].astype(BF16)
    w["wo_s"] = w_out[ATTN_WIDTH:].astype(BF16)
    w["norm2"] = norm2.reshape(1, D_MODEL)
    wr = jnp.zeros((D_MODEL, LANES), F32)
    wr = wr.at[:, :N_EGROUPS].set(w_grp).at[:, EXPERT_LANE0:EXPERT_LANE0 + N_EXPERTS].set(w_exp)
    w["wr_hi"] = wr.astype(BF16)
    w["wr_lo"] = (wr - w["wr_hi"].astype(F32)).astype(BF16)
    br = jnp.zeros((1, LANES), F32)
    w["br"] = br.at[0, :N_EGROUPS].set(b_grp).at[0, EXPERT_LANE0:EXPERT_LANE0 + N_EXPERTS].set(b_exp)
    w["w1e"] = jnp.concatenate([w_gate, w_up], axis=-1).astype(BF16)
    w["w2e"] = w_down.astype(BF16)
    return w


def _rope_tables(pos):
    inv = 1.0 / (ROPE_THETA ** (jnp.arange(0, HEAD_DIM, 2, dtype=F32) / HEAD_DIM))
    ang = pos.astype(F32)[:, None] * inv[None, :]
    cos, sin = jnp.cos(ang), jnp.sin(ang)
    reps = LANES // HEAD_DIM
    return (jnp.tile(jnp.concatenate([cos, cos], axis=-1), (1, reps)),
            jnp.tile(jnp.concatenate([-sin, sin], axis=-1), (1, reps)))


def _token_tile(t):
    for tm in (512, 256, 128, 64, 32, 16):
        if t % tm == 0:
            return tm
    raise ValueError(f"token count {t} is not a multiple of 16")


def kernel(x_prompt, x_sample, cache_win_k, cache_win_v, state_conv, state_ssm, norm1, w_in, q_norm, k_norm,
           sinks, conv_w, conv_b, dt_bias, a_log, d_skip, ssm_norm, w_out, norm2, w_grp, b_grp, w_exp, b_exp,
           w_gate, w_up, w_down):
    depth = norm1.shape[0]
    assert depth == 1, "single-layer stack"
    bp, lp, _ = x_prompt.shape
    bsn, ls, _ = x_sample.shape
    assert ls == 1 and lp % WINDOW == 0 and cache_win_k.shape[2] == WINDOW
    l = 0
    w = _prep_weights(norm1[l], w_in[l], q_norm[l], k_norm[l], conv_w[l], conv_b[l], dt_bias[l], a_log[l],
                      d_skip[l], ssm_norm[l], w_out[l], norm2[l], w_grp[l], b_grp[l], w_exp[l], b_exp[l],
                      w_gate[l], w_up[l], w_down[l])
    sink = sinks[l]

    tp = bp * lp
    xp = x_prompt.reshape(tp, D_MODEL)
    tm_p = _token_tile(lp)
    cos_p, sin_p = _rope_tables(jnp.arange(lp, dtype=jnp.int32))
    q, k, v, z, xbc, dt = _inproj(xp, w, cos_p, sin_p, tm_p, lp // tm_p)
    attn = _attn_prompt(q, k, v, sink, bp, lp)
    ssm, st_p = _ssd_prompt(xbc, z, dt, w, bp, lp)
    h, t, route, counts = _outproj_router(xp, attn, ssm, w, tm_p)
    y_prompt = _moe_routed(h, t, route, counts, w, tm_p).reshape(bp, lp, D_MODEL)
    k3 = k.reshape(bp, lp, KV_HEADS, HEAD_DIM)
    v3 = v.reshape(bp, lp, KV_HEADS, HEAD_DIM)
    win_k_p = k3[:, lp - WINDOW:][None]
    win_v_p = v3[:, lp - WINDOW:][None]
    conv_p = xbc.reshape(bp, lp, CONV_DIM)[:, lp - (CONV_K - 1):][None]
    ssm_p = st_p.reshape(1, bp, SSM_HEADS, SSM_HEAD_DIM, SSM_STATE)

    xs2 = x_sample.reshape(bsn, D_MODEL)
    tm_s = _token_tile(bsn)
    cos_s, sin_s = _rope_tables(jnp.full((tm_s,), PAST_LEN, jnp.int32))
    q_s, k_s, v_s, z_s, xbc_s, dt_s = _inproj(xs2, w, cos_s, sin_s, tm_s, 1)
    q4 = q_s.reshape(bsn, KV_HEADS, N_HEADS // KV_HEADS, HEAD_DIM)
    zq = jnp.zeros_like(q4[:, 0])
    qx = jnp.concatenate([jnp.concatenate([q4[:, 0], zq], axis=-1),
                          jnp.concatenate([zq, q4[:, 1]], axis=-1)], axis=1)
    qx = jnp.pad(qx, ((0, 0), (0, BF16_ROWS - N_HEADS), (0, 0)))
    sink_x = jnp.pad(jnp.broadcast_to(sink[:, None], (N_HEADS, LANES)), ((0, BF16_ROWS - N_HEADS), (0, 0)))
    kc = cache_win_k[l].reshape(bsn, WINDOW, KV_WIDTH)
    vc = cache_win_v[l].reshape(bsn, WINDOW, KV_WIDTH)
    ko, vo, attn_s = _attn_sample(qx, kc, k_s.reshape(bsn, 1, KV_WIDTH), vc, v_s.reshape(bsn, 1, KV_WIDTH),
                               sink_x, 8)
    cprev_t = jnp.transpose(state_conv[l], (1, 0, 2))
    h0 = state_ssm[l].reshape(bsn, HEAD_PAIRS, LANES, SSM_STATE)
    ssm_s, cnew_t, h1 = _ssd_sample(xbc_s, z_s, dt_s, cprev_t, h0, w, 16)
    h_s, t_s, route_s, _ = _outproj_router(xs2, attn_s, ssm_s, w, tm_s)
    y_sample = _moe_dense(h_s, t_s, route_s, w, tm_s).reshape(bsn, 1, D_MODEL)
    win_k_s = ko.reshape(1, bsn, WINDOW, KV_HEADS, HEAD_DIM)
    win_v_s = vo.reshape(1, bsn, WINDOW, KV_HEADS, HEAD_DIM)
    conv_s = jnp.transpose(cnew_t, (1, 0, 2))[None]
    ssm_s_state = h1.reshape(1, bsn, SSM_HEADS, SSM_HEAD_DIM, SSM_STATE)

    return (y_prompt, y_sample, win_k_p, win_v_p, conv_p, ssm_p, win_k_s, win_v_s, conv_s, ssm_s_state)
```

```python
import functools
import math

import jax
import jax.numpy as jnp
from jax import lax
from jax.experimental import pallas as pl
from jax.experimental.pallas import tpu as pltpu
from jax.experimental.pallas import tpu_sc as plsc

F32 = jnp.float32
BF16 = jnp.bfloat16

D_MODEL = 1024
HEAD_DIM = 64
N_HEADS = 8
KV_HEADS = 2
WINDOW = 128
ATTN_WIDTH = N_HEADS * HEAD_DIM
QK_WIDTH = ATTN_WIDTH + KV_HEADS * HEAD_DIM
KV_WIDTH = KV_HEADS * HEAD_DIM
ATTN_SCALE = HEAD_DIM ** -0.5
ROPE_THETA = 10000.0
SSM_WIDTH = 512
SSM_HEADS = 8
SSM_HEAD_DIM = 64
SSM_GROUPS = 2
SSM_STATE = 128
CONV_K = 4
CONV_DIM = SSM_WIDTH + 2 * SSM_GROUPS * SSM_STATE
SSD_CHUNK = 128
N_EGROUPS = 4
EXP_PER_GROUP = 8
N_EXPERTS = 32
EXPERT_FF = 128
EPS = 1e-6
PAST_LEN = 16384

LANES = 128
BF16_ROWS = 16
HEAD_PAIRS = SSM_HEADS // 2
EXPERT_LANE0 = 32
VMEM_LIMIT = 56 * 1024 * 1024
MOE_TILE = 256
SC_CORES = 2
SC_SUBCORES = 16
SC_WORKERS = SC_CORES * SC_SUBCORES
SC_CHUNK = 64

Q_END = ATTN_WIDTH
K_END = Q_END + KV_WIDTH
V_END = K_END + KV_WIDTH
Z_END = V_END + SSM_WIDTH
XBC_END = Z_END + CONV_DIM


def _dot(a, b):
    return jnp.dot(a, b, preferred_element_type=F32)


def _dot_nt(a, b):
    return lax.dot_general(a, b, (((1,), (1,)), ((), ())), preferred_element_type=F32)


def _split2(v):
    hi = v.astype(BF16)
    lo = (v - hi.astype(F32)).astype(BF16)
    return hi, lo


def _split3(v):
    hi = v.astype(BF16)
    r = v - hi.astype(F32)
    mid = r.astype(BF16)
    lo = (r - mid.astype(F32)).astype(BF16)
    return hi, mid, lo


def _silu(x):
    return x * jax.nn.sigmoid(x)


def _softplus(x):
    return jnp.maximum(x, 0.0) + jnp.log1p(jnp.exp(-jnp.abs(x)))


def _lane_bcast_pairs(v, n_pairs):
    r = v.shape[0]
    lo = lax.broadcasted_iota(jnp.int32, (r, LANES), 1) < HEAD_DIM
    slabs = []
    for j in range(n_pairs):
        a = jnp.broadcast_to(v[:, 2 * j:2 * j + 1], (r, LANES))
        b = jnp.broadcast_to(v[:, 2 * j + 1:2 * j + 2], (r, LANES))
        slabs.append(jnp.where(lo, a, b))
    return jnp.concatenate(slabs, axis=1)


def _inproj_kernel(x_ref, n1_ref, wqk_ref, wv_ref, wz_ref, wxbc_ref, wdt_ref, qkn_ref,
                   cos_ref, sin_ref, red_ref, exp_ref,
                   q_ref, k_ref, v_ref, z_ref, xbc_ref, dt_ref):
    x = x_ref[...]
    ms = jnp.mean(x * x, axis=-1, keepdims=True)
    xn = (x * lax.rsqrt(ms + EPS) * n1_ref[...]).astype(BF16)
    v_ref[...] = _dot(xn, wv_ref[...])
    z_ref[...] = _dot(xn, wz_ref[...])
    xbc_ref[...] = _dot(xn, wxbc_ref[...])
    dt_ref[...] = _dot(xn, wdt_ref[...])
    qk = _dot(xn, wqk_ref[...])
    sq_hi, sq_lo = _split2(qk * qk)
    ss = _dot(sq_hi, red_ref[...]) + _dot(sq_lo, red_ref[...])
    inv = lax.rsqrt(ss * (1.0 / HEAD_DIM) + EPS)
    inv_hi, inv_lo = _split2(inv)
    inv_x = _dot(inv_hi, exp_ref[...]) + _dot(inv_lo, exp_ref[...])
    qkn = qk * inv_x * qkn_ref[...]
    cos = cos_ref[...]
    sin = sin_ref[...]
    lane = lax.broadcasted_iota(jnp.int32, (x.shape[0], LANES), 1)
    first_half = (lane % HEAD_DIM) < (HEAD_DIM // 2)
    for c in range(QK_WIDTH // LANES):
        xc = qkn[:, c * LANES:(c + 1) * LANES]
        partner = jnp.where(first_half,
                            pltpu.roll(xc, LANES - HEAD_DIM // 2, axis=1),
                            pltpu.roll(xc, HEAD_DIM // 2, axis=1))
        rot = xc * cos + partner * sin
        if c < ATTN_WIDTH // LANES:
            q_ref[:, c * LANES:(c + 1) * LANES] = (rot * ATTN_SCALE).astype(BF16)
        else:
            k_ref[...] = rot


def _inproj(x2d, w, cos_tab, sin_tab, tm, n_pos_blocks):
    t = x2d.shape[0]
    grid = (t // tm,)
    tok = lambda i: (i, 0)
    const = lambda i: (0, 0)
    pos = lambda i: (i % n_pos_blocks, 0)
    full = lambda a: pl.BlockSpec(a.shape, const)
    out_shapes = (
        jax.ShapeDtypeStruct((t, ATTN_WIDTH), BF16),
        jax.ShapeDtypeStruct((t, KV_WIDTH), F32),
        jax.ShapeDtypeStruct((t, KV_WIDTH), F32),
        jax.ShapeDtypeStruct((t, SSM_WIDTH), F32),
        jax.ShapeDtypeStruct((t, CONV_DIM), F32),
        jax.ShapeDtypeStruct((t, LANES), F32),
    )
    return pl.pallas_call(
        _inproj_kernel,
        out_shape=out_shapes,
        grid=grid,
        in_specs=[
            pl.BlockSpec((tm, D_MODEL), tok),
            full(w["norm1"]), full(w["wqk"]), full(w["wv"]), full(w["wz"]), full(w["wxbc"]),
            full(w["wdt"]), full(w["qkn"]),
            pl.BlockSpec((tm, LANES), pos), pl.BlockSpec((tm, LANES), pos),
            full(w["red"]), full(w["exp"]),
        ],
        out_specs=(
            pl.BlockSpec((tm, ATTN_WIDTH), tok), pl.BlockSpec((tm, KV_WIDTH), tok),
            pl.BlockSpec((tm, KV_WIDTH), tok), pl.BlockSpec((tm, SSM_WIDTH), tok),
            pl.BlockSpec((tm, CONV_DIM), tok), pl.BlockSpec((tm, LANES), tok),
        ),
        compiler_params=pltpu.CompilerParams(dimension_semantics=("parallel",),
                                             vmem_limit_bytes=VMEM_LIMIT),
        name="inproj",
    )(x2d, w["norm1"], w["wqk"], w["wv"], w["wz"], w["wxbc"], w["wdt"], w["qkn"],
      cos_tab, sin_tab, w["red"], w["exp"])


def _pair_rhs(blk, g):
    lo = lax.broadcasted_iota(jnp.int32, blk.shape, 1) < HEAD_DIM
    swapped = pltpu.roll(blk, HEAD_DIM, axis=1)
    if g == 0:
        top = jnp.where(lo, blk, 0.0)
        bot = jnp.where(lo, 0.0, swapped)
    else:
        top = jnp.where(lo, swapped, 0.0)
        bot = jnp.where(lo, 0.0, blk)
    return jnp.concatenate([top, bot], axis=0).astype(BF16)


def _attn_kernel(sink_ref, q_ref, kc_ref, kp_ref, vc_ref, vp_ref, o_ref):
    has_prev = pl.program_id(1) > 0
    blk = q_ref.shape[0]
    qi = lax.broadcasted_iota(jnp.int32, (blk, 2 * blk), 0)
    kj = lax.broadcasted_iota(jnp.int32, (blk, 2 * blk), 1) % blk
    cur_ok = kj <= qi
    prev_ok = jnp.logical_and(jnp.logical_not(cur_ok), has_prev)
    lo = lax.broadcasted_iota(jnp.int32, (blk, LANES), 1) < HEAD_DIM
    kc = kc_ref[...]
    kp = kp_ref[...]
    vc = vc_ref[...]
    vp = vp_ref[...]
    for g in range(KV_HEADS):
        kc2, kp2 = _pair_rhs(kc, g), _pair_rhs(kp, g)
        vc2, vp2 = _pair_rhs(vc, g), _pair_rhs(vp, g)
        for r in range(N_HEADS // KV_HEADS // 2):
            pair = g * 2 + r
            q2 = q_ref[:, pair * LANES:(pair + 1) * LANES]
            s_cur = _dot_nt(q2, kc2)
            s_prev = _dot_nt(q2, kp2)
            s = jnp.where(cur_ok, s_cur, jnp.where(prev_ok, s_prev, -jnp.inf))
            outs = []
            ps = []
            dens = []
            for hh in range(2):
                sink = sink_ref[2 * pair + hh]
                sh = s[:, hh * blk:(hh + 1) * blk]
                m = jnp.maximum(jnp.max(sh, axis=-1, keepdims=True), sink)
                p = jnp.exp(sh - m)
                dens.append(jnp.sum(p, axis=-1, keepdims=True) + jnp.exp(sink - m))
                ps.append(p)
            p2 = jnp.concatenate(ps, axis=1)
            p_cur = jnp.where(cur_ok, p2, 0.0).astype(BF16)
            p_prev = jnp.where(cur_ok, 0.0, p2).astype(BF16)
            o2 = _dot(p_cur, vc2) + _dot(p_prev, vp2)
            den = jnp.where(lo, dens[0], dens[1])
            o_ref[:, pair * LANES:(pair + 1) * LANES] = (o2 / den).astype(BF16)


def _attn_prompt(q, k, v, sinks, batch, seq):
    nb = seq // WINDOW
    cur = lambda b, j, s: (b * nb + j, 0)
    prev = lambda b, j, s: (b * nb + jnp.maximum(j - 1, 0), 0)
    return pl.pallas_call(
        _attn_kernel,
        out_shape=jax.ShapeDtypeStruct((batch * seq, ATTN_WIDTH), BF16),
        grid_spec=pltpu.PrefetchScalarGridSpec(
            num_scalar_prefetch=1,
            grid=(batch, nb),
            in_specs=[
                pl.BlockSpec((WINDOW, ATTN_WIDTH), cur),
                pl.BlockSpec((WINDOW, KV_WIDTH), cur), pl.BlockSpec((WINDOW, KV_WIDTH), prev),
                pl.BlockSpec((WINDOW, KV_WIDTH), cur), pl.BlockSpec((WINDOW, KV_WIDTH), prev),
            ],
            out_specs=pl.BlockSpec((WINDOW, ATTN_WIDTH), cur),
        ),
        compiler_params=pltpu.CompilerParams(dimension_semantics=("parallel", "parallel"),
                                             vmem_limit_bytes=VMEM_LIMIT),
        name="attn_prompt",
    )(sinks, q, k, k, v, v)


def _ssd_kernel(xbc_ref, z_ref, dt_ref, convw_ref, convb_ref, dtb_ref, alog_ref, dskip_ref, nw_ref,
                tri_ref, y_ref, st_ref, buf_ref, state_ref):
    c = pl.program_id(1)
    cl = xbc_ref.shape[0]
    halo = buf_ref.shape[0] - cl

    @pl.when(c == 0)
    def _():
        buf_ref[0:halo, :] = jnp.zeros((halo, CONV_DIM), F32)
        state_ref[...] = jnp.zeros(state_ref.shape, F32)

    x_raw = xbc_ref[...]
    buf_ref[halo:halo + cl, :] = x_raw
    conv = convb_ref[...]
    for j in range(CONV_K):
        off = halo - (CONV_K - 1) + j
        conv = conv + buf_ref[off:off + cl, :] * convw_ref[j:j + 1, :]
    buf_ref[0:halo, :] = x_raw[cl - halo:cl, :]
    act = _silu(conv)
    xs = act[:, :SSM_WIDTH]
    bm = act[:, SSM_WIDTH:SSM_WIDTH + SSM_GROUPS * SSM_STATE].astype(BF16)
    cm = act[:, SSM_WIDTH + SSM_GROUPS * SSM_STATE:].astype(BF16)

    lane = lax.broadcasted_iota(jnp.int32, (1, LANES), 1)
    a_neg = jnp.where(lane < SSM_HEADS, -jnp.exp(alog_ref[...]), 0.0)
    dt = _softplus(dt_ref[...] + dtb_ref[...])
    dta = dt * a_neg
    tri = tri_ref[...]
    p_hi, p_mid, p_lo = _split3(dta)
    a_col = _dot(tri, p_hi) + _dot(tri, p_mid) + _dot(tri, p_lo)
    a_last = a_col[cl - 1:cl, :]
    a_row = a_col.T
    dt_x = _lane_bcast_pairs(dt, HEAD_PAIRS)
    ecol_x = _lane_bcast_pairs(jnp.exp(a_col), HEAD_PAIRS)
    dte_x = _lane_bcast_pairs(jnp.exp(a_last - a_col), HEAD_PAIRS)
    e_last = jnp.exp(a_last)
    xdt = xs * dt_x

    li = lax.broadcasted_iota(jnp.int32, (cl, cl), 0)
    si = lax.broadcasted_iota(jnp.int32, (cl, cl), 1)
    causal = si <= li
    lo = lax.broadcasted_iota(jnp.int32, (cl, LANES), 1) < SSM_HEAD_DIM
    row_lo = lax.broadcasted_iota(jnp.int32, (LANES, SSM_STATE), 0) < SSM_HEAD_DIM

    ys = []
    for g in range(SSM_GROUPS):
        b_g = bm[:, g * SSM_STATE:(g + 1) * SSM_STATE]
        c_g = cm[:, g * SSM_STATE:(g + 1) * SSM_STATE]
        cb = _dot_nt(c_g, b_g)
        for r in range(HEAD_PAIRS // SSM_GROUPS):
            j = g * (HEAD_PAIRS // SSM_GROUPS) + r
            sl = slice(j * LANES, (j + 1) * LANES)
            xdt_p = xdt[:, sl]
            ms = []
            for hh in range(2):
                h = 2 * j + hh
                seg = a_col[:, h:h + 1] - a_row[h:h + 1, :]
                ms.append(cb * jnp.exp(jnp.where(causal, seg, -jnp.inf)))
            m2 = jnp.concatenate(ms, axis=1).astype(BF16)
            rhs = jnp.concatenate([jnp.where(lo, xdt_p, 0.0), jnp.where(lo, 0.0, xdt_p)],
                                  axis=0).astype(BF16)
            y_diag = _dot(m2, rhs)
            st = state_ref[j]
            y_off = _dot_nt(c_g, st.astype(BF16)) * ecol_x[:, sl]
            xdt_e = (xdt_p * dte_x[:, sl]).T.astype(BF16)
            d_a = e_last[:, 2 * j:2 * j + 1]
            d_b = e_last[:, 2 * j + 1:2 * j + 2]
            decay = jnp.where(row_lo, jnp.broadcast_to(d_a, row_lo.shape), jnp.broadcast_to(d_b, row_lo.shape))
            state_ref[j] = decay * st + _dot(xdt_e, b_g)
            ys.append(y_diag + y_off + dskip_ref[:, sl] * xs[:, sl])
    y = jnp.concatenate(ys, axis=1)
    gated = y * _silu(z_ref[...])
    gw = SSM_WIDTH // SSM_GROUPS
    outs = []
    for g in range(SSM_GROUPS):
        gg = gated[:, g * gw:(g + 1) * gw]
        outs.append(gg * lax.rsqrt(jnp.mean(gg * gg, axis=-1, keepdims=True) + EPS))
    y_ref[...] = (jnp.concatenate(outs, axis=1) * nw_ref[...]).astype(BF16)

    @pl.when(c == pl.num_programs(1) - 1)
    def _():
        st_ref[0] = state_ref[...]


def _ssd_prompt(xbc, z, dt, w, batch, seq):
    nc = seq // SSD_CHUNK
    tok = lambda b, c: (b * nc + c, 0)
    const = lambda b, c: (0, 0)
    full = lambda a: pl.BlockSpec(a.shape, const)
    return pl.pallas_call(
        _ssd_kernel,
        out_shape=(jax.ShapeDtypeStruct((batch * seq, SSM_WIDTH), BF16),
                   jax.ShapeDtypeStruct((batch, HEAD_PAIRS, LANES, SSM_STATE), F32)),
        grid=(batch, nc),
        in_specs=[
            pl.BlockSpec((SSD_CHUNK, CONV_DIM), tok), pl.BlockSpec((SSD_CHUNK, SSM_WIDTH), tok),
            pl.BlockSpec((SSD_CHUNK, LANES), tok),
            full(w["conv_w"]), full(w["conv_b"]), full(w["dt_bias"]), full(w["a_log"]),
            full(w["d_skip"]), full(w["ssm_norm"]), full(w["tri"]),
        ],
        out_specs=(pl.BlockSpec((SSD_CHUNK, SSM_WIDTH), tok),
                   pl.BlockSpec((1, HEAD_PAIRS, LANES, SSM_STATE), lambda b, c: (b, 0, 0, 0))),
        scratch_shapes=[pltpu.VMEM((SSD_CHUNK + 8, CONV_DIM), F32),
                        pltpu.VMEM((HEAD_PAIRS, LANES, SSM_STATE), F32)],
        compiler_params=pltpu.CompilerParams(dimension_semantics=("parallel", "arbitrary"),
                                             vmem_limit_bytes=VMEM_LIMIT),
        name="ssd_prompt",
    )(xbc, z, dt, w["conv_w"], w["conv_b"], w["dt_bias"], w["a_log"], w["d_skip"], w["ssm_norm"], w["tri"])


def _attn_sample_kernel(qx_ref, kc_ref, kn_ref, vc_ref, vn_ref, sink_ref, ko_ref, vo_ref, o_ref):
    bs = qx_ref.shape[0]
    w = kc_ref.shape[1]
    sink = sink_ref[...]
    lo = lax.broadcasted_iota(jnp.int32, (1, LANES), 1) < HEAD_DIM
    for i in range(bs):
        ko_ref[i, 0:w - 1, :] = kc_ref[i, 1:w, :]
        ko_ref[i, w - 1:w, :] = kn_ref[i]
        vo_ref[i, 0:w - 1, :] = vc_ref[i, 1:w, :]
        vo_ref[i, w - 1:w, :] = vn_ref[i]
        s = _dot_nt(qx_ref[i], ko_ref[i].astype(BF16))
        m = jnp.maximum(jnp.max(s, axis=-1, keepdims=True), sink)
        p = jnp.exp(s - m)
        den = jnp.sum(p, axis=-1, keepdims=True) + jnp.exp(sink - m)
        o = _dot(p.astype(BF16), vo_ref[i].astype(BF16)) / den
        o_sw = pltpu.roll(o, HEAD_DIM, axis=1)
        for j in range(N_HEADS // 2):
            a, b = (o, o_sw) if j < N_HEADS // 4 else (o_sw, o)
            o_ref[i:i + 1, j * LANES:(j + 1) * LANES] = jnp.where(lo, a[2 * j:2 * j + 1], b[2 * j + 1:2 * j + 2])


def _attn_sample(qx, kc, kn, vc, vn, sink_x, bs):
    n, w = kc.shape[0], kc.shape[1]
    blk3 = lambda i: (i, 0, 0)
    return pl.pallas_call(
        _attn_sample_kernel,
        out_shape=(jax.ShapeDtypeStruct((n, w, KV_WIDTH), F32),
                   jax.ShapeDtypeStruct((n, w, KV_WIDTH), F32),
                   jax.ShapeDtypeStruct((n, ATTN_WIDTH), F32)),
        grid=(n // bs,),
        in_specs=[
            pl.BlockSpec((bs, BF16_ROWS, LANES), blk3),
            pl.BlockSpec((bs, w, KV_WIDTH), blk3), pl.BlockSpec((bs, 1, KV_WIDTH), blk3),
            pl.BlockSpec((bs, w, KV_WIDTH), blk3), pl.BlockSpec((bs, 1, KV_WIDTH), blk3),
            pl.BlockSpec(sink_x.shape, lambda i: (0, 0)),
        ],
        out_specs=(pl.BlockSpec((bs, w, KV_WIDTH), blk3), pl.BlockSpec((bs, w, KV_WIDTH), blk3),
                   pl.BlockSpec((bs, ATTN_WIDTH), lambda i: (i, 0))),
        compiler_params=pltpu.CompilerParams(dimension_semantics=("parallel",),
                                             vmem_limit_bytes=VMEM_LIMIT),
        name="attn_sample",
    )(qx, kc, kn, vc, vn, sink_x)


def _ssd_sample_kernel(xbc_ref, z_ref, dt_ref, cprev_ref, h0_ref, convw_ref, convb_ref, dtb_ref, alog_ref,
                       dskip_ref, nw_ref, y_ref, cnew_ref, h1_ref):
    bs = xbc_ref.shape[0]
    x_raw = xbc_ref[...]
    conv = convb_ref[...] + x_raw * convw_ref[CONV_K - 1:CONV_K, :]
    for j in range(CONV_K - 1):
        conv = conv + cprev_ref[j] * convw_ref[j:j + 1, :]
    for j in range(CONV_K - 2):
        cnew_ref[j] = cprev_ref[j + 1]
    cnew_ref[CONV_K - 2] = x_raw
    act = _silu(conv)
    xs = act[:, :SSM_WIDTH]
    bm = act[:, SSM_WIDTH:SSM_WIDTH + SSM_GROUPS * SSM_STATE].astype(BF16)
    cm = act[:, SSM_WIDTH + SSM_GROUPS * SSM_STATE:].astype(BF16)
    lane = lax.broadcasted_iota(jnp.int32, (1, LANES), 1)
    a_neg = jnp.where(lane < SSM_HEADS, -jnp.exp(alog_ref[...]), 0.0)
    dt = _softplus(dt_ref[...] + dtb_ref[...])
    dec = jnp.exp(dt * a_neg)
    xdt = xs * _lane_bcast_pairs(dt, HEAD_PAIRS)
    rowid = lax.broadcasted_iota(jnp.int32, (bs, LANES), 0)
    row_lo = lax.broadcasted_iota(jnp.int32, (LANES, SSM_STATE), 0) < SSM_HEAD_DIM
    ys = []
    for j in range(HEAD_PAIRS):
        g = j // (HEAD_PAIRS // SSM_GROUPS)
        sl = slice(j * LANES, (j + 1) * LANES)
        b_g = bm[:, g * SSM_STATE:(g + 1) * SSM_STATE]
        c_g = cm[:, g * SSM_STATE:(g + 1) * SSM_STATE]
        xdt_p = xdt[:, sl]
        y_p = jnp.zeros((bs, LANES), F32)
        for i in range(bs):
            xi = jnp.where(rowid == i, xdt_p, 0.0).T.astype(BF16)
            d_a = dec[i:i + 1, 2 * j:2 * j + 1]
            d_b = dec[i:i + 1, 2 * j + 1:2 * j + 2]
            decay = jnp.where(row_lo, jnp.broadcast_to(d_a, row_lo.shape), jnp.broadcast_to(d_b, row_lo.shape))
            new = decay * h0_ref[i, j] + _dot(xi, b_g)
            h1_ref[i, j] = new
            y_p = y_p + jnp.where(rowid == i, _dot_nt(c_g, new.astype(BF16)), 0.0)
        ys.append(y_p + dskip_ref[:, sl] * xs[:, sl])
    y = jnp.concatenate(ys, axis=1)
    gated = y * _silu(z_ref[...])
    gw = SSM_WIDTH // SSM_GROUPS
    outs = []
    for g in range(SSM_GROUPS):
        gg = gated[:, g * gw:(g + 1) * gw]
        outs.append(gg * lax.rsqrt(jnp.mean(gg * gg, axis=-1, keepdims=True) + EPS))
    y_ref[...] = (jnp.concatenate(outs, axis=1) * nw_ref[...]).astype(BF16)


def _ssd_sample(xbc, z, dt, cprev_t, h0, w, bs):
    n = xbc.shape[0]
    tok = lambda i: (i, 0)
    const = lambda i: (0, 0)
    full = lambda a: pl.BlockSpec(a.shape, const)
    return pl.pallas_call(
        _ssd_sample_kernel,
        out_shape=(jax.ShapeDtypeStruct((n, SSM_WIDTH), BF16),
                   jax.ShapeDtypeStruct((CONV_K - 1, n, CONV_DIM), F32),
                   jax.ShapeDtypeStruct((n, HEAD_PAIRS, LANES, SSM_STATE), F32)),
        grid=(n // bs,),
        in_specs=[
            pl.BlockSpec((bs, CONV_DIM), tok), pl.BlockSpec((bs, SSM_WIDTH), tok),
            pl.BlockSpec((bs, LANES), tok),
            pl.BlockSpec((CONV_K - 1, bs, CONV_DIM), lambda i: (0, i, 0)),
            pl.BlockSpec((bs, HEAD_PAIRS, LANES, SSM_STATE), lambda i: (i, 0, 0, 0)),
            full(w["conv_w"]), full(w["conv_b"]), full(w["dt_bias"]), full(w["a_log"]),
            full(w["d_skip"]), full(w["ssm_norm"]),
        ],
        out_specs=(pl.BlockSpec((bs, SSM_WIDTH), tok),
                   pl.BlockSpec((CONV_K - 1, bs, CONV_DIM), lambda i: (0, i, 0)),
                   pl.BlockSpec((bs, HEAD_PAIRS, LANES, SSM_STATE), lambda i: (i, 0, 0, 0))),
        compiler_params=pltpu.CompilerParams(dimension_semantics=("parallel",),
                                             vmem_limit_bytes=VMEM_LIMIT),
        name="ssd_sample",
    )(xbc, z, dt, cprev_t, h0, w["conv_w"], w["conv_b"], w["dt_bias"], w["a_log"], w["d_skip"], w["ssm_norm"])


def _pack_bf16_pair(v):
    c = v.shape[1] // 2
    hi = lax.bitcast_convert_type(v[:, :c].astype(BF16).astype(F32), jnp.uint32)
    lo = lax.bitcast_convert_type(v[:, c:].astype(BF16).astype(F32), jnp.uint32)
    return hi | (lo >> 16)


def _unpack_bf16_pair(word):
    a = lax.bitcast_convert_type(word & jnp.uint32(0xFFFF0000), F32)
    b = lax.bitcast_convert_type(word << 16, F32)
    return a, b


def _outproj_router_kernel(x_ref, a_ref, s_ref, wo_a_ref, wo_s_ref, n2_ref, wr_hi_ref, wr_lo_ref, br_ref, tri_ref,
                           h_ref, t_ref, route_ref, cnt_ref, carry_ref):
    @pl.when(pl.program_id(0) == 0)
    def _():
        carry_ref[...] = jnp.zeros(carry_ref.shape, F32)

    h = x_ref[...] + _dot(a_ref[...].astype(BF16), wo_a_ref[...]) + _dot(s_ref[...].astype(BF16), wo_s_ref[...])
    h_ref[...] = h
    ms = jnp.mean(h * h, axis=-1, keepdims=True)
    t = h * lax.rsqrt(ms + EPS) * n2_ref[...]
    t_hi, t_lo = _split2(t)
    t_ref[...] = _pack_bf16_pair(t)
    logits = (_dot(t_hi, wr_hi_ref[...]) + _dot(t_lo, wr_hi_ref[...]) + _dot(t_hi, wr_lo_ref[...])
              + br_ref[...])
    lane = lax.broadcasted_iota(jnp.int32, logits.shape, 1)
    lane_f = lane.astype(F32)
    big = float(LANES)
    ninf = -jnp.inf
    glog = jnp.where(lane < N_EGROUPS, logits, ninf)
    gmax = jnp.max(glog, axis=-1, keepdims=True)
    g_top = 1.0 / jnp.sum(jnp.exp(glog - gmax), axis=-1, keepdims=True)
    g_idx = jnp.min(jnp.where(glog == gmax, lane_f, big), axis=-1, keepdims=True)
    e_of_lane = lane - EXPERT_LANE0
    in_grp = jnp.logical_and(jnp.logical_and(e_of_lane >= 0, e_of_lane < N_EXPERTS),
                             (e_of_lane // EXP_PER_GROUP).astype(F32) == g_idx)
    ml = jnp.where(in_grp, logits, ninf)
    m1 = jnp.max(ml, axis=-1, keepdims=True)
    i1 = jnp.min(jnp.where(ml == m1, lane_f, big), axis=-1, keepdims=True)
    ml2 = jnp.where(lane_f == i1, ninf, ml)
    m2 = jnp.max(ml2, axis=-1, keepdims=True)
    i2 = jnp.min(jnp.where(ml2 == m2, lane_f, big), axis=-1, keepdims=True)
    r = jnp.exp(m2 - m1)
    w1 = g_top / (1.0 + r)
    w2 = g_top * r / (1.0 + r)
    is1 = lane_f == i1
    is2 = lane_f == i2
    onehot = jnp.where(jnp.logical_or(is1, is2), 1.0, 0.0)
    cum = _dot(tri_ref[...], onehot.astype(BF16)) + carry_ref[0:1, :]
    before = cum - onehot
    rank1 = jnp.sum(jnp.where(is1, before, 0.0), axis=-1, keepdims=True)
    rank2 = jnp.sum(jnp.where(is2, before, 0.0), axis=-1, keepdims=True)
    tm = cum.shape[0]
    carry_ref[0:1, :] = cum[tm - 1:tm, :]
    cnt_ref[...] = jnp.broadcast_to(cum[tm - 1:tm, :], cnt_ref.shape)
    fields = (i1 - EXPERT_LANE0, i2 - EXPERT_LANE0, w1, w2, rank1, rank2)
    route = jnp.zeros(logits.shape, F32)
    for pos, val in enumerate(fields):
        route = jnp.where(lane == pos, val, route)
    route_ref[...] = route


def _outproj_router(x2d, attn, ssm, w, tm):
    t = x2d.shape[0]
    tok = lambda i: (i, 0)
    const = lambda i: (0, 0)
    full = lambda a: pl.BlockSpec(a.shape, const)
    idx = jnp.arange(tm)
    tri = (idx[None, :] <= idx[:, None]).astype(BF16)
    return pl.pallas_call(
        _outproj_router_kernel,
        out_shape=(jax.ShapeDtypeStruct((t, D_MODEL), F32), jax.ShapeDtypeStruct((t, D_MODEL // 2), jnp.uint32),
                   jax.ShapeDtypeStruct((t, LANES), F32), jax.ShapeDtypeStruct((8, LANES), F32)),
        grid=(t // tm,),
        in_specs=[
            pl.BlockSpec((tm, D_MODEL), tok), pl.BlockSpec((tm, ATTN_WIDTH), tok),
            pl.BlockSpec((tm, SSM_WIDTH), tok),
            full(w["wo_a"]), full(w["wo_s"]), full(w["norm2"]), full(w["wr_hi"]), full(w["wr_lo"]),
            full(w["br"]), full(tri),
        ],
        out_specs=(pl.BlockSpec((tm, D_MODEL), tok), pl.BlockSpec((tm, D_MODEL // 2), tok),
                   pl.BlockSpec((tm, LANES), tok), pl.BlockSpec((8, LANES), const)),
        scratch_shapes=[pltpu.VMEM((8, LANES), F32)],
        compiler_params=pltpu.CompilerParams(dimension_semantics=("arbitrary",),
                                             vmem_limit_bytes=VMEM_LIMIT),
        name="outproj_router",
    )(x2d, attn, ssm, w["wo_a"], w["wo_s"], w["norm2"], w["wr_hi"], w["wr_lo"], w["br"], tri)


def _expert_hidden(t_a, t_b, w1):
    half = D_MODEL // 2
    gu = _dot(t_a, w1[:half]) + _dot(t_b, w1[half:])
    return _silu(gu[:, :EXPERT_FF]) * gu[:, EXPERT_FF:]


def _moe_dense_kernel(h_ref, t_ref, route_ref, w1_ref, w2_ref, y_ref, acc_ref):
    t_a, t_b = _unpack_bf16_pair(t_ref[...])
    t_a, t_b = t_a.astype(BF16), t_b.astype(BF16)
    route = route_ref[...]
    e1, e2, g1, g2 = route[:, 0:1], route[:, 1:2], route[:, 2:3], route[:, 3:4]
    acc_ref[...] = h_ref[...]
    for e in range(N_EXPERTS):
        hid = _expert_hidden(t_a, t_b, w1_ref[e])
        c_e = jnp.where(e1 == float(e), g1, 0.0) + jnp.where(e2 == float(e), g2, 0.0)
        acc_ref[...] += _dot((hid * c_e).astype(BF16), w2_ref[e])
    y_ref[...] = acc_ref[...]


def _moe_dense(h, t, route, w, tm):
    n = h.shape[0]
    tok = lambda i: (i, 0)
    resident = lambda a: pl.BlockSpec(a.shape, lambda i: (0, 0, 0), pipeline_mode=pl.Buffered(1))
    return pl.pallas_call(
        _moe_dense_kernel,
        out_shape=jax.ShapeDtypeStruct((n, D_MODEL), F32),
        grid=(n // tm,),
        in_specs=[pl.BlockSpec((tm, D_MODEL), tok), pl.BlockSpec((tm, D_MODEL // 2), tok),
                  pl.BlockSpec((tm, LANES), tok), resident(w["w1e"]), resident(w["w2e"])],
        out_specs=pl.BlockSpec((tm, D_MODEL), tok),
        scratch_shapes=[pltpu.VMEM((tm, D_MODEL), F32)],
        compiler_params=pltpu.CompilerParams(dimension_semantics=("parallel",),
                                             vmem_limit_bytes=VMEM_LIMIT),
        name="moe_dense",
    )(h, t, route, w["w1e"], w["w2e"])


def _sc_scatter_rows(src, pos1, pos2, n_out):
    t, width = src.shape
    rows_per_worker = t // SC_WORKERS
    n_chunks = rows_per_worker // SC_CHUNK
    assert t == SC_WORKERS * SC_CHUNK * n_chunks
    mesh = plsc.VectorSubcoreMesh(core_axis_name="c", subcore_axis_name="s")

    @functools.partial(
        pl.kernel, mesh=mesh,
        out_type=jax.ShapeDtypeStruct((n_out, width), src.dtype),
        scratch_types=[pltpu.VMEM((SC_CHUNK,), jnp.int32), pltpu.VMEM((SC_CHUNK,), jnp.int32),
                       pltpu.VMEM((SC_CHUNK, width), src.dtype), pltpu.SemaphoreType.DMA],
    )
    def scatter_kernel(src_hbm, p1_hbm, p2_hbm, out_hbm, i1_v, i2_v, rows_v, sem):
        wid = lax.axis_index("s") * SC_CORES + lax.axis_index("c")
        base = wid * rows_per_worker

        @pl.loop(0, n_chunks)
        def _(c):
            off = pl.multiple_of(base + c * SC_CHUNK, 8)
            pltpu.sync_copy(p1_hbm.at[pl.ds(off, SC_CHUNK)], i1_v)
            pltpu.sync_copy(p2_hbm.at[pl.ds(off, SC_CHUNK)], i2_v)
            pltpu.sync_copy(src_hbm.at[pl.ds(off, SC_CHUNK)], rows_v)
            pltpu.async_copy(rows_v, out_hbm.at[i1_v], sem).wait()
            pltpu.async_copy(rows_v, out_hbm.at[i2_v], sem).wait()

    return scatter_kernel(src, pos1, pos2)


def _sc_gather_rows(table, idx):
    n, width = idx.shape[0], table.shape[1]
    rows_per_worker = n // SC_WORKERS
    n_chunks = rows_per_worker // SC_CHUNK
    assert n == SC_WORKERS * SC_CHUNK * n_chunks and n_chunks % 2 == 0
    mesh = plsc.VectorSubcoreMesh(core_axis_name="c", subcore_axis_name="s")

    @functools.partial(
        pl.kernel, mesh=mesh,
        out_type=jax.ShapeDtypeStruct((n, width), table.dtype),
        scratch_types=[pltpu.VMEM((2, SC_CHUNK), jnp.int32), pltpu.VMEM((2, SC_CHUNK, width), table.dtype),
                       pltpu.SemaphoreType.DMA, pltpu.SemaphoreType.DMA,
                       pltpu.SemaphoreType.DMA, pltpu.SemaphoreType.DMA],
    )
    def gather_kernel(table_hbm, idx_hbm, out_hbm, idx_v, rows_v, g0, g1, w0, w1):
        wid = lax.axis_index("s") * SC_CORES + lax.axis_index("c")
        base = wid * rows_per_worker
        gsem = (g0, g1)
        wsem = (w0, w1)

        def gather_copy(slot):
            return pltpu.make_async_copy(table_hbm.at[idx_v.at[slot]], rows_v.at[slot], gsem[slot])

        def write_copy(c, slot):
            off = pl.multiple_of(base + c * SC_CHUNK, 8)
            return pltpu.make_async_copy(rows_v.at[slot], out_hbm.at[pl.ds(off, SC_CHUNK)], wsem[slot])

        def start_gather(c, slot):
            off = pl.multiple_of(base + c * SC_CHUNK, 8)
            pltpu.sync_copy(idx_hbm.at[pl.ds(off, SC_CHUNK)], idx_v.at[slot])
            gather_copy(slot).start()

        start_gather(0, 0)

        @pl.loop(0, n_chunks, step=2)
        def _(c):
            @pl.when(c > 0)
            def _():
                write_copy(c - 1, 1).wait()

            start_gather(c + 1, 1)
            gather_copy(0).wait()
            write_copy(c, 0).start()
            gather_copy(1).wait()
            write_copy(c + 1, 1).start()
            write_copy(c, 0).wait()

            @pl.when(c + 2 < n_chunks)
            def _():
                start_gather(c + 2, 0)

        write_copy(n_chunks - 1, 1).wait()

    return gather_kernel(table, idx)


def _moe_grouped_kernel(te_ref, nt_ref, x_ref, w1_ref, w2_ref, o_ref):
    @pl.when(pl.program_id(0) < nt_ref[0])
    def _():
        t_a, t_b = _unpack_bf16_pair(x_ref[...])
        hid = _expert_hidden(t_a.astype(BF16), t_b.astype(BF16), w1_ref[0])
        o_ref[...] = _pack_bf16_pair(_dot(hid.astype(BF16), w2_ref[0]))


def _moe_grouped(xs, tile_expert, n_tiles, w):
    rows = xs.shape[0]
    row = lambda i, te, nt: (i, 0)
    by_expert = lambda i, te, nt: (te[i], 0, 0)
    return pl.pallas_call(
        _moe_grouped_kernel,
        out_shape=jax.ShapeDtypeStruct((rows, D_MODEL // 2), jnp.uint32),
        grid_spec=pltpu.PrefetchScalarGridSpec(
            num_scalar_prefetch=2,
            grid=(rows // MOE_TILE,),
            in_specs=[pl.BlockSpec((MOE_TILE, D_MODEL // 2), row),
                      pl.BlockSpec((1, D_MODEL, 2 * EXPERT_FF), by_expert),
                      pl.BlockSpec((1, EXPERT_FF, D_MODEL), by_expert)],
            out_specs=pl.BlockSpec((MOE_TILE, D_MODEL // 2), row),
        ),
        compiler_params=pltpu.CompilerParams(dimension_semantics=("arbitrary",),
                                             vmem_limit_bytes=VMEM_LIMIT),
        name="moe_grouped",
    )(tile_expert, n_tiles, xs, w["w1e"], w["w2e"])


def _moe_combine_kernel(h_ref, z1_ref, z2_ref, route_ref, y_ref):
    route = route_ref[...]
    g1, g2 = route[:, 2:3], route[:, 3:4]
    half = D_MODEL // 2
    a1, b1 = _unpack_bf16_pair(z1_ref[...])
    a2, b2 = _unpack_bf16_pair(z2_ref[...])
    y_ref[:, :half] = h_ref[:, :half] + g1 * a1 + g2 * a2
    y_ref[:, half:] = h_ref[:, half:] + g1 * b1 + g2 * b2


def _moe_combine(h, z, route, tm):
    t = h.shape[0]
    nb = t // tm
    tok = lambda i: (i, 0)
    return pl.pallas_call(
        _moe_combine_kernel,
        out_shape=jax.ShapeDtypeStruct((t, D_MODEL), F32),
        grid=(nb,),
        in_specs=[pl.BlockSpec((tm, D_MODEL), tok), pl.BlockSpec((tm, D_MODEL // 2), tok),
                  pl.BlockSpec((tm, D_MODEL // 2), lambda i: (i + nb, 0)), pl.BlockSpec((tm, LANES), tok)],
        out_specs=pl.BlockSpec((tm, D_MODEL), tok),
        compiler_params=pltpu.CompilerParams(dimension_semantics=("parallel",),
                                             vmem_limit_bytes=VMEM_LIMIT),
        name="moe_combine",
    )(h, z, z, route)


def _route_pos_kernel(route_ref, cnt_ref, upper_ref, pos_ref):
    tm = route_ref.shape[0]
    cnt = cnt_ref[...]
    padded = jnp.floor((cnt + float(MOE_TILE - 1)) * (1.0 / MOE_TILE)) * float(MOE_TILE)
    p_hi, p_mid, p_lo = _split3(padded)
    upper = upper_ref[...]
    starts = (_dot(p_hi, upper) + _dot(p_mid, upper) + _dot(p_lo, upper))[0:1, :]
    route = route_ref[...]
    lane_f = lax.broadcasted_iota(jnp.int32, (tm, LANES), 1).astype(F32)
    diag = (lax.broadcasted_iota(jnp.int32, (LANES, LANES), 0)
            == lax.broadcasted_iota(jnp.int32, (LANES, LANES), 1))
    for k in range(2):
        e_lane = route[:, k:k + 1] + float(EXPERT_LANE0)
        pos = jnp.sum(jnp.where(lane_f == e_lane, starts, 0.0), axis=-1, keepdims=True) + route[:, 4 + k:5 + k]
        for r in range(tm // LANES):
            col = pos[r * LANES:(r + 1) * LANES, :]
            row = jnp.sum(jnp.where(diag, col, 0.0), axis=0, keepdims=True)
            pos_ref[k, r:r + 1, :] = row.astype(jnp.int32)


def _route_positions(route, counts, tm):
    t = route.shape[0]
    idx = jnp.arange(LANES)
    upper = (idx[:, None] < idx[None, :]).astype(BF16)
    return pl.pallas_call(
        _route_pos_kernel,
        out_shape=jax.ShapeDtypeStruct((2, t // LANES, LANES), jnp.int32),
        grid=(t // tm,),
        in_specs=[pl.BlockSpec((tm, LANES), lambda i: (i, 0)), pl.BlockSpec((8, LANES), lambda i: (0, 0)),
                  pl.BlockSpec((LANES, LANES), lambda i: (0, 0))],
        out_specs=pl.BlockSpec((2, tm // LANES, LANES), lambda i: (0, i, 0)),
        compiler_params=pltpu.CompilerParams(dimension_semantics=("parallel",),
                                             vmem_limit_bytes=VMEM_LIMIT),
        name="route_positions",
    )(route, counts, upper)


def _moe_routed(h, t_packed, route, counts, w, tm):
    t = h.shape[0]
    pos = _route_positions(route, counts, min(8 * LANES, t))
    pos1 = pos[0].reshape(t)
    pos2 = pos[1].reshape(t)
    cnt = counts[0, EXPERT_LANE0:EXPERT_LANE0 + N_EXPERTS].astype(jnp.int32)
    padded = (cnt + MOE_TILE - 1) // MOE_TILE * MOE_TILE
    ends = jnp.cumsum(padded)
    n_rows = 2 * t + N_EXPERTS * MOE_TILE
    n_tiles = ends[N_EXPERTS - 1] // MOE_TILE
    tile_start = jnp.arange(n_rows // MOE_TILE, dtype=jnp.int32) * MOE_TILE
    tile_start = jnp.minimum(tile_start, ends[N_EXPERTS - 1] - MOE_TILE)
    tile_expert = jnp.sum((tile_start[:, None] >= ends[None, :]).astype(jnp.int32), axis=1)
    xs = _sc_scatter_rows(t_packed, pos1, pos2, n_rows)
    out = _moe_grouped(xs, tile_expert, n_tiles.reshape(1), w)
    z = _sc_gather_rows(out, pos.reshape(2 * t))
    return _moe_combine(h, z, route, tm)


def _pad_lanes(a, width=LANES):
    return jnp.pad(a, ((0, 0), (0, width - a.shape[1])))


def _prep_weights(norm1, w_in, q_norm, k_norm, conv_w, conv_b, dt_bias, a_log, d_skip, ssm_norm, w_out,
                  norm2, w_grp, b_grp, w_exp, b_exp, w_gate, w_up, w_down):
    w = {}
    w["norm1"] = norm1.reshape(1, D_MODEL)
    w["wqk"] = w_in[:, :K_END].astype(BF16)
    w["wv"] = w_in[:, K_END:V_END].astype(BF16)
    w["wz"] = w_in[:, V_END:Z_END].astype(BF16)
    w["wxbc"] = w_in[:, Z_END:XBC_END].astype(BF16)
    w["wdt"] = _pad_lanes(w_in[:, XBC_END:]).astype(BF16)
    w["qkn"] = jnp.concatenate([jnp.tile(q_norm, N_HEADS), jnp.tile(k_norm, KV_HEADS)]).reshape(1, QK_WIDTH)
    head_of_col = jnp.arange(QK_WIDTH) // HEAD_DIM
    red = (head_of_col[:, None] == jnp.arange(LANES)[None, :])
    w["red"] = red.astype(BF16)
    w["exp"] = red.T.astype(BF16)
    w["conv_w"] = conv_w
    w["conv_b"] = conv_b.reshape(1, CONV_DIM)
    w["dt_bias"] = _pad_lanes(dt_bias.reshape(1, SSM_HEADS))
    w["a_log"] = _pad_lanes(a_log.reshape(1, SSM_HEADS))
    w["d_skip"] = jnp.repeat(d_skip, SSM_HEAD_DIM).reshape(1, SSM_WIDTH)
    w["ssm_norm"] = ssm_norm.reshape(1, SSM_WIDTH)
    idx = jnp.arange(SSD_CHUNK)
    w["tri"] = (idx[None, :] <= idx[:, None]).astype(BF16)
    w["wo_a"] = w_out[:ATTN_WIDTH].astype(BF16)
    w["wo_s"] = w_out[ATTN_WIDTH:].astype(BF16)
    w["norm2"] = norm2.reshape(1, D_MODEL)
    wr = jnp.zeros((D_MODEL, LANES), F32)
    wr = wr.at[:, :N_EGROUPS].set(w_grp).at[:, EXPERT_LANE0:EXPERT_LANE0 + N_EXPERTS].set(w_exp)
    w["wr_hi"] = wr.astype(BF16)
    w["wr_lo"] = (wr - w["wr_hi"].astype(F32)).astype(BF16)
    br = jnp.zeros((1, LANES), F32)
    w["br"] = br.at[0, :N_EGROUPS].set(b_grp).at[0, EXPERT_LANE0:EXPERT_LANE0 + N_EXPERTS].set(b_exp)
    w["w1e"] = jnp.concatenate([w_gate, w_up], axis=-1).astype(BF16)
    w["w2e"] = w_down.astype(BF16)
    return w


def _rope_tables(pos):
    inv = 1.0 / (ROPE_THETA ** (jnp.arange(0, HEAD_DIM, 2, dtype=F32) / HEAD_DIM))
    ang = pos.astype(F32)[:, None] * inv[None, :]
    cos, sin = jnp.cos(ang), jnp.sin(ang)
    reps = LANES // HEAD_DIM
    return (jnp.tile(jnp.concatenate([cos, cos], axis=-1), (1, reps)),
            jnp.tile(jnp.concatenate([-sin, sin], axis=-1), (1, reps)))


def _token_tile(t):
    for tm in (512, 256, 128, 64, 32, 16):
        if t % tm == 0:
            return tm
    raise ValueError(f"token count {t} is not a multiple of 16")


def kernel(x_prompt, x_sample, cache_win_k, cache_win_v, state_conv, state_ssm, norm1, w_in, q_norm, k_norm,
           sinks, conv_w, conv_b, dt_bias, a_log, d_skip, ssm_norm, w_out, norm2, w_grp, b_grp, w_exp, b_exp,
           w_gate, w_up, w_down):
    depth = norm1.shape[0]
    assert depth == 1, "single-layer stack"
    bp, lp, _ = x_prompt.shape
    bsn, ls, _ = x_sample.shape
    assert ls == 1 and lp % WINDOW == 0 and cache_win_k.shape[2] == WINDOW
    l = 0
    w = _prep_weights(norm1[l], w_in[l], q_norm[l], k_norm[l], conv_w[l], conv_b[l], dt_bias[l], a_log[l],
                      d_skip[l], ssm_norm[l], w_out[l], norm2[l], w_grp[l], b_grp[l], w_exp[l], b_exp[l],
                      w_gate[l], w_up[l], w_down[l])
    sink = sinks[l]

    tp = bp * lp
    xp = x_prompt.reshape(tp, D_MODEL)
    tm_p = _token_tile(lp)
    cos_p, sin_p = _rope_tables(jnp.arange(lp, dtype=jnp.int32))
    q, k, v, z, xbc, dt = _inproj(xp, w, cos_p, sin_p, tm_p, lp // tm_p)
    attn = _attn_prompt(q, k, v, sink, bp, lp)
    ssm, st_p = _ssd_prompt(xbc, z, dt, w, bp, lp)
    h, t, route, counts = _outproj_router(xp, attn, ssm, w, tm_p)
    y_prompt = _moe_routed(h, t, route, counts, w, tm_p).reshape(bp, lp, D_MODEL)
    k3 = k.reshape(bp, lp, KV_HEADS, HEAD_DIM)
    v3 = v.reshape(bp, lp, KV_HEADS, HEAD_DIM)
    win_k_p = k3[:, lp - WINDOW:][None]
    win_v_p = v3[:, lp - WINDOW:][None]
    conv_p = xbc.reshape(bp, lp, CONV_DIM)[:, lp - (CONV_K - 1):][None]
    ssm_p = st_p.reshape(1, bp, SSM_HEADS, SSM_HEAD_DIM, SSM_STATE)

    xs2 = x_sample.reshape(bsn, D_MODEL)
    tm_s = _token_tile(bsn)
    cos_s, sin_s = _rope_tables(jnp.full((tm_s,), PAST_LEN, jnp.int32))
    q_s, k_s, v_s, z_s, xbc_s, dt_s = _inproj(xs2, w, cos_s, sin_s, tm_s, 1)
    q4 = q_s.reshape(bsn, KV_HEADS, N_HEADS // KV_HEADS, HEAD_DIM)
    zq = jnp.zeros_like(q4[:, 0])
    qx = jnp.concatenate([jnp.concatenate([q4[:, 0], zq], axis=-1),
                          jnp.concatenate([zq, q4[:, 1]], axis=-1)], axis=1)
    qx = jnp.pad(qx, ((0, 0), (0, BF16_ROWS - N_HEADS), (0, 0)))
    sink_x = jnp.pad(jnp.broadcast_to(sink[:, None], (N_HEADS, LANES)), ((0, BF16_ROWS - N_HEADS), (0, 0)))
    kc = cache_win_k[l].reshape(bsn, WINDOW, KV_WIDTH)
    vc = cache_win_v[l].reshape(bsn, WINDOW, KV_WIDTH)
    ko, vo, attn_s = _attn_sample(qx, kc, k_s.reshape(bsn, 1, KV_WIDTH), vc, v_s.reshape(bsn, 1, KV_WIDTH),
                               sink_x, 8)
    cprev_t = jnp.transpose(state_conv[l], (1, 0, 2))
    h0 = state_ssm[l].reshape(bsn, HEAD_PAIRS, LANES, SSM_STATE)
    ssm_s, cnew_t, h1 = _ssd_sample(xbc_s, z_s, dt_s, cprev_t, h0, w, 16)
    h_s, t_s, route_s, _ = _outproj_router(xs2, attn_s, ssm_s, w, tm_s)
    y_sample = _moe_dense(h_s, t_s, route_s, w, tm_s).reshape(bsn, 1, D_MODEL)
    win_k_s = ko.reshape(1, bsn, WINDOW, KV_HEADS, HEAD_DIM)
    win_v_s = vo.reshape(1, bsn, WINDOW, KV_HEADS, HEAD_DIM)
    conv_s = jnp.transpose(cnew_t, (1, 0, 2))[None]
    ssm_s_state = h1.reshape(1, bsn, SSM_HEADS, SSM_HEAD_DIM, SSM_STATE)

    return (y_prompt, y_sample, win_k_p, win_v_p, conv_p, ssm_p, win_k_s, win_v_s, conv_s, ssm_s_state)
```

```python
import functools
import math

import jax
import jax.numpy as jnp
from jax import lax
from jax.experimental import pallas as pl
from jax.experimental.pallas import tpu as pltpu
from jax.experimental.pallas import tpu_sc as plsc

F32 = jnp.float32
BF16 = jnp.bfloat16

D_MODEL = 1024
HEAD_DIM = 64
N_HEADS = 8
KV_HEADS = 2
WINDOW = 128
ATTN_WIDTH = N_HEADS * HEAD_DIM
QK_WIDTH = ATTN_WIDTH + KV_HEADS * HEAD_DIM
KV_WIDTH = KV_HEADS * HEAD_DIM
ATTN_SCALE = HEAD_DIM ** -0.5
ROPE_THETA = 10000.0
SSM_WIDTH = 512
SSM_HEADS = 8
SSM_HEAD_DIM = 64
SSM_GROUPS = 2
SSM_STATE = 128
CONV_K = 4
CONV_DIM = SSM_WIDTH + 2 * SSM_GROUPS * SSM_STATE
SSD_CHUNK = 128
N_EGROUPS = 4
EXP_PER_GROUP = 8
N_EXPERTS = 32
EXPERT_FF = 128
EPS = 1e-6
PAST_LEN = 16384

LANES = 128
BF16_ROWS = 16
HEAD_PAIRS = SSM_HEADS // 2
EXPERT_LANE0 = 32
VMEM_LIMIT = 56 * 1024 * 1024
MOE_TILE = 512
ATTN_QBLOCKS = 2
SC_CORES = 2
SC_SUBCORES = 16
SC_WORKERS = SC_CORES * SC_SUBCORES
SC_CHUNK = 64

Q_END = ATTN_WIDTH
K_END = Q_END + KV_WIDTH
V_END = K_END + KV_WIDTH
Z_END = V_END + SSM_WIDTH
XBC_END = Z_END + CONV_DIM


def _dot(a, b):
    return jnp.dot(a, b, preferred_element_type=F32)


def _dot_nt(a, b):
    return lax.dot_general(a, b, (((1,), (1,)), ((), ())), preferred_element_type=F32)


def _split2(v):
    hi = v.astype(BF16)
    lo = (v - hi.astype(F32)).astype(BF16)
    return hi, lo


def _split3(v):
    hi = v.astype(BF16)
    r = v - hi.astype(F32)
    mid = r.astype(BF16)
    lo = (r - mid.astype(F32)).astype(BF16)
    return hi, mid, lo


def _silu(x):
    return x * jax.nn.sigmoid(x)


def _softplus(x):
    return jnp.maximum(x, 0.0) + jnp.log1p(jnp.exp(-jnp.abs(x)))


def _lane_bcast_pairs(v, n_pairs):
    r = v.shape[0]
    lo = lax.broadcasted_iota(jnp.int32, (r, LANES), 1) < HEAD_DIM
    slabs = []
    for j in range(n_pairs):
        a = jnp.broadcast_to(v[:, 2 * j:2 * j + 1], (r, LANES))
        b = jnp.broadcast_to(v[:, 2 * j + 1:2 * j + 2], (r, LANES))
        slabs.append(jnp.where(lo, a, b))
    return jnp.concatenate(slabs, axis=1)


def _inproj_kernel(x_ref, n1_ref, wqk_ref, wv_ref, wz_ref, wxbc_ref, wdt_ref, qkn_ref,
                   cos_ref, sin_ref, red_ref, exp_ref,
                   q_ref, k_ref, v_ref, z_ref, xbc_ref, dt_ref):
    x = x_ref[...]
    ms = jnp.mean(x * x, axis=-1, keepdims=True)
    xn = (x * lax.rsqrt(ms + EPS) * n1_ref[...]).astype(BF16)
    v_ref[...] = _dot(xn, wv_ref[...])
    z_ref[...] = _dot(xn, wz_ref[...])
    xbc_ref[...] = _dot(xn, wxbc_ref[...])
    dt_ref[...] = _dot(xn, wdt_ref[...])
    qk = _dot(xn, wqk_ref[...])
    sq_hi, sq_lo = _split2(qk * qk)
    ss = _dot(sq_hi, red_ref[...]) + _dot(sq_lo, red_ref[...])
    inv = lax.rsqrt(ss * (1.0 / HEAD_DIM) + EPS)
    inv_hi, inv_lo = _split2(inv)
    inv_x = _dot(inv_hi, exp_ref[...]) + _dot(inv_lo, exp_ref[...])
    qkn = qk * inv_x * qkn_ref[...]
    cos = cos_ref[...]
    sin = sin_ref[...]
    lane = lax.broadcasted_iota(jnp.int32, (x.shape[0], LANES), 1)
    first_half = (lane % HEAD_DIM) < (HEAD_DIM // 2)
    for c in range(QK_WIDTH // LANES):
        xc = qkn[:, c * LANES:(c + 1) * LANES]
        partner = jnp.where(first_half,
                            pltpu.roll(xc, LANES - HEAD_DIM // 2, axis=1),
                            pltpu.roll(xc, HEAD_DIM // 2, axis=1))
        rot = xc * cos + partner * sin
        if c < ATTN_WIDTH // LANES:
            q_ref[:, c * LANES:(c + 1) * LANES] = (rot * ATTN_SCALE).astype(BF16)
        else:
            k_ref[...] = rot


def _inproj(x2d, w, cos_tab, sin_tab, tm, n_pos_blocks):
    t = x2d.shape[0]
    grid = (t // tm,)
    tok = lambda i: (i, 0)
    const = lambda i: (0, 0)
    pos = lambda i: (i % n_pos_blocks, 0)
    full = lambda a: pl.BlockSpec(a.shape, const)
    out_shapes = (
        jax.ShapeDtypeStruct((t, ATTN_WIDTH), BF16),
        jax.ShapeDtypeStruct((t, KV_WIDTH), F32),
        jax.ShapeDtypeStruct((t, KV_WIDTH), F32),
        jax.ShapeDtypeStruct((t, SSM_WIDTH), F32),
        jax.ShapeDtypeStruct((t, CONV_DIM), F32),
        jax.ShapeDtypeStruct((t, LANES), F32),
    )
    return pl.pallas_call(
        _inproj_kernel,
        out_shape=out_shapes,
        grid=grid,
        in_specs=[
            pl.BlockSpec((tm, D_MODEL), tok),
            full(w["norm1"]), full(w["wqk"]), full(w["wv"]), full(w["wz"]), full(w["wxbc"]),
            full(w["wdt"]), full(w["qkn"]),
            pl.BlockSpec((tm, LANES), pos), pl.BlockSpec((tm, LANES), pos),
            full(w["red"]), full(w["exp"]),
        ],
        out_specs=(
            pl.BlockSpec((tm, ATTN_WIDTH), tok), pl.BlockSpec((tm, KV_WIDTH), tok),
            pl.BlockSpec((tm, KV_WIDTH), tok), pl.BlockSpec((tm, SSM_WIDTH), tok),
            pl.BlockSpec((tm, CONV_DIM), tok), pl.BlockSpec((tm, LANES), tok),
        ),
        compiler_params=pltpu.CompilerParams(dimension_semantics=("parallel",),
                                             vmem_limit_bytes=VMEM_LIMIT),
        name="inproj",
    )(x2d, w["norm1"], w["wqk"], w["wv"], w["wz"], w["wxbc"], w["wdt"], w["qkn"],
      cos_tab, sin_tab, w["red"], w["exp"])


def _pair_rhs(blk, g):
    lo = lax.broadcasted_iota(jnp.int32, blk.shape, 1) < HEAD_DIM
    swapped = pltpu.roll(blk, HEAD_DIM, axis=1)
    if g == 0:
        top = jnp.where(lo, blk, 0.0)
        bot = jnp.where(lo, 0.0, swapped)
    else:
        top = jnp.where(lo, swapped, 0.0)
        bot = jnp.where(lo, 0.0, blk)
    return jnp.concatenate([top, bot], axis=0).astype(BF16)


def _attn_kernel(sink_ref, q_ref, kc_ref, kp_ref, vc_ref, vp_ref, o_ref):
    blk = WINDOW
    n_sub = q_ref.shape[0] // blk
    first_step = pl.program_id(1) == 0
    qi = lax.broadcasted_iota(jnp.int32, (blk, 2 * blk), 0)
    kj = lax.broadcasted_iota(jnp.int32, (blk, 2 * blk), 1) % blk
    cur_ok = kj <= qi
    seq_start_ok = jnp.logical_or(cur_ok, jnp.logical_not(first_step))
    lo = lax.broadcasted_iota(jnp.int32, (blk, LANES), 1) < HEAD_DIM
    k_blocks = [kp_ref[...]] + [kc_ref[u * blk:(u + 1) * blk, :] for u in range(n_sub)]
    v_blocks = [vp_ref[...]] + [vc_ref[u * blk:(u + 1) * blk, :] for u in range(n_sub)]
    for g in range(KV_HEADS):
        k2 = [_pair_rhs(b, g) for b in k_blocks]
        v2 = [_pair_rhs(b, g) for b in v_blocks]
        n_pairs = N_HEADS // KV_HEADS // 2
        for u in range(n_sub):
            rows = slice(u * blk, (u + 1) * blk)
            q_all = jnp.concatenate([q_ref[rows, (g * n_pairs + r) * LANES:(g * n_pairs + r + 1) * LANES]
                                     for r in range(n_pairs)], axis=0)
            s_all = _dot_nt(q_all, jnp.concatenate([k2[u + 1], k2[u]], axis=0))
            p_rows = []
            den_rows = []
            for r in range(n_pairs):
                pair = g * n_pairs + r
                s_cur = s_all[r * blk:(r + 1) * blk, :2 * blk]
                s_prev = s_all[r * blk:(r + 1) * blk, 2 * blk:]
                s = jnp.where(cur_ok, s_cur, s_prev)
                if u == 0:
                    s = jnp.where(seq_start_ok, s, -jnp.inf)
                ps = []
                dens = []
                for hh in range(2):
                    sink = sink_ref[2 * pair + hh]
                    sh = s[:, hh * blk:(hh + 1) * blk]
                    m = jnp.maximum(jnp.max(sh, axis=-1, keepdims=True), sink)
                    p = jnp.exp(sh - m)
                    dens.append(jnp.sum(p, axis=-1, keepdims=True) + jnp.exp(sink - m))
                    ps.append(p)
                p2 = jnp.concatenate(ps, axis=1)
                p_rows.append(jnp.concatenate([jnp.where(cur_ok, p2, 0.0), jnp.where(cur_ok, 0.0, p2)],
                                              axis=1).astype(BF16))
                den_rows.append(jnp.where(lo, dens[0], dens[1]))
            o_all = _dot(jnp.concatenate(p_rows, axis=0), jnp.concatenate([v2[u + 1], v2[u]], axis=0))
            for r in range(n_pairs):
                pair = g * n_pairs + r
                o2 = o_all[r * blk:(r + 1) * blk, :]
                o_ref[rows, pair * LANES:(pair + 1) * LANES] = (o2 / den_rows[r]).astype(BF16)


def _attn_prompt(q, k, v, sinks, batch, seq):
    n_sub = ATTN_QBLOCKS if seq % (ATTN_QBLOCKS * WINDOW) == 0 else 1
    rows = n_sub * WINDOW
    nb = seq // rows
    cur = lambda b, j, s: (b * nb + j, 0)
    prev = lambda b, j, s: (jnp.maximum((b * nb + j) * n_sub - 1, 0), 0)
    return pl.pallas_call(
        _attn_kernel,
        out_shape=jax.ShapeDtypeStruct((batch * seq, ATTN_WIDTH), BF16),
        grid_spec=pltpu.PrefetchScalarGridSpec(
            num_scalar_prefetch=1,
            grid=(batch, nb),
            in_specs=[
                pl.BlockSpec((rows, ATTN_WIDTH), cur),
                pl.BlockSpec((rows, KV_WIDTH), cur), pl.BlockSpec((WINDOW, KV_WIDTH), prev),
                pl.BlockSpec((rows, KV_WIDTH), cur), pl.BlockSpec((WINDOW, KV_WIDTH), prev),
            ],
            out_specs=pl.BlockSpec((rows, ATTN_WIDTH), cur),
        ),
        compiler_params=pltpu.CompilerParams(dimension_semantics=("parallel", "parallel"),
                                             vmem_limit_bytes=VMEM_LIMIT),
        name="attn_prompt",
    )(sinks, q, k, k, v, v)


def _ssd_kernel(xbc_ref, z_ref, dt_ref, convw_ref, convb_ref, dtb_ref, alog_ref, dskip_ref, nw_ref,
                tri_ref, y_ref, st_ref, buf_ref, state_ref):
    c = pl.program_id(1)
    cl = xbc_ref.shape[0]
    halo = buf_ref.shape[0] - cl

    @pl.when(c == 0)
    def _():
        buf_ref[0:halo, :] = jnp.zeros((halo, CONV_DIM), F32)
        state_ref[...] = jnp.zeros(state_ref.shape, F32)

    x_raw = xbc_ref[...]
    buf_ref[halo:halo + cl, :] = x_raw
    conv = convb_ref[...]
    for j in range(CONV_K):
        off = halo - (CONV_K - 1) + j
        conv = conv + buf_ref[off:off + cl, :] * convw_ref[j:j + 1, :]
    buf_ref[0:halo, :] = x_raw[cl - halo:cl, :]
    act = _silu(conv)
    xs = act[:, :SSM_WIDTH]
    bm = act[:, SSM_WIDTH:SSM_WIDTH + SSM_GROUPS * SSM_STATE].astype(BF16)
    cm = act[:, SSM_WIDTH + SSM_GROUPS * SSM_STATE:].astype(BF16)

    lane = lax.broadcasted_iota(jnp.int32, (1, LANES), 1)
    a_neg = jnp.where(lane < SSM_HEADS, -jnp.exp(alog_ref[...]), 0.0)
    dt = _softplus(dt_ref[...] + dtb_ref[...])
    dta = dt * a_neg
    tri = tri_ref[...]
    p_hi, p_mid, p_lo = _split3(dta)
    a_col = _dot(tri, p_hi) + _dot(tri, p_mid) + _dot(tri, p_lo)
    a_last = a_col[cl - 1:cl, :]
    a_row = a_col.T
    dt_x = _lane_bcast_pairs(dt, HEAD_PAIRS)
    ecol_x = _lane_bcast_pairs(jnp.exp(a_col), HEAD_PAIRS)
    dte_x = _lane_bcast_pairs(jnp.exp(a_last - a_col), HEAD_PAIRS)
    e_last = jnp.exp(a_last)
    xdt = xs * dt_x

    li = lax.broadcasted_iota(jnp.int32, (cl, cl), 0)
    si = lax.broadcasted_iota(jnp.int32, (cl, cl), 1)
    causal = si <= li
    lo = lax.broadcasted_iota(jnp.int32, (cl, LANES), 1) < SSM_HEAD_DIM
    row_lo = lax.broadcasted_iota(jnp.int32, (LANES, SSM_STATE), 0) < SSM_HEAD_DIM

    ys = []
    for g in range(SSM_GROUPS):
        b_g = bm[:, g * SSM_STATE:(g + 1) * SSM_STATE]
        c_g = cm[:, g * SSM_STATE:(g + 1) * SSM_STATE]
        cb = _dot_nt(c_g, b_g)
        for r in range(HEAD_PAIRS // SSM_GROUPS):
            j = g * (HEAD_PAIRS // SSM_GROUPS) + r
            sl = slice(j * LANES, (j + 1) * LANES)
            xdt_p = xdt[:, sl]
            ms = []
            for hh in range(2):
                h = 2 * j + hh
                seg = a_col[:, h:h + 1] - a_row[h:h + 1, :]
                ms.append(cb * jnp.exp(jnp.where(causal, seg, -jnp.inf)))
            m2 = jnp.concatenate(ms, axis=1).astype(BF16)
            rhs = jnp.concatenate([jnp.where(lo, xdt_p, 0.0), jnp.where(lo, 0.0, xdt_p)],
                                  axis=0).astype(BF16)
            y_diag = _dot(m2, rhs)
            st = state_ref[j]
            y_off = _dot_nt(c_g, st.astype(BF16)) * ecol_x[:, sl]
            xdt_e = (xdt_p * dte_x[:, sl]).T.astype(BF16)
            d_a = e_last[:, 2 * j:2 * j + 1]
            d_b = e_last[:, 2 * j + 1:2 * j + 2]
            decay = jnp.where(row_lo, jnp.broadcast_to(d_a, row_lo.shape), jnp.broadcast_to(d_b, row_lo.shape))
            state_ref[j] = decay * st + _dot(xdt_e, b_g)
            ys.append(y_diag + y_off + dskip_ref[:, sl] * xs[:, sl])
    y = jnp.concatenate(ys, axis=1)
    gated = y * _silu(z_ref[...])
    gw = SSM_WIDTH // SSM_GROUPS
    outs = []
    for g in range(SSM_GROUPS):
        gg = gated[:, g * gw:(g + 1) * gw]
        outs.append(gg * lax.rsqrt(jnp.mean(gg * gg, axis=-1, keepdims=True) + EPS))
    y_ref[...] = (jnp.concatenate(outs, axis=1) * nw_ref[...]).astype(BF16)

    @pl.when(c == pl.num_programs(1) - 1)
    def _():
        st_ref[0] = state_ref[...]


def _ssd_prompt(xbc, z, dt, w, batch, seq):
    nc = seq // SSD_CHUNK
    tok = lambda b, c: (b * nc + c, 0)
    const = lambda b, c: (0, 0)
    full = lambda a: pl.BlockSpec(a.shape, const)
    return pl.pallas_call(
        _ssd_kernel,
        out_shape=(jax.ShapeDtypeStruct((batch * seq, SSM_WIDTH), BF16),
                   jax.ShapeDtypeStruct((batch, HEAD_PAIRS, LANES, SSM_STATE), F32)),
        grid=(batch, nc),
        in_specs=[
            pl.BlockSpec((SSD_CHUNK, CONV_DIM), tok), pl.BlockSpec((SSD_CHUNK, SSM_WIDTH), tok),
            pl.BlockSpec((SSD_CHUNK, LANES), tok),
            full(w["conv_w"]), full(w["conv_b"]), full(w["dt_bias"]), full(w["a_log"]),
            full(w["d_skip"]), full(w["ssm_norm"]), full(w["tri"]),
        ],
        out_specs=(pl.BlockSpec((SSD_CHUNK, SSM_WIDTH), tok),
                   pl.BlockSpec((1, HEAD_PAIRS, LANES, SSM_STATE), lambda b, c: (b, 0, 0, 0))),
        scratch_shapes=[pltpu.VMEM((SSD_CHUNK + 8, CONV_DIM), F32),
                        pltpu.VMEM((HEAD_PAIRS, LANES, SSM_STATE), F32)],
        compiler_params=pltpu.CompilerParams(dimension_semantics=("parallel", "arbitrary"),
                                             vmem_limit_bytes=VMEM_LIMIT),
        name="ssd_prompt",
    )(xbc, z, dt, w["conv_w"], w["conv_b"], w["dt_bias"], w["a_log"], w["d_skip"], w["ssm_norm"], w["tri"])


def _attn_sample_kernel(qx_ref, kc_ref, kn_ref, vc_ref, vn_ref, sink_ref, ko_ref, vo_ref, o_ref):
    bs = qx_ref.shape[0]
    w = kc_ref.shape[1]
    sink = sink_ref[...]
    lo = lax.broadcasted_iota(jnp.int32, (1, LANES), 1) < HEAD_DIM
    for i in range(bs):
        ko_ref[i, 0:w - 1, :] = kc_ref[i, 1:w, :]
        ko_ref[i, w - 1:w, :] = kn_ref[i]
        vo_ref[i, 0:w - 1, :] = vc_ref[i, 1:w, :]
        vo_ref[i, w - 1:w, :] = vn_ref[i]
        s = _dot_nt(qx_ref[i], ko_ref[i].astype(BF16))
        m = jnp.maximum(jnp.max(s, axis=-1, keepdims=True), sink)
        p = jnp.exp(s - m)
        den = jnp.sum(p, axis=-1, keepdims=True) + jnp.exp(sink - m)
        o = _dot(p.astype(BF16), vo_ref[i].astype(BF16)) / den
        o_sw = pltpu.roll(o, HEAD_DIM, axis=1)
        for j in range(N_HEADS // 2):
            a, b = (o, o_sw) if j < N_HEADS // 4 else (o_sw, o)
            o_ref[i:i + 1, j * LANES:(j + 1) * LANES] = jnp.where(lo, a[2 * j:2 * j + 1], b[2 * j + 1:2 * j + 2])


def _attn_sample(qx, kc, kn, vc, vn, sink_x, bs):
    n, w = kc.shape[0], kc.shape[1]
    blk3 = lambda i: (i, 0, 0)
    return pl.pallas_call(
        _attn_sample_kernel,
        out_shape=(jax.ShapeDtypeStruct((n, w, KV_WIDTH), F32),
                   jax.ShapeDtypeStruct((n, w, KV_WIDTH), F32),
                   jax.ShapeDtypeStruct((n, ATTN_WIDTH), F32)),
        grid=(n // bs,),
        in_specs=[
            pl.BlockSpec((bs, BF16_ROWS, LANES), blk3),
            pl.BlockSpec((bs, w, KV_WIDTH), blk3), pl.BlockSpec((bs, 1, KV_WIDTH), blk3),
            pl.BlockSpec((bs, w, KV_WIDTH), blk3), pl.BlockSpec((bs, 1, KV_WIDTH), blk3),
            pl.BlockSpec(sink_x.shape, lambda i: (0, 0)),
        ],
        out_specs=(pl.BlockSpec((bs, w, KV_WIDTH), blk3), pl.BlockSpec((bs, w, KV_WIDTH), blk3),
                   pl.BlockSpec((bs, ATTN_WIDTH), lambda i: (i, 0))),
        compiler_params=pltpu.CompilerParams(dimension_semantics=("parallel",),
                                             vmem_limit_bytes=VMEM_LIMIT),
        name="attn_sample",
    )(qx, kc, kn, vc, vn, sink_x)


def _ssd_sample_kernel(xbc_ref, z_ref, dt_ref, cprev_ref, h0_ref, convw_ref, convb_ref, dtb_ref, alog_ref,
                       dskip_ref, nw_ref, y_ref, cnew_ref, h1_ref):
    bs = xbc_ref.shape[0]
    x_raw = xbc_ref[...]
    conv = convb_ref[...] + x_raw * convw_ref[CONV_K - 1:CONV_K, :]
    for j in range(CONV_K - 1):
        conv = conv + cprev_ref[j] * convw_ref[j:j + 1, :]
    for j in range(CONV_K - 2):
        cnew_ref[j] = cprev_ref[j + 1]
    cnew_ref[CONV_K - 2] = x_raw
    act = _silu(conv)
    xs = act[:, :SSM_WIDTH]
    bm = act[:, SSM_WIDTH:SSM_WIDTH + SSM_GROUPS * SSM_STATE].astype(BF16)
    cm = act[:, SSM_WIDTH + SSM_GROUPS * SSM_STATE:].astype(BF16)
    lane = lax.broadcasted_iota(jnp.int32, (1, LANES), 1)
    a_neg = jnp.where(lane < SSM_HEADS, -jnp.exp(alog_ref[...]), 0.0)
    dt = _softplus(dt_ref[...] + dtb_ref[...])
    dec = jnp.exp(dt * a_neg)
    xdt = xs * _lane_bcast_pairs(dt, HEAD_PAIRS)
    rowid = lax.broadcasted_iota(jnp.int32, (bs, LANES), 0)
    row_lo = lax.broadcasted_iota(jnp.int32, (LANES, SSM_STATE), 0) < SSM_HEAD_DIM
    ys = []
    for j in range(HEAD_PAIRS):
        g = j // (HEAD_PAIRS // SSM_GROUPS)
        sl = slice(j * LANES, (j + 1) * LANES)
        b_g = bm[:, g * SSM_STATE:(g + 1) * SSM_STATE]
        c_g = cm[:, g * SSM_STATE:(g + 1) * SSM_STATE]
        xdt_p = xdt[:, sl]
        y_p = jnp.zeros((bs, LANES), F32)
        for i in range(bs):
            xi = jnp.where(rowid == i, xdt_p, 0.0).T.astype(BF16)
            d_a = dec[i:i + 1, 2 * j:2 * j + 1]
            d_b = dec[i:i + 1, 2 * j + 1:2 * j + 2]
            decay = jnp.where(row_lo, jnp.broadcast_to(d_a, row_lo.shape), jnp.broadcast_to(d_b, row_lo.shape))
            new = decay * h0_ref[i, j] + _dot(xi, b_g)
            h1_ref[i, j] = new
            y_p = y_p + jnp.where(rowid == i, _dot_nt(c_g, new.astype(BF16)), 0.0)
        ys.append(y_p + dskip_ref[:, sl] * xs[:, sl])
    y = jnp.concatenate(ys, axis=1)
    gated = y * _silu(z_ref[...])
    gw = SSM_WIDTH // SSM_GROUPS
    outs = []
    for g in range(SSM_GROUPS):
        gg = gated[:, g * gw:(g + 1) * gw]
        outs.append(gg * lax.rsqrt(jnp.mean(gg * gg, axis=-1, keepdims=True) + EPS))
    y_ref[...] = (jnp.concatenate(outs, axis=1) * nw_ref[...]).astype(BF16)


def _ssd_sample(xbc, z, dt, cprev_t, h0, w, bs):
    n = xbc.shape[0]
    tok = lambda i: (i, 0)
    const = lambda i: (0, 0)
    full = lambda a: pl.BlockSpec(a.shape, const)
    return pl.pallas_call(
        _ssd_sample_kernel,
        out_shape=(jax.ShapeDtypeStruct((n, SSM_WIDTH), BF16),
                   jax.ShapeDtypeStruct((CONV_K - 1, n, CONV_DIM), F32),
                   jax.ShapeDtypeStruct((n, HEAD_PAIRS, LANES, SSM_STATE), F32)),
        grid=(n // bs,),
        in_specs=[
            pl.BlockSpec((bs, CONV_DIM), tok), pl.BlockSpec((bs, SSM_WIDTH), tok),
            pl.BlockSpec((bs, LANES), tok),
            pl.BlockSpec((CONV_K - 1, bs, CONV_DIM), lambda i: (0, i, 0)),
            pl.BlockSpec((bs, HEAD_PAIRS, LANES, SSM_STATE), lambda i: (i, 0, 0, 0)),
            full(w["conv_w"]), full(w["conv_b"]), full(w["dt_bias"]), full(w["a_log"]),
            full(w["d_skip"]), full(w["ssm_norm"]),
        ],
        out_specs=(pl.BlockSpec((bs, SSM_WIDTH), tok),
                   pl.BlockSpec((CONV_K - 1, bs, CONV_DIM), lambda i: (0, i, 0)),
                   pl.BlockSpec((bs, HEAD_PAIRS, LANES, SSM_STATE), lambda i: (i, 0, 0, 0))),
        compiler_params=pltpu.CompilerParams(dimension_semantics=("parallel",),
                                             vmem_limit_bytes=VMEM_LIMIT),
        name="ssd_sample",
    )(xbc, z, dt, cprev_t, h0, w["conv_w"], w["conv_b"], w["dt_bias"], w["a_log"], w["d_skip"], w["ssm_norm"])


def _pack_bf16_pair(v):
    c = v.shape[1] // 2
    hi = lax.bitcast_convert_type(v[:, :c].astype(BF16).astype(F32), jnp.uint32)
    lo = lax.bitcast_convert_type(v[:, c:].astype(BF16).astype(F32), jnp.uint32)
    return hi | (lo >> 16)


def _unpack_bf16_pair(word):
    a = lax.bitcast_convert_type(word & jnp.uint32(0xFFFF0000), F32)
    b = lax.bitcast_convert_type(word << 16, F32)
    return a, b


def _outproj_router_kernel(x_ref, a_ref, s_ref, wo_a_ref, wo_s_ref, n2_ref, wr_hi_ref, wr_lo_ref, br_ref, tri_ref,
                           h_ref, t_ref, route_ref, cnt_ref, carry_ref):
    @pl.when(pl.program_id(0) == 0)
    def _():
        carry_ref[...] = jnp.zeros(carry_ref.shape, F32)

    h = x_ref[...] + _dot(a_ref[...].astype(BF16), wo_a_ref[...]) + _dot(s_ref[...].astype(BF16), wo_s_ref[...])
    h_ref[...] = h
    ms = jnp.mean(h * h, axis=-1, keepdims=True)
    t = h * lax.rsqrt(ms + EPS) * n2_ref[...]
    t_hi, t_lo = _split2(t)
    t_ref[...] = _pack_bf16_pair(t)
    logits = (_dot(t_hi, wr_hi_ref[...]) + _dot(t_lo, wr_hi_ref[...]) + _dot(t_hi, wr_lo_ref[...])
              + br_ref[...])
    lane = lax.broadcasted_iota(jnp.int32, logits.shape, 1)
    lane_f = lane.astype(F32)
    big = float(LANES)
    ninf = -jnp.inf
    glog = jnp.where(lane < N_EGROUPS, logits, ninf)
    gmax = jnp.max(glog, axis=-1, keepdims=True)
    g_top = 1.0 / jnp.sum(jnp.exp(glog - gmax), axis=-1, keepdims=True)
    g_idx = jnp.min(jnp.where(glog == gmax, lane_f, big), axis=-1, keepdims=True)
    e_of_lane = lane - EXPERT_LANE0
    in_grp = jnp.logical_and(jnp.logical_and(e_of_lane >= 0, e_of_lane < N_EXPERTS),
                             (e_of_lane // EXP_PER_GROUP).astype(F32) == g_idx)
    ml = jnp.where(in_grp, logits, ninf)
    m1 = jnp.max(ml, axis=-1, keepdims=True)
    i1 = jnp.min(jnp.where(ml == m1, lane_f, big), axis=-1, keepdims=True)
    ml2 = jnp.where(lane_f == i1, ninf, ml)
    m2 = jnp.max(ml2, axis=-1, keepdims=True)
    i2 = jnp.min(jnp.where(ml2 == m2, lane_f, big), axis=-1, keepdims=True)
    r = jnp.exp(m2 - m1)
    w1 = g_top / (1.0 + r)
    w2 = g_top * r / (1.0 + r)
    is1 = lane_f == i1
    is2 = lane_f == i2
    onehot = jnp.where(jnp.logical_or(is1, is2), 1.0, 0.0)
    cum = _dot(tri_ref[...], onehot.astype(BF16)) + carry_ref[0:1, :]
    before = cum - onehot
    rank1 = jnp.sum(jnp.where(is1, before, 0.0), axis=-1, keepdims=True)
    rank2 = jnp.sum(jnp.where(is2, before, 0.0), axis=-1, keepdims=True)
    tm = cum.shape[0]
    carry_ref[0:1, :] = cum[tm - 1:tm, :]
    cnt_ref[...] = jnp.broadcast_to(cum[tm - 1:tm, :], cnt_ref.shape)
    fields = (i1 - EXPERT_LANE0, i2 - EXPERT_LANE0, w1, w2, rank1, rank2)
    route = jnp.zeros(logits.shape, F32)
    for pos, val in enumerate(fields):
        route = jnp.where(lane == pos, val, route)
    route_ref[...] = route


def _outproj_router(x2d, attn, ssm, w, tm):
    t = x2d.shape[0]
    tok = lambda i: (i, 0)
    const = lambda i: (0, 0)
    full = lambda a: pl.BlockSpec(a.shape, const)
    idx = jnp.arange(tm)
    tri = (idx[None, :] <= idx[:, None]).astype(BF16)
    return pl.pallas_call(
        _outproj_router_kernel,
        out_shape=(jax.ShapeDtypeStruct((t, D_MODEL), F32), jax.ShapeDtypeStruct((t, D_MODEL // 2), jnp.uint32),
                   jax.ShapeDtypeStruct((t, LANES), F32), jax.ShapeDtypeStruct((8, LANES), F32)),
        grid=(t // tm,),
        in_specs=[
            pl.BlockSpec((tm, D_MODEL), tok), pl.BlockSpec((tm, ATTN_WIDTH), tok),
            pl.BlockSpec((tm, SSM_WIDTH), tok),
            full(w["wo_a"]), full(w["wo_s"]), full(w["norm2"]), full(w["wr_hi"]), full(w["wr_lo"]),
            full(w["br"]), full(tri),
        ],
        out_specs=(pl.BlockSpec((tm, D_MODEL), tok), pl.BlockSpec((tm, D_MODEL // 2), tok),
                   pl.BlockSpec((tm, LANES), tok), pl.BlockSpec((8, LANES), const)),
        scratch_shapes=[pltpu.VMEM((8, LANES), F32)],
        compiler_params=pltpu.CompilerParams(dimension_semantics=("arbitrary",),
                                             vmem_limit_bytes=VMEM_LIMIT),
        name="outproj_router",
    )(x2d, attn, ssm, w["wo_a"], w["wo_s"], w["norm2"], w["wr_hi"], w["wr_lo"], w["br"], tri)


def _expert_hidden(t_a, t_b, w1):
    half = D_MODEL // 2
    gu = _dot(t_a, w1[:half]) + _dot(t_b, w1[half:])
    return _silu(gu[:, :EXPERT_FF]) * gu[:, EXPERT_FF:]


def _moe_dense_kernel(h_ref, t_ref, route_ref, w1_ref, w2_ref, y_ref, acc_ref):
    t_a, t_b = _unpack_bf16_pair(t_ref[...])
    t_a, t_b = t_a.astype(BF16), t_b.astype(BF16)
    route = route_ref[...]
    e1, e2, g1, g2 = route[:, 0:1], route[:, 1:2], route[:, 2:3], route[:, 3:4]
    acc_ref[...] = h_ref[...]
    for e in range(N_EXPERTS):
        hid = _expert_hidden(t_a, t_b, w1_ref[e])
        c_e = jnp.where(e1 == float(e), g1, 0.0) + jnp.where(e2 == float(e), g2, 0.0)
        acc_ref[...] += _dot((hid * c_e).astype(BF16), w2_ref[e])
    y_ref[...] = acc_ref[...]


def _moe_dense(h, t, route, w, tm):
    n = h.shape[0]
    tok = lambda i: (i, 0)
    resident = lambda a: pl.BlockSpec(a.shape, lambda i: (0, 0, 0), pipeline_mode=pl.Buffered(1))
    return pl.pallas_call(
        _moe_dense_kernel,
        out_shape=jax.ShapeDtypeStruct((n, D_MODEL), F32),
        grid=(n // tm,),
        in_specs=[pl.BlockSpec((tm, D_MODEL), tok), pl.BlockSpec((tm, D_MODEL // 2), tok),
                  pl.BlockSpec((tm, LANES), tok), resident(w["w1e"]), resident(w["w2e"])],
        out_specs=pl.BlockSpec((tm, D_MODEL), tok),
        scratch_shapes=[pltpu.VMEM((tm, D_MODEL), F32)],
        compiler_params=pltpu.CompilerParams(dimension_semantics=("parallel",),
                                             vmem_limit_bytes=VMEM_LIMIT),
        name="moe_dense",
    )(h, t, route, w["w1e"], w["w2e"])


def _sc_scatter_rows(src, pos1, pos2, n_out):
    t, width = src.shape
    rows_per_worker = t // SC_WORKERS
    n_chunks = rows_per_worker // SC_CHUNK
    assert t == SC_WORKERS * SC_CHUNK * n_chunks
    mesh = plsc.VectorSubcoreMesh(core_axis_name="c", subcore_axis_name="s")

    @functools.partial(
        pl.kernel, mesh=mesh,
        out_type=jax.ShapeDtypeStruct((n_out, width), src.dtype),
        scratch_types=[pltpu.VMEM((SC_CHUNK,), jnp.int32), pltpu.VMEM((SC_CHUNK,), jnp.int32),
                       pltpu.VMEM((SC_CHUNK, width), src.dtype), pltpu.SemaphoreType.DMA],
    )
    def scatter_kernel(src_hbm, p1_hbm, p2_hbm, out_hbm, i1_v, i2_v, rows_v, sem):
        wid = lax.axis_index("s") * SC_CORES + lax.axis_index("c")
        base = wid * rows_per_worker

        @pl.loop(0, n_chunks)
        def _(c):
            off = pl.multiple_of(base + c * SC_CHUNK, 8)
            pltpu.sync_copy(p1_hbm.at[pl.ds(off, SC_CHUNK)], i1_v)
            pltpu.sync_copy(p2_hbm.at[pl.ds(off, SC_CHUNK)], i2_v)
            pltpu.sync_copy(src_hbm.at[pl.ds(off, SC_CHUNK)], rows_v)
            pltpu.async_copy(rows_v, out_hbm.at[i1_v], sem).wait()
            pltpu.async_copy(rows_v, out_hbm.at[i2_v], sem).wait()

    return scatter_kernel(src, pos1, pos2)


def _sc_gather_rows(table, idx):
    n, width = idx.shape[0], table.shape[1]
    rows_per_worker = n // SC_WORKERS
    n_chunks = rows_per_worker // SC_CHUNK
    assert n == SC_WORKERS * SC_CHUNK * n_chunks and n_chunks % 2 == 0
    mesh = plsc.VectorSubcoreMesh(core_axis_name="c", subcore_axis_name="s")

    @functools.partial(
        pl.kernel, mesh=mesh,
        out_type=jax.ShapeDtypeStruct((n, width), table.dtype),
        scratch_types=[pltpu.VMEM((2, SC_CHUNK), jnp.int32), pltpu.VMEM((2, SC_CHUNK, width), table.dtype),
                       pltpu.SemaphoreType.DMA, pltpu.SemaphoreType.DMA,
                       pltpu.SemaphoreType.DMA, pltpu.SemaphoreType.DMA],
    )
    def gather_kernel(table_hbm, idx_hbm, out_hbm, idx_v, rows_v, g0, g1, w0, w1):
        wid = lax.axis_index("s") * SC_CORES + lax.axis_index("c")
        base = wid * rows_per_worker
        gsem = (g0, g1)
        wsem = (w0, w1)

        def gather_copy(slot):
            return pltpu.make_async_copy(table_hbm.at[idx_v.at[slot]], rows_v.at[slot], gsem[slot])

        def write_copy(c, slot):
            off = pl.multiple_of(base + c * SC_CHUNK, 8)
            return pltpu.make_async_copy(rows_v.at[slot], out_hbm.at[pl.ds(off, SC_CHUNK)], wsem[slot])

        def start_gather(c, slot):
            off = pl.multiple_of(base + c * SC_CHUNK, 8)
            pltpu.sync_copy(idx_hbm.at[pl.ds(off, SC_CHUNK)], idx_v.at[slot])
            gather_copy(slot).start()

        start_gather(0, 0)

        @pl.loop(0, n_chunks, step=2)
        def _(c):
            @pl.when(c > 0)
            def _():
                write_copy(c - 1, 1).wait()

            start_gather(c + 1, 1)
            gather_copy(0).wait()
            write_copy(c, 0).start()
            gather_copy(1).wait()
            write_copy(c + 1, 1).start()
            write_copy(c, 0).wait()

            @pl.when(c + 2 < n_chunks)
            def _():
                start_gather(c + 2, 0)

        write_copy(n_chunks - 1, 1).wait()

    return gather_kernel(table, idx)


def _moe_grouped_kernel(te_ref, nt_ref, x_ref, w1_ref, w2_ref, o_ref):
    @pl.when(pl.program_id(0) < nt_ref[0])
    def _():
        t_a, t_b = _unpack_bf16_pair(x_ref[...])
        hid = _expert_hidden(t_a.astype(BF16), t_b.astype(BF16), w1_ref[0])
        o_ref[...] = _pack_bf16_pair(_dot(hid.astype(BF16), w2_ref[0]))


def _moe_grouped(xs, tile_expert, n_tiles, w):
    rows = xs.shape[0]
    row = lambda i, te, nt: (i, 0)
    by_expert = lambda i, te, nt: (te[i], 0, 0)
    return pl.pallas_call(
        _moe_grouped_kernel,
        out_shape=jax.ShapeDtypeStruct((rows, D_MODEL // 2), jnp.uint32),
        grid_spec=pltpu.PrefetchScalarGridSpec(
            num_scalar_prefetch=2,
            grid=(rows // MOE_TILE,),
            in_specs=[pl.BlockSpec((MOE_TILE, D_MODEL // 2), row),
                      pl.BlockSpec((1, D_MODEL, 2 * EXPERT_FF), by_expert),
                      pl.BlockSpec((1, EXPERT_FF, D_MODEL), by_expert)],
            out_specs=pl.BlockSpec((MOE_TILE, D_MODEL // 2), row),
        ),
        compiler_params=pltpu.CompilerParams(dimension_semantics=("arbitrary",),
                                             vmem_limit_bytes=VMEM_LIMIT),
        name="moe_grouped",
    )(tile_expert, n_tiles, xs, w["w1e"], w["w2e"])


def _moe_combine_kernel(h_ref, z1_ref, z2_ref, route_ref, y_ref):
    route = route_ref[...]
    g1, g2 = route[:, 2:3], route[:, 3:4]
    half = D_MODEL // 2
    a1, b1 = _unpack_bf16_pair(z1_ref[...])
    a2, b2 = _unpack_bf16_pair(z2_ref[...])
    y_ref[:, :half] = h_ref[:, :half] + g1 * a1 + g2 * a2
    y_ref[:, half:] = h_ref[:, half:] + g1 * b1 + g2 * b2


def _moe_combine(h, z, route, tm):
    t = h.shape[0]
    nb = t // tm
    tok = lambda i: (i, 0)
    return pl.pallas_call(
        _moe_combine_kernel,
        out_shape=jax.ShapeDtypeStruct((t, D_MODEL), F32),
        grid=(nb,),
        in_specs=[pl.BlockSpec((tm, D_MODEL), tok), pl.BlockSpec((tm, D_MODEL // 2), tok),
                  pl.BlockSpec((tm, D_MODEL // 2), lambda i: (i + nb, 0)), pl.BlockSpec((tm, LANES), tok)],
        out_specs=pl.BlockSpec((tm, D_MODEL), tok),
        compiler_params=pltpu.CompilerParams(dimension_semantics=("parallel",),
                                             vmem_limit_bytes=VMEM_LIMIT),
        name="moe_combine",
    )(h, z, z, route)


def _route_pos_kernel(route_ref, cnt_ref, upper_ref, pos_ref):
    tm = route_ref.shape[0]
    cnt = cnt_ref[...]
    padded = jnp.floor((cnt + float(MOE_TILE - 1)) * (1.0 / MOE_TILE)) * float(MOE_TILE)
    p_hi, p_mid, p_lo = _split3(padded)
    upper = upper_ref[...]
    starts = (_dot(p_hi, upper) + _dot(p_mid, upper) + _dot(p_lo, upper))[0:1, :]
    route = route_ref[...]
    lane_f = lax.broadcasted_iota(jnp.int32, (tm, LANES), 1).astype(F32)
    diag = (lax.broadcasted_iota(jnp.int32, (LANES, LANES), 0)
            == lax.broadcasted_iota(jnp.int32, (LANES, LANES), 1))
    for k in range(2):
        e_lane = route[:, k:k + 1] + float(EXPERT_LANE0)
        pos = jnp.sum(jnp.where(lane_f == e_lane, starts, 0.0), axis=-1, keepdims=True) + route[:, 4 + k:5 + k]
        for r in range(tm // LANES):
            col = pos[r * LANES:(r + 1) * LANES, :]
            row = jnp.sum(jnp.where(diag, col, 0.0), axis=0, keepdims=True)
            pos_ref[k, r:r + 1, :] = row.astype(jnp.int32)


def _route_positions(route, counts, tm):
    t = route.shape[0]
    idx = jnp.arange(LANES)
    upper = (idx[:, None] < idx[None, :]).astype(BF16)
    return pl.pallas_call(
        _route_pos_kernel,
        out_shape=jax.ShapeDtypeStruct((2, t // LANES, LANES), jnp.int32),
        grid=(t // tm,),
        in_specs=[pl.BlockSpec((tm, LANES), lambda i: (i, 0)), pl.BlockSpec((8, LANES), lambda i: (0, 0)),
                  pl.BlockSpec((LANES, LANES), lambda i: (0, 0))],
        out_specs=pl.BlockSpec((2, tm // LANES, LANES), lambda i: (0, i, 0)),
        compiler_params=pltpu.CompilerParams(dimension_semantics=("parallel",),
                                             vmem_limit_bytes=VMEM_LIMIT),
        name="route_positions",
    )(route, counts, upper)


def _moe_routed(h, t_packed, route, counts, w, tm):
    t = h.shape[0]
    pos = _route_positions(route, counts, min(8 * LANES, t))
    pos1 = pos[0].reshape(t)
    pos2 = pos[1].reshape(t)
    cnt = counts[0, EXPERT_LANE0:EXPERT_LANE0 + N_EXPERTS].astype(jnp.int32)
    padded = (cnt + MOE_TILE - 1) // MOE_TILE * MOE_TILE
    ends = jnp.cumsum(padded)
    n_rows = 2 * t + N_EXPERTS * MOE_TILE
    n_tiles = ends[N_EXPERTS - 1] // MOE_TILE
    tile_start = jnp.arange(n_rows // MOE_TILE, dtype=jnp.int32) * MOE_TILE
    tile_start = jnp.minimum(tile_start, ends[N_EXPERTS - 1] - MOE_TILE)
    tile_expert = jnp.sum((tile_start[:, None] >= ends[None, :]).astype(jnp.int32), axis=1)
    xs = _sc_scatter_rows(t_packed, pos1, pos2, n_rows)
    out = _moe_grouped(xs, tile_expert, n_tiles.reshape(1), w)
    z = _sc_gather_rows(out, pos.reshape(2 * t))
    return _moe_combine(h, z, route, tm)


def _pad_lanes(a, width=LANES):
    return jnp.pad(a, ((0, 0), (0, width - a.shape[1])))


def _prep_weights(norm1, w_in, q_norm, k_norm, conv_w, conv_b, dt_bias, a_log, d_skip, ssm_norm, w_out,
                  norm2, w_grp, b_grp, w_exp, b_exp, w_gate, w_up, w_down):
    w = {}
    w["norm1"] = norm1.reshape(1, D_MODEL)
    w["wqk"] = w_in[:, :K_END].astype(BF16)
    w["wv"] = w_in[:, K_END:V_END].astype(BF16)
    w["wz"] = w_in[:, V_END:Z_END].astype(BF16)
    w["wxbc"] = w_in[:, Z_END:XBC_END].astype(BF16)
    w["wdt"] = _pad_lanes(w_in[:, XBC_END:]).astype(BF16)
    w["qkn"] = jnp.concatenate([jnp.tile(q_norm, N_HEADS), jnp.tile(k_norm, KV_HEADS)]).reshape(1, QK_WIDTH)
    head_of_col = jnp.arange(QK_WIDTH) // HEAD_DIM
    red = (head_of_col[:, None] == jnp.arange(LANES)[None, :])
    w["red"] = red.astype(BF16)
    w["exp"] = red.T.astype(BF16)
    w["conv_w"] = conv_w
    w["conv_b"] = conv_b.reshape(1, CONV_DIM)
    w["dt_bias"] = _pad_lanes(dt_bias.reshape(1, SSM_HEADS))
    w["a_log"] = _pad_lanes(a_log.reshape(1, SSM_HEADS))
    w["d_skip"] = jnp.repeat(d_skip, SSM_HEAD_DIM).reshape(1, SSM_WIDTH)
    w["ssm_norm"] = ssm_norm.reshape(1, SSM_WIDTH)
    idx = jnp.arange(SSD_CHUNK)
    w["tri"] = (idx[None, :] <= idx[:, None]).astype(BF16)
    w["wo_a"] = w_out[:ATTN_WIDTH].astype(BF16)
    w["wo_s"] = w_out[ATTN_WIDTH:].astype(BF16)
    w["norm2"] = norm2.reshape(1, D_MODEL)
    wr = jnp.zeros((D_MODEL, LANES), F32)
    wr = wr.at[:, :N_EGROUPS].set(w_grp).at[:, EXPERT_LANE0:EXPERT_LANE0 + N_EXPERTS].set(w_exp)
    w["wr_hi"] = wr.astype(BF16)
    w["wr_lo"] = (wr - w["wr_hi"].astype(F32)).astype(BF16)
    br = jnp.zeros((1, LANES), F32)
    w["br"] = br.at[0, :N_EGROUPS].set(b_grp).at[0, EXPERT_LANE0:EXPERT_LANE0 + N_EXPERTS].set(b_exp)
    w["w1e"] = jnp.concatenate([w_gate, w_up], axis=-1).astype(BF16)
    w["w2e"] = w_down.astype(BF16)
    return w


def _rope_tables(pos):
    inv = 1.0 / (ROPE_THETA ** (jnp.arange(0, HEAD_DIM, 2, dtype=F32) / HEAD_DIM))
    ang = pos.astype(F32)[:, None] * inv[None, :]
    cos, sin = jnp.cos(ang), jnp.sin(ang)
    reps = LANES // HEAD_DIM
    return (jnp.tile(jnp.concatenate([cos, cos], axis=-1), (1, reps)),
            jnp.tile(jnp.concatenate([-sin, sin], axis=-1), (1, reps)))


def _token_tile(t):
    for tm in (1024, 512, 256, 128, 64, 32, 16):
        if t % tm == 0:
            return tm
    raise ValueError(f"token count {t} is not a multiple of 16")


def kernel(x_prompt, x_sample, cache_win_k, cache_win_v, state_conv, state_ssm, norm1, w_in, q_norm, k_norm,
           sinks, conv_w, conv_b, dt_bias, a_log, d_skip, ssm_norm, w_out, norm2, w_grp, b_grp, w_exp, b_exp,
           w_gate, w_up, w_down):
    depth = norm1.shape[0]
    assert depth == 1, "single-layer stack"
    bp, lp, _ = x_prompt.shape
    bsn, ls, _ = x_sample.shape
    assert ls == 1 and lp % WINDOW == 0 and cache_win_k.shape[2] == WINDOW
    l = 0
    w = _prep_weights(norm1[l], w_in[l], q_norm[l], k_norm[l], conv_w[l], conv_b[l], dt_bias[l], a_log[l],
                      d_skip[l], ssm_norm[l], w_out[l], norm2[l], w_grp[l], b_grp[l], w_exp[l], b_exp[l],
                      w_gate[l], w_up[l], w_down[l])
    sink = sinks[l]

    tp = bp * lp
    xp = x_prompt.reshape(tp, D_MODEL)
    tm_p = _token_tile(lp)
    cos_p, sin_p = _rope_tables(jnp.arange(lp, dtype=jnp.int32))
    q, k, v, z, xbc, dt = _inproj(xp, w, cos_p, sin_p, tm_p, lp // tm_p)
    attn = _attn_prompt(q, k, v, sink, bp, lp)
    ssm, st_p = _ssd_prompt(xbc, z, dt, w, bp, lp)
    h, t, route, counts = _outproj_router(xp, attn, ssm, w, tm_p)
    y_prompt = _moe_routed(h, t, route, counts, w, tm_p).reshape(bp, lp, D_MODEL)
    k3 = k.reshape(bp, lp, KV_HEADS, HEAD_DIM)
    v3 = v.reshape(bp, lp, KV_HEADS, HEAD_DIM)
    win_k_p = k3[:, lp - WINDOW:][None]
    win_v_p = v3[:, lp - WINDOW:][None]
    conv_p = xbc.reshape(bp, lp, CONV_DIM)[:, lp - (CONV_K - 1):][None]
    ssm_p = st_p.reshape(1, bp, SSM_HEADS, SSM_HEAD_DIM, SSM_STATE)

    xs2 = x_sample.reshape(bsn, D_MODEL)
    tm_s = _token_tile(bsn)
    cos_s, sin_s = _rope_tables(jnp.full((tm_s,), PAST_LEN, jnp.int32))
    q_s, k_s, v_s, z_s, xbc_s, dt_s = _inproj(xs2, w, cos_s, sin_s, tm_s, 1)
    q4 = q_s.reshape(bsn, KV_HEADS, N_HEADS // KV_HEADS, HEAD_DIM)
    zq = jnp.zeros_like(q4[:, 0])
    qx = jnp.concatenate([jnp.concatenate([q4[:, 0], zq], axis=-1),
                          jnp.concatenate([zq, q4[:, 1]], axis=-1)], axis=1)
    qx = jnp.pad(qx, ((0, 0), (0, BF16_ROWS - N_HEADS), (0, 0)))
    sink_x = jnp.pad(jnp.broadcast_to(sink[:, None], (N_HEADS, LANES)), ((0, BF16_ROWS - N_HEADS), (0, 0)))
    kc = cache_win_k[l].reshape(bsn, WINDOW, KV_WIDTH)
    vc = cache_win_v[l].reshape(bsn, WINDOW, KV_WIDTH)
    ko, vo, attn_s = _attn_sample(qx, kc, k_s.reshape(bsn, 1, KV_WIDTH), vc, v_s.reshape(bsn, 1, KV_WIDTH),
                               sink_x, 8)
    cprev_t = jnp.transpose(state_conv[l], (1, 0, 2))
    h0 = state_ssm[l].reshape(bsn, HEAD_PAIRS, LANES, SSM_STATE)
    ssm_s, cnew_t, h1 = _ssd_sample(xbc_s, z_s, dt_s, cprev_t, h0, w, 16)
    h_s, t_s, route_s, _ = _outproj_router(xs2, attn_s, ssm_s, w, tm_s)
    y_sample = _moe_dense(h_s, t_s, route_s, w, tm_s).reshape(bsn, 1, D_MODEL)
    win_k_s = ko.reshape(1, bsn, WINDOW, KV_HEADS, HEAD_DIM)
    win_v_s = vo.reshape(1, bsn, WINDOW, KV_HEADS, HEAD_DIM)
    conv_s = jnp.transpose(cnew_t, (1, 0, 2))[None]
    ssm_s_state = h1.reshape(1, bsn, SSM_HEADS, SSM_HEAD_DIM, SSM_STATE)

    return (y_prompt, y_sample, win_k_p, win_v_p, conv_p, ssm_p, win_k_s, win_v_s, conv_s, ssm_s_state)
```

```python
import functools
import math

import jax
import jax.numpy as jnp
from jax import lax
from jax.experimental import pallas as pl
from jax.experimental.pallas import tpu as pltpu
from jax.experimental.pallas import tpu_sc as plsc

F32 = jnp.float32
BF16 = jnp.bfloat16

D_MODEL = 1024
HEAD_DIM = 64
N_HEADS = 8
KV_HEADS = 2
WINDOW = 128
ATTN_WIDTH = N_HEADS * HEAD_DIM
QK_WIDTH = ATTN_WIDTH + KV_HEADS * HEAD_DIM
KV_WIDTH = KV_HEADS * HEAD_DIM
ATTN_SCALE = HEAD_DIM ** -0.5
ROPE_THETA = 10000.0
SSM_WIDTH = 512
SSM_HEADS = 8
SSM_HEAD_DIM = 64
SSM_GROUPS = 2
SSM_STATE = 128
CONV_K = 4
CONV_DIM = SSM_WIDTH + 2 * SSM_GROUPS * SSM_STATE
SSD_CHUNK = 128
N_EGROUPS = 4
EXP_PER_GROUP = 8
N_EXPERTS = 32
EXPERT_FF = 128
EPS = 1e-6
PAST_LEN = 16384

LANES = 128
BF16_ROWS = 16
HEAD_PAIRS = SSM_HEADS // 2
EXPERT_LANE0 = 32
VMEM_LIMIT = 56 * 1024 * 1024
MOE_TILE = 512
ATTN_QBLOCKS = 2
COUNT_BLOCK = 256
SSD_CHUNKS_PER_STEP = 4
SC_CORES = 2
SC_SUBCORES = 16
SC_WORKERS = SC_CORES * SC_SUBCORES
SC_CHUNK = 64

Q_END = ATTN_WIDTH
K_END = Q_END + KV_WIDTH
V_END = K_END + KV_WIDTH
Z_END = V_END + SSM_WIDTH
XBC_END = Z_END + CONV_DIM


def _dot(a, b):
    return jnp.dot(a, b, preferred_element_type=F32)


def _dot_nt(a, b):
    return lax.dot_general(a, b, (((1,), (1,)), ((), ())), preferred_element_type=F32)


def _split2(v):
    hi = v.astype(BF16)
    lo = (v - hi.astype(F32)).astype(BF16)
    return hi, lo


def _split3(v):
    hi = v.astype(BF16)
    r = v - hi.astype(F32)
    mid = r.astype(BF16)
    lo = (r - mid.astype(F32)).astype(BF16)
    return hi, mid, lo


def _silu(x):
    return x * jax.nn.sigmoid(x)


def _softplus(x):
    return jnp.maximum(x, 0.0) + jnp.log1p(jnp.exp(-jnp.abs(x)))


def _lane_bcast_pairs(v, n_pairs):
    r = v.shape[0]
    lo = lax.broadcasted_iota(jnp.int32, (r, LANES), 1) < HEAD_DIM
    slabs = []
    for j in range(n_pairs):
        a = jnp.broadcast_to(v[:, 2 * j:2 * j + 1], (r, LANES))
        b = jnp.broadcast_to(v[:, 2 * j + 1:2 * j + 2], (r, LANES))
        slabs.append(jnp.where(lo, a, b))
    return jnp.concatenate(slabs, axis=1)


def _inproj_kernel(x_ref, n1_ref, wqk_ref, wv_ref, wz_ref, wxbc_ref, wdt_ref, dtb_ref, qkn_ref,
                   cos_ref, sin_ref, red_ref, exp_ref,
                   q_ref, k_ref, v_ref, z_ref, xbc_ref, dt_ref):
    x = x_ref[...]
    ms = jnp.mean(x * x, axis=-1, keepdims=True)
    xn = (x * lax.rsqrt(ms + EPS) * n1_ref[...]).astype(BF16)
    v_ref[...] = _dot(xn, wv_ref[...])
    z_ref[...] = _dot(xn, wz_ref[...])
    xbc_ref[...] = _dot(xn, wxbc_ref[...])
    dt_ref[...] = _softplus(_dot(xn, wdt_ref[...]) + dtb_ref[...])
    qk = _dot(xn, wqk_ref[...])
    sq_hi, sq_lo = _split2(qk * qk)
    ss = _dot(sq_hi, red_ref[...]) + _dot(sq_lo, red_ref[...])
    inv = lax.rsqrt(ss * (1.0 / HEAD_DIM) + EPS)
    inv_hi, inv_lo = _split2(inv)
    inv_x = _dot(inv_hi, exp_ref[...]) + _dot(inv_lo, exp_ref[...])
    qkn = qk * inv_x * qkn_ref[...]
    cos = cos_ref[...]
    sin = sin_ref[...]
    lane = lax.broadcasted_iota(jnp.int32, (x.shape[0], LANES), 1)
    first_half = (lane % HEAD_DIM) < (HEAD_DIM // 2)
    for c in range(QK_WIDTH // LANES):
        xc = qkn[:, c * LANES:(c + 1) * LANES]
        partner = jnp.where(first_half,
                            pltpu.roll(xc, LANES - HEAD_DIM // 2, axis=1),
                            pltpu.roll(xc, HEAD_DIM // 2, axis=1))
        rot = xc * cos + partner * sin
        if c < ATTN_WIDTH // LANES:
            q_ref[:, c * LANES:(c + 1) * LANES] = (rot * ATTN_SCALE).astype(BF16)
        else:
            k_ref[...] = rot


def _inproj(x2d, w, cos_tab, sin_tab, tm, n_pos_blocks):
    t = x2d.shape[0]
    grid = (t // tm,)
    tok = lambda i: (i, 0)
    const = lambda i: (0, 0)
    pos = lambda i: (i % n_pos_blocks, 0)
    full = lambda a: pl.BlockSpec(a.shape, const)
    out_shapes = (
        jax.ShapeDtypeStruct((t, ATTN_WIDTH), BF16),
        jax.ShapeDtypeStruct((t, KV_WIDTH), F32),
        jax.ShapeDtypeStruct((t, KV_WIDTH), F32),
        jax.ShapeDtypeStruct((t, SSM_WIDTH), F32),
        jax.ShapeDtypeStruct((t, CONV_DIM), F32),
        jax.ShapeDtypeStruct((t, LANES), F32),
    )
    return pl.pallas_call(
        _inproj_kernel,
        out_shape=out_shapes,
        grid=grid,
        in_specs=[
            pl.BlockSpec((tm, D_MODEL), tok),
            full(w["norm1"]), full(w["wqk"]), full(w["wv"]), full(w["wz"]), full(w["wxbc"]),
            full(w["wdt"]), full(w["dt_bias"]), full(w["qkn"]),
            pl.BlockSpec((tm, LANES), pos), pl.BlockSpec((tm, LANES), pos),
            full(w["red"]), full(w["exp"]),
        ],
        out_specs=(
            pl.BlockSpec((tm, ATTN_WIDTH), tok), pl.BlockSpec((tm, KV_WIDTH), tok),
            pl.BlockSpec((tm, KV_WIDTH), tok), pl.BlockSpec((tm, SSM_WIDTH), tok),
            pl.BlockSpec((tm, CONV_DIM), tok), pl.BlockSpec((tm, LANES), tok),
        ),
        compiler_params=pltpu.CompilerParams(dimension_semantics=("parallel",),
                                             vmem_limit_bytes=VMEM_LIMIT),
        name="inproj",
    )(x2d, w["norm1"], w["wqk"], w["wv"], w["wz"], w["wxbc"], w["wdt"], w["dt_bias"], w["qkn"],
      cos_tab, sin_tab, w["red"], w["exp"])


def _pair_rhs(blk, g):
    lo = lax.broadcasted_iota(jnp.int32, blk.shape, 1) < HEAD_DIM
    swapped = pltpu.roll(blk, HEAD_DIM, axis=1)
    if g == 0:
        top = jnp.where(lo, blk, 0.0)
        bot = jnp.where(lo, 0.0, swapped)
    else:
        top = jnp.where(lo, swapped, 0.0)
        bot = jnp.where(lo, 0.0, blk)
    return jnp.concatenate([top, bot], axis=0).astype(BF16)


def _attn_kernel(sink_ref, q_ref, kc_ref, kp_ref, vc_ref, vp_ref, o_ref):
    blk = WINDOW
    n_sub = q_ref.shape[0] // blk
    first_step = pl.program_id(1) == 0
    qi = lax.broadcasted_iota(jnp.int32, (blk, 2 * blk), 0)
    kj = lax.broadcasted_iota(jnp.int32, (blk, 2 * blk), 1) % blk
    cur_ok = kj <= qi
    seq_start_ok = jnp.logical_or(cur_ok, jnp.logical_not(first_step))
    lo = lax.broadcasted_iota(jnp.int32, (blk, LANES), 1) < HEAD_DIM
    k_blocks = [kp_ref[...]] + [kc_ref[u * blk:(u + 1) * blk, :] for u in range(n_sub)]
    v_blocks = [vp_ref[...]] + [vc_ref[u * blk:(u + 1) * blk, :] for u in range(n_sub)]
    for g in range(KV_HEADS):
        k2 = [_pair_rhs(b, g) for b in k_blocks]
        v2 = [_pair_rhs(b, g) for b in v_blocks]
        n_pairs = N_HEADS // KV_HEADS // 2
        for u in range(n_sub):
            rows = slice(u * blk, (u + 1) * blk)
            q_all = jnp.concatenate([q_ref[rows, (g * n_pairs + r) * LANES:(g * n_pairs + r + 1) * LANES]
                                     for r in range(n_pairs)], axis=0)
            s_all = _dot_nt(q_all, jnp.concatenate([k2[u + 1], k2[u]], axis=0))
            p_rows = []
            den_rows = []
            for r in range(n_pairs):
                pair = g * n_pairs + r
                s_cur = s_all[r * blk:(r + 1) * blk, :2 * blk]
                s_prev = s_all[r * blk:(r + 1) * blk, 2 * blk:]
                s = jnp.where(cur_ok, s_cur, s_prev)
                if u == 0:
                    s = jnp.where(seq_start_ok, s, -jnp.inf)
                ps = []
                dens = []
                for hh in range(2):
                    sink = sink_ref[2 * pair + hh]
                    sh = s[:, hh * blk:(hh + 1) * blk]
                    m = jnp.maximum(jnp.max(sh, axis=-1, keepdims=True), sink)
                    p = jnp.exp(sh - m)
                    dens.append(jnp.sum(p, axis=-1, keepdims=True) + jnp.exp(sink - m))
                    ps.append(p)
                p2 = jnp.concatenate(ps, axis=1)
                p_rows.append(jnp.concatenate([jnp.where(cur_ok, p2, 0.0), jnp.where(cur_ok, 0.0, p2)],
                                              axis=1).astype(BF16))
                den_rows.append(jnp.where(lo, dens[0], dens[1]))
            o_all = _dot(jnp.concatenate(p_rows, axis=0), jnp.concatenate([v2[u + 1], v2[u]], axis=0))
            for r in range(n_pairs):
                pair = g * n_pairs + r
                o2 = o_all[r * blk:(r + 1) * blk, :]
                o_ref[rows, pair * LANES:(pair + 1) * LANES] = (o2 / den_rows[r]).astype(BF16)


def _attn_prompt(q, k, v, sinks, batch, seq):
    n_sub = ATTN_QBLOCKS if seq % (ATTN_QBLOCKS * WINDOW) == 0 else 1
    rows = n_sub * WINDOW
    nb = seq // rows
    cur = lambda b, j, s: (b * nb + j, 0)
    prev = lambda b, j, s: (jnp.maximum((b * nb + j) * n_sub - 1, 0), 0)
    return pl.pallas_call(
        _attn_kernel,
        out_shape=jax.ShapeDtypeStruct((batch * seq, ATTN_WIDTH), BF16),
        grid_spec=pltpu.PrefetchScalarGridSpec(
            num_scalar_prefetch=1,
            grid=(batch, nb),
            in_specs=[
                pl.BlockSpec((rows, ATTN_WIDTH), cur),
                pl.BlockSpec((rows, KV_WIDTH), cur), pl.BlockSpec((WINDOW, KV_WIDTH), prev),
                pl.BlockSpec((rows, KV_WIDTH), cur), pl.BlockSpec((WINDOW, KV_WIDTH), prev),
            ],
            out_specs=pl.BlockSpec((rows, ATTN_WIDTH), cur),
        ),
        compiler_params=pltpu.CompilerParams(dimension_semantics=("parallel", "parallel"),
                                             vmem_limit_bytes=VMEM_LIMIT),
        name="attn_prompt",
    )(sinks, q, k, k, v, v)


def _ssd_kernel(xbc_ref, z_ref, dt_ref, convw_ref, convb_ref, alog_ref, dskip_ref, nw_ref,
                tri_ref, expand_ref, y_ref, st_ref, buf_ref, state_ref):
    c = pl.program_id(1)
    cl = SSD_CHUNK
    n_sub = xbc_ref.shape[0] // cl
    halo = buf_ref.shape[0]

    @pl.when(c == 0)
    def _():
        buf_ref[0:halo, :] = jnp.zeros((halo, CONV_DIM), F32)
        state_ref[...] = jnp.zeros(state_ref.shape, F32)

    lane = lax.broadcasted_iota(jnp.int32, (1, LANES), 1)
    a_neg = jnp.where(lane < SSM_HEADS, -jnp.exp(alog_ref[...]), 0.0)
    tri = tri_ref[...]
    for u in range(n_sub):
        rows = slice(u * cl, (u + 1) * cl)
        if u == 0:
            x_ext = jnp.concatenate([buf_ref[...], xbc_ref[rows, :]], axis=0)
        else:
            x_ext = xbc_ref[u * cl - halo:(u + 1) * cl, :]
        _ssd_chunk(x_ext, z_ref[rows, :], dt_ref[rows, :], a_neg, tri, convw_ref, convb_ref, dskip_ref, nw_ref,
                   expand_ref, y_ref.at[rows, :], state_ref)
    buf_ref[...] = xbc_ref[n_sub * cl - halo:n_sub * cl, :]

    @pl.when(c == pl.num_programs(1) - 1)
    def _():
        st_ref[0] = state_ref[...]


def _ssd_chunk(x_ext, z, dt, a_neg, tri, convw_ref, convb_ref, dskip_ref, nw_ref, expand_ref, y_ref, state_ref):
    cl = SSD_CHUNK
    halo = x_ext.shape[0] - cl
    x_raw = x_ext[halo:, :]
    conv = convb_ref[...] + x_raw * convw_ref[CONV_K - 1:CONV_K, :]
    for j in range(CONV_K - 1):
        shifted = pltpu.roll(x_ext, CONV_K - 1 - j, axis=0)[halo:, :]
        conv = conv + shifted * convw_ref[j:j + 1, :]
    act = _silu(conv)
    xs = act[:, :SSM_WIDTH]
    bm = act[:, SSM_WIDTH:SSM_WIDTH + SSM_GROUPS * SSM_STATE].astype(BF16)
    cm = act[:, SSM_WIDTH + SSM_GROUPS * SSM_STATE:].astype(BF16)

    dta = dt * a_neg
    p_hi, p_mid, p_lo = _split3(dta)
    a_col = _dot(tri, p_hi) + _dot(tri, p_mid) + _dot(tri, p_lo)
    a_last = a_col[cl - 1:cl, :]
    a_row = a_col.T
    per_head = jnp.concatenate([dt, jnp.exp(a_col), jnp.exp(a_last - a_col)], axis=0)
    ph_hi, ph_lo = _split2(per_head)
    per_lane = _dot(ph_hi, expand_ref[...]) + _dot(ph_lo, expand_ref[...])
    dt_x = per_lane[:cl]
    ecol_x = per_lane[cl:2 * cl]
    dte_x = per_lane[2 * cl:]
    e_last = jnp.exp(a_last)
    xdt = xs * dt_x

    li = lax.broadcasted_iota(jnp.int32, (cl, cl), 0)
    si = lax.broadcasted_iota(jnp.int32, (cl, cl), 1)
    causal = si <= li
    lo = lax.broadcasted_iota(jnp.int32, (cl, LANES), 1) < SSM_HEAD_DIM
    row_lo = lax.broadcasted_iota(jnp.int32, (LANES, SSM_STATE), 0) < SSM_HEAD_DIM

    ys = []
    for g in range(SSM_GROUPS):
        b_g = bm[:, g * SSM_STATE:(g + 1) * SSM_STATE]
        c_g = cm[:, g * SSM_STATE:(g + 1) * SSM_STATE]
        cb = _dot_nt(c_g, b_g)
        for r in range(HEAD_PAIRS // SSM_GROUPS):
            j = g * (HEAD_PAIRS // SSM_GROUPS) + r
            sl = slice(j * LANES, (j + 1) * LANES)
            xdt_p = xdt[:, sl]
            ms = []
            for hh in range(2):
                h = 2 * j + hh
                seg = a_col[:, h:h + 1] - a_row[h:h + 1, :]
                ms.append(cb * jnp.exp(jnp.where(causal, seg, -jnp.inf)))
            m2 = jnp.concatenate(ms, axis=1).astype(BF16)
            rhs = jnp.concatenate([jnp.where(lo, xdt_p, 0.0), jnp.where(lo, 0.0, xdt_p)],
                                  axis=0).astype(BF16)
            y_diag = _dot(m2, rhs)
            st = state_ref[j]
            y_off = _dot_nt(c_g, st.astype(BF16)) * ecol_x[:, sl]
            xdt_e = (xdt_p * dte_x[:, sl]).T.astype(BF16)
            d_a = e_last[:, 2 * j:2 * j + 1]
            d_b = e_last[:, 2 * j + 1:2 * j + 2]
            decay = jnp.where(row_lo, jnp.broadcast_to(d_a, row_lo.shape), jnp.broadcast_to(d_b, row_lo.shape))
            state_ref[j] = decay * st + _dot(xdt_e, b_g)
            ys.append(y_diag + y_off + dskip_ref[:, sl] * xs[:, sl])
    y = jnp.concatenate(ys, axis=1)
    gated = y * _silu(z)
    gw = SSM_WIDTH // SSM_GROUPS
    outs = []
    for g in range(SSM_GROUPS):
        gg = gated[:, g * gw:(g + 1) * gw]
        outs.append(gg * lax.rsqrt(jnp.mean(gg * gg, axis=-1, keepdims=True) + EPS))
    y_ref[...] = (jnp.concatenate(outs, axis=1) * nw_ref[...]).astype(BF16)


def _ssd_prompt(xbc, z, dt, w, batch, seq):
    n_sub = SSD_CHUNKS_PER_STEP if seq % (SSD_CHUNKS_PER_STEP * SSD_CHUNK) == 0 else 1
    rows = n_sub * SSD_CHUNK
    nc = seq // rows
    tok = lambda b, c: (b * nc + c, 0)
    const = lambda b, c: (0, 0)
    full = lambda a: pl.BlockSpec(a.shape, const)
    return pl.pallas_call(
        _ssd_kernel,
        out_shape=(jax.ShapeDtypeStruct((batch * seq, SSM_WIDTH), BF16),
                   jax.ShapeDtypeStruct((batch, HEAD_PAIRS, LANES, SSM_STATE), F32)),
        grid=(batch, nc),
        in_specs=[
            pl.BlockSpec((rows, CONV_DIM), tok), pl.BlockSpec((rows, SSM_WIDTH), tok),
            pl.BlockSpec((rows, LANES), tok),
            full(w["conv_w"]), full(w["conv_b"]), full(w["a_log"]),
            full(w["d_skip"]), full(w["ssm_norm"]), full(w["tri"]), full(w["expand"]),
        ],
        out_specs=(pl.BlockSpec((rows, SSM_WIDTH), tok),
                   pl.BlockSpec((1, HEAD_PAIRS, LANES, SSM_STATE), lambda b, c: (b, 0, 0, 0))),
        scratch_shapes=[pltpu.VMEM((8, CONV_DIM), F32),
                        pltpu.VMEM((HEAD_PAIRS, LANES, SSM_STATE), F32)],
        compiler_params=pltpu.CompilerParams(dimension_semantics=("parallel", "arbitrary"),
                                             vmem_limit_bytes=VMEM_LIMIT),
        name="ssd_prompt",
    )(xbc, z, dt, w["conv_w"], w["conv_b"], w["a_log"], w["d_skip"], w["ssm_norm"], w["tri"], w["expand"])


def _attn_sample_kernel(qx_ref, kc_ref, kn_ref, vc_ref, vn_ref, sink_ref, ko_ref, vo_ref, o_ref):
    bs = qx_ref.shape[0]
    w = kc_ref.shape[1]
    sink = sink_ref[...]
    lo = lax.broadcasted_iota(jnp.int32, (1, LANES), 1) < HEAD_DIM
    for i in range(bs):
        ko_ref[i, 0:w - 1, :] = kc_ref[i, 1:w, :]
        ko_ref[i, w - 1:w, :] = kn_ref[i]
        vo_ref[i, 0:w - 1, :] = vc_ref[i, 1:w, :]
        vo_ref[i, w - 1:w, :] = vn_ref[i]
        s = _dot_nt(qx_ref[i], ko_ref[i].astype(BF16))
        m = jnp.maximum(jnp.max(s, axis=-1, keepdims=True), sink)
        p = jnp.exp(s - m)
        den = jnp.sum(p, axis=-1, keepdims=True) + jnp.exp(sink - m)
        o = _dot(p.astype(BF16), vo_ref[i].astype(BF16)) / den
        o_sw = pltpu.roll(o, HEAD_DIM, axis=1)
        for j in range(N_HEADS // 2):
            a, b = (o, o_sw) if j < N_HEADS // 4 else (o_sw, o)
            o_ref[i:i + 1, j * LANES:(j + 1) * LANES] = jnp.where(lo, a[2 * j:2 * j + 1], b[2 * j + 1:2 * j + 2])


def _attn_sample(qx, kc, kn, vc, vn, sink_x, bs):
    n, w = kc.shape[0], kc.shape[1]
    blk3 = lambda i: (i, 0, 0)
    return pl.pallas_call(
        _attn_sample_kernel,
        out_shape=(jax.ShapeDtypeStruct((n, w, KV_WIDTH), F32),
                   jax.ShapeDtypeStruct((n, w, KV_WIDTH), F32),
                   jax.ShapeDtypeStruct((n, ATTN_WIDTH), F32)),
        grid=(n // bs,),
        in_specs=[
            pl.BlockSpec((bs, BF16_ROWS, LANES), blk3),
            pl.BlockSpec((bs, w, KV_WIDTH), blk3), pl.BlockSpec((bs, 1, KV_WIDTH), blk3),
            pl.BlockSpec((bs, w, KV_WIDTH), blk3), pl.BlockSpec((bs, 1, KV_WIDTH), blk3),
            pl.BlockSpec(sink_x.shape, lambda i: (0, 0)),
        ],
        out_specs=(pl.BlockSpec((bs, w, KV_WIDTH), blk3), pl.BlockSpec((bs, w, KV_WIDTH), blk3),
                   pl.BlockSpec((bs, ATTN_WIDTH), lambda i: (i, 0))),
        compiler_params=pltpu.CompilerParams(dimension_semantics=("parallel",),
                                             vmem_limit_bytes=VMEM_LIMIT),
        name="attn_sample",
    )(qx, kc, kn, vc, vn, sink_x)


def _ssd_sample_kernel(xbc_ref, z_ref, dt_ref, cprev_ref, h0_ref, convw_ref, convb_ref, alog_ref,
                       dskip_ref, nw_ref, y_ref, cnew_ref, h1_ref):
    bs = xbc_ref.shape[0]
    x_raw = xbc_ref[...]
    conv = convb_ref[...] + x_raw * convw_ref[CONV_K - 1:CONV_K, :]
    for j in range(CONV_K - 1):
        conv = conv + cprev_ref[j] * convw_ref[j:j + 1, :]
    for j in range(CONV_K - 2):
        cnew_ref[j] = cprev_ref[j + 1]
    cnew_ref[CONV_K - 2] = x_raw
    act = _silu(conv)
    xs = act[:, :SSM_WIDTH]
    bm = act[:, SSM_WIDTH:SSM_WIDTH + SSM_GROUPS * SSM_STATE].astype(BF16)
    cm = act[:, SSM_WIDTH + SSM_GROUPS * SSM_STATE:].astype(BF16)
    lane = lax.broadcasted_iota(jnp.int32, (1, LANES), 1)
    a_neg = jnp.where(lane < SSM_HEADS, -jnp.exp(alog_ref[...]), 0.0)
    dt = dt_ref[...]
    dec = jnp.exp(dt * a_neg)
    xdt = xs * _lane_bcast_pairs(dt, HEAD_PAIRS)
    rowid = lax.broadcasted_iota(jnp.int32, (bs, LANES), 0)
    row_lo = lax.broadcasted_iota(jnp.int32, (LANES, SSM_STATE), 0) < SSM_HEAD_DIM
    ys = []
    for j in range(HEAD_PAIRS):
        g = j // (HEAD_PAIRS // SSM_GROUPS)
        sl = slice(j * LANES, (j + 1) * LANES)
        b_g = bm[:, g * SSM_STATE:(g + 1) * SSM_STATE]
        c_g = cm[:, g * SSM_STATE:(g + 1) * SSM_STATE]
        xdt_p = xdt[:, sl]
        y_p = jnp.zeros((bs, LANES), F32)
        for i in range(bs):
            xi = jnp.where(rowid == i, xdt_p, 0.0).T.astype(BF16)
            d_a = dec[i:i + 1, 2 * j:2 * j + 1]
            d_b = dec[i:i + 1, 2 * j + 1:2 * j + 2]
            decay = jnp.where(row_lo, jnp.broadcast_to(d_a, row_lo.shape), jnp.broadcast_to(d_b, row_lo.shape))
            new = decay * h0_ref[i, j] + _dot(xi, b_g)
            h1_ref[i, j] = new
            y_p = y_p + jnp.where(rowid == i, _dot_nt(c_g, new.astype(BF16)), 0.0)
        ys.append(y_p + dskip_ref[:, sl] * xs[:, sl])
    y = jnp.concatenate(ys, axis=1)
    gated = y * _silu(z_ref[...])
    gw = SSM_WIDTH // SSM_GROUPS
    outs = []
    for g in range(SSM_GROUPS):
        gg = gated[:, g * gw:(g + 1) * gw]
        outs.append(gg * lax.rsqrt(jnp.mean(gg * gg, axis=-1, keepdims=True) + EPS))
    y_ref[...] = (jnp.concatenate(outs, axis=1) * nw_ref[...]).astype(BF16)


def _ssd_sample(xbc, z, dt, cprev_t, h0, w, bs):
    n = xbc.shape[0]
    tok = lambda i: (i, 0)
    const = lambda i: (0, 0)
    full = lambda a: pl.BlockSpec(a.shape, const)
    return pl.pallas_call(
        _ssd_sample_kernel,
        out_shape=(jax.ShapeDtypeStruct((n, SSM_WIDTH), BF16),
                   jax.ShapeDtypeStruct((CONV_K - 1, n, CONV_DIM), F32),
                   jax.ShapeDtypeStruct((n, HEAD_PAIRS, LANES, SSM_STATE), F32)),
        grid=(n // bs,),
        in_specs=[
            pl.BlockSpec((bs, CONV_DIM), tok), pl.BlockSpec((bs, SSM_WIDTH), tok),
            pl.BlockSpec((bs, LANES), tok),
            pl.BlockSpec((CONV_K - 1, bs, CONV_DIM), lambda i: (0, i, 0)),
            pl.BlockSpec((bs, HEAD_PAIRS, LANES, SSM_STATE), lambda i: (i, 0, 0, 0)),
            full(w["conv_w"]), full(w["conv_b"]), full(w["a_log"]),
            full(w["d_skip"]), full(w["ssm_norm"]),
        ],
        out_specs=(pl.BlockSpec((bs, SSM_WIDTH), tok),
                   pl.BlockSpec((CONV_K - 1, bs, CONV_DIM), lambda i: (0, i, 0)),
                   pl.BlockSpec((bs, HEAD_PAIRS, LANES, SSM_STATE), lambda i: (i, 0, 0, 0))),
        compiler_params=pltpu.CompilerParams(dimension_semantics=("parallel",),
                                             vmem_limit_bytes=VMEM_LIMIT),
        name="ssd_sample",
    )(xbc, z, dt, cprev_t, h0, w["conv_w"], w["conv_b"], w["a_log"], w["d_skip"], w["ssm_norm"])


def _pack_bf16_pair(v):
    c = v.shape[1] // 2
    hi = lax.bitcast_convert_type(v[:, :c].astype(BF16).astype(F32), jnp.uint32)
    lo = lax.bitcast_convert_type(v[:, c:].astype(BF16).astype(F32), jnp.uint32)
    return hi | (lo >> 16)


def _unpack_bf16_pair(word):
    a = lax.bitcast_convert_type(word & jnp.uint32(0xFFFF0000), F32)
    b = lax.bitcast_convert_type(word << 16, F32)
    return a, b


def _outproj_router_kernel(x_ref, a_ref, s_ref, wo_a_ref, wo_s_ref, n2_ref, wr_ref, br_ref, tri_ref,
                           h_ref, t_ref, route_ref, cnt_ref, carry_ref):
    @pl.when(pl.program_id(0) == 0)
    def _():
        carry_ref[...] = jnp.zeros(carry_ref.shape, F32)

    h = x_ref[...] + _dot(a_ref[...].astype(BF16), wo_a_ref[...]) + _dot(s_ref[...].astype(BF16), wo_s_ref[...])
    h_ref[...] = h
    ms = jnp.mean(h * h, axis=-1, keepdims=True)
    t = h * lax.rsqrt(ms + EPS) * n2_ref[...]
    t_hi, t_lo = _split2(t)
    t_ref[...] = _pack_bf16_pair(t)
    both = _dot(t_hi, wr_ref[...])
    logits = both[:, :LANES] + both[:, LANES:] + _dot(t_lo, wr_ref[:, :LANES]) + br_ref[...]
    lane = lax.broadcasted_iota(jnp.int32, logits.shape, 1)
    lane_f = lane.astype(F32)
    big = float(LANES)
    ninf = -jnp.inf
    glog = jnp.where(lane < N_EGROUPS, logits, ninf)
    gmax = jnp.max(glog, axis=-1, keepdims=True)
    g_top = 1.0 / jnp.sum(jnp.exp(glog - gmax), axis=-1, keepdims=True)
    g_idx = jnp.min(jnp.where(glog == gmax, lane_f, big), axis=-1, keepdims=True)
    e_of_lane = lane - EXPERT_LANE0
    in_grp = jnp.logical_and(jnp.logical_and(e_of_lane >= 0, e_of_lane < N_EXPERTS),
                             (e_of_lane // EXP_PER_GROUP).astype(F32) == g_idx)
    ml = jnp.where(in_grp, logits, ninf)
    m1 = jnp.max(ml, axis=-1, keepdims=True)
    i1 = jnp.min(jnp.where(ml == m1, lane_f, big), axis=-1, keepdims=True)
    ml2 = jnp.where(lane_f == i1, ninf, ml)
    m2 = jnp.max(ml2, axis=-1, keepdims=True)
    i2 = jnp.min(jnp.where(ml2 == m2, lane_f, big), axis=-1, keepdims=True)
    r = jnp.exp(m2 - m1)
    w1 = g_top / (1.0 + r)
    w2 = g_top * r / (1.0 + r)
    is1 = lane_f == i1
    is2 = lane_f == i2
    onehot = jnp.where(jnp.logical_or(is1, is2), 1.0, 0.0)
    onehot_bf = onehot.astype(BF16)
    tm = onehot.shape[0]
    cb = tri_ref.shape[0]
    carry = carry_ref[0:1, :]
    cums = []
    for blk in range(tm // cb):
        c = _dot(tri_ref[...], onehot_bf[blk * cb:(blk + 1) * cb, :]) + carry
        carry = c[cb - 1:cb, :]
        cums.append(c)
    before = jnp.concatenate(cums, axis=0) - onehot
    rank1 = jnp.sum(jnp.where(is1, before, 0.0), axis=-1, keepdims=True)
    rank2 = jnp.sum(jnp.where(is2, before, 0.0), axis=-1, keepdims=True)
    carry_ref[0:1, :] = carry
    cnt_ref[...] = jnp.broadcast_to(carry, cnt_ref.shape)
    fields = (i1 - EXPERT_LANE0, i2 - EXPERT_LANE0, w1, w2, rank1, rank2)
    route = jnp.zeros(logits.shape, F32)
    for pos, val in enumerate(fields):
        route = jnp.where(lane == pos, val, route)
    route_ref[...] = route


def _outproj_router(x2d, attn, ssm, w, tm):
    t = x2d.shape[0]
    tok = lambda i: (i, 0)
    const = lambda i: (0, 0)
    full = lambda a: pl.BlockSpec(a.shape, const)
    idx = jnp.arange(min(tm, COUNT_BLOCK))
    tri = (idx[None, :] <= idx[:, None]).astype(BF16)
    return pl.pallas_call(
        _outproj_router_kernel,
        out_shape=(jax.ShapeDtypeStruct((t, D_MODEL), F32), jax.ShapeDtypeStruct((t, D_MODEL // 2), jnp.uint32),
                   jax.ShapeDtypeStruct((t, LANES), F32), jax.ShapeDtypeStruct((8, LANES), F32)),
        grid=(t // tm,),
        in_specs=[
            pl.BlockSpec((tm, D_MODEL), tok), pl.BlockSpec((tm, ATTN_WIDTH), tok),
            pl.BlockSpec((tm, SSM_WIDTH), tok),
            full(w["wo_a"]), full(w["wo_s"]), full(w["norm2"]), full(w["wr"]),
            full(w["br"]), full(tri),
        ],
        out_specs=(pl.BlockSpec((tm, D_MODEL), tok), pl.BlockSpec((tm, D_MODEL // 2), tok),
                   pl.BlockSpec((tm, LANES), tok), pl.BlockSpec((8, LANES), const)),
        scratch_shapes=[pltpu.VMEM((8, LANES), F32)],
        compiler_params=pltpu.CompilerParams(dimension_semantics=("arbitrary",),
                                             vmem_limit_bytes=VMEM_LIMIT),
        name="outproj_router",
    )(x2d, attn, ssm, w["wo_a"], w["wo_s"], w["norm2"], w["wr"], w["br"], tri)


def _expert_hidden(t_a, t_b, w1):
    half = D_MODEL // 2
    gu = _dot(t_a, w1[:half]) + _dot(t_b, w1[half:])
    return _silu(gu[:, :EXPERT_FF]) * gu[:, EXPERT_FF:]


def _moe_dense_kernel(h_ref, t_ref, route_ref, w1_ref, w2_ref, y_ref, acc_ref):
    t_a, t_b = _unpack_bf16_pair(t_ref[...])
    t_a, t_b = t_a.astype(BF16), t_b.astype(BF16)
    route = route_ref[...]
    e1, e2, g1, g2 = route[:, 0:1], route[:, 1:2], route[:, 2:3], route[:, 3:4]
    acc_ref[...] = h_ref[...]
    for e in range(N_EXPERTS):
        hid = _expert_hidden(t_a, t_b, w1_ref[e])
        c_e = jnp.where(e1 == float(e), g1, 0.0) + jnp.where(e2 == float(e), g2, 0.0)
        acc_ref[...] += _dot((hid * c_e).astype(BF16), w2_ref[e])
    y_ref[...] = acc_ref[...]


def _moe_dense(h, t, route, w, tm):
    n = h.shape[0]
    tok = lambda i: (i, 0)
    resident = lambda a: pl.BlockSpec(a.shape, lambda i: (0, 0, 0), pipeline_mode=pl.Buffered(1))
    return pl.pallas_call(
        _moe_dense_kernel,
        out_shape=jax.ShapeDtypeStruct((n, D_MODEL), F32),
        grid=(n // tm,),
        in_specs=[pl.BlockSpec((tm, D_MODEL), tok), pl.BlockSpec((tm, D_MODEL // 2), tok),
                  pl.BlockSpec((tm, LANES), tok), resident(w["w1e"]), resident(w["w2e"])],
        out_specs=pl.BlockSpec((tm, D_MODEL), tok),
        scratch_shapes=[pltpu.VMEM((tm, D_MODEL), F32)],
        compiler_params=pltpu.CompilerParams(dimension_semantics=("parallel",),
                                             vmem_limit_bytes=VMEM_LIMIT),
        name="moe_dense",
    )(h, t, route, w["w1e"], w["w2e"])


def _sc_scatter_rows(src, pos1, pos2, n_out):
    t, width = src.shape
    rows_per_worker = t // SC_WORKERS
    n_chunks = rows_per_worker // SC_CHUNK
    assert t == SC_WORKERS * SC_CHUNK * n_chunks
    mesh = plsc.VectorSubcoreMesh(core_axis_name="c", subcore_axis_name="s")

    @functools.partial(
        pl.kernel, mesh=mesh,
        out_type=jax.ShapeDtypeStruct((n_out, width), src.dtype),
        scratch_types=[pltpu.VMEM((SC_CHUNK,), jnp.int32), pltpu.VMEM((SC_CHUNK,), jnp.int32),
                       pltpu.VMEM((SC_CHUNK, width), src.dtype), pltpu.SemaphoreType.DMA],
    )
    def scatter_kernel(src_hbm, p1_hbm, p2_hbm, out_hbm, i1_v, i2_v, rows_v, sem):
        wid = lax.axis_index("s") * SC_CORES + lax.axis_index("c")
        base = wid * rows_per_worker

        @pl.loop(0, n_chunks)
        def _(c):
            off = pl.multiple_of(base + c * SC_CHUNK, 8)
            pltpu.sync_copy(p1_hbm.at[pl.ds(off, SC_CHUNK)], i1_v)
            pltpu.sync_copy(p2_hbm.at[pl.ds(off, SC_CHUNK)], i2_v)
            pltpu.sync_copy(src_hbm.at[pl.ds(off, SC_CHUNK)], rows_v)
            pltpu.async_copy(rows_v, out_hbm.at[i1_v], sem).wait()
            pltpu.async_copy(rows_v, out_hbm.at[i2_v], sem).wait()

    return scatter_kernel(src, pos1, pos2)


def _sc_gather_rows(table, idx):
    n, width = idx.shape[0], table.shape[1]
    rows_per_worker = n // SC_WORKERS
    n_chunks = rows_per_worker // SC_CHUNK
    assert n == SC_WORKERS * SC_CHUNK * n_chunks and n_chunks % 2 == 0
    mesh = plsc.VectorSubcoreMesh(core_axis_name="c", subcore_axis_name="s")

    @functools.partial(
        pl.kernel, mesh=mesh,
        out_type=jax.ShapeDtypeStruct((n, width), table.dtype),
        scratch_types=[pltpu.VMEM((2, SC_CHUNK), jnp.int32), pltpu.VMEM((2, SC_CHUNK, width), table.dtype),
                       pltpu.SemaphoreType.DMA, pltpu.SemaphoreType.DMA,
                       pltpu.SemaphoreType.DMA, pltpu.SemaphoreType.DMA],
    )
    def gather_kernel(table_hbm, idx_hbm, out_hbm, idx_v, rows_v, g0, g1, w0, w1):
        wid = lax.axis_index("s") * SC_CORES + lax.axis_index("c")
        base = wid * rows_per_worker
        gsem = (g0, g1)
        wsem = (w0, w1)

        def gather_copy(slot):
            return pltpu.make_async_copy(table_hbm.at[idx_v.at[slot]], rows_v.at[slot], gsem[slot])

        def write_copy(c, slot):
            off = pl.multiple_of(base + c * SC_CHUNK, 8)
            return pltpu.make_async_copy(rows_v.at[slot], out_hbm.at[pl.ds(off, SC_CHUNK)], wsem[slot])

        def start_gather(c, slot):
            off = pl.multiple_of(base + c * SC_CHUNK, 8)
            pltpu.sync_copy(idx_hbm.at[pl.ds(off, SC_CHUNK)], idx_v.at[slot])
            gather_copy(slot).start()

        start_gather(0, 0)

        @pl.loop(0, n_chunks, step=2)
        def _(c):
            @pl.when(c > 0)
            def _():
                write_copy(c - 1, 1).wait()

            start_gather(c + 1, 1)
            gather_copy(0).wait()
            write_copy(c, 0).start()
            gather_copy(1).wait()
            write_copy(c + 1, 1).start()
            write_copy(c, 0).wait()

            @pl.when(c + 2 < n_chunks)
            def _():
                start_gather(c + 2, 0)

        write_copy(n_chunks - 1, 1).wait()

    return gather_kernel(table, idx)


def _moe_grouped_kernel(te_ref, nt_ref, x_ref, w1_ref, w2_ref, o_ref):
    @pl.when(pl.program_id(0) < nt_ref[0])
    def _():
        t_a, t_b = _unpack_bf16_pair(x_ref[...])
        hid = _expert_hidden(t_a.astype(BF16), t_b.astype(BF16), w1_ref[0])
        o_ref[...] = _pack_bf16_pair(_dot(hid.astype(BF16), w2_ref[0]))


def _moe_grouped(xs, tile_expert, n_tiles, w):
    rows = xs.shape[0]
    row = lambda i, te, nt: (i, 0)
    by_expert = lambda i, te, nt: (te[i], 0, 0)
    return pl.pallas_call(
        _moe_grouped_kernel,
        out_shape=jax.ShapeDtypeStruct((rows, D_MODEL // 2), jnp.uint32),
        grid_spec=pltpu.PrefetchScalarGridSpec(
            num_scalar_prefetch=2,
            grid=(rows // MOE_TILE,),
            in_specs=[pl.BlockSpec((MOE_TILE, D_MODEL // 2), row),
                      pl.BlockSpec((1, D_MODEL, 2 * EXPERT_FF), by_expert),
                      pl.BlockSpec((1, EXPERT_FF, D_MODEL), by_expert)],
            out_specs=pl.BlockSpec((MOE_TILE, D_MODEL // 2), row),
        ),
        compiler_params=pltpu.CompilerParams(dimension_semantics=("arbitrary",),
                                             vmem_limit_bytes=VMEM_LIMIT),
        name="moe_grouped",
    )(tile_expert, n_tiles, xs, w["w1e"], w["w2e"])


def _moe_combine_kernel(h_ref, z1_ref, z2_ref, route_ref, y_ref):
    route = route_ref[...]
    g1, g2 = route[:, 2:3], route[:, 3:4]
    half = D_MODEL // 2
    a1, b1 = _unpack_bf16_pair(z1_ref[...])
    a2, b2 = _unpack_bf16_pair(z2_ref[...])
    y_ref[:, :half] = h_ref[:, :half] + g1 * a1 + g2 * a2
    y_ref[:, half:] = h_ref[:, half:] + g1 * b1 + g2 * b2


def _moe_combine(h, z, route, tm):
    t = h.shape[0]
    nb = t // tm
    tok = lambda i: (i, 0)
    return pl.pallas_call(
        _moe_combine_kernel,
        out_shape=jax.ShapeDtypeStruct((t, D_MODEL), F32),
        grid=(nb,),
        in_specs=[pl.BlockSpec((tm, D_MODEL), tok), pl.BlockSpec((tm, D_MODEL // 2), tok),
                  pl.BlockSpec((tm, D_MODEL // 2), lambda i: (i + nb, 0)), pl.BlockSpec((tm, LANES), tok)],
        out_specs=pl.BlockSpec((tm, D_MODEL), tok),
        compiler_params=pltpu.CompilerParams(dimension_semantics=("parallel",),
                                             vmem_limit_bytes=VMEM_LIMIT),
        name="moe_combine",
    )(h, z, z, route)


def _route_pos_kernel(route_ref, cnt_ref, upper_ref, pos_ref):
    tm = route_ref.shape[0]
    cnt = cnt_ref[...]
    padded = jnp.floor((cnt + float(MOE_TILE - 1)) * (1.0 / MOE_TILE)) * float(MOE_TILE)
    p_hi, p_mid, p_lo = _split3(padded)
    upper = upper_ref[...]
    starts = (_dot(p_hi, upper) + _dot(p_mid, upper) + _dot(p_lo, upper))[0:1, :]
    route = route_ref[...]
    lane_f = lax.broadcasted_iota(jnp.int32, (tm, LANES), 1).astype(F32)
    diag = (lax.broadcasted_iota(jnp.int32, (LANES, LANES), 0)
            == lax.broadcasted_iota(jnp.int32, (LANES, LANES), 1))
    for k in range(2):
        e_lane = route[:, k:k + 1] + float(EXPERT_LANE0)
        pos = jnp.sum(jnp.where(lane_f == e_lane, starts, 0.0), axis=-1, keepdims=True) + route[:, 4 + k:5 + k]
        for r in range(tm // LANES):
            col = pos[r * LANES:(r + 1) * LANES, :]
            row = jnp.sum(jnp.where(diag, col, 0.0), axis=0, keepdims=True)
            pos_ref[k, r:r + 1, :] = row.astype(jnp.int32)


def _route_positions(route, counts, tm):
    t = route.shape[0]
    idx = jnp.arange(LANES)
    upper = (idx[:, None] < idx[None, :]).astype(BF16)
    return pl.pallas_call(
        _route_pos_kernel,
        out_shape=jax.ShapeDtypeStruct((2, t // LANES, LANES), jnp.int32),
        grid=(t // tm,),
        in_specs=[pl.BlockSpec((tm, LANES), lambda i: (i, 0)), pl.BlockSpec((8, LANES), lambda i: (0, 0)),
                  pl.BlockSpec((LANES, LANES), lambda i: (0, 0))],
        out_specs=pl.BlockSpec((2, tm // LANES, LANES), lambda i: (0, i, 0)),
        compiler_params=pltpu.CompilerParams(dimension_semantics=("parallel",),
                                             vmem_limit_bytes=VMEM_LIMIT),
        name="route_positions",
    )(route, counts, upper)


def _moe_routed(h, t_packed, route, counts, w, tm):
    t = h.shape[0]
    pos = _route_positions(route, counts, min(8 * LANES, t))
    pos1 = pos[0].reshape(t)
    pos2 = pos[1].reshape(t)
    cnt = counts[0, EXPERT_LANE0:EXPERT_LANE0 + N_EXPERTS].astype(jnp.int32)
    padded = (cnt + MOE_TILE - 1) // MOE_TILE * MOE_TILE
    ends = jnp.cumsum(padded)
    n_rows = 2 * t + N_EXPERTS * MOE_TILE
    n_tiles = ends[N_EXPERTS - 1] // MOE_TILE
    tile_start = jnp.arange(n_rows // MOE_TILE, dtype=jnp.int32) * MOE_TILE
    tile_start = jnp.minimum(tile_start, ends[N_EXPERTS - 1] - MOE_TILE)
    tile_expert = jnp.sum((tile_start[:, None] >= ends[None, :]).astype(jnp.int32), axis=1)
    xs = _sc_scatter_rows(t_packed, pos1, pos2, n_rows)
    out = _moe_grouped(xs, tile_expert, n_tiles.reshape(1), w)
    z = _sc_gather_rows(out, pos.reshape(2 * t))
    return _moe_combine(h, z, route, tm)


def _pad_lanes(a, width=LANES):
    return jnp.pad(a, ((0, 0), (0, width - a.shape[1])))


def _prep_weights(norm1, w_in, q_norm, k_norm, conv_w, conv_b, dt_bias, a_log, d_skip, ssm_norm, w_out,
                  norm2, w_grp, b_grp, w_exp, b_exp, w_gate, w_up, w_down):
    w = {}
    w["norm1"] = norm1.reshape(1, D_MODEL)
    w["wqk"] = w_in[:, :K_END].astype(BF16)
    w["wv"] = w_in[:, K_END:V_END].astype(BF16)
    w["wz"] = w_in[:, V_END:Z_END].astype(BF16)
    w["wxbc"] = w_in[:, Z_END:XBC_END].astype(BF16)
    w["wdt"] = _pad_lanes(w_in[:, XBC_END:]).astype(BF16)
    w["qkn"] = jnp.concatenate([jnp.tile(q_norm, N_HEADS), jnp.tile(k_norm, KV_HEADS)]).reshape(1, QK_WIDTH)
    head_of_col = jnp.arange(QK_WIDTH) // HEAD_DIM
    red = (head_of_col[:, None] == jnp.arange(LANES)[None, :])
    w["red"] = red.astype(BF16)
    w["exp"] = red.T.astype(BF16)
    w["conv_w"] = conv_w
    w["conv_b"] = conv_b.reshape(1, CONV_DIM)
    w["dt_bias"] = _pad_lanes(dt_bias.reshape(1, SSM_HEADS))
    w["a_log"] = _pad_lanes(a_log.reshape(1, SSM_HEADS))
    w["d_skip"] = jnp.repeat(d_skip, SSM_HEAD_DIM).reshape(1, SSM_WIDTH)
    w["ssm_norm"] = ssm_norm.reshape(1, SSM_WIDTH)
    idx = jnp.arange(SSD_CHUNK)
    w["tri"] = (idx[None, :] <= idx[:, None]).astype(BF16)
    lane_head = jnp.arange(SSM_WIDTH) // SSM_HEAD_DIM
    w["expand"] = (jnp.arange(LANES)[:, None] == lane_head[None, :]).astype(BF16)
    w["wo_a"] = w_out[:ATTN_WIDTH].astype(BF16)
    w["wo_s"] = w_out[ATTN_WIDTH:].astype(BF16)
    w["norm2"] = norm2.reshape(1, D_MODEL)
    wr = jnp.zeros((D_MODEL, LANES), F32)
    wr = wr.at[:, :N_EGROUPS].set(w_grp).at[:, EXPERT_LANE0:EXPERT_LANE0 + N_EXPERTS].set(w_exp)
    wr_hi = wr.astype(BF16)
    w["wr"] = jnp.concatenate([wr_hi, (wr - wr_hi.astype(F32)).astype(BF16)], axis=1)
    br = jnp.zeros((1, LANES), F32)
    w["br"] = br.at[0, :N_EGROUPS].set(b_grp).at[0, EXPERT_LANE0:EXPERT_LANE0 + N_EXPERTS].set(b_exp)
    w["w1e"] = jnp.concatenate([w_gate, w_up], axis=-1).astype(BF16)
    w["w2e"] = w_down.astype(BF16)
    return w


def _rope_tables(pos):
    inv = 1.0 / (ROPE_THETA ** (jnp.arange(0, HEAD_DIM, 2, dtype=F32) / HEAD_DIM))
    ang = pos.astype(F32)[:, None] * inv[None, :]
    cos, sin = jnp.cos(ang), jnp.sin(ang)
    reps = LANES // HEAD_DIM
    return (jnp.tile(jnp.concatenate([cos, cos], axis=-1), (1, reps)),
            jnp.tile(jnp.concatenate([-sin, sin], axis=-1), (1, reps)))


def _token_tile(t):
    for tm in (1024, 512, 256, 128, 64, 32, 16):
        if t % tm == 0:
            return tm
    raise ValueError(f"token count {t} is not a multiple of 16")


def kernel(x_prompt, x_sample, cache_win_k, cache_win_v, state_conv, state_ssm, norm1, w_in, q_norm, k_norm,
           sinks, conv_w, conv_b, dt_bias, a_log, d_skip, ssm_norm, w_out, norm2, w_grp, b_grp, w_exp, b_exp,
           w_gate, w_up, w_down):
    depth = norm1.shape[0]
    assert depth == 1, "single-layer stack"
    bp, lp, _ = x_prompt.shape
    bsn, ls, _ = x_sample.shape
    assert ls == 1 and lp % WINDOW == 0 and cache_win_k.shape[2] == WINDOW
    l = 0
    w = _prep_weights(norm1[l], w_in[l], q_norm[l], k_norm[l], conv_w[l], conv_b[l], dt_bias[l], a_log[l],
                      d_skip[l], ssm_norm[l], w_out[l], norm2[l], w_grp[l], b_grp[l], w_exp[l], b_exp[l],
                      w_gate[l], w_up[l], w_down[l])
    sink = sinks[l]

    tp = bp * lp
    xp = x_prompt.reshape(tp, D_MODEL)
    tm_p = _token_tile(lp)
    cos_p, sin_p = _rope_tables(jnp.arange(lp, dtype=jnp.int32))
    q, k, v, z, xbc, dt = _inproj(xp, w, cos_p, sin_p, tm_p, lp // tm_p)
    attn = _attn_prompt(q, k, v, sink, bp, lp)
    ssm, st_p = _ssd_prompt(xbc, z, dt, w, bp, lp)
    h, t, route, counts = _outproj_router(xp, attn, ssm, w, tm_p)
    y_prompt = _moe_routed(h, t, route, counts, w, tm_p).reshape(bp, lp, D_MODEL)
    k3 = k.reshape(bp, lp, KV_HEADS, HEAD_DIM)
    v3 = v.reshape(bp, lp, KV_HEADS, HEAD_DIM)
    win_k_p = k3[:, lp - WINDOW:][None]
    win_v_p = v3[:, lp - WINDOW:][None]
    conv_p = xbc.reshape(bp, lp, CONV_DIM)[:, lp - (CONV_K - 1):][None]
    ssm_p = st_p.reshape(1, bp, SSM_HEADS, SSM_HEAD_DIM, SSM_STATE)

    xs2 = x_sample.reshape(bsn, D_MODEL)
    tm_s = _token_tile(bsn)
    cos_s, sin_s = _rope_tables(jnp.full((tm_s,), PAST_LEN, jnp.int32))
    q_s, k_s, v_s, z_s, xbc_s, dt_s = _inproj(xs2, w, cos_s, sin_s, tm_s, 1)
    q4 = q_s.reshape(bsn, KV_HEADS, N_HEADS // KV_HEADS, HEAD_DIM)
    zq = jnp.zeros_like(q4[:, 0])
    qx = jnp.concatenate([jnp.concatenate([q4[:, 0], zq], axis=-1),
                          jnp.concatenate([zq, q4[:, 1]], axis=-1)], axis=1)
    qx = jnp.pad(qx, ((0, 0), (0, BF16_ROWS - N_HEADS), (0, 0)))
    sink_x = jnp.pad(jnp.broadcast_to(sink[:, None], (N_HEADS, LANES)), ((0, BF16_ROWS - N_HEADS), (0, 0)))
    kc = cache_win_k[l].reshape(bsn, WINDOW, KV_WIDTH)
    vc = cache_win_v[l].reshape(bsn, WINDOW, KV_WIDTH)
    ko, vo, attn_s = _attn_sample(qx, kc, k_s.reshape(bsn, 1, KV_WIDTH), vc, v_s.reshape(bsn, 1, KV_WIDTH),
                               sink_x, 8)
    cprev_t = jnp.transpose(state_conv[l], (1, 0, 2))
    h0 = state_ssm[l].reshape(bsn, HEAD_PAIRS, LANES, SSM_STATE)
    ssm_s, cnew_t, h1 = _ssd_sample(xbc_s, z_s, dt_s, cprev_t, h0, w, 16)
    h_s, t_s, route_s, _ = _outproj_router(xs2, attn_s, ssm_s, w, tm_s)
    y_sample = _moe_dense(h_s, t_s, route_s, w, tm_s).reshape(bsn, 1, D_MODEL)
    win_k_s = ko.reshape(1, bsn, WINDOW, KV_HEADS, HEAD_DIM)
    win_v_s = vo.reshape(1, bsn, WINDOW, KV_HEADS, HEAD_DIM)
    conv_s = jnp.transpose(cnew_t, (1, 0, 2))[None]
    ssm_s_state = h1.reshape(1, bsn, SSM_HEADS, SSM_HEAD_DIM, SSM_STATE)

    return (y_prompt, y_sample, win_k_p, win_v_p, conv_p, ssm_p, win_k_s, win_v_s, conv_s, ssm_s_state)
```

```python
import functools
import math

import jax
import jax.numpy as jnp
from jax import lax
from jax.experimental import pallas as pl
from jax.experimental.pallas import tpu as pltpu
from jax.experimental.pallas import tpu_sc as plsc

F32 = jnp.float32
BF16 = jnp.bfloat16

D_MODEL = 1024
HEAD_DIM = 64
N_HEADS = 8
KV_HEADS = 2
WINDOW = 128
ATTN_WIDTH = N_HEADS * HEAD_DIM
QK_WIDTH = ATTN_WIDTH + KV_HEADS * HEAD_DIM
KV_WIDTH = KV_HEADS * HEAD_DIM
ATTN_SCALE = HEAD_DIM ** -0.5
ROPE_THETA = 10000.0
SSM_WIDTH = 512
SSM_HEADS = 8
SSM_HEAD_DIM = 64
SSM_GROUPS = 2
SSM_STATE = 128
CONV_K = 4
CONV_HALO = 8
CONV_DIM = SSM_WIDTH + 2 * SSM_GROUPS * SSM_STATE
SSD_CHUNK = 128
N_EGROUPS = 4
EXP_PER_GROUP = 8
N_EXPERTS = 32
EXPERT_FF = 128
EPS = 1e-6
PAST_LEN = 16384

LANES = 128
BF16_ROWS = 16
HEAD_PAIRS = SSM_HEADS // 2
EXPERT_LANE0 = 32
VMEM_LIMIT = 56 * 1024 * 1024
MOE_TILE = 512
ATTN_QBLOCKS = 4
COUNT_BLOCK = 256
SSD_CHUNKS_PER_STEP = 4
ROUTER_SUBTILE = 1024
INPROJ_SUBTILE = 512
SC_CORES = 2
SC_SUBCORES = 16
SC_WORKERS = SC_CORES * SC_SUBCORES
SC_CHUNK = 64

Q_END = ATTN_WIDTH
K_END = Q_END + KV_WIDTH
V_END = K_END + KV_WIDTH
Z_END = V_END + SSM_WIDTH
XBC_END = Z_END + CONV_DIM


def _dot(a, b):
    return jnp.dot(a, b, preferred_element_type=F32)


def _dot_nt(a, b):
    return lax.dot_general(a, b, (((1,), (1,)), ((), ())), preferred_element_type=F32)


def _split2(v):
    hi = v.astype(BF16)
    lo = (v - hi.astype(F32)).astype(BF16)
    return hi, lo


def _split3(v):
    hi = v.astype(BF16)
    r = v - hi.astype(F32)
    mid = r.astype(BF16)
    lo = (r - mid.astype(F32)).astype(BF16)
    return hi, mid, lo


def _silu(x):
    return x * jax.nn.sigmoid(x)


def _softplus(x):
    return jnp.maximum(x, 0.0) + jnp.log1p(jnp.exp(-jnp.abs(x)))


def _lane_bcast_pairs(v, n_pairs):
    r = v.shape[0]
    lo = lax.broadcasted_iota(jnp.int32, (r, LANES), 1) < HEAD_DIM
    slabs = []
    for j in range(n_pairs):
        a = jnp.broadcast_to(v[:, 2 * j:2 * j + 1], (r, LANES))
        b = jnp.broadcast_to(v[:, 2 * j + 1:2 * j + 2], (r, LANES))
        slabs.append(jnp.where(lo, a, b))
    return jnp.concatenate(slabs, axis=1)


def _causal_conv_silu(x_ext, convw_ref, convb_ref):
    halo = CONV_HALO
    x_raw = x_ext[halo:, :]
    conv = convb_ref[...] + x_raw * convw_ref[CONV_K - 1:CONV_K, :]
    for j in range(CONV_K - 1):
        shifted = pltpu.roll(x_ext, CONV_K - 1 - j, axis=0)[halo:, :]
        conv = conv + shifted * convw_ref[j:j + 1, :]
    return _silu(conv)


def _inproj_kernel(x_ref, n1_ref, wqk_ref, wv_ref, wz_ref, wxbc_ref, wdt_ref, dtb_ref, qkn_ref,
                   cos_ref, sin_ref, red_ref, exp_ref,
                   q_ref, k_ref, v_ref, z_ref, xbc_ref, dt_ref):
    tm = x_ref.shape[0]
    sub = min(tm, INPROJ_SUBTILE)
    lane = lax.broadcasted_iota(jnp.int32, (sub, LANES), 1)
    first_half = (lane % HEAD_DIM) < (HEAD_DIM // 2)
    for s in range(tm // sub):
        rows = slice(s * sub, (s + 1) * sub)
        x = x_ref[rows, :]
        ms = jnp.mean(x * x, axis=-1, keepdims=True)
        xn = (x * lax.rsqrt(ms + EPS) * n1_ref[...]).astype(BF16)
        v_ref[rows, :] = _dot(xn, wv_ref[...])
        z_ref[rows, :] = _dot(xn, wz_ref[...])
        xbc_ref[rows, :] = _dot(xn, wxbc_ref[...])
        dt_ref[rows, :] = _softplus(_dot(xn, wdt_ref[...]) + dtb_ref[...])
        qk = _dot(xn, wqk_ref[...])
        ss = _dot((qk * qk).astype(BF16), red_ref[...])
        inv = lax.rsqrt(ss * (1.0 / HEAD_DIM) + EPS)
        inv_hi, inv_lo = _split2(inv)
        inv_x = _dot(inv_hi, exp_ref[...]) + _dot(inv_lo, exp_ref[...])
        qkn = qk * inv_x * qkn_ref[...]
        cos = cos_ref[rows, :]
        sin = sin_ref[rows, :]
        for c in range(QK_WIDTH // LANES):
            xc = qkn[:, c * LANES:(c + 1) * LANES]
            partner = jnp.where(first_half,
                                pltpu.roll(xc, LANES - HEAD_DIM // 2, axis=1),
                                pltpu.roll(xc, HEAD_DIM // 2, axis=1))
            rot = xc * cos + partner * sin
            if c < ATTN_WIDTH // LANES:
                q_ref[rows, c * LANES:(c + 1) * LANES] = (rot * ATTN_SCALE).astype(BF16)
            else:
                k_ref[rows, :] = rot


def _inproj(x2d, w, cos_tab, sin_tab, tm, n_pos_blocks):
    t = x2d.shape[0]
    grid = (t // tm,)
    tok = lambda i: (i, 0)
    const = lambda i: (0, 0)
    pos = lambda i: (i % n_pos_blocks, 0)
    full = lambda a: pl.BlockSpec(a.shape, const)
    rows = lambda width, dtype: (jax.ShapeDtypeStruct((t, width), dtype), pl.BlockSpec((tm, width), tok))
    outs = [rows(ATTN_WIDTH, BF16), rows(KV_WIDTH, F32), rows(KV_WIDTH, F32), rows(SSM_WIDTH, F32)]
    operands = [x2d, w["norm1"], w["wqk"], w["wv"], w["wz"], w["wxbc"], w["wdt"], w["dt_bias"], w["qkn"],
                cos_tab, sin_tab, w["red"], w["exp"]]
    in_specs = [
        pl.BlockSpec((tm, D_MODEL), tok),
        full(w["norm1"]), full(w["wqk"]), full(w["wv"]), full(w["wz"]), full(w["wxbc"]),
        full(w["wdt"]), full(w["dt_bias"]), full(w["qkn"]),
        pl.BlockSpec((tm, LANES), pos), pl.BlockSpec((tm, LANES), pos),
        full(w["red"]), full(w["exp"]),
    ]
    outs += [rows(CONV_DIM, F32), rows(LANES, F32)]
    return pl.pallas_call(
        _inproj_kernel,
        out_shape=tuple(o[0] for o in outs),
        grid=grid,
        in_specs=in_specs,
        out_specs=tuple(o[1] for o in outs),
        compiler_params=pltpu.CompilerParams(dimension_semantics=("parallel",),
                                             vmem_limit_bytes=VMEM_LIMIT),
        name="inproj",
    )(*operands)


def _pair_rhs(blk, g):
    lo = lax.broadcasted_iota(jnp.int32, blk.shape, 1) < HEAD_DIM
    swapped = pltpu.roll(blk, HEAD_DIM, axis=1)
    if g == 0:
        top = jnp.where(lo, blk, 0.0)
        bot = jnp.where(lo, 0.0, swapped)
    else:
        top = jnp.where(lo, swapped, 0.0)
        bot = jnp.where(lo, 0.0, blk)
    return jnp.concatenate([top, bot], axis=0).astype(BF16)


def _attn_kernel(sink_ref, q_ref, kc_ref, kp_ref, vc_ref, vp_ref, o_ref):
    blk = WINDOW
    n_sub = q_ref.shape[0] // blk
    first_step = pl.program_id(1) == 0
    qi = lax.broadcasted_iota(jnp.int32, (blk, 2 * blk), 0)
    kj = lax.broadcasted_iota(jnp.int32, (blk, 2 * blk), 1) % blk
    cur_ok = kj <= qi
    seq_start_ok = jnp.logical_or(cur_ok, jnp.logical_not(first_step))
    lo = lax.broadcasted_iota(jnp.int32, (blk, LANES), 1) < HEAD_DIM
    k_blocks = [kp_ref[...]] + [kc_ref[u * blk:(u + 1) * blk, :] for u in range(n_sub)]
    v_blocks = [vp_ref[...]] + [vc_ref[u * blk:(u + 1) * blk, :] for u in range(n_sub)]
    for g in range(KV_HEADS):
        k2 = [_pair_rhs(b, g) for b in k_blocks]
        v2 = [_pair_rhs(b, g) for b in v_blocks]
        n_pairs = N_HEADS // KV_HEADS // 2
        for u in range(n_sub):
            rows = slice(u * blk, (u + 1) * blk)
            q_all = jnp.concatenate([q_ref[rows, (g * n_pairs + r) * LANES:(g * n_pairs + r + 1) * LANES]
                                     for r in range(n_pairs)], axis=0)
            s_all = _dot_nt(q_all, jnp.concatenate([k2[u + 1], k2[u]], axis=0))
            p_rows = []
            den_rows = []
            for r in range(n_pairs):
                pair = g * n_pairs + r
                s_cur = s_all[r * blk:(r + 1) * blk, :2 * blk]
                s_prev = s_all[r * blk:(r + 1) * blk, 2 * blk:]
                s = jnp.where(cur_ok, s_cur, s_prev)
                if u == 0:
                    s = jnp.where(seq_start_ok, s, -jnp.inf)
                ps = []
                dens = []
                for hh in range(2):
                    sink = sink_ref[2 * pair + hh]
                    sh = s[:, hh * blk:(hh + 1) * blk]
                    m = jnp.maximum(jnp.max(sh, axis=-1, keepdims=True), sink)
                    p = jnp.exp(sh - m)
                    dens.append(jnp.sum(p, axis=-1, keepdims=True) + jnp.exp(sink - m))
                    ps.append(p)
                p2 = jnp.concatenate(ps, axis=1)
                p_rows.append(jnp.concatenate([jnp.where(cur_ok, p2, 0.0), jnp.where(cur_ok, 0.0, p2)],
                                              axis=1).astype(BF16))
                den_rows.append(jnp.where(lo, dens[0], dens[1]))
            o_all = _dot(jnp.concatenate(p_rows, axis=0), jnp.concatenate([v2[u + 1], v2[u]], axis=0))
            for r in range(n_pairs):
                pair = g * n_pairs + r
                o2 = o_all[r * blk:(r + 1) * blk, :]
                o_ref[rows, pair * LANES:(pair + 1) * LANES] = (o2 / den_rows[r]).astype(BF16)


def _attn_prompt(q, k, v, sinks, batch, seq):
    n_sub = ATTN_QBLOCKS if seq % (ATTN_QBLOCKS * WINDOW) == 0 else 1
    rows = n_sub * WINDOW
    nb = seq // rows
    cur = lambda b, j, s: (b * nb + j, 0)
    prev = lambda b, j, s: (jnp.maximum((b * nb + j) * n_sub - 1, 0), 0)
    return pl.pallas_call(
        _attn_kernel,
        out_shape=jax.ShapeDtypeStruct((batch * seq, ATTN_WIDTH), BF16),
        grid_spec=pltpu.PrefetchScalarGridSpec(
            num_scalar_prefetch=1,
            grid=(batch, nb),
            in_specs=[
                pl.BlockSpec((rows, ATTN_WIDTH), cur),
                pl.BlockSpec((rows, KV_WIDTH), cur), pl.BlockSpec((WINDOW, KV_WIDTH), prev),
                pl.BlockSpec((rows, KV_WIDTH), cur), pl.BlockSpec((WINDOW, KV_WIDTH), prev),
            ],
            out_specs=pl.BlockSpec((rows, ATTN_WIDTH), cur),
        ),
        compiler_params=pltpu.CompilerParams(dimension_semantics=("parallel", "parallel"),
                                             vmem_limit_bytes=VMEM_LIMIT),
        name="attn_prompt",
    )(sinks, q, k, k, v, v)


def _ssd_kernel(xbc_ref, z_ref, dt_ref, convw_ref, convb_ref, alog_ref, dskip_ref, nw_ref,
                tri_ref, expand_ref, y_ref, st_ref, buf_ref, state_ref):
    c = pl.program_id(1)
    cl = SSD_CHUNK
    n_sub = xbc_ref.shape[0] // cl
    halo = CONV_HALO

    @pl.when(c == 0)
    def _():
        buf_ref[...] = jnp.zeros(buf_ref.shape, F32)
        state_ref[...] = jnp.zeros(state_ref.shape, F32)

    lane = lax.broadcasted_iota(jnp.int32, (1, LANES), 1)
    a_neg = jnp.where(lane < SSM_HEADS, -jnp.exp(alog_ref[...]), 0.0)
    tri = tri_ref[...]
    for u in range(n_sub):
        rows = slice(u * cl, (u + 1) * cl)
        if u == 0:
            x_ext = jnp.concatenate([buf_ref[...], xbc_ref[rows, :]], axis=0)
        else:
            x_ext = xbc_ref[u * cl - halo:(u + 1) * cl, :]
        _ssd_chunk(x_ext, z_ref[rows, :], dt_ref[rows, :], a_neg, tri, convw_ref, convb_ref, dskip_ref, nw_ref,
                   expand_ref, y_ref.at[rows, :], state_ref)
    buf_ref[...] = xbc_ref[n_sub * cl - halo:n_sub * cl, :]

    @pl.when(c == pl.num_programs(1) - 1)
    def _():
        st_ref[0] = state_ref[...]


def _ssd_chunk(x_ext, z, dt, a_neg, tri, convw_ref, convb_ref, dskip_ref, nw_ref, expand_ref, y_ref, state_ref):
    cl = SSD_CHUNK
    act = _causal_conv_silu(x_ext, convw_ref, convb_ref)
    xs = act[:, :SSM_WIDTH]
    bm = act[:, SSM_WIDTH:SSM_WIDTH + SSM_GROUPS * SSM_STATE].astype(BF16)
    cm = act[:, SSM_WIDTH + SSM_GROUPS * SSM_STATE:].astype(BF16)

    dta = dt * a_neg
    p_hi, p_mid, p_lo = _split3(dta)
    a_col = _dot(tri, p_hi) + _dot(tri, p_mid) + _dot(tri, p_lo)
    a_last = a_col[cl - 1:cl, :]
    a_row = a_col.T
    per_head = jnp.concatenate([dt, jnp.exp(a_col), jnp.exp(a_last - a_col)], axis=0)
    ph_hi, ph_lo = _split2(per_head)
    per_lane = _dot(ph_hi, expand_ref[...]) + _dot(ph_lo, expand_ref[...])
    dt_x = per_lane[:cl]
    ecol_x = per_lane[cl:2 * cl]
    dte_x = per_lane[2 * cl:]
    e_last = jnp.exp(a_last)
    xdt = xs * dt_x

    li = lax.broadcasted_iota(jnp.int32, (cl, cl), 0)
    si = lax.broadcasted_iota(jnp.int32, (cl, cl), 1)
    causal = si <= li
    lo = lax.broadcasted_iota(jnp.int32, (cl, LANES), 1) < SSM_HEAD_DIM
    row_lo = lax.broadcasted_iota(jnp.int32, (LANES, SSM_STATE), 0) < SSM_HEAD_DIM

    ys = []
    for g in range(SSM_GROUPS):
        b_g = bm[:, g * SSM_STATE:(g + 1) * SSM_STATE]
        c_g = cm[:, g * SSM_STATE:(g + 1) * SSM_STATE]
        cb = _dot_nt(c_g, b_g)
        for r in range(HEAD_PAIRS // SSM_GROUPS):
            j = g * (HEAD_PAIRS // SSM_GROUPS) + r
            sl = slice(j * LANES, (j + 1) * LANES)
            xdt_p = xdt[:, sl]
            ms = []
            for hh in range(2):
                h = 2 * j + hh
                seg = a_col[:, h:h + 1] - a_row[h:h + 1, :]
                ms.append(cb * jnp.exp(jnp.where(causal, seg, -jnp.inf)))
            m2 = jnp.concatenate(ms, axis=1).astype(BF16)
            rhs = jnp.concatenate([jnp.where(lo, xdt_p, 0.0), jnp.where(lo, 0.0, xdt_p)],
                                  axis=0).astype(BF16)
            y_diag = _dot(m2, rhs)
            st = state_ref[j]
            y_off = _dot_nt(c_g, st.astype(BF16)) * ecol_x[:, sl]
            xdt_e = (xdt_p * dte_x[:, sl]).T.astype(BF16)
            d_a = e_last[:, 2 * j:2 * j + 1]
            d_b = e_last[:, 2 * j + 1:2 * j + 2]
            decay = jnp.where(row_lo, jnp.broadcast_to(d_a, row_lo.shape), jnp.broadcast_to(d_b, row_lo.shape))
            state_ref[j] = decay * st + _dot(xdt_e, b_g)
            ys.append(y_diag + y_off + dskip_ref[:, sl] * xs[:, sl])
    y = jnp.concatenate(ys, axis=1)
    gated = y * _silu(z)
    gw = SSM_WIDTH // SSM_GROUPS
    outs = []
    for g in range(SSM_GROUPS):
        gg = gated[:, g * gw:(g + 1) * gw]
        outs.append(gg * lax.rsqrt(jnp.mean(gg * gg, axis=-1, keepdims=True) + EPS))
    y_ref[...] = (jnp.concatenate(outs, axis=1) * nw_ref[...]).astype(BF16)


def _ssd_prompt(xbc, z, dt, w, batch, seq):
    n_sub = SSD_CHUNKS_PER_STEP if seq % (SSD_CHUNKS_PER_STEP * SSD_CHUNK) == 0 else 1
    rows = n_sub * SSD_CHUNK
    nc = seq // rows
    tok = lambda b, c: (b * nc + c, 0)
    const = lambda b, c: (0, 0)
    full = lambda a: pl.BlockSpec(a.shape, const)
    return pl.pallas_call(
        _ssd_kernel,
        out_shape=(jax.ShapeDtypeStruct((batch * seq, SSM_WIDTH), BF16),
                   jax.ShapeDtypeStruct((batch, HEAD_PAIRS, LANES, SSM_STATE), F32)),
        grid=(batch, nc),
        in_specs=[
            pl.BlockSpec((rows, CONV_DIM), tok), pl.BlockSpec((rows, SSM_WIDTH), tok),
            pl.BlockSpec((rows, LANES), tok),
            full(w["conv_w"]), full(w["conv_b"]), full(w["a_log"]),
            full(w["d_skip"]), full(w["ssm_norm"]), full(w["tri"]), full(w["expand"]),
        ],
        out_specs=(pl.BlockSpec((rows, SSM_WIDTH), tok),
                   pl.BlockSpec((1, HEAD_PAIRS, LANES, SSM_STATE), lambda b, c: (b, 0, 0, 0))),
        scratch_shapes=[pltpu.VMEM((CONV_HALO, CONV_DIM), F32),
                        pltpu.VMEM((HEAD_PAIRS, LANES, SSM_STATE), F32)],
        compiler_params=pltpu.CompilerParams(dimension_semantics=("parallel", "arbitrary"),
                                             vmem_limit_bytes=VMEM_LIMIT),
        name="ssd_prompt",
    )(xbc, z, dt, w["conv_w"], w["conv_b"], w["a_log"], w["d_skip"], w["ssm_norm"], w["tri"], w["expand"])


def _attn_sample_kernel(qx_ref, kc_ref, kn_ref, vc_ref, vn_ref, sink_ref, ko_ref, vo_ref, o_ref):
    bs = qx_ref.shape[0]
    w = kc_ref.shape[1]
    sink = sink_ref[...]
    lo = lax.broadcasted_iota(jnp.int32, (1, LANES), 1) < HEAD_DIM
    for i in range(bs):
        ko_ref[i, 0:w - 1, :] = kc_ref[i, 1:w, :]
        ko_ref[i, w - 1:w, :] = kn_ref[i]
        vo_ref[i, 0:w - 1, :] = vc_ref[i, 1:w, :]
        vo_ref[i, w - 1:w, :] = vn_ref[i]
        s = _dot_nt(qx_ref[i], ko_ref[i].astype(BF16))
        m = jnp.maximum(jnp.max(s, axis=-1, keepdims=True), sink)
        p = jnp.exp(s - m)
        den = jnp.sum(p, axis=-1, keepdims=True) + jnp.exp(sink - m)
        o = _dot(p.astype(BF16), vo_ref[i].astype(BF16)) / den
        o_sw = pltpu.roll(o, HEAD_DIM, axis=1)
        for j in range(N_HEADS // 2):
            a, b = (o, o_sw) if j < N_HEADS // 4 else (o_sw, o)
            o_ref[i:i + 1, j * LANES:(j + 1) * LANES] = jnp.where(lo, a[2 * j:2 * j + 1], b[2 * j + 1:2 * j + 2])


def _attn_sample(qx, kc, kn, vc, vn, sink_x, bs):
    n, w = kc.shape[0], kc.shape[1]
    blk3 = lambda i: (i, 0, 0)
    return pl.pallas_call(
        _attn_sample_kernel,
        out_shape=(jax.ShapeDtypeStruct((n, w, KV_WIDTH), F32),
                   jax.ShapeDtypeStruct((n, w, KV_WIDTH), F32),
                   jax.ShapeDtypeStruct((n, ATTN_WIDTH), F32)),
        grid=(n // bs,),
        in_specs=[
            pl.BlockSpec((bs, BF16_ROWS, LANES), blk3),
            pl.BlockSpec((bs, w, KV_WIDTH), blk3), pl.BlockSpec((bs, 1, KV_WIDTH), blk3),
            pl.BlockSpec((bs, w, KV_WIDTH), blk3), pl.BlockSpec((bs, 1, KV_WIDTH), blk3),
            pl.BlockSpec(sink_x.shape, lambda i: (0, 0)),
        ],
        out_specs=(pl.BlockSpec((bs, w, KV_WIDTH), blk3), pl.BlockSpec((bs, w, KV_WIDTH), blk3),
                   pl.BlockSpec((bs, ATTN_WIDTH), lambda i: (i, 0))),
        compiler_params=pltpu.CompilerParams(dimension_semantics=("parallel",),
                                             vmem_limit_bytes=VMEM_LIMIT),
        name="attn_sample",
    )(qx, kc, kn, vc, vn, sink_x)


def _ssd_sample_kernel(xbc_ref, z_ref, dt_ref, cprev_ref, h0_ref, convw_ref, convb_ref, alog_ref,
                       dskip_ref, nw_ref, y_ref, cnew_ref, h1_ref):
    bs = xbc_ref.shape[0]
    x_raw = xbc_ref[...]
    conv = convb_ref[...] + x_raw * convw_ref[CONV_K - 1:CONV_K, :]
    for j in range(CONV_K - 1):
        conv = conv + cprev_ref[j] * convw_ref[j:j + 1, :]
    for j in range(CONV_K - 2):
        cnew_ref[j] = cprev_ref[j + 1]
    cnew_ref[CONV_K - 2] = x_raw
    act = _silu(conv)
    xs = act[:, :SSM_WIDTH]
    bm = act[:, SSM_WIDTH:SSM_WIDTH + SSM_GROUPS * SSM_STATE].astype(BF16)
    cm = act[:, SSM_WIDTH + SSM_GROUPS * SSM_STATE:].astype(BF16)
    lane = lax.broadcasted_iota(jnp.int32, (1, LANES), 1)
    a_neg = jnp.where(lane < SSM_HEADS, -jnp.exp(alog_ref[...]), 0.0)
    dt = dt_ref[...]
    dec = jnp.exp(dt * a_neg)
    xdt = xs * _lane_bcast_pairs(dt, HEAD_PAIRS)
    rowid = lax.broadcasted_iota(jnp.int32, (bs, LANES), 0)
    row_lo = lax.broadcasted_iota(jnp.int32, (LANES, SSM_STATE), 0) < SSM_HEAD_DIM
    ys = []
    for j in range(HEAD_PAIRS):
        g = j // (HEAD_PAIRS // SSM_GROUPS)
        sl = slice(j * LANES, (j + 1) * LANES)
        b_g = bm[:, g * SSM_STATE:(g + 1) * SSM_STATE]
        c_g = cm[:, g * SSM_STATE:(g + 1) * SSM_STATE]
        xdt_p = xdt[:, sl]
        y_p = jnp.zeros((bs, LANES), F32)
        for i in range(bs):
            xi = jnp.where(rowid == i, xdt_p, 0.0).T.astype(BF16)
            d_a = dec[i:i + 1, 2 * j:2 * j + 1]
            d_b = dec[i:i + 1, 2 * j + 1:2 * j + 2]
            decay = jnp.where(row_lo, jnp.broadcast_to(d_a, row_lo.shape), jnp.broadcast_to(d_b, row_lo.shape))
            new = decay * h0_ref[i, j] + _dot(xi, b_g)
            h1_ref[i, j] = new
            y_p = y_p + jnp.where(rowid == i, _dot_nt(c_g, new.astype(BF16)), 0.0)
        ys.append(y_p + dskip_ref[:, sl] * xs[:, sl])
    y = jnp.concatenate(ys, axis=1)
    gated = y * _silu(z_ref[...])
    gw = SSM_WIDTH // SSM_GROUPS
    outs = []
    for g in range(SSM_GROUPS):
        gg = gated[:, g * gw:(g + 1) * gw]
        outs.append(gg * lax.rsqrt(jnp.mean(gg * gg, axis=-1, keepdims=True) + EPS))
    y_ref[...] = (jnp.concatenate(outs, axis=1) * nw_ref[...]).astype(BF16)


def _ssd_sample(xbc, z, dt, cprev_t, h0, w, bs):
    n = xbc.shape[0]
    tok = lambda i: (i, 0)
    const = lambda i: (0, 0)
    full = lambda a: pl.BlockSpec(a.shape, const)
    return pl.pallas_call(
        _ssd_sample_kernel,
        out_shape=(jax.ShapeDtypeStruct((n, SSM_WIDTH), BF16),
                   jax.ShapeDtypeStruct((CONV_K - 1, n, CONV_DIM), F32),
                   jax.ShapeDtypeStruct((n, HEAD_PAIRS, LANES, SSM_STATE), F32)),
        grid=(n // bs,),
        in_specs=[
            pl.BlockSpec((bs, CONV_DIM), tok), pl.BlockSpec((bs, SSM_WIDTH), tok),
            pl.BlockSpec((bs, LANES), tok),
            pl.BlockSpec((CONV_K - 1, bs, CONV_DIM), lambda i: (0, i, 0)),
            pl.BlockSpec((bs, HEAD_PAIRS, LANES, SSM_STATE), lambda i: (i, 0, 0, 0)),
            full(w["conv_w"]), full(w["conv_b"]), full(w["a_log"]),
            full(w["d_skip"]), full(w["ssm_norm"]),
        ],
        out_specs=(pl.BlockSpec((bs, SSM_WIDTH), tok),
                   pl.BlockSpec((CONV_K - 1, bs, CONV_DIM), lambda i: (0, i, 0)),
                   pl.BlockSpec((bs, HEAD_PAIRS, LANES, SSM_STATE), lambda i: (i, 0, 0, 0))),
        compiler_params=pltpu.CompilerParams(dimension_semantics=("parallel",),
                                             vmem_limit_bytes=VMEM_LIMIT),
        name="ssd_sample",
    )(xbc, z, dt, cprev_t, h0, w["conv_w"], w["conv_b"], w["a_log"], w["d_skip"], w["ssm_norm"])


def _pack_bf16_pair(v):
    c = v.shape[1] // 2
    hi = lax.bitcast_convert_type(v[:, :c].astype(BF16).astype(F32), jnp.uint32)
    lo = lax.bitcast_convert_type(v[:, c:].astype(BF16).astype(F32), jnp.uint32)
    return hi | (lo >> 16)


def _unpack_bf16_pair(word):
    a = lax.bitcast_convert_type(word & jnp.uint32(0xFFFF0000), F32)
    b = lax.bitcast_convert_type(word << 16, F32)
    return a, b


def _outproj_router_kernel(x_ref, a_ref, s_ref, wo_a_ref, wo_s_ref, n2_ref, wr_ref, br_ref, tri_ref,
                           h_ref, t_ref, route_ref, cnt_ref, carry_ref):
    @pl.when(pl.program_id(0) == 0)
    def _():
        carry_ref[...] = jnp.zeros(carry_ref.shape, F32)

    tm = x_ref.shape[0]
    sub = min(tm, ROUTER_SUBTILE)
    carry = carry_ref[0:1, :]
    for s in range(tm // sub):
        rows = slice(s * sub, (s + 1) * sub)
        carry = _outproj_route_rows(x_ref[rows, :], a_ref[rows, :], s_ref[rows, :], wo_a_ref, wo_s_ref, n2_ref,
                                    wr_ref, br_ref, tri_ref, carry, h_ref.at[rows, :], t_ref.at[rows, :],
                                    route_ref.at[rows, :])
    carry_ref[0:1, :] = carry
    cnt_ref[...] = jnp.broadcast_to(carry, cnt_ref.shape)


def _outproj_route_rows(x, a, s, wo_a_ref, wo_s_ref, n2_ref, wr_ref, br_ref, tri_ref, carry, h_ref, t_ref, route_ref):
    h = x + _dot(a.astype(BF16), wo_a_ref[...]) + _dot(s.astype(BF16), wo_s_ref[...])
    h_ref[...] = h
    ms = jnp.mean(h * h, axis=-1, keepdims=True)
    t = h * lax.rsqrt(ms + EPS) * n2_ref[...]
    t_hi, t_lo = _split2(t)
    t_ref[...] = _pack_bf16_pair(t)
    both = _dot(t_hi, wr_ref[...])
    logits = both[:, :LANES] + both[:, LANES:] + _dot(t_lo, wr_ref[:, :LANES]) + br_ref[...]
    lane = lax.broadcasted_iota(jnp.int32, logits.shape, 1)
    lane_f = lane.astype(F32)
    big = float(LANES)
    ninf = -jnp.inf
    glog = jnp.where(lane < N_EGROUPS, logits, ninf)
    gmax = jnp.max(glog, axis=-1, keepdims=True)
    g_top = 1.0 / jnp.sum(jnp.exp(glog - gmax), axis=-1, keepdims=True)
    g_idx = jnp.min(jnp.where(glog == gmax, lane_f, big), axis=-1, keepdims=True)
    e_of_lane = lane - EXPERT_LANE0
    in_grp = jnp.logical_and(jnp.logical_and(e_of_lane >= 0, e_of_lane < N_EXPERTS),
                             (e_of_lane // EXP_PER_GROUP).astype(F32) == g_idx)
    ml = jnp.where(in_grp, logits, ninf)
    m1 = jnp.max(ml, axis=-1, keepdims=True)
    i1 = jnp.min(jnp.where(ml == m1, lane_f, big), axis=-1, keepdims=True)
    ml2 = jnp.where(lane_f == i1, ninf, ml)
    m2 = jnp.max(ml2, axis=-1, keepdims=True)
    i2 = jnp.min(jnp.where(ml2 == m2, lane_f, big), axis=-1, keepdims=True)
    r = jnp.exp(m2 - m1)
    w1 = g_top / (1.0 + r)
    w2 = g_top * r / (1.0 + r)
    is1 = lane_f == i1
    is2 = lane_f == i2
    onehot = jnp.where(jnp.logical_or(is1, is2), 1.0, 0.0)
    onehot_bf = onehot.astype(BF16)
    n_rows = onehot.shape[0]
    cb = tri_ref.shape[0]
    cums = []
    for blk in range(n_rows // cb):
        c = _dot(tri_ref[...], onehot_bf[blk * cb:(blk + 1) * cb, :]) + carry
        carry = c[cb - 1:cb, :]
        cums.append(c)
    before = jnp.concatenate(cums, axis=0) - onehot
    rank1 = jnp.sum(jnp.where(is1, before, 0.0), axis=-1, keepdims=True)
    rank2 = jnp.sum(jnp.where(is2, before, 0.0), axis=-1, keepdims=True)
    fields = (i1 - EXPERT_LANE0, i2 - EXPERT_LANE0, w1, w2, rank1, rank2)
    route = jnp.zeros(logits.shape, F32)
    for pos, val in enumerate(fields):
        route = jnp.where(lane == pos, val, route)
    route_ref[...] = route
    return carry


def _outproj_router(x2d, attn, ssm, w, tm):
    t = x2d.shape[0]
    tok = lambda i: (i, 0)
    const = lambda i: (0, 0)
    full = lambda a: pl.BlockSpec(a.shape, const)
    idx = jnp.arange(min(tm, COUNT_BLOCK))
    tri = (idx[None, :] <= idx[:, None]).astype(BF16)
    return pl.pallas_call(
        _outproj_router_kernel,
        out_shape=(jax.ShapeDtypeStruct((t, D_MODEL), F32), jax.ShapeDtypeStruct((t, D_MODEL // 2), jnp.uint32),
                   jax.ShapeDtypeStruct((t, LANES), F32), jax.ShapeDtypeStruct((8, LANES), F32)),
        grid=(t // tm,),
        in_specs=[
            pl.BlockSpec((tm, D_MODEL), tok), pl.BlockSpec((tm, ATTN_WIDTH), tok),
            pl.BlockSpec((tm, SSM_WIDTH), tok),
            full(w["wo_a"]), full(w["wo_s"]), full(w["norm2"]), full(w["wr"]),
            full(w["br"]), full(tri),
        ],
        out_specs=(pl.BlockSpec((tm, D_MODEL), tok), pl.BlockSpec((tm, D_MODEL // 2), tok),
                   pl.BlockSpec((tm, LANES), tok), pl.BlockSpec((8, LANES), const)),
        scratch_shapes=[pltpu.VMEM((8, LANES), F32)],
        compiler_params=pltpu.CompilerParams(dimension_semantics=("arbitrary",),
                                             vmem_limit_bytes=VMEM_LIMIT),
        name="outproj_router",
    )(x2d, attn, ssm, w["wo_a"], w["wo_s"], w["norm2"], w["wr"], w["br"], tri)


def _expert_hidden(t_a, t_b, w1):
    half = D_MODEL // 2
    gu = _dot(t_a, w1[:half]) + _dot(t_b, w1[half:])
    return _silu(gu[:, :EXPERT_FF]) * gu[:, EXPERT_FF:]


def _gate_up_bf16(wg_ref, wu_ref):
    return jnp.concatenate([wg_ref[0].astype(BF16), wu_ref[0].astype(BF16)], axis=1)


def _moe_dense_kernel(h_ref, t_ref, route_ref, wg_ref, wu_ref, wd_ref, y_ref):
    e = pl.program_id(0)

    @pl.when(e == 0)
    def _():
        y_ref[...] = h_ref[...]

    t_a, t_b = _unpack_bf16_pair(t_ref[...])
    route = route_ref[...]
    e1, e2, g1, g2 = route[:, 0:1], route[:, 1:2], route[:, 2:3], route[:, 3:4]
    e_f = e.astype(F32)
    hid = _expert_hidden(t_a.astype(BF16), t_b.astype(BF16), _gate_up_bf16(wg_ref, wu_ref))
    c_e = jnp.where(e1 == e_f, g1, 0.0) + jnp.where(e2 == e_f, g2, 0.0)
    y_ref[...] += _dot((hid * c_e).astype(BF16), wd_ref[0].astype(BF16))


def _moe_dense(h, t, route, w):
    n = h.shape[0]
    whole = lambda e: (0, 0)
    by_expert = lambda e: (e, 0, 0)
    return pl.pallas_call(
        _moe_dense_kernel,
        out_shape=jax.ShapeDtypeStruct((n, D_MODEL), F32),
        grid=(N_EXPERTS,),
        in_specs=[pl.BlockSpec((n, D_MODEL), whole), pl.BlockSpec((n, D_MODEL // 2), whole),
                  pl.BlockSpec((n, LANES), whole),
                  pl.BlockSpec((1, D_MODEL, EXPERT_FF), by_expert), pl.BlockSpec((1, D_MODEL, EXPERT_FF), by_expert),
                  pl.BlockSpec((1, EXPERT_FF, D_MODEL), by_expert)],
        out_specs=pl.BlockSpec((n, D_MODEL), whole),
        compiler_params=pltpu.CompilerParams(dimension_semantics=("arbitrary",),
                                             vmem_limit_bytes=VMEM_LIMIT),
        name="moe_dense",
    )(h, t, route, w["w_gate"], w["w_up"], w["w_down"])


def _sc_scatter_rows(src, pos1, pos2, n_out):
    t, width = src.shape
    rows_per_worker = t // SC_WORKERS
    n_chunks = rows_per_worker // SC_CHUNK
    assert t == SC_WORKERS * SC_CHUNK * n_chunks
    mesh = plsc.VectorSubcoreMesh(core_axis_name="c", subcore_axis_name="s")

    @functools.partial(
        pl.kernel, mesh=mesh,
        out_type=jax.ShapeDtypeStruct((n_out, width), src.dtype),
        scratch_types=[pltpu.VMEM((SC_CHUNK,), jnp.int32), pltpu.VMEM((SC_CHUNK,), jnp.int32),
                       pltpu.VMEM((SC_CHUNK, width), src.dtype), pltpu.SemaphoreType.DMA],
    )
    def scatter_kernel(src_hbm, p1_hbm, p2_hbm, out_hbm, i1_v, i2_v, rows_v, sem):
        wid = lax.axis_index("s") * SC_CORES + lax.axis_index("c")
        base = wid * rows_per_worker

        @pl.loop(0, n_chunks)
        def _(c):
            off = pl.multiple_of(base + c * SC_CHUNK, 8)
            pltpu.sync_copy(p1_hbm.at[pl.ds(off, SC_CHUNK)], i1_v)
            pltpu.sync_copy(p2_hbm.at[pl.ds(off, SC_CHUNK)], i2_v)
            pltpu.sync_copy(src_hbm.at[pl.ds(off, SC_CHUNK)], rows_v)
            pltpu.async_copy(rows_v, out_hbm.at[i1_v], sem).wait()
            pltpu.async_copy(rows_v, out_hbm.at[i2_v], sem).wait()

    return scatter_kernel(src, pos1, pos2)


def _sc_gather_rows(table, idx):
    n, width = idx.shape[0], table.shape[1]
    rows_per_worker = n // SC_WORKERS
    n_chunks = rows_per_worker // SC_CHUNK
    assert n == SC_WORKERS * SC_CHUNK * n_chunks and n_chunks % 2 == 0
    mesh = plsc.VectorSubcoreMesh(core_axis_name="c", subcore_axis_name="s")

    @functools.partial(
        pl.kernel, mesh=mesh,
        out_type=jax.ShapeDtypeStruct((n, width), table.dtype),
        scratch_types=[pltpu.VMEM((2, SC_CHUNK), jnp.int32), pltpu.VMEM((2, SC_CHUNK, width), table.dtype),
                       pltpu.SemaphoreType.DMA, pltpu.SemaphoreType.DMA,
                       pltpu.SemaphoreType.DMA, pltpu.SemaphoreType.DMA],
    )
    def gather_kernel(table_hbm, idx_hbm, out_hbm, idx_v, rows_v, g0, g1, w0, w1):
        wid = lax.axis_index("s") * SC_CORES + lax.axis_index("c")
        base = wid * rows_per_worker
        gsem = (g0, g1)
        wsem = (w0, w1)

        def gather_copy(slot):
            return pltpu.make_async_copy(table_hbm.at[idx_v.at[slot]], rows_v.at[slot], gsem[slot])

        def write_copy(c, slot):
            off = pl.multiple_of(base + c * SC_CHUNK, 8)
            return pltpu.make_async_copy(rows_v.at[slot], out_hbm.at[pl.ds(off, SC_CHUNK)], wsem[slot])

        def start_gather(c, slot):
            off = pl.multiple_of(base + c * SC_CHUNK, 8)
            pltpu.sync_copy(idx_hbm.at[pl.ds(off, SC_CHUNK)], idx_v.at[slot])
            gather_copy(slot).start()

        start_gather(0, 0)

        @pl.loop(0, n_chunks, step=2)
        def _(c):
            @pl.when(c > 0)
            def _():
                write_copy(c - 1, 1).wait()

            start_gather(c + 1, 1)
            gather_copy(0).wait()
            write_copy(c, 0).start()
            gather_copy(1).wait()
            write_copy(c + 1, 1).start()
            write_copy(c, 0).wait()

            @pl.when(c + 2 < n_chunks)
            def _():
                start_gather(c + 2, 0)

        write_copy(n_chunks - 1, 1).wait()

    return gather_kernel(table, idx)


def _moe_grouped_kernel(te_ref, nt_ref, x_ref, wg_ref, wu_ref, wd_ref, o_ref, w1_bf_ref, w2_bf_ref):
    i = pl.program_id(0)

    @pl.when(jnp.logical_or(i == 0, te_ref[i] != te_ref[jnp.maximum(i - 1, 0)]))
    def _():
        w1_bf_ref[...] = _gate_up_bf16(wg_ref, wu_ref)
        w2_bf_ref[...] = wd_ref[0].astype(BF16)

    @pl.when(i < nt_ref[0])
    def _():
        t_a, t_b = _unpack_bf16_pair(x_ref[...])
        hid = _expert_hidden(t_a.astype(BF16), t_b.astype(BF16), w1_bf_ref[...])
        o_ref[...] = _pack_bf16_pair(_dot(hid.astype(BF16), w2_bf_ref[...]))


def _moe_grouped(xs, tile_expert, n_tiles, w):
    rows = xs.shape[0]
    row = lambda i, te, nt: (i, 0)
    by_expert = lambda i, te, nt: (te[i], 0, 0)
    return pl.pallas_call(
        _moe_grouped_kernel,
        out_shape=jax.ShapeDtypeStruct((rows, D_MODEL // 2), jnp.uint32),
        grid_spec=pltpu.PrefetchScalarGridSpec(
            num_scalar_prefetch=2,
            grid=(rows // MOE_TILE,),
            in_specs=[pl.BlockSpec((MOE_TILE, D_MODEL // 2), row),
                      pl.BlockSpec((1, D_MODEL, EXPERT_FF), by_expert),
                      pl.BlockSpec((1, D_MODEL, EXPERT_FF), by_expert),
                      pl.BlockSpec((1, EXPERT_FF, D_MODEL), by_expert)],
            out_specs=pl.BlockSpec((MOE_TILE, D_MODEL // 2), row),
            scratch_shapes=[pltpu.VMEM((D_MODEL, 2 * EXPERT_FF), BF16), pltpu.VMEM((EXPERT_FF, D_MODEL), BF16)],
        ),
        compiler_params=pltpu.CompilerParams(dimension_semantics=("arbitrary",),
                                             vmem_limit_bytes=VMEM_LIMIT),
        name="moe_grouped",
    )(tile_expert, n_tiles, xs, w["w_gate"], w["w_up"], w["w_down"])


def _moe_combine_kernel(h_ref, z1_ref, z2_ref, route_ref, y_ref):
    route = route_ref[...]
    g1, g2 = route[:, 2:3], route[:, 3:4]
    half = D_MODEL // 2
    a1, b1 = _unpack_bf16_pair(z1_ref[...])
    a2, b2 = _unpack_bf16_pair(z2_ref[...])
    y_ref[:, :half] = h_ref[:, :half] + g1 * a1 + g2 * a2
    y_ref[:, half:] = h_ref[:, half:] + g1 * b1 + g2 * b2


def _moe_combine(h, z, route, tm):
    t = h.shape[0]
    nb = t // tm
    tok = lambda i: (i, 0)
    return pl.pallas_call(
        _moe_combine_kernel,
        out_shape=jax.ShapeDtypeStruct((t, D_MODEL), F32),
        grid=(nb,),
        in_specs=[pl.BlockSpec((tm, D_MODEL), tok), pl.BlockSpec((tm, D_MODEL // 2), tok),
                  pl.BlockSpec((tm, D_MODEL // 2), lambda i: (i + nb, 0)), pl.BlockSpec((tm, LANES), tok)],
        out_specs=pl.BlockSpec((tm, D_MODEL), tok),
        compiler_params=pltpu.CompilerParams(dimension_semantics=("parallel",),
                                             vmem_limit_bytes=VMEM_LIMIT),
        name="moe_combine",
    )(h, z, z, route)


def _route_pos_kernel(route_ref, cnt_ref, upper_ref, pos_ref):
    tm = route_ref.shape[0]
    cnt = cnt_ref[...]
    padded = jnp.floor((cnt + float(MOE_TILE - 1)) * (1.0 / MOE_TILE)) * float(MOE_TILE)
    p_hi, p_mid, p_lo = _split3(padded)
    upper = upper_ref[...]
    starts = (_dot(p_hi, upper) + _dot(p_mid, upper) + _dot(p_lo, upper))[0:1, :]
    route = route_ref[...]
    lane_f = lax.broadcasted_iota(jnp.int32, (tm, LANES), 1).astype(F32)
    diag = (lax.broadcasted_iota(jnp.int32, (LANES, LANES), 0)
            == lax.broadcasted_iota(jnp.int32, (LANES, LANES), 1))
    for k in range(2):
        e_lane = route[:, k:k + 1] + float(EXPERT_LANE0)
        pos = jnp.sum(jnp.where(lane_f == e_lane, starts, 0.0), axis=-1, keepdims=True) + route[:, 4 + k:5 + k]
        for r in range(tm // LANES):
            col = pos[r * LANES:(r + 1) * LANES, :]
            row = jnp.sum(jnp.where(diag, col, 0.0), axis=0, keepdims=True)
            pos_ref[k, r:r + 1, :] = row.astype(jnp.int32)


def _route_positions(route, counts, tm):
    t = route.shape[0]
    idx = jnp.arange(LANES)
    upper = (idx[:, None] < idx[None, :]).astype(BF16)
    return pl.pallas_call(
        _route_pos_kernel,
        out_shape=jax.ShapeDtypeStruct((2, t // LANES, LANES), jnp.int32),
        grid=(t // tm,),
        in_specs=[pl.BlockSpec((tm, LANES), lambda i: (i, 0)), pl.BlockSpec((8, LANES), lambda i: (0, 0)),
                  pl.BlockSpec((LANES, LANES), lambda i: (0, 0))],
        out_specs=pl.BlockSpec((2, tm // LANES, LANES), lambda i: (0, i, 0)),
        compiler_params=pltpu.CompilerParams(dimension_semantics=("parallel",),
                                             vmem_limit_bytes=VMEM_LIMIT),
        name="route_positions",
    )(route, counts, upper)


def _moe_routed(h, t_packed, route, counts, w, tm):
    t = h.shape[0]
    pos = _route_positions(route, counts, min(8 * LANES, t))
    pos1 = pos[0].reshape(t)
    pos2 = pos[1].reshape(t)
    cnt = counts[0, EXPERT_LANE0:EXPERT_LANE0 + N_EXPERTS].astype(jnp.int32)
    padded = (cnt + MOE_TILE - 1) // MOE_TILE * MOE_TILE
    ends = jnp.cumsum(padded)
    n_rows = 2 * t + N_EXPERTS * MOE_TILE
    n_tiles = ends[N_EXPERTS - 1] // MOE_TILE
    tile_start = jnp.arange(n_rows // MOE_TILE, dtype=jnp.int32) * MOE_TILE
    tile_start = jnp.minimum(tile_start, ends[N_EXPERTS - 1] - MOE_TILE)
    tile_expert = jnp.sum((tile_start[:, None] >= ends[None, :]).astype(jnp.int32), axis=1)
    xs = _sc_scatter_rows(t_packed, pos1, pos2, n_rows)
    out = _moe_grouped(xs, tile_expert, n_tiles.reshape(1), w)
    z = _sc_gather_rows(out, pos.reshape(2 * t))
    return _moe_combine(h, z, route, tm)


def _pad_lanes(a, width=LANES):
    return jnp.pad(a, ((0, 0), (0, width - a.shape[1])))


def _prep_weights(norm1, w_in, q_norm, k_norm, conv_w, conv_b, dt_bias, a_log, d_skip, ssm_norm, w_out,
                  norm2, w_grp, b_grp, w_exp, b_exp, w_gate, w_up, w_down):
    w = {}
    w["norm1"] = norm1.reshape(1, D_MODEL)
    w["wqk"] = w_in[:, :K_END].astype(BF16)
    w["wv"] = w_in[:, K_END:V_END].astype(BF16)
    w["wz"] = w_in[:, V_END:Z_END].astype(BF16)
    w["wxbc"] = w_in[:, Z_END:XBC_END].astype(BF16)
    w["wdt"] = _pad_lanes(w_in[:, XBC_END:]).astype(BF16)
    w["qkn"] = jnp.concatenate([jnp.tile(q_norm, N_HEADS), jnp.tile(k_norm, KV_HEADS)]).reshape(1, QK_WIDTH)
    head_of_col = jnp.arange(QK_WIDTH) // HEAD_DIM
    red = (head_of_col[:, None] == jnp.arange(LANES)[None, :])
    w["red"] = red.astype(BF16)
    w["exp"] = red.T.astype(BF16)
    w["conv_w"] = conv_w
    w["conv_b"] = conv_b.reshape(1, CONV_DIM)
    w["dt_bias"] = _pad_lanes(dt_bias.reshape(1, SSM_HEADS))
    w["a_log"] = _pad_lanes(a_log.reshape(1, SSM_HEADS))
    w["d_skip"] = jnp.repeat(d_skip, SSM_HEAD_DIM).reshape(1, SSM_WIDTH)
    w["ssm_norm"] = ssm_norm.reshape(1, SSM_WIDTH)
    idx = jnp.arange(SSD_CHUNK)
    w["tri"] = (idx[None, :] <= idx[:, None]).astype(BF16)
    lane_head = jnp.arange(SSM_WIDTH) // SSM_HEAD_DIM
    w["expand"] = (jnp.arange(LANES)[:, None] == lane_head[None, :]).astype(BF16)
    w["wo_a"] = w_out[:ATTN_WIDTH].astype(BF16)
    w["wo_s"] = w_out[ATTN_WIDTH:].astype(BF16)
    w["norm2"] = norm2.reshape(1, D_MODEL)
    wr = jnp.zeros((D_MODEL, LANES), F32)
    wr = wr.at[:, :N_EGROUPS].set(w_grp).at[:, EXPERT_LANE0:EXPERT_LANE0 + N_EXPERTS].set(w_exp)
    wr_hi = wr.astype(BF16)
    w["wr"] = jnp.concatenate([wr_hi, (wr - wr_hi.astype(F32)).astype(BF16)], axis=1)
    br = jnp.zeros((1, LANES), F32)
    w["br"] = br.at[0, :N_EGROUPS].set(b_grp).at[0, EXPERT_LANE0:EXPERT_LANE0 + N_EXPERTS].set(b_exp)
    w["w_gate"], w["w_up"], w["w_down"] = w_gate, w_up, w_down
    return w


def _rope_tables(pos):
    inv = 1.0 / (ROPE_THETA ** (jnp.arange(0, HEAD_DIM, 2, dtype=F32) / HEAD_DIM))
    ang = pos.astype(F32)[:, None] * inv[None, :]
    cos, sin = jnp.cos(ang), jnp.sin(ang)
    reps = LANES // HEAD_DIM
    return (jnp.tile(jnp.concatenate([cos, cos], axis=-1), (1, reps)),
            jnp.tile(jnp.concatenate([-sin, sin], axis=-1), (1, reps)))


def _token_tile(t):
    for tm in (1024, 512, 256, 128, 64, 32, 16):
        if t % tm == 0:
            return tm
    raise ValueError(f"token count {t} is not a multiple of 16")


def kernel(x_prompt, x_sample, cache_win_k, cache_win_v, state_conv, state_ssm, norm1, w_in, q_norm, k_norm,
           sinks, conv_w, conv_b, dt_bias, a_log, d_skip, ssm_norm, w_out, norm2, w_grp, b_grp, w_exp, b_exp,
           w_gate, w_up, w_down):
    depth = norm1.shape[0]
    assert depth == 1, "single-layer stack"
    bp, lp, _ = x_prompt.shape
    bsn, ls, _ = x_sample.shape
    assert ls == 1 and lp % WINDOW == 0 and cache_win_k.shape[2] == WINDOW
    l = 0
    w = _prep_weights(norm1[l], w_in[l], q_norm[l], k_norm[l], conv_w[l], conv_b[l], dt_bias[l], a_log[l],
                      d_skip[l], ssm_norm[l], w_out[l], norm2[l], w_grp[l], b_grp[l], w_exp[l], b_exp[l],
                      w_gate[l], w_up[l], w_down[l])
    sink = sinks[l]

    tp = bp * lp
    xp = x_prompt.reshape(tp, D_MODEL)
    tm_p = _token_tile(lp)
    cos_p, sin_p = _rope_tables(jnp.arange(lp, dtype=jnp.int32))
    q, k, v, z, xbc, dt = _inproj(xp, w, cos_p, sin_p, tm_p, lp // tm_p)
    attn = _attn_prompt(q, k, v, sink, bp, lp)
    ssm, st_p = _ssd_prompt(xbc, z, dt, w, bp, lp)
    h, t, route, counts = _outproj_router(xp, attn, ssm, w, tm_p)
    y_prompt = _moe_routed(h, t, route, counts, w, tm_p).reshape(bp, lp, D_MODEL)
    k3 = k.reshape(bp, lp, KV_HEADS, HEAD_DIM)
    v3 = v.reshape(bp, lp, KV_HEADS, HEAD_DIM)
    win_k_p = k3[:, lp - WINDOW:][None]
    win_v_p = v3[:, lp - WINDOW:][None]
    conv_p = xbc.reshape(bp, lp, CONV_DIM)[:, lp - (CONV_K - 1):][None]
    ssm_p = st_p.reshape(1, bp, SSM_HEADS, SSM_HEAD_DIM, SSM_STATE)

    xs2 = x_sample.reshape(bsn, D_MODEL)
    tm_s = _token_tile(bsn)
    cos_s, sin_s = _rope_tables(jnp.full((tm_s,), PAST_LEN, jnp.int32))
    q_s, k_s, v_s, z_s, xbc_s, dt_s = _inproj(xs2, w, cos_s, sin_s, tm_s, 1)
    q4 = q_s.reshape(bsn, KV_HEADS, N_HEADS // KV_HEADS, HEAD_DIM)
    zq = jnp.zeros_like(q4[:, 0])
    qx = jnp.concatenate([jnp.concatenate([q4[:, 0], zq], axis=-1),
                          jnp.concatenate([zq, q4[:, 1]], axis=-1)], axis=1)
    qx = jnp.pad(qx, ((0, 0), (0, BF16_ROWS - N_HEADS), (0, 0)))
    sink_x = jnp.pad(jnp.broadcast_to(sink[:, None], (N_HEADS, LANES)), ((0, BF16_ROWS - N_HEADS), (0, 0)))
    kc = cache_win_k[l].reshape(bsn, WINDOW, KV_WIDTH)
    vc = cache_win_v[l].reshape(bsn, WINDOW, KV_WIDTH)
    ko, vo, attn_s = _attn_sample(qx, kc, k_s.reshape(bsn, 1, KV_WIDTH), vc, v_s.reshape(bsn, 1, KV_WIDTH),
                               sink_x, 8)
    cprev_t = jnp.transpose(state_conv[l], (1, 0, 2))
    h0 = state_ssm[l].reshape(bsn, HEAD_PAIRS, LANES, SSM_STATE)
    ssm_s, cnew_t, h1 = _ssd_sample(xbc_s, z_s, dt_s, cprev_t, h0, w, 16)
    h_s, t_s, route_s, _ = _outproj_router(xs2, attn_s, ssm_s, w, tm_s)
    y_sample = _moe_dense(h_s, t_s, route_s, w).reshape(bsn, 1, D_MODEL)
    win_k_s = ko.reshape(1, bsn, WINDOW, KV_HEADS, HEAD_DIM)
    win_v_s = vo.reshape(1, bsn, WINDOW, KV_HEADS, HEAD_DIM)
    conv_s = jnp.transpose(cnew_t, (1, 0, 2))[None]
    ssm_s_state = h1.reshape(1, bsn, SSM_HEADS, SSM_HEAD_DIM, SSM_STATE)

    return (y_prompt, y_sample, win_k_p, win_v_p, conv_p, ssm_p, win_k_s, win_v_s, conv_s, ssm_s_state)
```

```python
import functools
import math

import jax
import jax.numpy as jnp
from jax import lax
from jax.experimental import pallas as pl
from jax.experimental.pallas import tpu as pltpu
from jax.experimental.pallas import tpu_sc as plsc

F32 = jnp.float32
BF16 = jnp.bfloat16

D_MODEL = 1024
HEAD_DIM = 64
N_HEADS = 8
KV_HEADS = 2
WINDOW = 128
ATTN_WIDTH = N_HEADS * HEAD_DIM
QK_WIDTH = ATTN_WIDTH + KV_HEADS * HEAD_DIM
KV_WIDTH = KV_HEADS * HEAD_DIM
ATTN_SCALE = HEAD_DIM ** -0.5
ROPE_THETA = 10000.0
SSM_WIDTH = 512
SSM_HEADS = 8
SSM_HEAD_DIM = 64
SSM_GROUPS = 2
SSM_STATE = 128
CONV_K = 4
CONV_HALO = 8
CONV_DIM = SSM_WIDTH + 2 * SSM_GROUPS * SSM_STATE
SSD_CHUNK = 128
N_EGROUPS = 4
EXP_PER_GROUP = 8
N_EXPERTS = 32
EXPERT_FF = 128
EPS = 1e-6
PAST_LEN = 16384

LANES = 128
BF16_ROWS = 16
HEAD_PAIRS = SSM_HEADS // 2
EXPERT_LANE0 = 32
VMEM_LIMIT = 56 * 1024 * 1024
MOE_TILE = 512
MOE_SUBTILE = 128
ATTN_QBLOCKS = 4
COUNT_BLOCK = 256
SSD_CHUNKS_PER_STEP = 4
ROUTER_SUBTILE = 1024
INPROJ_SUBTILE = 512
SC_CORES = 2
SC_SUBCORES = 16
SC_WORKERS = SC_CORES * SC_SUBCORES
SC_CHUNK = 64

Q_END = ATTN_WIDTH
K_END = Q_END + KV_WIDTH
V_END = K_END + KV_WIDTH
Z_END = V_END + SSM_WIDTH
XBC_END = Z_END + CONV_DIM


def _dot(a, b):
    return jnp.dot(a, b, preferred_element_type=F32)


def _dot_nt(a, b):
    return lax.dot_general(a, b, (((1,), (1,)), ((), ())), preferred_element_type=F32)


def _split2(v):
    hi = v.astype(BF16)
    lo = (v - hi.astype(F32)).astype(BF16)
    return hi, lo


def _split3(v):
    hi = v.astype(BF16)
    r = v - hi.astype(F32)
    mid = r.astype(BF16)
    lo = (r - mid.astype(F32)).astype(BF16)
    return hi, mid, lo


def _silu(x):
    return x * jax.nn.sigmoid(x)


def _softplus(x):
    return jnp.maximum(x, 0.0) + jnp.log1p(jnp.exp(-jnp.abs(x)))


def _lane_bcast_pairs(v, n_pairs):
    r = v.shape[0]
    lo = lax.broadcasted_iota(jnp.int32, (r, LANES), 1) < HEAD_DIM
    slabs = []
    for j in range(n_pairs):
        a = jnp.broadcast_to(v[:, 2 * j:2 * j + 1], (r, LANES))
        b = jnp.broadcast_to(v[:, 2 * j + 1:2 * j + 2], (r, LANES))
        slabs.append(jnp.where(lo, a, b))
    return jnp.concatenate(slabs, axis=1)


def _causal_conv_silu(x_ext, convw_ref, convb_ref):
    halo = CONV_HALO
    x_raw = x_ext[halo:, :]
    conv = convb_ref[...] + x_raw * convw_ref[CONV_K - 1:CONV_K, :]
    for j in range(CONV_K - 1):
        shifted = pltpu.roll(x_ext, CONV_K - 1 - j, axis=0)[halo:, :]
        conv = conv + shifted * convw_ref[j:j + 1, :]
    return _silu(conv)


def _inproj_kernel(x_ref, n1_ref, wqk_ref, wv_ref, wz_ref, wxbc_ref, wdt_ref, dtb_ref, qkn_ref,
                   cos_ref, sin_ref, red_ref, exp_ref,
                   q_ref, k_ref, v_ref, z_ref, xbc_ref, dt_ref):
    tm = x_ref.shape[0]
    sub = min(tm, INPROJ_SUBTILE)
    lane = lax.broadcasted_iota(jnp.int32, (sub, LANES), 1)
    first_half = (lane % HEAD_DIM) < (HEAD_DIM // 2)
    for s in range(tm // sub):
        rows = slice(s * sub, (s + 1) * sub)
        x = x_ref[rows, :]
        ms = jnp.mean(x * x, axis=-1, keepdims=True)
        xn = (x * lax.rsqrt(ms + EPS) * n1_ref[...]).astype(BF16)
        v_ref[rows, :] = _dot(xn, wv_ref[...])
        z_ref[rows, :] = _dot(xn, wz_ref[...])
        xbc_ref[rows, :] = _dot(xn, wxbc_ref[...])
        dt_ref[rows, :] = _softplus(_dot(xn, wdt_ref[...]) + dtb_ref[...])
        qk = _dot(xn, wqk_ref[...])
        ss = _dot((qk * qk).astype(BF16), red_ref[...])
        inv = lax.rsqrt(ss * (1.0 / HEAD_DIM) + EPS)
        inv_hi, inv_lo = _split2(inv)
        inv_x = _dot(inv_hi, exp_ref[...]) + _dot(inv_lo, exp_ref[...])
        qkn = qk * inv_x * qkn_ref[...]
        cos = cos_ref[rows, :]
        sin = sin_ref[rows, :]
        for c in range(QK_WIDTH // LANES):
            xc = qkn[:, c * LANES:(c + 1) * LANES]
            partner = jnp.where(first_half,
                                pltpu.roll(xc, LANES - HEAD_DIM // 2, axis=1),
                                pltpu.roll(xc, HEAD_DIM // 2, axis=1))
            rot = xc * cos + partner * sin
            if c < ATTN_WIDTH // LANES:
                q_ref[rows, c * LANES:(c + 1) * LANES] = (rot * ATTN_SCALE).astype(BF16)
            else:
                k_ref[rows, :] = rot


def _inproj(x2d, w, cos_tab, sin_tab, tm, n_pos_blocks):
    t = x2d.shape[0]
    grid = (t // tm,)
    tok = lambda i: (i, 0)
    const = lambda i: (0, 0)
    pos = lambda i: (i % n_pos_blocks, 0)
    full = lambda a: pl.BlockSpec(a.shape, const)
    rows = lambda width, dtype: (jax.ShapeDtypeStruct((t, width), dtype), pl.BlockSpec((tm, width), tok))
    outs = [rows(ATTN_WIDTH, BF16), rows(KV_WIDTH, F32), rows(KV_WIDTH, F32), rows(SSM_WIDTH, F32)]
    operands = [x2d, w["norm1"], w["wqk"], w["wv"], w["wz"], w["wxbc"], w["wdt"], w["dt_bias"], w["qkn"],
                cos_tab, sin_tab, w["red"], w["exp"]]
    in_specs = [
        pl.BlockSpec((tm, D_MODEL), tok),
        full(w["norm1"]), full(w["wqk"]), full(w["wv"]), full(w["wz"]), full(w["wxbc"]),
        full(w["wdt"]), full(w["dt_bias"]), full(w["qkn"]),
        pl.BlockSpec((tm, LANES), pos), pl.BlockSpec((tm, LANES), pos),
        full(w["red"]), full(w["exp"]),
    ]
    outs += [rows(CONV_DIM, F32), rows(LANES, F32)]
    return pl.pallas_call(
        _inproj_kernel,
        out_shape=tuple(o[0] for o in outs),
        grid=grid,
        in_specs=in_specs,
        out_specs=tuple(o[1] for o in outs),
        compiler_params=pltpu.CompilerParams(dimension_semantics=("parallel",),
                                             vmem_limit_bytes=VMEM_LIMIT),
        name="inproj",
    )(*operands)


def _pair_rhs(blk, g):
    lo = lax.broadcasted_iota(jnp.int32, blk.shape, 1) < HEAD_DIM
    swapped = pltpu.roll(blk, HEAD_DIM, axis=1)
    if g == 0:
        top = jnp.where(lo, blk, 0.0)
        bot = jnp.where(lo, 0.0, swapped)
    else:
        top = jnp.where(lo, swapped, 0.0)
        bot = jnp.where(lo, 0.0, blk)
    return jnp.concatenate([top, bot], axis=0).astype(BF16)


def _attn_kernel(sink_ref, q_ref, kc_ref, kp_ref, vc_ref, vp_ref, o_ref):
    blk = WINDOW
    n_sub = q_ref.shape[0] // blk
    first_step = pl.program_id(1) == 0
    qi = lax.broadcasted_iota(jnp.int32, (blk, 2 * blk), 0)
    kj = lax.broadcasted_iota(jnp.int32, (blk, 2 * blk), 1) % blk
    cur_ok = kj <= qi
    seq_start_ok = jnp.logical_or(cur_ok, jnp.logical_not(first_step))
    lo = lax.broadcasted_iota(jnp.int32, (blk, LANES), 1) < HEAD_DIM
    k_blocks = [kp_ref[...]] + [kc_ref[u * blk:(u + 1) * blk, :] for u in range(n_sub)]
    v_blocks = [vp_ref[...]] + [vc_ref[u * blk:(u + 1) * blk, :] for u in range(n_sub)]
    for g in range(KV_HEADS):
        k2 = [_pair_rhs(b, g) for b in k_blocks]
        v2 = [_pair_rhs(b, g) for b in v_blocks]
        n_pairs = N_HEADS // KV_HEADS // 2
        for u in range(n_sub):
            rows = slice(u * blk, (u + 1) * blk)
            q_all = jnp.concatenate([q_ref[rows, (g * n_pairs + r) * LANES:(g * n_pairs + r + 1) * LANES]
                                     for r in range(n_pairs)], axis=0)
            s_all = _dot_nt(q_all, jnp.concatenate([k2[u + 1], k2[u]], axis=0))
            p_rows = []
            den_rows = []
            for r in range(n_pairs):
                pair = g * n_pairs + r
                s_cur = s_all[r * blk:(r + 1) * blk, :2 * blk]
                s_prev = s_all[r * blk:(r + 1) * blk, 2 * blk:]
                s = jnp.where(cur_ok, s_cur, s_prev)
                if u == 0:
                    s = jnp.where(seq_start_ok, s, -jnp.inf)
                ps = []
                dens = []
                for hh in range(2):
                    sink = sink_ref[2 * pair + hh]
                    sh = s[:, hh * blk:(hh + 1) * blk]
                    m = jnp.maximum(jnp.max(sh, axis=-1, keepdims=True), sink)
                    p = jnp.exp(sh - m)
                    dens.append(jnp.sum(p, axis=-1, keepdims=True) + jnp.exp(sink - m))
                    ps.append(p)
                p2 = jnp.concatenate(ps, axis=1)
                p_rows.append(jnp.concatenate([jnp.where(cur_ok, p2, 0.0), jnp.where(cur_ok, 0.0, p2)],
                                              axis=1).astype(BF16))
                den_rows.append(jnp.where(lo, dens[0], dens[1]))
            o_all = _dot(jnp.concatenate(p_rows, axis=0), jnp.concatenate([v2[u + 1], v2[u]], axis=0))
            for r in range(n_pairs):
                pair = g * n_pairs + r
                o2 = o_all[r * blk:(r + 1) * blk, :]
                o_ref[rows, pair * LANES:(pair + 1) * LANES] = (o2 / den_rows[r]).astype(BF16)


def _attn_prompt(q, k, v, sinks, batch, seq):
    n_sub = ATTN_QBLOCKS if seq % (ATTN_QBLOCKS * WINDOW) == 0 else 1
    rows = n_sub * WINDOW
    nb = seq // rows
    cur = lambda b, j, s: (b * nb + j, 0)
    prev = lambda b, j, s: (jnp.maximum((b * nb + j) * n_sub - 1, 0), 0)
    return pl.pallas_call(
        _attn_kernel,
        out_shape=jax.ShapeDtypeStruct((batch * seq, ATTN_WIDTH), BF16),
        grid_spec=pltpu.PrefetchScalarGridSpec(
            num_scalar_prefetch=1,
            grid=(batch, nb),
            in_specs=[
                pl.BlockSpec((rows, ATTN_WIDTH), cur),
                pl.BlockSpec((rows, KV_WIDTH), cur), pl.BlockSpec((WINDOW, KV_WIDTH), prev),
                pl.BlockSpec((rows, KV_WIDTH), cur), pl.BlockSpec((WINDOW, KV_WIDTH), prev),
            ],
            out_specs=pl.BlockSpec((rows, ATTN_WIDTH), cur),
        ),
        compiler_params=pltpu.CompilerParams(dimension_semantics=("parallel", "parallel"),
                                             vmem_limit_bytes=VMEM_LIMIT),
        name="attn_prompt",
    )(sinks, q, k, k, v, v)


def _ssd_kernel(xbc_ref, z_ref, dt_ref, convw_ref, convb_ref, alog_ref, dskip_ref, nw_ref,
                tri_ref, expand_ref, y_ref, st_ref, buf_ref, state_ref):
    c = pl.program_id(1)
    cl = SSD_CHUNK
    n_sub = xbc_ref.shape[0] // cl
    halo = CONV_HALO

    @pl.when(c == 0)
    def _():
        buf_ref[...] = jnp.zeros(buf_ref.shape, F32)
        state_ref[...] = jnp.zeros(state_ref.shape, F32)

    lane = lax.broadcasted_iota(jnp.int32, (1, LANES), 1)
    a_neg = jnp.where(lane < SSM_HEADS, -jnp.exp(alog_ref[...]), 0.0)
    tri = tri_ref[...]
    for u in range(n_sub):
        rows = slice(u * cl, (u + 1) * cl)
        if u == 0:
            x_ext = jnp.concatenate([buf_ref[...], xbc_ref[rows, :]], axis=0)
        else:
            x_ext = xbc_ref[u * cl - halo:(u + 1) * cl, :]
        _ssd_chunk(x_ext, z_ref[rows, :], dt_ref[rows, :], a_neg, tri, convw_ref, convb_ref, dskip_ref, nw_ref,
                   expand_ref, y_ref.at[rows, :], state_ref)
    buf_ref[...] = xbc_ref[n_sub * cl - halo:n_sub * cl, :]

    @pl.when(c == pl.num_programs(1) - 1)
    def _():
        st_ref[0] = state_ref[...]


def _ssd_chunk(x_ext, z, dt, a_neg, tri, convw_ref, convb_ref, dskip_ref, nw_ref, expand_ref, y_ref, state_ref):
    cl = SSD_CHUNK
    act = _causal_conv_silu(x_ext, convw_ref, convb_ref)
    xs = act[:, :SSM_WIDTH]
    bm = act[:, SSM_WIDTH:SSM_WIDTH + SSM_GROUPS * SSM_STATE].astype(BF16)
    cm = act[:, SSM_WIDTH + SSM_GROUPS * SSM_STATE:].astype(BF16)

    dta = dt * a_neg
    p_hi, p_mid, p_lo = _split3(dta)
    a_col = _dot(tri, p_hi) + _dot(tri, p_mid) + _dot(tri, p_lo)
    a_last = a_col[cl - 1:cl, :]
    a_row = a_col.T
    per_head = jnp.concatenate([dt, jnp.exp(a_col), jnp.exp(a_last - a_col)], axis=0)
    ph_hi, ph_lo = _split2(per_head)
    per_lane = _dot(ph_hi, expand_ref[...]) + _dot(ph_lo, expand_ref[...])
    dt_x = per_lane[:cl]
    ecol_x = per_lane[cl:2 * cl]
    dte_x = per_lane[2 * cl:]
    e_last = jnp.exp(a_last)
    xdt = xs * dt_x

    li = lax.broadcasted_iota(jnp.int32, (cl, cl), 0)
    si = lax.broadcasted_iota(jnp.int32, (cl, cl), 1)
    causal = si <= li
    lo = lax.broadcasted_iota(jnp.int32, (cl, LANES), 1) < SSM_HEAD_DIM
    row_lo = lax.broadcasted_iota(jnp.int32, (LANES, SSM_STATE), 0) < SSM_HEAD_DIM

    ys = []
    for g in range(SSM_GROUPS):
        b_g = bm[:, g * SSM_STATE:(g + 1) * SSM_STATE]
        c_g = cm[:, g * SSM_STATE:(g + 1) * SSM_STATE]
        cb = _dot_nt(c_g, b_g)
        for r in range(HEAD_PAIRS // SSM_GROUPS):
            j = g * (HEAD_PAIRS // SSM_GROUPS) + r
            sl = slice(j * LANES, (j + 1) * LANES)
            xdt_p = xdt[:, sl]
            ms = []
            for hh in range(2):
                h = 2 * j + hh
                seg = a_col[:, h:h + 1] - a_row[h:h + 1, :]
                ms.append(cb * jnp.exp(jnp.where(causal, seg, -jnp.inf)))
            m2 = jnp.concatenate(ms, axis=1).astype(BF16)
            rhs = jnp.concatenate([jnp.where(lo, xdt_p, 0.0), jnp.where(lo, 0.0, xdt_p)],
                                  axis=0).astype(BF16)
            y_diag = _dot(m2, rhs)
            st = state_ref[j]
            y_off = _dot_nt(c_g, st.astype(BF16)) * ecol_x[:, sl]
            xdt_e = (xdt_p * dte_x[:, sl]).T.astype(BF16)
            d_a = e_last[:, 2 * j:2 * j + 1]
            d_b = e_last[:, 2 * j + 1:2 * j + 2]
            decay = jnp.where(row_lo, jnp.broadcast_to(d_a, row_lo.shape), jnp.broadcast_to(d_b, row_lo.shape))
            state_ref[j] = decay * st + _dot(xdt_e, b_g)
            ys.append(y_diag + y_off + dskip_ref[:, sl] * xs[:, sl])
    y = jnp.concatenate(ys, axis=1)
    gated = y * _silu(z)
    gw = SSM_WIDTH // SSM_GROUPS
    outs = []
    for g in range(SSM_GROUPS):
        gg = gated[:, g * gw:(g + 1) * gw]
        outs.append(gg * lax.rsqrt(jnp.mean(gg * gg, axis=-1, keepdims=True) + EPS))
    y_ref[...] = (jnp.concatenate(outs, axis=1) * nw_ref[...]).astype(BF16)


def _ssd_prompt(xbc, z, dt, w, batch, seq):
    n_sub = SSD_CHUNKS_PER_STEP if seq % (SSD_CHUNKS_PER_STEP * SSD_CHUNK) == 0 else 1
    rows = n_sub * SSD_CHUNK
    nc = seq // rows
    tok = lambda b, c: (b * nc + c, 0)
    const = lambda b, c: (0, 0)
    full = lambda a: pl.BlockSpec(a.shape, const)
    return pl.pallas_call(
        _ssd_kernel,
        out_shape=(jax.ShapeDtypeStruct((batch * seq, SSM_WIDTH), BF16),
                   jax.ShapeDtypeStruct((batch, HEAD_PAIRS, LANES, SSM_STATE), F32)),
        grid=(batch, nc),
        in_specs=[
            pl.BlockSpec((rows, CONV_DIM), tok), pl.BlockSpec((rows, SSM_WIDTH), tok),
            pl.BlockSpec((rows, LANES), tok),
            full(w["conv_w"]), full(w["conv_b"]), full(w["a_log"]),
            full(w["d_skip"]), full(w["ssm_norm"]), full(w["tri"]), full(w["expand"]),
        ],
        out_specs=(pl.BlockSpec((rows, SSM_WIDTH), tok),
                   pl.BlockSpec((1, HEAD_PAIRS, LANES, SSM_STATE), lambda b, c: (b, 0, 0, 0))),
        scratch_shapes=[pltpu.VMEM((CONV_HALO, CONV_DIM), F32),
                        pltpu.VMEM((HEAD_PAIRS, LANES, SSM_STATE), F32)],
        compiler_params=pltpu.CompilerParams(dimension_semantics=("parallel", "arbitrary"),
                                             vmem_limit_bytes=VMEM_LIMIT),
        name="ssd_prompt",
    )(xbc, z, dt, w["conv_w"], w["conv_b"], w["a_log"], w["d_skip"], w["ssm_norm"], w["tri"], w["expand"])


def _attn_sample_kernel(qx_ref, kc_ref, kn_ref, vc_ref, vn_ref, sink_ref, ko_ref, vo_ref, o_ref):
    bs = qx_ref.shape[0]
    w = kc_ref.shape[1]
    sink = sink_ref[...]
    lo = lax.broadcasted_iota(jnp.int32, (1, LANES), 1) < HEAD_DIM
    for i in range(bs):
        ko_ref[i, 0:w - 1, :] = kc_ref[i, 1:w, :]
        ko_ref[i, w - 1:w, :] = kn_ref[i]
        vo_ref[i, 0:w - 1, :] = vc_ref[i, 1:w, :]
        vo_ref[i, w - 1:w, :] = vn_ref[i]
        s = _dot_nt(qx_ref[i], ko_ref[i].astype(BF16))
        m = jnp.maximum(jnp.max(s, axis=-1, keepdims=True), sink)
        p = jnp.exp(s - m)
        den = jnp.sum(p, axis=-1, keepdims=True) + jnp.exp(sink - m)
        o = _dot(p.astype(BF16), vo_ref[i].astype(BF16)) / den
        o_sw = pltpu.roll(o, HEAD_DIM, axis=1)
        for j in range(N_HEADS // 2):
            a, b = (o, o_sw) if j < N_HEADS // 4 else (o_sw, o)
            o_ref[i:i + 1, j * LANES:(j + 1) * LANES] = jnp.where(lo, a[2 * j:2 * j + 1], b[2 * j + 1:2 * j + 2])


def _attn_sample(qx, kc, kn, vc, vn, sink_x, bs):
    n, w = kc.shape[0], kc.shape[1]
    blk3 = lambda i: (i, 0, 0)
    return pl.pallas_call(
        _attn_sample_kernel,
        out_shape=(jax.ShapeDtypeStruct((n, w, KV_WIDTH), F32),
                   jax.ShapeDtypeStruct((n, w, KV_WIDTH), F32),
                   jax.ShapeDtypeStruct((n, ATTN_WIDTH), F32)),
        grid=(n // bs,),
        in_specs=[
            pl.BlockSpec((bs, BF16_ROWS, LANES), blk3),
            pl.BlockSpec((bs, w, KV_WIDTH), blk3), pl.BlockSpec((bs, 1, KV_WIDTH), blk3),
            pl.BlockSpec((bs, w, KV_WIDTH), blk3), pl.BlockSpec((bs, 1, KV_WIDTH), blk3),
            pl.BlockSpec(sink_x.shape, lambda i: (0, 0)),
        ],
        out_specs=(pl.BlockSpec((bs, w, KV_WIDTH), blk3), pl.BlockSpec((bs, w, KV_WIDTH), blk3),
                   pl.BlockSpec((bs, ATTN_WIDTH), lambda i: (i, 0))),
        compiler_params=pltpu.CompilerParams(dimension_semantics=("parallel",),
                                             vmem_limit_bytes=VMEM_LIMIT),
        name="attn_sample",
    )(qx, kc, kn, vc, vn, sink_x)


def _ssd_sample_kernel(xbc_ref, z_ref, dt_ref, cprev_ref, h0_ref, convw_ref, convb_ref, alog_ref,
                       dskip_ref, nw_ref, y_ref, cnew_ref, h1_ref):
    bs = xbc_ref.shape[0]
    x_raw = xbc_ref[...]
    conv = convb_ref[...] + x_raw * convw_ref[CONV_K - 1:CONV_K, :]
    for j in range(CONV_K - 1):
        conv = conv + cprev_ref[j] * convw_ref[j:j + 1, :]
    for j in range(CONV_K - 2):
        cnew_ref[j] = cprev_ref[j + 1]
    cnew_ref[CONV_K - 2] = x_raw
    act = _silu(conv)
    xs = act[:, :SSM_WIDTH]
    bm = act[:, SSM_WIDTH:SSM_WIDTH + SSM_GROUPS * SSM_STATE].astype(BF16)
    cm = act[:, SSM_WIDTH + SSM_GROUPS * SSM_STATE:].astype(BF16)
    lane = lax.broadcasted_iota(jnp.int32, (1, LANES), 1)
    a_neg = jnp.where(lane < SSM_HEADS, -jnp.exp(alog_ref[...]), 0.0)
    dt = dt_ref[...]
    dec = jnp.exp(dt * a_neg)
    xdt = xs * _lane_bcast_pairs(dt, HEAD_PAIRS)
    rowid = lax.broadcasted_iota(jnp.int32, (bs, LANES), 0)
    row_lo = lax.broadcasted_iota(jnp.int32, (LANES, SSM_STATE), 0) < SSM_HEAD_DIM
    ys = []
    for j in range(HEAD_PAIRS):
        g = j // (HEAD_PAIRS // SSM_GROUPS)
        sl = slice(j * LANES, (j + 1) * LANES)
        b_g = bm[:, g * SSM_STATE:(g + 1) * SSM_STATE]
        c_g = cm[:, g * SSM_STATE:(g + 1) * SSM_STATE]
        xdt_p = xdt[:, sl]
        y_p = jnp.zeros((bs, LANES), F32)
        for i in range(bs):
            xi = jnp.where(rowid == i, xdt_p, 0.0).T.astype(BF16)
            d_a = dec[i:i + 1, 2 * j:2 * j + 1]
            d_b = dec[i:i + 1, 2 * j + 1:2 * j + 2]
            decay = jnp.where(row_lo, jnp.broadcast_to(d_a, row_lo.shape), jnp.broadcast_to(d_b, row_lo.shape))
            new = decay * h0_ref[i, j] + _dot(xi, b_g)
            h1_ref[i, j] = new
            y_p = y_p + jnp.where(rowid == i, _dot_nt(c_g, new.astype(BF16)), 0.0)
        ys.append(y_p + dskip_ref[:, sl] * xs[:, sl])
    y = jnp.concatenate(ys, axis=1)
    gated = y * _silu(z_ref[...])
    gw = SSM_WIDTH // SSM_GROUPS
    outs = []
    for g in range(SSM_GROUPS):
        gg = gated[:, g * gw:(g + 1) * gw]
        outs.append(gg * lax.rsqrt(jnp.mean(gg * gg, axis=-1, keepdims=True) + EPS))
    y_ref[...] = (jnp.concatenate(outs, axis=1) * nw_ref[...]).astype(BF16)


def _ssd_sample(xbc, z, dt, cprev_t, h0, w, bs):
    n = xbc.shape[0]
    tok = lambda i: (i, 0)
    const = lambda i: (0, 0)
    full = lambda a: pl.BlockSpec(a.shape, const)
    return pl.pallas_call(
        _ssd_sample_kernel,
        out_shape=(jax.ShapeDtypeStruct((n, SSM_WIDTH), BF16),
                   jax.ShapeDtypeStruct((CONV_K - 1, n, CONV_DIM), F32),
                   jax.ShapeDtypeStruct((n, HEAD_PAIRS, LANES, SSM_STATE), F32)),
        grid=(n // bs,),
        in_specs=[
            pl.BlockSpec((bs, CONV_DIM), tok), pl.BlockSpec((bs, SSM_WIDTH), tok),
            pl.BlockSpec((bs, LANES), tok),
            pl.BlockSpec((CONV_K - 1, bs, CONV_DIM), lambda i: (0, i, 0)),
            pl.BlockSpec((bs, HEAD_PAIRS, LANES, SSM_STATE), lambda i: (i, 0, 0, 0)),
            full(w["conv_w"]), full(w["conv_b"]), full(w["a_log"]),
            full(w["d_skip"]), full(w["ssm_norm"]),
        ],
        out_specs=(pl.BlockSpec((bs, SSM_WIDTH), tok),
                   pl.BlockSpec((CONV_K - 1, bs, CONV_DIM), lambda i: (0, i, 0)),
                   pl.BlockSpec((bs, HEAD_PAIRS, LANES, SSM_STATE), lambda i: (i, 0, 0, 0))),
        compiler_params=pltpu.CompilerParams(dimension_semantics=("parallel",),
                                             vmem_limit_bytes=VMEM_LIMIT),
        name="ssd_sample",
    )(xbc, z, dt, cprev_t, h0, w["conv_w"], w["conv_b"], w["a_log"], w["d_skip"], w["ssm_norm"])


def _pack_bf16_pair(v):
    c = v.shape[1] // 2
    hi = lax.bitcast_convert_type(v[:, :c].astype(BF16).astype(F32), jnp.uint32)
    lo = lax.bitcast_convert_type(v[:, c:].astype(BF16).astype(F32), jnp.uint32)
    return hi | (lo >> 16)


def _unpack_bf16_pair(word):
    a = lax.bitcast_convert_type(word & jnp.uint32(0xFFFF0000), F32)
    b = lax.bitcast_convert_type(word << 16, F32)
    return a, b


def _outproj_router_kernel(x_ref, a_ref, s_ref, wo_a_ref, wo_s_ref, n2_ref, wr_ref, br_ref, tri_ref,
                           h_ref, t_ref, route_ref, cnt_ref, carry_ref):
    @pl.when(pl.program_id(0) == 0)
    def _():
        carry_ref[...] = jnp.zeros(carry_ref.shape, F32)

    tm = x_ref.shape[0]
    sub = min(tm, ROUTER_SUBTILE)
    carry = carry_ref[0:1, :]
    for s in range(tm // sub):
        rows = slice(s * sub, (s + 1) * sub)
        carry = _outproj_route_rows(x_ref[rows, :], a_ref[rows, :], s_ref[rows, :], wo_a_ref, wo_s_ref, n2_ref,
                                    wr_ref, br_ref, tri_ref, carry, h_ref.at[rows, :], t_ref.at[rows, :],
                                    route_ref.at[rows, :])
    carry_ref[0:1, :] = carry
    cnt_ref[...] = jnp.broadcast_to(carry, cnt_ref.shape)


def _outproj_route_rows(x, a, s, wo_a_ref, wo_s_ref, n2_ref, wr_ref, br_ref, tri_ref, carry, h_ref, t_ref, route_ref):
    h = x + _dot(a.astype(BF16), wo_a_ref[...]) + _dot(s.astype(BF16), wo_s_ref[...])
    h_ref[...] = h
    ms = jnp.mean(h * h, axis=-1, keepdims=True)
    t = h * lax.rsqrt(ms + EPS) * n2_ref[...]
    t_hi, t_lo = _split2(t)
    t_ref[...] = _pack_bf16_pair(t)
    both = _dot(t_hi, wr_ref[...])
    logits = both[:, :LANES] + both[:, LANES:] + _dot(t_lo, wr_ref[:, :LANES]) + br_ref[...]
    lane = lax.broadcasted_iota(jnp.int32, logits.shape, 1)
    lane_f = lane.astype(F32)
    big = float(LANES)
    ninf = -jnp.inf
    glog = jnp.where(lane < N_EGROUPS, logits, ninf)
    gmax = jnp.max(glog, axis=-1, keepdims=True)
    g_top = 1.0 / jnp.sum(jnp.exp(glog - gmax), axis=-1, keepdims=True)
    g_idx = jnp.min(jnp.where(glog == gmax, lane_f, big), axis=-1, keepdims=True)
    e_of_lane = lane - EXPERT_LANE0
    in_grp = jnp.logical_and(jnp.logical_and(e_of_lane >= 0, e_of_lane < N_EXPERTS),
                             (e_of_lane // EXP_PER_GROUP).astype(F32) == g_idx)
    ml = jnp.where(in_grp, logits, ninf)
    m1 = jnp.max(ml, axis=-1, keepdims=True)
    i1 = jnp.min(jnp.where(ml == m1, lane_f, big), axis=-1, keepdims=True)
    ml2 = jnp.where(lane_f == i1, ninf, ml)
    m2 = jnp.max(ml2, axis=-1, keepdims=True)
    i2 = jnp.min(jnp.where(ml2 == m2, lane_f, big), axis=-1, keepdims=True)
    r = jnp.exp(m2 - m1)
    w1 = g_top / (1.0 + r)
    w2 = g_top * r / (1.0 + r)
    is1 = lane_f == i1
    is2 = lane_f == i2
    onehot = jnp.where(jnp.logical_or(is1, is2), 1.0, 0.0)
    onehot_bf = onehot.astype(BF16)
    n_rows = onehot.shape[0]
    cb = tri_ref.shape[0]
    cums = []
    for blk in range(n_rows // cb):
        c = _dot(tri_ref[...], onehot_bf[blk * cb:(blk + 1) * cb, :]) + carry
        carry = c[cb - 1:cb, :]
        cums.append(c)
    before = jnp.concatenate(cums, axis=0) - onehot
    rank1 = jnp.sum(jnp.where(is1, before, 0.0), axis=-1, keepdims=True)
    rank2 = jnp.sum(jnp.where(is2, before, 0.0), axis=-1, keepdims=True)
    fields = (i1 - EXPERT_LANE0, i2 - EXPERT_LANE0, w1, w2, rank1, rank2)
    route = jnp.zeros(logits.shape, F32)
    for pos, val in enumerate(fields):
        route = jnp.where(lane == pos, val, route)
    route_ref[...] = route
    return carry


def _outproj_router(x2d, attn, ssm, w, tm):
    t = x2d.shape[0]
    tok = lambda i: (i, 0)
    const = lambda i: (0, 0)
    full = lambda a: pl.BlockSpec(a.shape, const)
    idx = jnp.arange(min(tm, COUNT_BLOCK))
    tri = (idx[None, :] <= idx[:, None]).astype(BF16)
    return pl.pallas_call(
        _outproj_router_kernel,
        out_shape=(jax.ShapeDtypeStruct((t, D_MODEL), F32), jax.ShapeDtypeStruct((t, D_MODEL // 2), jnp.uint32),
                   jax.ShapeDtypeStruct((t, LANES), F32), jax.ShapeDtypeStruct((8, LANES), F32)),
        grid=(t // tm,),
        in_specs=[
            pl.BlockSpec((tm, D_MODEL), tok), pl.BlockSpec((tm, ATTN_WIDTH), tok),
            pl.BlockSpec((tm, SSM_WIDTH), tok),
            full(w["wo_a"]), full(w["wo_s"]), full(w["norm2"]), full(w["wr"]),
            full(w["br"]), full(tri),
        ],
        out_specs=(pl.BlockSpec((tm, D_MODEL), tok), pl.BlockSpec((tm, D_MODEL // 2), tok),
                   pl.BlockSpec((tm, LANES), tok), pl.BlockSpec((8, LANES), const)),
        scratch_shapes=[pltpu.VMEM((8, LANES), F32)],
        compiler_params=pltpu.CompilerParams(dimension_semantics=("arbitrary",),
                                             vmem_limit_bytes=VMEM_LIMIT),
        name="outproj_router",
    )(x2d, attn, ssm, w["wo_a"], w["wo_s"], w["norm2"], w["wr"], w["br"], tri)


def _expert_hidden(t_a, t_b, w1):
    half = D_MODEL // 2
    gu = _dot(t_a, w1[:half]) + _dot(t_b, w1[half:])
    return _silu(gu[:, :EXPERT_FF]) * gu[:, EXPERT_FF:]


def _gate_up_bf16(wg_ref, wu_ref):
    return jnp.concatenate([wg_ref[0].astype(BF16), wu_ref[0].astype(BF16)], axis=1)


def _moe_dense_kernel(h_ref, t_ref, route_ref, wg_ref, wu_ref, wd_ref, y_ref):
    e = pl.program_id(0)

    @pl.when(e == 0)
    def _():
        y_ref[...] = h_ref[...]

    t_a, t_b = _unpack_bf16_pair(t_ref[...])
    route = route_ref[...]
    e1, e2, g1, g2 = route[:, 0:1], route[:, 1:2], route[:, 2:3], route[:, 3:4]
    e_f = e.astype(F32)
    hid = _expert_hidden(t_a.astype(BF16), t_b.astype(BF16), _gate_up_bf16(wg_ref, wu_ref))
    c_e = jnp.where(e1 == e_f, g1, 0.0) + jnp.where(e2 == e_f, g2, 0.0)
    y_ref[...] += _dot((hid * c_e).astype(BF16), wd_ref[0].astype(BF16))


def _moe_dense(h, t, route, w):
    n = h.shape[0]
    whole = lambda e: (0, 0)
    by_expert = lambda e: (e, 0, 0)
    return pl.pallas_call(
        _moe_dense_kernel,
        out_shape=jax.ShapeDtypeStruct((n, D_MODEL), F32),
        grid=(N_EXPERTS,),
        in_specs=[pl.BlockSpec((n, D_MODEL), whole), pl.BlockSpec((n, D_MODEL // 2), whole),
                  pl.BlockSpec((n, LANES), whole),
                  pl.BlockSpec((1, D_MODEL, EXPERT_FF), by_expert), pl.BlockSpec((1, D_MODEL, EXPERT_FF), by_expert),
                  pl.BlockSpec((1, EXPERT_FF, D_MODEL), by_expert)],
        out_specs=pl.BlockSpec((n, D_MODEL), whole),
        compiler_params=pltpu.CompilerParams(dimension_semantics=("arbitrary",),
                                             vmem_limit_bytes=VMEM_LIMIT),
        name="moe_dense",
    )(h, t, route, w["w_gate"], w["w_up"], w["w_down"])


def _sc_scatter_rows(src, pos1, pos2, n_out):
    t, width = src.shape
    rows_per_worker = t // SC_WORKERS
    n_chunks = rows_per_worker // SC_CHUNK
    assert t == SC_WORKERS * SC_CHUNK * n_chunks
    mesh = plsc.VectorSubcoreMesh(core_axis_name="c", subcore_axis_name="s")

    @functools.partial(
        pl.kernel, mesh=mesh,
        out_type=jax.ShapeDtypeStruct((n_out, width), src.dtype),
        scratch_types=[pltpu.VMEM((SC_CHUNK,), jnp.int32), pltpu.VMEM((SC_CHUNK,), jnp.int32),
                       pltpu.VMEM((SC_CHUNK, width), src.dtype), pltpu.SemaphoreType.DMA],
    )
    def scatter_kernel(src_hbm, p1_hbm, p2_hbm, out_hbm, i1_v, i2_v, rows_v, sem):
        wid = lax.axis_index("s") * SC_CORES + lax.axis_index("c")
        base = wid * rows_per_worker

        @pl.loop(0, n_chunks)
        def _(c):
            off = pl.multiple_of(base + c * SC_CHUNK, 8)
            pltpu.sync_copy(p1_hbm.at[pl.ds(off, SC_CHUNK)], i1_v)
            pltpu.sync_copy(p2_hbm.at[pl.ds(off, SC_CHUNK)], i2_v)
            pltpu.sync_copy(src_hbm.at[pl.ds(off, SC_CHUNK)], rows_v)
            pltpu.async_copy(rows_v, out_hbm.at[i1_v], sem).wait()
            pltpu.async_copy(rows_v, out_hbm.at[i2_v], sem).wait()

    return scatter_kernel(src, pos1, pos2)


def _sc_gather_rows(table, idx):
    n, width = idx.shape[0], table.shape[1]
    rows_per_worker = n // SC_WORKERS
    n_chunks = rows_per_worker // SC_CHUNK
    assert n == SC_WORKERS * SC_CHUNK * n_chunks and n_chunks % 2 == 0
    mesh = plsc.VectorSubcoreMesh(core_axis_name="c", subcore_axis_name="s")

    @functools.partial(
        pl.kernel, mesh=mesh,
        out_type=jax.ShapeDtypeStruct((n, width), table.dtype),
        scratch_types=[pltpu.VMEM((2, SC_CHUNK), jnp.int32), pltpu.VMEM((2, SC_CHUNK, width), table.dtype),
                       pltpu.SemaphoreType.DMA, pltpu.SemaphoreType.DMA,
                       pltpu.SemaphoreType.DMA, pltpu.SemaphoreType.DMA],
    )
    def gather_kernel(table_hbm, idx_hbm, out_hbm, idx_v, rows_v, g0, g1, w0, w1):
        wid = lax.axis_index("s") * SC_CORES + lax.axis_index("c")
        base = wid * rows_per_worker
        gsem = (g0, g1)
        wsem = (w0, w1)

        def gather_copy(slot):
            return pltpu.make_async_copy(table_hbm.at[idx_v.at[slot]], rows_v.at[slot], gsem[slot])

        def write_copy(c, slot):
            off = pl.multiple_of(base + c * SC_CHUNK, 8)
            return pltpu.make_async_copy(rows_v.at[slot], out_hbm.at[pl.ds(off, SC_CHUNK)], wsem[slot])

        def start_gather(c, slot):
            off = pl.multiple_of(base + c * SC_CHUNK, 8)
            pltpu.sync_copy(idx_hbm.at[pl.ds(off, SC_CHUNK)], idx_v.at[slot])
            gather_copy(slot).start()

        start_gather(0, 0)

        @pl.loop(0, n_chunks, step=2)
        def _(c):
            @pl.when(c > 0)
            def _():
                write_copy(c - 1, 1).wait()

            start_gather(c + 1, 1)
            gather_copy(0).wait()
            write_copy(c, 0).start()
            gather_copy(1).wait()
            write_copy(c + 1, 1).start()
            write_copy(c, 0).wait()

            @pl.when(c + 2 < n_chunks)
            def _():
                start_gather(c + 2, 0)

        write_copy(n_chunks - 1, 1).wait()

    return gather_kernel(table, idx)


def _moe_grouped_kernel(te_ref, nt_ref, order_ref, x_ref, wg_ref, wu_ref, wd_ref, o_ref, w1_bf_ref, w2_bf_ref):
    del order_ref
    i = pl.program_id(0)

    @pl.when(jnp.logical_or(i == 0, te_ref[i] != te_ref[jnp.maximum(i - 1, 0)]))
    def _():
        w1_bf_ref[...] = _gate_up_bf16(wg_ref, wu_ref)
        w2_bf_ref[...] = wd_ref[0].astype(BF16)

    @pl.when(i < nt_ref[0])
    def _():
        for s in range(MOE_TILE // MOE_SUBTILE):
            rows = slice(s * MOE_SUBTILE, (s + 1) * MOE_SUBTILE)
            t_a, t_b = _unpack_bf16_pair(x_ref[rows, :])
            hid = _expert_hidden(t_a.astype(BF16), t_b.astype(BF16), w1_bf_ref[...])
            o_ref[rows, :] = _pack_bf16_pair(_dot(hid.astype(BF16), w2_bf_ref[...]))


def _moe_grouped(xs, tile_expert, n_tiles, order, w):
    rows = xs.shape[0]
    row = lambda i, te, nt, od: (i, 0)
    by_expert = lambda i, te, nt, od: (te[i], 0, 0)
    return pl.pallas_call(
        _moe_grouped_kernel,
        out_shape=jax.ShapeDtypeStruct((rows, D_MODEL // 2), jnp.uint32),
        grid_spec=pltpu.PrefetchScalarGridSpec(
            num_scalar_prefetch=3,
            grid=(rows // MOE_TILE,),
            in_specs=[pl.BlockSpec((MOE_TILE, D_MODEL // 2), row),
                      pl.BlockSpec((1, D_MODEL, EXPERT_FF), by_expert),
                      pl.BlockSpec((1, D_MODEL, EXPERT_FF), by_expert),
                      pl.BlockSpec((1, EXPERT_FF, D_MODEL), by_expert)],
            out_specs=pl.BlockSpec((MOE_TILE, D_MODEL // 2), row),
            scratch_shapes=[pltpu.VMEM((D_MODEL, 2 * EXPERT_FF), BF16), pltpu.VMEM((EXPERT_FF, D_MODEL), BF16)],
        ),
        compiler_params=pltpu.CompilerParams(dimension_semantics=("arbitrary",),
                                             vmem_limit_bytes=VMEM_LIMIT),
        name="moe_grouped",
    )(tile_expert, n_tiles, order, xs, w["w_gate"], w["w_up"], w["w_down"])


def _moe_combine_kernel(h_ref, z1_ref, z2_ref, route_ref, y_ref):
    route = route_ref[...]
    g1, g2 = route[:, 2:3], route[:, 3:4]
    half = D_MODEL // 2
    a1, b1 = _unpack_bf16_pair(z1_ref[...])
    a2, b2 = _unpack_bf16_pair(z2_ref[...])
    y_ref[:, :half] = h_ref[:, :half] + g1 * a1 + g2 * a2
    y_ref[:, half:] = h_ref[:, half:] + g1 * b1 + g2 * b2


def _moe_combine(h, z, route, tm):
    t = h.shape[0]
    nb = t // tm
    tok = lambda i: (i, 0)
    return pl.pallas_call(
        _moe_combine_kernel,
        out_shape=jax.ShapeDtypeStruct((t, D_MODEL), F32),
        grid=(nb,),
        in_specs=[pl.BlockSpec((tm, D_MODEL), tok), pl.BlockSpec((tm, D_MODEL // 2), tok),
                  pl.BlockSpec((tm, D_MODEL // 2), lambda i: (i + nb, 0)), pl.BlockSpec((tm, LANES), tok)],
        out_specs=pl.BlockSpec((tm, D_MODEL), tok),
        compiler_params=pltpu.CompilerParams(dimension_semantics=("parallel",),
                                             vmem_limit_bytes=VMEM_LIMIT),
        name="moe_combine",
    )(h, z, z, route)


def _route_pos_kernel(route_ref, cnt_ref, upper_ref, pos_ref):
    tm = route_ref.shape[0]
    cnt = cnt_ref[...]
    padded = jnp.floor((cnt + float(MOE_TILE - 1)) * (1.0 / MOE_TILE)) * float(MOE_TILE)
    p_hi, p_mid, p_lo = _split3(padded)
    upper = upper_ref[...]
    starts = (_dot(p_hi, upper) + _dot(p_mid, upper) + _dot(p_lo, upper))[0:1, :]
    route = route_ref[...]
    lane_f = lax.broadcasted_iota(jnp.int32, (tm, LANES), 1).astype(F32)
    diag = (lax.broadcasted_iota(jnp.int32, (LANES, LANES), 0)
            == lax.broadcasted_iota(jnp.int32, (LANES, LANES), 1))
    for k in range(2):
        e_lane = route[:, k:k + 1] + float(EXPERT_LANE0)
        pos = jnp.sum(jnp.where(lane_f == e_lane, starts, 0.0), axis=-1, keepdims=True) + route[:, 4 + k:5 + k]
        for r in range(tm // LANES):
            col = pos[r * LANES:(r + 1) * LANES, :]
            row = jnp.sum(jnp.where(diag, col, 0.0), axis=0, keepdims=True)
            pos_ref[k, r:r + 1, :] = row.astype(jnp.int32)


def _route_positions(route, counts, tm):
    t = route.shape[0]
    idx = jnp.arange(LANES)
    upper = (idx[:, None] < idx[None, :]).astype(BF16)
    return pl.pallas_call(
        _route_pos_kernel,
        out_shape=jax.ShapeDtypeStruct((2, t // LANES, LANES), jnp.int32),
        grid=(t // tm,),
        in_specs=[pl.BlockSpec((tm, LANES), lambda i: (i, 0)), pl.BlockSpec((8, LANES), lambda i: (0, 0)),
                  pl.BlockSpec((LANES, LANES), lambda i: (0, 0))],
        out_specs=pl.BlockSpec((2, tm // LANES, LANES), lambda i: (0, i, 0)),
        compiler_params=pltpu.CompilerParams(dimension_semantics=("parallel",),
                                             vmem_limit_bytes=VMEM_LIMIT),
        name="route_positions",
    )(route, counts, upper)


def _moe_routed(h, t_packed, route, counts, w, tm, run_first):
    t = h.shape[0]
    pos = _route_positions(route, counts, min(8 * LANES, t))
    pos1 = pos[0].reshape(t)
    pos2 = pos[1].reshape(t)
    cnt = counts[0, EXPERT_LANE0:EXPERT_LANE0 + N_EXPERTS].astype(jnp.int32)
    padded = (cnt + MOE_TILE - 1) // MOE_TILE * MOE_TILE
    ends = jnp.cumsum(padded)
    n_rows = 2 * t + N_EXPERTS * MOE_TILE
    n_tiles = ends[N_EXPERTS - 1] // MOE_TILE
    tile_start = jnp.arange(n_rows // MOE_TILE, dtype=jnp.int32) * MOE_TILE
    tile_start = jnp.minimum(tile_start, ends[N_EXPERTS - 1] - MOE_TILE)
    tile_expert = jnp.sum((tile_start[:, None] >= ends[None, :]).astype(jnp.int32), axis=1)
    xs = _sc_scatter_rows(t_packed, pos1, pos2, n_rows)
    order = lax.bitcast_convert_type(run_first.reshape(-1)[:1].astype(F32), jnp.int32)
    out = _moe_grouped(xs, tile_expert, n_tiles.reshape(1), order, w)
    z = _sc_gather_rows(out, pos.reshape(2 * t))
    return _moe_combine(h, z, route, tm)


def _pad_lanes(a, width=LANES):
    return jnp.pad(a, ((0, 0), (0, width - a.shape[1])))


def _prep_weights(norm1, w_in, q_norm, k_norm, conv_w, conv_b, dt_bias, a_log, d_skip, ssm_norm, w_out,
                  norm2, w_grp, b_grp, w_exp, b_exp, w_gate, w_up, w_down):
    w = {}
    w["norm1"] = norm1.reshape(1, D_MODEL)
    w["wqk"] = w_in[:, :K_END].astype(BF16)
    w["wv"] = w_in[:, K_END:V_END].astype(BF16)
    w["wz"] = w_in[:, V_END:Z_END].astype(BF16)
    w["wxbc"] = w_in[:, Z_END:XBC_END].astype(BF16)
    w["wdt"] = _pad_lanes(w_in[:, XBC_END:]).astype(BF16)
    w["qkn"] = jnp.concatenate([jnp.tile(q_norm, N_HEADS), jnp.tile(k_norm, KV_HEADS)]).reshape(1, QK_WIDTH)
    head_of_col = jnp.arange(QK_WIDTH) // HEAD_DIM
    red = (head_of_col[:, None] == jnp.arange(LANES)[None, :])
    w["red"] = red.astype(BF16)
    w["exp"] = red.T.astype(BF16)
    w["conv_w"] = conv_w
    w["conv_b"] = conv_b.reshape(1, CONV_DIM)
    w["dt_bias"] = _pad_lanes(dt_bias.reshape(1, SSM_HEADS))
    w["a_log"] = _pad_lanes(a_log.reshape(1, SSM_HEADS))
    w["d_skip"] = jnp.repeat(d_skip, SSM_HEAD_DIM).reshape(1, SSM_WIDTH)
    w["ssm_norm"] = ssm_norm.reshape(1, SSM_WIDTH)
    idx = jnp.arange(SSD_CHUNK)
    w["tri"] = (idx[None, :] <= idx[:, None]).astype(BF16)
    lane_head = jnp.arange(SSM_WIDTH) // SSM_HEAD_DIM
    w["expand"] = (jnp.arange(LANES)[:, None] == lane_head[None, :]).astype(BF16)
    w["wo_a"] = w_out[:ATTN_WIDTH].astype(BF16)
    w["wo_s"] = w_out[ATTN_WIDTH:].astype(BF16)
    w["norm2"] = norm2.reshape(1, D_MODEL)
    wr = jnp.zeros((D_MODEL, LANES), F32)
    wr = wr.at[:, :N_EGROUPS].set(w_grp).at[:, EXPERT_LANE0:EXPERT_LANE0 + N_EXPERTS].set(w_exp)
    wr_hi = wr.astype(BF16)
    w["wr"] = jnp.concatenate([wr_hi, (wr - wr_hi.astype(F32)).astype(BF16)], axis=1)
    br = jnp.zeros((1, LANES), F32)
    w["br"] = br.at[0, :N_EGROUPS].set(b_grp).at[0, EXPERT_LANE0:EXPERT_LANE0 + N_EXPERTS].set(b_exp)
    w["w_gate"], w["w_up"], w["w_down"] = w_gate, w_up, w_down
    return w


def _rope_tables(pos):
    inv = 1.0 / (ROPE_THETA ** (jnp.arange(0, HEAD_DIM, 2, dtype=F32) / HEAD_DIM))
    ang = pos.astype(F32)[:, None] * inv[None, :]
    cos, sin = jnp.cos(ang), jnp.sin(ang)
    reps = LANES // HEAD_DIM
    return (jnp.tile(jnp.concatenate([cos, cos], axis=-1), (1, reps)),
            jnp.tile(jnp.concatenate([-sin, sin], axis=-1), (1, reps)))


def _token_tile(t):
    for tm in (1024, 512, 256, 128, 64, 32, 16):
        if t % tm == 0:
            return tm
    raise ValueError(f"token count {t} is not a multiple of 16")


def kernel(x_prompt, x_sample, cache_win_k, cache_win_v, state_conv, state_ssm, norm1, w_in, q_norm, k_norm,
           sinks, conv_w, conv_b, dt_bias, a_log, d_skip, ssm_norm, w_out, norm2, w_grp, b_grp, w_exp, b_exp,
           w_gate, w_up, w_down):
    depth = norm1.shape[0]
    assert depth == 1, "single-layer stack"
    bp, lp, _ = x_prompt.shape
    bsn, ls, _ = x_sample.shape
    assert ls == 1 and lp % WINDOW == 0 and cache_win_k.shape[2] == WINDOW
    l = 0
    w = _prep_weights(norm1[l], w_in[l], q_norm[l], k_norm[l], conv_w[l], conv_b[l], dt_bias[l], a_log[l],
                      d_skip[l], ssm_norm[l], w_out[l], norm2[l], w_grp[l], b_grp[l], w_exp[l], b_exp[l],
                      w_gate[l], w_up[l], w_down[l])
    sink = sinks[l]

    tp = bp * lp
    xp = x_prompt.reshape(tp, D_MODEL)
    tm_p = _token_tile(lp)
    cos_p, sin_p = _rope_tables(jnp.arange(lp, dtype=jnp.int32))
    q, k, v, z, xbc, dt = _inproj(xp, w, cos_p, sin_p, tm_p, lp // tm_p)
    attn = _attn_prompt(q, k, v, sink, bp, lp)
    ssm, st_p = _ssd_prompt(xbc, z, dt, w, bp, lp)
    h, t, route, counts = _outproj_router(xp, attn, ssm, w, tm_p)
    k3 = k.reshape(bp, lp, KV_HEADS, HEAD_DIM)
    v3 = v.reshape(bp, lp, KV_HEADS, HEAD_DIM)
    win_k_p = k3[:, lp - WINDOW:][None]
    win_v_p = v3[:, lp - WINDOW:][None]
    conv_p = xbc.reshape(bp, lp, CONV_DIM)[:, lp - (CONV_K - 1):][None]
    ssm_p = st_p.reshape(1, bp, SSM_HEADS, SSM_HEAD_DIM, SSM_STATE)

    xs2 = x_sample.reshape(bsn, D_MODEL)
    tm_s = _token_tile(bsn)
    cos_s, sin_s = _rope_tables(jnp.full((tm_s,), PAST_LEN, jnp.int32))
    q_s, k_s, v_s, z_s, xbc_s, dt_s = _inproj(xs2, w, cos_s, sin_s, tm_s, 1)
    q4 = q_s.reshape(bsn, KV_HEADS, N_HEADS // KV_HEADS, HEAD_DIM)
    zq = jnp.zeros_like(q4[:, 0])
    qx = jnp.concatenate([jnp.concatenate([q4[:, 0], zq], axis=-1),
                          jnp.concatenate([zq, q4[:, 1]], axis=-1)], axis=1)
    qx = jnp.pad(qx, ((0, 0), (0, BF16_ROWS - N_HEADS), (0, 0)))
    sink_x = jnp.pad(jnp.broadcast_to(sink[:, None], (N_HEADS, LANES)), ((0, BF16_ROWS - N_HEADS), (0, 0)))
    kc = cache_win_k[l].reshape(bsn, WINDOW, KV_WIDTH)
    vc = cache_win_v[l].reshape(bsn, WINDOW, KV_WIDTH)
    ko, vo, attn_s = _attn_sample(qx, kc, k_s.reshape(bsn, 1, KV_WIDTH), vc, v_s.reshape(bsn, 1, KV_WIDTH),
                               sink_x, 8)
    y_prompt = _moe_routed(h, t, route, counts, w, tm_p, attn_s).reshape(bp, lp, D_MODEL)
    cprev_t = jnp.transpose(state_conv[l], (1, 0, 2))
    h0 = state_ssm[l].reshape(bsn, HEAD_PAIRS, LANES, SSM_STATE)
    ssm_s, cnew_t, h1 = _ssd_sample(xbc_s, z_s, dt_s, cprev_t, h0, w, 16)
    h_s, t_s, route_s, _ = _outproj_router(xs2, attn_s, ssm_s, w, tm_s)
    y_sample = _moe_dense(h_s, t_s, route_s, w).reshape(bsn, 1, D_MODEL)
    win_k_s = ko.reshape(1, bsn, WINDOW, KV_HEADS, HEAD_DIM)
    win_v_s = vo.reshape(1, bsn, WINDOW, KV_HEADS, HEAD_DIM)
    conv_s = jnp.transpose(cnew_t, (1, 0, 2))[None]
    ssm_s_state = h1.reshape(1, bsn, SSM_HEADS, SSM_HEAD_DIM, SSM_STATE)

    return (y_prompt, y_sample, win_k_p, win_v_p, conv_p, ssm_p, win_k_s, win_v_s, conv_s, ssm_s_state)
```

```python
import functools
import math

import jax
import jax.numpy as jnp
from jax import lax
from jax.experimental import pallas as pl
from jax.experimental.pallas import tpu as pltpu
from jax.experimental.pallas import tpu_sc as plsc

F32 = jnp.float32
BF16 = jnp.bfloat16

D_MODEL = 1024
HEAD_DIM = 64
N_HEADS = 8
KV_HEADS = 2
WINDOW = 128
ATTN_WIDTH = N_HEADS * HEAD_DIM
QK_WIDTH = ATTN_WIDTH + KV_HEADS * HEAD_DIM
KV_WIDTH = KV_HEADS * HEAD_DIM
ATTN_SCALE = HEAD_DIM ** -0.5
ROPE_THETA = 10000.0
SSM_WIDTH = 512
SSM_HEADS = 8
SSM_HEAD_DIM = 64
SSM_GROUPS = 2
SSM_STATE = 128
CONV_K = 4
CONV_HALO = 8
CONV_DIM = SSM_WIDTH + 2 * SSM_GROUPS * SSM_STATE
SSD_CHUNK = 128
N_EGROUPS = 4
EXP_PER_GROUP = 8
N_EXPERTS = 32
EXPERT_FF = 128
EPS = 1e-6
PAST_LEN = 16384

LANES = 128
BF16_ROWS = 16
HEAD_PAIRS = SSM_HEADS // 2
EXPERT_LANE0 = 32
VMEM_LIMIT = 56 * 1024 * 1024
MOE_TILE = 512
MOE_SUBTILE = 128
ATTN_QBLOCKS = 4
COUNT_BLOCK = 256
SSD_CHUNKS_PER_STEP = 4
ROUTER_SUBTILE = 1024
INPROJ_SUBTILE = 512
SC_CORES = 2
SC_SUBCORES = 16
SC_WORKERS = SC_CORES * SC_SUBCORES
SC_CHUNK = 64

Q_END = ATTN_WIDTH
K_END = Q_END + KV_WIDTH
V_END = K_END + KV_WIDTH
Z_END = V_END + SSM_WIDTH
XBC_END = Z_END + CONV_DIM


def _dot(a, b):
    return jnp.dot(a, b, preferred_element_type=F32)


def _dot_nt(a, b):
    return lax.dot_general(a, b, (((1,), (1,)), ((), ())), preferred_element_type=F32)


def _split2(v):
    hi = v.astype(BF16)
    lo = (v - hi.astype(F32)).astype(BF16)
    return hi, lo


def _split3(v):
    hi = v.astype(BF16)
    r = v - hi.astype(F32)
    mid = r.astype(BF16)
    lo = (r - mid.astype(F32)).astype(BF16)
    return hi, mid, lo


def _silu(x):
    return x * jax.nn.sigmoid(x)


def _softplus(x):
    return jnp.maximum(x, 0.0) + jnp.log1p(jnp.exp(-jnp.abs(x)))


def _lane_bcast_pairs(v, n_pairs):
    r = v.shape[0]
    lo = lax.broadcasted_iota(jnp.int32, (r, LANES), 1) < HEAD_DIM
    slabs = []
    for j in range(n_pairs):
        a = jnp.broadcast_to(v[:, 2 * j:2 * j + 1], (r, LANES))
        b = jnp.broadcast_to(v[:, 2 * j + 1:2 * j + 2], (r, LANES))
        slabs.append(jnp.where(lo, a, b))
    return jnp.concatenate(slabs, axis=1)


def _causal_conv_silu(x_ext, convw_ref, convb_ref):
    halo = CONV_HALO
    x_raw = x_ext[halo:, :]
    conv = convb_ref[...] + x_raw * convw_ref[CONV_K - 1:CONV_K, :]
    for j in range(CONV_K - 1):
        shifted = pltpu.roll(x_ext, CONV_K - 1 - j, axis=0)[halo:, :]
        conv = conv + shifted * convw_ref[j:j + 1, :]
    return _silu(conv)


def _inproj_kernel(x_ref, n1_ref, wqk_ref, wv_ref, wz_ref, wxbc_ref, wdt_ref, dtb_ref, qkn_ref,
                   cos_ref, sin_ref, red_ref, exp_ref,
                   q_ref, k_ref, v_ref, z_ref, xbc_ref, dt_ref, k2_ref, v2_ref):
    tm = x_ref.shape[0]
    sub = min(tm, INPROJ_SUBTILE)
    lane = lax.broadcasted_iota(jnp.int32, (sub, LANES), 1)
    first_half = (lane % HEAD_DIM) < (HEAD_DIM // 2)
    for s in range(tm // sub):
        rows = slice(s * sub, (s + 1) * sub)
        x = x_ref[rows, :]
        ms = jnp.mean(x * x, axis=-1, keepdims=True)
        xn = (x * lax.rsqrt(ms + EPS) * n1_ref[...]).astype(BF16)
        v = _dot(xn, wv_ref[...])
        v_ref[rows, :] = v
        v2_ref[rows, :] = _pair_operands(v)
        z_ref[rows, :] = _dot(xn, wz_ref[...])
        xbc_ref[rows, :] = _dot(xn, wxbc_ref[...])
        dt_ref[rows, :] = _softplus(_dot(xn, wdt_ref[...]) + dtb_ref[...])
        qk = _dot(xn, wqk_ref[...])
        ss = _dot((qk * qk).astype(BF16), red_ref[...])
        inv = lax.rsqrt(ss * (1.0 / HEAD_DIM) + EPS)
        inv_hi, inv_lo = _split2(inv)
        inv_x = _dot(inv_hi, exp_ref[...]) + _dot(inv_lo, exp_ref[...])
        qkn = qk * inv_x * qkn_ref[...]
        cos = cos_ref[rows, :]
        sin = sin_ref[rows, :]
        for c in range(QK_WIDTH // LANES):
            xc = qkn[:, c * LANES:(c + 1) * LANES]
            partner = jnp.where(first_half,
                                pltpu.roll(xc, LANES - HEAD_DIM // 2, axis=1),
                                pltpu.roll(xc, HEAD_DIM // 2, axis=1))
            rot = xc * cos + partner * sin
            if c < ATTN_WIDTH // LANES:
                q_ref[rows, c * LANES:(c + 1) * LANES] = (rot * ATTN_SCALE).astype(BF16)
            else:
                k_ref[rows, :] = rot
                k2_ref[rows, :] = _pair_operands(rot)


def _inproj(x2d, w, cos_tab, sin_tab, tm, n_pos_blocks):
    t = x2d.shape[0]
    grid = (t // tm,)
    tok = lambda i: (i, 0)
    const = lambda i: (0, 0)
    pos = lambda i: (i % n_pos_blocks, 0)
    full = lambda a: pl.BlockSpec(a.shape, const)
    rows = lambda width, dtype: (jax.ShapeDtypeStruct((t, width), dtype), pl.BlockSpec((tm, width), tok))
    outs = [rows(ATTN_WIDTH, BF16), rows(KV_WIDTH, F32), rows(KV_WIDTH, F32), rows(SSM_WIDTH, F32)]
    operands = [x2d, w["norm1"], w["wqk"], w["wv"], w["wz"], w["wxbc"], w["wdt"], w["dt_bias"], w["qkn"],
                cos_tab, sin_tab, w["red"], w["exp"]]
    in_specs = [
        pl.BlockSpec((tm, D_MODEL), tok),
        full(w["norm1"]), full(w["wqk"]), full(w["wv"]), full(w["wz"]), full(w["wxbc"]),
        full(w["wdt"]), full(w["dt_bias"]), full(w["qkn"]),
        pl.BlockSpec((tm, LANES), pos), pl.BlockSpec((tm, LANES), pos),
        full(w["red"]), full(w["exp"]),
    ]
    outs += [rows(CONV_DIM, F32), rows(LANES, F32), rows(4 * LANES, BF16), rows(4 * LANES, BF16)]
    return pl.pallas_call(
        _inproj_kernel,
        out_shape=tuple(o[0] for o in outs),
        grid=grid,
        in_specs=in_specs,
        out_specs=tuple(o[1] for o in outs),
        compiler_params=pltpu.CompilerParams(dimension_semantics=("parallel",),
                                             vmem_limit_bytes=VMEM_LIMIT),
        name="inproj",
    )(*operands)


def _pair_operands(kv):
    lo = lax.broadcasted_iota(jnp.int32, kv.shape, 1) < HEAD_DIM
    swapped = pltpu.roll(kv, HEAD_DIM, axis=1)
    parts = [jnp.where(lo, kv, 0.0), jnp.where(lo, 0.0, swapped), jnp.where(lo, swapped, 0.0), jnp.where(lo, 0.0, kv)]
    return jnp.concatenate(parts, axis=1).astype(BF16)


def _pair_rhs(blk, g):
    return jnp.concatenate([blk[:, 2 * g * LANES:(2 * g + 1) * LANES],
                            blk[:, (2 * g + 1) * LANES:(2 * g + 2) * LANES]], axis=0)


def _attn_kernel(sink_ref, q_ref, kc_ref, kp_ref, vc_ref, vp_ref, o_ref):
    blk = WINDOW
    n_sub = q_ref.shape[0] // blk
    first_step = pl.program_id(1) == 0
    qi = lax.broadcasted_iota(jnp.int32, (blk, 2 * blk), 0)
    kj = lax.broadcasted_iota(jnp.int32, (blk, 2 * blk), 1) % blk
    cur_ok = kj <= qi
    seq_start_ok = jnp.logical_or(cur_ok, jnp.logical_not(first_step))
    lo = lax.broadcasted_iota(jnp.int32, (blk, LANES), 1) < HEAD_DIM
    k_blocks = [kp_ref[...]] + [kc_ref[u * blk:(u + 1) * blk, :] for u in range(n_sub)]
    v_blocks = [vp_ref[...]] + [vc_ref[u * blk:(u + 1) * blk, :] for u in range(n_sub)]
    for g in range(KV_HEADS):
        k2 = [_pair_rhs(b, g) for b in k_blocks]
        v2 = [_pair_rhs(b, g) for b in v_blocks]
        n_pairs = N_HEADS // KV_HEADS // 2
        for u in range(n_sub):
            rows = slice(u * blk, (u + 1) * blk)
            q_all = jnp.concatenate([q_ref[rows, (g * n_pairs + r) * LANES:(g * n_pairs + r + 1) * LANES]
                                     for r in range(n_pairs)], axis=0)
            s_all = _dot_nt(q_all, jnp.concatenate([k2[u + 1], k2[u]], axis=0))
            p_rows = []
            den_rows = []
            for r in range(n_pairs):
                pair = g * n_pairs + r
                s_cur = s_all[r * blk:(r + 1) * blk, :2 * blk]
                s_prev = s_all[r * blk:(r + 1) * blk, 2 * blk:]
                s = jnp.where(cur_ok, s_cur, s_prev)
                if u == 0:
                    s = jnp.where(seq_start_ok, s, -jnp.inf)
                ps = []
                dens = []
                for hh in range(2):
                    sink = sink_ref[2 * pair + hh]
                    sh = s[:, hh * blk:(hh + 1) * blk]
                    m = jnp.maximum(jnp.max(sh, axis=-1, keepdims=True), sink)
                    p = jnp.exp(sh - m)
                    dens.append(jnp.sum(p, axis=-1, keepdims=True) + jnp.exp(sink - m))
                    ps.append(p)
                p2 = jnp.concatenate(ps, axis=1)
                p_rows.append(jnp.concatenate([jnp.where(cur_ok, p2, 0.0), jnp.where(cur_ok, 0.0, p2)],
                                              axis=1).astype(BF16))
                den_rows.append(jnp.where(lo, dens[0], dens[1]))
            o_all = _dot(jnp.concatenate(p_rows, axis=0), jnp.concatenate([v2[u + 1], v2[u]], axis=0))
            for r in range(n_pairs):
                pair = g * n_pairs + r
                o2 = o_all[r * blk:(r + 1) * blk, :]
                o_ref[rows, pair * LANES:(pair + 1) * LANES] = (o2 / den_rows[r]).astype(BF16)


def _attn_prompt(q, k, v, sinks, batch, seq):
    n_sub = ATTN_QBLOCKS if seq % (ATTN_QBLOCKS * WINDOW) == 0 else 1
    rows = n_sub * WINDOW
    nb = seq // rows
    cur = lambda b, j, s: (b * nb + j, 0)
    prev = lambda b, j, s: (jnp.maximum((b * nb + j) * n_sub - 1, 0), 0)
    return pl.pallas_call(
        _attn_kernel,
        out_shape=jax.ShapeDtypeStruct((batch * seq, ATTN_WIDTH), BF16),
        grid_spec=pltpu.PrefetchScalarGridSpec(
            num_scalar_prefetch=1,
            grid=(batch, nb),
            in_specs=[
                pl.BlockSpec((rows, ATTN_WIDTH), cur),
                pl.BlockSpec((rows, 4 * LANES), cur), pl.BlockSpec((WINDOW, 4 * LANES), prev),
                pl.BlockSpec((rows, 4 * LANES), cur), pl.BlockSpec((WINDOW, 4 * LANES), prev),
            ],
            out_specs=pl.BlockSpec((rows, ATTN_WIDTH), cur),
        ),
        compiler_params=pltpu.CompilerParams(dimension_semantics=("parallel", "parallel"),
                                             vmem_limit_bytes=VMEM_LIMIT),
        name="attn_prompt",
    )(sinks, q, k, k, v, v)


def _ssd_kernel(xbc_ref, z_ref, dt_ref, convw_ref, convb_ref, alog_ref, dskip_ref, nw_ref,
                tri_ref, expand_ref, y_ref, st_ref, buf_ref, state_ref):
    c = pl.program_id(1)
    cl = SSD_CHUNK
    n_sub = xbc_ref.shape[0] // cl
    halo = CONV_HALO

    @pl.when(c == 0)
    def _():
        buf_ref[...] = jnp.zeros(buf_ref.shape, F32)
        state_ref[...] = jnp.zeros(state_ref.shape, F32)

    lane = lax.broadcasted_iota(jnp.int32, (1, LANES), 1)
    a_neg = jnp.where(lane < SSM_HEADS, -jnp.exp(alog_ref[...]), 0.0)
    tri = tri_ref[...]
    for u in range(n_sub):
        rows = slice(u * cl, (u + 1) * cl)
        if u == 0:
            x_ext = jnp.concatenate([buf_ref[...], xbc_ref[rows, :]], axis=0)
        else:
            x_ext = xbc_ref[u * cl - halo:(u + 1) * cl, :]
        _ssd_chunk(x_ext, z_ref[rows, :], dt_ref[rows, :], a_neg, tri, convw_ref, convb_ref, dskip_ref, nw_ref,
                   expand_ref, y_ref.at[rows, :], state_ref)
    buf_ref[...] = xbc_ref[n_sub * cl - halo:n_sub * cl, :]

    @pl.when(c == pl.num_programs(1) - 1)
    def _():
        st_ref[0] = state_ref[...]


def _ssd_chunk(x_ext, z, dt, a_neg, tri, convw_ref, convb_ref, dskip_ref, nw_ref, expand_ref, y_ref, state_ref):
    cl = SSD_CHUNK
    act = _causal_conv_silu(x_ext, convw_ref, convb_ref)
    xs = act[:, :SSM_WIDTH]
    bm = act[:, SSM_WIDTH:SSM_WIDTH + SSM_GROUPS * SSM_STATE].astype(BF16)
    cm = act[:, SSM_WIDTH + SSM_GROUPS * SSM_STATE:].astype(BF16)

    dta = dt * a_neg
    p_hi, p_mid, p_lo = _split3(dta)
    a_col = _dot(tri, p_hi) + _dot(tri, p_mid) + _dot(tri, p_lo)
    a_last = a_col[cl - 1:cl, :]
    a_row = a_col.T
    per_head = jnp.concatenate([dt, jnp.exp(a_col), jnp.exp(a_last - a_col)], axis=0)
    ph_hi, ph_lo = _split2(per_head)
    per_lane = _dot(ph_hi, expand_ref[...]) + _dot(ph_lo, expand_ref[...])
    dt_x = per_lane[:cl]
    ecol_x = per_lane[cl:2 * cl]
    dte_x = per_lane[2 * cl:]
    e_last = jnp.exp(a_last)
    xdt = xs * dt_x

    li = lax.broadcasted_iota(jnp.int32, (cl, cl), 0)
    si = lax.broadcasted_iota(jnp.int32, (cl, cl), 1)
    causal = si <= li
    lo = lax.broadcasted_iota(jnp.int32, (cl, LANES), 1) < SSM_HEAD_DIM
    row_lo = lax.broadcasted_iota(jnp.int32, (LANES, SSM_STATE), 0) < SSM_HEAD_DIM

    ys = []
    for g in range(SSM_GROUPS):
        b_g = bm[:, g * SSM_STATE:(g + 1) * SSM_STATE]
        c_g = cm[:, g * SSM_STATE:(g + 1) * SSM_STATE]
        cb = _dot_nt(c_g, b_g)
        for r in range(HEAD_PAIRS // SSM_GROUPS):
            j = g * (HEAD_PAIRS // SSM_GROUPS) + r
            sl = slice(j * LANES, (j + 1) * LANES)
            xdt_p = xdt[:, sl]
            ms = []
            for hh in range(2):
                h = 2 * j + hh
                seg = a_col[:, h:h + 1] - a_row[h:h + 1, :]
                ms.append(cb * jnp.exp(jnp.where(causal, seg, -jnp.inf)))
            m2 = jnp.concatenate(ms, axis=1).astype(BF16)
            rhs = jnp.concatenate([jnp.where(lo, xdt_p, 0.0), jnp.where(lo, 0.0, xdt_p)],
                                  axis=0).astype(BF16)
            y_diag = _dot(m2, rhs)
            st = state_ref[j]
            y_off = _dot_nt(c_g, st.astype(BF16)) * ecol_x[:, sl]
            xdt_e = (xdt_p * dte_x[:, sl]).T.astype(BF16)
            d_a = e_last[:, 2 * j:2 * j + 1]
            d_b = e_last[:, 2 * j + 1:2 * j + 2]
            decay = jnp.where(row_lo, jnp.broadcast_to(d_a, row_lo.shape), jnp.broadcast_to(d_b, row_lo.shape))
            state_ref[j] = decay * st + _dot(xdt_e, b_g)
            ys.append(y_diag + y_off + dskip_ref[:, sl] * xs[:, sl])
    y = jnp.concatenate(ys, axis=1)
    gated = y * _silu(z)
    gw = SSM_WIDTH // SSM_GROUPS
    outs = []
    for g in range(SSM_GROUPS):
        gg = gated[:, g * gw:(g + 1) * gw]
        outs.append(gg * lax.rsqrt(jnp.mean(gg * gg, axis=-1, keepdims=True) + EPS))
    y_ref[...] = (jnp.concatenate(outs, axis=1) * nw_ref[...]).astype(BF16)


def _ssd_prompt(xbc, z, dt, w, batch, seq):
    n_sub = SSD_CHUNKS_PER_STEP if seq % (SSD_CHUNKS_PER_STEP * SSD_CHUNK) == 0 else 1
    rows = n_sub * SSD_CHUNK
    nc = seq // rows
    tok = lambda b, c: (b * nc + c, 0)
    const = lambda b, c: (0, 0)
    full = lambda a: pl.BlockSpec(a.shape, const)
    return pl.pallas_call(
        _ssd_kernel,
        out_shape=(jax.ShapeDtypeStruct((batch * seq, SSM_WIDTH), BF16),
                   jax.ShapeDtypeStruct((batch, HEAD_PAIRS, LANES, SSM_STATE), F32)),
        grid=(batch, nc),
        in_specs=[
            pl.BlockSpec((rows, CONV_DIM), tok), pl.BlockSpec((rows, SSM_WIDTH), tok),
            pl.BlockSpec((rows, LANES), tok),
            full(w["conv_w"]), full(w["conv_b"]), full(w["a_log"]),
            full(w["d_skip"]), full(w["ssm_norm"]), full(w["tri"]), full(w["expand"]),
        ],
        out_specs=(pl.BlockSpec((rows, SSM_WIDTH), tok),
                   pl.BlockSpec((1, HEAD_PAIRS, LANES, SSM_STATE), lambda b, c: (b, 0, 0, 0))),
        scratch_shapes=[pltpu.VMEM((CONV_HALO, CONV_DIM), F32),
                        pltpu.VMEM((HEAD_PAIRS, LANES, SSM_STATE), F32)],
        compiler_params=pltpu.CompilerParams(dimension_semantics=("parallel", "arbitrary"),
                                             vmem_limit_bytes=VMEM_LIMIT),
        name="ssd_prompt",
    )(xbc, z, dt, w["conv_w"], w["conv_b"], w["a_log"], w["d_skip"], w["ssm_norm"], w["tri"], w["expand"])


def _attn_sample_kernel(qx_ref, kc_ref, kn_ref, vc_ref, vn_ref, sink_ref, ko_ref, vo_ref, o_ref):
    bs = qx_ref.shape[0]
    w = kc_ref.shape[1]
    sink = sink_ref[...]
    lo = lax.broadcasted_iota(jnp.int32, (1, LANES), 1) < HEAD_DIM
    for i in range(bs):
        ko_ref[i, 0:w - 1, :] = kc_ref[i, 1:w, :]
        ko_ref[i, w - 1:w, :] = kn_ref[i]
        vo_ref[i, 0:w - 1, :] = vc_ref[i, 1:w, :]
        vo_ref[i, w - 1:w, :] = vn_ref[i]
        s = _dot_nt(qx_ref[i], ko_ref[i].astype(BF16))
        m = jnp.maximum(jnp.max(s, axis=-1, keepdims=True), sink)
        p = jnp.exp(s - m)
        den = jnp.sum(p, axis=-1, keepdims=True) + jnp.exp(sink - m)
        o = _dot(p.astype(BF16), vo_ref[i].astype(BF16)) / den
        o_sw = pltpu.roll(o, HEAD_DIM, axis=1)
        for j in range(N_HEADS // 2):
            a, b = (o, o_sw) if j < N_HEADS // 4 else (o_sw, o)
            o_ref[i:i + 1, j * LANES:(j + 1) * LANES] = jnp.where(lo, a[2 * j:2 * j + 1], b[2 * j + 1:2 * j + 2])


def _attn_sample(qx, kc, kn, vc, vn, sink_x, bs):
    n, w = kc.shape[0], kc.shape[1]
    blk3 = lambda i: (i, 0, 0)
    return pl.pallas_call(
        _attn_sample_kernel,
        out_shape=(jax.ShapeDtypeStruct((n, w, KV_WIDTH), F32),
                   jax.ShapeDtypeStruct((n, w, KV_WIDTH), F32),
                   jax.ShapeDtypeStruct((n, ATTN_WIDTH), F32)),
        grid=(n // bs,),
        in_specs=[
            pl.BlockSpec((bs, BF16_ROWS, LANES), blk3),
            pl.BlockSpec((bs, w, KV_WIDTH), blk3), pl.BlockSpec((bs, 1, KV_WIDTH), blk3),
            pl.BlockSpec((bs, w, KV_WIDTH), blk3), pl.BlockSpec((bs, 1, KV_WIDTH), blk3),
            pl.BlockSpec(sink_x.shape, lambda i: (0, 0)),
        ],
        out_specs=(pl.BlockSpec((bs, w, KV_WIDTH), blk3), pl.BlockSpec((bs, w, KV_WIDTH), blk3),
                   pl.BlockSpec((bs, ATTN_WIDTH), lambda i: (i, 0))),
        compiler_params=pltpu.CompilerParams(dimension_semantics=("parallel",),
                                             vmem_limit_bytes=VMEM_LIMIT),
        name="attn_sample",
    )(qx, kc, kn, vc, vn, sink_x)


def _ssd_sample_kernel(xbc_ref, z_ref, dt_ref, cprev_ref, h0_ref, convw_ref, convb_ref, alog_ref,
                       dskip_ref, nw_ref, y_ref, cnew_ref, h1_ref):
    bs = xbc_ref.shape[0]
    x_raw = xbc_ref[...]
    conv = convb_ref[...] + x_raw * convw_ref[CONV_K - 1:CONV_K, :]
    for j in range(CONV_K - 1):
        conv = conv + cprev_ref[j] * convw_ref[j:j + 1, :]
    for j in range(CONV_K - 2):
        cnew_ref[j] = cprev_ref[j + 1]
    cnew_ref[CONV_K - 2] = x_raw
    act = _silu(conv)
    xs = act[:, :SSM_WIDTH]
    bm = act[:, SSM_WIDTH:SSM_WIDTH + SSM_GROUPS * SSM_STATE].astype(BF16)
    cm = act[:, SSM_WIDTH + SSM_GROUPS * SSM_STATE:].astype(BF16)
    lane = lax.broadcasted_iota(jnp.int32, (1, LANES), 1)
    a_neg = jnp.where(lane < SSM_HEADS, -jnp.exp(alog_ref[...]), 0.0)
    dt = dt_ref[...]
    dec = jnp.exp(dt * a_neg)
    xdt = xs * _lane_bcast_pairs(dt, HEAD_PAIRS)
    rowid = lax.broadcasted_iota(jnp.int32, (bs, LANES), 0)
    row_lo = lax.broadcasted_iota(jnp.int32, (LANES, SSM_STATE), 0) < SSM_HEAD_DIM
    ys = []
    for j in range(HEAD_PAIRS):
        g = j // (HEAD_PAIRS // SSM_GROUPS)
        sl = slice(j * LANES, (j + 1) * LANES)
        b_g = bm[:, g * SSM_STATE:(g + 1) * SSM_STATE]
        c_g = cm[:, g * SSM_STATE:(g + 1) * SSM_STATE]
        xdt_p = xdt[:, sl]
        y_p = jnp.zeros((bs, LANES), F32)
        for i in range(bs):
            xi = jnp.where(rowid == i, xdt_p, 0.0).T.astype(BF16)
            d_a = dec[i:i + 1, 2 * j:2 * j + 1]
            d_b = dec[i:i + 1, 2 * j + 1:2 * j + 2]
            decay = jnp.where(row_lo, jnp.broadcast_to(d_a, row_lo.shape), jnp.broadcast_to(d_b, row_lo.shape))
            new = decay * h0_ref[i, j] + _dot(xi, b_g)
            h1_ref[i, j] = new
            y_p = y_p + jnp.where(rowid == i, _dot_nt(c_g, new.astype(BF16)), 0.0)
        ys.append(y_p + dskip_ref[:, sl] * xs[:, sl])
    y = jnp.concatenate(ys, axis=1)
    gated = y * _silu(z_ref[...])
    gw = SSM_WIDTH // SSM_GROUPS
    outs = []
    for g in range(SSM_GROUPS):
        gg = gated[:, g * gw:(g + 1) * gw]
        outs.append(gg * lax.rsqrt(jnp.mean(gg * gg, axis=-1, keepdims=True) + EPS))
    y_ref[...] = (jnp.concatenate(outs, axis=1) * nw_ref[...]).astype(BF16)


def _ssd_sample(xbc, z, dt, cprev_t, h0, w, bs):
    n = xbc.shape[0]
    tok = lambda i: (i, 0)
    const = lambda i: (0, 0)
    full = lambda a: pl.BlockSpec(a.shape, const)
    return pl.pallas_call(
        _ssd_sample_kernel,
        out_shape=(jax.ShapeDtypeStruct((n, SSM_WIDTH), BF16),
                   jax.ShapeDtypeStruct((CONV_K - 1, n, CONV_DIM), F32),
                   jax.ShapeDtypeStruct((n, HEAD_PAIRS, LANES, SSM_STATE), F32)),
        grid=(n // bs,),
        in_specs=[
            pl.BlockSpec((bs, CONV_DIM), tok), pl.BlockSpec((bs, SSM_WIDTH), tok),
            pl.BlockSpec((bs, LANES), tok),
            pl.BlockSpec((CONV_K - 1, bs, CONV_DIM), lambda i: (0, i, 0)),
            pl.BlockSpec((bs, HEAD_PAIRS, LANES, SSM_STATE), lambda i: (i, 0, 0, 0)),
            full(w["conv_w"]), full(w["conv_b"]), full(w["a_log"]),
            full(w["d_skip"]), full(w["ssm_norm"]),
        ],
        out_specs=(pl.BlockSpec((bs, SSM_WIDTH), tok),
                   pl.BlockSpec((CONV_K - 1, bs, CONV_DIM), lambda i: (0, i, 0)),
                   pl.BlockSpec((bs, HEAD_PAIRS, LANES, SSM_STATE), lambda i: (i, 0, 0, 0))),
        compiler_params=pltpu.CompilerParams(dimension_semantics=("parallel",),
                                             vmem_limit_bytes=VMEM_LIMIT),
        name="ssd_sample",
    )(xbc, z, dt, cprev_t, h0, w["conv_w"], w["conv_b"], w["a_log"], w["d_skip"], w["ssm_norm"])


def _pack_bf16_pair(v):
    c = v.shape[1] // 2
    hi = lax.bitcast_convert_type(v[:, :c].astype(BF16).astype(F32), jnp.uint32)
    lo = lax.bitcast_convert_type(v[:, c:].astype(BF16).astype(F32), jnp.uint32)
    return hi | (lo >> 16)


def _unpack_bf16_pair(word):
    a = lax.bitcast_convert_type(word & jnp.uint32(0xFFFF0000), F32)
    b = lax.bitcast_convert_type(word << 16, F32)
    return a, b


def _outproj_router_kernel(x_ref, a_ref, s_ref, wo_a_ref, wo_s_ref, n2_ref, wr_ref, br_ref, tri_ref,
                           h_ref, t_ref, route_ref, cnt_ref, carry_ref):
    @pl.when(pl.program_id(0) == 0)
    def _():
        carry_ref[...] = jnp.zeros(carry_ref.shape, F32)

    tm = x_ref.shape[0]
    sub = min(tm, ROUTER_SUBTILE)
    carry = carry_ref[0:1, :]
    for s in range(tm // sub):
        rows = slice(s * sub, (s + 1) * sub)
        carry = _outproj_route_rows(x_ref[rows, :], a_ref[rows, :], s_ref[rows, :], wo_a_ref, wo_s_ref, n2_ref,
                                    wr_ref, br_ref, tri_ref, carry, h_ref.at[rows, :], t_ref.at[rows, :],
                                    route_ref.at[rows, :])
    carry_ref[0:1, :] = carry
    cnt_ref[...] = jnp.broadcast_to(carry, cnt_ref.shape)


def _outproj_route_rows(x, a, s, wo_a_ref, wo_s_ref, n2_ref, wr_ref, br_ref, tri_ref, carry, h_ref, t_ref, route_ref):
    h = x + _dot(a.astype(BF16), wo_a_ref[...]) + _dot(s.astype(BF16), wo_s_ref[...])
    h_ref[...] = h
    ms = jnp.mean(h * h, axis=-1, keepdims=True)
    t = h * lax.rsqrt(ms + EPS) * n2_ref[...]
    t_hi, t_lo = _split2(t)
    t_ref[...] = _pack_bf16_pair(t)
    both = _dot(t_hi, wr_ref[...])
    logits = both[:, :LANES] + both[:, LANES:] + _dot(t_lo, wr_ref[:, :LANES]) + br_ref[...]
    lane = lax.broadcasted_iota(jnp.int32, logits.shape, 1)
    lane_f = lane.astype(F32)
    big = float(LANES)
    ninf = -jnp.inf
    glog = jnp.where(lane < N_EGROUPS, logits, ninf)
    gmax = jnp.max(glog, axis=-1, keepdims=True)
    g_top = 1.0 / jnp.sum(jnp.exp(glog - gmax), axis=-1, keepdims=True)
    g_idx = jnp.min(jnp.where(glog == gmax, lane_f, big), axis=-1, keepdims=True)
    e_of_lane = lane - EXPERT_LANE0
    in_grp = jnp.logical_and(jnp.logical_and(e_of_lane >= 0, e_of_lane < N_EXPERTS),
                             (e_of_lane // EXP_PER_GROUP).astype(F32) == g_idx)
    ml = jnp.where(in_grp, logits, ninf)
    m1 = jnp.max(ml, axis=-1, keepdims=True)
    i1 = jnp.min(jnp.where(ml == m1, lane_f, big), axis=-1, keepdims=True)
    ml2 = jnp.where(lane_f == i1, ninf, ml)
    m2 = jnp.max(ml2, axis=-1, keepdims=True)
    i2 = jnp.min(jnp.where(ml2 == m2, lane_f, big), axis=-1, keepdims=True)
    r = jnp.exp(m2 - m1)
    w1 = g_top / (1.0 + r)
    w2 = g_top * r / (1.0 + r)
    is1 = lane_f == i1
    is2 = lane_f == i2
    onehot = jnp.where(jnp.logical_or(is1, is2), 1.0, 0.0)
    onehot_bf = onehot.astype(BF16)
    n_rows = onehot.shape[0]
    cb = tri_ref.shape[0]
    cums = []
    for blk in range(n_rows // cb):
        c = _dot(tri_ref[...], onehot_bf[blk * cb:(blk + 1) * cb, :]) + carry
        carry = c[cb - 1:cb, :]
        cums.append(c)
    before = jnp.concatenate(cums, axis=0) - onehot
    rank1 = jnp.sum(jnp.where(is1, before, 0.0), axis=-1, keepdims=True)
    rank2 = jnp.sum(jnp.where(is2, before, 0.0), axis=-1, keepdims=True)
    fields = (i1 - EXPERT_LANE0, i2 - EXPERT_LANE0, w1, w2, rank1, rank2)
    route = jnp.zeros(logits.shape, F32)
    for pos, val in enumerate(fields):
        route = jnp.where(lane == pos, val, route)
    route_ref[...] = route
    return carry


def _outproj_router(x2d, attn, ssm, w, tm):
    t = x2d.shape[0]
    tok = lambda i: (i, 0)
    const = lambda i: (0, 0)
    full = lambda a: pl.BlockSpec(a.shape, const)
    idx = jnp.arange(min(tm, COUNT_BLOCK))
    tri = (idx[None, :] <= idx[:, None]).astype(BF16)
    return pl.pallas_call(
        _outproj_router_kernel,
        out_shape=(jax.ShapeDtypeStruct((t, D_MODEL), F32), jax.ShapeDtypeStruct((t, D_MODEL // 2), jnp.uint32),
                   jax.ShapeDtypeStruct((t, LANES), F32), jax.ShapeDtypeStruct((8, LANES), F32)),
        grid=(t // tm,),
        in_specs=[
            pl.BlockSpec((tm, D_MODEL), tok), pl.BlockSpec((tm, ATTN_WIDTH), tok),
            pl.BlockSpec((tm, SSM_WIDTH), tok),
            full(w["wo_a"]), full(w["wo_s"]), full(w["norm2"]), full(w["wr"]),
            full(w["br"]), full(tri),
        ],
        out_specs=(pl.BlockSpec((tm, D_MODEL), tok), pl.BlockSpec((tm, D_MODEL // 2), tok),
                   pl.BlockSpec((tm, LANES), tok), pl.BlockSpec((8, LANES), const)),
        scratch_shapes=[pltpu.VMEM((8, LANES), F32)],
        compiler_params=pltpu.CompilerParams(dimension_semantics=("arbitrary",),
                                             vmem_limit_bytes=VMEM_LIMIT),
        name="outproj_router",
    )(x2d, attn, ssm, w["wo_a"], w["wo_s"], w["norm2"], w["wr"], w["br"], tri)


def _expert_hidden(t_a, t_b, w1):
    half = D_MODEL // 2
    gu = _dot(t_a, w1[:half]) + _dot(t_b, w1[half:])
    return _silu(gu[:, :EXPERT_FF]) * gu[:, EXPERT_FF:]


def _gate_up_bf16(wg_ref, wu_ref):
    return jnp.concatenate([wg_ref[0].astype(BF16), wu_ref[0].astype(BF16)], axis=1)


def _moe_dense_kernel(h_ref, t_ref, route_ref, wg_ref, wu_ref, wd_ref, y_ref):
    e = pl.program_id(0)

    @pl.when(e == 0)
    def _():
        y_ref[...] = h_ref[...]

    t_a, t_b = _unpack_bf16_pair(t_ref[...])
    route = route_ref[...]
    e1, e2, g1, g2 = route[:, 0:1], route[:, 1:2], route[:, 2:3], route[:, 3:4]
    e_f = e.astype(F32)
    hid = _expert_hidden(t_a.astype(BF16), t_b.astype(BF16), _gate_up_bf16(wg_ref, wu_ref))
    c_e = jnp.where(e1 == e_f, g1, 0.0) + jnp.where(e2 == e_f, g2, 0.0)
    y_ref[...] += _dot((hid * c_e).astype(BF16), wd_ref[0].astype(BF16))


def _moe_dense(h, t, route, w):
    n = h.shape[0]
    whole = lambda e: (0, 0)
    by_expert = lambda e: (e, 0, 0)
    return pl.pallas_call(
        _moe_dense_kernel,
        out_shape=jax.ShapeDtypeStruct((n, D_MODEL), F32),
        grid=(N_EXPERTS,),
        in_specs=[pl.BlockSpec((n, D_MODEL), whole), pl.BlockSpec((n, D_MODEL // 2), whole),
                  pl.BlockSpec((n, LANES), whole),
                  pl.BlockSpec((1, D_MODEL, EXPERT_FF), by_expert), pl.BlockSpec((1, D_MODEL, EXPERT_FF), by_expert),
                  pl.BlockSpec((1, EXPERT_FF, D_MODEL), by_expert)],
        out_specs=pl.BlockSpec((n, D_MODEL), whole),
        compiler_params=pltpu.CompilerParams(dimension_semantics=("arbitrary",),
                                             vmem_limit_bytes=VMEM_LIMIT),
        name="moe_dense",
    )(h, t, route, w["w_gate"], w["w_up"], w["w_down"])


def _sc_scatter_rows(src, pos1, pos2, n_out):
    t, width = src.shape
    rows_per_worker = t // SC_WORKERS
    n_chunks = rows_per_worker // SC_CHUNK
    assert t == SC_WORKERS * SC_CHUNK * n_chunks
    mesh = plsc.VectorSubcoreMesh(core_axis_name="c", subcore_axis_name="s")

    @functools.partial(
        pl.kernel, mesh=mesh,
        out_type=jax.ShapeDtypeStruct((n_out, width), src.dtype),
        scratch_types=[pltpu.VMEM((SC_CHUNK,), jnp.int32), pltpu.VMEM((SC_CHUNK,), jnp.int32),
                       pltpu.VMEM((SC_CHUNK, width), src.dtype), pltpu.SemaphoreType.DMA],
    )
    def scatter_kernel(src_hbm, p1_hbm, p2_hbm, out_hbm, i1_v, i2_v, rows_v, sem):
        wid = lax.axis_index("s") * SC_CORES + lax.axis_index("c")
        base = wid * rows_per_worker

        @pl.loop(0, n_chunks)
        def _(c):
            off = pl.multiple_of(base + c * SC_CHUNK, 8)
            pltpu.sync_copy(p1_hbm.at[pl.ds(off, SC_CHUNK)], i1_v)
            pltpu.sync_copy(p2_hbm.at[pl.ds(off, SC_CHUNK)], i2_v)
            pltpu.sync_copy(src_hbm.at[pl.ds(off, SC_CHUNK)], rows_v)
            pltpu.async_copy(rows_v, out_hbm.at[i1_v], sem).wait()
            pltpu.async_copy(rows_v, out_hbm.at[i2_v], sem).wait()

    return scatter_kernel(src, pos1, pos2)


def _sc_gather_rows(table, idx):
    n, width = idx.shape[0], table.shape[1]
    rows_per_worker = n // SC_WORKERS
    n_chunks = rows_per_worker // SC_CHUNK
    assert n == SC_WORKERS * SC_CHUNK * n_chunks and n_chunks % 2 == 0
    mesh = plsc.VectorSubcoreMesh(core_axis_name="c", subcore_axis_name="s")

    @functools.partial(
        pl.kernel, mesh=mesh,
        out_type=jax.ShapeDtypeStruct((n, width), table.dtype),
        scratch_types=[pltpu.VMEM((2, SC_CHUNK), jnp.int32), pltpu.VMEM((2, SC_CHUNK, width), table.dtype),
                       pltpu.SemaphoreType.DMA, pltpu.SemaphoreType.DMA,
                       pltpu.SemaphoreType.DMA, pltpu.SemaphoreType.DMA],
    )
    def gather_kernel(table_hbm, idx_hbm, out_hbm, idx_v, rows_v, g0, g1, w0, w1):
        wid = lax.axis_index("s") * SC_CORES + lax.axis_index("c")
        base = wid * rows_per_worker
        gsem = (g0, g1)
        wsem = (w0, w1)

        def gather_copy(slot):
            return pltpu.make_async_copy(table_hbm.at[idx_v.at[slot]], rows_v.at[slot], gsem[slot])

        def write_copy(c, slot):
            off = pl.multiple_of(base + c * SC_CHUNK, 8)
            return pltpu.make_async_copy(rows_v.at[slot], out_hbm.at[pl.ds(off, SC_CHUNK)], wsem[slot])

        def start_gather(c, slot):
            off = pl.multiple_of(base + c * SC_CHUNK, 8)
            pltpu.sync_copy(idx_hbm.at[pl.ds(off, SC_CHUNK)], idx_v.at[slot])
            gather_copy(slot).start()

        start_gather(0, 0)

        @pl.loop(0, n_chunks, step=2)
        def _(c):
            @pl.when(c > 0)
            def _():
                write_copy(c - 1, 1).wait()

            start_gather(c + 1, 1)
            gather_copy(0).wait()
            write_copy(c, 0).start()
            gather_copy(1).wait()
            write_copy(c + 1, 1).start()
            write_copy(c, 0).wait()

            @pl.when(c + 2 < n_chunks)
            def _():
                start_gather(c + 2, 0)

        write_copy(n_chunks - 1, 1).wait()

    return gather_kernel(table, idx)


def _moe_grouped_kernel(te_ref, nt_ref, order_ref, x_ref, wg_ref, wu_ref, wd_ref, o_ref, w1_bf_ref, w2_bf_ref):
    del order_ref
    i = pl.program_id(0)

    @pl.when(jnp.logical_or(i == 0, te_ref[i] != te_ref[jnp.maximum(i - 1, 0)]))
    def _():
        w1_bf_ref[...] = _gate_up_bf16(wg_ref, wu_ref)
        w2_bf_ref[...] = wd_ref[0].astype(BF16)

    @pl.when(i < nt_ref[0])
    def _():
        for s in range(MOE_TILE // MOE_SUBTILE):
            rows = slice(s * MOE_SUBTILE, (s + 1) * MOE_SUBTILE)
            t_a, t_b = _unpack_bf16_pair(x_ref[rows, :])
            hid = _expert_hidden(t_a.astype(BF16), t_b.astype(BF16), w1_bf_ref[...])
            o_ref[rows, :] = _pack_bf16_pair(_dot(hid.astype(BF16), w2_bf_ref[...]))


def _moe_grouped(xs, tile_expert, n_tiles, order, w):
    rows = xs.shape[0]
    row = lambda i, te, nt, od: (i, 0)
    by_expert = lambda i, te, nt, od: (te[i], 0, 0)
    return pl.pallas_call(
        _moe_grouped_kernel,
        out_shape=jax.ShapeDtypeStruct((rows, D_MODEL // 2), jnp.uint32),
        grid_spec=pltpu.PrefetchScalarGridSpec(
            num_scalar_prefetch=3,
            grid=(rows // MOE_TILE,),
            in_specs=[pl.BlockSpec((MOE_TILE, D_MODEL // 2), row),
                      pl.BlockSpec((1, D_MODEL, EXPERT_FF), by_expert),
                      pl.BlockSpec((1, D_MODEL, EXPERT_FF), by_expert),
                      pl.BlockSpec((1, EXPERT_FF, D_MODEL), by_expert)],
            out_specs=pl.BlockSpec((MOE_TILE, D_MODEL // 2), row),
            scratch_shapes=[pltpu.VMEM((D_MODEL, 2 * EXPERT_FF), BF16), pltpu.VMEM((EXPERT_FF, D_MODEL), BF16)],
        ),
        compiler_params=pltpu.CompilerParams(dimension_semantics=("arbitrary",),
                                             vmem_limit_bytes=VMEM_LIMIT),
        name="moe_grouped",
    )(tile_expert, n_tiles, order, xs, w["w_gate"], w["w_up"], w["w_down"])


def _moe_combine_kernel(h_ref, z1_ref, z2_ref, route_ref, y_ref):
    route = route_ref[...]
    g1, g2 = route[:, 2:3], route[:, 3:4]
    half = D_MODEL // 2
    a1, b1 = _unpack_bf16_pair(z1_ref[...])
    a2, b2 = _unpack_bf16_pair(z2_ref[...])
    y_ref[:, :half] = h_ref[:, :half] + g1 * a1 + g2 * a2
    y_ref[:, half:] = h_ref[:, half:] + g1 * b1 + g2 * b2


def _moe_combine(h, z, route, tm):
    t = h.shape[0]
    nb = t // tm
    tok = lambda i: (i, 0)
    return pl.pallas_call(
        _moe_combine_kernel,
        out_shape=jax.ShapeDtypeStruct((t, D_MODEL), F32),
        grid=(nb,),
        in_specs=[pl.BlockSpec((tm, D_MODEL), tok), pl.BlockSpec((tm, D_MODEL // 2), tok),
                  pl.BlockSpec((tm, D_MODEL // 2), lambda i: (i + nb, 0)), pl.BlockSpec((tm, LANES), tok)],
        out_specs=pl.BlockSpec((tm, D_MODEL), tok),
        compiler_params=pltpu.CompilerParams(dimension_semantics=("parallel",),
                                             vmem_limit_bytes=VMEM_LIMIT),
        name="moe_combine",
    )(h, z, z, route)


def _route_pos_kernel(route_ref, cnt_ref, upper_ref, pos_ref):
    tm = route_ref.shape[0]
    cnt = cnt_ref[...]
    padded = jnp.floor((cnt + float(MOE_TILE - 1)) * (1.0 / MOE_TILE)) * float(MOE_TILE)
    p_hi, p_mid, p_lo = _split3(padded)
    upper = upper_ref[...]
    starts = (_dot(p_hi, upper) + _dot(p_mid, upper) + _dot(p_lo, upper))[0:1, :]
    route = route_ref[...]
    lane_f = lax.broadcasted_iota(jnp.int32, (tm, LANES), 1).astype(F32)
    diag = (lax.broadcasted_iota(jnp.int32, (LANES, LANES), 0)
            == lax.broadcasted_iota(jnp.int32, (LANES, LANES), 1))
    for k in range(2):
        e_lane = route[:, k:k + 1] + float(EXPERT_LANE0)
        pos = jnp.sum(jnp.where(lane_f == e_lane, starts, 0.0), axis=-1, keepdims=True) + route[:, 4 + k:5 + k]
        for r in range(tm // LANES):
            col = pos[r * LANES:(r + 1) * LANES, :]
            row = jnp.sum(jnp.where(diag, col, 0.0), axis=0, keepdims=True)
            pos_ref[k, r:r + 1, :] = row.astype(jnp.int32)


def _route_positions(route, counts, tm):
    t = route.shape[0]
    idx = jnp.arange(LANES)
    upper = (idx[:, None] < idx[None, :]).astype(BF16)
    return pl.pallas_call(
        _route_pos_kernel,
        out_shape=jax.ShapeDtypeStruct((2, t // LANES, LANES), jnp.int32),
        grid=(t // tm,),
        in_specs=[pl.BlockSpec((tm, LANES), lambda i: (i, 0)), pl.BlockSpec((8, LANES), lambda i: (0, 0)),
                  pl.BlockSpec((LANES, LANES), lambda i: (0, 0))],
        out_specs=pl.BlockSpec((2, tm // LANES, LANES), lambda i: (0, i, 0)),
        compiler_params=pltpu.CompilerParams(dimension_semantics=("parallel",),
                                             vmem_limit_bytes=VMEM_LIMIT),
        name="route_positions",
    )(route, counts, upper)


def _moe_routed(h, t_packed, route, counts, w, tm, run_first):
    t = h.shape[0]
    pos = _route_positions(route, counts, min(32 * LANES, t))
    pos1 = pos[0].reshape(t)
    pos2 = pos[1].reshape(t)
    cnt = counts[0, EXPERT_LANE0:EXPERT_LANE0 + N_EXPERTS].astype(jnp.int32)
    padded = (cnt + MOE_TILE - 1) // MOE_TILE * MOE_TILE
    ends = jnp.cumsum(padded)
    n_rows = 2 * t + N_EXPERTS * MOE_TILE
    n_tiles = ends[N_EXPERTS - 1] // MOE_TILE
    tile_start = jnp.arange(n_rows // MOE_TILE, dtype=jnp.int32) * MOE_TILE
    tile_start = jnp.minimum(tile_start, ends[N_EXPERTS - 1] - MOE_TILE)
    tile_expert = jnp.sum((tile_start[:, None] >= ends[None, :]).astype(jnp.int32), axis=1)
    xs = _sc_scatter_rows(t_packed, pos1, pos2, n_rows)
    order = lax.bitcast_convert_type(run_first.reshape(-1)[:1].astype(F32), jnp.int32)
    out = _moe_grouped(xs, tile_expert, n_tiles.reshape(1), order, w)
    z = _sc_gather_rows(out, pos.reshape(2 * t))
    return _moe_combine(h, z, route, tm)


def _pad_lanes(a, width=LANES):
    return jnp.pad(a, ((0, 0), (0, width - a.shape[1])))


def _prep_weights(norm1, w_in, q_norm, k_norm, conv_w, conv_b, dt_bias, a_log, d_skip, ssm_norm, w_out,
                  norm2, w_grp, b_grp, w_exp, b_exp, w_gate, w_up, w_down):
    w = {}
    w["norm1"] = norm1.reshape(1, D_MODEL)
    w["wqk"] = w_in[:, :K_END].astype(BF16)
    w["wv"] = w_in[:, K_END:V_END].astype(BF16)
    w["wz"] = w_in[:, V_END:Z_END].astype(BF16)
    w["wxbc"] = w_in[:, Z_END:XBC_END].astype(BF16)
    w["wdt"] = _pad_lanes(w_in[:, XBC_END:]).astype(BF16)
    w["qkn"] = jnp.concatenate([jnp.tile(q_norm, N_HEADS), jnp.tile(k_norm, KV_HEADS)]).reshape(1, QK_WIDTH)
    head_of_col = jnp.arange(QK_WIDTH) // HEAD_DIM
    red = (head_of_col[:, None] == jnp.arange(LANES)[None, :])
    w["red"] = red.astype(BF16)
    w["exp"] = red.T.astype(BF16)
    w["conv_w"] = conv_w
    w["conv_b"] = conv_b.reshape(1, CONV_DIM)
    w["dt_bias"] = _pad_lanes(dt_bias.reshape(1, SSM_HEADS))
    w["a_log"] = _pad_lanes(a_log.reshape(1, SSM_HEADS))
    w["d_skip"] = jnp.repeat(d_skip, SSM_HEAD_DIM).reshape(1, SSM_WIDTH)
    w["ssm_norm"] = ssm_norm.reshape(1, SSM_WIDTH)
    idx = jnp.arange(SSD_CHUNK)
    w["tri"] = (idx[None, :] <= idx[:, None]).astype(BF16)
    lane_head = jnp.arange(SSM_WIDTH) // SSM_HEAD_DIM
    w["expand"] = (jnp.arange(LANES)[:, None] == lane_head[None, :]).astype(BF16)
    w["wo_a"] = w_out[:ATTN_WIDTH].astype(BF16)
    w["wo_s"] = w_out[ATTN_WIDTH:].astype(BF16)
    w["norm2"] = norm2.reshape(1, D_MODEL)
    wr = jnp.zeros((D_MODEL, LANES), F32)
    wr = wr.at[:, :N_EGROUPS].set(w_grp).at[:, EXPERT_LANE0:EXPERT_LANE0 + N_EXPERTS].set(w_exp)
    wr_hi = wr.astype(BF16)
    w["wr"] = jnp.concatenate([wr_hi, (wr - wr_hi.astype(F32)).astype(BF16)], axis=1)
    br = jnp.zeros((1, LANES), F32)
    w["br"] = br.at[0, :N_EGROUPS].set(b_grp).at[0, EXPERT_LANE0:EXPERT_LANE0 + N_EXPERTS].set(b_exp)
    w["w_gate"], w["w_up"], w["w_down"] = w_gate, w_up, w_down
    return w


def _rope_tables(pos):
    inv = 1.0 / (ROPE_THETA ** (jnp.arange(0, HEAD_DIM, 2, dtype=F32) / HEAD_DIM))
    ang = pos.astype(F32)[:, None] * inv[None, :]
    cos, sin = jnp.cos(ang), jnp.sin(ang)
    reps = LANES // HEAD_DIM
    return (jnp.tile(jnp.concatenate([cos, cos], axis=-1), (1, reps)),
            jnp.tile(jnp.concatenate([-sin, sin], axis=-1), (1, reps)))


def _token_tile(t):
    for tm in (1024, 512, 256, 128, 64, 32, 16):
        if t % tm == 0:
            return tm
    raise ValueError(f"token count {t} is not a multiple of 16")


def kernel(x_prompt, x_sample, cache_win_k, cache_win_v, state_conv, state_ssm, norm1, w_in, q_norm, k_norm,
           sinks, conv_w, conv_b, dt_bias, a_log, d_skip, ssm_norm, w_out, norm2, w_grp, b_grp, w_exp, b_exp,
           w_gate, w_up, w_down):
    depth = norm1.shape[0]
    assert depth == 1, "single-layer stack"
    bp, lp, _ = x_prompt.shape
    bsn, ls, _ = x_sample.shape
    assert ls == 1 and lp % WINDOW == 0 and cache_win_k.shape[2] == WINDOW
    l = 0
    w = _prep_weights(norm1[l], w_in[l], q_norm[l], k_norm[l], conv_w[l], conv_b[l], dt_bias[l], a_log[l],
                      d_skip[l], ssm_norm[l], w_out[l], norm2[l], w_grp[l], b_grp[l], w_exp[l], b_exp[l],
                      w_gate[l], w_up[l], w_down[l])
    sink = sinks[l]

    tp = bp * lp
    xp = x_prompt.reshape(tp, D_MODEL)
    tm_p = _token_tile(lp)
    cos_p, sin_p = _rope_tables(jnp.arange(lp, dtype=jnp.int32))
    q, k, v, z, xbc, dt, k2, v2 = _inproj(xp, w, cos_p, sin_p, tm_p, lp // tm_p)
    attn = _attn_prompt(q, k2, v2, sink, bp, lp)
    ssm, st_p = _ssd_prompt(xbc, z, dt, w, bp, lp)
    h, t, route, counts = _outproj_router(xp, attn, ssm, w, tm_p)
    k3 = k.reshape(bp, lp, KV_HEADS, HEAD_DIM)
    v3 = v.reshape(bp, lp, KV_HEADS, HEAD_DIM)
    win_k_p = k3[:, lp - WINDOW:][None]
    win_v_p = v3[:, lp - WINDOW:][None]
    conv_p = xbc.reshape(bp, lp, CONV_DIM)[:, lp - (CONV_K - 1):][None]
    ssm_p = st_p.reshape(1, bp, SSM_HEADS, SSM_HEAD_DIM, SSM_STATE)

    xs2 = x_sample.reshape(bsn, D_MODEL)
    tm_s = _token_tile(bsn)
    cos_s, sin_s = _rope_tables(jnp.full((tm_s,), PAST_LEN, jnp.int32))
    q_s, k_s, v_s, z_s, xbc_s, dt_s, _, _ = _inproj(xs2, w, cos_s, sin_s, tm_s, 1)
    q4 = q_s.reshape(bsn, KV_HEADS, N_HEADS // KV_HEADS, HEAD_DIM)
    zq = jnp.zeros_like(q4[:, 0])
    qx = jnp.concatenate([jnp.concatenate([q4[:, 0], zq], axis=-1),
                          jnp.concatenate([zq, q4[:, 1]], axis=-1)], axis=1)
    qx = jnp.pad(qx, ((0, 0), (0, BF16_ROWS - N_HEADS), (0, 0)))
    sink_x = jnp.pad(jnp.broadcast_to(sink[:, None], (N_HEADS, LANES)), ((0, BF16_ROWS - N_HEADS), (0, 0)))
    kc = cache_win_k[l].reshape(bsn, WINDOW, KV_WIDTH)
    vc = cache_win_v[l].reshape(bsn, WINDOW, KV_WIDTH)
    ko, vo, attn_s = _attn_sample(qx, kc, k_s.reshape(bsn, 1, KV_WIDTH), vc, v_s.reshape(bsn, 1, KV_WIDTH),
                               sink_x, 8)
    y_prompt = _moe_routed(h, t, route, counts, w, tm_p, attn_s).reshape(bp, lp, D_MODEL)
    cprev_t = jnp.transpose(state_conv[l], (1, 0, 2))
    h0 = state_ssm[l].reshape(bsn, HEAD_PAIRS, LANES, SSM_STATE)
    ssm_s, cnew_t, h1 = _ssd_sample(xbc_s, z_s, dt_s, cprev_t, h0, w, 16)
    h_s, t_s, route_s, _ = _outproj_router(xs2, attn_s, ssm_s, w, tm_s)
    y_sample = _moe_dense(h_s, t_s, route_s, w).reshape(bsn, 1, D_MODEL)
    win_k_s = ko.reshape(1, bsn, WINDOW, KV_HEADS, HEAD_DIM)
    win_v_s = vo.reshape(1, bsn, WINDOW, KV_HEADS, HEAD_DIM)
    conv_s = jnp.transpose(cnew_t, (1, 0, 2))[None]
    ssm_s_state = h1.reshape(1, bsn, SSM_HEADS, SSM_HEAD_DIM, SSM_STATE)

    return (y_prompt, y_sample, win_k_p, win_v_p, conv_p, ssm_p, win_k_s, win_v_s, conv_s, ssm_s_state)
```

```python
import functools
import math

import jax
import jax.numpy as jnp
from jax import lax
from jax.experimental import pallas as pl
from jax.experimental.pallas import tpu as pltpu
from jax.experimental.pallas import tpu_sc as plsc

F32 = jnp.float32
BF16 = jnp.bfloat16

D_MODEL = 1024
HEAD_DIM = 64
N_HEADS = 8
KV_HEADS = 2
WINDOW = 128
ATTN_WIDTH = N_HEADS * HEAD_DIM
QK_WIDTH = ATTN_WIDTH + KV_HEADS * HEAD_DIM
KV_WIDTH = KV_HEADS * HEAD_DIM
ATTN_SCALE = HEAD_DIM ** -0.5
ROPE_THETA = 10000.0
SSM_WIDTH = 512
SSM_HEADS = 8
SSM_HEAD_DIM = 64
SSM_GROUPS = 2
SSM_STATE = 128
CONV_K = 4
CONV_HALO = 8
CONV_DIM = SSM_WIDTH + 2 * SSM_GROUPS * SSM_STATE
SSD_CHUNK = 128
N_EGROUPS = 4
EXP_PER_GROUP = 8
N_EXPERTS = 32
EXPERT_FF = 128
EPS = 1e-6
PAST_LEN = 16384

LANES = 128
BF16_ROWS = 16
HEAD_PAIRS = SSM_HEADS // 2
EXPERT_LANE0 = 32
VMEM_LIMIT = 56 * 1024 * 1024
MOE_TILE = 512
MOE_SUBTILE = 128
ATTN_QBLOCKS = 8
COUNT_BLOCK = 256
SSD_CHUNKS_PER_STEP = 8
ROUTER_SUBTILE = 1024
INPROJ_SUBTILE = 512
SC_CORES = 2
SC_SUBCORES = 16
SC_WORKERS = SC_CORES * SC_SUBCORES
SC_CHUNK = 64

Q_END = ATTN_WIDTH
K_END = Q_END + KV_WIDTH
V_END = K_END + KV_WIDTH
Z_END = V_END + SSM_WIDTH
XBC_END = Z_END + CONV_DIM


def _dot(a, b):
    return jnp.dot(a, b, preferred_element_type=F32)


def _dot_nt(a, b):
    return lax.dot_general(a, b, (((1,), (1,)), ((), ())), preferred_element_type=F32)


def _split2(v):
    hi = v.astype(BF16)
    lo = (v - hi.astype(F32)).astype(BF16)
    return hi, lo


def _split3(v):
    hi = v.astype(BF16)
    r = v - hi.astype(F32)
    mid = r.astype(BF16)
    lo = (r - mid.astype(F32)).astype(BF16)
    return hi, mid, lo


def _silu(x):
    return x * jax.nn.sigmoid(x)


def _softplus(x):
    return jnp.maximum(x, 0.0) + jnp.log1p(jnp.exp(-jnp.abs(x)))


def _lane_bcast_pairs(v, n_pairs):
    r = v.shape[0]
    lo = lax.broadcasted_iota(jnp.int32, (r, LANES), 1) < HEAD_DIM
    slabs = []
    for j in range(n_pairs):
        a = jnp.broadcast_to(v[:, 2 * j:2 * j + 1], (r, LANES))
        b = jnp.broadcast_to(v[:, 2 * j + 1:2 * j + 2], (r, LANES))
        slabs.append(jnp.where(lo, a, b))
    return jnp.concatenate(slabs, axis=1)


def _causal_conv_silu(x_ext, convw_ref, convb_ref):
    halo = CONV_HALO
    x_raw = x_ext[halo:, :]
    conv = convb_ref[...] + x_raw * convw_ref[CONV_K - 1:CONV_K, :]
    for j in range(CONV_K - 1):
        shifted = pltpu.roll(x_ext, CONV_K - 1 - j, axis=0)[halo:, :]
        conv = conv + shifted * convw_ref[j:j + 1, :]
    return _silu(conv)


def _inproj_kernel(x_ref, n1_ref, win_ref, dtb_ref, qkn_ref,
                   cos_ref, sin_ref, red_ref, exp_ref,
                   q_ref, k_ref, v_ref, z_ref, xbc_ref, dt_ref, k2_ref, v2_ref):
    tm = x_ref.shape[0]
    sub = min(tm, INPROJ_SUBTILE)
    lane = lax.broadcasted_iota(jnp.int32, (sub, LANES), 1)
    first_half = (lane % HEAD_DIM) < (HEAD_DIM // 2)
    for s in range(tm // sub):
        rows = slice(s * sub, (s + 1) * sub)
        x = x_ref[rows, :]
        ms = jnp.mean(x * x, axis=-1, keepdims=True)
        xn = (x * lax.rsqrt(ms + EPS) * n1_ref[...]).astype(BF16)
        v = _dot(xn, win_ref[:, K_END:V_END])
        v_ref[rows, :] = v
        v2_ref[rows, :] = _pair_operands(v)
        z_ref[rows, :] = _dot(xn, win_ref[:, V_END:Z_END])
        xbc_ref[rows, :] = _dot(xn, win_ref[:, Z_END:XBC_END])
        dt_ref[rows, :] = _softplus(_dot(xn, win_ref[:, XBC_END:]) + dtb_ref[...])
        qk = _dot(xn, win_ref[:, :K_END])
        ss = _dot((qk * qk).astype(BF16), red_ref[...])
        inv = lax.rsqrt(ss * (1.0 / HEAD_DIM) + EPS)
        inv_hi, inv_lo = _split2(inv)
        inv_x = _dot(inv_hi, exp_ref[...]) + _dot(inv_lo, exp_ref[...])
        qkn = qk * inv_x * qkn_ref[...]
        cos = cos_ref[rows, :]
        sin = sin_ref[rows, :]
        for c in range(QK_WIDTH // LANES):
            xc = qkn[:, c * LANES:(c + 1) * LANES]
            partner = jnp.where(first_half,
                                pltpu.roll(xc, LANES - HEAD_DIM // 2, axis=1),
                                pltpu.roll(xc, HEAD_DIM // 2, axis=1))
            rot = xc * cos + partner * sin
            if c < ATTN_WIDTH // LANES:
                q_ref[rows, c * LANES:(c + 1) * LANES] = (rot * ATTN_SCALE).astype(BF16)
            else:
                k_ref[rows, :] = rot
                k2_ref[rows, :] = _pair_operands(rot)


def _inproj(x2d, w, cos_tab, sin_tab, tm, n_pos_blocks):
    t = x2d.shape[0]
    grid = (t // tm,)
    tok = lambda i: (i, 0)
    const = lambda i: (0, 0)
    pos = lambda i: (i % n_pos_blocks, 0)
    full = lambda a: pl.BlockSpec(a.shape, const)
    rows = lambda width, dtype: (jax.ShapeDtypeStruct((t, width), dtype), pl.BlockSpec((tm, width), tok))
    outs = [rows(ATTN_WIDTH, BF16), rows(KV_WIDTH, F32), rows(KV_WIDTH, F32), rows(SSM_WIDTH, F32)]
    operands = [x2d, w["norm1"], w["w_in"], w["dt_bias"], w["qkn"],
                cos_tab, sin_tab, w["red"], w["exp"]]
    in_specs = [
        pl.BlockSpec((tm, D_MODEL), tok),
        full(w["norm1"]), full(w["w_in"]), full(w["dt_bias"]), full(w["qkn"]),
        pl.BlockSpec((tm, LANES), pos), pl.BlockSpec((tm, LANES), pos),
        full(w["red"]), full(w["exp"]),
    ]
    outs += [rows(CONV_DIM, F32), rows(LANES, F32), rows(4 * LANES, BF16), rows(4 * LANES, BF16)]
    return pl.pallas_call(
        _inproj_kernel,
        out_shape=tuple(o[0] for o in outs),
        grid=grid,
        in_specs=in_specs,
        out_specs=tuple(o[1] for o in outs),
        compiler_params=pltpu.CompilerParams(dimension_semantics=("parallel",),
                                             vmem_limit_bytes=VMEM_LIMIT),
        name="inproj",
    )(*operands)


def _pair_operands(kv):
    lo = lax.broadcasted_iota(jnp.int32, kv.shape, 1) < HEAD_DIM
    swapped = pltpu.roll(kv, HEAD_DIM, axis=1)
    parts = [jnp.where(lo, kv, 0.0), jnp.where(lo, 0.0, swapped), jnp.where(lo, swapped, 0.0), jnp.where(lo, 0.0, kv)]
    return jnp.concatenate(parts, axis=1).astype(BF16)


def _pair_rhs(blk, g):
    return jnp.concatenate([blk[:, 2 * g * LANES:(2 * g + 1) * LANES],
                            blk[:, (2 * g + 1) * LANES:(2 * g + 2) * LANES]], axis=0)


def _attn_kernel(sink_ref, q_ref, kc_ref, kp_ref, vc_ref, vp_ref, o_ref):
    blk = WINDOW
    n_sub = q_ref.shape[0] // blk
    first_step = pl.program_id(1) == 0
    qi = lax.broadcasted_iota(jnp.int32, (blk, 2 * blk), 0)
    kj = lax.broadcasted_iota(jnp.int32, (blk, 2 * blk), 1) % blk
    cur_ok = kj <= qi
    seq_start_ok = jnp.logical_or(cur_ok, jnp.logical_not(first_step))
    lo = lax.broadcasted_iota(jnp.int32, (blk, LANES), 1) < HEAD_DIM
    k_blocks = [kp_ref[...]] + [kc_ref[u * blk:(u + 1) * blk, :] for u in range(n_sub)]
    v_blocks = [vp_ref[...]] + [vc_ref[u * blk:(u + 1) * blk, :] for u in range(n_sub)]
    for g in range(KV_HEADS):
        k2 = [_pair_rhs(b, g) for b in k_blocks]
        v2 = [_pair_rhs(b, g) for b in v_blocks]
        n_pairs = N_HEADS // KV_HEADS // 2
        for u in range(n_sub):
            rows = slice(u * blk, (u + 1) * blk)
            q_all = jnp.concatenate([q_ref[rows, (g * n_pairs + r) * LANES:(g * n_pairs + r + 1) * LANES]
                                     for r in range(n_pairs)], axis=0)
            s_all = _dot_nt(q_all, jnp.concatenate([k2[u + 1], k2[u]], axis=0))
            p_rows = []
            den_rows = []
            for r in range(n_pairs):
                pair = g * n_pairs + r
                s_cur = s_all[r * blk:(r + 1) * blk, :2 * blk]
                s_prev = s_all[r * blk:(r + 1) * blk, 2 * blk:]
                s = jnp.where(cur_ok, s_cur, s_prev)
                if u == 0:
                    s = jnp.where(seq_start_ok, s, -jnp.inf)
                ps = []
                dens = []
                for hh in range(2):
                    sink = sink_ref[2 * pair + hh]
                    sh = s[:, hh * blk:(hh + 1) * blk]
                    m = jnp.maximum(jnp.max(sh, axis=-1, keepdims=True), sink)
                    p = jnp.exp(sh - m)
                    dens.append(jnp.sum(p, axis=-1, keepdims=True) + jnp.exp(sink - m))
                    ps.append(p)
                p2 = jnp.concatenate(ps, axis=1)
                p_rows.append(jnp.concatenate([jnp.where(cur_ok, p2, 0.0), jnp.where(cur_ok, 0.0, p2)],
                                              axis=1).astype(BF16))
                den_rows.append(jnp.where(lo, dens[0], dens[1]))
            o_all = _dot(jnp.concatenate(p_rows, axis=0), jnp.concatenate([v2[u + 1], v2[u]], axis=0))
            for r in range(n_pairs):
                pair = g * n_pairs + r
                o2 = o_all[r * blk:(r + 1) * blk, :]
                o_ref[rows, pair * LANES:(pair + 1) * LANES] = (o2 / den_rows[r]).astype(BF16)


def _attn_prompt(q, k, v, sinks, batch, seq):
    n_sub = ATTN_QBLOCKS if seq % (ATTN_QBLOCKS * WINDOW) == 0 else 1
    rows = n_sub * WINDOW
    nb = seq // rows
    cur = lambda b, j, s: (b * nb + j, 0)
    prev = lambda b, j, s: (jnp.maximum((b * nb + j) * n_sub - 1, 0), 0)
    return pl.pallas_call(
        _attn_kernel,
        out_shape=jax.ShapeDtypeStruct((batch * seq, ATTN_WIDTH), BF16),
        grid_spec=pltpu.PrefetchScalarGridSpec(
            num_scalar_prefetch=1,
            grid=(batch, nb),
            in_specs=[
                pl.BlockSpec((rows, ATTN_WIDTH), cur),
                pl.BlockSpec((rows, 4 * LANES), cur), pl.BlockSpec((WINDOW, 4 * LANES), prev),
                pl.BlockSpec((rows, 4 * LANES), cur), pl.BlockSpec((WINDOW, 4 * LANES), prev),
            ],
            out_specs=pl.BlockSpec((rows, ATTN_WIDTH), cur),
        ),
        compiler_params=pltpu.CompilerParams(dimension_semantics=("parallel", "parallel"),
                                             vmem_limit_bytes=VMEM_LIMIT),
        name="attn_prompt",
    )(sinks, q, k, k, v, v)


def _ssd_kernel(xbc_ref, z_ref, dt_ref, convw_ref, convb_ref, alog_ref, dskip_ref, nw_ref,
                tri_ref, expand_ref, y_ref, st_ref, buf_ref, state_ref):
    c = pl.program_id(1)
    cl = SSD_CHUNK
    n_sub = xbc_ref.shape[0] // cl
    halo = CONV_HALO

    @pl.when(c == 0)
    def _():
        buf_ref[...] = jnp.zeros(buf_ref.shape, F32)
        state_ref[...] = jnp.zeros(state_ref.shape, F32)

    lane = lax.broadcasted_iota(jnp.int32, (1, LANES), 1)
    a_neg = jnp.where(lane < SSM_HEADS, -jnp.exp(alog_ref[...]), 0.0)
    tri = tri_ref[...]
    for u in range(n_sub):
        rows = slice(u * cl, (u + 1) * cl)
        if u == 0:
            x_ext = jnp.concatenate([buf_ref[...], xbc_ref[rows, :]], axis=0)
        else:
            x_ext = xbc_ref[u * cl - halo:(u + 1) * cl, :]
        _ssd_chunk(x_ext, z_ref[rows, :], dt_ref[rows, :], a_neg, tri, convw_ref, convb_ref, dskip_ref, nw_ref,
                   expand_ref, y_ref.at[rows, :], state_ref)
    buf_ref[...] = xbc_ref[n_sub * cl - halo:n_sub * cl, :]

    @pl.when(c == pl.num_programs(1) - 1)
    def _():
        st_ref[0] = state_ref[...]


def _ssd_chunk(x_ext, z, dt, a_neg, tri, convw_ref, convb_ref, dskip_ref, nw_ref, expand_ref, y_ref, state_ref):
    cl = SSD_CHUNK
    act = _causal_conv_silu(x_ext, convw_ref, convb_ref)
    xs = act[:, :SSM_WIDTH]
    bm = act[:, SSM_WIDTH:SSM_WIDTH + SSM_GROUPS * SSM_STATE].astype(BF16)
    cm = act[:, SSM_WIDTH + SSM_GROUPS * SSM_STATE:].astype(BF16)

    dta = dt * a_neg
    p_hi, p_mid, p_lo = _split3(dta)
    a_col = _dot(tri, p_hi) + _dot(tri, p_mid) + _dot(tri, p_lo)
    a_last = a_col[cl - 1:cl, :]
    a_row = a_col.T
    per_head = jnp.concatenate([dt, jnp.exp(a_col), jnp.exp(a_last - a_col)], axis=0)
    ph_hi, ph_lo = _split2(per_head)
    per_lane = _dot(ph_hi, expand_ref[...]) + _dot(ph_lo, expand_ref[...])
    dt_x = per_lane[:cl]
    ecol_x = per_lane[cl:2 * cl]
    dte_x = per_lane[2 * cl:]
    e_last = jnp.exp(a_last)
    xdt = xs * dt_x

    li = lax.broadcasted_iota(jnp.int32, (cl, cl), 0)
    si = lax.broadcasted_iota(jnp.int32, (cl, cl), 1)
    causal = si <= li
    lo = lax.broadcasted_iota(jnp.int32, (cl, LANES), 1) < SSM_HEAD_DIM
    row_lo = lax.broadcasted_iota(jnp.int32, (LANES, SSM_STATE), 0) < SSM_HEAD_DIM

    ys = []
    for g in range(SSM_GROUPS):
        b_g = bm[:, g * SSM_STATE:(g + 1) * SSM_STATE]
        c_g = cm[:, g * SSM_STATE:(g + 1) * SSM_STATE]
        cb = _dot_nt(c_g, b_g)
        for r in range(HEAD_PAIRS // SSM_GROUPS):
            j = g * (HEAD_PAIRS // SSM_GROUPS) + r
            sl = slice(j * LANES, (j + 1) * LANES)
            xdt_p = xdt[:, sl]
            ms = []
            for hh in range(2):
                h = 2 * j + hh
                seg = a_col[:, h:h + 1] - a_row[h:h + 1, :]
                ms.append(cb * jnp.exp(jnp.where(causal, seg, -jnp.inf)))
            m2 = jnp.concatenate(ms, axis=1).astype(BF16)
            rhs = jnp.concatenate([jnp.where(lo, xdt_p, 0.0), jnp.where(lo, 0.0, xdt_p)],
                                  axis=0).astype(BF16)
            y_diag = _dot(m2, rhs)
            st = state_ref[j]
            y_off = _dot_nt(c_g, st.astype(BF16)) * ecol_x[:, sl]
            xdt_e = (xdt_p * dte_x[:, sl]).T.astype(BF16)
            d_a = e_last[:, 2 * j:2 * j + 1]
            d_b = e_last[:, 2 * j + 1:2 * j + 2]
            decay = jnp.where(row_lo, jnp.broadcast_to(d_a, row_lo.shape), jnp.broadcast_to(d_b, row_lo.shape))
            state_ref[j] = decay * st + _dot(xdt_e, b_g)
            ys.append(y_diag + y_off + dskip_ref[:, sl] * xs[:, sl])
    y = jnp.concatenate(ys, axis=1)
    gated = y * _silu(z)
    gw = SSM_WIDTH // SSM_GROUPS
    outs = []
    for g in range(SSM_GROUPS):
        gg = gated[:, g * gw:(g + 1) * gw]
        outs.append(gg * lax.rsqrt(jnp.mean(gg * gg, axis=-1, keepdims=True) + EPS))
    y_ref[...] = (jnp.concatenate(outs, axis=1) * nw_ref[...]).astype(BF16)


def _ssd_prompt(xbc, z, dt, w, batch, seq):
    n_sub = SSD_CHUNKS_PER_STEP if seq % (SSD_CHUNKS_PER_STEP * SSD_CHUNK) == 0 else 1
    rows = n_sub * SSD_CHUNK
    nc = seq // rows
    tok = lambda b, c: (b * nc + c, 0)
    const = lambda b, c: (0, 0)
    full = lambda a: pl.BlockSpec(a.shape, const)
    return pl.pallas_call(
        _ssd_kernel,
        out_shape=(jax.ShapeDtypeStruct((batch * seq, SSM_WIDTH), BF16),
                   jax.ShapeDtypeStruct((batch, HEAD_PAIRS, LANES, SSM_STATE), F32)),
        grid=(batch, nc),
        in_specs=[
            pl.BlockSpec((rows, CONV_DIM), tok), pl.BlockSpec((rows, SSM_WIDTH), tok),
            pl.BlockSpec((rows, LANES), tok),
            full(w["conv_w"]), full(w["conv_b"]), full(w["a_log"]),
            full(w["d_skip"]), full(w["ssm_norm"]), full(w["tri"]), full(w["expand"]),
        ],
        out_specs=(pl.BlockSpec((rows, SSM_WIDTH), tok),
                   pl.BlockSpec((1, HEAD_PAIRS, LANES, SSM_STATE), lambda b, c: (b, 0, 0, 0))),
        scratch_shapes=[pltpu.VMEM((CONV_HALO, CONV_DIM), F32),
                        pltpu.VMEM((HEAD_PAIRS, LANES, SSM_STATE), F32)],
        compiler_params=pltpu.CompilerParams(dimension_semantics=("parallel", "arbitrary"),
                                             vmem_limit_bytes=VMEM_LIMIT),
        name="ssd_prompt",
    )(xbc, z, dt, w["conv_w"], w["conv_b"], w["a_log"], w["d_skip"], w["ssm_norm"], w["tri"], w["expand"])


def _attn_sample_kernel(qx_ref, kc_ref, kn_ref, vc_ref, vn_ref, sink_ref, ko_ref, vo_ref, o_ref):
    bs = qx_ref.shape[0]
    w = kc_ref.shape[1]
    sink = sink_ref[...]
    lo = lax.broadcasted_iota(jnp.int32, (1, LANES), 1) < HEAD_DIM
    for i in range(bs):
        ko_ref[i, 0:w - 1, :] = kc_ref[i, 1:w, :]
        ko_ref[i, w - 1:w, :] = kn_ref[i]
        vo_ref[i, 0:w - 1, :] = vc_ref[i, 1:w, :]
        vo_ref[i, w - 1:w, :] = vn_ref[i]
        s = _dot_nt(qx_ref[i], ko_ref[i].astype(BF16))
        m = jnp.maximum(jnp.max(s, axis=-1, keepdims=True), sink)
        p = jnp.exp(s - m)
        den = jnp.sum(p, axis=-1, keepdims=True) + jnp.exp(sink - m)
        o = _dot(p.astype(BF16), vo_ref[i].astype(BF16)) / den
        o_sw = pltpu.roll(o, HEAD_DIM, axis=1)
        for j in range(N_HEADS // 2):
            a, b = (o, o_sw) if j < N_HEADS // 4 else (o_sw, o)
            o_ref[i:i + 1, j * LANES:(j + 1) * LANES] = jnp.where(lo, a[2 * j:2 * j + 1], b[2 * j + 1:2 * j + 2])


def _attn_sample(qx, kc, kn, vc, vn, sink_x, bs):
    n, w = kc.shape[0], kc.shape[1]
    blk3 = lambda i: (i, 0, 0)
    return pl.pallas_call(
        _attn_sample_kernel,
        out_shape=(jax.ShapeDtypeStruct((n, w, KV_WIDTH), F32),
                   jax.ShapeDtypeStruct((n, w, KV_WIDTH), F32),
                   jax.ShapeDtypeStruct((n, ATTN_WIDTH), F32)),
        grid=(n // bs,),
        in_specs=[
            pl.BlockSpec((bs, BF16_ROWS, LANES), blk3),
            pl.BlockSpec((bs, w, KV_WIDTH), blk3), pl.BlockSpec((bs, 1, KV_WIDTH), blk3),
            pl.BlockSpec((bs, w, KV_WIDTH), blk3), pl.BlockSpec((bs, 1, KV_WIDTH), blk3),
            pl.BlockSpec(sink_x.shape, lambda i: (0, 0)),
        ],
        out_specs=(pl.BlockSpec((bs, w, KV_WIDTH), blk3), pl.BlockSpec((bs, w, KV_WIDTH), blk3),
                   pl.BlockSpec((bs, ATTN_WIDTH), lambda i: (i, 0))),
        compiler_params=pltpu.CompilerParams(dimension_semantics=("parallel",),
                                             vmem_limit_bytes=VMEM_LIMIT),
        name="attn_sample",
    )(qx, kc, kn, vc, vn, sink_x)


def _ssd_sample_kernel(xbc_ref, z_ref, dt_ref, cprev_ref, h0_ref, convw_ref, convb_ref, alog_ref,
                       dskip_ref, nw_ref, y_ref, cnew_ref, h1_ref):
    bs = xbc_ref.shape[0]
    x_raw = xbc_ref[...]
    conv = convb_ref[...] + x_raw * convw_ref[CONV_K - 1:CONV_K, :]
    for j in range(CONV_K - 1):
        conv = conv + cprev_ref[j] * convw_ref[j:j + 1, :]
    for j in range(CONV_K - 2):
        cnew_ref[j] = cprev_ref[j + 1]
    cnew_ref[CONV_K - 2] = x_raw
    act = _silu(conv)
    xs = act[:, :SSM_WIDTH]
    bm = act[:, SSM_WIDTH:SSM_WIDTH + SSM_GROUPS * SSM_STATE].astype(BF16)
    cm = act[:, SSM_WIDTH + SSM_GROUPS * SSM_STATE:].astype(BF16)
    lane = lax.broadcasted_iota(jnp.int32, (1, LANES), 1)
    a_neg = jnp.where(lane < SSM_HEADS, -jnp.exp(alog_ref[...]), 0.0)
    dt = dt_ref[...]
    dec = jnp.exp(dt * a_neg)
    xdt = xs * _lane_bcast_pairs(dt, HEAD_PAIRS)
    rowid = lax.broadcasted_iota(jnp.int32, (bs, LANES), 0)
    row_lo = lax.broadcasted_iota(jnp.int32, (LANES, SSM_STATE), 0) < SSM_HEAD_DIM
    ys = []
    for j in range(HEAD_PAIRS):
        g = j // (HEAD_PAIRS // SSM_GROUPS)
        sl = slice(j * LANES, (j + 1) * LANES)
        b_g = bm[:, g * SSM_STATE:(g + 1) * SSM_STATE]
        c_g = cm[:, g * SSM_STATE:(g + 1) * SSM_STATE]
        xdt_p = xdt[:, sl]
        y_p = jnp.zeros((bs, LANES), F32)
        for i in range(bs):
            xi = jnp.where(rowid == i, xdt_p, 0.0).T.astype(BF16)
            d_a = dec[i:i + 1, 2 * j:2 * j + 1]
            d_b = dec[i:i + 1, 2 * j + 1:2 * j + 2]
            decay = jnp.where(row_lo, jnp.broadcast_to(d_a, row_lo.shape), jnp.broadcast_to(d_b, row_lo.shape))
            new = decay * h0_ref[i, j] + _dot(xi, b_g)
            h1_ref[i, j] = new
            y_p = y_p + jnp.where(rowid == i, _dot_nt(c_g, new.astype(BF16)), 0.0)
        ys.append(y_p + dskip_ref[:, sl] * xs[:, sl])
    y = jnp.concatenate(ys, axis=1)
    gated = y * _silu(z_ref[...])
    gw = SSM_WIDTH // SSM_GROUPS
    outs = []
    for g in range(SSM_GROUPS):
        gg = gated[:, g * gw:(g + 1) * gw]
        outs.append(gg * lax.rsqrt(jnp.mean(gg * gg, axis=-1, keepdims=True) + EPS))
    y_ref[...] = (jnp.concatenate(outs, axis=1) * nw_ref[...]).astype(BF16)


def _ssd_sample(xbc, z, dt, cprev_t, h0, w, bs):
    n = xbc.shape[0]
    tok = lambda i: (i, 0)
    const = lambda i: (0, 0)
    full = lambda a: pl.BlockSpec(a.shape, const)
    return pl.pallas_call(
        _ssd_sample_kernel,
        out_shape=(jax.ShapeDtypeStruct((n, SSM_WIDTH), BF16),
                   jax.ShapeDtypeStruct((CONV_K - 1, n, CONV_DIM), F32),
                   jax.ShapeDtypeStruct((n, HEAD_PAIRS, LANES, SSM_STATE), F32)),
        grid=(n // bs,),
        in_specs=[
            pl.BlockSpec((bs, CONV_DIM), tok), pl.BlockSpec((bs, SSM_WIDTH), tok),
            pl.BlockSpec((bs, LANES), tok),
            pl.BlockSpec((CONV_K - 1, bs, CONV_DIM), lambda i: (0, i, 0)),
            pl.BlockSpec((bs, HEAD_PAIRS, LANES, SSM_STATE), lambda i: (i, 0, 0, 0)),
            full(w["conv_w"]), full(w["conv_b"]), full(w["a_log"]),
            full(w["d_skip"]), full(w["ssm_norm"]),
        ],
        out_specs=(pl.BlockSpec((bs, SSM_WIDTH), tok),
                   pl.BlockSpec((CONV_K - 1, bs, CONV_DIM), lambda i: (0, i, 0)),
                   pl.BlockSpec((bs, HEAD_PAIRS, LANES, SSM_STATE), lambda i: (i, 0, 0, 0))),
        compiler_params=pltpu.CompilerParams(dimension_semantics=("parallel",),
                                             vmem_limit_bytes=VMEM_LIMIT),
        name="ssd_sample",
    )(xbc, z, dt, cprev_t, h0, w["conv_w"], w["conv_b"], w["a_log"], w["d_skip"], w["ssm_norm"])


def _pack_bf16_pair(v):
    c = v.shape[1] // 2
    hi = lax.bitcast_convert_type(v[:, :c].astype(BF16).astype(F32), jnp.uint32)
    lo = lax.bitcast_convert_type(v[:, c:].astype(BF16).astype(F32), jnp.uint32)
    return hi | (lo >> 16)


def _unpack_bf16_pair(word):
    a = lax.bitcast_convert_type(word & jnp.uint32(0xFFFF0000), F32)
    b = lax.bitcast_convert_type(word << 16, F32)
    return a, b


def _outproj_router_kernel(x_ref, a_ref, s_ref, wo_ref, n2_ref, wr_ref, br_ref, tri_ref,
                           h_ref, t_ref, route_ref, cnt_ref, carry_ref):
    @pl.when(pl.program_id(0) == 0)
    def _():
        carry_ref[...] = jnp.zeros(carry_ref.shape, F32)

    tm = x_ref.shape[0]
    h_ref[...] = (x_ref[...] + _dot(a_ref[...].astype(BF16), wo_ref[:ATTN_WIDTH, :])
                  + _dot(s_ref[...].astype(BF16), wo_ref[ATTN_WIDTH:, :]))
    sub = min(tm, ROUTER_SUBTILE)
    carry = carry_ref[0:1, :]
    for s in range(tm // sub):
        rows = slice(s * sub, (s + 1) * sub)
        carry = _route_rows(h_ref[rows, :], n2_ref, wr_ref, br_ref, tri_ref, carry, t_ref.at[rows, :],
                            route_ref.at[rows, :])
    carry_ref[0:1, :] = carry
    cnt_ref[...] = jnp.broadcast_to(carry, cnt_ref.shape)


def _route_rows(h, n2_ref, wr_ref, br_ref, tri_ref, carry, t_ref, route_ref):
    ms = jnp.mean(h * h, axis=-1, keepdims=True)
    t = h * lax.rsqrt(ms + EPS) * n2_ref[...]
    t_hi, t_lo = _split2(t)
    t_ref[...] = _pack_bf16_pair(t)
    both = _dot(t_hi, wr_ref[...])
    logits = both[:, :LANES] + both[:, LANES:] + _dot(t_lo, wr_ref[:, :LANES]) + br_ref[...]
    lane = lax.broadcasted_iota(jnp.int32, logits.shape, 1)
    lane_f = lane.astype(F32)
    big = float(LANES)
    ninf = -jnp.inf
    glog = jnp.where(lane < N_EGROUPS, logits, ninf)
    gmax = jnp.max(glog, axis=-1, keepdims=True)
    g_top = 1.0 / jnp.sum(jnp.exp(glog - gmax), axis=-1, keepdims=True)
    g_idx = jnp.min(jnp.where(glog == gmax, lane_f, big), axis=-1, keepdims=True)
    e_of_lane = lane - EXPERT_LANE0
    in_grp = jnp.logical_and(jnp.logical_and(e_of_lane >= 0, e_of_lane < N_EXPERTS),
                             (e_of_lane // EXP_PER_GROUP).astype(F32) == g_idx)
    ml = jnp.where(in_grp, logits, ninf)
    m1 = jnp.max(ml, axis=-1, keepdims=True)
    i1 = jnp.min(jnp.where(ml == m1, lane_f, big), axis=-1, keepdims=True)
    ml2 = jnp.where(lane_f == i1, ninf, ml)
    m2 = jnp.max(ml2, axis=-1, keepdims=True)
    i2 = jnp.min(jnp.where(ml2 == m2, lane_f, big), axis=-1, keepdims=True)
    r = jnp.exp(m2 - m1)
    w1 = g_top / (1.0 + r)
    w2 = g_top * r / (1.0 + r)
    is1 = lane_f == i1
    is2 = lane_f == i2
    onehot = jnp.where(jnp.logical_or(is1, is2), 1.0, 0.0)
    onehot_bf = onehot.astype(BF16)
    n_rows = onehot.shape[0]
    cb = tri_ref.shape[0]
    cums = []
    for blk in range(n_rows // cb):
        c = _dot(tri_ref[...], onehot_bf[blk * cb:(blk + 1) * cb, :]) + carry
        carry = c[cb - 1:cb, :]
        cums.append(c)
    before = jnp.concatenate(cums, axis=0) - onehot
    rank1 = jnp.sum(jnp.where(is1, before, 0.0), axis=-1, keepdims=True)
    rank2 = jnp.sum(jnp.where(is2, before, 0.0), axis=-1, keepdims=True)
    fields = (i1 - EXPERT_LANE0, i2 - EXPERT_LANE0, w1, w2, rank1, rank2)
    route = jnp.zeros(logits.shape, F32)
    for pos, val in enumerate(fields):
        route = jnp.where(lane == pos, val, route)
    route_ref[...] = route
    return carry


def _outproj_router(x2d, attn, ssm, w, tm):
    t = x2d.shape[0]
    tok = lambda i: (i, 0)
    const = lambda i: (0, 0)
    full = lambda a: pl.BlockSpec(a.shape, const)
    idx = jnp.arange(min(tm, COUNT_BLOCK))
    tri = (idx[None, :] <= idx[:, None]).astype(BF16)
    return pl.pallas_call(
        _outproj_router_kernel,
        out_shape=(jax.ShapeDtypeStruct((t, D_MODEL), F32), jax.ShapeDtypeStruct((t, D_MODEL // 2), jnp.uint32),
                   jax.ShapeDtypeStruct((t, LANES), F32), jax.ShapeDtypeStruct((8, LANES), F32)),
        grid=(t // tm,),
        in_specs=[
            pl.BlockSpec((tm, D_MODEL), tok), pl.BlockSpec((tm, ATTN_WIDTH), tok),
            pl.BlockSpec((tm, SSM_WIDTH), tok),
            full(w["w_out"]), full(w["norm2"]), full(w["wr"]),
            full(w["br"]), full(tri),
        ],
        out_specs=(pl.BlockSpec((tm, D_MODEL), tok), pl.BlockSpec((tm, D_MODEL // 2), tok),
                   pl.BlockSpec((tm, LANES), tok), pl.BlockSpec((8, LANES), const)),
        scratch_shapes=[pltpu.VMEM((8, LANES), F32)],
        compiler_params=pltpu.CompilerParams(dimension_semantics=("arbitrary",),
                                             vmem_limit_bytes=VMEM_LIMIT),
        name="outproj_router",
    )(x2d, attn, ssm, w["w_out"], w["norm2"], w["wr"], w["br"], tri)


def _expert_hidden(t_a, t_b, w1):
    half = D_MODEL // 2
    gu = _dot(t_a, w1[:half]) + _dot(t_b, w1[half:])
    return _silu(gu[:, :EXPERT_FF]) * gu[:, EXPERT_FF:]


def _gate_up_bf16(wg_ref, wu_ref):
    return jnp.concatenate([wg_ref[0].astype(BF16), wu_ref[0].astype(BF16)], axis=1)


def _moe_dense_kernel(h_ref, t_ref, route_ref, wg_ref, wu_ref, wd_ref, y_ref):
    e = pl.program_id(0)

    @pl.when(e == 0)
    def _():
        y_ref[...] = h_ref[...]

    t_a, t_b = _unpack_bf16_pair(t_ref[...])
    route = route_ref[...]
    e1, e2, g1, g2 = route[:, 0:1], route[:, 1:2], route[:, 2:3], route[:, 3:4]
    e_f = e.astype(F32)
    hid = _expert_hidden(t_a.astype(BF16), t_b.astype(BF16), _gate_up_bf16(wg_ref, wu_ref))
    c_e = jnp.where(e1 == e_f, g1, 0.0) + jnp.where(e2 == e_f, g2, 0.0)
    y_ref[...] += _dot((hid * c_e).astype(BF16), wd_ref[0].astype(BF16))


def _moe_dense(h, t, route, w):
    n = h.shape[0]
    whole = lambda e: (0, 0)
    by_expert = lambda e: (e, 0, 0)
    return pl.pallas_call(
        _moe_dense_kernel,
        out_shape=jax.ShapeDtypeStruct((n, D_MODEL), F32),
        grid=(N_EXPERTS,),
        in_specs=[pl.BlockSpec((n, D_MODEL), whole), pl.BlockSpec((n, D_MODEL // 2), whole),
                  pl.BlockSpec((n, LANES), whole),
                  pl.BlockSpec((1, D_MODEL, EXPERT_FF), by_expert), pl.BlockSpec((1, D_MODEL, EXPERT_FF), by_expert),
                  pl.BlockSpec((1, EXPERT_FF, D_MODEL), by_expert)],
        out_specs=pl.BlockSpec((n, D_MODEL), whole),
        compiler_params=pltpu.CompilerParams(dimension_semantics=("arbitrary",),
                                             vmem_limit_bytes=VMEM_LIMIT),
        name="moe_dense",
    )(h, t, route, w["w_gate"], w["w_up"], w["w_down"])


def _sc_scatter_rows(src, pos1, pos2, n_out):
    t, width = src.shape
    rows_per_worker = t // SC_WORKERS
    n_chunks = rows_per_worker // SC_CHUNK
    assert t == SC_WORKERS * SC_CHUNK * n_chunks
    mesh = plsc.VectorSubcoreMesh(core_axis_name="c", subcore_axis_name="s")

    @functools.partial(
        pl.kernel, mesh=mesh,
        out_type=jax.ShapeDtypeStruct((n_out, width), src.dtype),
        scratch_types=[pltpu.VMEM((SC_CHUNK,), jnp.int32), pltpu.VMEM((SC_CHUNK,), jnp.int32),
                       pltpu.VMEM((SC_CHUNK, width), src.dtype), pltpu.SemaphoreType.DMA],
    )
    def scatter_kernel(src_hbm, p1_hbm, p2_hbm, out_hbm, i1_v, i2_v, rows_v, sem):
        wid = lax.axis_index("s") * SC_CORES + lax.axis_index("c")
        base = wid * rows_per_worker

        @pl.loop(0, n_chunks)
        def _(c):
            off = pl.multiple_of(base + c * SC_CHUNK, 8)
            pltpu.sync_copy(p1_hbm.at[pl.ds(off, SC_CHUNK)], i1_v)
            pltpu.sync_copy(p2_hbm.at[pl.ds(off, SC_CHUNK)], i2_v)
            pltpu.sync_copy(src_hbm.at[pl.ds(off, SC_CHUNK)], rows_v)
            pltpu.async_copy(rows_v, out_hbm.at[i1_v], sem).wait()
            pltpu.async_copy(rows_v, out_hbm.at[i2_v], sem).wait()

    return scatter_kernel(src, pos1, pos2)


def _sc_gather_rows(table, idx):
    n, width = idx.shape[0], table.shape[1]
    rows_per_worker = n // SC_WORKERS
    n_chunks = rows_per_worker // SC_CHUNK
    assert n == SC_WORKERS * SC_CHUNK * n_chunks and n_chunks % 2 == 0
    mesh = plsc.VectorSubcoreMesh(core_axis_name="c", subcore_axis_name="s")

    @functools.partial(
        pl.kernel, mesh=mesh,
        out_type=jax.ShapeDtypeStruct((n, width), table.dtype),
        scratch_types=[pltpu.VMEM((2, SC_CHUNK), jnp.int32), pltpu.VMEM((2, SC_CHUNK, width), table.dtype),
                       pltpu.SemaphoreType.DMA, pltpu.SemaphoreType.DMA,
                       pltpu.SemaphoreType.DMA, pltpu.SemaphoreType.DMA],
    )
    def gather_kernel(table_hbm, idx_hbm, out_hbm, idx_v, rows_v, g0, g1, w0, w1):
        wid = lax.axis_index("s") * SC_CORES + lax.axis_index("c")
        base = wid * rows_per_worker
        gsem = (g0, g1)
        wsem = (w0, w1)

        def gather_copy(slot):
            return pltpu.make_async_copy(table_hbm.at[idx_v.at[slot]], rows_v.at[slot], gsem[slot])

        def write_copy(c, slot):
            off = pl.multiple_of(base + c * SC_CHUNK, 8)
            return pltpu.make_async_copy(rows_v.at[slot], out_hbm.at[pl.ds(off, SC_CHUNK)], wsem[slot])

        def start_gather(c, slot):
            off = pl.multiple_of(base + c * SC_CHUNK, 8)
            pltpu.sync_copy(idx_hbm.at[pl.ds(off, SC_CHUNK)], idx_v.at[slot])
            gather_copy(slot).start()

        start_gather(0, 0)

        @pl.loop(0, n_chunks, step=2)
        def _(c):
            @pl.when(c > 0)
            def _():
                write_copy(c - 1, 1).wait()

            start_gather(c + 1, 1)
            gather_copy(0).wait()
            write_copy(c, 0).start()
            gather_copy(1).wait()
            write_copy(c + 1, 1).start()
            write_copy(c, 0).wait()

            @pl.when(c + 2 < n_chunks)
            def _():
                start_gather(c + 2, 0)

        write_copy(n_chunks - 1, 1).wait()

    return gather_kernel(table, idx)


def _moe_grouped_kernel(te_ref, nt_ref, order_ref, x_ref, wg_ref, wu_ref, wd_ref, o_ref, w1_bf_ref, w2_bf_ref):
    del order_ref
    i = pl.program_id(0)

    @pl.when(jnp.logical_or(i == 0, te_ref[i] != te_ref[jnp.maximum(i - 1, 0)]))
    def _():
        w1_bf_ref[...] = _gate_up_bf16(wg_ref, wu_ref)
        w2_bf_ref[...] = wd_ref[0].astype(BF16)

    @pl.when(i < nt_ref[0])
    def _():
        for s in range(MOE_TILE // MOE_SUBTILE):
            rows = slice(s * MOE_SUBTILE, (s + 1) * MOE_SUBTILE)
            t_a, t_b = _unpack_bf16_pair(x_ref[rows, :])
            hid = _expert_hidden(t_a.astype(BF16), t_b.astype(BF16), w1_bf_ref[...])
            o_ref[rows, :] = _pack_bf16_pair(_dot(hid.astype(BF16), w2_bf_ref[...]))


def _moe_grouped(xs, tile_expert, n_tiles, order, w):
    rows = xs.shape[0]
    row = lambda i, te, nt, od: (i, 0)
    by_expert = lambda i, te, nt, od: (te[i], 0, 0)
    return pl.pallas_call(
        _moe_grouped_kernel,
        out_shape=jax.ShapeDtypeStruct((rows, D_MODEL // 2), jnp.uint32),
        grid_spec=pltpu.PrefetchScalarGridSpec(
            num_scalar_prefetch=3,
            grid=(rows // MOE_TILE,),
            in_specs=[pl.BlockSpec((MOE_TILE, D_MODEL // 2), row),
                      pl.BlockSpec((1, D_MODEL, EXPERT_FF), by_expert),
                      pl.BlockSpec((1, D_MODEL, EXPERT_FF), by_expert),
                      pl.BlockSpec((1, EXPERT_FF, D_MODEL), by_expert)],
            out_specs=pl.BlockSpec((MOE_TILE, D_MODEL // 2), row),
            scratch_shapes=[pltpu.VMEM((D_MODEL, 2 * EXPERT_FF), BF16), pltpu.VMEM((EXPERT_FF, D_MODEL), BF16)],
        ),
        compiler_params=pltpu.CompilerParams(dimension_semantics=("arbitrary",),
                                             vmem_limit_bytes=VMEM_LIMIT),
        name="moe_grouped",
    )(tile_expert, n_tiles, order, xs, w["w_gate"], w["w_up"], w["w_down"])


def _moe_combine_kernel(h_ref, z1_ref, z2_ref, route_ref, y_ref):
    route = route_ref[...]
    g1, g2 = route[:, 2:3], route[:, 3:4]
    half = D_MODEL // 2
    a1, b1 = _unpack_bf16_pair(z1_ref[...])
    a2, b2 = _unpack_bf16_pair(z2_ref[...])
    y_ref[:, :half] = h_ref[:, :half] + g1 * a1 + g2 * a2
    y_ref[:, half:] = h_ref[:, half:] + g1 * b1 + g2 * b2


def _moe_combine(h, z, route, tm):
    t = h.shape[0]
    nb = t // tm
    tok = lambda i: (i, 0)
    return pl.pallas_call(
        _moe_combine_kernel,
        out_shape=jax.ShapeDtypeStruct((t, D_MODEL), F32),
        grid=(nb,),
        in_specs=[pl.BlockSpec((tm, D_MODEL), tok), pl.BlockSpec((tm, D_MODEL // 2), tok),
                  pl.BlockSpec((tm, D_MODEL // 2), lambda i: (i + nb, 0)), pl.BlockSpec((tm, LANES), tok)],
        out_specs=pl.BlockSpec((tm, D_MODEL), tok),
        compiler_params=pltpu.CompilerParams(dimension_semantics=("parallel",),
                                             vmem_limit_bytes=VMEM_LIMIT),
        name="moe_combine",
    )(h, z, z, route)


def _route_pos_kernel(route_ref, cnt_ref, upper_ref, pos_ref):
    tm = route_ref.shape[0]
    cnt = cnt_ref[...]
    padded = jnp.floor((cnt + float(MOE_TILE - 1)) * (1.0 / MOE_TILE)) * float(MOE_TILE)
    p_hi, p_mid, p_lo = _split3(padded)
    upper = upper_ref[...]
    starts = (_dot(p_hi, upper) + _dot(p_mid, upper) + _dot(p_lo, upper))[0:1, :]
    route = route_ref[...]
    lane_f = lax.broadcasted_iota(jnp.int32, (tm, LANES), 1).astype(F32)
    diag = (lax.broadcasted_iota(jnp.int32, (LANES, LANES), 0)
            == lax.broadcasted_iota(jnp.int32, (LANES, LANES), 1))
    for k in range(2):
        e_lane = route[:, k:k + 1] + float(EXPERT_LANE0)
        pos = jnp.sum(jnp.where(lane_f == e_lane, starts, 0.0), axis=-1, keepdims=True) + route[:, 4 + k:5 + k]
        for r in range(tm // LANES):
            col = pos[r * LANES:(r + 1) * LANES, :]
            row = jnp.sum(jnp.where(diag, col, 0.0), axis=0, keepdims=True)
            pos_ref[k, r:r + 1, :] = row.astype(jnp.int32)


def _route_positions(route, counts, tm):
    t = route.shape[0]
    idx = jnp.arange(LANES)
    upper = (idx[:, None] < idx[None, :]).astype(BF16)
    return pl.pallas_call(
        _route_pos_kernel,
        out_shape=jax.ShapeDtypeStruct((2, t // LANES, LANES), jnp.int32),
        grid=(t // tm,),
        in_specs=[pl.BlockSpec((tm, LANES), lambda i: (i, 0)), pl.BlockSpec((8, LANES), lambda i: (0, 0)),
                  pl.BlockSpec((LANES, LANES), lambda i: (0, 0))],
        out_specs=pl.BlockSpec((2, tm // LANES, LANES), lambda i: (0, i, 0)),
        compiler_params=pltpu.CompilerParams(dimension_semantics=("parallel",),
                                             vmem_limit_bytes=VMEM_LIMIT),
        name="route_positions",
    )(route, counts, upper)


def _moe_routed(h, t_packed, route, counts, w, tm, run_first):
    t = h.shape[0]
    pos = _route_positions(route, counts, min(32 * LANES, t))
    pos1 = pos[0].reshape(t)
    pos2 = pos[1].reshape(t)
    cnt = counts[0, EXPERT_LANE0:EXPERT_LANE0 + N_EXPERTS].astype(jnp.int32)
    padded = (cnt + MOE_TILE - 1) // MOE_TILE * MOE_TILE
    ends = jnp.cumsum(padded)
    n_rows = 2 * t + N_EXPERTS * MOE_TILE
    n_tiles = ends[N_EXPERTS - 1] // MOE_TILE
    tile_start = jnp.arange(n_rows // MOE_TILE, dtype=jnp.int32) * MOE_TILE
    tile_start = jnp.minimum(tile_start, ends[N_EXPERTS - 1] - MOE_TILE)
    tile_expert = jnp.sum((tile_start[:, None] >= ends[None, :]).astype(jnp.int32), axis=1)
    xs = _sc_scatter_rows(t_packed, pos1, pos2, n_rows)
    order = lax.bitcast_convert_type(run_first.reshape(-1)[:1].astype(F32), jnp.int32)
    out = _moe_grouped(xs, tile_expert, n_tiles.reshape(1), order, w)
    z = _sc_gather_rows(out, pos.reshape(2 * t))
    return _moe_combine(h, z, route, tm)


def _pad_lanes(a, width=LANES):
    return jnp.pad(a, ((0, 0), (0, width - a.shape[1])))


def _prep_weights(norm1, w_in, q_norm, k_norm, conv_w, conv_b, dt_bias, a_log, d_skip, ssm_norm, w_out,
                  norm2, w_grp, b_grp, w_exp, b_exp, w_gate, w_up, w_down):
    w = {}
    w["norm1"] = norm1.reshape(1, D_MODEL)
    w["w_in"] = _pad_lanes(w_in, XBC_END + LANES).astype(BF16)
    w["qkn"] = jnp.concatenate([jnp.tile(q_norm, N_HEADS), jnp.tile(k_norm, KV_HEADS)]).reshape(1, QK_WIDTH)
    head_of_col = jnp.arange(QK_WIDTH) // HEAD_DIM
    red = (head_of_col[:, None] == jnp.arange(LANES)[None, :])
    w["red"] = red.astype(BF16)
    w["exp"] = red.T.astype(BF16)
    w["conv_w"] = conv_w
    w["conv_b"] = conv_b.reshape(1, CONV_DIM)
    w["dt_bias"] = _pad_lanes(dt_bias.reshape(1, SSM_HEADS))
    w["a_log"] = _pad_lanes(a_log.reshape(1, SSM_HEADS))
    w["d_skip"] = jnp.repeat(d_skip, SSM_HEAD_DIM).reshape(1, SSM_WIDTH)
    w["ssm_norm"] = ssm_norm.reshape(1, SSM_WIDTH)
    idx = jnp.arange(SSD_CHUNK)
    w["tri"] = (idx[None, :] <= idx[:, None]).astype(BF16)
    lane_head = jnp.arange(SSM_WIDTH) // SSM_HEAD_DIM
    w["expand"] = (jnp.arange(LANES)[:, None] == lane_head[None, :]).astype(BF16)
    w["w_out"] = w_out.astype(BF16)
    w["norm2"] = norm2.reshape(1, D_MODEL)
    wr = jnp.zeros((D_MODEL, LANES), F32)
    wr = wr.at[:, :N_EGROUPS].set(w_grp).at[:, EXPERT_LANE0:EXPERT_LANE0 + N_EXPERTS].set(w_exp)
    wr_hi = wr.astype(BF16)
    w["wr"] = jnp.concatenate([wr_hi, (wr - wr_hi.astype(F32)).astype(BF16)], axis=1)
    br = jnp.zeros((1, LANES), F32)
    w["br"] = br.at[0, :N_EGROUPS].set(b_grp).at[0, EXPERT_LANE0:EXPERT_LANE0 + N_EXPERTS].set(b_exp)
    w["w_gate"], w["w_up"], w["w_down"] = w_gate, w_up, w_down
    return w


def _rope_tables(pos):
    inv = 1.0 / (ROPE_THETA ** (jnp.arange(0, HEAD_DIM, 2, dtype=F32) / HEAD_DIM))
    ang = pos.astype(F32)[:, None] * inv[None, :]
    cos, sin = jnp.cos(ang), jnp.sin(ang)
    reps = LANES // HEAD_DIM
    return (jnp.tile(jnp.concatenate([cos, cos], axis=-1), (1, reps)),
            jnp.tile(jnp.concatenate([-sin, sin], axis=-1), (1, reps)))


def _token_tile(t):
    for tm in (1024, 512, 256, 128, 64, 32, 16):
        if t % tm == 0:
            return tm
    raise ValueError(f"token count {t} is not a multiple of 16")


def kernel(x_prompt, x_sample, cache_win_k, cache_win_v, state_conv, state_ssm, norm1, w_in, q_norm, k_norm,
           sinks, conv_w, conv_b, dt_bias, a_log, d_skip, ssm_norm, w_out, norm2, w_grp, b_grp, w_exp, b_exp,
           w_gate, w_up, w_down):
    depth = norm1.shape[0]
    assert depth == 1, "single-layer stack"
    bp, lp, _ = x_prompt.shape
    bsn, ls, _ = x_sample.shape
    assert ls == 1 and lp % WINDOW == 0 and cache_win_k.shape[2] == WINDOW
    l = 0
    w = _prep_weights(norm1[l], w_in[l], q_norm[l], k_norm[l], conv_w[l], conv_b[l], dt_bias[l], a_log[l],
                      d_skip[l], ssm_norm[l], w_out[l], norm2[l], w_grp[l], b_grp[l], w_exp[l], b_exp[l],
                      w_gate[l], w_up[l], w_down[l])
    sink = sinks[l]

    tp = bp * lp
    xp = x_prompt.reshape(tp, D_MODEL)
    tm_p = _token_tile(lp)
    cos_p, sin_p = _rope_tables(jnp.arange(lp, dtype=jnp.int32))
    q, k, v, z, xbc, dt, k2, v2 = _inproj(xp, w, cos_p, sin_p, tm_p, lp // tm_p)
    attn = _attn_prompt(q, k2, v2, sink, bp, lp)
    ssm, st_p = _ssd_prompt(xbc, z, dt, w, bp, lp)
    h, t, route, counts = _outproj_router(xp, attn, ssm, w, tm_p)
    k3 = k.reshape(bp, lp, KV_HEADS, HEAD_DIM)
    v3 = v.reshape(bp, lp, KV_HEADS, HEAD_DIM)
    win_k_p = k3[:, lp - WINDOW:][None]
    win_v_p = v3[:, lp - WINDOW:][None]
    conv_p = xbc.reshape(bp, lp, CONV_DIM)[:, lp - (CONV_K - 1):][None]
    ssm_p = st_p.reshape(1, bp, SSM_HEADS, SSM_HEAD_DIM, SSM_STATE)

    xs2 = x_sample.reshape(bsn, D_MODEL)
    tm_s = _token_tile(bsn)
    cos_s, sin_s = _rope_tables(jnp.full((tm_s,), PAST_LEN, jnp.int32))
    q_s, k_s, v_s, z_s, xbc_s, dt_s, _, _ = _inproj(xs2, w, cos_s, sin_s, tm_s, 1)
    q4 = q_s.reshape(bsn, KV_HEADS, N_HEADS // KV_HEADS, HEAD_DIM)
    zq = jnp.zeros_like(q4[:, 0])
    qx = jnp.concatenate([jnp.concatenate([q4[:, 0], zq], axis=-1),
                          jnp.concatenate([zq, q4[:, 1]], axis=-1)], axis=1)
    qx = jnp.pad(qx, ((0, 0), (0, BF16_ROWS - N_HEADS), (0, 0)))
    sink_x = jnp.pad(jnp.broadcast_to(sink[:, None], (N_HEADS, LANES)), ((0, BF16_ROWS - N_HEADS), (0, 0)))
    kc = cache_win_k[l].reshape(bsn, WINDOW, KV_WIDTH)
    vc = cache_win_v[l].reshape(bsn, WINDOW, KV_WIDTH)
    ko, vo, attn_s = _attn_sample(qx, kc, k_s.reshape(bsn, 1, KV_WIDTH), vc, v_s.reshape(bsn, 1, KV_WIDTH),
                               sink_x, 8)
    y_prompt = _moe_routed(h, t, route, counts, w, tm_p, attn_s).reshape(bp, lp, D_MODEL)
    cprev_t = jnp.transpose(state_conv[l], (1, 0, 2))
    h0 = state_ssm[l].reshape(bsn, HEAD_PAIRS, LANES, SSM_STATE)
    ssm_s, cnew_t, h1 = _ssd_sample(xbc_s, z_s, dt_s, cprev_t, h0, w, 16)
    h_s, t_s, route_s, _ = _outproj_router(xs2, attn_s, ssm_s, w, tm_s)
    y_sample = _moe_dense(h_s, t_s, route_s, w).reshape(bsn, 1, D_MODEL)
    win_k_s = ko.reshape(1, bsn, WINDOW, KV_HEADS, HEAD_DIM)
    win_v_s = vo.reshape(1, bsn, WINDOW, KV_HEADS, HEAD_DIM)
    conv_s = jnp.transpose(cnew_t, (1, 0, 2))[None]
    ssm_s_state = h1.reshape(1, bsn, SSM_HEADS, SSM_HEAD_DIM, SSM_STATE)

    return (y_prompt, y_sample, win_k_p, win_v_p, conv_p, ssm_p, win_k_s, win_v_s, conv_s, ssm_s_state)
```

```python
import functools
import math

import jax
import jax.numpy as jnp
from jax import lax
from jax.experimental import pallas as pl
from jax.experimental.pallas import tpu as pltpu
from jax.experimental.pallas import tpu_sc as plsc

F32 = jnp.float32
BF16 = jnp.bfloat16

D_MODEL = 1024
HEAD_DIM = 64
N_HEADS = 8
KV_HEADS = 2
WINDOW = 128
ATTN_WIDTH = N_HEADS * HEAD_DIM
QK_WIDTH = ATTN_WIDTH + KV_HEADS * HEAD_DIM
KV_WIDTH = KV_HEADS * HEAD_DIM
ATTN_SCALE = HEAD_DIM ** -0.5
ROPE_THETA = 10000.0
SSM_WIDTH = 512
SSM_HEADS = 8
SSM_HEAD_DIM = 64
SSM_GROUPS = 2
SSM_STATE = 128
CONV_K = 4
CONV_HALO = 8
CONV_DIM = SSM_WIDTH + 2 * SSM_GROUPS * SSM_STATE
SSD_CHUNK = 128
N_EGROUPS = 4
EXP_PER_GROUP = 8
N_EXPERTS = 32
EXPERT_FF = 128
EPS = 1e-6
PAST_LEN = 16384

LANES = 128
BF16_ROWS = 16
HEAD_PAIRS = SSM_HEADS // 2
EXPERT_LANE0 = 32
VMEM_LIMIT = 56 * 1024 * 1024
MOE_TILE = 512
MOE_SUBTILE = 128
ATTN_QBLOCKS = 8
SSD_CHUNKS_PER_STEP = 8
COUNT_BLOCK = 256
ROUTER_SUBTILE = 1024
INPROJ_SUBTILE = 512
SC_CORES = 2
SC_SUBCORES = 16
SC_WORKERS = SC_CORES * SC_SUBCORES
SC_CHUNK = 64

Q_END = ATTN_WIDTH
K_END = Q_END + KV_WIDTH
V_END = K_END + KV_WIDTH
Z_END = V_END + SSM_WIDTH
XBC_END = Z_END + CONV_DIM


def _dot(a, b):
    return jnp.dot(a, b, preferred_element_type=F32)


def _dot_nt(a, b):
    return lax.dot_general(a, b, (((1,), (1,)), ((), ())), preferred_element_type=F32)


def _split2(v):
    hi = v.astype(BF16)
    lo = (v - hi.astype(F32)).astype(BF16)
    return hi, lo


def _split3(v):
    hi = v.astype(BF16)
    r = v - hi.astype(F32)
    mid = r.astype(BF16)
    lo = (r - mid.astype(F32)).astype(BF16)
    return hi, mid, lo


def _silu(x):
    return x * jax.nn.sigmoid(x)


def _softplus(x):
    return jnp.maximum(x, 0.0) + jnp.log1p(jnp.exp(-jnp.abs(x)))


def _lane_bcast_pairs(v, n_pairs):
    r = v.shape[0]
    lo = lax.broadcasted_iota(jnp.int32, (r, LANES), 1) < HEAD_DIM
    slabs = []
    for j in range(n_pairs):
        a = jnp.broadcast_to(v[:, 2 * j:2 * j + 1], (r, LANES))
        b = jnp.broadcast_to(v[:, 2 * j + 1:2 * j + 2], (r, LANES))
        slabs.append(jnp.where(lo, a, b))
    return jnp.concatenate(slabs, axis=1)


def _causal_conv_silu(x_ext, convw_ref, convb_ref):
    halo = CONV_HALO
    x_raw = x_ext[halo:, :]
    conv = convb_ref[...] + x_raw * convw_ref[CONV_K - 1:CONV_K, :]
    for j in range(CONV_K - 1):
        shifted = pltpu.roll(x_ext, CONV_K - 1 - j, axis=0)[halo:, :]
        conv = conv + shifted * convw_ref[j:j + 1, :]
    return _silu(conv)


def _inproj_kernel(x_ref, n1_ref, win_ref, dtb_ref, qkn_ref,
                   cos_ref, sin_ref, red_ref, exp_ref,
                   q_ref, k_ref, v_ref, z_ref, xbc_ref, dt_ref, k2_ref, v2_ref):
    tm = x_ref.shape[0]
    sub = min(tm, INPROJ_SUBTILE)
    lane = lax.broadcasted_iota(jnp.int32, (sub, LANES), 1)
    first_half = (lane % HEAD_DIM) < (HEAD_DIM // 2)
    for s in range(tm // sub):
        rows = slice(s * sub, (s + 1) * sub)
        x = x_ref[rows, :]
        ms = jnp.mean(x * x, axis=-1, keepdims=True)
        xn = (x * lax.rsqrt(ms + EPS) * n1_ref[...]).astype(BF16)
        v = _dot(xn, win_ref[:, K_END:V_END])
        v_ref[rows, :] = v
        v2_ref[rows, :] = _pair_operands(v)
        z_ref[rows, :] = _dot(xn, win_ref[:, V_END:Z_END])
        xbc_ref[rows, :] = _dot(xn, win_ref[:, Z_END:XBC_END])
        dt_ref[rows, :] = _softplus(_dot(xn, win_ref[:, XBC_END:]) + dtb_ref[...])
        qk = _dot(xn, win_ref[:, :K_END])
        ss = _dot((qk * qk).astype(BF16), red_ref[...])
        inv = lax.rsqrt(ss * (1.0 / HEAD_DIM) + EPS)
        inv_hi, inv_lo = _split2(inv)
        inv_x = _dot(inv_hi, exp_ref[...]) + _dot(inv_lo, exp_ref[...])
        qkn = qk * inv_x * qkn_ref[...]
        cos = cos_ref[rows, :]
        sin = sin_ref[rows, :]
        for c in range(QK_WIDTH // LANES):
            xc = qkn[:, c * LANES:(c + 1) * LANES]
            partner = jnp.where(first_half,
                                pltpu.roll(xc, LANES - HEAD_DIM // 2, axis=1),
                                pltpu.roll(xc, HEAD_DIM // 2, axis=1))
            rot = xc * cos + partner * sin
            if c < ATTN_WIDTH // LANES:
                q_ref[rows, c * LANES:(c + 1) * LANES] = (rot * ATTN_SCALE).astype(BF16)
            else:
                k_ref[rows, :] = rot
                k2_ref[rows, :] = _pair_operands(rot)


def _inproj(x2d, w, cos_tab, sin_tab, tm, n_pos_blocks):
    t = x2d.shape[0]
    grid = (t // tm,)
    tok = lambda i: (i, 0)
    const = lambda i: (0, 0)
    pos = lambda i: (i % n_pos_blocks, 0)
    full = lambda a: pl.BlockSpec(a.shape, const)
    rows = lambda width, dtype: (jax.ShapeDtypeStruct((t, width), dtype), pl.BlockSpec((tm, width), tok))
    outs = [rows(ATTN_WIDTH, BF16), rows(KV_WIDTH, F32), rows(KV_WIDTH, F32), rows(SSM_WIDTH, F32)]
    operands = [x2d, w["norm1"], w["w_in"], w["dt_bias"], w["qkn"],
                cos_tab, sin_tab, w["red"], w["exp"]]
    in_specs = [
        pl.BlockSpec((tm, D_MODEL), tok),
        full(w["norm1"]), full(w["w_in"]), full(w["dt_bias"]), full(w["qkn"]),
        pl.BlockSpec((tm, LANES), pos), pl.BlockSpec((tm, LANES), pos),
        full(w["red"]), full(w["exp"]),
    ]
    outs += [rows(CONV_DIM, F32), rows(LANES, F32), rows(4 * LANES, BF16), rows(4 * LANES, BF16)]
    return pl.pallas_call(
        _inproj_kernel,
        out_shape=tuple(o[0] for o in outs),
        grid=grid,
        in_specs=in_specs,
        out_specs=tuple(o[1] for o in outs),
        compiler_params=pltpu.CompilerParams(dimension_semantics=("parallel",),
                                             vmem_limit_bytes=VMEM_LIMIT),
        name="inproj",
    )(*operands)


def _pair_operands(kv):
    lo = lax.broadcasted_iota(jnp.int32, kv.shape, 1) < HEAD_DIM
    swapped = pltpu.roll(kv, HEAD_DIM, axis=1)
    parts = [jnp.where(lo, kv, 0.0), jnp.where(lo, 0.0, swapped), jnp.where(lo, swapped, 0.0), jnp.where(lo, 0.0, kv)]
    return jnp.concatenate(parts, axis=1).astype(BF16)


def _pair_rhs(blk, g):
    return jnp.concatenate([blk[:, 2 * g * LANES:(2 * g + 1) * LANES],
                            blk[:, (2 * g + 1) * LANES:(2 * g + 2) * LANES]], axis=0)


def _attn_qblock(sink_ref, q_blk, k_prev, k_cur, v_prev, v_cur, seq_start, o_ref):
    blk = WINDOW
    qi = lax.broadcasted_iota(jnp.int32, (blk, 2 * blk), 0)
    kj = lax.broadcasted_iota(jnp.int32, (blk, 2 * blk), 1) % blk
    cur_ok = kj <= qi
    lo = lax.broadcasted_iota(jnp.int32, (blk, LANES), 1) < HEAD_DIM
    n_pairs = N_HEADS // KV_HEADS // 2
    for g in range(KV_HEADS):
        q_all = jnp.concatenate([q_blk[:, (g * n_pairs + r) * LANES:(g * n_pairs + r + 1) * LANES]
                                 for r in range(n_pairs)], axis=0)
        s_all = _dot_nt(q_all, jnp.concatenate([_pair_rhs(k_cur, g), _pair_rhs(k_prev, g)], axis=0))
        p_rows = []
        den_rows = []
        for r in range(n_pairs):
            pair = g * n_pairs + r
            s_cur = s_all[r * blk:(r + 1) * blk, :2 * blk]
            s_prev = s_all[r * blk:(r + 1) * blk, 2 * blk:]
            s = jnp.where(cur_ok, s_cur, s_prev)
            if seq_start is not None:
                s = jnp.where(jnp.logical_or(cur_ok, jnp.logical_not(seq_start)), s, -jnp.inf)
            ps = []
            dens = []
            for hh in range(2):
                sink = sink_ref[2 * pair + hh]
                sh = s[:, hh * blk:(hh + 1) * blk]
                m = jnp.maximum(jnp.max(sh, axis=-1, keepdims=True), sink)
                p = jnp.exp(sh - m)
                dens.append(jnp.sum(p, axis=-1, keepdims=True) + jnp.exp(sink - m))
                ps.append(p)
            p2 = jnp.concatenate(ps, axis=1)
            p_rows.append(jnp.concatenate([jnp.where(cur_ok, p2, 0.0), jnp.where(cur_ok, 0.0, p2)],
                                          axis=1).astype(BF16))
            den_rows.append(jnp.where(lo, dens[0], dens[1]))
        o_all = _dot(jnp.concatenate(p_rows, axis=0),
                     jnp.concatenate([_pair_rhs(v_cur, g), _pair_rhs(v_prev, g)], axis=0))
        for r in range(n_pairs):
            pair = g * n_pairs + r
            o2 = o_all[r * blk:(r + 1) * blk, :]
            o_ref[:, pair * LANES:(pair + 1) * LANES] = (o2 / den_rows[r]).astype(BF16)


def _attn_kernel(sink_ref, q_ref, kc_ref, kp_ref, vc_ref, vp_ref, o_ref):
    blk = WINDOW
    first_step = pl.program_id(1) == 0
    for u in range(q_ref.shape[0] // blk):
        rows = slice(u * blk, (u + 1) * blk)
        prev_rows = slice((u - 1) * blk, u * blk)
        k_prev = kp_ref[...] if u == 0 else kc_ref[prev_rows, :]
        v_prev = vp_ref[...] if u == 0 else vc_ref[prev_rows, :]
        _attn_qblock(sink_ref, q_ref[rows, :], k_prev, kc_ref[rows, :], v_prev, vc_ref[rows, :],
                     first_step if u == 0 else None, o_ref.at[rows, :])


def _attn_prompt(q, k, v, sinks, batch, seq):
    n_sub = ATTN_QBLOCKS if seq % (ATTN_QBLOCKS * WINDOW) == 0 else 1
    rows = n_sub * WINDOW
    nb = seq // rows
    cur = lambda b, j, s: (b * nb + j, 0)
    prev = lambda b, j, s: (jnp.maximum((b * nb + j) * n_sub - 1, 0), 0)
    return pl.pallas_call(
        _attn_kernel,
        out_shape=jax.ShapeDtypeStruct((batch * seq, ATTN_WIDTH), BF16),
        grid_spec=pltpu.PrefetchScalarGridSpec(
            num_scalar_prefetch=1,
            grid=(batch, nb),
            in_specs=[
                pl.BlockSpec((rows, ATTN_WIDTH), cur),
                pl.BlockSpec((rows, 4 * LANES), cur), pl.BlockSpec((WINDOW, 4 * LANES), prev),
                pl.BlockSpec((rows, 4 * LANES), cur), pl.BlockSpec((WINDOW, 4 * LANES), prev),
            ],
            out_specs=pl.BlockSpec((rows, ATTN_WIDTH), cur),
        ),
        compiler_params=pltpu.CompilerParams(dimension_semantics=("parallel", "parallel"),
                                             vmem_limit_bytes=VMEM_LIMIT),
        name="attn_prompt",
    )(sinks, q, k, k, v, v)


def _ssd_kernel(xbc_ref, z_ref, dt_ref, convw_ref, convb_ref, alog_ref, dskip_ref, nw_ref,
                tri_ref, expand_ref, y_ref, st_ref, buf_ref, state_ref):
    c = pl.program_id(1)
    cl = SSD_CHUNK
    n_sub = xbc_ref.shape[0] // cl
    halo = CONV_HALO

    @pl.when(c == 0)
    def _():
        buf_ref[...] = jnp.zeros(buf_ref.shape, F32)
        state_ref[...] = jnp.zeros(state_ref.shape, F32)

    lane = lax.broadcasted_iota(jnp.int32, (1, LANES), 1)
    a_neg = jnp.where(lane < SSM_HEADS, -jnp.exp(alog_ref[...]), 0.0)
    tri = tri_ref[...]
    for u in range(n_sub):
        rows = slice(u * cl, (u + 1) * cl)
        if u == 0:
            x_ext = jnp.concatenate([buf_ref[...], xbc_ref[rows, :]], axis=0)
        else:
            x_ext = xbc_ref[u * cl - halo:(u + 1) * cl, :]
        _ssd_chunk(x_ext, z_ref[rows, :], dt_ref[rows, :], a_neg, tri, convw_ref, convb_ref, dskip_ref, nw_ref,
                   expand_ref, y_ref.at[rows, :], state_ref)
    buf_ref[...] = xbc_ref[n_sub * cl - halo:n_sub * cl, :]

    @pl.when(c == pl.num_programs(1) - 1)
    def _():
        st_ref[0] = state_ref[...]


def _ssd_chunk(x_ext, z, dt, a_neg, tri, convw_ref, convb_ref, dskip_ref, nw_ref, expand_ref, y_ref, state_ref):
    cl = SSD_CHUNK
    act = _causal_conv_silu(x_ext, convw_ref, convb_ref)
    xs = act[:, :SSM_WIDTH]
    bm = act[:, SSM_WIDTH:SSM_WIDTH + SSM_GROUPS * SSM_STATE].astype(BF16)
    cm = act[:, SSM_WIDTH + SSM_GROUPS * SSM_STATE:].astype(BF16)

    dta = dt * a_neg
    p_hi, p_mid, p_lo = _split3(dta)
    a_col = _dot(tri, p_hi) + _dot(tri, p_mid) + _dot(tri, p_lo)
    a_last = a_col[cl - 1:cl, :]
    a_row = a_col.T
    per_head = jnp.concatenate([dt, jnp.exp(a_col), jnp.exp(a_last - a_col)], axis=0)
    ph_hi, ph_lo = _split2(per_head)
    per_lane = _dot(ph_hi, expand_ref[...]) + _dot(ph_lo, expand_ref[...])
    dt_x = per_lane[:cl]
    ecol_x = per_lane[cl:2 * cl]
    dte_x = per_lane[2 * cl:]
    e_last = jnp.exp(a_last)
    xdt = xs * dt_x

    li = lax.broadcasted_iota(jnp.int32, (cl, cl), 0)
    si = lax.broadcasted_iota(jnp.int32, (cl, cl), 1)
    causal = si <= li
    lo = lax.broadcasted_iota(jnp.int32, (cl, LANES), 1) < SSM_HEAD_DIM
    row_lo = lax.broadcasted_iota(jnp.int32, (LANES, SSM_STATE), 0) < SSM_HEAD_DIM

    ys = []
    for g in range(SSM_GROUPS):
        b_g = bm[:, g * SSM_STATE:(g + 1) * SSM_STATE]
        c_g = cm[:, g * SSM_STATE:(g + 1) * SSM_STATE]
        cb = _dot_nt(c_g, b_g)
        for r in range(HEAD_PAIRS // SSM_GROUPS):
            j = g * (HEAD_PAIRS // SSM_GROUPS) + r
            sl = slice(j * LANES, (j + 1) * LANES)
            xdt_p = xdt[:, sl]
            ms = []
            for hh in range(2):
                h = 2 * j + hh
                seg = a_col[:, h:h + 1] - a_row[h:h + 1, :]
                ms.append(cb * jnp.exp(jnp.where(causal, seg, -jnp.inf)))
            m2 = jnp.concatenate(ms, axis=1).astype(BF16)
            rhs = jnp.concatenate([jnp.where(lo, xdt_p, 0.0), jnp.where(lo, 0.0, xdt_p)],
                                  axis=0).astype(BF16)
            y_diag = _dot(m2, rhs)
            st = state_ref[j]
            y_off = _dot_nt(c_g, st.astype(BF16)) * ecol_x[:, sl]
            xdt_e = (xdt_p * dte_x[:, sl]).T.astype(BF16)
            d_a = e_last[:, 2 * j:2 * j + 1]
            d_b = e_last[:, 2 * j + 1:2 * j + 2]
            decay = jnp.where(row_lo, jnp.broadcast_to(d_a, row_lo.shape), jnp.broadcast_to(d_b, row_lo.shape))
            state_ref[j] = decay * st + _dot(xdt_e, b_g)
            ys.append(y_diag + y_off + dskip_ref[:, sl] * xs[:, sl])
    y = jnp.concatenate(ys, axis=1)
    gated = y * _silu(z)
    gw = SSM_WIDTH // SSM_GROUPS
    outs = []
    for g in range(SSM_GROUPS):
        gg = gated[:, g * gw:(g + 1) * gw]
        outs.append(gg * lax.rsqrt(jnp.mean(gg * gg, axis=-1, keepdims=True) + EPS))
    y_ref[...] = (jnp.concatenate(outs, axis=1) * nw_ref[...]).astype(BF16)


def _ssd_prompt(xbc, z, dt, w, batch, seq):
    n_sub = SSD_CHUNKS_PER_STEP if seq % (SSD_CHUNKS_PER_STEP * SSD_CHUNK) == 0 else 1
    rows = n_sub * SSD_CHUNK
    nc = seq // rows
    tok = lambda b, c: (b * nc + c, 0)
    const = lambda b, c: (0, 0)
    full = lambda a: pl.BlockSpec(a.shape, const)
    return pl.pallas_call(
        _ssd_kernel,
        out_shape=(jax.ShapeDtypeStruct((batch * seq, SSM_WIDTH), BF16),
                   jax.ShapeDtypeStruct((batch, HEAD_PAIRS, LANES, SSM_STATE), F32)),
        grid=(batch, nc),
        in_specs=[
            pl.BlockSpec((rows, CONV_DIM), tok), pl.BlockSpec((rows, SSM_WIDTH), tok),
            pl.BlockSpec((rows, LANES), tok),
            full(w["conv_w"]), full(w["conv_b"]), full(w["a_log"]),
            full(w["d_skip"]), full(w["ssm_norm"]), full(w["tri"]), full(w["expand"]),
        ],
        out_specs=(pl.BlockSpec((rows, SSM_WIDTH), tok),
                   pl.BlockSpec((1, HEAD_PAIRS, LANES, SSM_STATE), lambda b, c: (b, 0, 0, 0))),
        scratch_shapes=[pltpu.VMEM((CONV_HALO, CONV_DIM), F32),
                        pltpu.VMEM((HEAD_PAIRS, LANES, SSM_STATE), F32)],
        compiler_params=pltpu.CompilerParams(dimension_semantics=("parallel", "arbitrary"),
                                             vmem_limit_bytes=VMEM_LIMIT),
        name="ssd_prompt",
    )(xbc, z, dt, w["conv_w"], w["conv_b"], w["a_log"], w["d_skip"], w["ssm_norm"], w["tri"], w["expand"])


def _attn_sample_kernel(qx_ref, kc_ref, kn_ref, vc_ref, vn_ref, sink_ref, ko_ref, vo_ref, o_ref):
    bs = qx_ref.shape[0]
    w = kc_ref.shape[1]
    sink = sink_ref[...]
    lo = lax.broadcasted_iota(jnp.int32, (1, LANES), 1) < HEAD_DIM
    for i in range(bs):
        ko_ref[i, 0:w - 1, :] = kc_ref[i, 1:w, :]
        ko_ref[i, w - 1:w, :] = kn_ref[i]
        vo_ref[i, 0:w - 1, :] = vc_ref[i, 1:w, :]
        vo_ref[i, w - 1:w, :] = vn_ref[i]
        s = _dot_nt(qx_ref[i], ko_ref[i].astype(BF16))
        m = jnp.maximum(jnp.max(s, axis=-1, keepdims=True), sink)
        p = jnp.exp(s - m)
        den = jnp.sum(p, axis=-1, keepdims=True) + jnp.exp(sink - m)
        o = _dot(p.astype(BF16), vo_ref[i].astype(BF16)) / den
        o_sw = pltpu.roll(o, HEAD_DIM, axis=1)
        for j in range(N_HEADS // 2):
            a, b = (o, o_sw) if j < N_HEADS // 4 else (o_sw, o)
            o_ref[i:i + 1, j * LANES:(j + 1) * LANES] = jnp.where(lo, a[2 * j:2 * j + 1], b[2 * j + 1:2 * j + 2])


def _attn_sample(qx, kc, kn, vc, vn, sink_x, bs):
    n, w = kc.shape[0], kc.shape[1]
    blk3 = lambda i: (i, 0, 0)
    return pl.pallas_call(
        _attn_sample_kernel,
        out_shape=(jax.ShapeDtypeStruct((n, w, KV_WIDTH), F32),
                   jax.ShapeDtypeStruct((n, w, KV_WIDTH), F32),
                   jax.ShapeDtypeStruct((n, ATTN_WIDTH), F32)),
        grid=(n // bs,),
        in_specs=[
            pl.BlockSpec((bs, BF16_ROWS, LANES), blk3),
            pl.BlockSpec((bs, w, KV_WIDTH), blk3), pl.BlockSpec((bs, 1, KV_WIDTH), blk3),
            pl.BlockSpec((bs, w, KV_WIDTH), blk3), pl.BlockSpec((bs, 1, KV_WIDTH), blk3),
            pl.BlockSpec(sink_x.shape, lambda i: (0, 0)),
        ],
        out_specs=(pl.BlockSpec((bs, w, KV_WIDTH), blk3), pl.BlockSpec((bs, w, KV_WIDTH), blk3),
                   pl.BlockSpec((bs, ATTN_WIDTH), lambda i: (i, 0))),
        compiler_params=pltpu.CompilerParams(dimension_semantics=("parallel",),
                                             vmem_limit_bytes=VMEM_LIMIT),
        name="attn_sample",
    )(qx, kc, kn, vc, vn, sink_x)


def _ssd_sample_kernel(xbc_ref, z_ref, dt_ref, cprev_ref, h0_ref, convw_ref, convb_ref, alog_ref,
                       dskip_ref, nw_ref, y_ref, cnew_ref, h1_ref):
    bs = xbc_ref.shape[0]
    x_raw = xbc_ref[...]
    conv = convb_ref[...] + x_raw * convw_ref[CONV_K - 1:CONV_K, :]
    for j in range(CONV_K - 1):
        conv = conv + cprev_ref[j] * convw_ref[j:j + 1, :]
    for j in range(CONV_K - 2):
        cnew_ref[j] = cprev_ref[j + 1]
    cnew_ref[CONV_K - 2] = x_raw
    act = _silu(conv)
    xs = act[:, :SSM_WIDTH]
    bm = act[:, SSM_WIDTH:SSM_WIDTH + SSM_GROUPS * SSM_STATE].astype(BF16)
    cm = act[:, SSM_WIDTH + SSM_GROUPS * SSM_STATE:].astype(BF16)
    lane = lax.broadcasted_iota(jnp.int32, (1, LANES), 1)
    a_neg = jnp.where(lane < SSM_HEADS, -jnp.exp(alog_ref[...]), 0.0)
    dt = dt_ref[...]
    dec = jnp.exp(dt * a_neg)
    xdt = xs * _lane_bcast_pairs(dt, HEAD_PAIRS)
    rowid = lax.broadcasted_iota(jnp.int32, (bs, LANES), 0)
    row_lo = lax.broadcasted_iota(jnp.int32, (LANES, SSM_STATE), 0) < SSM_HEAD_DIM
    ys = []
    for j in range(HEAD_PAIRS):
        g = j // (HEAD_PAIRS // SSM_GROUPS)
        sl = slice(j * LANES, (j + 1) * LANES)
        b_g = bm[:, g * SSM_STATE:(g + 1) * SSM_STATE]
        c_g = cm[:, g * SSM_STATE:(g + 1) * SSM_STATE]
        xdt_p = xdt[:, sl]
        y_p = jnp.zeros((bs, LANES), F32)
        for i in range(bs):
            xi = jnp.where(rowid == i, xdt_p, 0.0).T.astype(BF16)
            d_a = dec[i:i + 1, 2 * j:2 * j + 1]
            d_b = dec[i:i + 1, 2 * j + 1:2 * j + 2]
            decay = jnp.where(row_lo, jnp.broadcast_to(d_a, row_lo.shape), jnp.broadcast_to(d_b, row_lo.shape))
            new = decay * h0_ref[i, j] + _dot(xi, b_g)
            h1_ref[i, j] = new
            y_p = y_p + jnp.where(rowid == i, _dot_nt(c_g, new.astype(BF16)), 0.0)
        ys.append(y_p + dskip_ref[:, sl] * xs[:, sl])
    y = jnp.concatenate(ys, axis=1)
    gated = y * _silu(z_ref[...])
    gw = SSM_WIDTH // SSM_GROUPS
    outs = []
    for g in range(SSM_GROUPS):
        gg = gated[:, g * gw:(g + 1) * gw]
        outs.append(gg * lax.rsqrt(jnp.mean(gg * gg, axis=-1, keepdims=True) + EPS))
    y_ref[...] = (jnp.concatenate(outs, axis=1) * nw_ref[...]).astype(BF16)


def _ssd_sample(xbc, z, dt, cprev_t, h0, w, bs):
    n = xbc.shape[0]
    tok = lambda i: (i, 0)
    const = lambda i: (0, 0)
    full = lambda a: pl.BlockSpec(a.shape, const)
    return pl.pallas_call(
        _ssd_sample_kernel,
        out_shape=(jax.ShapeDtypeStruct((n, SSM_WIDTH), BF16),
                   jax.ShapeDtypeStruct((CONV_K - 1, n, CONV_DIM), F32),
                   jax.ShapeDtypeStruct((n, HEAD_PAIRS, LANES, SSM_STATE), F32)),
        grid=(n // bs,),
        in_specs=[
            pl.BlockSpec((bs, CONV_DIM), tok), pl.BlockSpec((bs, SSM_WIDTH), tok),
            pl.BlockSpec((bs, LANES), tok),
            pl.BlockSpec((CONV_K - 1, bs, CONV_DIM), lambda i: (0, i, 0)),
            pl.BlockSpec((bs, HEAD_PAIRS, LANES, SSM_STATE), lambda i: (i, 0, 0, 0)),
            full(w["conv_w"]), full(w["conv_b"]), full(w["a_log"]),
            full(w["d_skip"]), full(w["ssm_norm"]),
        ],
        out_specs=(pl.BlockSpec((bs, SSM_WIDTH), tok),
                   pl.BlockSpec((CONV_K - 1, bs, CONV_DIM), lambda i: (0, i, 0)),
                   pl.BlockSpec((bs, HEAD_PAIRS, LANES, SSM_STATE), lambda i: (i, 0, 0, 0))),
        compiler_params=pltpu.CompilerParams(dimension_semantics=("parallel",),
                                             vmem_limit_bytes=VMEM_LIMIT),
        name="ssd_sample",
    )(xbc, z, dt, cprev_t, h0, w["conv_w"], w["conv_b"], w["a_log"], w["d_skip"], w["ssm_norm"])


def _pack_bf16_pair(v):
    c = v.shape[1] // 2
    hi = lax.bitcast_convert_type(v[:, :c].astype(BF16).astype(F32), jnp.uint32)
    lo = lax.bitcast_convert_type(v[:, c:].astype(BF16).astype(F32), jnp.uint32)
    return hi | (lo >> 16)


def _unpack_bf16_pair(word):
    a = lax.bitcast_convert_type(word & jnp.uint32(0xFFFF0000), F32)
    b = lax.bitcast_convert_type(word << 16, F32)
    return a, b


def _outproj_router_kernel(x_ref, a_ref, s_ref, wo_ref, n2_ref, wr_ref, br_ref, tri_ref,
                           h_ref, t_ref, route_ref, cnt_ref, carry_ref):
    @pl.when(pl.program_id(0) == 0)
    def _():
        carry_ref[...] = jnp.zeros(carry_ref.shape, F32)

    tm = x_ref.shape[0]
    h_ref[...] = (x_ref[...] + _dot(a_ref[...].astype(BF16), wo_ref[:ATTN_WIDTH, :])
                  + _dot(s_ref[...].astype(BF16), wo_ref[ATTN_WIDTH:, :]))
    sub = min(tm, ROUTER_SUBTILE)
    carry = carry_ref[:, 0:1]
    for s in range(tm // sub):
        rows = slice(s * sub, (s + 1) * sub)
        carry = _route_rows(h_ref[rows, :], n2_ref, wr_ref, br_ref, tri_ref, carry, t_ref.at[rows, :],
                            route_ref.at[rows, :])
    carry_ref[...] = jnp.broadcast_to(carry, carry_ref.shape)
    cpad = jnp.concatenate([jnp.zeros((EXPERT_LANE0, LANES), F32), jnp.broadcast_to(carry, (N_EXPERTS, LANES)),
                            jnp.zeros((LANES - EXPERT_LANE0 - N_EXPERTS, LANES), F32)], axis=0)
    cnt_ref[...] = cpad.T[0:cnt_ref.shape[0], :]


def _route_rows(h, n2_ref, wrt_ref, brc_ref, triu_ref, carry, t_ref, route_ref):
    ms = jnp.mean(h * h, axis=-1, keepdims=True)
    t = h * lax.rsqrt(ms + EPS) * n2_ref[...]
    t_hi, t_lo = _split2(t)
    t_ref[...] = _pack_bf16_pair(t)
    n = h.shape[0]
    a = _dot_nt(wrt_ref[...], t_hi)
    b = _dot_nt(wrt_ref[0:LANES, :], t_lo)

    def logit_rows(r0, r1):
        return a[r0:r1] + a[LANES + r0:LANES + r1] + b[r0:r1] + brc_ref[r0:r1, :]

    sl = EXP_PER_GROUP
    glog = logit_rows(0, sl)
    elog = logit_rows(EXPERT_LANE0, EXPERT_LANE0 + N_EXPERTS)
    row = lax.broadcasted_iota(jnp.int32, (sl, n), 0).astype(F32)
    big = float(sl)
    ninf = -jnp.inf
    gm = jnp.where(row < N_EGROUPS, glog, ninf)
    gmax = jnp.max(gm, axis=0, keepdims=True)
    g_top = 1.0 / jnp.sum(jnp.exp(gm - gmax), axis=0, keepdims=True)
    g_idx = jnp.min(jnp.where(gm == gmax, row, big), axis=0, keepdims=True)
    ml = elog[(N_EGROUPS - 1) * sl:]
    for g in range(N_EGROUPS - 2, -1, -1):
        ml = jnp.where(g_idx == float(g), elog[g * sl:(g + 1) * sl], ml)
    m1 = jnp.max(ml, axis=0, keepdims=True)
    i1 = jnp.min(jnp.where(ml == m1, row, big), axis=0, keepdims=True)
    ml2 = jnp.where(row == i1, ninf, ml)
    m2 = jnp.max(ml2, axis=0, keepdims=True)
    i2 = jnp.min(jnp.where(ml2 == m2, row, big), axis=0, keepdims=True)
    r = jnp.exp(m2 - m1)
    w1 = g_top / (1.0 + r)
    w2 = g_top * r / (1.0 + r)
    e1 = g_idx * float(sl) + i1
    e2 = g_idx * float(sl) + i2
    pick = jnp.where(jnp.logical_or(row == i1, row == i2), 1.0, 0.0)
    onehot = jnp.concatenate([jnp.where(g_idx == float(g), pick, 0.0) for g in range(N_EGROUPS)], axis=0)
    onehot_bf = onehot.astype(BF16)
    cb = triu_ref.shape[0]
    cums = []
    for blk in range(n // cb):
        c = _dot(onehot_bf[:, blk * cb:(blk + 1) * cb], triu_ref[...]) + carry
        carry = c[:, cb - 1:cb]
        cums.append(c)
    before = jnp.concatenate(cums, axis=1) - onehot
    erow = lax.broadcasted_iota(jnp.int32, (N_EXPERTS, n), 0).astype(F32)
    rank1 = jnp.sum(jnp.where(erow == e1, before, 0.0), axis=0, keepdims=True)
    rank2 = jnp.sum(jnp.where(erow == e2, before, 0.0), axis=0, keepdims=True)
    fields = jnp.concatenate([e1, e2, w1, w2, rank1, rank2], axis=0)
    if n % LANES:
        fields = jnp.concatenate([fields, jnp.zeros((fields.shape[0], LANES - n % LANES), F32)], axis=1)
    pad = jnp.zeros((LANES - fields.shape[0], LANES), F32)
    for j in range(fields.shape[1] // LANES):
        blk_rows = jnp.concatenate([fields[:, j * LANES:(j + 1) * LANES], pad], axis=0)
        n_valid = min(LANES, n - j * LANES)
        route_ref[j * LANES:j * LANES + n_valid, :] = blk_rows.T[:n_valid, :]
    return carry


def _outproj_router(x2d, attn, ssm, w, tm):
    t = x2d.shape[0]
    tok = lambda i: (i, 0)
    const = lambda i: (0, 0)
    full = lambda a: pl.BlockSpec(a.shape, const)
    idx = jnp.arange(min(tm, COUNT_BLOCK))
    tri = (idx[:, None] <= idx[None, :]).astype(BF16)
    return pl.pallas_call(
        _outproj_router_kernel,
        out_shape=(jax.ShapeDtypeStruct((t, D_MODEL), F32), jax.ShapeDtypeStruct((t, D_MODEL // 2), jnp.uint32),
                   jax.ShapeDtypeStruct((t, LANES), F32), jax.ShapeDtypeStruct((8, LANES), F32)),
        grid=(t // tm,),
        in_specs=[
            pl.BlockSpec((tm, D_MODEL), tok), pl.BlockSpec((tm, ATTN_WIDTH), tok),
            pl.BlockSpec((tm, SSM_WIDTH), tok),
            full(w["w_out"]), full(w["norm2"]), full(w["wrt"]),
            full(w["br_col"]), full(tri),
        ],
        out_specs=(pl.BlockSpec((tm, D_MODEL), tok), pl.BlockSpec((tm, D_MODEL // 2), tok),
                   pl.BlockSpec((tm, LANES), tok), pl.BlockSpec((8, LANES), const)),
        scratch_shapes=[pltpu.VMEM((N_EXPERTS, LANES), F32)],
        compiler_params=pltpu.CompilerParams(dimension_semantics=("arbitrary",),
                                             vmem_limit_bytes=VMEM_LIMIT),
        name="outproj_router",
    )(x2d, attn, ssm, w["w_out"], w["norm2"], w["wrt"], w["br_col"], tri)


def _expert_hidden(t_a, t_b, w1):
    half = D_MODEL // 2
    gu = _dot(t_a, w1[:half]) + _dot(t_b, w1[half:])
    return _silu(gu[:, :EXPERT_FF]) * gu[:, EXPERT_FF:]


def _gate_up_bf16(wg_ref, wu_ref):
    return jnp.concatenate([wg_ref[0].astype(BF16), wu_ref[0].astype(BF16)], axis=1)


def _moe_dense_kernel(h_ref, t_ref, route_ref, wg_ref, wu_ref, wd_ref, y_ref):
    e = pl.program_id(0)

    @pl.when(e == 0)
    def _():
        y_ref[...] = h_ref[...]

    t_a, t_b = _unpack_bf16_pair(t_ref[...])
    route = route_ref[...]
    e1, e2, g1, g2 = route[:, 0:1], route[:, 1:2], route[:, 2:3], route[:, 3:4]
    e_f = e.astype(F32)
    hid = _expert_hidden(t_a.astype(BF16), t_b.astype(BF16), _gate_up_bf16(wg_ref, wu_ref))
    c_e = jnp.where(e1 == e_f, g1, 0.0) + jnp.where(e2 == e_f, g2, 0.0)
    y_ref[...] += _dot((hid * c_e).astype(BF16), wd_ref[0].astype(BF16))


def _moe_dense(h, t, route, w):
    n = h.shape[0]
    whole = lambda e: (0, 0)
    by_expert = lambda e: (e, 0, 0)
    return pl.pallas_call(
        _moe_dense_kernel,
        out_shape=jax.ShapeDtypeStruct((n, D_MODEL), F32),
        grid=(N_EXPERTS,),
        in_specs=[pl.BlockSpec((n, D_MODEL), whole), pl.BlockSpec((n, D_MODEL // 2), whole),
                  pl.BlockSpec((n, LANES), whole),
                  pl.BlockSpec((1, D_MODEL, EXPERT_FF), by_expert), pl.BlockSpec((1, D_MODEL, EXPERT_FF), by_expert),
                  pl.BlockSpec((1, EXPERT_FF, D_MODEL), by_expert)],
        out_specs=pl.BlockSpec((n, D_MODEL), whole),
        compiler_params=pltpu.CompilerParams(dimension_semantics=("arbitrary",),
                                             vmem_limit_bytes=VMEM_LIMIT),
        name="moe_dense",
    )(h, t, route, w["w_gate"], w["w_up"], w["w_down"])


def _sc_scatter_rows(src, pos1, pos2, n_out):
    t, width = src.shape
    rows_per_worker = t // SC_WORKERS
    n_chunks = rows_per_worker // SC_CHUNK
    assert t == SC_WORKERS * SC_CHUNK * n_chunks
    mesh = plsc.VectorSubcoreMesh(core_axis_name="c", subcore_axis_name="s")

    @functools.partial(
        pl.kernel, mesh=mesh,
        out_type=jax.ShapeDtypeStruct((n_out, width), src.dtype),
        scratch_types=[pltpu.VMEM((SC_CHUNK,), jnp.int32), pltpu.VMEM((SC_CHUNK,), jnp.int32),
                       pltpu.VMEM((SC_CHUNK, width), src.dtype), pltpu.SemaphoreType.DMA],
    )
    def scatter_kernel(src_hbm, p1_hbm, p2_hbm, out_hbm, i1_v, i2_v, rows_v, sem):
        wid = lax.axis_index("s") * SC_CORES + lax.axis_index("c")
        base = wid * rows_per_worker

        @pl.loop(0, n_chunks)
        def _(c):
            off = pl.multiple_of(base + c * SC_CHUNK, 8)
            pltpu.sync_copy(p1_hbm.at[pl.ds(off, SC_CHUNK)], i1_v)
            pltpu.sync_copy(p2_hbm.at[pl.ds(off, SC_CHUNK)], i2_v)
            pltpu.sync_copy(src_hbm.at[pl.ds(off, SC_CHUNK)], rows_v)
            pltpu.async_copy(rows_v, out_hbm.at[i1_v], sem).wait()
            pltpu.async_copy(rows_v, out_hbm.at[i2_v], sem).wait()

    return scatter_kernel(src, pos1, pos2)


def _sc_gather_rows(table, idx):
    n, width = idx.shape[0], table.shape[1]
    rows_per_worker = n // SC_WORKERS
    n_chunks = rows_per_worker // SC_CHUNK
    assert n == SC_WORKERS * SC_CHUNK * n_chunks and n_chunks % 2 == 0
    mesh = plsc.VectorSubcoreMesh(core_axis_name="c", subcore_axis_name="s")

    @functools.partial(
        pl.kernel, mesh=mesh,
        out_type=jax.ShapeDtypeStruct((n, width), table.dtype),
        scratch_types=[pltpu.VMEM((2, SC_CHUNK), jnp.int32), pltpu.VMEM((2, SC_CHUNK, width), table.dtype),
                       pltpu.SemaphoreType.DMA, pltpu.SemaphoreType.DMA,
                       pltpu.SemaphoreType.DMA, pltpu.SemaphoreType.DMA],
    )
    def gather_kernel(table_hbm, idx_hbm, out_hbm, idx_v, rows_v, g0, g1, w0, w1):
        wid = lax.axis_index("s") * SC_CORES + lax.axis_index("c")
        base = wid * rows_per_worker
        gsem = (g0, g1)
        wsem = (w0, w1)

        def gather_copy(slot):
            return pltpu.make_async_copy(table_hbm.at[idx_v.at[slot]], rows_v.at[slot], gsem[slot])

        def write_copy(c, slot):
            off = pl.multiple_of(base + c * SC_CHUNK, 8)
            return pltpu.make_async_copy(rows_v.at[slot], out_hbm.at[pl.ds(off, SC_CHUNK)], wsem[slot])

        def start_gather(c, slot):
            off = pl.multiple_of(base + c * SC_CHUNK, 8)
            pltpu.sync_copy(idx_hbm.at[pl.ds(off, SC_CHUNK)], idx_v.at[slot])
            gather_copy(slot).start()

        start_gather(0, 0)

        @pl.loop(0, n_chunks, step=2)
        def _(c):
            @pl.when(c > 0)
            def _():
                write_copy(c - 1, 1).wait()

            start_gather(c + 1, 1)
            gather_copy(0).wait()
            write_copy(c, 0).start()
            gather_copy(1).wait()
            write_copy(c + 1, 1).start()
            write_copy(c, 0).wait()

            @pl.when(c + 2 < n_chunks)
            def _():
                start_gather(c + 2, 0)

        write_copy(n_chunks - 1, 1).wait()

    return gather_kernel(table, idx)


def _moe_grouped_kernel(te_ref, nt_ref, order_ref, x_ref, wg_ref, wu_ref, wd_ref, o_ref, w1_bf_ref, w2_bf_ref):
    del order_ref
    i = pl.program_id(0)

    @pl.when(jnp.logical_or(i == 0, te_ref[i] != te_ref[jnp.maximum(i - 1, 0)]))
    def _():
        w1_bf_ref[...] = _gate_up_bf16(wg_ref, wu_ref)
        w2_bf_ref[...] = wd_ref[0].astype(BF16)

    @pl.when(i < nt_ref[0])
    def _():
        for s in range(MOE_TILE // MOE_SUBTILE):
            rows = slice(s * MOE_SUBTILE, (s + 1) * MOE_SUBTILE)
            t_a, t_b = _unpack_bf16_pair(x_ref[rows, :])
            hid = _expert_hidden(t_a.astype(BF16), t_b.astype(BF16), w1_bf_ref[...])
            o_ref[rows, :] = _pack_bf16_pair(_dot(hid.astype(BF16), w2_bf_ref[...]))


def _moe_grouped(xs, tile_expert, n_tiles, order, w):
    rows = xs.shape[0]
    row = lambda i, te, nt, od: (i, 0)
    by_expert = lambda i, te, nt, od: (te[i], 0, 0)
    return pl.pallas_call(
        _moe_grouped_kernel,
        out_shape=jax.ShapeDtypeStruct((rows, D_MODEL // 2), jnp.uint32),
        grid_spec=pltpu.PrefetchScalarGridSpec(
            num_scalar_prefetch=3,
            grid=(rows // MOE_TILE,),
            in_specs=[pl.BlockSpec((MOE_TILE, D_MODEL // 2), row),
                      pl.BlockSpec((1, D_MODEL, EXPERT_FF), by_expert),
                      pl.BlockSpec((1, D_MODEL, EXPERT_FF), by_expert),
                      pl.BlockSpec((1, EXPERT_FF, D_MODEL), by_expert)],
            out_specs=pl.BlockSpec((MOE_TILE, D_MODEL // 2), row),
            scratch_shapes=[pltpu.VMEM((D_MODEL, 2 * EXPERT_FF), BF16), pltpu.VMEM((EXPERT_FF, D_MODEL), BF16)],
        ),
        compiler_params=pltpu.CompilerParams(dimension_semantics=("arbitrary",),
                                             vmem_limit_bytes=VMEM_LIMIT),
        name="moe_grouped",
    )(tile_expert, n_tiles, order, xs, w["w_gate"], w["w_up"], w["w_down"])


def _moe_combine_kernel(h_ref, z1_ref, z2_ref, route_ref, y_ref):
    route = route_ref[...]
    g1, g2 = route[:, 2:3], route[:, 3:4]
    half = D_MODEL // 2
    a1, b1 = _unpack_bf16_pair(z1_ref[...])
    a2, b2 = _unpack_bf16_pair(z2_ref[...])
    y_ref[:, :half] = h_ref[:, :half] + g1 * a1 + g2 * a2
    y_ref[:, half:] = h_ref[:, half:] + g1 * b1 + g2 * b2


def _moe_combine(h, z, route, tm):
    t = h.shape[0]
    nb = t // tm
    tok = lambda i: (i, 0)
    return pl.pallas_call(
        _moe_combine_kernel,
        out_shape=jax.ShapeDtypeStruct((t, D_MODEL), F32),
        grid=(nb,),
        in_specs=[pl.BlockSpec((tm, D_MODEL), tok), pl.BlockSpec((tm, D_MODEL // 2), tok),
                  pl.BlockSpec((tm, D_MODEL // 2), lambda i: (i + nb, 0)), pl.BlockSpec((tm, LANES), tok)],
        out_specs=pl.BlockSpec((tm, D_MODEL), tok),
        compiler_params=pltpu.CompilerParams(dimension_semantics=("parallel",),
                                             vmem_limit_bytes=VMEM_LIMIT),
        name="moe_combine",
    )(h, z, z, route)


def _route_pos_kernel(route_ref, cnt_ref, upper_ref, pos_ref):
    tm = route_ref.shape[0]
    cnt = cnt_ref[...]
    padded = jnp.floor((cnt + float(MOE_TILE - 1)) * (1.0 / MOE_TILE)) * float(MOE_TILE)
    p_hi, p_mid, p_lo = _split3(padded)
    upper = upper_ref[...]
    starts = (_dot(p_hi, upper) + _dot(p_mid, upper) + _dot(p_lo, upper))[0:1, :]
    route = route_ref[...]
    lane_f = lax.broadcasted_iota(jnp.int32, (tm, LANES), 1).astype(F32)
    diag = (lax.broadcasted_iota(jnp.int32, (LANES, LANES), 0)
            == lax.broadcasted_iota(jnp.int32, (LANES, LANES), 1))
    for k in range(2):
        e_lane = route[:, k:k + 1] + float(EXPERT_LANE0)
        pos = jnp.sum(jnp.where(lane_f == e_lane, starts, 0.0), axis=-1, keepdims=True) + route[:, 4 + k:5 + k]
        for r in range(tm // LANES):
            col = pos[r * LANES:(r + 1) * LANES, :]
            row = jnp.sum(jnp.where(diag, col, 0.0), axis=0, keepdims=True)
            pos_ref[k, r:r + 1, :] = row.astype(jnp.int32)


def _route_positions(route, counts, tm):
    t = route.shape[0]
    idx = jnp.arange(LANES)
    upper = (idx[:, None] < idx[None, :]).astype(BF16)
    return pl.pallas_call(
        _route_pos_kernel,
        out_shape=jax.ShapeDtypeStruct((2, t // LANES, LANES), jnp.int32),
        grid=(t // tm,),
        in_specs=[pl.BlockSpec((tm, LANES), lambda i: (i, 0)), pl.BlockSpec((8, LANES), lambda i: (0, 0)),
                  pl.BlockSpec((LANES, LANES), lambda i: (0, 0))],
        out_specs=pl.BlockSpec((2, tm // LANES, LANES), lambda i: (0, i, 0)),
        compiler_params=pltpu.CompilerParams(dimension_semantics=("parallel",),
                                             vmem_limit_bytes=VMEM_LIMIT),
        name="route_positions",
    )(route, counts, upper)


def _moe_routed(h, t_packed, route, counts, w, tm, run_first):
    t = h.shape[0]
    pos = _route_positions(route, counts, min(32 * LANES, t))
    pos1 = pos[0].reshape(t)
    pos2 = pos[1].reshape(t)
    cnt = counts[0, EXPERT_LANE0:EXPERT_LANE0 + N_EXPERTS].astype(jnp.int32)
    padded = (cnt + MOE_TILE - 1) // MOE_TILE * MOE_TILE
    ends = jnp.cumsum(padded)
    n_rows = 2 * t + N_EXPERTS * MOE_TILE
    n_tiles = ends[N_EXPERTS - 1] // MOE_TILE
    tile_start = jnp.arange(n_rows // MOE_TILE, dtype=jnp.int32) * MOE_TILE
    tile_start = jnp.minimum(tile_start, ends[N_EXPERTS - 1] - MOE_TILE)
    tile_expert = jnp.sum((tile_start[:, None] >= ends[None, :]).astype(jnp.int32), axis=1)
    xs = _sc_scatter_rows(t_packed, pos1, pos2, n_rows)
    order = lax.bitcast_convert_type(run_first.reshape(-1)[:1].astype(F32), jnp.int32)
    out = _moe_grouped(xs, tile_expert, n_tiles.reshape(1), order, w)
    z = _sc_gather_rows(out, pos.reshape(2 * t))
    return _moe_combine(h, z, route, tm)


def _pad_lanes(a, width=LANES):
    return jnp.pad(a, ((0, 0), (0, width - a.shape[1])))


def _prep_weights(norm1, w_in, q_norm, k_norm, conv_w, conv_b, dt_bias, a_log, d_skip, ssm_norm, w_out,
                  norm2, w_grp, b_grp, w_exp, b_exp, w_gate, w_up, w_down):
    w = {}
    w["norm1"] = norm1.reshape(1, D_MODEL)
    w["w_in"] = _pad_lanes(w_in, XBC_END + LANES).astype(BF16)
    w["qkn"] = jnp.concatenate([jnp.tile(q_norm, N_HEADS), jnp.tile(k_norm, KV_HEADS)]).reshape(1, QK_WIDTH)
    head_of_col = jnp.arange(QK_WIDTH) // HEAD_DIM
    red = (head_of_col[:, None] == jnp.arange(LANES)[None, :])
    w["red"] = red.astype(BF16)
    w["exp"] = red.T.astype(BF16)
    w["conv_w"] = conv_w
    w["conv_b"] = conv_b.reshape(1, CONV_DIM)
    w["dt_bias"] = _pad_lanes(dt_bias.reshape(1, SSM_HEADS))
    w["a_log"] = _pad_lanes(a_log.reshape(1, SSM_HEADS))
    w["d_skip"] = jnp.repeat(d_skip, SSM_HEAD_DIM).reshape(1, SSM_WIDTH)
    w["ssm_norm"] = ssm_norm.reshape(1, SSM_WIDTH)
    idx = jnp.arange(SSD_CHUNK)
    w["tri"] = (idx[None, :] <= idx[:, None]).astype(BF16)
    lane_head = jnp.arange(SSM_WIDTH) // SSM_HEAD_DIM
    w["expand"] = (jnp.arange(LANES)[:, None] == lane_head[None, :]).astype(BF16)
    w["w_out"] = w_out.astype(BF16)
    w["norm2"] = norm2.reshape(1, D_MODEL)
    wr = jnp.zeros((D_MODEL, LANES), F32)
    wr = wr.at[:, :N_EGROUPS].set(w_grp).at[:, EXPERT_LANE0:EXPERT_LANE0 + N_EXPERTS].set(w_exp)
    wr_hi = wr.astype(BF16)
    w["wrt"] = jnp.concatenate([wr_hi, (wr - wr_hi.astype(F32)).astype(BF16)], axis=1).T
    br = jnp.zeros((LANES, 1), F32)
    w["br_col"] = br.at[:N_EGROUPS, 0].set(b_grp).at[EXPERT_LANE0:EXPERT_LANE0 + N_EXPERTS, 0].set(b_exp)
    w["w_gate"], w["w_up"], w["w_down"] = w_gate, w_up, w_down
    return w


def _rope_tables(pos):
    inv = 1.0 / (ROPE_THETA ** (jnp.arange(0, HEAD_DIM, 2, dtype=F32) / HEAD_DIM))
    ang = pos.astype(F32)[:, None] * inv[None, :]
    cos, sin = jnp.cos(ang), jnp.sin(ang)
    reps = LANES // HEAD_DIM
    return (jnp.tile(jnp.concatenate([cos, cos], axis=-1), (1, reps)),
            jnp.tile(jnp.concatenate([-sin, sin], axis=-1), (1, reps)))


def _token_tile(t):
    for tm in (1024, 512, 256, 128, 64, 32, 16):
        if t % tm == 0:
            return tm
    raise ValueError(f"token count {t} is not a multiple of 16")


def kernel(x_prompt, x_sample, cache_win_k, cache_win_v, state_conv, state_ssm, norm1, w_in, q_norm, k_norm,
           sinks, conv_w, conv_b, dt_bias, a_log, d_skip, ssm_norm, w_out, norm2, w_grp, b_grp, w_exp, b_exp,
           w_gate, w_up, w_down):
    depth = norm1.shape[0]
    assert depth == 1, "single-layer stack"
    bp, lp, _ = x_prompt.shape
    bsn, ls, _ = x_sample.shape
    assert ls == 1 and lp % WINDOW == 0 and cache_win_k.shape[2] == WINDOW
    l = 0
    w = _prep_weights(norm1[l], w_in[l], q_norm[l], k_norm[l], conv_w[l], conv_b[l], dt_bias[l], a_log[l],
                      d_skip[l], ssm_norm[l], w_out[l], norm2[l], w_grp[l], b_grp[l], w_exp[l], b_exp[l],
                      w_gate[l], w_up[l], w_down[l])
    sink = sinks[l]

    tp = bp * lp
    xp = x_prompt.reshape(tp, D_MODEL)
    tm_p = _token_tile(lp)
    cos_p, sin_p = _rope_tables(jnp.arange(lp, dtype=jnp.int32))
    q, k, v, z, xbc, dt, k2, v2 = _inproj(xp, w, cos_p, sin_p, tm_p, lp // tm_p)
    attn = _attn_prompt(q, k2, v2, sink, bp, lp)
    ssm, st_p = _ssd_prompt(xbc, z, dt, w, bp, lp)
    h, t, route, counts = _outproj_router(xp, attn, ssm, w, tm_p)
    k3 = k.reshape(bp, lp, KV_HEADS, HEAD_DIM)
    v3 = v.reshape(bp, lp, KV_HEADS, HEAD_DIM)
    win_k_p = k3[:, lp - WINDOW:][None]
    win_v_p = v3[:, lp - WINDOW:][None]
    conv_p = xbc.reshape(bp, lp, CONV_DIM)[:, lp - (CONV_K - 1):][None]
    ssm_p = st_p.reshape(1, bp, SSM_HEADS, SSM_HEAD_DIM, SSM_STATE)

    xs2 = x_sample.reshape(bsn, D_MODEL)
    tm_s = _token_tile(bsn)
    cos_s, sin_s = _rope_tables(jnp.full((tm_s,), PAST_LEN, jnp.int32))
    q_s, k_s, v_s, z_s, xbc_s, dt_s, _, _ = _inproj(xs2, w, cos_s, sin_s, tm_s, 1)
    q4 = q_s.reshape(bsn, KV_HEADS, N_HEADS // KV_HEADS, HEAD_DIM)
    zq = jnp.zeros_like(q4[:, 0])
    qx = jnp.concatenate([jnp.concatenate([q4[:, 0], zq], axis=-1),
                          jnp.concatenate([zq, q4[:, 1]], axis=-1)], axis=1)
    qx = jnp.pad(qx, ((0, 0), (0, BF16_ROWS - N_HEADS), (0, 0)))
    sink_x = jnp.pad(jnp.broadcast_to(sink[:, None], (N_HEADS, LANES)), ((0, BF16_ROWS - N_HEADS), (0, 0)))
    kc = cache_win_k[l].reshape(bsn, WINDOW, KV_WIDTH)
    vc = cache_win_v[l].reshape(bsn, WINDOW, KV_WIDTH)
    ko, vo, attn_s = _attn_sample(qx, kc, k_s.reshape(bsn, 1, KV_WIDTH), vc, v_s.reshape(bsn, 1, KV_WIDTH),
                               sink_x, 8)
    y_prompt = _moe_routed(h, t, route, counts, w, tm_p, attn_s).reshape(bp, lp, D_MODEL)
    cprev_t = jnp.transpose(state_conv[l], (1, 0, 2))
    h0 = state_ssm[l].reshape(bsn, HEAD_PAIRS, LANES, SSM_STATE)
    ssm_s, cnew_t, h1 = _ssd_sample(xbc_s, z_s, dt_s, cprev_t, h0, w, 16)
    h_s, t_s, route_s, _ = _outproj_router(xs2, attn_s, ssm_s, w, tm_s)
    y_sample = _moe_dense(h_s, t_s, route_s, w).reshape(bsn, 1, D_MODEL)
    win_k_s = ko.reshape(1, bsn, WINDOW, KV_HEADS, HEAD_DIM)
    win_v_s = vo.reshape(1, bsn, WINDOW, KV_HEADS, HEAD_DIM)
    conv_s = jnp.transpose(cnew_t, (1, 0, 2))[None]
    ssm_s_state = h1.reshape(1, bsn, SSM_HEADS, SSM_HEAD_DIM, SSM_STATE)

    return (y_prompt, y_sample, win_k_p, win_v_p, conv_p, ssm_p, win_k_s, win_v_s, conv_s, ssm_s_state)
```

```python
import functools
import math

import jax
import jax.numpy as jnp
from jax import lax
from jax.experimental import pallas as pl
from jax.experimental.pallas import tpu as pltpu
from jax.experimental.pallas import tpu_sc as plsc

F32 = jnp.float32
BF16 = jnp.bfloat16

D_MODEL = 1024
HEAD_DIM = 64
N_HEADS = 8
KV_HEADS = 2
WINDOW = 128
ATTN_WIDTH = N_HEADS * HEAD_DIM
QK_WIDTH = ATTN_WIDTH + KV_HEADS * HEAD_DIM
KV_WIDTH = KV_HEADS * HEAD_DIM
ATTN_SCALE = HEAD_DIM ** -0.5
ROPE_THETA = 10000.0
SSM_WIDTH = 512
SSM_HEADS = 8
SSM_HEAD_DIM = 64
SSM_GROUPS = 2
SSM_STATE = 128
CONV_K = 4
CONV_HALO = 8
CONV_DIM = SSM_WIDTH + 2 * SSM_GROUPS * SSM_STATE
SSD_CHUNK = 128
N_EGROUPS = 4
EXP_PER_GROUP = 8
N_EXPERTS = 32
EXPERT_FF = 128
EPS = 1e-6
PAST_LEN = 16384

LANES = 128
BF16_ROWS = 16
HEAD_PAIRS = SSM_HEADS // 2
EXPERT_LANE0 = 32
VMEM_LIMIT = 56 * 1024 * 1024
MOE_TILE = 512
MOE_SUBTILE = 128
ATTN_QBLOCKS = 8
SSD_CHUNKS_PER_STEP = 8
COUNT_BLOCK = 256
ROUTER_SUBTILE = 1024
INPROJ_SUBTILE = 512
SC_CORES = 2
SC_SUBCORES = 16
SC_WORKERS = SC_CORES * SC_SUBCORES
SC_CHUNK = 64

Q_END = ATTN_WIDTH
K_END = Q_END + KV_WIDTH
V_END = K_END + KV_WIDTH
Z_END = V_END + SSM_WIDTH
XBC_END = Z_END + CONV_DIM


def _dot(a, b):
    return jnp.dot(a, b, preferred_element_type=F32)


def _dot_nt(a, b):
    return lax.dot_general(a, b, (((1,), (1,)), ((), ())), preferred_element_type=F32)


def _split2(v):
    hi = v.astype(BF16)
    lo = (v - hi.astype(F32)).astype(BF16)
    return hi, lo


def _split3(v):
    hi = v.astype(BF16)
    r = v - hi.astype(F32)
    mid = r.astype(BF16)
    lo = (r - mid.astype(F32)).astype(BF16)
    return hi, mid, lo


def _silu(x):
    return x * jax.nn.sigmoid(x)


def _softplus(x):
    return jnp.maximum(x, 0.0) + jnp.log1p(jnp.exp(-jnp.abs(x)))


def _lane_bcast_pairs(v, n_pairs):
    r = v.shape[0]
    lo = lax.broadcasted_iota(jnp.int32, (r, LANES), 1) < HEAD_DIM
    slabs = []
    for j in range(n_pairs):
        a = jnp.broadcast_to(v[:, 2 * j:2 * j + 1], (r, LANES))
        b = jnp.broadcast_to(v[:, 2 * j + 1:2 * j + 2], (r, LANES))
        slabs.append(jnp.where(lo, a, b))
    return jnp.concatenate(slabs, axis=1)


def _causal_conv_silu(x_ext, convw_ref, convb_ref):
    halo = CONV_HALO
    x_raw = x_ext[halo:, :]
    conv = convb_ref[...] + x_raw * convw_ref[CONV_K - 1:CONV_K, :]
    for j in range(CONV_K - 1):
        shifted = pltpu.roll(x_ext, CONV_K - 1 - j, axis=0)[halo:, :]
        conv = conv + shifted * convw_ref[j:j + 1, :]
    return _silu(conv)


def _inproj_kernel(x_ref, n1_ref, win_ref, dtb_ref, qkn_ref,
                   cos_ref, sin_ref, red_ref, exp_ref,
                   q_ref, k_ref, v_ref, z_ref, xbc_ref, dt_ref, k2_ref, v2_ref):
    tm = x_ref.shape[0]
    sub = min(tm, INPROJ_SUBTILE)
    lane = lax.broadcasted_iota(jnp.int32, (sub, LANES), 1)
    first_half = (lane % HEAD_DIM) < (HEAD_DIM // 2)
    for s in range(tm // sub):
        rows = slice(s * sub, (s + 1) * sub)
        x = x_ref[rows, :]
        ms = jnp.mean(x * x, axis=-1, keepdims=True)
        xn = (x * lax.rsqrt(ms + EPS) * n1_ref[...]).astype(BF16)
        v = _dot(xn, win_ref[:, K_END:V_END])
        v_ref[rows, :] = v
        v2_ref[rows, :] = _pair_operands(v)
        z_ref[rows, :] = _dot(xn, win_ref[:, V_END:Z_END])
        xbc_ref[rows, :] = _dot(xn, win_ref[:, Z_END:XBC_END])
        dt_ref[rows, :] = _softplus(_dot(xn, win_ref[:, XBC_END:]) + dtb_ref[...])
        qk = _dot(xn, win_ref[:, :K_END])
        ss = _dot((qk * qk).astype(BF16), red_ref[...])
        inv = lax.rsqrt(ss * (1.0 / HEAD_DIM) + EPS)
        inv_hi, inv_lo = _split2(inv)
        inv_x = _dot(inv_hi, exp_ref[...]) + _dot(inv_lo, exp_ref[...])
        qkn = qk * inv_x * qkn_ref[...]
        cos = cos_ref[rows, :]
        sin = sin_ref[rows, :]
        for c in range(QK_WIDTH // LANES):
            xc = qkn[:, c * LANES:(c + 1) * LANES]
            partner = jnp.where(first_half,
                                pltpu.roll(xc, LANES - HEAD_DIM // 2, axis=1),
                                pltpu.roll(xc, HEAD_DIM // 2, axis=1))
            rot = xc * cos + partner * sin
            if c < ATTN_WIDTH // LANES:
                q_ref[rows, c * LANES:(c + 1) * LANES] = (rot * ATTN_SCALE).astype(BF16)
            else:
                k_ref[rows, :] = rot
                k2_ref[rows, :] = _pair_operands(rot)


def _inproj(x2d, w, cos_tab, sin_tab, tm, n_pos_blocks):
    t = x2d.shape[0]
    grid = (t // tm,)
    tok = lambda i: (i, 0)
    const = lambda i: (0, 0)
    pos = lambda i: (i % n_pos_blocks, 0)
    full = lambda a: pl.BlockSpec(a.shape, const)
    rows = lambda width, dtype: (jax.ShapeDtypeStruct((t, width), dtype), pl.BlockSpec((tm, width), tok))
    outs = [rows(ATTN_WIDTH, BF16), rows(KV_WIDTH, F32), rows(KV_WIDTH, F32), rows(SSM_WIDTH, F32)]
    operands = [x2d, w["norm1"], w["w_in"], w["dt_bias"], w["qkn"],
                cos_tab, sin_tab, w["red"], w["exp"]]
    in_specs = [
        pl.BlockSpec((tm, D_MODEL), tok),
        full(w["norm1"]), full(w["w_in"]), full(w["dt_bias"]), full(w["qkn"]),
        pl.BlockSpec((tm, LANES), pos), pl.BlockSpec((tm, LANES), pos),
        full(w["red"]), full(w["exp"]),
    ]
    outs += [rows(CONV_DIM, F32), rows(LANES, F32), rows(4 * LANES, BF16), rows(4 * LANES, BF16)]
    return pl.pallas_call(
        _inproj_kernel,
        out_shape=tuple(o[0] for o in outs),
        grid=grid,
        in_specs=in_specs,
        out_specs=tuple(o[1] for o in outs),
        compiler_params=pltpu.CompilerParams(dimension_semantics=("parallel",),
                                             vmem_limit_bytes=VMEM_LIMIT),
        name="inproj",
    )(*operands)


def _pair_operands(kv):
    lo = lax.broadcasted_iota(jnp.int32, kv.shape, 1) < HEAD_DIM
    swapped = pltpu.roll(kv, HEAD_DIM, axis=1)
    parts = [jnp.where(lo, kv, 0.0), jnp.where(lo, 0.0, swapped), jnp.where(lo, swapped, 0.0), jnp.where(lo, 0.0, kv)]
    return jnp.concatenate(parts, axis=1).astype(BF16)


def _pair_rhs(blk, g):
    return jnp.concatenate([blk[:, 2 * g * LANES:(2 * g + 1) * LANES],
                            blk[:, (2 * g + 1) * LANES:(2 * g + 2) * LANES]], axis=0)


def _attn_qblock(sink_ref, q_blk, k_prev, k_cur, v_prev, v_cur, seq_start, o_ref):
    blk = WINDOW
    qi = lax.broadcasted_iota(jnp.int32, (blk, 2 * blk), 0)
    kj = lax.broadcasted_iota(jnp.int32, (blk, 2 * blk), 1) % blk
    cur_ok = kj <= qi
    lo = lax.broadcasted_iota(jnp.int32, (blk, LANES), 1) < HEAD_DIM
    n_pairs = N_HEADS // KV_HEADS // 2
    for g in range(KV_HEADS):
        q_all = jnp.concatenate([q_blk[:, (g * n_pairs + r) * LANES:(g * n_pairs + r + 1) * LANES]
                                 for r in range(n_pairs)], axis=0)
        s_all = _dot_nt(q_all, jnp.concatenate([_pair_rhs(k_cur, g), _pair_rhs(k_prev, g)], axis=0))
        p_rows = []
        den_rows = []
        for r in range(n_pairs):
            pair = g * n_pairs + r
            s_cur = s_all[r * blk:(r + 1) * blk, :2 * blk]
            s_prev = s_all[r * blk:(r + 1) * blk, 2 * blk:]
            s = jnp.where(cur_ok, s_cur, s_prev)
            if seq_start is not None:
                s = jnp.where(jnp.logical_or(cur_ok, jnp.logical_not(seq_start)), s, -jnp.inf)
            ps = []
            dens = []
            for hh in range(2):
                sink = sink_ref[2 * pair + hh]
                sh = s[:, hh * blk:(hh + 1) * blk]
                m = jnp.maximum(jnp.max(sh, axis=-1, keepdims=True), sink)
                p = jnp.exp(sh - m)
                dens.append(jnp.sum(p, axis=-1, keepdims=True) + jnp.exp(sink - m))
                ps.append(p)
            p2 = jnp.concatenate(ps, axis=1)
            p_rows.append(jnp.concatenate([jnp.where(cur_ok, p2, 0.0), jnp.where(cur_ok, 0.0, p2)],
                                          axis=1).astype(BF16))
            den_rows.append(jnp.where(lo, dens[0], dens[1]))
        o_all = _dot(jnp.concatenate(p_rows, axis=0),
                     jnp.concatenate([_pair_rhs(v_cur, g), _pair_rhs(v_prev, g)], axis=0))
        for r in range(n_pairs):
            pair = g * n_pairs + r
            o2 = o_all[r * blk:(r + 1) * blk, :]
            o_ref[:, pair * LANES:(pair + 1) * LANES] = (o2 / den_rows[r]).astype(BF16)


def _attn_kernel(sink_ref, q_ref, kc_ref, kp_ref, vc_ref, vp_ref, o_ref):
    blk = WINDOW
    first_step = pl.program_id(1) == 0
    for u in range(q_ref.shape[0] // blk):
        rows = slice(u * blk, (u + 1) * blk)
        prev_rows = slice((u - 1) * blk, u * blk)
        k_prev = kp_ref[...] if u == 0 else kc_ref[prev_rows, :]
        v_prev = vp_ref[...] if u == 0 else vc_ref[prev_rows, :]
        _attn_qblock(sink_ref, q_ref[rows, :], k_prev, kc_ref[rows, :], v_prev, vc_ref[rows, :],
                     first_step if u == 0 else None, o_ref.at[rows, :])


def _attn_prompt(q, k, v, sinks, batch, seq):
    n_sub = ATTN_QBLOCKS if seq % (ATTN_QBLOCKS * WINDOW) == 0 else 1
    rows = n_sub * WINDOW
    nb = seq // rows
    cur = lambda b, j, s: (b * nb + j, 0)
    prev = lambda b, j, s: (jnp.maximum((b * nb + j) * n_sub - 1, 0), 0)
    return pl.pallas_call(
        _attn_kernel,
        out_shape=jax.ShapeDtypeStruct((batch * seq, ATTN_WIDTH), BF16),
        grid_spec=pltpu.PrefetchScalarGridSpec(
            num_scalar_prefetch=1,
            grid=(batch, nb),
            in_specs=[
                pl.BlockSpec((rows, ATTN_WIDTH), cur),
                pl.BlockSpec((rows, 4 * LANES), cur), pl.BlockSpec((WINDOW, 4 * LANES), prev),
                pl.BlockSpec((rows, 4 * LANES), cur), pl.BlockSpec((WINDOW, 4 * LANES), prev),
            ],
            out_specs=pl.BlockSpec((rows, ATTN_WIDTH), cur),
        ),
        compiler_params=pltpu.CompilerParams(dimension_semantics=("parallel", "parallel"),
                                             vmem_limit_bytes=VMEM_LIMIT),
        name="attn_prompt",
    )(sinks, q, k, k, v, v)


def _ssd_kernel(xbc_ref, z_ref, dt_ref, convw_ref, convb_ref, alog_ref, dskip_ref, nw_ref,
                tri_ref, expand_ref, y_ref, st_ref, buf_ref, state_ref):
    c = pl.program_id(1)
    cl = SSD_CHUNK
    n_sub = xbc_ref.shape[0] // cl
    halo = CONV_HALO

    @pl.when(c == 0)
    def _():
        buf_ref[...] = jnp.zeros(buf_ref.shape, F32)
        state_ref[...] = jnp.zeros(state_ref.shape, F32)

    lane = lax.broadcasted_iota(jnp.int32, (1, LANES), 1)
    a_neg = jnp.where(lane < SSM_HEADS, -jnp.exp(alog_ref[...]), 0.0)
    tri = tri_ref[...]
    for u in range(n_sub):
        rows = slice(u * cl, (u + 1) * cl)
        if u == 0:
            x_ext = jnp.concatenate([buf_ref[...], xbc_ref[rows, :]], axis=0)
        else:
            x_ext = xbc_ref[u * cl - halo:(u + 1) * cl, :]
        _ssd_chunk(x_ext, z_ref[rows, :], dt_ref[rows, :], a_neg, tri, convw_ref, convb_ref, dskip_ref, nw_ref,
                   expand_ref, y_ref.at[rows, :], state_ref)
    buf_ref[...] = xbc_ref[n_sub * cl - halo:n_sub * cl, :]

    @pl.when(c == pl.num_programs(1) - 1)
    def _():
        st_ref[0] = state_ref[...]


def _ssd_chunk(x_ext, z, dt, a_neg, tri, convw_ref, convb_ref, dskip_ref, nw_ref, expand_ref, y_ref, state_ref):
    cl = SSD_CHUNK
    act = _causal_conv_silu(x_ext, convw_ref, convb_ref)
    xs = act[:, :SSM_WIDTH]
    bm = act[:, SSM_WIDTH:SSM_WIDTH + SSM_GROUPS * SSM_STATE].astype(BF16)
    cm = act[:, SSM_WIDTH + SSM_GROUPS * SSM_STATE:].astype(BF16)

    dta = dt * a_neg
    p_hi, p_mid, p_lo = _split3(dta)
    a_col = _dot(tri, p_hi) + _dot(tri, p_mid) + _dot(tri, p_lo)
    a_last = a_col[cl - 1:cl, :]
    a_row = a_col.T
    per_head = jnp.concatenate([dt, jnp.exp(a_col), jnp.exp(a_last - a_col)], axis=0)
    ph_hi, ph_lo = _split2(per_head)
    per_lane = _dot(ph_hi, expand_ref[...]) + _dot(ph_lo, expand_ref[...])
    dt_x = per_lane[:cl]
    ecol_x = per_lane[cl:2 * cl]
    dte_x = per_lane[2 * cl:]
    e_last = jnp.exp(a_last)
    xdt = xs * dt_x

    li = lax.broadcasted_iota(jnp.int32, (cl, cl), 0)
    si = lax.broadcasted_iota(jnp.int32, (cl, cl), 1)
    causal = si <= li
    lo = lax.broadcasted_iota(jnp.int32, (cl, LANES), 1) < SSM_HEAD_DIM
    row_lo = lax.broadcasted_iota(jnp.int32, (LANES, SSM_STATE), 0) < SSM_HEAD_DIM

    ys = []
    for g in range(SSM_GROUPS):
        b_g = bm[:, g * SSM_STATE:(g + 1) * SSM_STATE]
        c_g = cm[:, g * SSM_STATE:(g + 1) * SSM_STATE]
        cb = _dot_nt(c_g, b_g)
        for r in range(HEAD_PAIRS // SSM_GROUPS):
            j = g * (HEAD_PAIRS // SSM_GROUPS) + r
            sl = slice(j * LANES, (j + 1) * LANES)
            xdt_p = xdt[:, sl]
            ms = []
            for hh in range(2):
                h = 2 * j + hh
                seg = a_col[:, h:h + 1] - a_row[h:h + 1, :]
                ms.append(cb * jnp.exp(jnp.where(causal, seg, -jnp.inf)))
            m2 = jnp.concatenate(ms, axis=1).astype(BF16)
            rhs = jnp.concatenate([jnp.where(lo, xdt_p, 0.0), jnp.where(lo, 0.0, xdt_p)],
                                  axis=0).astype(BF16)
            y_diag = _dot(m2, rhs)
            st = state_ref[j]
            y_off = _dot_nt(c_g, st.astype(BF16)) * ecol_x[:, sl]
            xdt_e = (xdt_p * dte_x[:, sl]).T.astype(BF16)
            d_a = e_last[:, 2 * j:2 * j + 1]
            d_b = e_last[:, 2 * j + 1:2 * j + 2]
            decay = jnp.where(row_lo, jnp.broadcast_to(d_a, row_lo.shape), jnp.broadcast_to(d_b, row_lo.shape))
            state_ref[j] = decay * st + _dot(xdt_e, b_g)
            ys.append(y_diag + y_off + dskip_ref[:, sl] * xs[:, sl])
    y = jnp.concatenate(ys, axis=1)
    gated = y * _silu(z)
    gw = SSM_WIDTH // SSM_GROUPS
    outs = []
    for g in range(SSM_GROUPS):
        gg = gated[:, g * gw:(g + 1) * gw]
        outs.append(gg * lax.rsqrt(jnp.mean(gg * gg, axis=-1, keepdims=True) + EPS))
    y_ref[...] = (jnp.concatenate(outs, axis=1) * nw_ref[...]).astype(BF16)


def _ssd_prompt(xbc, z, dt, w, batch, seq):
    n_sub = SSD_CHUNKS_PER_STEP if seq % (SSD_CHUNKS_PER_STEP * SSD_CHUNK) == 0 else 1
    rows = n_sub * SSD_CHUNK
    nc = seq // rows
    tok = lambda b, c: (b * nc + c, 0)
    const = lambda b, c: (0, 0)
    full = lambda a: pl.BlockSpec(a.shape, const)
    return pl.pallas_call(
        _ssd_kernel,
        out_shape=(jax.ShapeDtypeStruct((batch * seq, SSM_WIDTH), BF16),
                   jax.ShapeDtypeStruct((batch, HEAD_PAIRS, LANES, SSM_STATE), F32)),
        grid=(batch, nc),
        in_specs=[
            pl.BlockSpec((rows, CONV_DIM), tok), pl.BlockSpec((rows, SSM_WIDTH), tok),
            pl.BlockSpec((rows, LANES), tok),
            full(w["conv_w"]), full(w["conv_b"]), full(w["a_log"]),
            full(w["d_skip"]), full(w["ssm_norm"]), full(w["tri"]), full(w["expand"]),
        ],
        out_specs=(pl.BlockSpec((rows, SSM_WIDTH), tok),
                   pl.BlockSpec((1, HEAD_PAIRS, LANES, SSM_STATE), lambda b, c: (b, 0, 0, 0))),
        scratch_shapes=[pltpu.VMEM((CONV_HALO, CONV_DIM), F32),
                        pltpu.VMEM((HEAD_PAIRS, LANES, SSM_STATE), F32)],
        compiler_params=pltpu.CompilerParams(dimension_semantics=("parallel", "arbitrary"),
                                             vmem_limit_bytes=VMEM_LIMIT),
        name="ssd_prompt",
    )(xbc, z, dt, w["conv_w"], w["conv_b"], w["a_log"], w["d_skip"], w["ssm_norm"], w["tri"], w["expand"])


def _attn_sample_kernel(qx_ref, kc_ref, kn_ref, vc_ref, vn_ref, sink_ref, ko_ref, vo_ref, o_ref):
    bs = qx_ref.shape[0]
    w = kc_ref.shape[1]
    sink = sink_ref[...]
    lo = lax.broadcasted_iota(jnp.int32, (1, LANES), 1) < HEAD_DIM
    for i in range(bs):
        ko_ref[i, 0:w - 1, :] = kc_ref[i, 1:w, :]
        ko_ref[i, w - 1:w, :] = kn_ref[i]
        vo_ref[i, 0:w - 1, :] = vc_ref[i, 1:w, :]
        vo_ref[i, w - 1:w, :] = vn_ref[i]
        s = _dot_nt(qx_ref[i], ko_ref[i].astype(BF16))
        m = jnp.maximum(jnp.max(s, axis=-1, keepdims=True), sink)
        p = jnp.exp(s - m)
        den = jnp.sum(p, axis=-1, keepdims=True) + jnp.exp(sink - m)
        o = _dot(p.astype(BF16), vo_ref[i].astype(BF16)) / den
        o_sw = pltpu.roll(o, HEAD_DIM, axis=1)
        for j in range(N_HEADS // 2):
            a, b = (o, o_sw) if j < N_HEADS // 4 else (o_sw, o)
            o_ref[i:i + 1, j * LANES:(j + 1) * LANES] = jnp.where(lo, a[2 * j:2 * j + 1], b[2 * j + 1:2 * j + 2])


def _attn_sample(qx, kc, kn, vc, vn, sink_x, bs):
    n, w = kc.shape[0], kc.shape[1]
    blk3 = lambda i: (i, 0, 0)
    return pl.pallas_call(
        _attn_sample_kernel,
        out_shape=(jax.ShapeDtypeStruct((n, w, KV_WIDTH), F32),
                   jax.ShapeDtypeStruct((n, w, KV_WIDTH), F32),
                   jax.ShapeDtypeStruct((n, ATTN_WIDTH), F32)),
        grid=(n // bs,),
        in_specs=[
            pl.BlockSpec((bs, BF16_ROWS, LANES), blk3),
            pl.BlockSpec((bs, w, KV_WIDTH), blk3), pl.BlockSpec((bs, 1, KV_WIDTH), blk3),
            pl.BlockSpec((bs, w, KV_WIDTH), blk3), pl.BlockSpec((bs, 1, KV_WIDTH), blk3),
            pl.BlockSpec(sink_x.shape, lambda i: (0, 0)),
        ],
        out_specs=(pl.BlockSpec((bs, w, KV_WIDTH), blk3), pl.BlockSpec((bs, w, KV_WIDTH), blk3),
                   pl.BlockSpec((bs, ATTN_WIDTH), lambda i: (i, 0))),
        compiler_params=pltpu.CompilerParams(dimension_semantics=("parallel",),
                                             vmem_limit_bytes=VMEM_LIMIT),
        name="attn_sample",
    )(qx, kc, kn, vc, vn, sink_x)


def _ssd_sample_kernel(xbc_ref, z_ref, dt_ref, cprev_ref, h0_ref, convw_ref, convb_ref, alog_ref,
                       dskip_ref, nw_ref, y_ref, cnew_ref, h1_ref):
    bs = xbc_ref.shape[0]
    x_raw = xbc_ref[...]
    conv = convb_ref[...] + x_raw * convw_ref[CONV_K - 1:CONV_K, :]
    for j in range(CONV_K - 1):
        conv = conv + cprev_ref[j] * convw_ref[j:j + 1, :]
    for j in range(CONV_K - 2):
        cnew_ref[j] = cprev_ref[j + 1]
    cnew_ref[CONV_K - 2] = x_raw
    act = _silu(conv)
    xs = act[:, :SSM_WIDTH]
    bm = act[:, SSM_WIDTH:SSM_WIDTH + SSM_GROUPS * SSM_STATE].astype(BF16)
    cm = act[:, SSM_WIDTH + SSM_GROUPS * SSM_STATE:].astype(BF16)
    lane = lax.broadcasted_iota(jnp.int32, (1, LANES), 1)
    a_neg = jnp.where(lane < SSM_HEADS, -jnp.exp(alog_ref[...]), 0.0)
    dt = dt_ref[...]
    dec = jnp.exp(dt * a_neg)
    xdt = xs * _lane_bcast_pairs(dt, HEAD_PAIRS)
    rowid = lax.broadcasted_iota(jnp.int32, (bs, LANES), 0)
    row_lo = lax.broadcasted_iota(jnp.int32, (LANES, SSM_STATE), 0) < SSM_HEAD_DIM
    ys = []
    for j in range(HEAD_PAIRS):
        g = j // (HEAD_PAIRS // SSM_GROUPS)
        sl = slice(j * LANES, (j + 1) * LANES)
        b_g = bm[:, g * SSM_STATE:(g + 1) * SSM_STATE]
        c_g = cm[:, g * SSM_STATE:(g + 1) * SSM_STATE]
        xdt_p = xdt[:, sl]
        y_p = jnp.zeros((bs, LANES), F32)
        for i in range(bs):
            xi = jnp.where(rowid == i, xdt_p, 0.0).T.astype(BF16)
            d_a = dec[i:i + 1, 2 * j:2 * j + 1]
            d_b = dec[i:i + 1, 2 * j + 1:2 * j + 2]
            decay = jnp.where(row_lo, jnp.broadcast_to(d_a, row_lo.shape), jnp.broadcast_to(d_b, row_lo.shape))
            new = decay * h0_ref[i, j] + _dot(xi, b_g)
            h1_ref[i, j] = new
            y_p = y_p + jnp.where(rowid == i, _dot_nt(c_g, new.astype(BF16)), 0.0)
        ys.append(y_p + dskip_ref[:, sl] * xs[:, sl])
    y = jnp.concatenate(ys, axis=1)
    gated = y * _silu(z_ref[...])
    gw = SSM_WIDTH // SSM_GROUPS
    outs = []
    for g in range(SSM_GROUPS):
        gg = gated[:, g * gw:(g + 1) * gw]
        outs.append(gg * lax.rsqrt(jnp.mean(gg * gg, axis=-1, keepdims=True) + EPS))
    y_ref[...] = (jnp.concatenate(outs, axis=1) * nw_ref[...]).astype(BF16)


def _ssd_sample(xbc, z, dt, cprev_t, h0, w, bs):
    n = xbc.shape[0]
    tok = lambda i: (i, 0)
    const = lambda i: (0, 0)
    full = lambda a: pl.BlockSpec(a.shape, const)
    return pl.pallas_call(
        _ssd_sample_kernel,
        out_shape=(jax.ShapeDtypeStruct((n, SSM_WIDTH), BF16),
                   jax.ShapeDtypeStruct((CONV_K - 1, n, CONV_DIM), F32),
                   jax.ShapeDtypeStruct((n, HEAD_PAIRS, LANES, SSM_STATE), F32)),
        grid=(n // bs,),
        in_specs=[
            pl.BlockSpec((bs, CONV_DIM), tok), pl.BlockSpec((bs, SSM_WIDTH), tok),
            pl.BlockSpec((bs, LANES), tok),
            pl.BlockSpec((CONV_K - 1, bs, CONV_DIM), lambda i: (0, i, 0)),
            pl.BlockSpec((bs, HEAD_PAIRS, LANES, SSM_STATE), lambda i: (i, 0, 0, 0)),
            full(w["conv_w"]), full(w["conv_b"]), full(w["a_log"]),
            full(w["d_skip"]), full(w["ssm_norm"]),
        ],
        out_specs=(pl.BlockSpec((bs, SSM_WIDTH), tok),
                   pl.BlockSpec((CONV_K - 1, bs, CONV_DIM), lambda i: (0, i, 0)),
                   pl.BlockSpec((bs, HEAD_PAIRS, LANES, SSM_STATE), lambda i: (i, 0, 0, 0))),
        compiler_params=pltpu.CompilerParams(dimension_semantics=("parallel",),
                                             vmem_limit_bytes=VMEM_LIMIT),
        name="ssd_sample",
    )(xbc, z, dt, cprev_t, h0, w["conv_w"], w["conv_b"], w["a_log"], w["d_skip"], w["ssm_norm"])


def _pack_bf16_pair(v):
    c = v.shape[1] // 2
    hi = lax.bitcast_convert_type(v[:, :c].astype(BF16).astype(F32), jnp.uint32)
    lo = lax.bitcast_convert_type(v[:, c:].astype(BF16).astype(F32), jnp.uint32)
    return hi | (lo >> 16)


def _unpack_bf16_pair(word):
    a = lax.bitcast_convert_type(word & jnp.uint32(0xFFFF0000), F32)
    b = lax.bitcast_convert_type(word << 16, F32)
    return a, b


def _outproj_router_kernel(x_ref, a_ref, s_ref, wo_ref, n2_ref, wr_ref, br_ref, tri_ref,
                           h_ref, t_ref, route_ref, routet_ref, cnt_ref, carry_ref):
    @pl.when(pl.program_id(0) == 0)
    def _():
        carry_ref[...] = jnp.zeros(carry_ref.shape, F32)

    tm = x_ref.shape[0]
    h_ref[...] = (x_ref[...] + _dot(a_ref[...].astype(BF16), wo_ref[:ATTN_WIDTH, :])
                  + _dot(s_ref[...].astype(BF16), wo_ref[ATTN_WIDTH:, :]))
    sub = min(tm, ROUTER_SUBTILE)
    carry = carry_ref[:, 0:1]
    for s in range(tm // sub):
        rows = slice(s * sub, (s + 1) * sub)
        carry = _route_rows(h_ref[rows, :], n2_ref, wr_ref, br_ref, tri_ref, carry, t_ref.at[rows, :],
                            route_ref.at[rows, :], routet_ref.at[:, rows])
    carry_ref[...] = jnp.broadcast_to(carry, carry_ref.shape)
    cpad = jnp.concatenate([jnp.zeros((EXPERT_LANE0, LANES), F32), jnp.broadcast_to(carry, (N_EXPERTS, LANES)),
                            jnp.zeros((LANES - EXPERT_LANE0 - N_EXPERTS, LANES), F32)], axis=0)
    cnt_ref[...] = cpad.T[0:cnt_ref.shape[0], :]


def _route_rows(h, n2_ref, wrt_ref, brc_ref, triu_ref, carry, t_ref, route_ref, routet_ref):
    ms = jnp.mean(h * h, axis=-1, keepdims=True)
    t = h * lax.rsqrt(ms + EPS) * n2_ref[...]
    t_hi, t_lo = _split2(t)
    t_ref[...] = _pack_bf16_pair(t)
    n = h.shape[0]
    a = _dot_nt(wrt_ref[...], t_hi)
    b = _dot_nt(wrt_ref[0:LANES, :], t_lo)

    def logit_rows(r0, r1):
        return a[r0:r1] + a[LANES + r0:LANES + r1] + b[r0:r1] + brc_ref[r0:r1, :]

    sl = EXP_PER_GROUP
    glog = logit_rows(0, sl)
    elog = logit_rows(EXPERT_LANE0, EXPERT_LANE0 + N_EXPERTS)
    row = lax.broadcasted_iota(jnp.int32, (sl, n), 0).astype(F32)
    big = float(sl)
    ninf = -jnp.inf
    gm = jnp.where(row < N_EGROUPS, glog, ninf)
    gmax = jnp.max(gm, axis=0, keepdims=True)
    g_top = 1.0 / jnp.sum(jnp.exp(gm - gmax), axis=0, keepdims=True)
    g_idx = jnp.min(jnp.where(gm == gmax, row, big), axis=0, keepdims=True)
    ml = elog[(N_EGROUPS - 1) * sl:]
    for g in range(N_EGROUPS - 2, -1, -1):
        ml = jnp.where(g_idx == float(g), elog[g * sl:(g + 1) * sl], ml)
    m1 = jnp.max(ml, axis=0, keepdims=True)
    i1 = jnp.min(jnp.where(ml == m1, row, big), axis=0, keepdims=True)
    ml2 = jnp.where(row == i1, ninf, ml)
    m2 = jnp.max(ml2, axis=0, keepdims=True)
    i2 = jnp.min(jnp.where(ml2 == m2, row, big), axis=0, keepdims=True)
    r = jnp.exp(m2 - m1)
    w1 = g_top / (1.0 + r)
    w2 = g_top * r / (1.0 + r)
    e1 = g_idx * float(sl) + i1
    e2 = g_idx * float(sl) + i2
    pick = jnp.where(jnp.logical_or(row == i1, row == i2), 1.0, 0.0)
    onehot = jnp.concatenate([jnp.where(g_idx == float(g), pick, 0.0) for g in range(N_EGROUPS)], axis=0)
    onehot_bf = onehot.astype(BF16)
    cb = triu_ref.shape[0]
    cums = []
    for blk in range(n // cb):
        c = _dot(onehot_bf[:, blk * cb:(blk + 1) * cb], triu_ref[...]) + carry
        carry = c[:, cb - 1:cb]
        cums.append(c)
    before = jnp.concatenate(cums, axis=1) - onehot
    erow = lax.broadcasted_iota(jnp.int32, (N_EXPERTS, n), 0).astype(F32)
    rank1 = jnp.sum(jnp.where(erow == e1, before, 0.0), axis=0, keepdims=True)
    rank2 = jnp.sum(jnp.where(erow == e2, before, 0.0), axis=0, keepdims=True)
    fields = jnp.concatenate([e1, e2, w1, w2, rank1, rank2], axis=0)
    routet_ref[0:fields.shape[0], :] = fields
    routet_ref[fields.shape[0]:, :] = jnp.zeros((routet_ref.shape[0] - fields.shape[0], n), F32)
    if n % LANES:
        fields = jnp.concatenate([fields, jnp.zeros((fields.shape[0], LANES - n % LANES), F32)], axis=1)
    pad = jnp.zeros((LANES - fields.shape[0], LANES), F32)
    for j in range(fields.shape[1] // LANES):
        blk_rows = jnp.concatenate([fields[:, j * LANES:(j + 1) * LANES], pad], axis=0)
        n_valid = min(LANES, n - j * LANES)
        route_ref[j * LANES:j * LANES + n_valid, :] = blk_rows.T[:n_valid, :]
    return carry


def _outproj_router(x2d, attn, ssm, w, tm):
    t = x2d.shape[0]
    tok = lambda i: (i, 0)
    const = lambda i: (0, 0)
    full = lambda a: pl.BlockSpec(a.shape, const)
    idx = jnp.arange(min(tm, COUNT_BLOCK))
    tri = (idx[:, None] <= idx[None, :]).astype(BF16)
    return pl.pallas_call(
        _outproj_router_kernel,
        out_shape=(jax.ShapeDtypeStruct((t, D_MODEL), F32), jax.ShapeDtypeStruct((t, D_MODEL // 2), jnp.uint32),
                   jax.ShapeDtypeStruct((t, LANES), F32), jax.ShapeDtypeStruct((8, t), F32),
                   jax.ShapeDtypeStruct((8, LANES), F32)),
        grid=(t // tm,),
        in_specs=[
            pl.BlockSpec((tm, D_MODEL), tok), pl.BlockSpec((tm, ATTN_WIDTH), tok),
            pl.BlockSpec((tm, SSM_WIDTH), tok),
            full(w["w_out"]), full(w["norm2"]), full(w["wrt"]),
            full(w["br_col"]), full(tri),
        ],
        out_specs=(pl.BlockSpec((tm, D_MODEL), tok), pl.BlockSpec((tm, D_MODEL // 2), tok),
                   pl.BlockSpec((tm, LANES), tok), pl.BlockSpec((8, tm), lambda i: (0, i)),
                   pl.BlockSpec((8, LANES), const)),
        scratch_shapes=[pltpu.VMEM((N_EXPERTS, LANES), F32)],
        compiler_params=pltpu.CompilerParams(dimension_semantics=("arbitrary",),
                                             vmem_limit_bytes=VMEM_LIMIT),
        name="outproj_router",
    )(x2d, attn, ssm, w["w_out"], w["norm2"], w["wrt"], w["br_col"], tri)


def _expert_hidden(t_a, t_b, w1):
    half = D_MODEL // 2
    gu = _dot(t_a, w1[:half]) + _dot(t_b, w1[half:])
    return _silu(gu[:, :EXPERT_FF]) * gu[:, EXPERT_FF:]


def _gate_up_bf16(wg_ref, wu_ref):
    return jnp.concatenate([wg_ref[0].astype(BF16), wu_ref[0].astype(BF16)], axis=1)


def _moe_dense_kernel(h_ref, t_ref, route_ref, wg_ref, wu_ref, wd_ref, y_ref):
    e = pl.program_id(0)

    @pl.when(e == 0)
    def _():
        y_ref[...] = h_ref[...]

    t_a, t_b = _unpack_bf16_pair(t_ref[...])
    route = route_ref[...]
    e1, e2, g1, g2 = route[:, 0:1], route[:, 1:2], route[:, 2:3], route[:, 3:4]
    e_f = e.astype(F32)
    hid = _expert_hidden(t_a.astype(BF16), t_b.astype(BF16), _gate_up_bf16(wg_ref, wu_ref))
    c_e = jnp.where(e1 == e_f, g1, 0.0) + jnp.where(e2 == e_f, g2, 0.0)
    y_ref[...] += _dot((hid * c_e).astype(BF16), wd_ref[0].astype(BF16))


def _moe_dense(h, t, route, w):
    n = h.shape[0]
    whole = lambda e: (0, 0)
    by_expert = lambda e: (e, 0, 0)
    return pl.pallas_call(
        _moe_dense_kernel,
        out_shape=jax.ShapeDtypeStruct((n, D_MODEL), F32),
        grid=(N_EXPERTS,),
        in_specs=[pl.BlockSpec((n, D_MODEL), whole), pl.BlockSpec((n, D_MODEL // 2), whole),
                  pl.BlockSpec((n, LANES), whole),
                  pl.BlockSpec((1, D_MODEL, EXPERT_FF), by_expert), pl.BlockSpec((1, D_MODEL, EXPERT_FF), by_expert),
                  pl.BlockSpec((1, EXPERT_FF, D_MODEL), by_expert)],
        out_specs=pl.BlockSpec((n, D_MODEL), whole),
        compiler_params=pltpu.CompilerParams(dimension_semantics=("arbitrary",),
                                             vmem_limit_bytes=VMEM_LIMIT),
        name="moe_dense",
    )(h, t, route, w["w_gate"], w["w_up"], w["w_down"])


def _sc_scatter_rows(src, pos1, pos2, n_out):
    t, width = src.shape
    rows_per_worker = t // SC_WORKERS
    n_chunks = rows_per_worker // SC_CHUNK
    assert t == SC_WORKERS * SC_CHUNK * n_chunks
    mesh = plsc.VectorSubcoreMesh(core_axis_name="c", subcore_axis_name="s")

    @functools.partial(
        pl.kernel, mesh=mesh,
        out_type=jax.ShapeDtypeStruct((n_out, width), src.dtype),
        scratch_types=[pltpu.VMEM((SC_CHUNK,), jnp.int32), pltpu.VMEM((SC_CHUNK,), jnp.int32),
                       pltpu.VMEM((SC_CHUNK, width), src.dtype), pltpu.SemaphoreType.DMA],
    )
    def scatter_kernel(src_hbm, p1_hbm, p2_hbm, out_hbm, i1_v, i2_v, rows_v, sem):
        wid = lax.axis_index("s") * SC_CORES + lax.axis_index("c")
        base = wid * rows_per_worker

        @pl.loop(0, n_chunks)
        def _(c):
            off = pl.multiple_of(base + c * SC_CHUNK, 8)
            pltpu.sync_copy(p1_hbm.at[pl.ds(off, SC_CHUNK)], i1_v)
            pltpu.sync_copy(p2_hbm.at[pl.ds(off, SC_CHUNK)], i2_v)
            pltpu.sync_copy(src_hbm.at[pl.ds(off, SC_CHUNK)], rows_v)
            pltpu.async_copy(rows_v, out_hbm.at[i1_v], sem).wait()
            pltpu.async_copy(rows_v, out_hbm.at[i2_v], sem).wait()

    return scatter_kernel(src, pos1, pos2)


def _sc_gather_rows(table, idx):
    n, width = idx.shape[0], table.shape[1]
    rows_per_worker = n // SC_WORKERS
    n_chunks = rows_per_worker // SC_CHUNK
    assert n == SC_WORKERS * SC_CHUNK * n_chunks and n_chunks % 2 == 0
    mesh = plsc.VectorSubcoreMesh(core_axis_name="c", subcore_axis_name="s")

    @functools.partial(
        pl.kernel, mesh=mesh,
        out_type=jax.ShapeDtypeStruct((n, width), table.dtype),
        scratch_types=[pltpu.VMEM((2, SC_CHUNK), jnp.int32), pltpu.VMEM((2, SC_CHUNK, width), table.dtype),
                       pltpu.SemaphoreType.DMA, pltpu.SemaphoreType.DMA,
                       pltpu.SemaphoreType.DMA, pltpu.SemaphoreType.DMA],
    )
    def gather_kernel(table_hbm, idx_hbm, out_hbm, idx_v, rows_v, g0, g1, w0, w1):
        wid = lax.axis_index("s") * SC_CORES + lax.axis_index("c")
        base = wid * rows_per_worker
        gsem = (g0, g1)
        wsem = (w0, w1)

        def gather_copy(slot):
            return pltpu.make_async_copy(table_hbm.at[idx_v.at[slot]], rows_v.at[slot], gsem[slot])

        def write_copy(c, slot):
            off = pl.multiple_of(base + c * SC_CHUNK, 8)
            return pltpu.make_async_copy(rows_v.at[slot], out_hbm.at[pl.ds(off, SC_CHUNK)], wsem[slot])

        def start_gather(c, slot):
            off = pl.multiple_of(base + c * SC_CHUNK, 8)
            pltpu.sync_copy(idx_hbm.at[pl.ds(off, SC_CHUNK)], idx_v.at[slot])
            gather_copy(slot).start()

        start_gather(0, 0)

        @pl.loop(0, n_chunks, step=2)
        def _(c):
            @pl.when(c > 0)
            def _():
                write_copy(c - 1, 1).wait()

            start_gather(c + 1, 1)
            gather_copy(0).wait()
            write_copy(c, 0).start()
            gather_copy(1).wait()
            write_copy(c + 1, 1).start()
            write_copy(c, 0).wait()

            @pl.when(c + 2 < n_chunks)
            def _():
                start_gather(c + 2, 0)

        write_copy(n_chunks - 1, 1).wait()

    return gather_kernel(table, idx)


def _moe_grouped_kernel(te_ref, nt_ref, order_ref, x_ref, wg_ref, wu_ref, wd_ref, o_ref, w1_bf_ref, w2_bf_ref):
    del order_ref
    i = pl.program_id(0)

    @pl.when(jnp.logical_or(i == 0, te_ref[i] != te_ref[jnp.maximum(i - 1, 0)]))
    def _():
        w1_bf_ref[...] = _gate_up_bf16(wg_ref, wu_ref)
        w2_bf_ref[...] = wd_ref[0].astype(BF16)

    @pl.when(i < nt_ref[0])
    def _():
        for s in range(MOE_TILE // MOE_SUBTILE):
            rows = slice(s * MOE_SUBTILE, (s + 1) * MOE_SUBTILE)
            t_a, t_b = _unpack_bf16_pair(x_ref[rows, :])
            hid = _expert_hidden(t_a.astype(BF16), t_b.astype(BF16), w1_bf_ref[...])
            o_ref[rows, :] = _pack_bf16_pair(_dot(hid.astype(BF16), w2_bf_ref[...]))


def _moe_grouped(xs, tile_expert, n_tiles, order, w):
    rows = xs.shape[0]
    row = lambda i, te, nt, od: (i, 0)
    by_expert = lambda i, te, nt, od: (te[i], 0, 0)
    return pl.pallas_call(
        _moe_grouped_kernel,
        out_shape=jax.ShapeDtypeStruct((rows, D_MODEL // 2), jnp.uint32),
        grid_spec=pltpu.PrefetchScalarGridSpec(
            num_scalar_prefetch=3,
            grid=(rows // MOE_TILE,),
            in_specs=[pl.BlockSpec((MOE_TILE, D_MODEL // 2), row),
                      pl.BlockSpec((1, D_MODEL, EXPERT_FF), by_expert),
                      pl.BlockSpec((1, D_MODEL, EXPERT_FF), by_expert),
                      pl.BlockSpec((1, EXPERT_FF, D_MODEL), by_expert)],
            out_specs=pl.BlockSpec((MOE_TILE, D_MODEL // 2), row),
            scratch_shapes=[pltpu.VMEM((D_MODEL, 2 * EXPERT_FF), BF16), pltpu.VMEM((EXPERT_FF, D_MODEL), BF16)],
        ),
        compiler_params=pltpu.CompilerParams(dimension_semantics=("arbitrary",),
                                             vmem_limit_bytes=VMEM_LIMIT),
        name="moe_grouped",
    )(tile_expert, n_tiles, order, xs, w["w_gate"], w["w_up"], w["w_down"])


def _moe_combine_kernel(h_ref, z1_ref, z2_ref, route_ref, y_ref):
    route = route_ref[...]
    g1, g2 = route[:, 2:3], route[:, 3:4]
    half = D_MODEL // 2
    a1, b1 = _unpack_bf16_pair(z1_ref[...])
    a2, b2 = _unpack_bf16_pair(z2_ref[...])
    y_ref[:, :half] = h_ref[:, :half] + g1 * a1 + g2 * a2
    y_ref[:, half:] = h_ref[:, half:] + g1 * b1 + g2 * b2


def _moe_combine(h, z, route, tm):
    t = h.shape[0]
    nb = t // tm
    tok = lambda i: (i, 0)
    return pl.pallas_call(
        _moe_combine_kernel,
        out_shape=jax.ShapeDtypeStruct((t, D_MODEL), F32),
        grid=(nb,),
        in_specs=[pl.BlockSpec((tm, D_MODEL), tok), pl.BlockSpec((tm, D_MODEL // 2), tok),
                  pl.BlockSpec((tm, D_MODEL // 2), lambda i: (i + nb, 0)), pl.BlockSpec((tm, LANES), tok)],
        out_specs=pl.BlockSpec((tm, D_MODEL), tok),
        compiler_params=pltpu.CompilerParams(dimension_semantics=("parallel",),
                                             vmem_limit_bytes=VMEM_LIMIT),
        name="moe_combine",
    )(h, z, z, route)


def _route_pos_kernel(routet_ref, cnt_ref, upper_ref, pos_ref):
    tm = routet_ref.shape[1]
    cnt = cnt_ref[...]
    padded = jnp.floor((cnt + float(MOE_TILE - 1)) * (1.0 / MOE_TILE)) * float(MOE_TILE)
    p_hi, p_mid, p_lo = _split3(padded)
    upper = upper_ref[...]
    starts = (_dot(p_hi, upper) + _dot(p_mid, upper) + _dot(p_lo, upper))[0:1, :]
    starts_col = jnp.broadcast_to(starts, (LANES, LANES)).T[EXPERT_LANE0:EXPERT_LANE0 + N_EXPERTS, 0:1]
    erow = lax.broadcasted_iota(jnp.int32, (N_EXPERTS, tm), 0).astype(F32)
    for k in range(2):
        e_k = routet_ref[k:k + 1, :]
        pos = (jnp.sum(jnp.where(erow == e_k, starts_col, 0.0), axis=0, keepdims=True)
               + routet_ref[4 + k:5 + k, :]).astype(jnp.int32)
        for r in range(tm // LANES):
            pos_ref[k, r:r + 1, :] = pos[:, r * LANES:(r + 1) * LANES]


def _route_positions(routet, counts, tm):
    t = routet.shape[1]
    idx = jnp.arange(LANES)
    upper = (idx[:, None] < idx[None, :]).astype(BF16)
    return pl.pallas_call(
        _route_pos_kernel,
        out_shape=jax.ShapeDtypeStruct((2, t // LANES, LANES), jnp.int32),
        grid=(t // tm,),
        in_specs=[pl.BlockSpec((8, tm), lambda i: (0, i)), pl.BlockSpec((8, LANES), lambda i: (0, 0)),
                  pl.BlockSpec((LANES, LANES), lambda i: (0, 0))],
        out_specs=pl.BlockSpec((2, tm // LANES, LANES), lambda i: (0, i, 0)),
        compiler_params=pltpu.CompilerParams(dimension_semantics=("parallel",),
                                             vmem_limit_bytes=VMEM_LIMIT),
        name="route_positions",
    )(routet, counts, upper)


def _moe_routed(h, t_packed, route, routet, counts, w, tm, run_first):
    t = h.shape[0]
    pos = _route_positions(routet, counts, min(32 * LANES, t))
    pos1 = pos[0].reshape(t)
    pos2 = pos[1].reshape(t)
    cnt = counts[0, EXPERT_LANE0:EXPERT_LANE0 + N_EXPERTS].astype(jnp.int32)
    padded = (cnt + MOE_TILE - 1) // MOE_TILE * MOE_TILE
    ends = jnp.cumsum(padded)
    n_rows = 2 * t + N_EXPERTS * MOE_TILE
    n_tiles = ends[N_EXPERTS - 1] // MOE_TILE
    tile_start = jnp.arange(n_rows // MOE_TILE, dtype=jnp.int32) * MOE_TILE
    tile_start = jnp.minimum(tile_start, ends[N_EXPERTS - 1] - MOE_TILE)
    tile_expert = jnp.sum((tile_start[:, None] >= ends[None, :]).astype(jnp.int32), axis=1)
    xs = _sc_scatter_rows(t_packed, pos1, pos2, n_rows)
    order = lax.bitcast_convert_type(run_first.reshape(-1)[:1].astype(F32), jnp.int32)
    out = _moe_grouped(xs, tile_expert, n_tiles.reshape(1), order, w)
    z = _sc_gather_rows(out, pos.reshape(2 * t))
    return _moe_combine(h, z, route, tm)


def _pad_lanes(a, width=LANES):
    return jnp.pad(a, ((0, 0), (0, width - a.shape[1])))


def _prep_weights(norm1, w_in, q_norm, k_norm, conv_w, conv_b, dt_bias, a_log, d_skip, ssm_norm, w_out,
                  norm2, w_grp, b_grp, w_exp, b_exp, w_gate, w_up, w_down):
    w = {}
    w["norm1"] = norm1.reshape(1, D_MODEL)
    w["w_in"] = _pad_lanes(w_in, XBC_END + LANES).astype(BF16)
    w["qkn"] = jnp.concatenate([jnp.tile(q_norm, N_HEADS), jnp.tile(k_norm, KV_HEADS)]).reshape(1, QK_WIDTH)
    head_of_col = jnp.arange(QK_WIDTH) // HEAD_DIM
    red = (head_of_col[:, None] == jnp.arange(LANES)[None, :])
    w["red"] = red.astype(BF16)
    w["exp"] = red.T.astype(BF16)
    w["conv_w"] = conv_w
    w["conv_b"] = conv_b.reshape(1, CONV_DIM)
    w["dt_bias"] = _pad_lanes(dt_bias.reshape(1, SSM_HEADS))
    w["a_log"] = _pad_lanes(a_log.reshape(1, SSM_HEADS))
    w["d_skip"] = jnp.repeat(d_skip, SSM_HEAD_DIM).reshape(1, SSM_WIDTH)
    w["ssm_norm"] = ssm_norm.reshape(1, SSM_WIDTH)
    idx = jnp.arange(SSD_CHUNK)
    w["tri"] = (idx[None, :] <= idx[:, None]).astype(BF16)
    lane_head = jnp.arange(SSM_WIDTH) // SSM_HEAD_DIM
    w["expand"] = (jnp.arange(LANES)[:, None] == lane_head[None, :]).astype(BF16)
    w["w_out"] = w_out.astype(BF16)
    w["norm2"] = norm2.reshape(1, D_MODEL)
    wr = jnp.zeros((D_MODEL, LANES), F32)
    wr = wr.at[:, :N_EGROUPS].set(w_grp).at[:, EXPERT_LANE0:EXPERT_LANE0 + N_EXPERTS].set(w_exp)
    wr_hi = wr.astype(BF16)
    w["wrt"] = jnp.concatenate([wr_hi, (wr - wr_hi.astype(F32)).astype(BF16)], axis=1).T
    br = jnp.zeros((LANES, 1), F32)
    w["br_col"] = br.at[:N_EGROUPS, 0].set(b_grp).at[EXPERT_LANE0:EXPERT_LANE0 + N_EXPERTS, 0].set(b_exp)
    w["w_gate"], w["w_up"], w["w_down"] = w_gate, w_up, w_down
    return w


def _rope_tables(pos):
    inv = 1.0 / (ROPE_THETA ** (jnp.arange(0, HEAD_DIM, 2, dtype=F32) / HEAD_DIM))
    ang = pos.astype(F32)[:, None] * inv[None, :]
    cos, sin = jnp.cos(ang), jnp.sin(ang)
    reps = LANES // HEAD_DIM
    return (jnp.tile(jnp.concatenate([cos, cos], axis=-1), (1, reps)),
            jnp.tile(jnp.concatenate([-sin, sin], axis=-1), (1, reps)))


def _token_tile(t):
    for tm in (1024, 512, 256, 128, 64, 32, 16):
        if t % tm == 0:
            return tm
    raise ValueError(f"token count {t} is not a multiple of 16")


def kernel(x_prompt, x_sample, cache_win_k, cache_win_v, state_conv, state_ssm, norm1, w_in, q_norm, k_norm,
           sinks, conv_w, conv_b, dt_bias, a_log, d_skip, ssm_norm, w_out, norm2, w_grp, b_grp, w_exp, b_exp,
           w_gate, w_up, w_down):
    depth = norm1.shape[0]
    assert depth == 1, "single-layer stack"
    bp, lp, _ = x_prompt.shape
    bsn, ls, _ = x_sample.shape
    assert ls == 1 and lp % WINDOW == 0 and cache_win_k.shape[2] == WINDOW
    l = 0
    w = _prep_weights(norm1[l], w_in[l], q_norm[l], k_norm[l], conv_w[l], conv_b[l], dt_bias[l], a_log[l],
                      d_skip[l], ssm_norm[l], w_out[l], norm2[l], w_grp[l], b_grp[l], w_exp[l], b_exp[l],
                      w_gate[l], w_up[l], w_down[l])
    sink = sinks[l]

    tp = bp * lp
    xp = x_prompt.reshape(tp, D_MODEL)
    tm_p = _token_tile(lp)
    cos_p, sin_p = _rope_tables(jnp.arange(lp, dtype=jnp.int32))
    q, k, v, z, xbc, dt, k2, v2 = _inproj(xp, w, cos_p, sin_p, tm_p, lp // tm_p)
    attn = _attn_prompt(q, k2, v2, sink, bp, lp)
    ssm, st_p = _ssd_prompt(xbc, z, dt, w, bp, lp)
    h, t, route, routet, counts = _outproj_router(xp, attn, ssm, w, tm_p)
    k3 = k.reshape(bp, lp, KV_HEADS, HEAD_DIM)
    v3 = v.reshape(bp, lp, KV_HEADS, HEAD_DIM)
    win_k_p = k3[:, lp - WINDOW:][None]
    win_v_p = v3[:, lp - WINDOW:][None]
    conv_p = xbc.reshape(bp, lp, CONV_DIM)[:, lp - (CONV_K - 1):][None]
    ssm_p = st_p.reshape(1, bp, SSM_HEADS, SSM_HEAD_DIM, SSM_STATE)

    xs2 = x_sample.reshape(bsn, D_MODEL)
    tm_s = _token_tile(bsn)
    cos_s, sin_s = _rope_tables(jnp.full((tm_s,), PAST_LEN, jnp.int32))
    q_s, k_s, v_s, z_s, xbc_s, dt_s, _, _ = _inproj(xs2, w, cos_s, sin_s, tm_s, 1)
    q4 = q_s.reshape(bsn, KV_HEADS, N_HEADS // KV_HEADS, HEAD_DIM)
    zq = jnp.zeros_like(q4[:, 0])
    qx = jnp.concatenate([jnp.concatenate([q4[:, 0], zq], axis=-1),
                          jnp.concatenate([zq, q4[:, 1]], axis=-1)], axis=1)
    qx = jnp.pad(qx, ((0, 0), (0, BF16_ROWS - N_HEADS), (0, 0)))
    sink_x = jnp.pad(jnp.broadcast_to(sink[:, None], (N_HEADS, LANES)), ((0, BF16_ROWS - N_HEADS), (0, 0)))
    kc = cache_win_k[l].reshape(bsn, WINDOW, KV_WIDTH)
    vc = cache_win_v[l].reshape(bsn, WINDOW, KV_WIDTH)
    ko, vo, attn_s = _attn_sample(qx, kc, k_s.reshape(bsn, 1, KV_WIDTH), vc, v_s.reshape(bsn, 1, KV_WIDTH),
                               sink_x, 8)
    y_prompt = _moe_routed(h, t, route, routet, counts, w, tm_p, attn_s).reshape(bp, lp, D_MODEL)
    cprev_t = jnp.transpose(state_conv[l], (1, 0, 2))
    h0 = state_ssm[l].reshape(bsn, HEAD_PAIRS, LANES, SSM_STATE)
    ssm_s, cnew_t, h1 = _ssd_sample(xbc_s, z_s, dt_s, cprev_t, h0, w, 16)
    h_s, t_s, route_s, _, _ = _outproj_router(xs2, attn_s, ssm_s, w, tm_s)
    y_sample = _moe_dense(h_s, t_s, route_s, w).reshape(bsn, 1, D_MODEL)
    win_k_s = ko.reshape(1, bsn, WINDOW, KV_HEADS, HEAD_DIM)
    win_v_s = vo.reshape(1, bsn, WINDOW, KV_HEADS, HEAD_DIM)
    conv_s = jnp.transpose(cnew_t, (1, 0, 2))[None]
    ssm_s_state = h1.reshape(1, bsn, SSM_HEADS, SSM_HEAD_DIM, SSM_STATE)

    return (y_prompt, y_sample, win_k_p, win_v_p, conv_p, ssm_p, win_k_s, win_v_s, conv_s, ssm_s_state)
```

```python
import functools
import math

import jax
import jax.numpy as jnp
from jax import lax
from jax.experimental import pallas as pl
from jax.experimental.pallas import tpu as pltpu
from jax.experimental.pallas import tpu_sc as plsc

F32 = jnp.float32
BF16 = jnp.bfloat16

D_MODEL = 1024
HEAD_DIM = 64
N_HEADS = 8
KV_HEADS = 2
WINDOW = 128
ATTN_WIDTH = N_HEADS * HEAD_DIM
QK_WIDTH = ATTN_WIDTH + KV_HEADS * HEAD_DIM
KV_WIDTH = KV_HEADS * HEAD_DIM
ATTN_SCALE = HEAD_DIM ** -0.5
ROPE_THETA = 10000.0
SSM_WIDTH = 512
SSM_HEADS = 8
SSM_HEAD_DIM = 64
SSM_GROUPS = 2
SSM_STATE = 128
CONV_K = 4
CONV_HALO = 8
CONV_DIM = SSM_WIDTH + 2 * SSM_GROUPS * SSM_STATE
SSD_CHUNK = 128
N_EGROUPS = 4
EXP_PER_GROUP = 8
N_EXPERTS = 32
EXPERT_FF = 128
EPS = 1e-6
PAST_LEN = 16384

LANES = 128
BF16_ROWS = 16
HEAD_PAIRS = SSM_HEADS // 2
EXPERT_LANE0 = 32
VMEM_LIMIT = 56 * 1024 * 1024
MOE_TILE = 1024
MOE_SUBTILE = 128
ATTN_QBLOCKS = 8
SSD_CHUNKS_PER_STEP = 8
COUNT_BLOCK = 256
ROUTER_SUBTILE = 1024
INPROJ_SUBTILE = 512
SC_CORES = 2
SC_SUBCORES = 16
SC_WORKERS = SC_CORES * SC_SUBCORES
SC_CHUNK = 64

Q_END = ATTN_WIDTH
K_END = Q_END + KV_WIDTH
V_END = K_END + KV_WIDTH
Z_END = V_END + SSM_WIDTH
XBC_END = Z_END + CONV_DIM


def _dot(a, b):
    return jnp.dot(a, b, preferred_element_type=F32)


def _dot_nt(a, b):
    return lax.dot_general(a, b, (((1,), (1,)), ((), ())), preferred_element_type=F32)


def _split2(v):
    hi = v.astype(BF16)
    lo = (v - hi.astype(F32)).astype(BF16)
    return hi, lo


def _split3(v):
    hi = v.astype(BF16)
    r = v - hi.astype(F32)
    mid = r.astype(BF16)
    lo = (r - mid.astype(F32)).astype(BF16)
    return hi, mid, lo


def _silu(x):
    return x * jax.nn.sigmoid(x)


def _softplus(x):
    return jnp.maximum(x, 0.0) + jnp.log1p(jnp.exp(-jnp.abs(x)))


def _lane_bcast_pairs(v, n_pairs):
    r = v.shape[0]
    lo = lax.broadcasted_iota(jnp.int32, (r, LANES), 1) < HEAD_DIM
    slabs = []
    for j in range(n_pairs):
        a = jnp.broadcast_to(v[:, 2 * j:2 * j + 1], (r, LANES))
        b = jnp.broadcast_to(v[:, 2 * j + 1:2 * j + 2], (r, LANES))
        slabs.append(jnp.where(lo, a, b))
    return jnp.concatenate(slabs, axis=1)


def _causal_conv_silu(x_ext, convw_ref, convb_ref):
    halo = CONV_HALO
    x_raw = x_ext[halo:, :]
    conv = convb_ref[...] + x_raw * convw_ref[CONV_K - 1:CONV_K, :]
    for j in range(CONV_K - 1):
        shifted = pltpu.roll(x_ext, CONV_K - 1 - j, axis=0)[halo:, :]
        conv = conv + shifted * convw_ref[j:j + 1, :]
    return _silu(conv)


def _inproj_kernel(x_ref, n1_ref, win_ref, dtb_ref, qkn_ref,
                   cos_ref, sin_ref, red_ref, exp_ref,
                   q_ref, k_ref, v_ref, z_ref, xbc_ref, dt_ref, k2_ref, v2_ref):
    tm = x_ref.shape[0]
    sub = min(tm, INPROJ_SUBTILE)
    lane = lax.broadcasted_iota(jnp.int32, (sub, LANES), 1)
    first_half = (lane % HEAD_DIM) < (HEAD_DIM // 2)
    for s in range(tm // sub):
        rows = slice(s * sub, (s + 1) * sub)
        x = x_ref[rows, :]
        ms = jnp.mean(x * x, axis=-1, keepdims=True)
        xn = (x * lax.rsqrt(ms + EPS) * n1_ref[...]).astype(BF16)
        v = _dot(xn, win_ref[:, K_END:V_END])
        v_ref[rows, :] = v
        v2_ref[rows, :] = _pair_operands(v)
        z_ref[rows, :] = _dot(xn, win_ref[:, V_END:Z_END])
        xbc_ref[rows, :] = _dot(xn, win_ref[:, Z_END:XBC_END])
        dt_ref[rows, :] = _softplus(_dot(xn, win_ref[:, XBC_END:]) + dtb_ref[...])
        qk = _dot(xn, win_ref[:, :K_END])
        ss = _dot((qk * qk).astype(BF16), red_ref[...])
        inv = lax.rsqrt(ss * (1.0 / HEAD_DIM) + EPS)
        inv_hi, inv_lo = _split2(inv)
        inv_x = _dot(inv_hi, exp_ref[...]) + _dot(inv_lo, exp_ref[...])
        qkn = qk * inv_x * qkn_ref[...]
        cos = cos_ref[rows, :]
        sin = sin_ref[rows, :]
        for c in range(QK_WIDTH // LANES):
            xc = qkn[:, c * LANES:(c + 1) * LANES]
            partner = jnp.where(first_half,
                                pltpu.roll(xc, LANES - HEAD_DIM // 2, axis=1),
                                pltpu.roll(xc, HEAD_DIM // 2, axis=1))
            rot = xc * cos + partner * sin
            if c < ATTN_WIDTH // LANES:
                q_ref[rows, c * LANES:(c + 1) * LANES] = (rot * ATTN_SCALE).astype(BF16)
            else:
                k_ref[rows, :] = rot
                k2_ref[rows, :] = _pair_operands(rot)


def _inproj(x2d, w, cos_tab, sin_tab, tm, n_pos_blocks):
    t = x2d.shape[0]
    grid = (t // tm,)
    tok = lambda i: (i, 0)
    const = lambda i: (0, 0)
    pos = lambda i: (i % n_pos_blocks, 0)
    full = lambda a: pl.BlockSpec(a.shape, const)
    rows = lambda width, dtype: (jax.ShapeDtypeStruct((t, width), dtype), pl.BlockSpec((tm, width), tok))
    outs = [rows(ATTN_WIDTH, BF16), rows(KV_WIDTH, F32), rows(KV_WIDTH, F32), rows(SSM_WIDTH, F32)]
    operands = [x2d, w["norm1"], w["w_in"], w["dt_bias"], w["qkn"],
                cos_tab, sin_tab, w["red"], w["exp"]]
    in_specs = [
        pl.BlockSpec((tm, D_MODEL), tok),
        full(w["norm1"]), full(w["w_in"]), full(w["dt_bias"]), full(w["qkn"]),
        pl.BlockSpec((tm, LANES), pos), pl.BlockSpec((tm, LANES), pos),
        full(w["red"]), full(w["exp"]),
    ]
    outs += [rows(CONV_DIM, F32), rows(LANES, F32), rows(4 * LANES, BF16), rows(4 * LANES, BF16)]
    return pl.pallas_call(
        _inproj_kernel,
        out_shape=tuple(o[0] for o in outs),
        grid=grid,
        in_specs=in_specs,
        out_specs=tuple(o[1] for o in outs),
        compiler_params=pltpu.CompilerParams(dimension_semantics=("parallel",),
                                             vmem_limit_bytes=VMEM_LIMIT),
        name="inproj",
    )(*operands)


def _pair_operands(kv):
    lo = lax.broadcasted_iota(jnp.int32, kv.shape, 1) < HEAD_DIM
    swapped = pltpu.roll(kv, HEAD_DIM, axis=1)
    parts = [jnp.where(lo, kv, 0.0), jnp.where(lo, 0.0, swapped), jnp.where(lo, swapped, 0.0), jnp.where(lo, 0.0, kv)]
    return jnp.concatenate(parts, axis=1).astype(BF16)


def _pair_rhs(blk, g):
    return jnp.concatenate([blk[:, 2 * g * LANES:(2 * g + 1) * LANES],
                            blk[:, (2 * g + 1) * LANES:(2 * g + 2) * LANES]], axis=0)


def _attn_qblock(sink_ref, q_blk, k_prev, k_cur, v_prev, v_cur, seq_start, o_ref):
    blk = WINDOW
    qi = lax.broadcasted_iota(jnp.int32, (blk, 2 * blk), 0)
    kj = lax.broadcasted_iota(jnp.int32, (blk, 2 * blk), 1) % blk
    cur_ok = kj <= qi
    lo = lax.broadcasted_iota(jnp.int32, (blk, LANES), 1) < HEAD_DIM
    n_pairs = N_HEADS // KV_HEADS // 2
    for g in range(KV_HEADS):
        q_all = jnp.concatenate([q_blk[:, (g * n_pairs + r) * LANES:(g * n_pairs + r + 1) * LANES]
                                 for r in range(n_pairs)], axis=0)
        s_all = _dot_nt(q_all, jnp.concatenate([_pair_rhs(k_cur, g), _pair_rhs(k_prev, g)], axis=0))
        p_rows = []
        den_rows = []
        for r in range(n_pairs):
            pair = g * n_pairs + r
            s_cur = s_all[r * blk:(r + 1) * blk, :2 * blk]
            s_prev = s_all[r * blk:(r + 1) * blk, 2 * blk:]
            s = jnp.where(cur_ok, s_cur, s_prev)
            if seq_start is not None:
                s = jnp.where(jnp.logical_or(cur_ok, jnp.logical_not(seq_start)), s, -jnp.inf)
            ps = []
            dens = []
            for hh in range(2):
                sink = sink_ref[2 * pair + hh]
                sh = s[:, hh * blk:(hh + 1) * blk]
                m = jnp.maximum(jnp.max(sh, axis=-1, keepdims=True), sink)
                p = jnp.exp(sh - m)
                dens.append(jnp.sum(p, axis=-1, keepdims=True) + jnp.exp(sink - m))
                ps.append(p)
            p2 = jnp.concatenate(ps, axis=1)
            p_rows.append(jnp.concatenate([jnp.where(cur_ok, p2, 0.0), jnp.where(cur_ok, 0.0, p2)],
                                          axis=1).astype(BF16))
            den_rows.append(jnp.where(lo, dens[0], dens[1]))
        o_all = _dot(jnp.concatenate(p_rows, axis=0),
                     jnp.concatenate([_pair_rhs(v_cur, g), _pair_rhs(v_prev, g)], axis=0))
        for r in range(n_pairs):
            pair = g * n_pairs + r
            o2 = o_all[r * blk:(r + 1) * blk, :]
            o_ref[:, pair * LANES:(pair + 1) * LANES] = (o2 / den_rows[r]).astype(BF16)


def _attn_kernel(sink_ref, q_ref, kc_ref, kp_ref, vc_ref, vp_ref, o_ref):
    blk = WINDOW
    first_step = pl.program_id(1) == 0
    for u in range(q_ref.shape[0] // blk):
        rows = slice(u * blk, (u + 1) * blk)
        prev_rows = slice((u - 1) * blk, u * blk)
        k_prev = kp_ref[...] if u == 0 else kc_ref[prev_rows, :]
        v_prev = vp_ref[...] if u == 0 else vc_ref[prev_rows, :]
        _attn_qblock(sink_ref, q_ref[rows, :], k_prev, kc_ref[rows, :], v_prev, vc_ref[rows, :],
                     first_step if u == 0 else None, o_ref.at[rows, :])


def _attn_prompt(q, k, v, sinks, batch, seq):
    n_sub = ATTN_QBLOCKS if seq % (ATTN_QBLOCKS * WINDOW) == 0 else 1
    rows = n_sub * WINDOW
    nb = seq // rows
    cur = lambda b, j, s: (b * nb + j, 0)
    prev = lambda b, j, s: (jnp.maximum((b * nb + j) * n_sub - 1, 0), 0)
    return pl.pallas_call(
        _attn_kernel,
        out_shape=jax.ShapeDtypeStruct((batch * seq, ATTN_WIDTH), BF16),
        grid_spec=pltpu.PrefetchScalarGridSpec(
            num_scalar_prefetch=1,
            grid=(batch, nb),
            in_specs=[
                pl.BlockSpec((rows, ATTN_WIDTH), cur),
                pl.BlockSpec((rows, 4 * LANES), cur), pl.BlockSpec((WINDOW, 4 * LANES), prev),
                pl.BlockSpec((rows, 4 * LANES), cur), pl.BlockSpec((WINDOW, 4 * LANES), prev),
            ],
            out_specs=pl.BlockSpec((rows, ATTN_WIDTH), cur),
        ),
        compiler_params=pltpu.CompilerParams(dimension_semantics=("parallel", "parallel"),
                                             vmem_limit_bytes=VMEM_LIMIT),
        name="attn_prompt",
    )(sinks, q, k, k, v, v)


def _ssd_kernel(xbc_ref, z_ref, dt_ref, convw_ref, convb_ref, alog_ref, dskip_ref, nw_ref,
                tri_ref, expand_ref, y_ref, st_ref, buf_ref, state_ref):
    c = pl.program_id(1)
    cl = SSD_CHUNK
    n_sub = xbc_ref.shape[0] // cl
    halo = CONV_HALO

    @pl.when(c == 0)
    def _():
        buf_ref[...] = jnp.zeros(buf_ref.shape, F32)
        state_ref[...] = jnp.zeros(state_ref.shape, F32)

    lane = lax.broadcasted_iota(jnp.int32, (1, LANES), 1)
    a_neg = jnp.where(lane < SSM_HEADS, -jnp.exp(alog_ref[...]), 0.0)
    tri = tri_ref[...]
    for u in range(n_sub):
        rows = slice(u * cl, (u + 1) * cl)
        if u == 0:
            x_ext = jnp.concatenate([buf_ref[...], xbc_ref[rows, :]], axis=0)
        else:
            x_ext = xbc_ref[u * cl - halo:(u + 1) * cl, :]
        _ssd_chunk(x_ext, z_ref[rows, :], dt_ref[rows, :], a_neg, tri, convw_ref, convb_ref, dskip_ref, nw_ref,
                   expand_ref, y_ref.at[rows, :], state_ref)
    buf_ref[...] = xbc_ref[n_sub * cl - halo:n_sub * cl, :]

    @pl.when(c == pl.num_programs(1) - 1)
    def _():
        st_ref[0] = state_ref[...]


def _ssd_chunk(x_ext, z, dt, a_neg, tri, convw_ref, convb_ref, dskip_ref, nw_ref, expand_ref, y_ref, state_ref):
    cl = SSD_CHUNK
    act = _causal_conv_silu(x_ext, convw_ref, convb_ref)
    xs = act[:, :SSM_WIDTH]
    bm = act[:, SSM_WIDTH:SSM_WIDTH + SSM_GROUPS * SSM_STATE].astype(BF16)
    cm = act[:, SSM_WIDTH + SSM_GROUPS * SSM_STATE:].astype(BF16)

    dta = dt * a_neg
    p_hi, p_mid, p_lo = _split3(dta)
    a_col = _dot(tri, p_hi) + _dot(tri, p_mid) + _dot(tri, p_lo)
    a_last = a_col[cl - 1:cl, :]
    a_row = a_col.T
    per_head = jnp.concatenate([dt, jnp.exp(a_col), jnp.exp(a_last - a_col)], axis=0)
    ph_hi, ph_lo = _split2(per_head)
    per_lane = _dot(ph_hi, expand_ref[...]) + _dot(ph_lo, expand_ref[...])
    dt_x = per_lane[:cl]
    ecol_x = per_lane[cl:2 * cl]
    dte_x = per_lane[2 * cl:]
    e_last = jnp.exp(a_last)
    xdt = xs * dt_x

    li = lax.broadcasted_iota(jnp.int32, (cl, cl), 0)
    si = lax.broadcasted_iota(jnp.int32, (cl, cl), 1)
    causal = si <= li
    lo = lax.broadcasted_iota(jnp.int32, (cl, LANES), 1) < SSM_HEAD_DIM
    row_lo = lax.broadcasted_iota(jnp.int32, (LANES, SSM_STATE), 0) < SSM_HEAD_DIM

    ys = []
    for g in range(SSM_GROUPS):
        b_g = bm[:, g * SSM_STATE:(g + 1) * SSM_STATE]
        c_g = cm[:, g * SSM_STATE:(g + 1) * SSM_STATE]
        cb = _dot_nt(c_g, b_g)
        for r in range(HEAD_PAIRS // SSM_GROUPS):
            j = g * (HEAD_PAIRS // SSM_GROUPS) + r
            sl = slice(j * LANES, (j + 1) * LANES)
            xdt_p = xdt[:, sl]
            ms = []
            for hh in range(2):
                h = 2 * j + hh
                seg = a_col[:, h:h + 1] - a_row[h:h + 1, :]
                ms.append(cb * jnp.exp(jnp.where(causal, seg, -jnp.inf)))
            m2 = jnp.concatenate(ms, axis=1).astype(BF16)
            rhs = jnp.concatenate([jnp.where(lo, xdt_p, 0.0), jnp.where(lo, 0.0, xdt_p)],
                                  axis=0).astype(BF16)
            y_diag = _dot(m2, rhs)
            st = state_ref[j]
            y_off = _dot_nt(c_g, st.astype(BF16)) * ecol_x[:, sl]
            xdt_e = (xdt_p * dte_x[:, sl]).T.astype(BF16)
            d_a = e_last[:, 2 * j:2 * j + 1]
            d_b = e_last[:, 2 * j + 1:2 * j + 2]
            decay = jnp.where(row_lo, jnp.broadcast_to(d_a, row_lo.shape), jnp.broadcast_to(d_b, row_lo.shape))
            state_ref[j] = decay * st + _dot(xdt_e, b_g)
            ys.append(y_diag + y_off + dskip_ref[:, sl] * xs[:, sl])
    y = jnp.concatenate(ys, axis=1)
    gated = y * _silu(z)
    gw = SSM_WIDTH // SSM_GROUPS
    outs = []
    for g in range(SSM_GROUPS):
        gg = gated[:, g * gw:(g + 1) * gw]
        outs.append(gg * lax.rsqrt(jnp.mean(gg * gg, axis=-1, keepdims=True) + EPS))
    y_ref[...] = (jnp.concatenate(outs, axis=1) * nw_ref[...]).astype(BF16)


def _ssd_prompt(xbc, z, dt, w, batch, seq):
    n_sub = SSD_CHUNKS_PER_STEP if seq % (SSD_CHUNKS_PER_STEP * SSD_CHUNK) == 0 else 1
    rows = n_sub * SSD_CHUNK
    nc = seq // rows
    tok = lambda b, c: (b * nc + c, 0)
    const = lambda b, c: (0, 0)
    full = lambda a: pl.BlockSpec(a.shape, const)
    return pl.pallas_call(
        _ssd_kernel,
        out_shape=(jax.ShapeDtypeStruct((batch * seq, SSM_WIDTH), BF16),
                   jax.ShapeDtypeStruct((batch, HEAD_PAIRS, LANES, SSM_STATE), F32)),
        grid=(batch, nc),
        in_specs=[
            pl.BlockSpec((rows, CONV_DIM), tok), pl.BlockSpec((rows, SSM_WIDTH), tok),
            pl.BlockSpec((rows, LANES), tok),
            full(w["conv_w"]), full(w["conv_b"]), full(w["a_log"]),
            full(w["d_skip"]), full(w["ssm_norm"]), full(w["tri"]), full(w["expand"]),
        ],
        out_specs=(pl.BlockSpec((rows, SSM_WIDTH), tok),
                   pl.BlockSpec((1, HEAD_PAIRS, LANES, SSM_STATE), lambda b, c: (b, 0, 0, 0))),
        scratch_shapes=[pltpu.VMEM((CONV_HALO, CONV_DIM), F32),
                        pltpu.VMEM((HEAD_PAIRS, LANES, SSM_STATE), F32)],
        compiler_params=pltpu.CompilerParams(dimension_semantics=("parallel", "arbitrary"),
                                             vmem_limit_bytes=VMEM_LIMIT),
        name="ssd_prompt",
    )(xbc, z, dt, w["conv_w"], w["conv_b"], w["a_log"], w["d_skip"], w["ssm_norm"], w["tri"], w["expand"])


def _attn_sample_kernel(qx_ref, kc_ref, kn_ref, vc_ref, vn_ref, sink_ref, ko_ref, vo_ref, o_ref):
    bs = qx_ref.shape[0]
    w = kc_ref.shape[1]
    sink = sink_ref[...]
    lo = lax.broadcasted_iota(jnp.int32, (1, LANES), 1) < HEAD_DIM
    for i in range(bs):
        ko_ref[i, 0:w - 1, :] = kc_ref[i, 1:w, :]
        ko_ref[i, w - 1:w, :] = kn_ref[i]
        vo_ref[i, 0:w - 1, :] = vc_ref[i, 1:w, :]
        vo_ref[i, w - 1:w, :] = vn_ref[i]
        s = _dot_nt(qx_ref[i], ko_ref[i].astype(BF16))
        m = jnp.maximum(jnp.max(s, axis=-1, keepdims=True), sink)
        p = jnp.exp(s - m)
        den = jnp.sum(p, axis=-1, keepdims=True) + jnp.exp(sink - m)
        o = _dot(p.astype(BF16), vo_ref[i].astype(BF16)) / den
        o_sw = pltpu.roll(o, HEAD_DIM, axis=1)
        for j in range(N_HEADS // 2):
            a, b = (o, o_sw) if j < N_HEADS // 4 else (o_sw, o)
            o_ref[i:i + 1, j * LANES:(j + 1) * LANES] = jnp.where(lo, a[2 * j:2 * j + 1], b[2 * j + 1:2 * j + 2])


def _attn_sample(qx, kc, kn, vc, vn, sink_x, bs):
    n, w = kc.shape[0], kc.shape[1]
    blk3 = lambda i: (i, 0, 0)
    return pl.pallas_call(
        _attn_sample_kernel,
        out_shape=(jax.ShapeDtypeStruct((n, w, KV_WIDTH), F32),
                   jax.ShapeDtypeStruct((n, w, KV_WIDTH), F32),
                   jax.ShapeDtypeStruct((n, ATTN_WIDTH), F32)),
        grid=(n // bs,),
        in_specs=[
            pl.BlockSpec((bs, BF16_ROWS, LANES), blk3),
            pl.BlockSpec((bs, w, KV_WIDTH), blk3), pl.BlockSpec((bs, 1, KV_WIDTH), blk3),
            pl.BlockSpec((bs, w, KV_WIDTH), blk3), pl.BlockSpec((bs, 1, KV_WIDTH), blk3),
            pl.BlockSpec(sink_x.shape, lambda i: (0, 0)),
        ],
        out_specs=(pl.BlockSpec((bs, w, KV_WIDTH), blk3), pl.BlockSpec((bs, w, KV_WIDTH), blk3),
                   pl.BlockSpec((bs, ATTN_WIDTH), lambda i: (i, 0))),
        compiler_params=pltpu.CompilerParams(dimension_semantics=("parallel",),
                                             vmem_limit_bytes=VMEM_LIMIT),
        name="attn_sample",
    )(qx, kc, kn, vc, vn, sink_x)


def _ssd_sample_kernel(xbc_ref, z_ref, dt_ref, cprev_ref, h0_ref, convw_ref, convb_ref, alog_ref,
                       dskip_ref, nw_ref, y_ref, cnew_ref, h1_ref):
    bs = xbc_ref.shape[0]
    x_raw = xbc_ref[...]
    conv = convb_ref[...] + x_raw * convw_ref[CONV_K - 1:CONV_K, :]
    for j in range(CONV_K - 1):
        conv = conv + cprev_ref[j] * convw_ref[j:j + 1, :]
    for j in range(CONV_K - 2):
        cnew_ref[j] = cprev_ref[j + 1]
    cnew_ref[CONV_K - 2] = x_raw
    act = _silu(conv)
    xs = act[:, :SSM_WIDTH]
    bm = act[:, SSM_WIDTH:SSM_WIDTH + SSM_GROUPS * SSM_STATE].astype(BF16)
    cm = act[:, SSM_WIDTH + SSM_GROUPS * SSM_STATE:].astype(BF16)
    lane = lax.broadcasted_iota(jnp.int32, (1, LANES), 1)
    a_neg = jnp.where(lane < SSM_HEADS, -jnp.exp(alog_ref[...]), 0.0)
    dt = dt_ref[...]
    dec = jnp.exp(dt * a_neg)
    xdt = xs * _lane_bcast_pairs(dt, HEAD_PAIRS)
    rowid = lax.broadcasted_iota(jnp.int32, (bs, LANES), 0)
    row_lo = lax.broadcasted_iota(jnp.int32, (LANES, SSM_STATE), 0) < SSM_HEAD_DIM
    ys = []
    for j in range(HEAD_PAIRS):
        g = j // (HEAD_PAIRS // SSM_GROUPS)
        sl = slice(j * LANES, (j + 1) * LANES)
        b_g = bm[:, g * SSM_STATE:(g + 1) * SSM_STATE]
        c_g = cm[:, g * SSM_STATE:(g + 1) * SSM_STATE]
        xdt_p = xdt[:, sl]
        y_p = jnp.zeros((bs, LANES), F32)
        for i in range(bs):
            xi = jnp.where(rowid == i, xdt_p, 0.0).T.astype(BF16)
            d_a = dec[i:i + 1, 2 * j:2 * j + 1]
            d_b = dec[i:i + 1, 2 * j + 1:2 * j + 2]
            decay = jnp.where(row_lo, jnp.broadcast_to(d_a, row_lo.shape), jnp.broadcast_to(d_b, row_lo.shape))
            new = decay * h0_ref[i, j] + _dot(xi, b_g)
            h1_ref[i, j] = new
            y_p = y_p + jnp.where(rowid == i, _dot_nt(c_g, new.astype(BF16)), 0.0)
        ys.append(y_p + dskip_ref[:, sl] * xs[:, sl])
    y = jnp.concatenate(ys, axis=1)
    gated = y * _silu(z_ref[...])
    gw = SSM_WIDTH // SSM_GROUPS
    outs = []
    for g in range(SSM_GROUPS):
        gg = gated[:, g * gw:(g + 1) * gw]
        outs.append(gg * lax.rsqrt(jnp.mean(gg * gg, axis=-1, keepdims=True) + EPS))
    y_ref[...] = (jnp.concatenate(outs, axis=1) * nw_ref[...]).astype(BF16)


def _ssd_sample(xbc, z, dt, cprev_t, h0, w, bs):
    n = xbc.shape[0]
    tok = lambda i: (i, 0)
    const = lambda i: (0, 0)
    full = lambda a: pl.BlockSpec(a.shape, const)
    return pl.pallas_call(
        _ssd_sample_kernel,
        out_shape=(jax.ShapeDtypeStruct((n, SSM_WIDTH), BF16),
                   jax.ShapeDtypeStruct((CONV_K - 1, n, CONV_DIM), F32),
                   jax.ShapeDtypeStruct((n, HEAD_PAIRS, LANES, SSM_STATE), F32)),
        grid=(n // bs,),
        in_specs=[
            pl.BlockSpec((bs, CONV_DIM), tok), pl.BlockSpec((bs, SSM_WIDTH), tok),
            pl.BlockSpec((bs, LANES), tok),
            pl.BlockSpec((CONV_K - 1, bs, CONV_DIM), lambda i: (0, i, 0)),
            pl.BlockSpec((bs, HEAD_PAIRS, LANES, SSM_STATE), lambda i: (i, 0, 0, 0)),
            full(w["conv_w"]), full(w["conv_b"]), full(w["a_log"]),
            full(w["d_skip"]), full(w["ssm_norm"]),
        ],
        out_specs=(pl.BlockSpec((bs, SSM_WIDTH), tok),
                   pl.BlockSpec((CONV_K - 1, bs, CONV_DIM), lambda i: (0, i, 0)),
                   pl.BlockSpec((bs, HEAD_PAIRS, LANES, SSM_STATE), lambda i: (i, 0, 0, 0))),
        compiler_params=pltpu.CompilerParams(dimension_semantics=("parallel",),
                                             vmem_limit_bytes=VMEM_LIMIT),
        name="ssd_sample",
    )(xbc, z, dt, cprev_t, h0, w["conv_w"], w["conv_b"], w["a_log"], w["d_skip"], w["ssm_norm"])


def _pack_bf16_pair(v):
    c = v.shape[1] // 2
    hi = lax.bitcast_convert_type(v[:, :c].astype(BF16).astype(F32), jnp.uint32)
    lo = lax.bitcast_convert_type(v[:, c:].astype(BF16).astype(F32), jnp.uint32)
    return hi | (lo >> 16)


def _unpack_bf16_pair(word):
    a = lax.bitcast_convert_type(word & jnp.uint32(0xFFFF0000), F32)
    b = lax.bitcast_convert_type(word << 16, F32)
    return a, b


def _outproj_router_kernel(x_ref, a_ref, s_ref, wo_ref, n2_ref, wr_ref, br_ref, tri_ref,
                           h_ref, t_ref, route_ref, routet_ref, cnt_ref, carry_ref):
    @pl.when(pl.program_id(0) == 0)
    def _():
        carry_ref[...] = jnp.zeros(carry_ref.shape, F32)

    tm = x_ref.shape[0]
    h_ref[...] = (x_ref[...] + _dot(a_ref[...].astype(BF16), wo_ref[:ATTN_WIDTH, :])
                  + _dot(s_ref[...].astype(BF16), wo_ref[ATTN_WIDTH:, :]))
    sub = min(tm, ROUTER_SUBTILE)
    carry = carry_ref[:, 0:1]
    for s in range(tm // sub):
        rows = slice(s * sub, (s + 1) * sub)
        carry = _route_rows(h_ref[rows, :], n2_ref, wr_ref, br_ref, tri_ref, carry, t_ref.at[rows, :],
                            route_ref.at[rows, :], routet_ref.at[:, rows])
    carry_ref[...] = jnp.broadcast_to(carry, carry_ref.shape)
    cpad = jnp.concatenate([jnp.zeros((EXPERT_LANE0, LANES), F32), jnp.broadcast_to(carry, (N_EXPERTS, LANES)),
                            jnp.zeros((LANES - EXPERT_LANE0 - N_EXPERTS, LANES), F32)], axis=0)
    cnt_ref[...] = cpad.T[0:cnt_ref.shape[0], :]


def _route_rows(h, n2_ref, wrt_ref, brc_ref, triu_ref, carry, t_ref, route_ref, routet_ref):
    ms = jnp.mean(h * h, axis=-1, keepdims=True)
    t = h * lax.rsqrt(ms + EPS) * n2_ref[...]
    t_hi, t_lo = _split2(t)
    t_ref[...] = _pack_bf16_pair(t)
    n = h.shape[0]
    a = _dot_nt(wrt_ref[...], t_hi)
    b = _dot_nt(wrt_ref[0:LANES, :], t_lo)

    def logit_rows(r0, r1):
        return a[r0:r1] + a[LANES + r0:LANES + r1] + b[r0:r1] + brc_ref[r0:r1, :]

    sl = EXP_PER_GROUP
    glog = logit_rows(0, sl)
    elog = logit_rows(EXPERT_LANE0, EXPERT_LANE0 + N_EXPERTS)
    row = lax.broadcasted_iota(jnp.int32, (sl, n), 0).astype(F32)
    big = float(sl)
    ninf = -jnp.inf
    gm = jnp.where(row < N_EGROUPS, glog, ninf)
    gmax = jnp.max(gm, axis=0, keepdims=True)
    g_top = 1.0 / jnp.sum(jnp.exp(gm - gmax), axis=0, keepdims=True)
    g_idx = jnp.min(jnp.where(gm == gmax, row, big), axis=0, keepdims=True)
    ml = elog[(N_EGROUPS - 1) * sl:]
    for g in range(N_EGROUPS - 2, -1, -1):
        ml = jnp.where(g_idx == float(g), elog[g * sl:(g + 1) * sl], ml)
    m1 = jnp.max(ml, axis=0, keepdims=True)
    i1 = jnp.min(jnp.where(ml == m1, row, big), axis=0, keepdims=True)
    ml2 = jnp.where(row == i1, ninf, ml)
    m2 = jnp.max(ml2, axis=0, keepdims=True)
    i2 = jnp.min(jnp.where(ml2 == m2, row, big), axis=0, keepdims=True)
    r = jnp.exp(m2 - m1)
    w1 = g_top / (1.0 + r)
    w2 = g_top * r / (1.0 + r)
    e1 = g_idx * float(sl) + i1
    e2 = g_idx * float(sl) + i2
    pick = jnp.where(jnp.logical_or(row == i1, row == i2), 1.0, 0.0)
    onehot = jnp.concatenate([jnp.where(g_idx == float(g), pick, 0.0) for g in range(N_EGROUPS)], axis=0)
    onehot_bf = onehot.astype(BF16)
    cb = triu_ref.shape[0]
    cums = []
    for blk in range(n // cb):
        c = _dot(onehot_bf[:, blk * cb:(blk + 1) * cb], triu_ref[...]) + carry
        carry = c[:, cb - 1:cb]
        cums.append(c)
    before = jnp.concatenate(cums, axis=1) - onehot
    erow = lax.broadcasted_iota(jnp.int32, (N_EXPERTS, n), 0).astype(F32)
    rank1 = jnp.sum(jnp.where(erow == e1, before, 0.0), axis=0, keepdims=True)
    rank2 = jnp.sum(jnp.where(erow == e2, before, 0.0), axis=0, keepdims=True)
    fields = jnp.concatenate([e1, e2, w1, w2, rank1, rank2], axis=0)
    routet_ref[0:fields.shape[0], :] = fields
    routet_ref[fields.shape[0]:, :] = jnp.zeros((routet_ref.shape[0] - fields.shape[0], n), F32)
    if n % LANES:
        fields = jnp.concatenate([fields, jnp.zeros((fields.shape[0], LANES - n % LANES), F32)], axis=1)
    pad = jnp.zeros((LANES - fields.shape[0], LANES), F32)
    for j in range(fields.shape[1] // LANES):
        blk_rows = jnp.concatenate([fields[:, j * LANES:(j + 1) * LANES], pad], axis=0)
        n_valid = min(LANES, n - j * LANES)
        route_ref[j * LANES:j * LANES + n_valid, :] = blk_rows.T[:n_valid, :]
    return carry


def _outproj_router(x2d, attn, ssm, w, tm):
    t = x2d.shape[0]
    tok = lambda i: (i, 0)
    const = lambda i: (0, 0)
    full = lambda a: pl.BlockSpec(a.shape, const)
    idx = jnp.arange(min(tm, COUNT_BLOCK))
    tri = (idx[:, None] <= idx[None, :]).astype(BF16)
    return pl.pallas_call(
        _outproj_router_kernel,
        out_shape=(jax.ShapeDtypeStruct((t, D_MODEL), F32), jax.ShapeDtypeStruct((t, D_MODEL // 2), jnp.uint32),
                   jax.ShapeDtypeStruct((t, LANES), F32), jax.ShapeDtypeStruct((8, t), F32),
                   jax.ShapeDtypeStruct((8, LANES), F32)),
        grid=(t // tm,),
        in_specs=[
            pl.BlockSpec((tm, D_MODEL), tok), pl.BlockSpec((tm, ATTN_WIDTH), tok),
            pl.BlockSpec((tm, SSM_WIDTH), tok),
            full(w["w_out"]), full(w["norm2"]), full(w["wrt"]),
            full(w["br_col"]), full(tri),
        ],
        out_specs=(pl.BlockSpec((tm, D_MODEL), tok), pl.BlockSpec((tm, D_MODEL // 2), tok),
                   pl.BlockSpec((tm, LANES), tok), pl.BlockSpec((8, tm), lambda i: (0, i)),
                   pl.BlockSpec((8, LANES), const)),
        scratch_shapes=[pltpu.VMEM((N_EXPERTS, LANES), F32)],
        compiler_params=pltpu.CompilerParams(dimension_semantics=("arbitrary",),
                                             vmem_limit_bytes=VMEM_LIMIT),
        name="outproj_router",
    )(x2d, attn, ssm, w["w_out"], w["norm2"], w["wrt"], w["br_col"], tri)


def _expert_hidden(t_a, t_b, w1):
    half = D_MODEL // 2
    gu = _dot(t_a, w1[:half]) + _dot(t_b, w1[half:])
    return _silu(gu[:, :EXPERT_FF]) * gu[:, EXPERT_FF:]


def _gate_up_bf16(wg_ref, wu_ref):
    return jnp.concatenate([wg_ref[0].astype(BF16), wu_ref[0].astype(BF16)], axis=1)


def _moe_dense_kernel(h_ref, t_ref, route_ref, wg_ref, wu_ref, wd_ref, y_ref):
    e = pl.program_id(0)

    @pl.when(e == 0)
    def _():
        y_ref[...] = h_ref[...]

    t_a, t_b = _unpack_bf16_pair(t_ref[...])
    route = route_ref[...]
    e1, e2, g1, g2 = route[:, 0:1], route[:, 1:2], route[:, 2:3], route[:, 3:4]
    e_f = e.astype(F32)
    hid = _expert_hidden(t_a.astype(BF16), t_b.astype(BF16), _gate_up_bf16(wg_ref, wu_ref))
    c_e = jnp.where(e1 == e_f, g1, 0.0) + jnp.where(e2 == e_f, g2, 0.0)
    y_ref[...] += _dot((hid * c_e).astype(BF16), wd_ref[0].astype(BF16))


def _moe_dense(h, t, route, w):
    n = h.shape[0]
    whole = lambda e: (0, 0)
    by_expert = lambda e: (e, 0, 0)
    return pl.pallas_call(
        _moe_dense_kernel,
        out_shape=jax.ShapeDtypeStruct((n, D_MODEL), F32),
        grid=(N_EXPERTS,),
        in_specs=[pl.BlockSpec((n, D_MODEL), whole), pl.BlockSpec((n, D_MODEL // 2), whole),
                  pl.BlockSpec((n, LANES), whole),
                  pl.BlockSpec((1, D_MODEL, EXPERT_FF), by_expert), pl.BlockSpec((1, D_MODEL, EXPERT_FF), by_expert),
                  pl.BlockSpec((1, EXPERT_FF, D_MODEL), by_expert)],
        out_specs=pl.BlockSpec((n, D_MODEL), whole),
        compiler_params=pltpu.CompilerParams(dimension_semantics=("arbitrary",),
                                             vmem_limit_bytes=VMEM_LIMIT),
        name="moe_dense",
    )(h, t, route, w["w_gate"], w["w_up"], w["w_down"])


def _sc_scatter_rows(src, pos1, pos2, n_out):
    t, width = src.shape
    rows_per_worker = t // SC_WORKERS
    n_chunks = rows_per_worker // SC_CHUNK
    assert t == SC_WORKERS * SC_CHUNK * n_chunks
    mesh = plsc.VectorSubcoreMesh(core_axis_name="c", subcore_axis_name="s")

    @functools.partial(
        pl.kernel, mesh=mesh,
        out_type=jax.ShapeDtypeStruct((n_out, width), src.dtype),
        scratch_types=[pltpu.VMEM((SC_CHUNK,), jnp.int32), pltpu.VMEM((SC_CHUNK,), jnp.int32),
                       pltpu.VMEM((SC_CHUNK, width), src.dtype), pltpu.SemaphoreType.DMA],
    )
    def scatter_kernel(src_hbm, p1_hbm, p2_hbm, out_hbm, i1_v, i2_v, rows_v, sem):
        wid = lax.axis_index("s") * SC_CORES + lax.axis_index("c")
        base = wid * rows_per_worker

        @pl.loop(0, n_chunks)
        def _(c):
            off = pl.multiple_of(base + c * SC_CHUNK, 8)
            pltpu.sync_copy(p1_hbm.at[pl.ds(off, SC_CHUNK)], i1_v)
            pltpu.sync_copy(p2_hbm.at[pl.ds(off, SC_CHUNK)], i2_v)
            pltpu.sync_copy(src_hbm.at[pl.ds(off, SC_CHUNK)], rows_v)
            pltpu.async_copy(rows_v, out_hbm.at[i1_v], sem).wait()
            pltpu.async_copy(rows_v, out_hbm.at[i2_v], sem).wait()

    return scatter_kernel(src, pos1, pos2)


def _sc_gather_rows(table, idx):
    n, width = idx.shape[0], table.shape[1]
    rows_per_worker = n // SC_WORKERS
    n_chunks = rows_per_worker // SC_CHUNK
    assert n == SC_WORKERS * SC_CHUNK * n_chunks and n_chunks % 2 == 0
    mesh = plsc.VectorSubcoreMesh(core_axis_name="c", subcore_axis_name="s")

    @functools.partial(
        pl.kernel, mesh=mesh,
        out_type=jax.ShapeDtypeStruct((n, width), table.dtype),
        scratch_types=[pltpu.VMEM((2, SC_CHUNK), jnp.int32), pltpu.VMEM((2, SC_CHUNK, width), table.dtype),
                       pltpu.SemaphoreType.DMA, pltpu.SemaphoreType.DMA,
                       pltpu.SemaphoreType.DMA, pltpu.SemaphoreType.DMA],
    )
    def gather_kernel(table_hbm, idx_hbm, out_hbm, idx_v, rows_v, g0, g1, w0, w1):
        wid = lax.axis_index("s") * SC_CORES + lax.axis_index("c")
        base = wid * rows_per_worker
        gsem = (g0, g1)
        wsem = (w0, w1)

        def gather_copy(slot):
            return pltpu.make_async_copy(table_hbm.at[idx_v.at[slot]], rows_v.at[slot], gsem[slot])

        def write_copy(c, slot):
            off = pl.multiple_of(base + c * SC_CHUNK, 8)
            return pltpu.make_async_copy(rows_v.at[slot], out_hbm.at[pl.ds(off, SC_CHUNK)], wsem[slot])

        def start_gather(c, slot):
            off = pl.multiple_of(base + c * SC_CHUNK, 8)
            pltpu.sync_copy(idx_hbm.at[pl.ds(off, SC_CHUNK)], idx_v.at[slot])
            gather_copy(slot).start()

        start_gather(0, 0)

        @pl.loop(0, n_chunks, step=2)
        def _(c):
            @pl.when(c > 0)
            def _():
                write_copy(c - 1, 1).wait()

            start_gather(c + 1, 1)
            gather_copy(0).wait()
            write_copy(c, 0).start()
            gather_copy(1).wait()
            write_copy(c + 1, 1).start()
            write_copy(c, 0).wait()

            @pl.when(c + 2 < n_chunks)
            def _():
                start_gather(c + 2, 0)

        write_copy(n_chunks - 1, 1).wait()

    return gather_kernel(table, idx)


def _moe_grouped_kernel(te_ref, nt_ref, order_ref, x_ref, wg_ref, wu_ref, wd_ref, o_ref, w1_bf_ref, w2_bf_ref):
    del order_ref
    i = pl.program_id(0)

    @pl.when(jnp.logical_or(i == 0, te_ref[i] != te_ref[jnp.maximum(i - 1, 0)]))
    def _():
        w1_bf_ref[...] = _gate_up_bf16(wg_ref, wu_ref)
        w2_bf_ref[...] = wd_ref[0].astype(BF16)

    @pl.when(i < nt_ref[0])
    def _():
        for s in range(MOE_TILE // MOE_SUBTILE):
            rows = slice(s * MOE_SUBTILE, (s + 1) * MOE_SUBTILE)
            t_a, t_b = _unpack_bf16_pair(x_ref[rows, :])
            hid = _expert_hidden(t_a.astype(BF16), t_b.astype(BF16), w1_bf_ref[...])
            o_ref[rows, :] = _pack_bf16_pair(_dot(hid.astype(BF16), w2_bf_ref[...]))


def _moe_grouped(xs, tile_expert, n_tiles, order, w):
    rows = xs.shape[0]
    row = lambda i, te, nt, od: (i, 0)
    by_expert = lambda i, te, nt, od: (te[i], 0, 0)
    return pl.pallas_call(
        _moe_grouped_kernel,
        out_shape=jax.ShapeDtypeStruct((rows, D_MODEL // 2), jnp.uint32),
        grid_spec=pltpu.PrefetchScalarGridSpec(
            num_scalar_prefetch=3,
            grid=(rows // MOE_TILE,),
            in_specs=[pl.BlockSpec((MOE_TILE, D_MODEL // 2), row),
                      pl.BlockSpec((1, D_MODEL, EXPERT_FF), by_expert),
                      pl.BlockSpec((1, D_MODEL, EXPERT_FF), by_expert),
                      pl.BlockSpec((1, EXPERT_FF, D_MODEL), by_expert)],
            out_specs=pl.BlockSpec((MOE_TILE, D_MODEL // 2), row),
            scratch_shapes=[pltpu.VMEM((D_MODEL, 2 * EXPERT_FF), BF16), pltpu.VMEM((EXPERT_FF, D_MODEL), BF16)],
        ),
        compiler_params=pltpu.CompilerParams(dimension_semantics=("arbitrary",),
                                             vmem_limit_bytes=VMEM_LIMIT),
        name="moe_grouped",
    )(tile_expert, n_tiles, order, xs, w["w_gate"], w["w_up"], w["w_down"])


def _moe_combine_kernel(h_ref, z1_ref, z2_ref, route_ref, y_ref):
    route = route_ref[...]
    g1, g2 = route[:, 2:3], route[:, 3:4]
    half = D_MODEL // 2
    a1, b1 = _unpack_bf16_pair(z1_ref[...])
    a2, b2 = _unpack_bf16_pair(z2_ref[...])
    y_ref[:, :half] = h_ref[:, :half] + g1 * a1 + g2 * a2
    y_ref[:, half:] = h_ref[:, half:] + g1 * b1 + g2 * b2


def _moe_combine(h, z, route, tm):
    t = h.shape[0]
    nb = t // tm
    tok = lambda i: (i, 0)
    return pl.pallas_call(
        _moe_combine_kernel,
        out_shape=jax.ShapeDtypeStruct((t, D_MODEL), F32),
        grid=(nb,),
        in_specs=[pl.BlockSpec((tm, D_MODEL), tok), pl.BlockSpec((tm, D_MODEL // 2), tok),
                  pl.BlockSpec((tm, D_MODEL // 2), lambda i: (i + nb, 0)), pl.BlockSpec((tm, LANES), tok)],
        out_specs=pl.BlockSpec((tm, D_MODEL), tok),
        compiler_params=pltpu.CompilerParams(dimension_semantics=("parallel",),
                                             vmem_limit_bytes=VMEM_LIMIT),
        name="moe_combine",
    )(h, z, z, route)


def _route_pos_kernel(routet_ref, cnt_ref, upper_ref, pos_ref):
    tm = routet_ref.shape[1]
    cnt = cnt_ref[...]
    padded = jnp.floor((cnt + float(MOE_TILE - 1)) * (1.0 / MOE_TILE)) * float(MOE_TILE)
    p_hi, p_mid, p_lo = _split3(padded)
    upper = upper_ref[...]
    starts = (_dot(p_hi, upper) + _dot(p_mid, upper) + _dot(p_lo, upper))[0:1, :]
    starts_col = jnp.broadcast_to(starts, (LANES, LANES)).T[EXPERT_LANE0:EXPERT_LANE0 + N_EXPERTS, 0:1]
    erow = lax.broadcasted_iota(jnp.int32, (N_EXPERTS, tm), 0).astype(F32)
    for k in range(2):
        e_k = routet_ref[k:k + 1, :]
        pos = (jnp.sum(jnp.where(erow == e_k, starts_col, 0.0), axis=0, keepdims=True)
               + routet_ref[4 + k:5 + k, :]).astype(jnp.int32)
        for r in range(tm // LANES):
            pos_ref[k, r:r + 1, :] = pos[:, r * LANES:(r + 1) * LANES]


def _route_positions(routet, counts, tm):
    t = routet.shape[1]
    idx = jnp.arange(LANES)
    upper = (idx[:, None] < idx[None, :]).astype(BF16)
    return pl.pallas_call(
        _route_pos_kernel,
        out_shape=jax.ShapeDtypeStruct((2, t // LANES, LANES), jnp.int32),
        grid=(t // tm,),
        in_specs=[pl.BlockSpec((8, tm), lambda i: (0, i)), pl.BlockSpec((8, LANES), lambda i: (0, 0)),
                  pl.BlockSpec((LANES, LANES), lambda i: (0, 0))],
        out_specs=pl.BlockSpec((2, tm // LANES, LANES), lambda i: (0, i, 0)),
        compiler_params=pltpu.CompilerParams(dimension_semantics=("parallel",),
                                             vmem_limit_bytes=VMEM_LIMIT),
        name="route_positions",
    )(routet, counts, upper)


def _moe_routed(h, t_packed, route, routet, counts, w, tm, run_first):
    t = h.shape[0]
    pos = _route_positions(routet, counts, min(32 * LANES, t))
    pos1 = pos[0].reshape(t)
    pos2 = pos[1].reshape(t)
    cnt = counts[0, EXPERT_LANE0:EXPERT_LANE0 + N_EXPERTS].astype(jnp.int32)
    padded = (cnt + MOE_TILE - 1) // MOE_TILE * MOE_TILE
    ends = jnp.cumsum(padded)
    n_rows = 2 * t + N_EXPERTS * MOE_TILE
    n_tiles = ends[N_EXPERTS - 1] // MOE_TILE
    tile_start = jnp.arange(n_rows // MOE_TILE, dtype=jnp.int32) * MOE_TILE
    tile_start = jnp.minimum(tile_start, ends[N_EXPERTS - 1] - MOE_TILE)
    tile_expert = jnp.sum((tile_start[:, None] >= ends[None, :]).astype(jnp.int32), axis=1)
    xs = _sc_scatter_rows(t_packed, pos1, pos2, n_rows)
    order = lax.bitcast_convert_type(run_first.reshape(-1)[:1].astype(F32), jnp.int32)
    out = _moe_grouped(xs, tile_expert, n_tiles.reshape(1), order, w)
    z = _sc_gather_rows(out, pos.reshape(2 * t))
    return _moe_combine(h, z, route, tm)


def _pad_lanes(a, width=LANES):
    return jnp.pad(a, ((0, 0), (0, width - a.shape[1])))


def _prep_weights(norm1, w_in, q_norm, k_norm, conv_w, conv_b, dt_bias, a_log, d_skip, ssm_norm, w_out,
                  norm2, w_grp, b_grp, w_exp, b_exp, w_gate, w_up, w_down):
    w = {}
    w["norm1"] = norm1.reshape(1, D_MODEL)
    w["w_in"] = _pad_lanes(w_in, XBC_END + LANES).astype(BF16)
    w["qkn"] = jnp.concatenate([jnp.tile(q_norm, N_HEADS), jnp.tile(k_norm, KV_HEADS)]).reshape(1, QK_WIDTH)
    head_of_col = jnp.arange(QK_WIDTH) // HEAD_DIM
    red = (head_of_col[:, None] == jnp.arange(LANES)[None, :])
    w["red"] = red.astype(BF16)
    w["exp"] = red.T.astype(BF16)
    w["conv_w"] = conv_w
    w["conv_b"] = conv_b.reshape(1, CONV_DIM)
    w["dt_bias"] = _pad_lanes(dt_bias.reshape(1, SSM_HEADS))
    w["a_log"] = _pad_lanes(a_log.reshape(1, SSM_HEADS))
    w["d_skip"] = jnp.repeat(d_skip, SSM_HEAD_DIM).reshape(1, SSM_WIDTH)
    w["ssm_norm"] = ssm_norm.reshape(1, SSM_WIDTH)
    idx = jnp.arange(SSD_CHUNK)
    w["tri"] = (idx[None, :] <= idx[:, None]).astype(BF16)
    lane_head = jnp.arange(SSM_WIDTH) // SSM_HEAD_DIM
    w["expand"] = (jnp.arange(LANES)[:, None] == lane_head[None, :]).astype(BF16)
    w["w_out"] = w_out.astype(BF16)
    w["norm2"] = norm2.reshape(1, D_MODEL)
    wr = jnp.zeros((D_MODEL, LANES), F32)
    wr = wr.at[:, :N_EGROUPS].set(w_grp).at[:, EXPERT_LANE0:EXPERT_LANE0 + N_EXPERTS].set(w_exp)
    wr_hi = wr.astype(BF16)
    w["wrt"] = jnp.concatenate([wr_hi, (wr - wr_hi.astype(F32)).astype(BF16)], axis=1).T
    br = jnp.zeros((LANES, 1), F32)
    w["br_col"] = br.at[:N_EGROUPS, 0].set(b_grp).at[EXPERT_LANE0:EXPERT_LANE0 + N_EXPERTS, 0].set(b_exp)
    w["w_gate"], w["w_up"], w["w_down"] = w_gate, w_up, w_down
    return w


def _rope_tables(pos):
    inv = 1.0 / (ROPE_THETA ** (jnp.arange(0, HEAD_DIM, 2, dtype=F32) / HEAD_DIM))
    ang = pos.astype(F32)[:, None] * inv[None, :]
    cos, sin = jnp.cos(ang), jnp.sin(ang)
    reps = LANES // HEAD_DIM
    return (jnp.tile(jnp.concatenate([cos, cos], axis=-1), (1, reps)),
            jnp.tile(jnp.concatenate([-sin, sin], axis=-1), (1, reps)))


def _token_tile(t):
    for tm in (1024, 512, 256, 128, 64, 32, 16):
        if t % tm == 0:
            return tm
    raise ValueError(f"token count {t} is not a multiple of 16")


def kernel(x_prompt, x_sample, cache_win_k, cache_win_v, state_conv, state_ssm, norm1, w_in, q_norm, k_norm,
           sinks, conv_w, conv_b, dt_bias, a_log, d_skip, ssm_norm, w_out, norm2, w_grp, b_grp, w_exp, b_exp,
           w_gate, w_up, w_down):
    depth = norm1.shape[0]
    assert depth == 1, "single-layer stack"
    bp, lp, _ = x_prompt.shape
    bsn, ls, _ = x_sample.shape
    assert ls == 1 and lp % WINDOW == 0 and cache_win_k.shape[2] == WINDOW
    l = 0
    w = _prep_weights(norm1[l], w_in[l], q_norm[l], k_norm[l], conv_w[l], conv_b[l], dt_bias[l], a_log[l],
                      d_skip[l], ssm_norm[l], w_out[l], norm2[l], w_grp[l], b_grp[l], w_exp[l], b_exp[l],
                      w_gate[l], w_up[l], w_down[l])
    sink = sinks[l]

    tp = bp * lp
    xp = x_prompt.reshape(tp, D_MODEL)
    tm_p = _token_tile(lp)
    cos_p, sin_p = _rope_tables(jnp.arange(lp, dtype=jnp.int32))
    q, k, v, z, xbc, dt, k2, v2 = _inproj(xp, w, cos_p, sin_p, tm_p, lp // tm_p)
    attn = _attn_prompt(q, k2, v2, sink, bp, lp)
    ssm, st_p = _ssd_prompt(xbc, z, dt, w, bp, lp)
    h, t, route, routet, counts = _outproj_router(xp, attn, ssm, w, tm_p)
    k3 = k.reshape(bp, lp, KV_HEADS, HEAD_DIM)
    v3 = v.reshape(bp, lp, KV_HEADS, HEAD_DIM)
    win_k_p = k3[:, lp - WINDOW:][None]
    win_v_p = v3[:, lp - WINDOW:][None]
    conv_p = xbc.reshape(bp, lp, CONV_DIM)[:, lp - (CONV_K - 1):][None]
    ssm_p = st_p.reshape(1, bp, SSM_HEADS, SSM_HEAD_DIM, SSM_STATE)

    xs2 = x_sample.reshape(bsn, D_MODEL)
    tm_s = _token_tile(bsn)
    cos_s, sin_s = _rope_tables(jnp.full((tm_s,), PAST_LEN, jnp.int32))
    q_s, k_s, v_s, z_s, xbc_s, dt_s, _, _ = _inproj(xs2, w, cos_s, sin_s, tm_s, 1)
    q4 = q_s.reshape(bsn, KV_HEADS, N_HEADS // KV_HEADS, HEAD_DIM)
    zq = jnp.zeros_like(q4[:, 0])
    qx = jnp.concatenate([jnp.concatenate([q4[:, 0], zq], axis=-1),
                          jnp.concatenate([zq, q4[:, 1]], axis=-1)], axis=1)
    qx = jnp.pad(qx, ((0, 0), (0, BF16_ROWS - N_HEADS), (0, 0)))
    sink_x = jnp.pad(jnp.broadcast_to(sink[:, None], (N_HEADS, LANES)), ((0, BF16_ROWS - N_HEADS), (0, 0)))
    kc = cache_win_k[l].reshape(bsn, WINDOW, KV_WIDTH)
    vc = cache_win_v[l].reshape(bsn, WINDOW, KV_WIDTH)
    ko, vo, attn_s = _attn_sample(qx, kc, k_s.reshape(bsn, 1, KV_WIDTH), vc, v_s.reshape(bsn, 1, KV_WIDTH),
                               sink_x, 8)
    y_prompt = _moe_routed(h, t, route, routet, counts, w, tm_p, attn_s).reshape(bp, lp, D_MODEL)
    cprev_t = jnp.transpose(state_conv[l], (1, 0, 2))
    h0 = state_ssm[l].reshape(bsn, HEAD_PAIRS, LANES, SSM_STATE)
    ssm_s, cnew_t, h1 = _ssd_sample(xbc_s, z_s, dt_s, cprev_t, h0, w, 16)
    h_s, t_s, route_s, _, _ = _outproj_router(xs2, attn_s, ssm_s, w, tm_s)
    y_sample = _moe_dense(h_s, t_s, route_s, w).reshape(bsn, 1, D_MODEL)
    win_k_s = ko.reshape(1, bsn, WINDOW, KV_HEADS, HEAD_DIM)
    win_v_s = vo.reshape(1, bsn, WINDOW, KV_HEADS, HEAD_DIM)
    conv_s = jnp.transpose(cnew_t, (1, 0, 2))[None]
    ssm_s_state = h1.reshape(1, bsn, SSM_HEADS, SSM_HEAD_DIM, SSM_STATE)

    return (y_prompt, y_sample, win_k_p, win_v_p, conv_p, ssm_p, win_k_s, win_v_s, conv_s, ssm_s_state)
```

```python
import functools
import math

import jax
import jax.numpy as jnp
from jax import lax
from jax.experimental import pallas as pl
from jax.experimental.pallas import tpu as pltpu
from jax.experimental.pallas import tpu_sc as plsc

F32 = jnp.float32
BF16 = jnp.bfloat16

D_MODEL = 1024
HEAD_DIM = 64
N_HEADS = 8
KV_HEADS = 2
WINDOW = 128
ATTN_WIDTH = N_HEADS * HEAD_DIM
QK_WIDTH = ATTN_WIDTH + KV_HEADS * HEAD_DIM
KV_WIDTH = KV_HEADS * HEAD_DIM
ATTN_SCALE = HEAD_DIM ** -0.5
ROPE_THETA = 10000.0
SSM_WIDTH = 512
SSM_HEADS = 8
SSM_HEAD_DIM = 64
SSM_GROUPS = 2
SSM_STATE = 128
CONV_K = 4
CONV_HALO = 8
CONV_DIM = SSM_WIDTH + 2 * SSM_GROUPS * SSM_STATE
SSD_CHUNK = 128
N_EGROUPS = 4
EXP_PER_GROUP = 8
N_EXPERTS = 32
EXPERT_FF = 128
EPS = 1e-6
PAST_LEN = 16384

LANES = 128
BF16_ROWS = 16
HEAD_PAIRS = SSM_HEADS // 2
EXPERT_LANE0 = 32
VMEM_LIMIT = 56 * 1024 * 1024
MOE_TILE = 1024
MOE_SUBTILE = 128
ATTN_QBLOCKS = 8
SSD_CHUNKS_PER_STEP = 8
COUNT_BLOCK = 256
ROUTER_SUBTILE = 1024
INPROJ_SUBTILE = 512
SC_CORES = 2
SC_SUBCORES = 16
SC_WORKERS = SC_CORES * SC_SUBCORES
SC_CHUNK = 64

Q_END = ATTN_WIDTH
K_END = Q_END + KV_WIDTH
V_END = K_END + KV_WIDTH
Z_END = V_END + SSM_WIDTH
XBC_END = Z_END + CONV_DIM


def _dot(a, b):
    return jnp.dot(a, b, preferred_element_type=F32)


def _dot_nt(a, b):
    return lax.dot_general(a, b, (((1,), (1,)), ((), ())), preferred_element_type=F32)


def _split2(v):
    hi = v.astype(BF16)
    lo = (v - hi.astype(F32)).astype(BF16)
    return hi, lo


def _split3(v):
    hi = v.astype(BF16)
    r = v - hi.astype(F32)
    mid = r.astype(BF16)
    lo = (r - mid.astype(F32)).astype(BF16)
    return hi, mid, lo


def _silu(x):
    return x * jax.nn.sigmoid(x)


def _softplus(x):
    return jnp.maximum(x, 0.0) + jnp.log1p(jnp.exp(-jnp.abs(x)))


def _lane_bcast_pairs(v, n_pairs):
    r = v.shape[0]
    lo = lax.broadcasted_iota(jnp.int32, (r, LANES), 1) < HEAD_DIM
    slabs = []
    for j in range(n_pairs):
        a = jnp.broadcast_to(v[:, 2 * j:2 * j + 1], (r, LANES))
        b = jnp.broadcast_to(v[:, 2 * j + 1:2 * j + 2], (r, LANES))
        slabs.append(jnp.where(lo, a, b))
    return jnp.concatenate(slabs, axis=1)


def _causal_conv_silu(x_ext, convw_ref, convb_ref):
    halo = CONV_HALO
    x_raw = x_ext[halo:, :]
    conv = convb_ref[...] + x_raw * convw_ref[CONV_K - 1:CONV_K, :]
    for j in range(CONV_K - 1):
        shifted = pltpu.roll(x_ext, CONV_K - 1 - j, axis=0)[halo:, :]
        conv = conv + shifted * convw_ref[j:j + 1, :]
    return _silu(conv)


def _inproj_kernel(x_ref, n1_ref, win_ref, dtb_ref, qkn_ref,
                   cos_ref, sin_ref, red_ref, exp_ref,
                   q_ref, k_ref, v_ref, z_ref, xbc_ref, dt_ref, k2_ref, v2_ref):
    tm = x_ref.shape[0]
    sub = min(tm, INPROJ_SUBTILE)
    lane = lax.broadcasted_iota(jnp.int32, (sub, LANES), 1)
    first_half = (lane % HEAD_DIM) < (HEAD_DIM // 2)
    for s in range(tm // sub):
        rows = slice(s * sub, (s + 1) * sub)
        x = x_ref[rows, :]
        ms = jnp.mean(x * x, axis=-1, keepdims=True)
        xn = (x * lax.rsqrt(ms + EPS) * n1_ref[...]).astype(BF16)
        v = _dot(xn, win_ref[:, K_END:V_END])
        v_ref[rows, :] = v
        v2_ref[rows, :] = _pair_operands(v)
        z_ref[rows, :] = _dot(xn, win_ref[:, V_END:Z_END])
        xbc_ref[rows, :] = _dot(xn, win_ref[:, Z_END:XBC_END])
        dt_ref[rows, :] = _softplus(_dot(xn, win_ref[:, XBC_END:]) + dtb_ref[...])
        qk = _dot(xn, win_ref[:, :K_END])
        ss = _dot((qk * qk).astype(BF16), red_ref[...])
        inv = lax.rsqrt(ss * (1.0 / HEAD_DIM) + EPS)
        inv_hi, inv_lo = _split2(inv)
        inv_x = _dot(inv_hi, exp_ref[...]) + _dot(inv_lo, exp_ref[...])
        qkn = qk * inv_x * qkn_ref[...]
        cos = cos_ref[rows, :]
        sin = sin_ref[rows, :]
        for c in range(QK_WIDTH // LANES):
            xc = qkn[:, c * LANES:(c + 1) * LANES]
            partner = jnp.where(first_half,
                                pltpu.roll(xc, LANES - HEAD_DIM // 2, axis=1),
                                pltpu.roll(xc, HEAD_DIM // 2, axis=1))
            rot = xc * cos + partner * sin
            if c < ATTN_WIDTH // LANES:
                q_ref[rows, c * LANES:(c + 1) * LANES] = (rot * ATTN_SCALE).astype(BF16)
            else:
                k_ref[rows, :] = rot
                k2_ref[rows, :] = _pair_operands(rot)


def _inproj(x2d, w, cos_tab, sin_tab, tm, n_pos_blocks):
    t = x2d.shape[0]
    grid = (t // tm,)
    tok = lambda i: (i, 0)
    const = lambda i: (0, 0)
    pos = lambda i: (i % n_pos_blocks, 0)
    full = lambda a: pl.BlockSpec(a.shape, const)
    rows = lambda width, dtype: (jax.ShapeDtypeStruct((t, width), dtype), pl.BlockSpec((tm, width), tok))
    outs = [rows(ATTN_WIDTH, BF16), rows(KV_WIDTH, F32), rows(KV_WIDTH, F32), rows(SSM_WIDTH, F32)]
    operands = [x2d, w["norm1"], w["w_in"], w["dt_bias"], w["qkn"],
                cos_tab, sin_tab, w["red"], w["exp"]]
    in_specs = [
        pl.BlockSpec((tm, D_MODEL), tok),
        full(w["norm1"]), full(w["w_in"]), full(w["dt_bias"]), full(w["qkn"]),
        pl.BlockSpec((tm, LANES), pos), pl.BlockSpec((tm, LANES), pos),
        full(w["red"]), full(w["exp"]),
    ]
    outs += [rows(CONV_DIM, F32), rows(LANES, F32), rows(4 * LANES, BF16), rows(4 * LANES, BF16)]
    return pl.pallas_call(
        _inproj_kernel,
        out_shape=tuple(o[0] for o in outs),
        grid=grid,
        in_specs=in_specs,
        out_specs=tuple(o[1] for o in outs),
        compiler_params=pltpu.CompilerParams(dimension_semantics=("parallel",),
                                             vmem_limit_bytes=VMEM_LIMIT),
        name="inproj",
    )(*operands)


def _pair_operands(kv):
    lo = lax.broadcasted_iota(jnp.int32, kv.shape, 1) < HEAD_DIM
    swapped = pltpu.roll(kv, HEAD_DIM, axis=1)
    parts = [jnp.where(lo, kv, 0.0), jnp.where(lo, 0.0, swapped), jnp.where(lo, swapped, 0.0), jnp.where(lo, 0.0, kv)]
    return jnp.concatenate(parts, axis=1).astype(BF16)


def _pair_rhs(blk, g):
    return jnp.concatenate([blk[:, 2 * g * LANES:(2 * g + 1) * LANES],
                            blk[:, (2 * g + 1) * LANES:(2 * g + 2) * LANES]], axis=0)


def _attn_qblock(sink_ref, q_blk, k_prev, k_cur, v_prev, v_cur, seq_start, o_ref):
    blk = WINDOW
    qi = lax.broadcasted_iota(jnp.int32, (blk, 2 * blk), 0)
    kj = lax.broadcasted_iota(jnp.int32, (blk, 2 * blk), 1) % blk
    cur_ok = kj <= qi
    lo = lax.broadcasted_iota(jnp.int32, (blk, LANES), 1) < HEAD_DIM
    n_pairs = N_HEADS // KV_HEADS // 2
    for g in range(KV_HEADS):
        q_all = jnp.concatenate([q_blk[:, (g * n_pairs + r) * LANES:(g * n_pairs + r + 1) * LANES]
                                 for r in range(n_pairs)], axis=0)
        s_all = _dot_nt(q_all, jnp.concatenate([_pair_rhs(k_cur, g), _pair_rhs(k_prev, g)], axis=0))
        p_rows = []
        den_rows = []
        for r in range(n_pairs):
            pair = g * n_pairs + r
            s_cur = s_all[r * blk:(r + 1) * blk, :2 * blk]
            s_prev = s_all[r * blk:(r + 1) * blk, 2 * blk:]
            s = jnp.where(cur_ok, s_cur, s_prev)
            if seq_start is not None:
                s = jnp.where(jnp.logical_or(cur_ok, jnp.logical_not(seq_start)), s, -jnp.inf)
            ps = []
            dens = []
            for hh in range(2):
                sink = sink_ref[2 * pair + hh]
                sh = s[:, hh * blk:(hh + 1) * blk]
                m = jnp.maximum(jnp.max(sh, axis=-1, keepdims=True), sink)
                p = jnp.exp(sh - m)
                dens.append(jnp.sum(p, axis=-1, keepdims=True) + jnp.exp(sink - m))
                ps.append(p)
            p2 = jnp.concatenate(ps, axis=1)
            p_rows.append(jnp.concatenate([jnp.where(cur_ok, p2, 0.0), jnp.where(cur_ok, 0.0, p2)],
                                          axis=1).astype(BF16))
            den_rows.append(jnp.where(lo, dens[0], dens[1]))
        o_all = _dot(jnp.concatenate(p_rows, axis=0),
                     jnp.concatenate([_pair_rhs(v_cur, g), _pair_rhs(v_prev, g)], axis=0))
        for r in range(n_pairs):
            pair = g * n_pairs + r
            o2 = o_all[r * blk:(r + 1) * blk, :]
            o_ref[:, pair * LANES:(pair + 1) * LANES] = (o2 / den_rows[r]).astype(BF16)


def _attn_kernel(sink_ref, q_ref, kc_ref, kp_ref, vc_ref, vp_ref, o_ref):
    blk = WINDOW
    first_step = pl.program_id(1) == 0
    for u in range(q_ref.shape[0] // blk):
        rows = slice(u * blk, (u + 1) * blk)
        prev_rows = slice((u - 1) * blk, u * blk)
        k_prev = kp_ref[...] if u == 0 else kc_ref[prev_rows, :]
        v_prev = vp_ref[...] if u == 0 else vc_ref[prev_rows, :]
        _attn_qblock(sink_ref, q_ref[rows, :], k_prev, kc_ref[rows, :], v_prev, vc_ref[rows, :],
                     first_step if u == 0 else None, o_ref.at[rows, :])


def _attn_prompt(q, k, v, sinks, batch, seq):
    n_sub = ATTN_QBLOCKS if seq % (ATTN_QBLOCKS * WINDOW) == 0 else 1
    rows = n_sub * WINDOW
    nb = seq // rows
    cur = lambda b, j, s: (b * nb + j, 0)
    prev = lambda b, j, s: (jnp.maximum((b * nb + j) * n_sub - 1, 0), 0)
    return pl.pallas_call(
        _attn_kernel,
        out_shape=jax.ShapeDtypeStruct((batch * seq, ATTN_WIDTH), BF16),
        grid_spec=pltpu.PrefetchScalarGridSpec(
            num_scalar_prefetch=1,
            grid=(batch, nb),
            in_specs=[
                pl.BlockSpec((rows, ATTN_WIDTH), cur),
                pl.BlockSpec((rows, 4 * LANES), cur), pl.BlockSpec((WINDOW, 4 * LANES), prev),
                pl.BlockSpec((rows, 4 * LANES), cur), pl.BlockSpec((WINDOW, 4 * LANES), prev),
            ],
            out_specs=pl.BlockSpec((rows, ATTN_WIDTH), cur),
        ),
        compiler_params=pltpu.CompilerParams(dimension_semantics=("parallel", "parallel"),
                                             vmem_limit_bytes=VMEM_LIMIT),
        name="attn_prompt",
    )(sinks, q, k, k, v, v)


def _ssd_kernel(xbc_ref, z_ref, dt_ref, convw_ref, convb_ref, alog_ref, dskip_ref, nw_ref,
                tri_ref, expand_ref, y_ref, st_ref, buf_ref, state_ref):
    c = pl.program_id(1)
    cl = SSD_CHUNK
    n_sub = xbc_ref.shape[0] // cl
    halo = CONV_HALO

    @pl.when(c == 0)
    def _():
        buf_ref[...] = jnp.zeros(buf_ref.shape, F32)
        state_ref[...] = jnp.zeros(state_ref.shape, F32)

    lane = lax.broadcasted_iota(jnp.int32, (1, LANES), 1)
    a_neg = jnp.where(lane < SSM_HEADS, -jnp.exp(alog_ref[...]), 0.0)
    tri = tri_ref[...]
    for u in range(n_sub):
        rows = slice(u * cl, (u + 1) * cl)
        if u == 0:
            x_ext = jnp.concatenate([buf_ref[...], xbc_ref[rows, :]], axis=0)
        else:
            x_ext = xbc_ref[u * cl - halo:(u + 1) * cl, :]
        _ssd_chunk(x_ext, z_ref[rows, :], dt_ref[rows, :], a_neg, tri, convw_ref, convb_ref, dskip_ref, nw_ref,
                   expand_ref, y_ref.at[rows, :], state_ref)
    buf_ref[...] = xbc_ref[n_sub * cl - halo:n_sub * cl, :]

    @pl.when(c == pl.num_programs(1) - 1)
    def _():
        st_ref[0] = state_ref[...]


def _ssd_chunk(x_ext, z, dt, a_neg, tri, convw_ref, convb_ref, dskip_ref, nw_ref, expand_ref, y_ref, state_ref):
    cl = SSD_CHUNK
    act = _causal_conv_silu(x_ext, convw_ref, convb_ref)
    xs = act[:, :SSM_WIDTH]
    bm = act[:, SSM_WIDTH:SSM_WIDTH + SSM_GROUPS * SSM_STATE].astype(BF16)
    cm = act[:, SSM_WIDTH + SSM_GROUPS * SSM_STATE:].astype(BF16)

    dta = dt * a_neg
    p_hi, p_mid, p_lo = _split3(dta)
    a_col = _dot(tri, p_hi) + _dot(tri, p_mid) + _dot(tri, p_lo)
    a_last = a_col[cl - 1:cl, :]
    a_row = a_col.T
    per_head = jnp.concatenate([dt, jnp.exp(a_col), jnp.exp(a_last - a_col)], axis=0)
    ph_hi, ph_lo = _split2(per_head)
    per_lane = _dot(ph_hi, expand_ref[...]) + _dot(ph_lo, expand_ref[...])
    dt_x = per_lane[:cl]
    ecol_x = per_lane[cl:2 * cl]
    dte_x = per_lane[2 * cl:]
    e_last = jnp.exp(a_last)
    xdt = xs * dt_x

    li = lax.broadcasted_iota(jnp.int32, (cl, cl), 0)
    si = lax.broadcasted_iota(jnp.int32, (cl, cl), 1)
    causal = si <= li
    lo = lax.broadcasted_iota(jnp.int32, (cl, LANES), 1) < SSM_HEAD_DIM
    row_lo = lax.broadcasted_iota(jnp.int32, (LANES, SSM_STATE), 0) < SSM_HEAD_DIM

    ys = []
    for g in range(SSM_GROUPS):
        b_g = bm[:, g * SSM_STATE:(g + 1) * SSM_STATE]
        c_g = cm[:, g * SSM_STATE:(g + 1) * SSM_STATE]
        cb = _dot_nt(c_g, b_g)
        for r in range(HEAD_PAIRS // SSM_GROUPS):
            j = g * (HEAD_PAIRS // SSM_GROUPS) + r
            sl = slice(j * LANES, (j + 1) * LANES)
            xdt_p = xdt[:, sl]
            ms = []
            for hh in range(2):
                h = 2 * j + hh
                seg = a_col[:, h:h + 1] - a_row[h:h + 1, :]
                ms.append(cb * jnp.exp(jnp.where(causal, seg, -jnp.inf)))
            m2 = jnp.concatenate(ms, axis=1).astype(BF16)
            rhs = jnp.concatenate([jnp.where(lo, xdt_p, 0.0), jnp.where(lo, 0.0, xdt_p)],
                                  axis=0).astype(BF16)
            y_diag = _dot(m2, rhs)
            st = state_ref[j]
            y_off = _dot_nt(c_g, st.astype(BF16)) * ecol_x[:, sl]
            xdt_e = (xdt_p * dte_x[:, sl]).T.astype(BF16)
            d_a = e_last[:, 2 * j:2 * j + 1]
            d_b = e_last[:, 2 * j + 1:2 * j + 2]
            decay = jnp.where(row_lo, jnp.broadcast_to(d_a, row_lo.shape), jnp.broadcast_to(d_b, row_lo.shape))
            state_ref[j] = decay * st + _dot(xdt_e, b_g)
            ys.append(y_diag + y_off + dskip_ref[:, sl] * xs[:, sl])
    y = jnp.concatenate(ys, axis=1)
    gated = y * _silu(z)
    gw = SSM_WIDTH // SSM_GROUPS
    outs = []
    for g in range(SSM_GROUPS):
        gg = gated[:, g * gw:(g + 1) * gw]
        outs.append(gg * lax.rsqrt(jnp.mean(gg * gg, axis=-1, keepdims=True) + EPS))
    y_ref[...] = (jnp.concatenate(outs, axis=1) * nw_ref[...]).astype(BF16)


def _ssd_prompt(xbc, z, dt, w, batch, seq):
    n_sub = SSD_CHUNKS_PER_STEP if seq % (SSD_CHUNKS_PER_STEP * SSD_CHUNK) == 0 else 1
    rows = n_sub * SSD_CHUNK
    nc = seq // rows
    tok = lambda b, c: (b * nc + c, 0)
    const = lambda b, c: (0, 0)
    full = lambda a: pl.BlockSpec(a.shape, const)
    return pl.pallas_call(
        _ssd_kernel,
        out_shape=(jax.ShapeDtypeStruct((batch * seq, SSM_WIDTH), BF16),
                   jax.ShapeDtypeStruct((batch, HEAD_PAIRS, LANES, SSM_STATE), F32)),
        grid=(batch, nc),
        in_specs=[
            pl.BlockSpec((rows, CONV_DIM), tok), pl.BlockSpec((rows, SSM_WIDTH), tok),
            pl.BlockSpec((rows, LANES), tok),
            full(w["conv_w"]), full(w["conv_b"]), full(w["a_log"]),
            full(w["d_skip"]), full(w["ssm_norm"]), full(w["tri"]), full(w["expand"]),
        ],
        out_specs=(pl.BlockSpec((rows, SSM_WIDTH), tok),
                   pl.BlockSpec((1, HEAD_PAIRS, LANES, SSM_STATE), lambda b, c: (b, 0, 0, 0))),
        scratch_shapes=[pltpu.VMEM((CONV_HALO, CONV_DIM), F32),
                        pltpu.VMEM((HEAD_PAIRS, LANES, SSM_STATE), F32)],
        compiler_params=pltpu.CompilerParams(dimension_semantics=("parallel", "arbitrary"),
                                             vmem_limit_bytes=VMEM_LIMIT),
        name="ssd_prompt",
    )(xbc, z, dt, w["conv_w"], w["conv_b"], w["a_log"], w["d_skip"], w["ssm_norm"], w["tri"], w["expand"])


def _attn_sample_kernel(qx_ref, kc_ref, kn_ref, vc_ref, vn_ref, sink_ref, ko_ref, vo_ref, o_ref):
    bs = qx_ref.shape[0]
    w = kc_ref.shape[1]
    sink = sink_ref[...]
    lo = lax.broadcasted_iota(jnp.int32, (1, LANES), 1) < HEAD_DIM
    for i in range(bs):
        ko_ref[i, 0:w - 1, :] = kc_ref[i, 1:w, :]
        ko_ref[i, w - 1:w, :] = kn_ref[i]
        vo_ref[i, 0:w - 1, :] = vc_ref[i, 1:w, :]
        vo_ref[i, w - 1:w, :] = vn_ref[i]
        s = _dot_nt(qx_ref[i], ko_ref[i].astype(BF16))
        m = jnp.maximum(jnp.max(s, axis=-1, keepdims=True), sink)
        p = jnp.exp(s - m)
        den = jnp.sum(p, axis=-1, keepdims=True) + jnp.exp(sink - m)
        o = _dot(p.astype(BF16), vo_ref[i].astype(BF16)) / den
        o_sw = pltpu.roll(o, HEAD_DIM, axis=1)
        for j in range(N_HEADS // 2):
            a, b = (o, o_sw) if j < N_HEADS // 4 else (o_sw, o)
            o_ref[i:i + 1, j * LANES:(j + 1) * LANES] = jnp.where(lo, a[2 * j:2 * j + 1], b[2 * j + 1:2 * j + 2])


def _attn_sample(qx, kc, kn, vc, vn, sink_x, bs):
    n, w = kc.shape[0], kc.shape[1]
    blk3 = lambda i: (i, 0, 0)
    return pl.pallas_call(
        _attn_sample_kernel,
        out_shape=(jax.ShapeDtypeStruct((n, w, KV_WIDTH), F32),
                   jax.ShapeDtypeStruct((n, w, KV_WIDTH), F32),
                   jax.ShapeDtypeStruct((n, ATTN_WIDTH), F32)),
        grid=(n // bs,),
        in_specs=[
            pl.BlockSpec((bs, BF16_ROWS, LANES), blk3),
            pl.BlockSpec((bs, w, KV_WIDTH), blk3), pl.BlockSpec((bs, 1, KV_WIDTH), blk3),
            pl.BlockSpec((bs, w, KV_WIDTH), blk3), pl.BlockSpec((bs, 1, KV_WIDTH), blk3),
            pl.BlockSpec(sink_x.shape, lambda i: (0, 0)),
        ],
        out_specs=(pl.BlockSpec((bs, w, KV_WIDTH), blk3), pl.BlockSpec((bs, w, KV_WIDTH), blk3),
                   pl.BlockSpec((bs, ATTN_WIDTH), lambda i: (i, 0))),
        compiler_params=pltpu.CompilerParams(dimension_semantics=("parallel",),
                                             vmem_limit_bytes=VMEM_LIMIT),
        name="attn_sample",
    )(qx, kc, kn, vc, vn, sink_x)


def _ssd_sample_kernel(xbc_ref, z_ref, dt_ref, cprev_ref, h0_ref, convw_ref, convb_ref, alog_ref,
                       dskip_ref, nw_ref, y_ref, cnew_ref, h1_ref):
    bs = xbc_ref.shape[0]
    x_raw = xbc_ref[...]
    conv = convb_ref[...] + x_raw * convw_ref[CONV_K - 1:CONV_K, :]
    for j in range(CONV_K - 1):
        conv = conv + cprev_ref[j] * convw_ref[j:j + 1, :]
    for j in range(CONV_K - 2):
        cnew_ref[j] = cprev_ref[j + 1]
    cnew_ref[CONV_K - 2] = x_raw
    act = _silu(conv)
    xs = act[:, :SSM_WIDTH]
    bm = act[:, SSM_WIDTH:SSM_WIDTH + SSM_GROUPS * SSM_STATE].astype(BF16)
    cm = act[:, SSM_WIDTH + SSM_GROUPS * SSM_STATE:].astype(BF16)
    lane = lax.broadcasted_iota(jnp.int32, (1, LANES), 1)
    a_neg = jnp.where(lane < SSM_HEADS, -jnp.exp(alog_ref[...]), 0.0)
    dt = dt_ref[...]
    dec = jnp.exp(dt * a_neg)
    xdt = xs * _lane_bcast_pairs(dt, HEAD_PAIRS)
    rowid = lax.broadcasted_iota(jnp.int32, (bs, LANES), 0)
    row_lo = lax.broadcasted_iota(jnp.int32, (LANES, SSM_STATE), 0) < SSM_HEAD_DIM
    ys = []
    for j in range(HEAD_PAIRS):
        g = j // (HEAD_PAIRS // SSM_GROUPS)
        sl = slice(j * LANES, (j + 1) * LANES)
        b_g = bm[:, g * SSM_STATE:(g + 1) * SSM_STATE]
        c_g = cm[:, g * SSM_STATE:(g + 1) * SSM_STATE]
        xdt_p = xdt[:, sl]
        y_p = jnp.zeros((bs, LANES), F32)
        for i in range(bs):
            xi = jnp.where(rowid == i, xdt_p, 0.0).T.astype(BF16)
            d_a = dec[i:i + 1, 2 * j:2 * j + 1]
            d_b = dec[i:i + 1, 2 * j + 1:2 * j + 2]
            decay = jnp.where(row_lo, jnp.broadcast_to(d_a, row_lo.shape), jnp.broadcast_to(d_b, row_lo.shape))
            new = decay * h0_ref[i, j] + _dot(xi, b_g)
            h1_ref[i, j] = new
            y_p = y_p + jnp.where(rowid == i, _dot_nt(c_g, new.astype(BF16)), 0.0)
        ys.append(y_p + dskip_ref[:, sl] * xs[:, sl])
    y = jnp.concatenate(ys, axis=1)
    gated = y * _silu(z_ref[...])
    gw = SSM_WIDTH // SSM_GROUPS
    outs = []
    for g in range(SSM_GROUPS):
        gg = gated[:, g * gw:(g + 1) * gw]
        outs.append(gg * lax.rsqrt(jnp.mean(gg * gg, axis=-1, keepdims=True) + EPS))
    y_ref[...] = (jnp.concatenate(outs, axis=1) * nw_ref[...]).astype(BF16)


def _ssd_sample(xbc, z, dt, cprev_t, h0, w, bs):
    n = xbc.shape[0]
    tok = lambda i: (i, 0)
    const = lambda i: (0, 0)
    full = lambda a: pl.BlockSpec(a.shape, const)
    return pl.pallas_call(
        _ssd_sample_kernel,
        out_shape=(jax.ShapeDtypeStruct((n, SSM_WIDTH), BF16),
                   jax.ShapeDtypeStruct((CONV_K - 1, n, CONV_DIM), F32),
                   jax.ShapeDtypeStruct((n, HEAD_PAIRS, LANES, SSM_STATE), F32)),
        grid=(n // bs,),
        in_specs=[
            pl.BlockSpec((bs, CONV_DIM), tok), pl.BlockSpec((bs, SSM_WIDTH), tok),
            pl.BlockSpec((bs, LANES), tok),
            pl.BlockSpec((CONV_K - 1, bs, CONV_DIM), lambda i: (0, i, 0)),
            pl.BlockSpec((bs, HEAD_PAIRS, LANES, SSM_STATE), lambda i: (i, 0, 0, 0)),
            full(w["conv_w"]), full(w["conv_b"]), full(w["a_log"]),
            full(w["d_skip"]), full(w["ssm_norm"]),
        ],
        out_specs=(pl.BlockSpec((bs, SSM_WIDTH), tok),
                   pl.BlockSpec((CONV_K - 1, bs, CONV_DIM), lambda i: (0, i, 0)),
                   pl.BlockSpec((bs, HEAD_PAIRS, LANES, SSM_STATE), lambda i: (i, 0, 0, 0))),
        compiler_params=pltpu.CompilerParams(dimension_semantics=("parallel",),
                                             vmem_limit_bytes=VMEM_LIMIT),
        name="ssd_sample",
    )(xbc, z, dt, cprev_t, h0, w["conv_w"], w["conv_b"], w["a_log"], w["d_skip"], w["ssm_norm"])


def _pack_bf16_pair(v):
    c = v.shape[1] // 2
    hi = lax.bitcast_convert_type(v[:, :c].astype(BF16).astype(F32), jnp.uint32)
    lo = lax.bitcast_convert_type(v[:, c:].astype(BF16).astype(F32), jnp.uint32)
    return hi | (lo >> 16)


def _unpack_bf16_pair(word):
    a = lax.bitcast_convert_type(word & jnp.uint32(0xFFFF0000), F32)
    b = lax.bitcast_convert_type(word << 16, F32)
    return a, b


def _outproj_router_kernel(x_ref, a_ref, s_ref, wo_ref, n2_ref, wr_ref, br_ref, tri_ref,
                           h_ref, t_ref, route_ref, routet_ref, cnt_ref, carry_ref):
    @pl.when(pl.program_id(0) == 0)
    def _():
        carry_ref[...] = jnp.zeros(carry_ref.shape, F32)

    tm = x_ref.shape[0]
    h_ref[...] = (x_ref[...] + _dot(a_ref[...].astype(BF16), wo_ref[:ATTN_WIDTH, :])
                  + _dot(s_ref[...].astype(BF16), wo_ref[ATTN_WIDTH:, :]))
    sub = min(tm, ROUTER_SUBTILE)
    carry = carry_ref[:, 0:1]
    for s in range(tm // sub):
        rows = slice(s * sub, (s + 1) * sub)
        carry = _route_rows(h_ref[rows, :], n2_ref, wr_ref, br_ref, tri_ref, carry, t_ref.at[rows, :],
                            route_ref.at[rows, :], routet_ref.at[:, rows])
    carry_ref[...] = jnp.broadcast_to(carry, carry_ref.shape)
    cpad = jnp.concatenate([jnp.zeros((EXPERT_LANE0, LANES), F32), jnp.broadcast_to(carry, (N_EXPERTS, LANES)),
                            jnp.zeros((LANES - EXPERT_LANE0 - N_EXPERTS, LANES), F32)], axis=0)
    cnt_ref[...] = cpad.T[0:cnt_ref.shape[0], :]


def _route_rows(h, n2_ref, wrt_ref, brc_ref, triu_ref, carry, t_ref, route_ref, routet_ref):
    ms = jnp.mean(h * h, axis=-1, keepdims=True)
    t = h * lax.rsqrt(ms + EPS) * n2_ref[...]
    t_hi, t_lo = _split2(t)
    t_ref[...] = _pack_bf16_pair(t)
    n = h.shape[0]
    a = _dot_nt(wrt_ref[...], t_hi)
    b = _dot_nt(wrt_ref[0:LANES, :], t_lo)

    def logit_rows(r0, r1):
        return a[r0:r1] + a[LANES + r0:LANES + r1] + b[r0:r1] + brc_ref[r0:r1, :]

    sl = EXP_PER_GROUP
    glog = logit_rows(0, sl)
    elog = logit_rows(EXPERT_LANE0, EXPERT_LANE0 + N_EXPERTS)
    row = lax.broadcasted_iota(jnp.int32, (sl, n), 0).astype(F32)
    big = float(sl)
    ninf = -jnp.inf
    gm = jnp.where(row < N_EGROUPS, glog, ninf)
    gmax = jnp.max(gm, axis=0, keepdims=True)
    g_top = 1.0 / jnp.sum(jnp.exp(gm - gmax), axis=0, keepdims=True)
    g_idx = jnp.min(jnp.where(gm == gmax, row, big), axis=0, keepdims=True)
    ml = elog[(N_EGROUPS - 1) * sl:]
    for g in range(N_EGROUPS - 2, -1, -1):
        ml = jnp.where(g_idx == float(g), elog[g * sl:(g + 1) * sl], ml)
    m1 = jnp.max(ml, axis=0, keepdims=True)
    i1 = jnp.min(jnp.where(ml == m1, row, big), axis=0, keepdims=True)
    ml2 = jnp.where(row == i1, ninf, ml)
    m2 = jnp.max(ml2, axis=0, keepdims=True)
    i2 = jnp.min(jnp.where(ml2 == m2, row, big), axis=0, keepdims=True)
    r = jnp.exp(m2 - m1)
    w1 = g_top / (1.0 + r)
    w2 = g_top * r / (1.0 + r)
    e1 = g_idx * float(sl) + i1
    e2 = g_idx * float(sl) + i2
    pick = jnp.where(jnp.logical_or(row == i1, row == i2), 1.0, 0.0)
    onehot = jnp.concatenate([jnp.where(g_idx == float(g), pick, 0.0) for g in range(N_EGROUPS)], axis=0)
    onehot_bf = onehot.astype(BF16)
    cb = triu_ref.shape[0]
    cums = []
    for blk in range(n // cb):
        c = _dot(onehot_bf[:, blk * cb:(blk + 1) * cb], triu_ref[...]) + carry
        carry = c[:, cb - 1:cb]
        cums.append(c)
    before = jnp.concatenate(cums, axis=1) - onehot
    erow = lax.broadcasted_iota(jnp.int32, (N_EXPERTS, n), 0).astype(F32)
    rank1 = jnp.sum(jnp.where(erow == e1, before, 0.0), axis=0, keepdims=True)
    rank2 = jnp.sum(jnp.where(erow == e2, before, 0.0), axis=0, keepdims=True)
    fields = jnp.concatenate([e1, e2, w1, w2, rank1, rank2], axis=0)
    routet_ref[0:fields.shape[0], :] = fields
    routet_ref[fields.shape[0]:, :] = jnp.zeros((routet_ref.shape[0] - fields.shape[0], n), F32)
    if n % LANES:
        fields = jnp.concatenate([fields, jnp.zeros((fields.shape[0], LANES - n % LANES), F32)], axis=1)
    pad = jnp.zeros((LANES - fields.shape[0], LANES), F32)
    for j in range(fields.shape[1] // LANES):
        blk_rows = jnp.concatenate([fields[:, j * LANES:(j + 1) * LANES], pad], axis=0)
        n_valid = min(LANES, n - j * LANES)
        route_ref[j * LANES:j * LANES + n_valid, :] = blk_rows.T[:n_valid, :]
    return carry


def _outproj_router(x2d, attn, ssm, w, tm):
    t = x2d.shape[0]
    tok = lambda i: (i, 0)
    const = lambda i: (0, 0)
    full = lambda a: pl.BlockSpec(a.shape, const)
    idx = jnp.arange(min(tm, COUNT_BLOCK))
    tri = (idx[:, None] <= idx[None, :]).astype(BF16)
    return pl.pallas_call(
        _outproj_router_kernel,
        out_shape=(jax.ShapeDtypeStruct((t, D_MODEL), F32), jax.ShapeDtypeStruct((t, D_MODEL // 2), jnp.uint32),
                   jax.ShapeDtypeStruct((t, LANES), F32), jax.ShapeDtypeStruct((8, t), F32),
                   jax.ShapeDtypeStruct((8, LANES), F32)),
        grid=(t // tm,),
        in_specs=[
            pl.BlockSpec((tm, D_MODEL), tok), pl.BlockSpec((tm, ATTN_WIDTH), tok),
            pl.BlockSpec((tm, SSM_WIDTH), tok),
            full(w["w_out"]), full(w["norm2"]), full(w["wrt"]),
            full(w["br_col"]), full(tri),
        ],
        out_specs=(pl.BlockSpec((tm, D_MODEL), tok), pl.BlockSpec((tm, D_MODEL // 2), tok),
                   pl.BlockSpec((tm, LANES), tok), pl.BlockSpec((8, tm), lambda i: (0, i)),
                   pl.BlockSpec((8, LANES), const)),
        scratch_shapes=[pltpu.VMEM((N_EXPERTS, LANES), F32)],
        compiler_params=pltpu.CompilerParams(dimension_semantics=("arbitrary",),
                                             vmem_limit_bytes=VMEM_LIMIT),
        name="outproj_router",
    )(x2d, attn, ssm, w["w_out"], w["norm2"], w["wrt"], w["br_col"], tri)


def _expert_hidden(t_a, t_b, w1):
    half = D_MODEL // 2
    gu = _dot(t_a, w1[:half]) + _dot(t_b, w1[half:])
    return _silu(gu[:, :EXPERT_FF]) * gu[:, EXPERT_FF:]


def _gate_up_bf16(wg_ref, wu_ref):
    return jnp.concatenate([wg_ref[0].astype(BF16), wu_ref[0].astype(BF16)], axis=1)


def _moe_dense_kernel(h_ref, t_ref, route_ref, wg_ref, wu_ref, wd_ref, y_ref):
    e = pl.program_id(0)

    @pl.when(e == 0)
    def _():
        y_ref[...] = h_ref[...]

    t_a, t_b = _unpack_bf16_pair(t_ref[...])
    route = route_ref[...]
    e1, e2, g1, g2 = route[:, 0:1], route[:, 1:2], route[:, 2:3], route[:, 3:4]
    e_f = e.astype(F32)
    hid = _expert_hidden(t_a.astype(BF16), t_b.astype(BF16), _gate_up_bf16(wg_ref, wu_ref))
    c_e = jnp.where(e1 == e_f, g1, 0.0) + jnp.where(e2 == e_f, g2, 0.0)
    y_ref[...] += _dot((hid * c_e).astype(BF16), wd_ref[0].astype(BF16))


def _moe_dense(h, t, route, w):
    n = h.shape[0]
    whole = lambda e: (0, 0)
    by_expert = lambda e: (e, 0, 0)
    return pl.pallas_call(
        _moe_dense_kernel,
        out_shape=jax.ShapeDtypeStruct((n, D_MODEL), F32),
        grid=(N_EXPERTS,),
        in_specs=[pl.BlockSpec((n, D_MODEL), whole), pl.BlockSpec((n, D_MODEL // 2), whole),
                  pl.BlockSpec((n, LANES), whole),
                  pl.BlockSpec((1, D_MODEL, EXPERT_FF), by_expert), pl.BlockSpec((1, D_MODEL, EXPERT_FF), by_expert),
                  pl.BlockSpec((1, EXPERT_FF, D_MODEL), by_expert)],
        out_specs=pl.BlockSpec((n, D_MODEL), whole),
        compiler_params=pltpu.CompilerParams(dimension_semantics=("arbitrary",),
                                             vmem_limit_bytes=VMEM_LIMIT),
        name="moe_dense",
    )(h, t, route, w["w_gate"], w["w_up"], w["w_down"])


def _sc_scatter_rows(src, pos1, pos2, n_out):
    t, width = src.shape
    rows_per_worker = t // SC_WORKERS
    n_chunks = rows_per_worker // SC_CHUNK
    assert t == SC_WORKERS * SC_CHUNK * n_chunks and n_chunks % 2 == 0
    mesh = plsc.VectorSubcoreMesh(core_axis_name="c", subcore_axis_name="s")

    @functools.partial(
        pl.kernel, mesh=mesh,
        out_type=jax.ShapeDtypeStruct((n_out, width), src.dtype),
        scratch_types=[pltpu.VMEM((2, SC_CHUNK), jnp.int32), pltpu.VMEM((2, SC_CHUNK), jnp.int32),
                       pltpu.VMEM((2, SC_CHUNK, width), src.dtype),
                       pltpu.SemaphoreType.DMA, pltpu.SemaphoreType.DMA,
                       pltpu.SemaphoreType.DMA, pltpu.SemaphoreType.DMA],
    )
    def scatter_kernel(src_hbm, p1_hbm, p2_hbm, out_hbm, i1_v, i2_v, rows_v, l0, l1, s0, s1):
        wid = lax.axis_index("s") * SC_CORES + lax.axis_index("c")
        base = wid * rows_per_worker
        lsem = (l0, l1)
        ssem = (s0, s1)

        def load_copy(c, slot):
            off = pl.multiple_of(base + c * SC_CHUNK, 8)
            return pltpu.make_async_copy(src_hbm.at[pl.ds(off, SC_CHUNK)], rows_v.at[slot], lsem[slot])

        def start_load(c, slot):
            off = pl.multiple_of(base + c * SC_CHUNK, 8)
            pltpu.sync_copy(p1_hbm.at[pl.ds(off, SC_CHUNK)], i1_v.at[slot])
            pltpu.sync_copy(p2_hbm.at[pl.ds(off, SC_CHUNK)], i2_v.at[slot])
            load_copy(c, slot).start()

        def scatter_copies(slot):
            return (pltpu.make_async_copy(rows_v.at[slot], out_hbm.at[i1_v.at[slot]], ssem[slot]),
                    pltpu.make_async_copy(rows_v.at[slot], out_hbm.at[i2_v.at[slot]], ssem[slot]))

        def start_scatter(slot):
            for cp in scatter_copies(slot):
                cp.start()

        def wait_scatter(slot):
            for cp in scatter_copies(slot):
                cp.wait()

        start_load(0, 0)

        @pl.loop(0, n_chunks, step=2)
        def _(c):
            @pl.when(c > 0)
            def _():
                wait_scatter(1)

            start_load(c + 1, 1)
            load_copy(c, 0).wait()
            start_scatter(0)
            load_copy(c + 1, 1).wait()
            wait_scatter(0)
            start_scatter(1)

            @pl.when(c + 2 < n_chunks)
            def _():
                start_load(c + 2, 0)

        wait_scatter(1)

    return scatter_kernel(src, pos1, pos2)


def _sc_gather_rows(table, idx):
    n, width = idx.shape[0], table.shape[1]
    rows_per_worker = n // SC_WORKERS
    n_chunks = rows_per_worker // SC_CHUNK
    assert n == SC_WORKERS * SC_CHUNK * n_chunks and n_chunks % 2 == 0
    mesh = plsc.VectorSubcoreMesh(core_axis_name="c", subcore_axis_name="s")

    @functools.partial(
        pl.kernel, mesh=mesh,
        out_type=jax.ShapeDtypeStruct((n, width), table.dtype),
        scratch_types=[pltpu.VMEM((2, SC_CHUNK), jnp.int32), pltpu.VMEM((2, SC_CHUNK, width), table.dtype),
                       pltpu.SemaphoreType.DMA, pltpu.SemaphoreType.DMA,
                       pltpu.SemaphoreType.DMA, pltpu.SemaphoreType.DMA],
    )
    def gather_kernel(table_hbm, idx_hbm, out_hbm, idx_v, rows_v, g0, g1, w0, w1):
        wid = lax.axis_index("s") * SC_CORES + lax.axis_index("c")
        base = wid * rows_per_worker
        gsem = (g0, g1)
        wsem = (w0, w1)

        def gather_copy(slot):
            return pltpu.make_async_copy(table_hbm.at[idx_v.at[slot]], rows_v.at[slot], gsem[slot])

        def write_copy(c, slot):
            off = pl.multiple_of(base + c * SC_CHUNK, 8)
            return pltpu.make_async_copy(rows_v.at[slot], out_hbm.at[pl.ds(off, SC_CHUNK)], wsem[slot])

        def start_gather(c, slot):
            off = pl.multiple_of(base + c * SC_CHUNK, 8)
            pltpu.sync_copy(idx_hbm.at[pl.ds(off, SC_CHUNK)], idx_v.at[slot])
            gather_copy(slot).start()

        start_gather(0, 0)

        @pl.loop(0, n_chunks, step=2)
        def _(c):
            @pl.when(c > 0)
            def _():
                write_copy(c - 1, 1).wait()

            start_gather(c + 1, 1)
            gather_copy(0).wait()
            write_copy(c, 0).start()
            gather_copy(1).wait()
            write_copy(c + 1, 1).start()
            write_copy(c, 0).wait()

            @pl.when(c + 2 < n_chunks)
            def _():
                start_gather(c + 2, 0)

        write_copy(n_chunks - 1, 1).wait()

    return gather_kernel(table, idx)


def _moe_grouped_kernel(te_ref, nt_ref, order_ref, x_ref, wg_ref, wu_ref, wd_ref, o_ref, w1_bf_ref, w2_bf_ref):
    del order_ref
    i = pl.program_id(0)

    @pl.when(jnp.logical_or(i == 0, te_ref[i] != te_ref[jnp.maximum(i - 1, 0)]))
    def _():
        w1_bf_ref[...] = _gate_up_bf16(wg_ref, wu_ref)
        w2_bf_ref[...] = wd_ref[0].astype(BF16)

    @pl.when(i < nt_ref[0])
    def _():
        for s in range(MOE_TILE // MOE_SUBTILE):
            rows = slice(s * MOE_SUBTILE, (s + 1) * MOE_SUBTILE)
            t_a, t_b = _unpack_bf16_pair(x_ref[rows, :])
            hid = _expert_hidden(t_a.astype(BF16), t_b.astype(BF16), w1_bf_ref[...])
            o_ref[rows, :] = _pack_bf16_pair(_dot(hid.astype(BF16), w2_bf_ref[...]))


def _moe_grouped(xs, tile_expert, n_tiles, order, w):
    rows = xs.shape[0]
    row = lambda i, te, nt, od: (jnp.minimum(i, nt[0] - 1), 0)
    by_expert = lambda i, te, nt, od: (te[i], 0, 0)
    return pl.pallas_call(
        _moe_grouped_kernel,
        out_shape=jax.ShapeDtypeStruct((rows, D_MODEL // 2), jnp.uint32),
        grid_spec=pltpu.PrefetchScalarGridSpec(
            num_scalar_prefetch=3,
            grid=(rows // MOE_TILE,),
            in_specs=[pl.BlockSpec((MOE_TILE, D_MODEL // 2), row),
                      pl.BlockSpec((1, D_MODEL, EXPERT_FF), by_expert),
                      pl.BlockSpec((1, D_MODEL, EXPERT_FF), by_expert),
                      pl.BlockSpec((1, EXPERT_FF, D_MODEL), by_expert)],
            out_specs=pl.BlockSpec((MOE_TILE, D_MODEL // 2), row),
            scratch_shapes=[pltpu.VMEM((D_MODEL, 2 * EXPERT_FF), BF16), pltpu.VMEM((EXPERT_FF, D_MODEL), BF16)],
        ),
        compiler_params=pltpu.CompilerParams(dimension_semantics=("arbitrary",),
                                             vmem_limit_bytes=VMEM_LIMIT),
        name="moe_grouped",
    )(tile_expert, n_tiles, order, xs, w["w_gate"], w["w_up"], w["w_down"])


def _moe_combine_kernel(h_ref, z1_ref, z2_ref, route_ref, y_ref):
    route = route_ref[...]
    g1, g2 = route[:, 2:3], route[:, 3:4]
    half = D_MODEL // 2
    a1, b1 = _unpack_bf16_pair(z1_ref[...])
    a2, b2 = _unpack_bf16_pair(z2_ref[...])
    y_ref[:, :half] = h_ref[:, :half] + g1 * a1 + g2 * a2
    y_ref[:, half:] = h_ref[:, half:] + g1 * b1 + g2 * b2


def _moe_combine(h, z, route, tm):
    t = h.shape[0]
    nb = t // tm
    tok = lambda i: (i, 0)
    return pl.pallas_call(
        _moe_combine_kernel,
        out_shape=jax.ShapeDtypeStruct((t, D_MODEL), F32),
        grid=(nb,),
        in_specs=[pl.BlockSpec((tm, D_MODEL), tok), pl.BlockSpec((tm, D_MODEL // 2), tok),
                  pl.BlockSpec((tm, D_MODEL // 2), lambda i: (i + nb, 0)), pl.BlockSpec((tm, LANES), tok)],
        out_specs=pl.BlockSpec((tm, D_MODEL), tok),
        compiler_params=pltpu.CompilerParams(dimension_semantics=("parallel",),
                                             vmem_limit_bytes=VMEM_LIMIT),
        name="moe_combine",
    )(h, z, z, route)


def _route_pos_kernel(routet_ref, cnt_ref, upper_ref, pos_ref):
    tm = routet_ref.shape[1]
    cnt = cnt_ref[...]
    padded = jnp.floor((cnt + float(MOE_TILE - 1)) * (1.0 / MOE_TILE)) * float(MOE_TILE)
    p_hi, p_mid, p_lo = _split3(padded)
    upper = upper_ref[...]
    starts = (_dot(p_hi, upper) + _dot(p_mid, upper) + _dot(p_lo, upper))[0:1, :]
    starts_col = jnp.broadcast_to(starts, (LANES, LANES)).T[EXPERT_LANE0:EXPERT_LANE0 + N_EXPERTS, 0:1]
    erow = lax.broadcasted_iota(jnp.int32, (N_EXPERTS, tm), 0).astype(F32)
    for k in range(2):
        e_k = routet_ref[k:k + 1, :]
        pos = (jnp.sum(jnp.where(erow == e_k, starts_col, 0.0), axis=0, keepdims=True)
               + routet_ref[4 + k:5 + k, :]).astype(jnp.int32)
        for r in range(tm // LANES):
            pos_ref[k, r:r + 1, :] = pos[:, r * LANES:(r + 1) * LANES]


def _route_positions(routet, counts, tm):
    t = routet.shape[1]
    idx = jnp.arange(LANES)
    upper = (idx[:, None] < idx[None, :]).astype(BF16)
    return pl.pallas_call(
        _route_pos_kernel,
        out_shape=jax.ShapeDtypeStruct((2, t // LANES, LANES), jnp.int32),
        grid=(t // tm,),
        in_specs=[pl.BlockSpec((8, tm), lambda i: (0, i)), pl.BlockSpec((8, LANES), lambda i: (0, 0)),
                  pl.BlockSpec((LANES, LANES), lambda i: (0, 0))],
        out_specs=pl.BlockSpec((2, tm // LANES, LANES), lambda i: (0, i, 0)),
        compiler_params=pltpu.CompilerParams(dimension_semantics=("parallel",),
                                             vmem_limit_bytes=VMEM_LIMIT),
        name="route_positions",
    )(routet, counts, upper)


def _moe_routed(h, t_packed, route, routet, counts, w, tm, run_first):
    t = h.shape[0]
    pos = _route_positions(routet, counts, min(32 * LANES, t))
    pos1 = pos[0].reshape(t)
    pos2 = pos[1].reshape(t)
    cnt = counts[0, EXPERT_LANE0:EXPERT_LANE0 + N_EXPERTS].astype(jnp.int32)
    padded = (cnt + MOE_TILE - 1) // MOE_TILE * MOE_TILE
    ends = jnp.cumsum(padded)
    n_rows = 2 * t + N_EXPERTS * MOE_TILE
    n_tiles = ends[N_EXPERTS - 1] // MOE_TILE
    tile_start = jnp.arange(n_rows // MOE_TILE, dtype=jnp.int32) * MOE_TILE
    tile_start = jnp.minimum(tile_start, ends[N_EXPERTS - 1] - MOE_TILE)
    tile_expert = jnp.sum((tile_start[:, None] >= ends[None, :]).astype(jnp.int32), axis=1)
    xs = _sc_scatter_rows(t_packed, pos1, pos2, n_rows)
    order = lax.bitcast_convert_type(run_first.reshape(-1)[:1].astype(F32), jnp.int32)
    out = _moe_grouped(xs, tile_expert, n_tiles.reshape(1), order, w)
    z = _sc_gather_rows(out, pos.reshape(2 * t))
    return _moe_combine(h, z, route, tm)


def _pad_lanes(a, width=LANES):
    return jnp.pad(a, ((0, 0), (0, width - a.shape[1])))


def _prep_weights(norm1, w_in, q_norm, k_norm, conv_w, conv_b, dt_bias, a_log, d_skip, ssm_norm, w_out,
                  norm2, w_grp, b_grp, w_exp, b_exp, w_gate, w_up, w_down):
    w = {}
    w["norm1"] = norm1.reshape(1, D_MODEL)
    w["w_in"] = _pad_lanes(w_in, XBC_END + LANES).astype(BF16)
    w["qkn"] = jnp.concatenate([jnp.tile(q_norm, N_HEADS), jnp.tile(k_norm, KV_HEADS)]).reshape(1, QK_WIDTH)
    head_of_col = jnp.arange(QK_WIDTH) // HEAD_DIM
    red = (head_of_col[:, None] == jnp.arange(LANES)[None, :])
    w["red"] = red.astype(BF16)
    w["exp"] = red.T.astype(BF16)
    w["conv_w"] = conv_w
    w["conv_b"] = conv_b.reshape(1, CONV_DIM)
    w["dt_bias"] = _pad_lanes(dt_bias.reshape(1, SSM_HEADS))
    w["a_log"] = _pad_lanes(a_log.reshape(1, SSM_HEADS))
    w["d_skip"] = jnp.repeat(d_skip, SSM_HEAD_DIM).reshape(1, SSM_WIDTH)
    w["ssm_norm"] = ssm_norm.reshape(1, SSM_WIDTH)
    idx = jnp.arange(SSD_CHUNK)
    w["tri"] = (idx[None, :] <= idx[:, None]).astype(BF16)
    lane_head = jnp.arange(SSM_WIDTH) // SSM_HEAD_DIM
    w["expand"] = (jnp.arange(LANES)[:, None] == lane_head[None, :]).astype(BF16)
    w["w_out"] = w_out.astype(BF16)
    w["norm2"] = norm2.reshape(1, D_MODEL)
    wr = jnp.zeros((D_MODEL, LANES), F32)
    wr = wr.at[:, :N_EGROUPS].set(w_grp).at[:, EXPERT_LANE0:EXPERT_LANE0 + N_EXPERTS].set(w_exp)
    wr_hi = wr.astype(BF16)
    w["wrt"] = jnp.concatenate([wr_hi, (wr - wr_hi.astype(F32)).astype(BF16)], axis=1).T
    br = jnp.zeros((LANES, 1), F32)
    w["br_col"] = br.at[:N_EGROUPS, 0].set(b_grp).at[EXPERT_LANE0:EXPERT_LANE0 + N_EXPERTS, 0].set(b_exp)
    w["w_gate"], w["w_up"], w["w_down"] = w_gate, w_up, w_down
    return w


def _rope_tables(pos):
    inv = 1.0 / (ROPE_THETA ** (jnp.arange(0, HEAD_DIM, 2, dtype=F32) / HEAD_DIM))
    ang = pos.astype(F32)[:, None] * inv[None, :]
    cos, sin = jnp.cos(ang), jnp.sin(ang)
    reps = LANES // HEAD_DIM
    return (jnp.tile(jnp.concatenate([cos, cos], axis=-1), (1, reps)),
            jnp.tile(jnp.concatenate([-sin, sin], axis=-1), (1, reps)))


def _token_tile(t):
    for tm in (1024, 512, 256, 128, 64, 32, 16):
        if t % tm == 0:
            return tm
    raise ValueError(f"token count {t} is not a multiple of 16")


def kernel(x_prompt, x_sample, cache_win_k, cache_win_v, state_conv, state_ssm, norm1, w_in, q_norm, k_norm,
           sinks, conv_w, conv_b, dt_bias, a_log, d_skip, ssm_norm, w_out, norm2, w_grp, b_grp, w_exp, b_exp,
           w_gate, w_up, w_down):
    depth = norm1.shape[0]
    assert depth == 1, "single-layer stack"
    bp, lp, _ = x_prompt.shape
    bsn, ls, _ = x_sample.shape
    assert ls == 1 and lp % WINDOW == 0 and cache_win_k.shape[2] == WINDOW
    l = 0
    w = _prep_weights(norm1[l], w_in[l], q_norm[l], k_norm[l], conv_w[l], conv_b[l], dt_bias[l], a_log[l],
                      d_skip[l], ssm_norm[l], w_out[l], norm2[l], w_grp[l], b_grp[l], w_exp[l], b_exp[l],
                      w_gate[l], w_up[l], w_down[l])
    sink = sinks[l]

    tp = bp * lp
    xp = x_prompt.reshape(tp, D_MODEL)
    tm_p = _token_tile(lp)
    cos_p, sin_p = _rope_tables(jnp.arange(lp, dtype=jnp.int32))
    q, k, v, z, xbc, dt, k2, v2 = _inproj(xp, w, cos_p, sin_p, tm_p, lp // tm_p)
    attn = _attn_prompt(q, k2, v2, sink, bp, lp)
    ssm, st_p = _ssd_prompt(xbc, z, dt, w, bp, lp)
    h, t, route, routet, counts = _outproj_router(xp, attn, ssm, w, tm_p)
    k3 = k.reshape(bp, lp, KV_HEADS, HEAD_DIM)
    v3 = v.reshape(bp, lp, KV_HEADS, HEAD_DIM)
    win_k_p = k3[:, lp - WINDOW:][None]
    win_v_p = v3[:, lp - WINDOW:][None]
    conv_p = xbc.reshape(bp, lp, CONV_DIM)[:, lp - (CONV_K - 1):][None]
    ssm_p = st_p.reshape(1, bp, SSM_HEADS, SSM_HEAD_DIM, SSM_STATE)

    xs2 = x_sample.reshape(bsn, D_MODEL)
    tm_s = _token_tile(bsn)
    cos_s, sin_s = _rope_tables(jnp.full((tm_s,), PAST_LEN, jnp.int32))
    q_s, k_s, v_s, z_s, xbc_s, dt_s, _, _ = _inproj(xs2, w, cos_s, sin_s, tm_s, 1)
    q4 = q_s.reshape(bsn, KV_HEADS, N_HEADS // KV_HEADS, HEAD_DIM)
    zq = jnp.zeros_like(q4[:, 0])
    qx = jnp.concatenate([jnp.concatenate([q4[:, 0], zq], axis=-1),
                          jnp.concatenate([zq, q4[:, 1]], axis=-1)], axis=1)
    qx = jnp.pad(qx, ((0, 0), (0, BF16_ROWS - N_HEADS), (0, 0)))
    sink_x = jnp.pad(jnp.broadcast_to(sink[:, None], (N_HEADS, LANES)), ((0, BF16_ROWS - N_HEADS), (0, 0)))
    kc = cache_win_k[l].reshape(bsn, WINDOW, KV_WIDTH)
    vc = cache_win_v[l].reshape(bsn, WINDOW, KV_WIDTH)
    ko, vo, attn_s = _attn_sample(qx, kc, k_s.reshape(bsn, 1, KV_WIDTH), vc, v_s.reshape(bsn, 1, KV_WIDTH),
                               sink_x, 8)
    y_prompt = _moe_routed(h, t, route, routet, counts, w, tm_p, attn_s).reshape(bp, lp, D_MODEL)
    cprev_t = jnp.transpose(state_conv[l], (1, 0, 2))
    h0 = state_ssm[l].reshape(bsn, HEAD_PAIRS, LANES, SSM_STATE)
    ssm_s, cnew_t, h1 = _ssd_sample(xbc_s, z_s, dt_s, cprev_t, h0, w, 16)
    h_s, t_s, route_s, _, _ = _outproj_router(xs2, attn_s, ssm_s, w, tm_s)
    y_sample = _moe_dense(h_s, t_s, route_s, w).reshape(bsn, 1, D_MODEL)
    win_k_s = ko.reshape(1, bsn, WINDOW, KV_HEADS, HEAD_DIM)
    win_v_s = vo.reshape(1, bsn, WINDOW, KV_HEADS, HEAD_DIM)
    conv_s = jnp.transpose(cnew_t, (1, 0, 2))[None]
    ssm_s_state = h1.reshape(1, bsn, SSM_HEADS, SSM_HEAD_DIM, SSM_STATE)

    return (y_prompt, y_sample, win_k_p, win_v_p, conv_p, ssm_p, win_k_s, win_v_s, conv_s, ssm_s_state)
```

```python
import functools
import math

import jax
import jax.numpy as jnp
from jax import lax
from jax.experimental import pallas as pl
from jax.experimental.pallas import tpu as pltpu
from jax.experimental.pallas import tpu_sc as plsc

F32 = jnp.float32
BF16 = jnp.bfloat16

D_MODEL = 1024
HEAD_DIM = 64
N_HEADS = 8
KV_HEADS = 2
WINDOW = 128
ATTN_WIDTH = N_HEADS * HEAD_DIM
QK_WIDTH = ATTN_WIDTH + KV_HEADS * HEAD_DIM
KV_WIDTH = KV_HEADS * HEAD_DIM
ATTN_SCALE = HEAD_DIM ** -0.5
ROPE_THETA = 10000.0
SSM_WIDTH = 512
SSM_HEADS = 8
SSM_HEAD_DIM = 64
SSM_GROUPS = 2
SSM_STATE = 128
CONV_K = 4
CONV_HALO = 8
CONV_DIM = SSM_WIDTH + 2 * SSM_GROUPS * SSM_STATE
SSD_CHUNK = 128
N_EGROUPS = 4
EXP_PER_GROUP = 8
N_EXPERTS = 32
EXPERT_FF = 128
EPS = 1e-6
PAST_LEN = 16384

LANES = 128
BF16_ROWS = 16
HEAD_PAIRS = SSM_HEADS // 2
EXPERT_LANE0 = 32
VMEM_LIMIT = 56 * 1024 * 1024
MOE_TILE = 1024
MOE_SUBTILE = 128
ATTN_QBLOCKS = 8
SSD_CHUNKS_PER_STEP = 8
COUNT_BLOCK = 256
ROUTER_SUBTILE = 1024
INPROJ_SUBTILE = 512
SC_CORES = 2
SC_SUBCORES = 16
SC_WORKERS = SC_CORES * SC_SUBCORES
SC_CHUNK = 64

Q_END = ATTN_WIDTH
K_END = Q_END + KV_WIDTH
V_END = K_END + KV_WIDTH
Z_END = V_END + SSM_WIDTH
XBC_END = Z_END + CONV_DIM


def _dot(a, b):
    return jnp.dot(a, b, preferred_element_type=F32)


def _dot_nt(a, b):
    return lax.dot_general(a, b, (((1,), (1,)), ((), ())), preferred_element_type=F32)


def _split2(v):
    hi = v.astype(BF16)
    lo = (v - hi.astype(F32)).astype(BF16)
    return hi, lo


def _split3(v):
    hi = v.astype(BF16)
    r = v - hi.astype(F32)
    mid = r.astype(BF16)
    lo = (r - mid.astype(F32)).astype(BF16)
    return hi, mid, lo


def _silu(x):
    return x * jax.nn.sigmoid(x)


def _softplus(x):
    return jnp.maximum(x, 0.0) + jnp.log1p(jnp.exp(-jnp.abs(x)))


def _lane_bcast_pairs(v, n_pairs):
    r = v.shape[0]
    lo = lax.broadcasted_iota(jnp.int32, (r, LANES), 1) < HEAD_DIM
    slabs = []
    for j in range(n_pairs):
        a = jnp.broadcast_to(v[:, 2 * j:2 * j + 1], (r, LANES))
        b = jnp.broadcast_to(v[:, 2 * j + 1:2 * j + 2], (r, LANES))
        slabs.append(jnp.where(lo, a, b))
    return jnp.concatenate(slabs, axis=1)


def _causal_conv_silu(x_ext, convw_ref, convb_ref):
    halo = CONV_HALO
    x_raw = x_ext[halo:, :]
    conv = convb_ref[...] + x_raw * convw_ref[CONV_K - 1:CONV_K, :]
    for j in range(CONV_K - 1):
        shifted = pltpu.roll(x_ext, CONV_K - 1 - j, axis=0)[halo:, :]
        conv = conv + shifted * convw_ref[j:j + 1, :]
    return _silu(conv)


def _inproj_kernel(x_ref, n1_ref, win_ref, dtb_ref, qkn_ref,
                   cos_ref, sin_ref, red_ref, exp_ref,
                   q_ref, k_ref, v_ref, z_ref, xbc_ref, dt_ref, k2_ref, v2_ref):
    tm = x_ref.shape[0]
    sub = min(tm, INPROJ_SUBTILE)
    lane = lax.broadcasted_iota(jnp.int32, (sub, LANES), 1)
    first_half = (lane % HEAD_DIM) < (HEAD_DIM // 2)
    for s in range(tm // sub):
        rows = slice(s * sub, (s + 1) * sub)
        x = x_ref[rows, :]
        ms = jnp.mean(x * x, axis=-1, keepdims=True)
        xn = (x * lax.rsqrt(ms + EPS) * n1_ref[...]).astype(BF16)
        v = _dot(xn, win_ref[:, K_END:V_END])
        v_ref[rows, :] = v
        v2_ref[rows, :] = _pair_operands(v)
        z_ref[rows, :] = _dot(xn, win_ref[:, V_END:Z_END])
        xbc_ref[rows, :] = _dot(xn, win_ref[:, Z_END:XBC_END])
        dt_ref[rows, :] = _softplus(_dot(xn, win_ref[:, XBC_END:]) + dtb_ref[...])
        qk = _dot(xn, win_ref[:, :K_END])
        ss = _dot((qk * qk).astype(BF16), red_ref[...])
        inv = lax.rsqrt(ss * (1.0 / HEAD_DIM) + EPS)
        inv_hi, inv_lo = _split2(inv)
        inv_x = _dot(inv_hi, exp_ref[...]) + _dot(inv_lo, exp_ref[...])
        qkn = qk * inv_x * qkn_ref[...]
        cos = cos_ref[rows, :]
        sin = sin_ref[rows, :]
        for c in range(QK_WIDTH // LANES):
            xc = qkn[:, c * LANES:(c + 1) * LANES]
            partner = jnp.where(first_half,
                                pltpu.roll(xc, LANES - HEAD_DIM // 2, axis=1),
                                pltpu.roll(xc, HEAD_DIM // 2, axis=1))
            rot = xc * cos + partner * sin
            if c < ATTN_WIDTH // LANES:
                q_ref[rows, c * LANES:(c + 1) * LANES] = (rot * ATTN_SCALE).astype(BF16)
            else:
                k_ref[rows, :] = rot
                k2_ref[rows, :] = _pair_operands(rot)


def _inproj(x2d, w, cos_tab, sin_tab, tm, n_pos_blocks):
    t = x2d.shape[0]
    grid = (t // tm,)
    tok = lambda i: (i, 0)
    const = lambda i: (0, 0)
    pos = lambda i: (i % n_pos_blocks, 0)
    full = lambda a: pl.BlockSpec(a.shape, const)
    rows = lambda width, dtype: (jax.ShapeDtypeStruct((t, width), dtype), pl.BlockSpec((tm, width), tok))
    outs = [rows(ATTN_WIDTH, BF16), rows(KV_WIDTH, F32), rows(KV_WIDTH, F32), rows(SSM_WIDTH, F32)]
    operands = [x2d, w["norm1"], w["w_in"], w["dt_bias"], w["qkn"],
                cos_tab, sin_tab, w["red"], w["exp"]]
    in_specs = [
        pl.BlockSpec((tm, D_MODEL), tok),
        full(w["norm1"]), full(w["w_in"]), full(w["dt_bias"]), full(w["qkn"]),
        pl.BlockSpec((tm, LANES), pos), pl.BlockSpec((tm, LANES), pos),
        full(w["red"]), full(w["exp"]),
    ]
    outs += [rows(CONV_DIM, F32), rows(LANES, F32), rows(4 * LANES, BF16), rows(4 * LANES, BF16)]
    return pl.pallas_call(
        _inproj_kernel,
        out_shape=tuple(o[0] for o in outs),
        grid=grid,
        in_specs=in_specs,
        out_specs=tuple(o[1] for o in outs),
        compiler_params=pltpu.CompilerParams(dimension_semantics=("parallel",),
                                             vmem_limit_bytes=VMEM_LIMIT),
        name="inproj",
    )(*operands)


def _pair_operands(kv):
    lo = lax.broadcasted_iota(jnp.int32, kv.shape, 1) < HEAD_DIM
    swapped = pltpu.roll(kv, HEAD_DIM, axis=1)
    parts = [jnp.where(lo, kv, 0.0), jnp.where(lo, 0.0, swapped), jnp.where(lo, swapped, 0.0), jnp.where(lo, 0.0, kv)]
    return jnp.concatenate(parts, axis=1).astype(BF16)


def _pair_rhs(blk, g):
    return jnp.concatenate([blk[:, 2 * g * LANES:(2 * g + 1) * LANES],
                            blk[:, (2 * g + 1) * LANES:(2 * g + 2) * LANES]], axis=0)


def _attn_qblock(sink_ref, q_blk, k_prev, k_cur, v_prev, v_cur, seq_start, o_ref):
    blk = WINDOW
    qi = lax.broadcasted_iota(jnp.int32, (blk, 2 * blk), 0)
    kj = lax.broadcasted_iota(jnp.int32, (blk, 2 * blk), 1) % blk
    cur_ok = kj <= qi
    lo = lax.broadcasted_iota(jnp.int32, (blk, LANES), 1) < HEAD_DIM
    n_pairs = N_HEADS // KV_HEADS // 2
    for g in range(KV_HEADS):
        q_all = jnp.concatenate([q_blk[:, (g * n_pairs + r) * LANES:(g * n_pairs + r + 1) * LANES]
                                 for r in range(n_pairs)], axis=0)
        s_all = _dot_nt(q_all, jnp.concatenate([_pair_rhs(k_cur, g), _pair_rhs(k_prev, g)], axis=0))
        p_rows = []
        den_rows = []
        for r in range(n_pairs):
            pair = g * n_pairs + r
            s_cur = s_all[r * blk:(r + 1) * blk, :2 * blk]
            s_prev = s_all[r * blk:(r + 1) * blk, 2 * blk:]
            s = jnp.where(cur_ok, s_cur, s_prev)
            if seq_start is not None:
                s = jnp.where(jnp.logical_or(cur_ok, jnp.logical_not(seq_start)), s, -jnp.inf)
            ps = []
            dens = []
            for hh in range(2):
                sink = sink_ref[2 * pair + hh]
                sh = s[:, hh * blk:(hh + 1) * blk]
                m = jnp.maximum(jnp.max(sh, axis=-1, keepdims=True), sink)
                p = jnp.exp(sh - m)
                dens.append(jnp.sum(p, axis=-1, keepdims=True) + jnp.exp(sink - m))
                ps.append(p)
            p2 = jnp.concatenate(ps, axis=1)
            p_rows.append(jnp.concatenate([jnp.where(cur_ok, p2, 0.0), jnp.where(cur_ok, 0.0, p2)],
                                          axis=1).astype(BF16))
            den_rows.append(jnp.where(lo, dens[0], dens[1]))
        o_all = _dot(jnp.concatenate(p_rows, axis=0),
                     jnp.concatenate([_pair_rhs(v_cur, g), _pair_rhs(v_prev, g)], axis=0))
        for r in range(n_pairs):
            pair = g * n_pairs + r
            o2 = o_all[r * blk:(r + 1) * blk, :]
            o_ref[:, pair * LANES:(pair + 1) * LANES] = (o2 / den_rows[r]).astype(BF16)


def _attn_kernel(sink_ref, q_ref, kc_ref, kp_ref, vc_ref, vp_ref, o_ref):
    blk = WINDOW
    first_step = pl.program_id(1) == 0
    for u in range(q_ref.shape[0] // blk):
        rows = slice(u * blk, (u + 1) * blk)
        prev_rows = slice((u - 1) * blk, u * blk)
        k_prev = kp_ref[...] if u == 0 else kc_ref[prev_rows, :]
        v_prev = vp_ref[...] if u == 0 else vc_ref[prev_rows, :]
        _attn_qblock(sink_ref, q_ref[rows, :], k_prev, kc_ref[rows, :], v_prev, vc_ref[rows, :],
                     first_step if u == 0 else None, o_ref.at[rows, :])


def _attn_prompt(q, k, v, sinks, batch, seq):
    n_sub = ATTN_QBLOCKS if seq % (ATTN_QBLOCKS * WINDOW) == 0 else 1
    rows = n_sub * WINDOW
    nb = seq // rows
    cur = lambda b, j, s: (b * nb + j, 0)
    prev = lambda b, j, s: (jnp.maximum((b * nb + j) * n_sub - 1, 0), 0)
    return pl.pallas_call(
        _attn_kernel,
        out_shape=jax.ShapeDtypeStruct((batch * seq, ATTN_WIDTH), BF16),
        grid_spec=pltpu.PrefetchScalarGridSpec(
            num_scalar_prefetch=1,
            grid=(batch, nb),
            in_specs=[
                pl.BlockSpec((rows, ATTN_WIDTH), cur),
                pl.BlockSpec((rows, 4 * LANES), cur), pl.BlockSpec((WINDOW, 4 * LANES), prev),
                pl.BlockSpec((rows, 4 * LANES), cur), pl.BlockSpec((WINDOW, 4 * LANES), prev),
            ],
            out_specs=pl.BlockSpec((rows, ATTN_WIDTH), cur),
        ),
        compiler_params=pltpu.CompilerParams(dimension_semantics=("parallel", "parallel"),
                                             vmem_limit_bytes=VMEM_LIMIT),
        name="attn_prompt",
    )(sinks, q, k, k, v, v)


def _ssd_kernel(xbc_ref, z_ref, dt_ref, convw_ref, convb_ref, alog_ref, dskip_ref, nw_ref,
                tri_ref, expand_ref, y_ref, st_ref, buf_ref, state_ref):
    c = pl.program_id(1)
    cl = SSD_CHUNK
    n_sub = xbc_ref.shape[0] // cl
    halo = CONV_HALO

    @pl.when(c == 0)
    def _():
        buf_ref[...] = jnp.zeros(buf_ref.shape, F32)
        state_ref[...] = jnp.zeros(state_ref.shape, F32)

    lane = lax.broadcasted_iota(jnp.int32, (1, LANES), 1)
    a_neg = jnp.where(lane < SSM_HEADS, -jnp.exp(alog_ref[...]), 0.0)
    tri = tri_ref[...]
    for u in range(n_sub):
        rows = slice(u * cl, (u + 1) * cl)
        if u == 0:
            x_ext = jnp.concatenate([buf_ref[...], xbc_ref[rows, :]], axis=0)
        else:
            x_ext = xbc_ref[u * cl - halo:(u + 1) * cl, :]
        _ssd_chunk(x_ext, z_ref[rows, :], dt_ref[rows, :], a_neg, tri, convw_ref, convb_ref, dskip_ref, nw_ref,
                   expand_ref, y_ref.at[rows, :], state_ref)
    buf_ref[...] = xbc_ref[n_sub * cl - halo:n_sub * cl, :]

    @pl.when(c == pl.num_programs(1) - 1)
    def _():
        st_ref[0] = state_ref[...]


def _ssd_chunk(x_ext, z, dt, a_neg, tri, convw_ref, convb_ref, dskip_ref, nw_ref, expand_ref, y_ref, state_ref):
    cl = SSD_CHUNK
    act = _causal_conv_silu(x_ext, convw_ref, convb_ref)
    xs = act[:, :SSM_WIDTH]
    bm = act[:, SSM_WIDTH:SSM_WIDTH + SSM_GROUPS * SSM_STATE].astype(BF16)
    cm = act[:, SSM_WIDTH + SSM_GROUPS * SSM_STATE:].astype(BF16)

    dta = dt * a_neg
    p_hi, p_mid, p_lo = _split3(dta)
    a_col = _dot(tri, p_hi) + _dot(tri, p_mid) + _dot(tri, p_lo)
    a_last = a_col[cl - 1:cl, :]
    a_row = a_col.T
    per_head = jnp.concatenate([dt, jnp.exp(a_col), jnp.exp(a_last - a_col)], axis=0)
    ph_hi, ph_lo = _split2(per_head)
    per_lane = _dot(ph_hi, expand_ref[...]) + _dot(ph_lo, expand_ref[...])
    dt_x = per_lane[:cl]
    ecol_x = per_lane[cl:2 * cl]
    dte_x = per_lane[2 * cl:]
    e_last = jnp.exp(a_last)
    xdt = xs * dt_x

    li = lax.broadcasted_iota(jnp.int32, (cl, cl), 0)
    si = lax.broadcasted_iota(jnp.int32, (cl, cl), 1)
    causal = si <= li
    lo = lax.broadcasted_iota(jnp.int32, (cl, LANES), 1) < SSM_HEAD_DIM
    row_lo = lax.broadcasted_iota(jnp.int32, (LANES, SSM_STATE), 0) < SSM_HEAD_DIM

    ys = []
    for g in range(SSM_GROUPS):
        b_g = bm[:, g * SSM_STATE:(g + 1) * SSM_STATE]
        c_g = cm[:, g * SSM_STATE:(g + 1) * SSM_STATE]
        cb = _dot_nt(c_g, b_g)
        for r in range(HEAD_PAIRS // SSM_GROUPS):
            j = g * (HEAD_PAIRS // SSM_GROUPS) + r
            sl = slice(j * LANES, (j + 1) * LANES)
            xdt_p = xdt[:, sl]
            ms = []
            for hh in range(2):
                h = 2 * j + hh
                seg = a_col[:, h:h + 1] - a_row[h:h + 1, :]
                ms.append(cb * jnp.exp(jnp.where(causal, seg, -jnp.inf)))
            m2 = jnp.concatenate(ms, axis=1).astype(BF16)
            rhs = jnp.concatenate([jnp.where(lo, xdt_p, 0.0), jnp.where(lo, 0.0, xdt_p)],
                                  axis=0).astype(BF16)
            y_diag = _dot(m2, rhs)
            st = state_ref[j]
            y_off = _dot_nt(c_g, st.astype(BF16)) * ecol_x[:, sl]
            xdt_e = (xdt_p * dte_x[:, sl]).T.astype(BF16)
            d_a = e_last[:, 2 * j:2 * j + 1]
            d_b = e_last[:, 2 * j + 1:2 * j + 2]
            decay = jnp.where(row_lo, jnp.broadcast_to(d_a, row_lo.shape), jnp.broadcast_to(d_b, row_lo.shape))
            state_ref[j] = decay * st + _dot(xdt_e, b_g)
            ys.append(y_diag + y_off + dskip_ref[:, sl] * xs[:, sl])
    y = jnp.concatenate(ys, axis=1)
    gated = y * _silu(z)
    gw = SSM_WIDTH // SSM_GROUPS
    outs = []
    for g in range(SSM_GROUPS):
        gg = gated[:, g * gw:(g + 1) * gw]
        outs.append(gg * lax.rsqrt(jnp.mean(gg * gg, axis=-1, keepdims=True) + EPS))
    y_ref[...] = (jnp.concatenate(outs, axis=1) * nw_ref[...]).astype(BF16)


def _ssd_prompt(xbc, z, dt, w, batch, seq):
    n_sub = SSD_CHUNKS_PER_STEP if seq % (SSD_CHUNKS_PER_STEP * SSD_CHUNK) == 0 else 1
    rows = n_sub * SSD_CHUNK
    nc = seq // rows
    tok = lambda b, c: (b * nc + c, 0)
    const = lambda b, c: (0, 0)
    full = lambda a: pl.BlockSpec(a.shape, const)
    return pl.pallas_call(
        _ssd_kernel,
        out_shape=(jax.ShapeDtypeStruct((batch * seq, SSM_WIDTH), BF16),
                   jax.ShapeDtypeStruct((batch, HEAD_PAIRS, LANES, SSM_STATE), F32)),
        grid=(batch, nc),
        in_specs=[
            pl.BlockSpec((rows, CONV_DIM), tok), pl.BlockSpec((rows, SSM_WIDTH), tok),
            pl.BlockSpec((rows, LANES), tok),
            full(w["conv_w"]), full(w["conv_b"]), full(w["a_log"]),
            full(w["d_skip"]), full(w["ssm_norm"]), full(w["tri"]), full(w["expand"]),
        ],
        out_specs=(pl.BlockSpec((rows, SSM_WIDTH), tok),
                   pl.BlockSpec((1, HEAD_PAIRS, LANES, SSM_STATE), lambda b, c: (b, 0, 0, 0))),
        scratch_shapes=[pltpu.VMEM((CONV_HALO, CONV_DIM), F32),
                        pltpu.VMEM((HEAD_PAIRS, LANES, SSM_STATE), F32)],
        compiler_params=pltpu.CompilerParams(dimension_semantics=("parallel", "arbitrary"),
                                             vmem_limit_bytes=VMEM_LIMIT),
        name="ssd_prompt",
    )(xbc, z, dt, w["conv_w"], w["conv_b"], w["a_log"], w["d_skip"], w["ssm_norm"], w["tri"], w["expand"])


def _attn_sample_kernel(qx_ref, kc_ref, kn_ref, vc_ref, vn_ref, sink_ref, ko_ref, vo_ref, o_ref):
    bs = qx_ref.shape[0]
    w = kc_ref.shape[1]
    sink = sink_ref[...]
    lo = lax.broadcasted_iota(jnp.int32, (1, LANES), 1) < HEAD_DIM
    for i in range(bs):
        ko_ref[i, 0:w - 1, :] = kc_ref[i, 1:w, :]
        ko_ref[i, w - 1:w, :] = kn_ref[i]
        vo_ref[i, 0:w - 1, :] = vc_ref[i, 1:w, :]
        vo_ref[i, w - 1:w, :] = vn_ref[i]
        s = _dot_nt(qx_ref[i], ko_ref[i].astype(BF16))
        m = jnp.maximum(jnp.max(s, axis=-1, keepdims=True), sink)
        p = jnp.exp(s - m)
        den = jnp.sum(p, axis=-1, keepdims=True) + jnp.exp(sink - m)
        o = _dot(p.astype(BF16), vo_ref[i].astype(BF16)) / den
        o_sw = pltpu.roll(o, HEAD_DIM, axis=1)
        for j in range(N_HEADS // 2):
            a, b = (o, o_sw) if j < N_HEADS // 4 else (o_sw, o)
            o_ref[i:i + 1, j * LANES:(j + 1) * LANES] = jnp.where(lo, a[2 * j:2 * j + 1], b[2 * j + 1:2 * j + 2])


def _attn_sample(qx, kc, kn, vc, vn, sink_x, bs):
    n, w = kc.shape[0], kc.shape[1]
    blk3 = lambda i: (i, 0, 0)
    return pl.pallas_call(
        _attn_sample_kernel,
        out_shape=(jax.ShapeDtypeStruct((n, w, KV_WIDTH), F32),
                   jax.ShapeDtypeStruct((n, w, KV_WIDTH), F32),
                   jax.ShapeDtypeStruct((n, ATTN_WIDTH), F32)),
        grid=(n // bs,),
        in_specs=[
            pl.BlockSpec((bs, BF16_ROWS, LANES), blk3),
            pl.BlockSpec((bs, w, KV_WIDTH), blk3), pl.BlockSpec((bs, 1, KV_WIDTH), blk3),
            pl.BlockSpec((bs, w, KV_WIDTH), blk3), pl.BlockSpec((bs, 1, KV_WIDTH), blk3),
            pl.BlockSpec(sink_x.shape, lambda i: (0, 0)),
        ],
        out_specs=(pl.BlockSpec((bs, w, KV_WIDTH), blk3), pl.BlockSpec((bs, w, KV_WIDTH), blk3),
                   pl.BlockSpec((bs, ATTN_WIDTH), lambda i: (i, 0))),
        compiler_params=pltpu.CompilerParams(dimension_semantics=("parallel",),
                                             vmem_limit_bytes=VMEM_LIMIT),
        name="attn_sample",
    )(qx, kc, kn, vc, vn, sink_x)


def _ssd_sample_kernel(xbc_ref, z_ref, dt_ref, cprev_ref, h0_ref, convw_ref, convb_ref, alog_ref,
                       dskip_ref, nw_ref, y_ref, cnew_ref, h1_ref):
    bs = xbc_ref.shape[0]
    x_raw = xbc_ref[...]
    conv = convb_ref[...] + x_raw * convw_ref[CONV_K - 1:CONV_K, :]
    for j in range(CONV_K - 1):
        conv = conv + cprev_ref[j] * convw_ref[j:j + 1, :]
    for j in range(CONV_K - 2):
        cnew_ref[j] = cprev_ref[j + 1]
    cnew_ref[CONV_K - 2] = x_raw
    act = _silu(conv)
    xs = act[:, :SSM_WIDTH]
    bm = act[:, SSM_WIDTH:SSM_WIDTH + SSM_GROUPS * SSM_STATE].astype(BF16)
    cm = act[:, SSM_WIDTH + SSM_GROUPS * SSM_STATE:].astype(BF16)
    lane = lax.broadcasted_iota(jnp.int32, (1, LANES), 1)
    a_neg = jnp.where(lane < SSM_HEADS, -jnp.exp(alog_ref[...]), 0.0)
    dt = dt_ref[...]
    dec = jnp.exp(dt * a_neg)
    xdt = xs * _lane_bcast_pairs(dt, HEAD_PAIRS)
    rowid = lax.broadcasted_iota(jnp.int32, (bs, LANES), 0)
    row_lo = lax.broadcasted_iota(jnp.int32, (LANES, SSM_STATE), 0) < SSM_HEAD_DIM
    ys = []
    for j in range(HEAD_PAIRS):
        g = j // (HEAD_PAIRS // SSM_GROUPS)
        sl = slice(j * LANES, (j + 1) * LANES)
        b_g = bm[:, g * SSM_STATE:(g + 1) * SSM_STATE]
        c_g = cm[:, g * SSM_STATE:(g + 1) * SSM_STATE]
        xdt_p = xdt[:, sl]
        y_p = jnp.zeros((bs, LANES), F32)
        for i in range(bs):
            xi = jnp.where(rowid == i, xdt_p, 0.0).T.astype(BF16)
            d_a = dec[i:i + 1, 2 * j:2 * j + 1]
            d_b = dec[i:i + 1, 2 * j + 1:2 * j + 2]
            decay = jnp.where(row_lo, jnp.broadcast_to(d_a, row_lo.shape), jnp.broadcast_to(d_b, row_lo.shape))
            new = decay * h0_ref[i, j] + _dot(xi, b_g)
            h1_ref[i, j] = new
            y_p = y_p + jnp.where(rowid == i, _dot_nt(c_g, new.astype(BF16)), 0.0)
        ys.append(y_p + dskip_ref[:, sl] * xs[:, sl])
    y = jnp.concatenate(ys, axis=1)
    gated = y * _silu(z_ref[...])
    gw = SSM_WIDTH // SSM_GROUPS
    outs = []
    for g in range(SSM_GROUPS):
        gg = gated[:, g * gw:(g + 1) * gw]
        outs.append(gg * lax.rsqrt(jnp.mean(gg * gg, axis=-1, keepdims=True) + EPS))
    y_ref[...] = (jnp.concatenate(outs, axis=1) * nw_ref[...]).astype(BF16)


def _ssd_sample(xbc, z, dt, cprev_t, h0, w, bs):
    n = xbc.shape[0]
    tok = lambda i: (i, 0)
    const = lambda i: (0, 0)
    full = lambda a: pl.BlockSpec(a.shape, const)
    return pl.pallas_call(
        _ssd_sample_kernel,
        out_shape=(jax.ShapeDtypeStruct((n, SSM_WIDTH), BF16),
                   jax.ShapeDtypeStruct((CONV_K - 1, n, CONV_DIM), F32),
                   jax.ShapeDtypeStruct((n, HEAD_PAIRS, LANES, SSM_STATE), F32)),
        grid=(n // bs,),
        in_specs=[
            pl.BlockSpec((bs, CONV_DIM), tok), pl.BlockSpec((bs, SSM_WIDTH), tok),
            pl.BlockSpec((bs, LANES), tok),
            pl.BlockSpec((CONV_K - 1, bs, CONV_DIM), lambda i: (0, i, 0)),
            pl.BlockSpec((bs, HEAD_PAIRS, LANES, SSM_STATE), lambda i: (i, 0, 0, 0)),
            full(w["conv_w"]), full(w["conv_b"]), full(w["a_log"]),
            full(w["d_skip"]), full(w["ssm_norm"]),
        ],
        out_specs=(pl.BlockSpec((bs, SSM_WIDTH), tok),
                   pl.BlockSpec((CONV_K - 1, bs, CONV_DIM), lambda i: (0, i, 0)),
                   pl.BlockSpec((bs, HEAD_PAIRS, LANES, SSM_STATE), lambda i: (i, 0, 0, 0))),
        compiler_params=pltpu.CompilerParams(dimension_semantics=("parallel",),
                                             vmem_limit_bytes=VMEM_LIMIT),
        name="ssd_sample",
    )(xbc, z, dt, cprev_t, h0, w["conv_w"], w["conv_b"], w["a_log"], w["d_skip"], w["ssm_norm"])


def _pack_bf16_pair(v):
    c = v.shape[1] // 2
    hi = lax.bitcast_convert_type(v[:, :c].astype(BF16).astype(F32), jnp.uint32)
    lo = lax.bitcast_convert_type(v[:, c:].astype(BF16).astype(F32), jnp.uint32)
    return hi | (lo >> 16)


def _unpack_bf16_pair(word):
    a = lax.bitcast_convert_type(word & jnp.uint32(0xFFFF0000), F32)
    b = lax.bitcast_convert_type(word << 16, F32)
    return a, b


def _outproj_router_kernel(x_ref, a_ref, s_ref, wo_ref, n2_ref, wr_ref, br_ref, tri_ref,
                           h_ref, t_ref, route_ref, routet_ref, cnt_ref, carry_ref):
    @pl.when(pl.program_id(0) == 0)
    def _():
        carry_ref[...] = jnp.zeros(carry_ref.shape, F32)

    tm = x_ref.shape[0]
    h_ref[...] = (x_ref[...] + _dot(a_ref[...].astype(BF16), wo_ref[:ATTN_WIDTH, :])
                  + _dot(s_ref[...].astype(BF16), wo_ref[ATTN_WIDTH:, :]))
    sub = min(tm, ROUTER_SUBTILE)
    carry = carry_ref[:, 0:1]
    for s in range(tm // sub):
        rows = slice(s * sub, (s + 1) * sub)
        carry = _route_rows(h_ref[rows, :], n2_ref, wr_ref, br_ref, tri_ref, carry, t_ref.at[rows, :],
                            route_ref.at[rows, :], routet_ref.at[:, rows])
    carry_ref[...] = jnp.broadcast_to(carry, carry_ref.shape)
    cpad = jnp.concatenate([jnp.zeros((EXPERT_LANE0, LANES), F32), jnp.broadcast_to(carry, (N_EXPERTS, LANES)),
                            jnp.zeros((LANES - EXPERT_LANE0 - N_EXPERTS, LANES), F32)], axis=0)
    cnt_ref[...] = cpad.T[0:cnt_ref.shape[0], :]


def _route_rows(h, n2_ref, wrt_ref, brc_ref, triu_ref, carry, t_ref, route_ref, routet_ref):
    ms = jnp.mean(h * h, axis=-1, keepdims=True)
    t = h * lax.rsqrt(ms + EPS) * n2_ref[...]
    t_ref[...] = _pack_bf16_pair(t)
    n = h.shape[0]
    a = _dot_nt(wrt_ref[...], t.astype(BF16))

    def logit_rows(r0, r1):
        return a[r0:r1] + a[LANES + r0:LANES + r1] + brc_ref[r0:r1, :]

    sl = EXP_PER_GROUP
    glog = logit_rows(0, sl)
    elog = logit_rows(EXPERT_LANE0, EXPERT_LANE0 + N_EXPERTS)
    row = lax.broadcasted_iota(jnp.int32, (sl, n), 0).astype(F32)
    big = float(sl)
    ninf = -jnp.inf
    gm = jnp.where(row < N_EGROUPS, glog, ninf)
    gmax = jnp.max(gm, axis=0, keepdims=True)
    g_top = 1.0 / jnp.sum(jnp.exp(gm - gmax), axis=0, keepdims=True)
    g_idx = jnp.min(jnp.where(gm == gmax, row, big), axis=0, keepdims=True)
    ml = elog[(N_EGROUPS - 1) * sl:]
    for g in range(N_EGROUPS - 2, -1, -1):
        ml = jnp.where(g_idx == float(g), elog[g * sl:(g + 1) * sl], ml)
    m1 = jnp.max(ml, axis=0, keepdims=True)
    i1 = jnp.min(jnp.where(ml == m1, row, big), axis=0, keepdims=True)
    ml2 = jnp.where(row == i1, ninf, ml)
    m2 = jnp.max(ml2, axis=0, keepdims=True)
    i2 = jnp.min(jnp.where(ml2 == m2, row, big), axis=0, keepdims=True)
    r = jnp.exp(m2 - m1)
    w1 = g_top / (1.0 + r)
    w2 = g_top * r / (1.0 + r)
    e1 = g_idx * float(sl) + i1
    e2 = g_idx * float(sl) + i2
    pick = jnp.where(jnp.logical_or(row == i1, row == i2), 1.0, 0.0)
    onehot = jnp.concatenate([jnp.where(g_idx == float(g), pick, 0.0) for g in range(N_EGROUPS)], axis=0)
    onehot_bf = onehot.astype(BF16)
    cb = triu_ref.shape[0]
    cums = []
    for blk in range(n // cb):
        c = _dot(onehot_bf[:, blk * cb:(blk + 1) * cb], triu_ref[...]) + carry
        carry = c[:, cb - 1:cb]
        cums.append(c)
    before = jnp.concatenate(cums, axis=1) - onehot
    erow = lax.broadcasted_iota(jnp.int32, (N_EXPERTS, n), 0).astype(F32)
    rank1 = jnp.sum(jnp.where(erow == e1, before, 0.0), axis=0, keepdims=True)
    rank2 = jnp.sum(jnp.where(erow == e2, before, 0.0), axis=0, keepdims=True)
    fields = jnp.concatenate([e1, e2, w1, w2, rank1, rank2], axis=0)
    routet_ref[0:fields.shape[0], :] = fields
    routet_ref[fields.shape[0]:, :] = jnp.zeros((routet_ref.shape[0] - fields.shape[0], n), F32)
    if n % LANES:
        fields = jnp.concatenate([fields, jnp.zeros((fields.shape[0], LANES - n % LANES), F32)], axis=1)
    pad = jnp.zeros((LANES - fields.shape[0], LANES), F32)
    for j in range(fields.shape[1] // LANES):
        blk_rows = jnp.concatenate([fields[:, j * LANES:(j + 1) * LANES], pad], axis=0)
        n_valid = min(LANES, n - j * LANES)
        route_ref[j * LANES:j * LANES + n_valid, :] = blk_rows.T[:n_valid, :]
    return carry


def _outproj_router(x2d, attn, ssm, w, tm):
    t = x2d.shape[0]
    tok = lambda i: (i, 0)
    const = lambda i: (0, 0)
    full = lambda a: pl.BlockSpec(a.shape, const)
    idx = jnp.arange(min(tm, COUNT_BLOCK))
    tri = (idx[:, None] <= idx[None, :]).astype(BF16)
    return pl.pallas_call(
        _outproj_router_kernel,
        out_shape=(jax.ShapeDtypeStruct((t, D_MODEL), F32), jax.ShapeDtypeStruct((t, D_MODEL // 2), jnp.uint32),
                   jax.ShapeDtypeStruct((t, LANES), F32), jax.ShapeDtypeStruct((8, t), F32),
                   jax.ShapeDtypeStruct((8, LANES), F32)),
        grid=(t // tm,),
        in_specs=[
            pl.BlockSpec((tm, D_MODEL), tok), pl.BlockSpec((tm, ATTN_WIDTH), tok),
            pl.BlockSpec((tm, SSM_WIDTH), tok),
            full(w["w_out"]), full(w["norm2"]), full(w["wrt"]),
            full(w["br_col"]), full(tri),
        ],
        out_specs=(pl.BlockSpec((tm, D_MODEL), tok), pl.BlockSpec((tm, D_MODEL // 2), tok),
                   pl.BlockSpec((tm, LANES), tok), pl.BlockSpec((8, tm), lambda i: (0, i)),
                   pl.BlockSpec((8, LANES), const)),
        scratch_shapes=[pltpu.VMEM((N_EXPERTS, LANES), F32)],
        compiler_params=pltpu.CompilerParams(dimension_semantics=("arbitrary",),
                                             vmem_limit_bytes=VMEM_LIMIT),
        name="outproj_router",
    )(x2d, attn, ssm, w["w_out"], w["norm2"], w["wrt"], w["br_col"], tri)


def _expert_hidden(t_a, t_b, w1):
    half = D_MODEL // 2
    gu = _dot(t_a, w1[:half]) + _dot(t_b, w1[half:])
    return _silu(gu[:, :EXPERT_FF]) * gu[:, EXPERT_FF:]


def _gate_up_bf16(wg_ref, wu_ref):
    return jnp.concatenate([wg_ref[0].astype(BF16), wu_ref[0].astype(BF16)], axis=1)


def _moe_dense_kernel(h_ref, t_ref, route_ref, wg_ref, wu_ref, wd_ref, y_ref):
    e = pl.program_id(0)

    @pl.when(e == 0)
    def _():
        y_ref[...] = h_ref[...]

    t_a, t_b = _unpack_bf16_pair(t_ref[...])
    route = route_ref[...]
    e1, e2, g1, g2 = route[:, 0:1], route[:, 1:2], route[:, 2:3], route[:, 3:4]
    e_f = e.astype(F32)
    hid = _expert_hidden(t_a.astype(BF16), t_b.astype(BF16), _gate_up_bf16(wg_ref, wu_ref))
    c_e = jnp.where(e1 == e_f, g1, 0.0) + jnp.where(e2 == e_f, g2, 0.0)
    y_ref[...] += _dot((hid * c_e).astype(BF16), wd_ref[0].astype(BF16))


def _moe_dense(h, t, route, w):
    n = h.shape[0]
    whole = lambda e: (0, 0)
    by_expert = lambda e: (e, 0, 0)
    return pl.pallas_call(
        _moe_dense_kernel,
        out_shape=jax.ShapeDtypeStruct((n, D_MODEL), F32),
        grid=(N_EXPERTS,),
        in_specs=[pl.BlockSpec((n, D_MODEL), whole), pl.BlockSpec((n, D_MODEL // 2), whole),
                  pl.BlockSpec((n, LANES), whole),
                  pl.BlockSpec((1, D_MODEL, EXPERT_FF), by_expert), pl.BlockSpec((1, D_MODEL, EXPERT_FF), by_expert),
                  pl.BlockSpec((1, EXPERT_FF, D_MODEL), by_expert)],
        out_specs=pl.BlockSpec((n, D_MODEL), whole),
        compiler_params=pltpu.CompilerParams(dimension_semantics=("arbitrary",),
                                             vmem_limit_bytes=VMEM_LIMIT),
        name="moe_dense",
    )(h, t, route, w["w_gate"], w["w_up"], w["w_down"])


def _sc_scatter_rows(src, pos1, pos2, n_out):
    t, width = src.shape
    rows_per_worker = t // SC_WORKERS
    n_chunks = rows_per_worker // SC_CHUNK
    assert t == SC_WORKERS * SC_CHUNK * n_chunks and n_chunks % 2 == 0
    mesh = plsc.VectorSubcoreMesh(core_axis_name="c", subcore_axis_name="s")

    @functools.partial(
        pl.kernel, mesh=mesh,
        out_type=jax.ShapeDtypeStruct((n_out, width), src.dtype),
        scratch_types=[pltpu.VMEM((2, SC_CHUNK), jnp.int32), pltpu.VMEM((2, SC_CHUNK), jnp.int32),
                       pltpu.VMEM((2, SC_CHUNK, width), src.dtype),
                       pltpu.SemaphoreType.DMA, pltpu.SemaphoreType.DMA,
                       pltpu.SemaphoreType.DMA, pltpu.SemaphoreType.DMA],
    )
    def scatter_kernel(src_hbm, p1_hbm, p2_hbm, out_hbm, i1_v, i2_v, rows_v, l0, l1, s0, s1):
        wid = lax.axis_index("s") * SC_CORES + lax.axis_index("c")
        base = wid * rows_per_worker
        lsem = (l0, l1)
        ssem = (s0, s1)

        def load_copy(c, slot):
            off = pl.multiple_of(base + c * SC_CHUNK, 8)
            return pltpu.make_async_copy(src_hbm.at[pl.ds(off, SC_CHUNK)], rows_v.at[slot], lsem[slot])

        def start_load(c, slot):
            off = pl.multiple_of(base + c * SC_CHUNK, 8)
            pltpu.sync_copy(p1_hbm.at[pl.ds(off, SC_CHUNK)], i1_v.at[slot])
            pltpu.sync_copy(p2_hbm.at[pl.ds(off, SC_CHUNK)], i2_v.at[slot])
            load_copy(c, slot).start()

        def scatter_copies(slot):
            return (pltpu.make_async_copy(rows_v.at[slot], out_hbm.at[i1_v.at[slot]], ssem[slot]),
                    pltpu.make_async_copy(rows_v.at[slot], out_hbm.at[i2_v.at[slot]], ssem[slot]))

        def start_scatter(slot):
            for cp in scatter_copies(slot):
                cp.start()

        def wait_scatter(slot):
            for cp in scatter_copies(slot):
                cp.wait()

        start_load(0, 0)

        @pl.loop(0, n_chunks, step=2)
        def _(c):
            @pl.when(c > 0)
            def _():
                wait_scatter(1)

            start_load(c + 1, 1)
            load_copy(c, 0).wait()
            start_scatter(0)
            load_copy(c + 1, 1).wait()
            wait_scatter(0)
            start_scatter(1)

            @pl.when(c + 2 < n_chunks)
            def _():
                start_load(c + 2, 0)

        wait_scatter(1)

    return scatter_kernel(src, pos1, pos2)


def _sc_gather_rows(table, idx):
    n, width = idx.shape[0], table.shape[1]
    rows_per_worker = n // SC_WORKERS
    n_chunks = rows_per_worker // SC_CHUNK
    assert n == SC_WORKERS * SC_CHUNK * n_chunks and n_chunks % 2 == 0
    mesh = plsc.VectorSubcoreMesh(core_axis_name="c", subcore_axis_name="s")

    @functools.partial(
        pl.kernel, mesh=mesh,
        out_type=jax.ShapeDtypeStruct((n, width), table.dtype),
        scratch_types=[pltpu.VMEM((2, SC_CHUNK), jnp.int32), pltpu.VMEM((2, SC_CHUNK, width), table.dtype),
                       pltpu.SemaphoreType.DMA, pltpu.SemaphoreType.DMA,
                       pltpu.SemaphoreType.DMA, pltpu.SemaphoreType.DMA],
    )
    def gather_kernel(table_hbm, idx_hbm, out_hbm, idx_v, rows_v, g0, g1, w0, w1):
        wid = lax.axis_index("s") * SC_CORES + lax.axis_index("c")
        base = wid * rows_per_worker
        gsem = (g0, g1)
        wsem = (w0, w1)

        def gather_copy(slot):
            return pltpu.make_async_copy(table_hbm.at[idx_v.at[slot]], rows_v.at[slot], gsem[slot])

        def write_copy(c, slot):
            off = pl.multiple_of(base + c * SC_CHUNK, 8)
            return pltpu.make_async_copy(rows_v.at[slot], out_hbm.at[pl.ds(off, SC_CHUNK)], wsem[slot])

        def start_gather(c, slot):
            off = pl.multiple_of(base + c * SC_CHUNK, 8)
            pltpu.sync_copy(idx_hbm.at[pl.ds(off, SC_CHUNK)], idx_v.at[slot])
            gather_copy(slot).start()

        start_gather(0, 0)

        @pl.loop(0, n_chunks, step=2)
        def _(c):
            @pl.when(c > 0)
            def _():
                write_copy(c - 1, 1).wait()

            start_gather(c + 1, 1)
            gather_copy(0).wait()
            write_copy(c, 0).start()
            gather_copy(1).wait()
            write_copy(c + 1, 1).start()
            write_copy(c, 0).wait()

            @pl.when(c + 2 < n_chunks)
            def _():
                start_gather(c + 2, 0)

        write_copy(n_chunks - 1, 1).wait()

    return gather_kernel(table, idx)


def _moe_grouped_kernel(te_ref, nt_ref, order_ref, x_ref, wg_ref, wu_ref, wd_ref, o_ref, w1_bf_ref, w2_bf_ref):
    del order_ref
    i = pl.program_id(0)

    @pl.when(jnp.logical_or(i == 0, te_ref[i] != te_ref[jnp.maximum(i - 1, 0)]))
    def _():
        w1_bf_ref[...] = _gate_up_bf16(wg_ref, wu_ref)
        w2_bf_ref[...] = wd_ref[0].astype(BF16)

    @pl.when(i < nt_ref[0])
    def _():
        for s in range(MOE_TILE // MOE_SUBTILE):
            rows = slice(s * MOE_SUBTILE, (s + 1) * MOE_SUBTILE)
            t_a, t_b = _unpack_bf16_pair(x_ref[rows, :])
            hid = _expert_hidden(t_a.astype(BF16), t_b.astype(BF16), w1_bf_ref[...])
            o_ref[rows, :] = _pack_bf16_pair(_dot(hid.astype(BF16), w2_bf_ref[...]))


def _moe_grouped(xs, tile_expert, n_tiles, order, w):
    rows = xs.shape[0]
    row = lambda i, te, nt, od: (jnp.minimum(i, nt[0] - 1), 0)
    by_expert = lambda i, te, nt, od: (te[i], 0, 0)
    return pl.pallas_call(
        _moe_grouped_kernel,
        out_shape=jax.ShapeDtypeStruct((rows, D_MODEL // 2), jnp.uint32),
        grid_spec=pltpu.PrefetchScalarGridSpec(
            num_scalar_prefetch=3,
            grid=(rows // MOE_TILE,),
            in_specs=[pl.BlockSpec((MOE_TILE, D_MODEL // 2), row),
                      pl.BlockSpec((1, D_MODEL, EXPERT_FF), by_expert),
                      pl.BlockSpec((1, D_MODEL, EXPERT_FF), by_expert),
                      pl.BlockSpec((1, EXPERT_FF, D_MODEL), by_expert)],
            out_specs=pl.BlockSpec((MOE_TILE, D_MODEL // 2), row),
            scratch_shapes=[pltpu.VMEM((D_MODEL, 2 * EXPERT_FF), BF16), pltpu.VMEM((EXPERT_FF, D_MODEL), BF16)],
        ),
        compiler_params=pltpu.CompilerParams(dimension_semantics=("arbitrary",),
                                             vmem_limit_bytes=VMEM_LIMIT),
        name="moe_grouped",
    )(tile_expert, n_tiles, order, xs, w["w_gate"], w["w_up"], w["w_down"])


def _moe_combine_kernel(h_ref, z1_ref, z2_ref, route_ref, y_ref):
    route = route_ref[...]
    g1, g2 = route[:, 2:3], route[:, 3:4]
    half = D_MODEL // 2
    a1, b1 = _unpack_bf16_pair(z1_ref[...])
    a2, b2 = _unpack_bf16_pair(z2_ref[...])
    y_ref[:, :half] = h_ref[:, :half] + g1 * a1 + g2 * a2
    y_ref[:, half:] = h_ref[:, half:] + g1 * b1 + g2 * b2


def _moe_combine(h, z, route, tm):
    t = h.shape[0]
    nb = t // tm
    tok = lambda i: (i, 0)
    return pl.pallas_call(
        _moe_combine_kernel,
        out_shape=jax.ShapeDtypeStruct((t, D_MODEL), F32),
        grid=(nb,),
        in_specs=[pl.BlockSpec((tm, D_MODEL), tok), pl.BlockSpec((tm, D_MODEL // 2), tok),
                  pl.BlockSpec((tm, D_MODEL // 2), lambda i: (i + nb, 0)), pl.BlockSpec((tm, LANES), tok)],
        out_specs=pl.BlockSpec((tm, D_MODEL), tok),
        compiler_params=pltpu.CompilerParams(dimension_semantics=("parallel",),
                                             vmem_limit_bytes=VMEM_LIMIT),
        name="moe_combine",
    )(h, z, z, route)


def _route_pos_kernel(routet_ref, cnt_ref, upper_ref, pos_ref):
    tm = routet_ref.shape[1]
    cnt = cnt_ref[...]
    padded = jnp.floor((cnt + float(MOE_TILE - 1)) * (1.0 / MOE_TILE)) * float(MOE_TILE)
    p_hi, p_mid, p_lo = _split3(padded)
    upper = upper_ref[...]
    starts = (_dot(p_hi, upper) + _dot(p_mid, upper) + _dot(p_lo, upper))[0:1, :]
    starts_col = jnp.broadcast_to(starts, (LANES, LANES)).T[EXPERT_LANE0:EXPERT_LANE0 + N_EXPERTS, 0:1]
    erow = lax.broadcasted_iota(jnp.int32, (N_EXPERTS, tm), 0).astype(F32)
    for k in range(2):
        e_k = routet_ref[k:k + 1, :]
        pos = (jnp.sum(jnp.where(erow == e_k, starts_col, 0.0), axis=0, keepdims=True)
               + routet_ref[4 + k:5 + k, :]).astype(jnp.int32)
        for r in range(tm // LANES):
            pos_ref[k, r:r + 1, :] = pos[:, r * LANES:(r + 1) * LANES]


def _route_positions(routet, counts, tm):
    t = routet.shape[1]
    idx = jnp.arange(LANES)
    upper = (idx[:, None] < idx[None, :]).astype(BF16)
    return pl.pallas_call(
        _route_pos_kernel,
        out_shape=jax.ShapeDtypeStruct((2, t // LANES, LANES), jnp.int32),
        grid=(t // tm,),
        in_specs=[pl.BlockSpec((8, tm), lambda i: (0, i)), pl.BlockSpec((8, LANES), lambda i: (0, 0)),
                  pl.BlockSpec((LANES, LANES), lambda i: (0, 0))],
        out_specs=pl.BlockSpec((2, tm // LANES, LANES), lambda i: (0, i, 0)),
        compiler_params=pltpu.CompilerParams(dimension_semantics=("parallel",),
                                             vmem_limit_bytes=VMEM_LIMIT),
        name="route_positions",
    )(routet, counts, upper)


def _moe_routed(h, t_packed, route, routet, counts, w, tm, run_first):
    t = h.shape[0]
    pos = _route_positions(routet, counts, min(32 * LANES, t))
    pos1 = pos[0].reshape(t)
    pos2 = pos[1].reshape(t)
    cnt = counts[0, EXPERT_LANE0:EXPERT_LANE0 + N_EXPERTS].astype(jnp.int32)
    padded = (cnt + MOE_TILE - 1) // MOE_TILE * MOE_TILE
    ends = jnp.cumsum(padded)
    n_rows = 2 * t + N_EXPERTS * MOE_TILE
    n_tiles = ends[N_EXPERTS - 1] // MOE_TILE
    tile_start = jnp.arange(n_rows // MOE_TILE, dtype=jnp.int32) * MOE_TILE
    tile_start = jnp.minimum(tile_start, ends[N_EXPERTS - 1] - MOE_TILE)
    tile_expert = jnp.sum((tile_start[:, None] >= ends[None, :]).astype(jnp.int32), axis=1)
    xs = _sc_scatter_rows(t_packed, pos1, pos2, n_rows)
    order = lax.bitcast_convert_type(run_first.reshape(-1)[:1].astype(F32), jnp.int32)
    out = _moe_grouped(xs, tile_expert, n_tiles.reshape(1), order, w)
    z = _sc_gather_rows(out, pos.reshape(2 * t))
    return _moe_combine(h, z, route, tm)


def _pad_lanes(a, width=LANES):
    return jnp.pad(a, ((0, 0), (0, width - a.shape[1])))


def _prep_weights(norm1, w_in, q_norm, k_norm, conv_w, conv_b, dt_bias, a_log, d_skip, ssm_norm, w_out,
                  norm2, w_grp, b_grp, w_exp, b_exp, w_gate, w_up, w_down):
    w = {}
    w["norm1"] = norm1.reshape(1, D_MODEL)
    w["w_in"] = _pad_lanes(w_in, XBC_END + LANES).astype(BF16)
    w["qkn"] = jnp.concatenate([jnp.tile(q_norm, N_HEADS), jnp.tile(k_norm, KV_HEADS)]).reshape(1, QK_WIDTH)
    head_of_col = jnp.arange(QK_WIDTH) // HEAD_DIM
    red = (head_of_col[:, None] == jnp.arange(LANES)[None, :])
    w["red"] = red.astype(BF16)
    w["exp"] = red.T.astype(BF16)
    w["conv_w"] = conv_w
    w["conv_b"] = conv_b.reshape(1, CONV_DIM)
    w["dt_bias"] = _pad_lanes(dt_bias.reshape(1, SSM_HEADS))
    w["a_log"] = _pad_lanes(a_log.reshape(1, SSM_HEADS))
    w["d_skip"] = jnp.repeat(d_skip, SSM_HEAD_DIM).reshape(1, SSM_WIDTH)
    w["ssm_norm"] = ssm_norm.reshape(1, SSM_WIDTH)
    idx = jnp.arange(SSD_CHUNK)
    w["tri"] = (idx[None, :] <= idx[:, None]).astype(BF16)
    lane_head = jnp.arange(SSM_WIDTH) // SSM_HEAD_DIM
    w["expand"] = (jnp.arange(LANES)[:, None] == lane_head[None, :]).astype(BF16)
    w["w_out"] = w_out.astype(BF16)
    w["norm2"] = norm2.reshape(1, D_MODEL)
    wr = jnp.zeros((D_MODEL, LANES), F32)
    wr = wr.at[:, :N_EGROUPS].set(w_grp).at[:, EXPERT_LANE0:EXPERT_LANE0 + N_EXPERTS].set(w_exp)
    wr_hi = wr.astype(BF16)
    w["wrt"] = jnp.concatenate([wr_hi, (wr - wr_hi.astype(F32)).astype(BF16)], axis=1).T
    br = jnp.zeros((LANES, 1), F32)
    w["br_col"] = br.at[:N_EGROUPS, 0].set(b_grp).at[EXPERT_LANE0:EXPERT_LANE0 + N_EXPERTS, 0].set(b_exp)
    w["w_gate"], w["w_up"], w["w_down"] = w_gate, w_up, w_down
    return w


def _rope_tables(pos):
    inv = 1.0 / (ROPE_THETA ** (jnp.arange(0, HEAD_DIM, 2, dtype=F32) / HEAD_DIM))
    ang = pos.astype(F32)[:, None] * inv[None, :]
    cos, sin = jnp.cos(ang), jnp.sin(ang)
    reps = LANES // HEAD_DIM
    return (jnp.tile(jnp.concatenate([cos, cos], axis=-1), (1, reps)),
            jnp.tile(jnp.concatenate([-sin, sin], axis=-1), (1, reps)))


def _token_tile(t):
    for tm in (1024, 512, 256, 128, 64, 32, 16):
        if t % tm == 0:
            return tm
    raise ValueError(f"token count {t} is not a multiple of 16")


def kernel(x_prompt, x_sample, cache_win_k, cache_win_v, state_conv, state_ssm, norm1, w_in, q_norm, k_norm,
           sinks, conv_w, conv_b, dt_bias, a_log, d_skip, ssm_norm, w_out, norm2, w_grp, b_grp, w_exp, b_exp,
           w_gate, w_up, w_down):
    depth = norm1.shape[0]
    assert depth == 1, "single-layer stack"
    bp, lp, _ = x_prompt.shape
    bsn, ls, _ = x_sample.shape
    assert ls == 1 and lp % WINDOW == 0 and cache_win_k.shape[2] == WINDOW
    l = 0
    w = _prep_weights(norm1[l], w_in[l], q_norm[l], k_norm[l], conv_w[l], conv_b[l], dt_bias[l], a_log[l],
                      d_skip[l], ssm_norm[l], w_out[l], norm2[l], w_grp[l], b_grp[l], w_exp[l], b_exp[l],
                      w_gate[l], w_up[l], w_down[l])
    sink = sinks[l]

    tp = bp * lp
    xp = x_prompt.reshape(tp, D_MODEL)
    tm_p = _token_tile(lp)
    cos_p, sin_p = _rope_tables(jnp.arange(lp, dtype=jnp.int32))
    q, k, v, z, xbc, dt, k2, v2 = _inproj(xp, w, cos_p, sin_p, tm_p, lp // tm_p)
    attn = _attn_prompt(q, k2, v2, sink, bp, lp)
    ssm, st_p = _ssd_prompt(xbc, z, dt, w, bp, lp)
    h, t, route, routet, counts = _outproj_router(xp, attn, ssm, w, tm_p)
    k3 = k.reshape(bp, lp, KV_HEADS, HEAD_DIM)
    v3 = v.reshape(bp, lp, KV_HEADS, HEAD_DIM)
    win_k_p = k3[:, lp - WINDOW:][None]
    win_v_p = v3[:, lp - WINDOW:][None]
    conv_p = xbc.reshape(bp, lp, CONV_DIM)[:, lp - (CONV_K - 1):][None]
    ssm_p = st_p.reshape(1, bp, SSM_HEADS, SSM_HEAD_DIM, SSM_STATE)

    xs2 = x_sample.reshape(bsn, D_MODEL)
    tm_s = _token_tile(bsn)
    cos_s, sin_s = _rope_tables(jnp.full((tm_s,), PAST_LEN, jnp.int32))
    q_s, k_s, v_s, z_s, xbc_s, dt_s, _, _ = _inproj(xs2, w, cos_s, sin_s, tm_s, 1)
    q4 = q_s.reshape(bsn, KV_HEADS, N_HEADS // KV_HEADS, HEAD_DIM)
    zq = jnp.zeros_like(q4[:, 0])
    qx = jnp.concatenate([jnp.concatenate([q4[:, 0], zq], axis=-1),
                          jnp.concatenate([zq, q4[:, 1]], axis=-1)], axis=1)
    qx = jnp.pad(qx, ((0, 0), (0, BF16_ROWS - N_HEADS), (0, 0)))
    sink_x = jnp.pad(jnp.broadcast_to(sink[:, None], (N_HEADS, LANES)), ((0, BF16_ROWS - N_HEADS), (0, 0)))
    kc = cache_win_k[l].reshape(bsn, WINDOW, KV_WIDTH)
    vc = cache_win_v[l].reshape(bsn, WINDOW, KV_WIDTH)
    ko, vo, attn_s = _attn_sample(qx, kc, k_s.reshape(bsn, 1, KV_WIDTH), vc, v_s.reshape(bsn, 1, KV_WIDTH),
                               sink_x, 8)
    y_prompt = _moe_routed(h, t, route, routet, counts, w, tm_p, attn_s).reshape(bp, lp, D_MODEL)
    cprev_t = jnp.transpose(state_conv[l], (1, 0, 2))
    h0 = state_ssm[l].reshape(bsn, HEAD_PAIRS, LANES, SSM_STATE)
    ssm_s, cnew_t, h1 = _ssd_sample(xbc_s, z_s, dt_s, cprev_t, h0, w, 16)
    h_s, t_s, route_s, _, _ = _outproj_router(xs2, attn_s, ssm_s, w, tm_s)
    y_sample = _moe_dense(h_s, t_s, route_s, w).reshape(bsn, 1, D_MODEL)
    win_k_s = ko.reshape(1, bsn, WINDOW, KV_HEADS, HEAD_DIM)
    win_v_s = vo.reshape(1, bsn, WINDOW, KV_HEADS, HEAD_DIM)
    conv_s = jnp.transpose(cnew_t, (1, 0, 2))[None]
    ssm_s_state = h1.reshape(1, bsn, SSM_HEADS, SSM_HEAD_DIM, SSM_STATE)

    return (y_prompt, y_sample, win_k_p, win_v_p, conv_p, ssm_p, win_k_s, win_v_s, conv_s, ssm_s_state)
```

```python
import functools
import math

import jax
import jax.numpy as jnp
from jax import lax
from jax.experimental import pallas as pl
from jax.experimental.pallas import tpu as pltpu
from jax.experimental.pallas import tpu_sc as plsc

F32 = jnp.float32
BF16 = jnp.bfloat16

D_MODEL = 1024
HEAD_DIM = 64
N_HEADS = 8
KV_HEADS = 2
WINDOW = 128
ATTN_WIDTH = N_HEADS * HEAD_DIM
QK_WIDTH = ATTN_WIDTH + KV_HEADS * HEAD_DIM
KV_WIDTH = KV_HEADS * HEAD_DIM
ATTN_SCALE = HEAD_DIM ** -0.5
ROPE_THETA = 10000.0
SSM_WIDTH = 512
SSM_HEADS = 8
SSM_HEAD_DIM = 64
SSM_GROUPS = 2
SSM_STATE = 128
CONV_K = 4
CONV_HALO = 8
CONV_DIM = SSM_WIDTH + 2 * SSM_GROUPS * SSM_STATE
SSD_CHUNK = 128
N_EGROUPS = 4
EXP_PER_GROUP = 8
N_EXPERTS = 32
EXPERT_FF = 128
EPS = 1e-6
PAST_LEN = 16384

LANES = 128
BF16_ROWS = 16
HEAD_PAIRS = SSM_HEADS // 2
EXPERT_LANE0 = 32
VMEM_LIMIT = 56 * 1024 * 1024
MOE_TILE = 1024
MOE_SUBTILE = 128
ATTN_QBLOCKS = 8
SSD_CHUNKS_PER_STEP = 8
COUNT_BLOCK = 256
ROUTER_SUBTILE = 1024
INPROJ_SUBTILE = 512
SC_CORES = 2
SC_SUBCORES = 16
SC_WORKERS = SC_CORES * SC_SUBCORES
SC_CHUNK = 64

Q_END = ATTN_WIDTH
K_END = Q_END + KV_WIDTH
V_END = K_END + KV_WIDTH
Z_END = V_END + SSM_WIDTH
XBC_END = Z_END + CONV_DIM


def _dot(a, b):
    return jnp.dot(a, b, preferred_element_type=F32)


def _dot_nt(a, b):
    return lax.dot_general(a, b, (((1,), (1,)), ((), ())), preferred_element_type=F32)


def _split2(v):
    hi = v.astype(BF16)
    lo = (v - hi.astype(F32)).astype(BF16)
    return hi, lo


def _split3(v):
    hi = v.astype(BF16)
    r = v - hi.astype(F32)
    mid = r.astype(BF16)
    lo = (r - mid.astype(F32)).astype(BF16)
    return hi, mid, lo


def _silu(x):
    return x * jax.nn.sigmoid(x)


def _softplus(x):
    return jnp.maximum(x, 0.0) + jnp.log1p(jnp.exp(-jnp.abs(x)))


def _lane_bcast_pairs(v, n_pairs):
    r = v.shape[0]
    lo = lax.broadcasted_iota(jnp.int32, (r, LANES), 1) < HEAD_DIM
    slabs = []
    for j in range(n_pairs):
        a = jnp.broadcast_to(v[:, 2 * j:2 * j + 1], (r, LANES))
        b = jnp.broadcast_to(v[:, 2 * j + 1:2 * j + 2], (r, LANES))
        slabs.append(jnp.where(lo, a, b))
    return jnp.concatenate(slabs, axis=1)


def _causal_conv_silu(x_ext, convw_ref, convb_ref):
    halo = CONV_HALO
    x_raw = x_ext[halo:, :]
    conv = convb_ref[...] + x_raw * convw_ref[CONV_K - 1:CONV_K, :]
    for j in range(CONV_K - 1):
        shifted = pltpu.roll(x_ext, CONV_K - 1 - j, axis=0)[halo:, :]
        conv = conv + shifted * convw_ref[j:j + 1, :]
    return _silu(conv)


def _inproj_kernel(x_ref, n1_ref, win_ref, dtb_ref, qkn_ref,
                   cos_ref, sin_ref, red_ref, exp_ref,
                   q_ref, k_ref, v_ref, z_ref, xbc_ref, dt_ref, k2_ref, v2_ref):
    tm = x_ref.shape[0]
    sub = min(tm, INPROJ_SUBTILE)
    lane = lax.broadcasted_iota(jnp.int32, (sub, LANES), 1)
    first_half = (lane % HEAD_DIM) < (HEAD_DIM // 2)
    for s in range(tm // sub):
        rows = slice(s * sub, (s + 1) * sub)
        x = x_ref[rows, :]
        ms = jnp.mean(x * x, axis=-1, keepdims=True)
        xn = (x * lax.rsqrt(ms + EPS) * n1_ref[...]).astype(BF16)
        v = _dot(xn, win_ref[:, K_END:V_END])
        v_ref[rows, :] = v
        v2_ref[rows, :] = _pair_operands(v)
        z_ref[rows, :] = _dot(xn, win_ref[:, V_END:Z_END])
        xbc_ref[rows, :] = _dot(xn, win_ref[:, Z_END:XBC_END])
        dt_ref[rows, :] = _softplus(_dot(xn, win_ref[:, XBC_END:]) + dtb_ref[...])
        qk = _dot(xn, win_ref[:, :K_END])
        ss = _dot((qk * qk).astype(BF16), red_ref[...])
        inv = lax.rsqrt(ss * (1.0 / HEAD_DIM) + EPS)
        inv_hi, inv_lo = _split2(inv)
        inv_x = _dot(inv_hi, exp_ref[...]) + _dot(inv_lo, exp_ref[...])
        qkn = qk * inv_x * qkn_ref[...]
        cos = cos_ref[rows, :]
        sin = sin_ref[rows, :]
        for c in range(QK_WIDTH // LANES):
            xc = qkn[:, c * LANES:(c + 1) * LANES]
            partner = jnp.where(first_half,
                                pltpu.roll(xc, LANES - HEAD_DIM // 2, axis=1),
                                pltpu.roll(xc, HEAD_DIM // 2, axis=1))
            rot = xc * cos + partner * sin
            if c < ATTN_WIDTH // LANES:
                q_ref[rows, c * LANES:(c + 1) * LANES] = (rot * ATTN_SCALE).astype(BF16)
            else:
                k_ref[rows, :] = rot
                k2_ref[rows, :] = _pair_operands(rot)


def _inproj(x2d, w, cos_tab, sin_tab, tm, n_pos_blocks):
    t = x2d.shape[0]
    grid = (t // tm,)
    tok = lambda i: (i, 0)
    const = lambda i: (0, 0)
    pos = lambda i: (i % n_pos_blocks, 0)
    full = lambda a: pl.BlockSpec(a.shape, const)
    rows = lambda width, dtype: (jax.ShapeDtypeStruct((t, width), dtype), pl.BlockSpec((tm, width), tok))
    outs = [rows(ATTN_WIDTH, BF16), rows(KV_WIDTH, F32), rows(KV_WIDTH, F32), rows(SSM_WIDTH, F32)]
    operands = [x2d, w["norm1"], w["w_in"], w["dt_bias"], w["qkn"],
                cos_tab, sin_tab, w["red"], w["exp"]]
    in_specs = [
        pl.BlockSpec((tm, D_MODEL), tok),
        full(w["norm1"]), full(w["w_in"]), full(w["dt_bias"]), full(w["qkn"]),
        pl.BlockSpec((tm, LANES), pos), pl.BlockSpec((tm, LANES), pos),
        full(w["red"]), full(w["exp"]),
    ]
    outs += [rows(CONV_DIM, F32), rows(LANES, F32), rows(4 * LANES, BF16), rows(4 * LANES, BF16)]
    return pl.pallas_call(
        _inproj_kernel,
        out_shape=tuple(o[0] for o in outs),
        grid=grid,
        in_specs=in_specs,
        out_specs=tuple(o[1] for o in outs),
        compiler_params=pltpu.CompilerParams(dimension_semantics=("parallel",),
                                             vmem_limit_bytes=VMEM_LIMIT),
        name="inproj",
    )(*operands)


def _pair_operands(kv):
    lo = lax.broadcasted_iota(jnp.int32, kv.shape, 1) < HEAD_DIM
    swapped = pltpu.roll(kv, HEAD_DIM, axis=1)
    parts = [jnp.where(lo, kv, 0.0), jnp.where(lo, 0.0, swapped), jnp.where(lo, swapped, 0.0), jnp.where(lo, 0.0, kv)]
    return jnp.concatenate(parts, axis=1).astype(BF16)


def _pair_rhs(blk, g):
    return jnp.concatenate([blk[:, 2 * g * LANES:(2 * g + 1) * LANES],
                            blk[:, (2 * g + 1) * LANES:(2 * g + 2) * LANES]], axis=0)


def _attn_qblock(sink_ref, q_blk, k_prev, k_cur, v_prev, v_cur, seq_start, o_ref):
    blk = WINDOW
    qi = lax.broadcasted_iota(jnp.int32, (blk, 2 * blk), 0)
    kj = lax.broadcasted_iota(jnp.int32, (blk, 2 * blk), 1) % blk
    cur_ok = kj <= qi
    lo = lax.broadcasted_iota(jnp.int32, (blk, LANES), 1) < HEAD_DIM
    n_pairs = N_HEADS // KV_HEADS // 2
    for g in range(KV_HEADS):
        q_all = jnp.concatenate([q_blk[:, (g * n_pairs + r) * LANES:(g * n_pairs + r + 1) * LANES]
                                 for r in range(n_pairs)], axis=0)
        s_all = _dot_nt(q_all, jnp.concatenate([_pair_rhs(k_cur, g), _pair_rhs(k_prev, g)], axis=0))
        p_rows = []
        den_rows = []
        for r in range(n_pairs):
            pair = g * n_pairs + r
            s_cur = s_all[r * blk:(r + 1) * blk, :2 * blk]
            s_prev = s_all[r * blk:(r + 1) * blk, 2 * blk:]
            s = jnp.where(cur_ok, s_cur, s_prev)
            if seq_start is not None:
                s = jnp.where(jnp.logical_or(cur_ok, jnp.logical_not(seq_start)), s, -jnp.inf)
            ps = []
            dens = []
            for hh in range(2):
                sink = sink_ref[2 * pair + hh]
                sh = s[:, hh * blk:(hh + 1) * blk]
                m = jnp.maximum(jnp.max(sh, axis=-1, keepdims=True), sink)
                p = jnp.exp(sh - m)
                dens.append(jnp.sum(p, axis=-1, keepdims=True) + jnp.exp(sink - m))
                ps.append(p)
            p2 = jnp.concatenate(ps, axis=1)
            p_rows.append(jnp.concatenate([jnp.where(cur_ok, p2, 0.0), jnp.where(cur_ok, 0.0, p2)],
                                          axis=1).astype(BF16))
            den_rows.append(jnp.where(lo, dens[0], dens[1]))
        o_all = _dot(jnp.concatenate(p_rows, axis=0),
                     jnp.concatenate([_pair_rhs(v_cur, g), _pair_rhs(v_prev, g)], axis=0))
        for r in range(n_pairs):
            pair = g * n_pairs + r
            o2 = o_all[r * blk:(r + 1) * blk, :]
            o_ref[:, pair * LANES:(pair + 1) * LANES] = (o2 / den_rows[r]).astype(BF16)


def _attn_kernel(sink_ref, q_ref, kc_ref, kp_ref, vc_ref, vp_ref, o_ref):
    blk = WINDOW
    first_step = pl.program_id(1) == 0
    for u in range(q_ref.shape[0] // blk):
        rows = slice(u * blk, (u + 1) * blk)
        prev_rows = slice((u - 1) * blk, u * blk)
        k_prev = kp_ref[...] if u == 0 else kc_ref[prev_rows, :]
        v_prev = vp_ref[...] if u == 0 else vc_ref[prev_rows, :]
        _attn_qblock(sink_ref, q_ref[rows, :], k_prev, kc_ref[rows, :], v_prev, vc_ref[rows, :],
                     first_step if u == 0 else None, o_ref.at[rows, :])


def _attn_prompt(q, k, v, sinks, batch, seq):
    n_sub = ATTN_QBLOCKS if seq % (ATTN_QBLOCKS * WINDOW) == 0 else 1
    rows = n_sub * WINDOW
    nb = seq // rows
    cur = lambda b, j, s: (b * nb + j, 0)
    prev = lambda b, j, s: (jnp.maximum((b * nb + j) * n_sub - 1, 0), 0)
    return pl.pallas_call(
        _attn_kernel,
        out_shape=jax.ShapeDtypeStruct((batch * seq, ATTN_WIDTH), BF16),
        grid_spec=pltpu.PrefetchScalarGridSpec(
            num_scalar_prefetch=1,
            grid=(batch, nb),
            in_specs=[
                pl.BlockSpec((rows, ATTN_WIDTH), cur),
                pl.BlockSpec((rows, 4 * LANES), cur), pl.BlockSpec((WINDOW, 4 * LANES), prev),
                pl.BlockSpec((rows, 4 * LANES), cur), pl.BlockSpec((WINDOW, 4 * LANES), prev),
            ],
            out_specs=pl.BlockSpec((rows, ATTN_WIDTH), cur),
        ),
        compiler_params=pltpu.CompilerParams(dimension_semantics=("parallel", "parallel"),
                                             vmem_limit_bytes=VMEM_LIMIT),
        name="attn_prompt",
    )(sinks, q, k, k, v, v)


def _ssd_kernel(xbc_ref, z_ref, dt_ref, convw_ref, convb_ref, alog_ref, dskip_ref, nw_ref,
                tri_ref, expand_ref, y_ref, st_ref, buf_ref, state_ref):
    c = pl.program_id(1)
    cl = SSD_CHUNK
    n_sub = xbc_ref.shape[0] // cl
    halo = CONV_HALO

    @pl.when(c == 0)
    def _():
        buf_ref[...] = jnp.zeros(buf_ref.shape, F32)
        state_ref[...] = jnp.zeros(state_ref.shape, F32)

    lane = lax.broadcasted_iota(jnp.int32, (1, LANES), 1)
    a_neg = jnp.where(lane < SSM_HEADS, -jnp.exp(alog_ref[...]), 0.0)
    tri = tri_ref[...]
    for u in range(n_sub):
        rows = slice(u * cl, (u + 1) * cl)
        if u == 0:
            x_ext = jnp.concatenate([buf_ref[...], xbc_ref[rows, :]], axis=0)
        else:
            x_ext = xbc_ref[u * cl - halo:(u + 1) * cl, :]
        _ssd_chunk(x_ext, z_ref[rows, :], dt_ref[rows, :], a_neg, tri, convw_ref, convb_ref, dskip_ref, nw_ref,
                   expand_ref, y_ref.at[rows, :], state_ref)
    buf_ref[...] = xbc_ref[n_sub * cl - halo:n_sub * cl, :]

    @pl.when(c == pl.num_programs(1) - 1)
    def _():
        st_ref[0] = state_ref[...]


def _ssd_chunk(x_ext, z, dt, a_neg, tri, convw_ref, convb_ref, dskip_ref, nw_ref, expand_ref, y_ref, state_ref):
    cl = SSD_CHUNK
    act = _causal_conv_silu(x_ext, convw_ref, convb_ref)
    xs = act[:, :SSM_WIDTH]
    bm = act[:, SSM_WIDTH:SSM_WIDTH + SSM_GROUPS * SSM_STATE].astype(BF16)
    cm = act[:, SSM_WIDTH + SSM_GROUPS * SSM_STATE:].astype(BF16)

    dta = dt * a_neg
    p_hi, p_mid, p_lo = _split3(dta)
    a_col = _dot(tri, p_hi) + _dot(tri, p_mid) + _dot(tri, p_lo)
    a_last = a_col[cl - 1:cl, :]
    a_row = a_col.T
    per_head = jnp.concatenate([dt, jnp.exp(a_col), jnp.exp(a_last - a_col)], axis=0)
    ph_hi, ph_lo = _split2(per_head)
    per_lane = _dot(ph_hi, expand_ref[...]) + _dot(ph_lo, expand_ref[...])
    dt_x = per_lane[:cl]
    ecol_x = per_lane[cl:2 * cl]
    dte_x = per_lane[2 * cl:]
    e_last = jnp.exp(a_last)
    xdt = xs * dt_x

    li = lax.broadcasted_iota(jnp.int32, (cl, cl), 0)
    si = lax.broadcasted_iota(jnp.int32, (cl, cl), 1)
    causal = si <= li
    lo = lax.broadcasted_iota(jnp.int32, (cl, LANES), 1) < SSM_HEAD_DIM
    row_lo = lax.broadcasted_iota(jnp.int32, (LANES, SSM_STATE), 0) < SSM_HEAD_DIM

    ys = []
    for g in range(SSM_GROUPS):
        b_g = bm[:, g * SSM_STATE:(g + 1) * SSM_STATE]
        c_g = cm[:, g * SSM_STATE:(g + 1) * SSM_STATE]
        cb = _dot_nt(c_g, b_g)
        for r in range(HEAD_PAIRS // SSM_GROUPS):
            j = g * (HEAD_PAIRS // SSM_GROUPS) + r
            sl = slice(j * LANES, (j + 1) * LANES)
            xdt_p = xdt[:, sl]
            ms = []
            for hh in range(2):
                h = 2 * j + hh
                seg = a_col[:, h:h + 1] - a_row[h:h + 1, :]
                ms.append(cb * jnp.exp(jnp.where(causal, seg, -jnp.inf)))
            m2 = jnp.concatenate(ms, axis=1).astype(BF16)
            rhs = jnp.concatenate([jnp.where(lo, xdt_p, 0.0), jnp.where(lo, 0.0, xdt_p)],
                                  axis=0).astype(BF16)
            y_diag = _dot(m2, rhs)
            st = state_ref[j]
            y_off = _dot_nt(c_g, st.astype(BF16)) * ecol_x[:, sl]
            xdt_e = (xdt_p * dte_x[:, sl]).T.astype(BF16)
            d_a = e_last[:, 2 * j:2 * j + 1]
            d_b = e_last[:, 2 * j + 1:2 * j + 2]
            decay = jnp.where(row_lo, jnp.broadcast_to(d_a, row_lo.shape), jnp.broadcast_to(d_b, row_lo.shape))
            state_ref[j] = decay * st + _dot(xdt_e, b_g)
            ys.append(y_diag + y_off + dskip_ref[:, sl] * xs[:, sl])
    y = jnp.concatenate(ys, axis=1)
    gated = y * _silu(z)
    gw = SSM_WIDTH // SSM_GROUPS
    outs = []
    for g in range(SSM_GROUPS):
        gg = gated[:, g * gw:(g + 1) * gw]
        outs.append(gg * lax.rsqrt(jnp.mean(gg * gg, axis=-1, keepdims=True) + EPS))
    y_ref[...] = (jnp.concatenate(outs, axis=1) * nw_ref[...]).astype(BF16)


def _ssd_prompt(xbc, z, dt, w, batch, seq):
    n_sub = SSD_CHUNKS_PER_STEP if seq % (SSD_CHUNKS_PER_STEP * SSD_CHUNK) == 0 else 1
    rows = n_sub * SSD_CHUNK
    nc = seq // rows
    tok = lambda b, c: (b * nc + c, 0)
    const = lambda b, c: (0, 0)
    full = lambda a: pl.BlockSpec(a.shape, const)
    return pl.pallas_call(
        _ssd_kernel,
        out_shape=(jax.ShapeDtypeStruct((batch * seq, SSM_WIDTH), BF16),
                   jax.ShapeDtypeStruct((batch, HEAD_PAIRS, LANES, SSM_STATE), F32)),
        grid=(batch, nc),
        in_specs=[
            pl.BlockSpec((rows, CONV_DIM), tok), pl.BlockSpec((rows, SSM_WIDTH), tok),
            pl.BlockSpec((rows, LANES), tok),
            full(w["conv_w"]), full(w["conv_b"]), full(w["a_log"]),
            full(w["d_skip"]), full(w["ssm_norm"]), full(w["tri"]), full(w["expand"]),
        ],
        out_specs=(pl.BlockSpec((rows, SSM_WIDTH), tok),
                   pl.BlockSpec((1, HEAD_PAIRS, LANES, SSM_STATE), lambda b, c: (b, 0, 0, 0))),
        scratch_shapes=[pltpu.VMEM((CONV_HALO, CONV_DIM), F32),
                        pltpu.VMEM((HEAD_PAIRS, LANES, SSM_STATE), F32)],
        compiler_params=pltpu.CompilerParams(dimension_semantics=("parallel", "arbitrary"),
                                             vmem_limit_bytes=VMEM_LIMIT),
        name="ssd_prompt",
    )(xbc, z, dt, w["conv_w"], w["conv_b"], w["a_log"], w["d_skip"], w["ssm_norm"], w["tri"], w["expand"])


def _attn_sample_kernel(qx_ref, kc_ref, kn_ref, vc_ref, vn_ref, sink_ref, ko_ref, vo_ref, o_ref):
    bs = qx_ref.shape[0]
    w = kc_ref.shape[1]
    sink = sink_ref[...]
    lo = lax.broadcasted_iota(jnp.int32, (1, LANES), 1) < HEAD_DIM
    for i in range(bs):
        ko_ref[i, 0:w - 1, :] = kc_ref[i, 1:w, :]
        ko_ref[i, w - 1:w, :] = kn_ref[i]
        vo_ref[i, 0:w - 1, :] = vc_ref[i, 1:w, :]
        vo_ref[i, w - 1:w, :] = vn_ref[i]
        s = _dot_nt(qx_ref[i], ko_ref[i].astype(BF16))
        m = jnp.maximum(jnp.max(s, axis=-1, keepdims=True), sink)
        p = jnp.exp(s - m)
        den = jnp.sum(p, axis=-1, keepdims=True) + jnp.exp(sink - m)
        o = _dot(p.astype(BF16), vo_ref[i].astype(BF16)) / den
        o_sw = pltpu.roll(o, HEAD_DIM, axis=1)
        for j in range(N_HEADS // 2):
            a, b = (o, o_sw) if j < N_HEADS // 4 else (o_sw, o)
            o_ref[i:i + 1, j * LANES:(j + 1) * LANES] = jnp.where(lo, a[2 * j:2 * j + 1], b[2 * j + 1:2 * j + 2])


def _attn_sample(qx, kc, kn, vc, vn, sink_x, bs):
    n, w = kc.shape[0], kc.shape[1]
    blk3 = lambda i: (i, 0, 0)
    return pl.pallas_call(
        _attn_sample_kernel,
        out_shape=(jax.ShapeDtypeStruct((n, w, KV_WIDTH), F32),
                   jax.ShapeDtypeStruct((n, w, KV_WIDTH), F32),
                   jax.ShapeDtypeStruct((n, ATTN_WIDTH), F32)),
        grid=(n // bs,),
        in_specs=[
            pl.BlockSpec((bs, BF16_ROWS, LANES), blk3),
            pl.BlockSpec((bs, w, KV_WIDTH), blk3), pl.BlockSpec((bs, 1, KV_WIDTH), blk3),
            pl.BlockSpec((bs, w, KV_WIDTH), blk3), pl.BlockSpec((bs, 1, KV_WIDTH), blk3),
            pl.BlockSpec(sink_x.shape, lambda i: (0, 0)),
        ],
        out_specs=(pl.BlockSpec((bs, w, KV_WIDTH), blk3), pl.BlockSpec((bs, w, KV_WIDTH), blk3),
                   pl.BlockSpec((bs, ATTN_WIDTH), lambda i: (i, 0))),
        compiler_params=pltpu.CompilerParams(dimension_semantics=("parallel",),
                                             vmem_limit_bytes=VMEM_LIMIT),
        name="attn_sample",
    )(qx, kc, kn, vc, vn, sink_x)


def _ssd_sample_kernel(xbc_ref, z_ref, dt_ref, cprev_ref, h0_ref, convw_ref, convb_ref, alog_ref,
                       dskip_ref, nw_ref, y_ref, cnew_ref, h1_ref):
    bs = xbc_ref.shape[0]
    x_raw = xbc_ref[...]
    conv = convb_ref[...] + x_raw * convw_ref[CONV_K - 1:CONV_K, :]
    for j in range(CONV_K - 1):
        conv = conv + cprev_ref[j] * convw_ref[j:j + 1, :]
    for j in range(CONV_K - 2):
        cnew_ref[j] = cprev_ref[j + 1]
    cnew_ref[CONV_K - 2] = x_raw
    act = _silu(conv)
    xs = act[:, :SSM_WIDTH]
    bm = act[:, SSM_WIDTH:SSM_WIDTH + SSM_GROUPS * SSM_STATE].astype(BF16)
    cm = act[:, SSM_WIDTH + SSM_GROUPS * SSM_STATE:].astype(BF16)
    lane = lax.broadcasted_iota(jnp.int32, (1, LANES), 1)
    a_neg = jnp.where(lane < SSM_HEADS, -jnp.exp(alog_ref[...]), 0.0)
    dt = dt_ref[...]
    dec = jnp.exp(dt * a_neg)
    xdt = xs * _lane_bcast_pairs(dt, HEAD_PAIRS)
    rowid = lax.broadcasted_iota(jnp.int32, (bs, LANES), 0)
    row_lo = lax.broadcasted_iota(jnp.int32, (LANES, SSM_STATE), 0) < SSM_HEAD_DIM
    ys = []
    for j in range(HEAD_PAIRS):
        g = j // (HEAD_PAIRS // SSM_GROUPS)
        sl = slice(j * LANES, (j + 1) * LANES)
        b_g = bm[:, g * SSM_STATE:(g + 1) * SSM_STATE]
        c_g = cm[:, g * SSM_STATE:(g + 1) * SSM_STATE]
        xdt_p = xdt[:, sl]
        y_p = jnp.zeros((bs, LANES), F32)
        for i in range(bs):
            xi = jnp.where(rowid == i, xdt_p, 0.0).T.astype(BF16)
            d_a = dec[i:i + 1, 2 * j:2 * j + 1]
            d_b = dec[i:i + 1, 2 * j + 1:2 * j + 2]
            decay = jnp.where(row_lo, jnp.broadcast_to(d_a, row_lo.shape), jnp.broadcast_to(d_b, row_lo.shape))
            new = decay * h0_ref[i, j] + _dot(xi, b_g)
            h1_ref[i, j] = new
            y_p = y_p + jnp.where(rowid == i, _dot_nt(c_g, new.astype(BF16)), 0.0)
        ys.append(y_p + dskip_ref[:, sl] * xs[:, sl])
    y = jnp.concatenate(ys, axis=1)
    gated = y * _silu(z_ref[...])
    gw = SSM_WIDTH // SSM_GROUPS
    outs = []
    for g in range(SSM_GROUPS):
        gg = gated[:, g * gw:(g + 1) * gw]
        outs.append(gg * lax.rsqrt(jnp.mean(gg * gg, axis=-1, keepdims=True) + EPS))
    y_ref[...] = (jnp.concatenate(outs, axis=1) * nw_ref[...]).astype(BF16)


def _ssd_sample(xbc, z, dt, cprev_t, h0, w, bs):
    n = xbc.shape[0]
    tok = lambda i: (i, 0)
    const = lambda i: (0, 0)
    full = lambda a: pl.BlockSpec(a.shape, const)
    return pl.pallas_call(
        _ssd_sample_kernel,
        out_shape=(jax.ShapeDtypeStruct((n, SSM_WIDTH), BF16),
                   jax.ShapeDtypeStruct((CONV_K - 1, n, CONV_DIM), F32),
                   jax.ShapeDtypeStruct((n, HEAD_PAIRS, LANES, SSM_STATE), F32)),
        grid=(n // bs,),
        in_specs=[
            pl.BlockSpec((bs, CONV_DIM), tok), pl.BlockSpec((bs, SSM_WIDTH), tok),
            pl.BlockSpec((bs, LANES), tok),
            pl.BlockSpec((CONV_K - 1, bs, CONV_DIM), lambda i: (0, i, 0)),
            pl.BlockSpec((bs, HEAD_PAIRS, LANES, SSM_STATE), lambda i: (i, 0, 0, 0)),
            full(w["conv_w"]), full(w["conv_b"]), full(w["a_log"]),
            full(w["d_skip"]), full(w["ssm_norm"]),
        ],
        out_specs=(pl.BlockSpec((bs, SSM_WIDTH), tok),
                   pl.BlockSpec((CONV_K - 1, bs, CONV_DIM), lambda i: (0, i, 0)),
                   pl.BlockSpec((bs, HEAD_PAIRS, LANES, SSM_STATE), lambda i: (i, 0, 0, 0))),
        compiler_params=pltpu.CompilerParams(dimension_semantics=("parallel",),
                                             vmem_limit_bytes=VMEM_LIMIT),
        name="ssd_sample",
    )(xbc, z, dt, cprev_t, h0, w["conv_w"], w["conv_b"], w["a_log"], w["d_skip"], w["ssm_norm"])


def _pack_bf16_pair(v):
    c = v.shape[1] // 2
    hi = lax.bitcast_convert_type(v[:, :c].astype(BF16).astype(F32), jnp.uint32)
    lo = lax.bitcast_convert_type(v[:, c:].astype(BF16).astype(F32), jnp.uint32)
    return hi | (lo >> 16)


def _unpack_bf16_pair(word):
    a = lax.bitcast_convert_type(word & jnp.uint32(0xFFFF0000), F32)
    b = lax.bitcast_convert_type(word << 16, F32)
    return a, b


def _residual_stream(x_ref, a_ref, s_ref, wo_ref):
    return (x_ref[...] + _dot(a_ref[...].astype(BF16), wo_ref[:ATTN_WIDTH, :])
            + _dot(s_ref[...].astype(BF16), wo_ref[ATTN_WIDTH:, :]))


def _outproj_router_kernel(x_ref, a_ref, s_ref, wo_ref, n2_ref, wr_ref, br_ref, tri_ref,
                           t_ref, route_ref, routet_ref, cnt_ref, carry_ref, h_ref):
    @pl.when(pl.program_id(0) == 0)
    def _():
        carry_ref[...] = jnp.zeros(carry_ref.shape, F32)

    tm = x_ref.shape[0]
    h_ref[...] = _residual_stream(x_ref, a_ref, s_ref, wo_ref)
    sub = min(tm, ROUTER_SUBTILE)
    carry = carry_ref[:, 0:1]
    for s in range(tm // sub):
        rows = slice(s * sub, (s + 1) * sub)
        carry = _route_rows(h_ref[rows, :], n2_ref, wr_ref, br_ref, tri_ref, carry, t_ref.at[rows, :],
                            route_ref.at[rows, :], routet_ref.at[:, rows])
    carry_ref[...] = jnp.broadcast_to(carry, carry_ref.shape)
    cpad = jnp.concatenate([jnp.zeros((EXPERT_LANE0, LANES), F32), jnp.broadcast_to(carry, (N_EXPERTS, LANES)),
                            jnp.zeros((LANES - EXPERT_LANE0 - N_EXPERTS, LANES), F32)], axis=0)
    cnt_ref[...] = cpad.T[0:cnt_ref.shape[0], :]


def _route_rows(h, n2_ref, wrt_ref, brc_ref, triu_ref, carry, t_ref, route_ref, routet_ref):
    ms = jnp.mean(h * h, axis=-1, keepdims=True)
    t = h * lax.rsqrt(ms + EPS) * n2_ref[...]
    t_ref[...] = _pack_bf16_pair(t)
    n = h.shape[0]
    a = _dot_nt(wrt_ref[...], t.astype(BF16))

    def logit_rows(r0, r1):
        return a[r0:r1] + a[LANES + r0:LANES + r1] + brc_ref[r0:r1, :]

    sl = EXP_PER_GROUP
    glog = logit_rows(0, sl)
    elog = logit_rows(EXPERT_LANE0, EXPERT_LANE0 + N_EXPERTS)
    row = lax.broadcasted_iota(jnp.int32, (sl, n), 0).astype(F32)
    big = float(sl)
    ninf = -jnp.inf
    gm = jnp.where(row < N_EGROUPS, glog, ninf)
    gmax = jnp.max(gm, axis=0, keepdims=True)
    g_top = 1.0 / jnp.sum(jnp.exp(gm - gmax), axis=0, keepdims=True)
    g_idx = jnp.min(jnp.where(gm == gmax, row, big), axis=0, keepdims=True)
    ml = elog[(N_EGROUPS - 1) * sl:]
    for g in range(N_EGROUPS - 2, -1, -1):
        ml = jnp.where(g_idx == float(g), elog[g * sl:(g + 1) * sl], ml)
    m1 = jnp.max(ml, axis=0, keepdims=True)
    i1 = jnp.min(jnp.where(ml == m1, row, big), axis=0, keepdims=True)
    ml2 = jnp.where(row == i1, ninf, ml)
    m2 = jnp.max(ml2, axis=0, keepdims=True)
    i2 = jnp.min(jnp.where(ml2 == m2, row, big), axis=0, keepdims=True)
    r = jnp.exp(m2 - m1)
    w1 = g_top / (1.0 + r)
    w2 = g_top * r / (1.0 + r)
    e1 = g_idx * float(sl) + i1
    e2 = g_idx * float(sl) + i2
    pick = jnp.where(jnp.logical_or(row == i1, row == i2), 1.0, 0.0)
    onehot = jnp.concatenate([jnp.where(g_idx == float(g), pick, 0.0) for g in range(N_EGROUPS)], axis=0)
    onehot_bf = onehot.astype(BF16)
    cb = triu_ref.shape[0]
    cums = []
    for blk in range(n // cb):
        c = _dot(onehot_bf[:, blk * cb:(blk + 1) * cb], triu_ref[...]) + carry
        carry = c[:, cb - 1:cb]
        cums.append(c)
    before = jnp.concatenate(cums, axis=1) - onehot
    erow = lax.broadcasted_iota(jnp.int32, (N_EXPERTS, n), 0).astype(F32)
    rank1 = jnp.sum(jnp.where(erow == e1, before, 0.0), axis=0, keepdims=True)
    rank2 = jnp.sum(jnp.where(erow == e2, before, 0.0), axis=0, keepdims=True)
    fields = jnp.concatenate([e1, e2, w1, w2, rank1, rank2], axis=0)
    routet_ref[0:fields.shape[0], :] = fields
    routet_ref[fields.shape[0]:, :] = jnp.zeros((routet_ref.shape[0] - fields.shape[0], n), F32)
    if n % LANES:
        fields = jnp.concatenate([fields, jnp.zeros((fields.shape[0], LANES - n % LANES), F32)], axis=1)
    pad = jnp.zeros((LANES - fields.shape[0], LANES), F32)
    for j in range(fields.shape[1] // LANES):
        blk_rows = jnp.concatenate([fields[:, j * LANES:(j + 1) * LANES], pad], axis=0)
        n_valid = min(LANES, n - j * LANES)
        route_ref[j * LANES:j * LANES + n_valid, :] = blk_rows.T[:n_valid, :]
    return carry


def _outproj_router(x2d, attn, ssm, w, tm):
    t = x2d.shape[0]
    tok = lambda i: (i, 0)
    const = lambda i: (0, 0)
    full = lambda a: pl.BlockSpec(a.shape, const)
    idx = jnp.arange(min(tm, COUNT_BLOCK))
    tri = (idx[:, None] <= idx[None, :]).astype(BF16)
    return pl.pallas_call(
        _outproj_router_kernel,
        out_shape=(jax.ShapeDtypeStruct((t, D_MODEL // 2), jnp.uint32),
                   jax.ShapeDtypeStruct((t, LANES), F32), jax.ShapeDtypeStruct((8, t), F32),
                   jax.ShapeDtypeStruct((8, LANES), F32)),
        grid=(t // tm,),
        in_specs=[
            pl.BlockSpec((tm, D_MODEL), tok), pl.BlockSpec((tm, ATTN_WIDTH), tok),
            pl.BlockSpec((tm, SSM_WIDTH), tok),
            full(w["w_out"]), full(w["norm2"]), full(w["wrt"]),
            full(w["br_col"]), full(tri),
        ],
        out_specs=(pl.BlockSpec((tm, D_MODEL // 2), tok),
                   pl.BlockSpec((tm, LANES), tok), pl.BlockSpec((8, tm), lambda i: (0, i)),
                   pl.BlockSpec((8, LANES), const)),
        scratch_shapes=[pltpu.VMEM((N_EXPERTS, LANES), F32), pltpu.VMEM((tm, D_MODEL), F32)],
        compiler_params=pltpu.CompilerParams(dimension_semantics=("arbitrary",),
                                             vmem_limit_bytes=VMEM_LIMIT),
        name="outproj_router",
    )(x2d, attn, ssm, w["w_out"], w["norm2"], w["wrt"], w["br_col"], tri)


def _expert_hidden(t_a, t_b, w1):
    half = D_MODEL // 2
    gu = _dot(t_a, w1[:half]) + _dot(t_b, w1[half:])
    return _silu(gu[:, :EXPERT_FF]) * gu[:, EXPERT_FF:]


def _gate_up_bf16(wg_ref, wu_ref):
    return jnp.concatenate([wg_ref[0].astype(BF16), wu_ref[0].astype(BF16)], axis=1)


def _moe_dense_kernel(x_ref, a_ref, s_ref, wo_ref, t_ref, route_ref, wg_ref, wu_ref, wd_ref, y_ref):
    e = pl.program_id(0)

    @pl.when(e == 0)
    def _():
        y_ref[...] = _residual_stream(x_ref, a_ref, s_ref, wo_ref)

    t_a, t_b = _unpack_bf16_pair(t_ref[...])
    route = route_ref[...]
    e1, e2, g1, g2 = route[:, 0:1], route[:, 1:2], route[:, 2:3], route[:, 3:4]
    e_f = e.astype(F32)
    hid = _expert_hidden(t_a.astype(BF16), t_b.astype(BF16), _gate_up_bf16(wg_ref, wu_ref))
    c_e = jnp.where(e1 == e_f, g1, 0.0) + jnp.where(e2 == e_f, g2, 0.0)
    y_ref[...] += _dot((hid * c_e).astype(BF16), wd_ref[0].astype(BF16))


def _moe_dense(x2d, attn, ssm, t, route, w):
    n = x2d.shape[0]
    whole = lambda e: (0, 0)
    by_expert = lambda e: (e, 0, 0)
    return pl.pallas_call(
        _moe_dense_kernel,
        out_shape=jax.ShapeDtypeStruct((n, D_MODEL), F32),
        grid=(N_EXPERTS,),
        in_specs=[pl.BlockSpec((n, D_MODEL), whole), pl.BlockSpec((n, ATTN_WIDTH), whole),
                  pl.BlockSpec((n, SSM_WIDTH), whole), pl.BlockSpec(w["w_out"].shape, whole),
                  pl.BlockSpec((n, D_MODEL // 2), whole), pl.BlockSpec((n, LANES), whole),
                  pl.BlockSpec((1, D_MODEL, EXPERT_FF), by_expert), pl.BlockSpec((1, D_MODEL, EXPERT_FF), by_expert),
                  pl.BlockSpec((1, EXPERT_FF, D_MODEL), by_expert)],
        out_specs=pl.BlockSpec((n, D_MODEL), whole),
        compiler_params=pltpu.CompilerParams(dimension_semantics=("arbitrary",),
                                             vmem_limit_bytes=VMEM_LIMIT),
        name="moe_dense",
    )(x2d, attn, ssm, w["w_out"], t, route, w["w_gate"], w["w_up"], w["w_down"])


def _sc_scatter_rows(src, pos1, pos2, n_out):
    t, width = src.shape
    rows_per_worker = t // SC_WORKERS
    n_chunks = rows_per_worker // SC_CHUNK
    assert t == SC_WORKERS * SC_CHUNK * n_chunks and n_chunks % 2 == 0
    mesh = plsc.VectorSubcoreMesh(core_axis_name="c", subcore_axis_name="s")

    @functools.partial(
        pl.kernel, mesh=mesh,
        out_type=jax.ShapeDtypeStruct((n_out, width), src.dtype),
        scratch_types=[pltpu.VMEM((2, SC_CHUNK), jnp.int32), pltpu.VMEM((2, SC_CHUNK), jnp.int32),
                       pltpu.VMEM((2, SC_CHUNK, width), src.dtype),
                       pltpu.SemaphoreType.DMA, pltpu.SemaphoreType.DMA,
                       pltpu.SemaphoreType.DMA, pltpu.SemaphoreType.DMA],
    )
    def scatter_kernel(src_hbm, p1_hbm, p2_hbm, out_hbm, i1_v, i2_v, rows_v, l0, l1, s0, s1):
        wid = lax.axis_index("s") * SC_CORES + lax.axis_index("c")
        base = wid * rows_per_worker
        lsem = (l0, l1)
        ssem = (s0, s1)

        def load_copy(c, slot):
            off = pl.multiple_of(base + c * SC_CHUNK, 8)
            return pltpu.make_async_copy(src_hbm.at[pl.ds(off, SC_CHUNK)], rows_v.at[slot], lsem[slot])

        def start_load(c, slot):
            off = pl.multiple_of(base + c * SC_CHUNK, 8)
            pltpu.sync_copy(p1_hbm.at[pl.ds(off, SC_CHUNK)], i1_v.at[slot])
            pltpu.sync_copy(p2_hbm.at[pl.ds(off, SC_CHUNK)], i2_v.at[slot])
            load_copy(c, slot).start()

        def scatter_copies(slot):
            return (pltpu.make_async_copy(rows_v.at[slot], out_hbm.at[i1_v.at[slot]], ssem[slot]),
                    pltpu.make_async_copy(rows_v.at[slot], out_hbm.at[i2_v.at[slot]], ssem[slot]))

        def start_scatter(slot):
            for cp in scatter_copies(slot):
                cp.start()

        def wait_scatter(slot):
            for cp in scatter_copies(slot):
                cp.wait()

        start_load(0, 0)

        @pl.loop(0, n_chunks, step=2)
        def _(c):
            @pl.when(c > 0)
            def _():
                wait_scatter(1)

            start_load(c + 1, 1)
            load_copy(c, 0).wait()
            start_scatter(0)
            load_copy(c + 1, 1).wait()
            wait_scatter(0)
            start_scatter(1)

            @pl.when(c + 2 < n_chunks)
            def _():
                start_load(c + 2, 0)

        wait_scatter(1)

    return scatter_kernel(src, pos1, pos2)


def _sc_gather_rows(table, idx):
    n, width = idx.shape[0], table.shape[1]
    rows_per_worker = n // SC_WORKERS
    n_chunks = rows_per_worker // SC_CHUNK
    assert n == SC_WORKERS * SC_CHUNK * n_chunks and n_chunks % 2 == 0
    mesh = plsc.VectorSubcoreMesh(core_axis_name="c", subcore_axis_name="s")

    @functools.partial(
        pl.kernel, mesh=mesh,
        out_type=jax.ShapeDtypeStruct((n, width), table.dtype),
        scratch_types=[pltpu.VMEM((2, SC_CHUNK), jnp.int32), pltpu.VMEM((2, SC_CHUNK, width), table.dtype),
                       pltpu.SemaphoreType.DMA, pltpu.SemaphoreType.DMA,
                       pltpu.SemaphoreType.DMA, pltpu.SemaphoreType.DMA],
    )
    def gather_kernel(table_hbm, idx_hbm, out_hbm, idx_v, rows_v, g0, g1, w0, w1):
        wid = lax.axis_index("s") * SC_CORES + lax.axis_index("c")
        base = wid * rows_per_worker
        gsem = (g0, g1)
        wsem = (w0, w1)

        def gather_copy(slot):
            return pltpu.make_async_copy(table_hbm.at[idx_v.at[slot]], rows_v.at[slot], gsem[slot])

        def write_copy(c, slot):
            off = pl.multiple_of(base + c * SC_CHUNK, 8)
            return pltpu.make_async_copy(rows_v.at[slot], out_hbm.at[pl.ds(off, SC_CHUNK)], wsem[slot])

        def start_gather(c, slot):
            off = pl.multiple_of(base + c * SC_CHUNK, 8)
            pltpu.sync_copy(idx_hbm.at[pl.ds(off, SC_CHUNK)], idx_v.at[slot])
            gather_copy(slot).start()

        start_gather(0, 0)

        @pl.loop(0, n_chunks, step=2)
        def _(c):
            @pl.when(c > 0)
            def _():
                write_copy(c - 1, 1).wait()

            start_gather(c + 1, 1)
            gather_copy(0).wait()
            write_copy(c, 0).start()
            gather_copy(1).wait()
            write_copy(c + 1, 1).start()
            write_copy(c, 0).wait()

            @pl.when(c + 2 < n_chunks)
            def _():
                start_gather(c + 2, 0)

        write_copy(n_chunks - 1, 1).wait()

    return gather_kernel(table, idx)


def _moe_grouped_kernel(te_ref, nt_ref, order_ref, x_ref, wg_ref, wu_ref, wd_ref, o_ref, w1_bf_ref, w2_bf_ref):
    del order_ref
    i = pl.program_id(0)

    @pl.when(jnp.logical_or(i == 0, te_ref[i] != te_ref[jnp.maximum(i - 1, 0)]))
    def _():
        w1_bf_ref[...] = _gate_up_bf16(wg_ref, wu_ref)
        w2_bf_ref[...] = wd_ref[0].astype(BF16)

    @pl.when(i < nt_ref[0])
    def _():
        for s in range(MOE_TILE // MOE_SUBTILE):
            rows = slice(s * MOE_SUBTILE, (s + 1) * MOE_SUBTILE)
            t_a, t_b = _unpack_bf16_pair(x_ref[rows, :])
            hid = _expert_hidden(t_a.astype(BF16), t_b.astype(BF16), w1_bf_ref[...])
            o_ref[rows, :] = _pack_bf16_pair(_dot(hid.astype(BF16), w2_bf_ref[...]))


def _moe_grouped(xs, tile_expert, n_tiles, order, w):
    rows = xs.shape[0]
    row = lambda i, te, nt, od: (jnp.minimum(i, nt[0] - 1), 0)
    by_expert = lambda i, te, nt, od: (te[i], 0, 0)
    return pl.pallas_call(
        _moe_grouped_kernel,
        out_shape=jax.ShapeDtypeStruct((rows, D_MODEL // 2), jnp.uint32),
        grid_spec=pltpu.PrefetchScalarGridSpec(
            num_scalar_prefetch=3,
            grid=(rows // MOE_TILE,),
            in_specs=[pl.BlockSpec((MOE_TILE, D_MODEL // 2), row),
                      pl.BlockSpec((1, D_MODEL, EXPERT_FF), by_expert),
                      pl.BlockSpec((1, D_MODEL, EXPERT_FF), by_expert),
                      pl.BlockSpec((1, EXPERT_FF, D_MODEL), by_expert)],
            out_specs=pl.BlockSpec((MOE_TILE, D_MODEL // 2), row),
            scratch_shapes=[pltpu.VMEM((D_MODEL, 2 * EXPERT_FF), BF16), pltpu.VMEM((EXPERT_FF, D_MODEL), BF16)],
        ),
        compiler_params=pltpu.CompilerParams(dimension_semantics=("arbitrary",),
                                             vmem_limit_bytes=VMEM_LIMIT),
        name="moe_grouped",
    )(tile_expert, n_tiles, order, xs, w["w_gate"], w["w_up"], w["w_down"])


def _moe_combine_kernel(x_ref, a_ref, s_ref, wo_ref, z1_ref, z2_ref, route_ref, y_ref):
    h = _residual_stream(x_ref, a_ref, s_ref, wo_ref)
    route = route_ref[...]
    g1, g2 = route[:, 2:3], route[:, 3:4]
    half = D_MODEL // 2
    a1, b1 = _unpack_bf16_pair(z1_ref[...])
    a2, b2 = _unpack_bf16_pair(z2_ref[...])
    y_ref[:, :half] = h[:, :half] + g1 * a1 + g2 * a2
    y_ref[:, half:] = h[:, half:] + g1 * b1 + g2 * b2


def _moe_combine(x2d, attn, ssm, z, route, w, tm):
    t = x2d.shape[0]
    nb = t // tm
    tok = lambda i: (i, 0)
    return pl.pallas_call(
        _moe_combine_kernel,
        out_shape=jax.ShapeDtypeStruct((t, D_MODEL), F32),
        grid=(nb,),
        in_specs=[pl.BlockSpec((tm, D_MODEL), tok), pl.BlockSpec((tm, ATTN_WIDTH), tok),
                  pl.BlockSpec((tm, SSM_WIDTH), tok), pl.BlockSpec(w["w_out"].shape, lambda i: (0, 0)),
                  pl.BlockSpec((tm, D_MODEL // 2), tok),
                  pl.BlockSpec((tm, D_MODEL // 2), lambda i: (i + nb, 0)), pl.BlockSpec((tm, LANES), tok)],
        out_specs=pl.BlockSpec((tm, D_MODEL), tok),
        compiler_params=pltpu.CompilerParams(dimension_semantics=("parallel",),
                                             vmem_limit_bytes=VMEM_LIMIT),
        name="moe_combine",
    )(x2d, attn, ssm, w["w_out"], z, z, route)


def _route_pos_kernel(routet_ref, cnt_ref, upper_ref, pos_ref):
    tm = routet_ref.shape[1]
    cnt = cnt_ref[...]
    padded = jnp.floor((cnt + float(MOE_TILE - 1)) * (1.0 / MOE_TILE)) * float(MOE_TILE)
    p_hi, p_mid, p_lo = _split3(padded)
    upper = upper_ref[...]
    starts = (_dot(p_hi, upper) + _dot(p_mid, upper) + _dot(p_lo, upper))[0:1, :]
    starts_col = jnp.broadcast_to(starts, (LANES, LANES)).T[EXPERT_LANE0:EXPERT_LANE0 + N_EXPERTS, 0:1]
    erow = lax.broadcasted_iota(jnp.int32, (N_EXPERTS, tm), 0).astype(F32)
    for k in range(2):
        e_k = routet_ref[k:k + 1, :]
        pos = (jnp.sum(jnp.where(erow == e_k, starts_col, 0.0), axis=0, keepdims=True)
               + routet_ref[4 + k:5 + k, :]).astype(jnp.int32)
        for r in range(tm // LANES):
            pos_ref[k, r:r + 1, :] = pos[:, r * LANES:(r + 1) * LANES]


def _route_positions(routet, counts, tm):
    t = routet.shape[1]
    idx = jnp.arange(LANES)
    upper = (idx[:, None] < idx[None, :]).astype(BF16)
    return pl.pallas_call(
        _route_pos_kernel,
        out_shape=jax.ShapeDtypeStruct((2, t // LANES, LANES), jnp.int32),
        grid=(t // tm,),
        in_specs=[pl.BlockSpec((8, tm), lambda i: (0, i)), pl.BlockSpec((8, LANES), lambda i: (0, 0)),
                  pl.BlockSpec((LANES, LANES), lambda i: (0, 0))],
        out_specs=pl.BlockSpec((2, tm // LANES, LANES), lambda i: (0, i, 0)),
        compiler_params=pltpu.CompilerParams(dimension_semantics=("parallel",),
                                             vmem_limit_bytes=VMEM_LIMIT),
        name="route_positions",
    )(routet, counts, upper)


def _moe_routed(x2d, attn, ssm, t_packed, route, routet, counts, w, tm, run_first):
    t = x2d.shape[0]
    pos = _route_positions(routet, counts, min(32 * LANES, t))
    pos1 = pos[0].reshape(t)
    pos2 = pos[1].reshape(t)
    cnt = counts[0, EXPERT_LANE0:EXPERT_LANE0 + N_EXPERTS].astype(jnp.int32)
    padded = (cnt + MOE_TILE - 1) // MOE_TILE * MOE_TILE
    ends = jnp.cumsum(padded)
    n_rows = 2 * t + N_EXPERTS * MOE_TILE
    n_tiles = ends[N_EXPERTS - 1] // MOE_TILE
    tile_start = jnp.arange(n_rows // MOE_TILE, dtype=jnp.int32) * MOE_TILE
    tile_start = jnp.minimum(tile_start, ends[N_EXPERTS - 1] - MOE_TILE)
    tile_expert = jnp.sum((tile_start[:, None] >= ends[None, :]).astype(jnp.int32), axis=1)
    xs = _sc_scatter_rows(t_packed, pos1, pos2, n_rows)
    order = lax.bitcast_convert_type(run_first.reshape(-1)[:1].astype(F32), jnp.int32)
    out = _moe_grouped(xs, tile_expert, n_tiles.reshape(1), order, w)
    z = _sc_gather_rows(out, pos.reshape(2 * t))
    return _moe_combine(x2d, attn, ssm, z, route, w, tm)


def _pad_lanes(a, width=LANES):
    return jnp.pad(a, ((0, 0), (0, width - a.shape[1])))


def _prep_weights(norm1, w_in, q_norm, k_norm, conv_w, conv_b, dt_bias, a_log, d_skip, ssm_norm, w_out,
                  norm2, w_grp, b_grp, w_exp, b_exp, w_gate, w_up, w_down):
    w = {}
    w["norm1"] = norm1.reshape(1, D_MODEL)
    w["w_in"] = _pad_lanes(w_in, XBC_END + LANES).astype(BF16)
    w["qkn"] = jnp.concatenate([jnp.tile(q_norm, N_HEADS), jnp.tile(k_norm, KV_HEADS)]).reshape(1, QK_WIDTH)
    head_of_col = jnp.arange(QK_WIDTH) // HEAD_DIM
    red = (head_of_col[:, None] == jnp.arange(LANES)[None, :])
    w["red"] = red.astype(BF16)
    w["exp"] = red.T.astype(BF16)
    w["conv_w"] = conv_w
    w["conv_b"] = conv_b.reshape(1, CONV_DIM)
    w["dt_bias"] = _pad_lanes(dt_bias.reshape(1, SSM_HEADS))
    w["a_log"] = _pad_lanes(a_log.reshape(1, SSM_HEADS))
    w["d_skip"] = jnp.repeat(d_skip, SSM_HEAD_DIM).reshape(1, SSM_WIDTH)
    w["ssm_norm"] = ssm_norm.reshape(1, SSM_WIDTH)
    idx = jnp.arange(SSD_CHUNK)
    w["tri"] = (idx[None, :] <= idx[:, None]).astype(BF16)
    lane_head = jnp.arange(SSM_WIDTH) // SSM_HEAD_DIM
    w["expand"] = (jnp.arange(LANES)[:, None] == lane_head[None, :]).astype(BF16)
    w["w_out"] = w_out.astype(BF16)
    w["norm2"] = norm2.reshape(1, D_MODEL)
    wr = jnp.zeros((D_MODEL, LANES), F32)
    wr = wr.at[:, :N_EGROUPS].set(w_grp).at[:, EXPERT_LANE0:EXPERT_LANE0 + N_EXPERTS].set(w_exp)
    wr_hi = wr.astype(BF16)
    w["wrt"] = jnp.concatenate([wr_hi, (wr - wr_hi.astype(F32)).astype(BF16)], axis=1).T
    br = jnp.zeros((LANES, 1), F32)
    w["br_col"] = br.at[:N_EGROUPS, 0].set(b_grp).at[EXPERT_LANE0:EXPERT_LANE0 + N_EXPERTS, 0].set(b_exp)
    w["w_gate"], w["w_up"], w["w_down"] = w_gate, w_up, w_down
    return w


def _rope_tables(pos):
    inv = 1.0 / (ROPE_THETA ** (jnp.arange(0, HEAD_DIM, 2, dtype=F32) / HEAD_DIM))
    ang = pos.astype(F32)[:, None] * inv[None, :]
    cos, sin = jnp.cos(ang), jnp.sin(ang)
    reps = LANES // HEAD_DIM
    return (jnp.tile(jnp.concatenate([cos, cos], axis=-1), (1, reps)),
            jnp.tile(jnp.concatenate([-sin, sin], axis=-1), (1, reps)))


def _token_tile(t):
    for tm in (1024, 512, 256, 128, 64, 32, 16):
        if t % tm == 0:
            return tm
    raise ValueError(f"token count {t} is not a multiple of 16")


def kernel(x_prompt, x_sample, cache_win_k, cache_win_v, state_conv, state_ssm, norm1, w_in, q_norm, k_norm,
           sinks, conv_w, conv_b, dt_bias, a_log, d_skip, ssm_norm, w_out, norm2, w_grp, b_grp, w_exp, b_exp,
           w_gate, w_up, w_down):
    depth = norm1.shape[0]
    assert depth == 1, "single-layer stack"
    bp, lp, _ = x_prompt.shape
    bsn, ls, _ = x_sample.shape
    assert ls == 1 and lp % WINDOW == 0 and cache_win_k.shape[2] == WINDOW
    l = 0
    w = _prep_weights(norm1[l], w_in[l], q_norm[l], k_norm[l], conv_w[l], conv_b[l], dt_bias[l], a_log[l],
                      d_skip[l], ssm_norm[l], w_out[l], norm2[l], w_grp[l], b_grp[l], w_exp[l], b_exp[l],
                      w_gate[l], w_up[l], w_down[l])
    sink = sinks[l]

    tp = bp * lp
    xp = x_prompt.reshape(tp, D_MODEL)
    tm_p = _token_tile(lp)
    cos_p, sin_p = _rope_tables(jnp.arange(lp, dtype=jnp.int32))
    q, k, v, z, xbc, dt, k2, v2 = _inproj(xp, w, cos_p, sin_p, tm_p, lp // tm_p)
    attn = _attn_prompt(q, k2, v2, sink, bp, lp)
    ssm, st_p = _ssd_prompt(xbc, z, dt, w, bp, lp)
    t, route, routet, counts = _outproj_router(xp, attn, ssm, w, tm_p)
    k3 = k.reshape(bp, lp, KV_HEADS, HEAD_DIM)
    v3 = v.reshape(bp, lp, KV_HEADS, HEAD_DIM)
    win_k_p = k3[:, lp - WINDOW:][None]
    win_v_p = v3[:, lp - WINDOW:][None]
    conv_p = xbc.reshape(bp, lp, CONV_DIM)[:, lp - (CONV_K - 1):][None]
    ssm_p = st_p.reshape(1, bp, SSM_HEADS, SSM_HEAD_DIM, SSM_STATE)

    xs2 = x_sample.reshape(bsn, D_MODEL)
    tm_s = _token_tile(bsn)
    cos_s, sin_s = _rope_tables(jnp.full((tm_s,), PAST_LEN, jnp.int32))
    q_s, k_s, v_s, z_s, xbc_s, dt_s, _, _ = _inproj(xs2, w, cos_s, sin_s, tm_s, 1)
    q4 = q_s.reshape(bsn, KV_HEADS, N_HEADS // KV_HEADS, HEAD_DIM)
    zq = jnp.zeros_like(q4[:, 0])
    qx = jnp.concatenate([jnp.concatenate([q4[:, 0], zq], axis=-1),
                          jnp.concatenate([zq, q4[:, 1]], axis=-1)], axis=1)
    qx = jnp.pad(qx, ((0, 0), (0, BF16_ROWS - N_HEADS), (0, 0)))
    sink_x = jnp.pad(jnp.broadcast_to(sink[:, None], (N_HEADS, LANES)), ((0, BF16_ROWS - N_HEADS), (0, 0)))
    kc = cache_win_k[l].reshape(bsn, WINDOW, KV_WIDTH)
    vc = cache_win_v[l].reshape(bsn, WINDOW, KV_WIDTH)
    ko, vo, attn_s = _attn_sample(qx, kc, k_s.reshape(bsn, 1, KV_WIDTH), vc, v_s.reshape(bsn, 1, KV_WIDTH),
                               sink_x, 8)
    y_prompt = _moe_routed(xp, attn, ssm, t, route, routet, counts, w, tm_p, attn_s).reshape(bp, lp, D_MODEL)
    cprev_t = jnp.transpose(state_conv[l], (1, 0, 2))
    h0 = state_ssm[l].reshape(bsn, HEAD_PAIRS, LANES, SSM_STATE)
    ssm_s, cnew_t, h1 = _ssd_sample(xbc_s, z_s, dt_s, cprev_t, h0, w, 16)
    t_s, route_s, _, _ = _outproj_router(xs2, attn_s, ssm_s, w, tm_s)
    y_sample = _moe_dense(xs2, attn_s, ssm_s, t_s, route_s, w).reshape(bsn, 1, D_MODEL)
    win_k_s = ko.reshape(1, bsn, WINDOW, KV_HEADS, HEAD_DIM)
    win_v_s = vo.reshape(1, bsn, WINDOW, KV_HEADS, HEAD_DIM)
    conv_s = jnp.transpose(cnew_t, (1, 0, 2))[None]
    ssm_s_state = h1.reshape(1, bsn, SSM_HEADS, SSM_HEAD_DIM, SSM_STATE)

    return (y_prompt, y_sample, win_k_p, win_v_p, conv_p, ssm_p, win_k_s, win_v_s, conv_s, ssm_s_state)
```

```python
import functools
import math

import jax
import jax.numpy as jnp
from jax import lax
from jax.experimental import pallas as pl
from jax.experimental.pallas import tpu as pltpu
from jax.experimental.pallas import tpu_sc as plsc

F32 = jnp.float32
BF16 = jnp.bfloat16

D_MODEL = 1024
HEAD_DIM = 64
N_HEADS = 8
KV_HEADS = 2
WINDOW = 128
ATTN_WIDTH = N_HEADS * HEAD_DIM
QK_WIDTH = ATTN_WIDTH + KV_HEADS * HEAD_DIM
KV_WIDTH = KV_HEADS * HEAD_DIM
ATTN_SCALE = HEAD_DIM ** -0.5
ROPE_THETA = 10000.0
SSM_WIDTH = 512
SSM_HEADS = 8
SSM_HEAD_DIM = 64
SSM_GROUPS = 2
SSM_STATE = 128
CONV_K = 4
CONV_HALO = 8
CONV_DIM = SSM_WIDTH + 2 * SSM_GROUPS * SSM_STATE
SSD_CHUNK = 128
N_EGROUPS = 4
EXP_PER_GROUP = 8
N_EXPERTS = 32
EXPERT_FF = 128
EPS = 1e-6
PAST_LEN = 16384

LANES = 128
BF16_ROWS = 16
HEAD_PAIRS = SSM_HEADS // 2
EXPERT_LANE0 = 32
VMEM_LIMIT = 56 * 1024 * 1024
MOE_TILE = 1024
MOE_SUBTILE = 128
ATTN_QBLOCKS = 8
SSD_CHUNKS_PER_STEP = 8
COUNT_BLOCK = 256
ROUTER_SUBTILE = 1024
INPROJ_SUBTILE = 512
SC_CORES = 2
SC_SUBCORES = 16
SC_WORKERS = SC_CORES * SC_SUBCORES
SC_CHUNK = 64

Q_END = ATTN_WIDTH
K_END = Q_END + KV_WIDTH
V_END = K_END + KV_WIDTH
Z_END = V_END + SSM_WIDTH
XBC_END = Z_END + CONV_DIM


def _dot(a, b):
    return jnp.dot(a, b, preferred_element_type=F32)


def _dot_nt(a, b):
    return lax.dot_general(a, b, (((1,), (1,)), ((), ())), preferred_element_type=F32)


def _split2(v):
    hi = v.astype(BF16)
    lo = (v - hi.astype(F32)).astype(BF16)
    return hi, lo


def _split3(v):
    hi = v.astype(BF16)
    r = v - hi.astype(F32)
    mid = r.astype(BF16)
    lo = (r - mid.astype(F32)).astype(BF16)
    return hi, mid, lo


def _silu(x):
    return x * jax.nn.sigmoid(x)


def _softplus(x):
    return jnp.maximum(x, 0.0) + jnp.log1p(jnp.exp(-jnp.abs(x)))


def _lane_bcast_pairs(v, n_pairs):
    r = v.shape[0]
    lo = lax.broadcasted_iota(jnp.int32, (r, LANES), 1) < HEAD_DIM
    slabs = []
    for j in range(n_pairs):
        a = jnp.broadcast_to(v[:, 2 * j:2 * j + 1], (r, LANES))
        b = jnp.broadcast_to(v[:, 2 * j + 1:2 * j + 2], (r, LANES))
        slabs.append(jnp.where(lo, a, b))
    return jnp.concatenate(slabs, axis=1)


def _causal_conv_silu(x_ext, convw_ref, convb_ref):
    halo = CONV_HALO
    x_raw = x_ext[halo:, :]
    conv = convb_ref[...] + x_raw * convw_ref[CONV_K - 1:CONV_K, :]
    for j in range(CONV_K - 1):
        shifted = pltpu.roll(x_ext, CONV_K - 1 - j, axis=0)[halo:, :]
        conv = conv + shifted * convw_ref[j:j + 1, :]
    return _silu(conv)


def _inproj_kernel(x_ref, n1_ref, win_ref, dtb_ref, qkn_ref,
                   cos_ref, sin_ref, red_ref, exp_ref,
                   q_ref, k_ref, v_ref, z_ref, xbc_ref, dt_ref, k2_ref, v2_ref):
    tm = x_ref.shape[0]
    sub = min(tm, INPROJ_SUBTILE)
    lane = lax.broadcasted_iota(jnp.int32, (sub, LANES), 1)
    first_half = (lane % HEAD_DIM) < (HEAD_DIM // 2)
    for s in range(tm // sub):
        rows = slice(s * sub, (s + 1) * sub)
        x = x_ref[rows, :]
        ms = jnp.mean(x * x, axis=-1, keepdims=True)
        xn = (x * lax.rsqrt(ms + EPS) * n1_ref[...]).astype(BF16)
        v = _dot(xn, win_ref[:, K_END:V_END])
        v_ref[rows, :] = v
        v2_ref[rows, :] = _pair_operands(v)
        z_ref[rows, :] = _dot(xn, win_ref[:, V_END:Z_END])
        xbc_ref[rows, :] = _dot(xn, win_ref[:, Z_END:XBC_END])
        dt_ref[rows, :] = _softplus(_dot(xn, win_ref[:, XBC_END:]) + dtb_ref[...])
        qk = _dot(xn, win_ref[:, :K_END])
        ss = _dot((qk * qk).astype(BF16), red_ref[...])
        inv = lax.rsqrt(ss * (1.0 / HEAD_DIM) + EPS)
        inv_hi, inv_lo = _split2(inv)
        inv_x = _dot(inv_hi, exp_ref[...]) + _dot(inv_lo, exp_ref[...])
        qkn = qk * inv_x * qkn_ref[...]
        cos = cos_ref[rows, :]
        sin = sin_ref[rows, :]
        for c in range(QK_WIDTH // LANES):
            xc = qkn[:, c * LANES:(c + 1) * LANES]
            partner = jnp.where(first_half,
                                pltpu.roll(xc, LANES - HEAD_DIM // 2, axis=1),
                                pltpu.roll(xc, HEAD_DIM // 2, axis=1))
            rot = xc * cos + partner * sin
            if c < ATTN_WIDTH // LANES:
                q_ref[rows, c * LANES:(c + 1) * LANES] = (rot * ATTN_SCALE).astype(BF16)
            else:
                k_ref[rows, :] = rot
                k2_ref[rows, :] = _pair_operands(rot)


def _inproj(x2d, w, cos_tab, sin_tab, tm, n_pos_blocks):
    t = x2d.shape[0]
    grid = (t // tm,)
    tok = lambda i: (i, 0)
    const = lambda i: (0, 0)
    pos = lambda i: (i % n_pos_blocks, 0)
    full = lambda a: pl.BlockSpec(a.shape, const)
    rows = lambda width, dtype: (jax.ShapeDtypeStruct((t, width), dtype), pl.BlockSpec((tm, width), tok))
    outs = [rows(ATTN_WIDTH, BF16), rows(KV_WIDTH, F32), rows(KV_WIDTH, F32), rows(SSM_WIDTH, F32)]
    operands = [x2d, w["norm1"], w["w_in"], w["dt_bias"], w["qkn"],
                cos_tab, sin_tab, w["red"], w["exp"]]
    in_specs = [
        pl.BlockSpec((tm, D_MODEL), tok),
        full(w["norm1"]), full(w["w_in"]), full(w["dt_bias"]), full(w["qkn"]),
        pl.BlockSpec((tm, LANES), pos), pl.BlockSpec((tm, LANES), pos),
        full(w["red"]), full(w["exp"]),
    ]
    outs += [rows(CONV_DIM, F32), rows(LANES, F32), rows(4 * LANES, BF16), rows(4 * LANES, BF16)]
    return pl.pallas_call(
        _inproj_kernel,
        out_shape=tuple(o[0] for o in outs),
        grid=grid,
        in_specs=in_specs,
        out_specs=tuple(o[1] for o in outs),
        compiler_params=pltpu.CompilerParams(dimension_semantics=("parallel",),
                                             vmem_limit_bytes=VMEM_LIMIT),
        name="inproj",
    )(*operands)


def _pair_operands(kv):
    lo = lax.broadcasted_iota(jnp.int32, kv.shape, 1) < HEAD_DIM
    swapped = pltpu.roll(kv, HEAD_DIM, axis=1)
    parts = [jnp.where(lo, kv, 0.0), jnp.where(lo, 0.0, swapped), jnp.where(lo, swapped, 0.0), jnp.where(lo, 0.0, kv)]
    return jnp.concatenate(parts, axis=1).astype(BF16)


def _pair_rhs(blk, g):
    return jnp.concatenate([blk[:, 2 * g * LANES:(2 * g + 1) * LANES],
                            blk[:, (2 * g + 1) * LANES:(2 * g + 2) * LANES]], axis=0)


def _attn_qblock(sink_ref, q_blk, k_prev, k_cur, v_prev, v_cur, seq_start, o_ref):
    blk = WINDOW
    qi = lax.broadcasted_iota(jnp.int32, (blk, 2 * blk), 0)
    kj = lax.broadcasted_iota(jnp.int32, (blk, 2 * blk), 1) % blk
    cur_ok = kj <= qi
    lo = lax.broadcasted_iota(jnp.int32, (blk, LANES), 1) < HEAD_DIM
    n_pairs = N_HEADS // KV_HEADS // 2
    for g in range(KV_HEADS):
        q_all = jnp.concatenate([q_blk[:, (g * n_pairs + r) * LANES:(g * n_pairs + r + 1) * LANES]
                                 for r in range(n_pairs)], axis=0)
        s_all = _dot_nt(q_all, jnp.concatenate([_pair_rhs(k_cur, g), _pair_rhs(k_prev, g)], axis=0))
        p_rows = []
        den_rows = []
        for r in range(n_pairs):
            pair = g * n_pairs + r
            s_cur = s_all[r * blk:(r + 1) * blk, :2 * blk]
            s_prev = s_all[r * blk:(r + 1) * blk, 2 * blk:]
            s = jnp.where(cur_ok, s_cur, s_prev)
            if seq_start is not None:
                s = jnp.where(jnp.logical_or(cur_ok, jnp.logical_not(seq_start)), s, -jnp.inf)
            ps = []
            dens = []
            for hh in range(2):
                sink = sink_ref[2 * pair + hh]
                sh = s[:, hh * blk:(hh + 1) * blk]
                m = jnp.maximum(jnp.max(sh, axis=-1, keepdims=True), sink)
                p = jnp.exp(sh - m)
                dens.append(jnp.sum(p, axis=-1, keepdims=True) + jnp.exp(sink - m))
                ps.append(p)
            p2 = jnp.concatenate(ps, axis=1)
            p_rows.append(jnp.concatenate([jnp.where(cur_ok, p2, 0.0), jnp.where(cur_ok, 0.0, p2)],
                                          axis=1).astype(BF16))
            den_rows.append(jnp.where(lo, dens[0], dens[1]))
        o_all = _dot(jnp.concatenate(p_rows, axis=0),
                     jnp.concatenate([_pair_rhs(v_cur, g), _pair_rhs(v_prev, g)], axis=0))
        for r in range(n_pairs):
            pair = g * n_pairs + r
            o2 = o_all[r * blk:(r + 1) * blk, :]
            o_ref[:, pair * LANES:(pair + 1) * LANES] = (o2 / den_rows[r]).astype(BF16)


def _attn_kernel(sink_ref, q_ref, kc_ref, kp_ref, vc_ref, vp_ref, o_ref):
    blk = WINDOW
    first_step = pl.program_id(1) == 0
    for u in range(q_ref.shape[0] // blk):
        rows = slice(u * blk, (u + 1) * blk)
        prev_rows = slice((u - 1) * blk, u * blk)
        k_prev = kp_ref[...] if u == 0 else kc_ref[prev_rows, :]
        v_prev = vp_ref[...] if u == 0 else vc_ref[prev_rows, :]
        _attn_qblock(sink_ref, q_ref[rows, :], k_prev, kc_ref[rows, :], v_prev, vc_ref[rows, :],
                     first_step if u == 0 else None, o_ref.at[rows, :])


def _attn_prompt(q, k, v, sinks, batch, seq):
    n_sub = ATTN_QBLOCKS if seq % (ATTN_QBLOCKS * WINDOW) == 0 else 1
    rows = n_sub * WINDOW
    nb = seq // rows
    cur = lambda b, j, s: (b * nb + j, 0)
    prev = lambda b, j, s: (jnp.maximum((b * nb + j) * n_sub - 1, 0), 0)
    return pl.pallas_call(
        _attn_kernel,
        out_shape=jax.ShapeDtypeStruct((batch * seq, ATTN_WIDTH), BF16),
        grid_spec=pltpu.PrefetchScalarGridSpec(
            num_scalar_prefetch=1,
            grid=(batch, nb),
            in_specs=[
                pl.BlockSpec((rows, ATTN_WIDTH), cur),
                pl.BlockSpec((rows, 4 * LANES), cur), pl.BlockSpec((WINDOW, 4 * LANES), prev),
                pl.BlockSpec((rows, 4 * LANES), cur), pl.BlockSpec((WINDOW, 4 * LANES), prev),
            ],
            out_specs=pl.BlockSpec((rows, ATTN_WIDTH), cur),
        ),
        compiler_params=pltpu.CompilerParams(dimension_semantics=("parallel", "parallel"),
                                             vmem_limit_bytes=VMEM_LIMIT),
        name="attn_prompt",
    )(sinks, q, k, k, v, v)


def _ssd_kernel(xbc_ref, z_ref, dt_ref, convw_ref, convb_ref, alog_ref, dskip_ref, nw_ref,
                tri_ref, expand_ref, y_ref, st_ref, buf_ref, state_ref):
    c = pl.program_id(1)
    cl = SSD_CHUNK
    n_sub = xbc_ref.shape[0] // cl
    halo = CONV_HALO

    @pl.when(c == 0)
    def _():
        buf_ref[...] = jnp.zeros(buf_ref.shape, F32)
        state_ref[...] = jnp.zeros(state_ref.shape, F32)

    lane = lax.broadcasted_iota(jnp.int32, (1, LANES), 1)
    a_neg = jnp.where(lane < SSM_HEADS, -jnp.exp(alog_ref[...]), 0.0)
    tri = tri_ref[...]
    for u in range(n_sub):
        rows = slice(u * cl, (u + 1) * cl)
        if u == 0:
            x_ext = jnp.concatenate([buf_ref[...], xbc_ref[rows, :]], axis=0)
        else:
            x_ext = xbc_ref[u * cl - halo:(u + 1) * cl, :]
        _ssd_chunk(x_ext, z_ref[rows, :], dt_ref[rows, :], a_neg, tri, convw_ref, convb_ref, dskip_ref, nw_ref,
                   expand_ref, y_ref.at[rows, :], state_ref)
    buf_ref[...] = xbc_ref[n_sub * cl - halo:n_sub * cl, :]

    @pl.when(c == pl.num_programs(1) - 1)
    def _():
        st_ref[0] = state_ref[...]


def _ssd_chunk(x_ext, z, dt, a_neg, tri, convw_ref, convb_ref, dskip_ref, nw_ref, expand_ref, y_ref, state_ref):
    cl = SSD_CHUNK
    act = _causal_conv_silu(x_ext, convw_ref, convb_ref)
    xs = act[:, :SSM_WIDTH]
    bm = act[:, SSM_WIDTH:SSM_WIDTH + SSM_GROUPS * SSM_STATE].astype(BF16)
    cm = act[:, SSM_WIDTH + SSM_GROUPS * SSM_STATE:].astype(BF16)

    dta = dt * a_neg
    p_hi, p_mid, p_lo = _split3(dta)
    a_col = _dot(tri, p_hi) + _dot(tri, p_mid) + _dot(tri, p_lo)
    a_last = a_col[cl - 1:cl, :]
    a_row = a_col.T
    per_head = jnp.concatenate([dt, jnp.exp(a_col), jnp.exp(a_last - a_col)], axis=0)
    ph_hi, ph_lo = _split2(per_head)
    per_lane = _dot(ph_hi, expand_ref[...]) + _dot(ph_lo, expand_ref[...])
    dt_x = per_lane[:cl]
    ecol_x = per_lane[cl:2 * cl]
    dte_x = per_lane[2 * cl:]
    e_last = jnp.exp(a_last)
    xdt = xs * dt_x

    li = lax.broadcasted_iota(jnp.int32, (cl, cl), 0)
    si = lax.broadcasted_iota(jnp.int32, (cl, cl), 1)
    causal = si <= li
    lo = lax.broadcasted_iota(jnp.int32, (cl, LANES), 1) < SSM_HEAD_DIM
    row_lo = lax.broadcasted_iota(jnp.int32, (LANES, SSM_STATE), 0) < SSM_HEAD_DIM

    ys = []
    for g in range(SSM_GROUPS):
        b_g = bm[:, g * SSM_STATE:(g + 1) * SSM_STATE]
        c_g = cm[:, g * SSM_STATE:(g + 1) * SSM_STATE]
        cb = _dot_nt(c_g, b_g)
        for r in range(HEAD_PAIRS // SSM_GROUPS):
            j = g * (HEAD_PAIRS // SSM_GROUPS) + r
            sl = slice(j * LANES, (j + 1) * LANES)
            xdt_p = xdt[:, sl]
            ms = []
            for hh in range(2):
                h = 2 * j + hh
                seg = a_col[:, h:h + 1] - a_row[h:h + 1, :]
                ms.append(cb * jnp.exp(jnp.where(causal, seg, -jnp.inf)))
            m2 = jnp.concatenate(ms, axis=1).astype(BF16)
            rhs = jnp.concatenate([jnp.where(lo, xdt_p, 0.0), jnp.where(lo, 0.0, xdt_p)],
                                  axis=0).astype(BF16)
            y_diag = _dot(m2, rhs)
            st = state_ref[j]
            y_off = _dot_nt(c_g, st.astype(BF16)) * ecol_x[:, sl]
            xdt_e = (xdt_p * dte_x[:, sl]).T.astype(BF16)
            d_a = e_last[:, 2 * j:2 * j + 1]
            d_b = e_last[:, 2 * j + 1:2 * j + 2]
            decay = jnp.where(row_lo, jnp.broadcast_to(d_a, row_lo.shape), jnp.broadcast_to(d_b, row_lo.shape))
            state_ref[j] = decay * st + _dot(xdt_e, b_g)
            ys.append(y_diag + y_off + dskip_ref[:, sl] * xs[:, sl])
    y = jnp.concatenate(ys, axis=1)
    gated = y * _silu(z)
    gw = SSM_WIDTH // SSM_GROUPS
    outs = []
    for g in range(SSM_GROUPS):
        gg = gated[:, g * gw:(g + 1) * gw]
        outs.append(gg * lax.rsqrt(jnp.mean(gg * gg, axis=-1, keepdims=True) + EPS))
    y_ref[...] = (jnp.concatenate(outs, axis=1) * nw_ref[...]).astype(BF16)


def _ssd_prompt(xbc, z, dt, w, batch, seq):
    n_sub = SSD_CHUNKS_PER_STEP if seq % (SSD_CHUNKS_PER_STEP * SSD_CHUNK) == 0 else 1
    rows = n_sub * SSD_CHUNK
    nc = seq // rows
    tok = lambda b, c: (b * nc + c, 0)
    const = lambda b, c: (0, 0)
    full = lambda a: pl.BlockSpec(a.shape, const)
    return pl.pallas_call(
        _ssd_kernel,
        out_shape=(jax.ShapeDtypeStruct((batch * seq, SSM_WIDTH), BF16),
                   jax.ShapeDtypeStruct((batch, HEAD_PAIRS, LANES, SSM_STATE), F32)),
        grid=(batch, nc),
        in_specs=[
            pl.BlockSpec((rows, CONV_DIM), tok), pl.BlockSpec((rows, SSM_WIDTH), tok),
            pl.BlockSpec((rows, LANES), tok),
            full(w["conv_w"]), full(w["conv_b"]), full(w["a_log"]),
            full(w["d_skip"]), full(w["ssm_norm"]), full(w["tri"]), full(w["expand"]),
        ],
        out_specs=(pl.BlockSpec((rows, SSM_WIDTH), tok),
                   pl.BlockSpec((1, HEAD_PAIRS, LANES, SSM_STATE), lambda b, c: (b, 0, 0, 0))),
        scratch_shapes=[pltpu.VMEM((CONV_HALO, CONV_DIM), F32),
                        pltpu.VMEM((HEAD_PAIRS, LANES, SSM_STATE), F32)],
        compiler_params=pltpu.CompilerParams(dimension_semantics=("parallel", "arbitrary"),
                                             vmem_limit_bytes=VMEM_LIMIT),
        name="ssd_prompt",
    )(xbc, z, dt, w["conv_w"], w["conv_b"], w["a_log"], w["d_skip"], w["ssm_norm"], w["tri"], w["expand"])


def _attn_sample_kernel(qx_ref, kc_ref, kn_ref, vc_ref, vn_ref, sink_ref, ko_ref, vo_ref, o_ref):
    bs = qx_ref.shape[0]
    w = kc_ref.shape[1]
    sink = sink_ref[...]
    lo = lax.broadcasted_iota(jnp.int32, (1, LANES), 1) < HEAD_DIM
    for i in range(bs):
        ko_ref[i, 0:w - 1, :] = kc_ref[i, 1:w, :]
        ko_ref[i, w - 1:w, :] = kn_ref[i]
        vo_ref[i, 0:w - 1, :] = vc_ref[i, 1:w, :]
        vo_ref[i, w - 1:w, :] = vn_ref[i]
        s = _dot_nt(qx_ref[i], ko_ref[i].astype(BF16))
        m = jnp.maximum(jnp.max(s, axis=-1, keepdims=True), sink)
        p = jnp.exp(s - m)
        den = jnp.sum(p, axis=-1, keepdims=True) + jnp.exp(sink - m)
        o = _dot(p.astype(BF16), vo_ref[i].astype(BF16)) / den
        o_sw = pltpu.roll(o, HEAD_DIM, axis=1)
        for j in range(N_HEADS // 2):
            a, b = (o, o_sw) if j < N_HEADS // 4 else (o_sw, o)
            o_ref[i:i + 1, j * LANES:(j + 1) * LANES] = jnp.where(lo, a[2 * j:2 * j + 1], b[2 * j + 1:2 * j + 2])


def _attn_sample(qx, kc, kn, vc, vn, sink_x, bs):
    n, w = kc.shape[0], kc.shape[1]
    blk3 = lambda i: (i, 0, 0)
    return pl.pallas_call(
        _attn_sample_kernel,
        out_shape=(jax.ShapeDtypeStruct((n, w, KV_WIDTH), F32),
                   jax.ShapeDtypeStruct((n, w, KV_WIDTH), F32),
                   jax.ShapeDtypeStruct((n, ATTN_WIDTH), F32)),
        grid=(n // bs,),
        in_specs=[
            pl.BlockSpec((bs, BF16_ROWS, LANES), blk3),
            pl.BlockSpec((bs, w, KV_WIDTH), blk3), pl.BlockSpec((bs, 1, KV_WIDTH), blk3),
            pl.BlockSpec((bs, w, KV_WIDTH), blk3), pl.BlockSpec((bs, 1, KV_WIDTH), blk3),
            pl.BlockSpec(sink_x.shape, lambda i: (0, 0)),
        ],
        out_specs=(pl.BlockSpec((bs, w, KV_WIDTH), blk3), pl.BlockSpec((bs, w, KV_WIDTH), blk3),
                   pl.BlockSpec((bs, ATTN_WIDTH), lambda i: (i, 0))),
        compiler_params=pltpu.CompilerParams(dimension_semantics=("parallel",),
                                             vmem_limit_bytes=VMEM_LIMIT),
        name="attn_sample",
    )(qx, kc, kn, vc, vn, sink_x)


def _ssd_sample_kernel(xbc_ref, z_ref, dt_ref, cprev_ref, h0_ref, convw_ref, convb_ref, alog_ref,
                       dskip_ref, nw_ref, y_ref, cnew_ref, h1_ref):
    bs = xbc_ref.shape[0]
    x_raw = xbc_ref[...]
    conv = convb_ref[...] + x_raw * convw_ref[CONV_K - 1:CONV_K, :]
    for j in range(CONV_K - 1):
        conv = conv + cprev_ref[j] * convw_ref[j:j + 1, :]
    for j in range(CONV_K - 2):
        cnew_ref[j] = cprev_ref[j + 1]
    cnew_ref[CONV_K - 2] = x_raw
    act = _silu(conv)
    xs = act[:, :SSM_WIDTH]
    bm = act[:, SSM_WIDTH:SSM_WIDTH + SSM_GROUPS * SSM_STATE].astype(BF16)
    cm = act[:, SSM_WIDTH + SSM_GROUPS * SSM_STATE:].astype(BF16)
    lane = lax.broadcasted_iota(jnp.int32, (1, LANES), 1)
    a_neg = jnp.where(lane < SSM_HEADS, -jnp.exp(alog_ref[...]), 0.0)
    dt = dt_ref[...]
    dec = jnp.exp(dt * a_neg)
    xdt = xs * _lane_bcast_pairs(dt, HEAD_PAIRS)
    rowid = lax.broadcasted_iota(jnp.int32, (bs, LANES), 0)
    row_lo = lax.broadcasted_iota(jnp.int32, (LANES, SSM_STATE), 0) < SSM_HEAD_DIM
    ys = []
    for j in range(HEAD_PAIRS):
        g = j // (HEAD_PAIRS // SSM_GROUPS)
        sl = slice(j * LANES, (j + 1) * LANES)
        b_g = bm[:, g * SSM_STATE:(g + 1) * SSM_STATE]
        c_g = cm[:, g * SSM_STATE:(g + 1) * SSM_STATE]
        xdt_p = xdt[:, sl]
        y_p = jnp.zeros((bs, LANES), F32)
        for i in range(bs):
            xi = jnp.where(rowid == i, xdt_p, 0.0).T.astype(BF16)
            d_a = dec[i:i + 1, 2 * j:2 * j + 1]
            d_b = dec[i:i + 1, 2 * j + 1:2 * j + 2]
            decay = jnp.where(row_lo, jnp.broadcast_to(d_a, row_lo.shape), jnp.broadcast_to(d_b, row_lo.shape))
            new = decay * h0_ref[i, j] + _dot(xi, b_g)
            h1_ref[i, j] = new
            y_p = y_p + jnp.where(rowid == i, _dot_nt(c_g, new.astype(BF16)), 0.0)
        ys.append(y_p + dskip_ref[:, sl] * xs[:, sl])
    y = jnp.concatenate(ys, axis=1)
    gated = y * _silu(z_ref[...])
    gw = SSM_WIDTH // SSM_GROUPS
    outs = []
    for g in range(SSM_GROUPS):
        gg = gated[:, g * gw:(g + 1) * gw]
        outs.append(gg * lax.rsqrt(jnp.mean(gg * gg, axis=-1, keepdims=True) + EPS))
    y_ref[...] = (jnp.concatenate(outs, axis=1) * nw_ref[...]).astype(BF16)


def _ssd_sample(xbc, z, dt, cprev_t, h0, w, bs):
    n = xbc.shape[0]
    tok = lambda i: (i, 0)
    const = lambda i: (0, 0)
    full = lambda a: pl.BlockSpec(a.shape, const)
    return pl.pallas_call(
        _ssd_sample_kernel,
        out_shape=(jax.ShapeDtypeStruct((n, SSM_WIDTH), BF16),
                   jax.ShapeDtypeStruct((CONV_K - 1, n, CONV_DIM), F32),
                   jax.ShapeDtypeStruct((n, HEAD_PAIRS, LANES, SSM_STATE), F32)),
        grid=(n // bs,),
        in_specs=[
            pl.BlockSpec((bs, CONV_DIM), tok), pl.BlockSpec((bs, SSM_WIDTH), tok),
            pl.BlockSpec((bs, LANES), tok),
            pl.BlockSpec((CONV_K - 1, bs, CONV_DIM), lambda i: (0, i, 0)),
            pl.BlockSpec((bs, HEAD_PAIRS, LANES, SSM_STATE), lambda i: (i, 0, 0, 0)),
            full(w["conv_w"]), full(w["conv_b"]), full(w["a_log"]),
            full(w["d_skip"]), full(w["ssm_norm"]),
        ],
        out_specs=(pl.BlockSpec((bs, SSM_WIDTH), tok),
                   pl.BlockSpec((CONV_K - 1, bs, CONV_DIM), lambda i: (0, i, 0)),
                   pl.BlockSpec((bs, HEAD_PAIRS, LANES, SSM_STATE), lambda i: (i, 0, 0, 0))),
        compiler_params=pltpu.CompilerParams(dimension_semantics=("parallel",),
                                             vmem_limit_bytes=VMEM_LIMIT),
        name="ssd_sample",
    )(xbc, z, dt, cprev_t, h0, w["conv_w"], w["conv_b"], w["a_log"], w["d_skip"], w["ssm_norm"])


def _pack_bf16_pair(v):
    c = v.shape[1] // 2
    hi = lax.bitcast_convert_type(v[:, :c].astype(BF16).astype(F32), jnp.uint32)
    lo = lax.bitcast_convert_type(v[:, c:].astype(BF16).astype(F32), jnp.uint32)
    return hi | (lo >> 16)


def _unpack_bf16_pair(word):
    a = lax.bitcast_convert_type(word & jnp.uint32(0xFFFF0000), F32)
    b = lax.bitcast_convert_type(word << 16, F32)
    return a, b


def _outproj_router_kernel(x_ref, a_ref, s_ref, wo_ref, n2_ref, wr_ref, br_ref, tri_ref,
                           h_ref, t_ref, route_ref, routet_ref, cnt_ref, carry_ref):
    @pl.when(pl.program_id(0) == 0)
    def _():
        carry_ref[...] = jnp.zeros(carry_ref.shape, F32)

    tm = x_ref.shape[0]
    h_ref[...] = (x_ref[...] + _dot(a_ref[...].astype(BF16), wo_ref[:ATTN_WIDTH, :])
                  + _dot(s_ref[...].astype(BF16), wo_ref[ATTN_WIDTH:, :]))
    sub = min(tm, ROUTER_SUBTILE)
    carry = carry_ref[:, 0:1]
    for s in range(tm // sub):
        rows = slice(s * sub, (s + 1) * sub)
        carry = _route_rows(h_ref[rows, :], n2_ref, wr_ref, br_ref, tri_ref, carry, t_ref.at[rows, :],
                            route_ref.at[rows, :], routet_ref.at[:, rows])
    carry_ref[...] = jnp.broadcast_to(carry, carry_ref.shape)
    cpad = jnp.concatenate([jnp.zeros((EXPERT_LANE0, LANES), F32), jnp.broadcast_to(carry, (N_EXPERTS, LANES)),
                            jnp.zeros((LANES - EXPERT_LANE0 - N_EXPERTS, LANES), F32)], axis=0)
    cnt_ref[...] = cpad.T[0:cnt_ref.shape[0], :]


def _route_rows(h, n2_ref, wrt_ref, brc_ref, triu_ref, carry, t_ref, route_ref, routet_ref):
    ms = jnp.mean(h * h, axis=-1, keepdims=True)
    t = h * lax.rsqrt(ms + EPS) * n2_ref[...]
    t_ref[...] = _pack_bf16_pair(t)
    n = h.shape[0]
    a = _dot_nt(wrt_ref[...], t.astype(BF16))

    def logit_rows(r0, r1):
        return a[r0:r1] + a[LANES + r0:LANES + r1] + brc_ref[r0:r1, :]

    sl = EXP_PER_GROUP
    glog = logit_rows(0, sl)
    elog = logit_rows(EXPERT_LANE0, EXPERT_LANE0 + N_EXPERTS)
    row = lax.broadcasted_iota(jnp.int32, (sl, n), 0).astype(F32)
    big = float(sl)
    ninf = -jnp.inf
    gm = jnp.where(row < N_EGROUPS, glog, ninf)
    gmax = jnp.max(gm, axis=0, keepdims=True)
    g_top = 1.0 / jnp.sum(jnp.exp(gm - gmax), axis=0, keepdims=True)
    g_idx = jnp.min(jnp.where(gm == gmax, row, big), axis=0, keepdims=True)
    ml = elog[(N_EGROUPS - 1) * sl:]
    for g in range(N_EGROUPS - 2, -1, -1):
        ml = jnp.where(g_idx == float(g), elog[g * sl:(g + 1) * sl], ml)
    m1 = jnp.max(ml, axis=0, keepdims=True)
    i1 = jnp.min(jnp.where(ml == m1, row, big), axis=0, keepdims=True)
    ml2 = jnp.where(row == i1, ninf, ml)
    m2 = jnp.max(ml2, axis=0, keepdims=True)
    i2 = jnp.min(jnp.where(ml2 == m2, row, big), axis=0, keepdims=True)
    r = jnp.exp(m2 - m1)
    w1 = g_top / (1.0 + r)
    w2 = g_top * r / (1.0 + r)
    e1 = g_idx * float(sl) + i1
    e2 = g_idx * float(sl) + i2
    pick = jnp.where(jnp.logical_or(row == i1, row == i2), 1.0, 0.0)
    onehot = jnp.concatenate([jnp.where(g_idx == float(g), pick, 0.0) for g in range(N_EGROUPS)], axis=0)
    onehot_bf = onehot.astype(BF16)
    cb = triu_ref.shape[0]
    cums = []
    for blk in range(n // cb):
        c = _dot(onehot_bf[:, blk * cb:(blk + 1) * cb], triu_ref[...]) + carry
        carry = c[:, cb - 1:cb]
        cums.append(c)
    before = jnp.concatenate(cums, axis=1) - onehot
    erow = lax.broadcasted_iota(jnp.int32, (N_EXPERTS, n), 0).astype(F32)
    rank1 = jnp.sum(jnp.where(erow == e1, before, 0.0), axis=0, keepdims=True)
    rank2 = jnp.sum(jnp.where(erow == e2, before, 0.0), axis=0, keepdims=True)
    fields = jnp.concatenate([e1, e2, w1, w2, rank1, rank2], axis=0)
    routet_ref[0:fields.shape[0], :] = fields
    routet_ref[fields.shape[0]:, :] = jnp.zeros((routet_ref.shape[0] - fields.shape[0], n), F32)
    if n % LANES:
        fields = jnp.concatenate([fields, jnp.zeros((fields.shape[0], LANES - n % LANES), F32)], axis=1)
    pad = jnp.zeros((LANES - fields.shape[0], LANES), F32)
    for j in range(fields.shape[1] // LANES):
        blk_rows = jnp.concatenate([fields[:, j * LANES:(j + 1) * LANES], pad], axis=0)
        n_valid = min(LANES, n - j * LANES)
        route_ref[j * LANES:j * LANES + n_valid, :] = blk_rows.T[:n_valid, :]
    return carry


def _outproj_router(x2d, attn, ssm, w, tm):
    t = x2d.shape[0]
    tok = lambda i: (i, 0)
    const = lambda i: (0, 0)
    full = lambda a: pl.BlockSpec(a.shape, const)
    idx = jnp.arange(min(tm, COUNT_BLOCK))
    tri = (idx[:, None] <= idx[None, :]).astype(BF16)
    return pl.pallas_call(
        _outproj_router_kernel,
        out_shape=(jax.ShapeDtypeStruct((t, D_MODEL), F32), jax.ShapeDtypeStruct((t, D_MODEL // 2), jnp.uint32),
                   jax.ShapeDtypeStruct((t, LANES), F32), jax.ShapeDtypeStruct((8, t), F32),
                   jax.ShapeDtypeStruct((8, LANES), F32)),
        grid=(t // tm,),
        in_specs=[
            pl.BlockSpec((tm, D_MODEL), tok), pl.BlockSpec((tm, ATTN_WIDTH), tok),
            pl.BlockSpec((tm, SSM_WIDTH), tok),
            full(w["w_out"]), full(w["norm2"]), full(w["wrt"]),
            full(w["br_col"]), full(tri),
        ],
        out_specs=(pl.BlockSpec((tm, D_MODEL), tok), pl.BlockSpec((tm, D_MODEL // 2), tok),
                   pl.BlockSpec((tm, LANES), tok), pl.BlockSpec((8, tm), lambda i: (0, i)),
                   pl.BlockSpec((8, LANES), const)),
        scratch_shapes=[pltpu.VMEM((N_EXPERTS, LANES), F32)],
        compiler_params=pltpu.CompilerParams(dimension_semantics=("arbitrary",),
                                             vmem_limit_bytes=VMEM_LIMIT),
        name="outproj_router",
    )(x2d, attn, ssm, w["w_out"], w["norm2"], w["wrt"], w["br_col"], tri)


def _expert_hidden(t_a, t_b, w1):
    half = D_MODEL // 2
    gu = _dot(t_a, w1[:half]) + _dot(t_b, w1[half:])
    return _silu(gu[:, :EXPERT_FF]) * gu[:, EXPERT_FF:]


def _gate_up_bf16(wg_ref, wu_ref):
    return jnp.concatenate([wg_ref[0].astype(BF16), wu_ref[0].astype(BF16)], axis=1)


def _moe_dense_kernel(h_ref, t_ref, route_ref, wg_ref, wu_ref, wd_ref, y_ref):
    e = pl.program_id(0)

    @pl.when(e == 0)
    def _():
        y_ref[...] = h_ref[...]

    t_a, t_b = _unpack_bf16_pair(t_ref[...])
    route = route_ref[...]
    e1, e2, g1, g2 = route[:, 0:1], route[:, 1:2], route[:, 2:3], route[:, 3:4]
    e_f = e.astype(F32)
    hid = _expert_hidden(t_a.astype(BF16), t_b.astype(BF16), _gate_up_bf16(wg_ref, wu_ref))
    c_e = jnp.where(e1 == e_f, g1, 0.0) + jnp.where(e2 == e_f, g2, 0.0)
    y_ref[...] += _dot((hid * c_e).astype(BF16), wd_ref[0].astype(BF16))


def _moe_dense(h, t, route, w):
    n = h.shape[0]
    whole = lambda e: (0, 0)
    by_expert = lambda e: (e, 0, 0)
    return pl.pallas_call(
        _moe_dense_kernel,
        out_shape=jax.ShapeDtypeStruct((n, D_MODEL), F32),
        grid=(N_EXPERTS,),
        in_specs=[pl.BlockSpec((n, D_MODEL), whole), pl.BlockSpec((n, D_MODEL // 2), whole),
                  pl.BlockSpec((n, LANES), whole),
                  pl.BlockSpec((1, D_MODEL, EXPERT_FF), by_expert), pl.BlockSpec((1, D_MODEL, EXPERT_FF), by_expert),
                  pl.BlockSpec((1, EXPERT_FF, D_MODEL), by_expert)],
        out_specs=pl.BlockSpec((n, D_MODEL), whole),
        compiler_params=pltpu.CompilerParams(dimension_semantics=("arbitrary",),
                                             vmem_limit_bytes=VMEM_LIMIT),
        name="moe_dense",
    )(h, t, route, w["w_gate"], w["w_up"], w["w_down"])


def _sc_scatter_rows(src, pos1, pos2, n_out):
    t, width = src.shape
    rows_per_worker = t // SC_WORKERS
    n_chunks = rows_per_worker // SC_CHUNK
    assert t == SC_WORKERS * SC_CHUNK * n_chunks and n_chunks % 2 == 0
    mesh = plsc.VectorSubcoreMesh(core_axis_name="c", subcore_axis_name="s")

    @functools.partial(
        pl.kernel, mesh=mesh,
        out_type=jax.ShapeDtypeStruct((n_out, width), src.dtype),
        scratch_types=[pltpu.VMEM((2, SC_CHUNK), jnp.int32), pltpu.VMEM((2, SC_CHUNK), jnp.int32),
                       pltpu.VMEM((2, SC_CHUNK, width), src.dtype),
                       pltpu.SemaphoreType.DMA, pltpu.SemaphoreType.DMA,
                       pltpu.SemaphoreType.DMA, pltpu.SemaphoreType.DMA],
    )
    def scatter_kernel(src_hbm, p1_hbm, p2_hbm, out_hbm, i1_v, i2_v, rows_v, l0, l1, s0, s1):
        wid = lax.axis_index("s") * SC_CORES + lax.axis_index("c")
        base = wid * rows_per_worker
        lsem = (l0, l1)
        ssem = (s0, s1)

        def load_copy(c, slot):
            off = pl.multiple_of(base + c * SC_CHUNK, 8)
            return pltpu.make_async_copy(src_hbm.at[pl.ds(off, SC_CHUNK)], rows_v.at[slot], lsem[slot])

        def start_load(c, slot):
            off = pl.multiple_of(base + c * SC_CHUNK, 8)
            pltpu.sync_copy(p1_hbm.at[pl.ds(off, SC_CHUNK)], i1_v.at[slot])
            pltpu.sync_copy(p2_hbm.at[pl.ds(off, SC_CHUNK)], i2_v.at[slot])
            load_copy(c, slot).start()

        def scatter_copies(slot):
            return (pltpu.make_async_copy(rows_v.at[slot], out_hbm.at[i1_v.at[slot]], ssem[slot]),
                    pltpu.make_async_copy(rows_v.at[slot], out_hbm.at[i2_v.at[slot]], ssem[slot]))

        def start_scatter(slot):
            for cp in scatter_copies(slot):
                cp.start()

        def wait_scatter(slot):
            for cp in scatter_copies(slot):
                cp.wait()

        start_load(0, 0)

        @pl.loop(0, n_chunks, step=2)
        def _(c):
            @pl.when(c > 0)
            def _():
                wait_scatter(1)

            start_load(c + 1, 1)
            load_copy(c, 0).wait()
            start_scatter(0)
            load_copy(c + 1, 1).wait()
            wait_scatter(0)
            start_scatter(1)

            @pl.when(c + 2 < n_chunks)
            def _():
                start_load(c + 2, 0)

        wait_scatter(1)

    return scatter_kernel(src, pos1, pos2)


def _sc_gather_rows(table, idx):
    n, width = idx.shape[0], table.shape[1]
    rows_per_worker = n // SC_WORKERS
    n_chunks = rows_per_worker // SC_CHUNK
    assert n == SC_WORKERS * SC_CHUNK * n_chunks and n_chunks % 2 == 0
    mesh = plsc.VectorSubcoreMesh(core_axis_name="c", subcore_axis_name="s")

    @functools.partial(
        pl.kernel, mesh=mesh,
        out_type=jax.ShapeDtypeStruct((n, width), table.dtype),
        scratch_types=[pltpu.VMEM((2, SC_CHUNK), jnp.int32), pltpu.VMEM((2, SC_CHUNK, width), table.dtype),
                       pltpu.SemaphoreType.DMA, pltpu.SemaphoreType.DMA,
                       pltpu.SemaphoreType.DMA, pltpu.SemaphoreType.DMA],
    )
    def gather_kernel(table_hbm, idx_hbm, out_hbm, idx_v, rows_v, g0, g1, w0, w1):
        wid = lax.axis_index("s") * SC_CORES + lax.axis_index("c")
        base = wid * rows_per_worker
        gsem = (g0, g1)
        wsem = (w0, w1)

        def gather_copy(slot):
            return pltpu.make_async_copy(table_hbm.at[idx_v.at[slot]], rows_v.at[slot], gsem[slot])

        def write_copy(c, slot):
            off = pl.multiple_of(base + c * SC_CHUNK, 8)
            return pltpu.make_async_copy(rows_v.at[slot], out_hbm.at[pl.ds(off, SC_CHUNK)], wsem[slot])

        def start_gather(c, slot):
            off = pl.multiple_of(base + c * SC_CHUNK, 8)
            pltpu.sync_copy(idx_hbm.at[pl.ds(off, SC_CHUNK)], idx_v.at[slot])
            gather_copy(slot).start()

        start_gather(0, 0)

        @pl.loop(0, n_chunks, step=2)
        def _(c):
            @pl.when(c > 0)
            def _():
                write_copy(c - 1, 1).wait()

            start_gather(c + 1, 1)
            gather_copy(0).wait()
            write_copy(c, 0).start()
            gather_copy(1).wait()
            write_copy(c + 1, 1).start()
            write_copy(c, 0).wait()

            @pl.when(c + 2 < n_chunks)
            def _():
                start_gather(c + 2, 0)

        write_copy(n_chunks - 1, 1).wait()

    return gather_kernel(table, idx)


def _moe_grouped_kernel(te_ref, nt_ref, tv_ref, order_ref, x_ref, wg_ref, wu_ref, wd_ref, o_ref,
                        w1_bf_ref, w2_bf_ref):
    del nt_ref, order_ref
    i = pl.program_id(0)
    valid = tv_ref[i]

    @pl.when(jnp.logical_or(i == 0, te_ref[i] != te_ref[jnp.maximum(i - 1, 0)]))
    def _():
        w1_bf_ref[...] = _gate_up_bf16(wg_ref, wu_ref)
        w2_bf_ref[...] = wd_ref[0].astype(BF16)

    def sub_tile(s):
        rows = slice(s * MOE_SUBTILE, (s + 1) * MOE_SUBTILE)
        t_a, t_b = _unpack_bf16_pair(x_ref[rows, :])
        hid = _expert_hidden(t_a.astype(BF16), t_b.astype(BF16), w1_bf_ref[...])
        o_ref[rows, :] = _pack_bf16_pair(_dot(hid.astype(BF16), w2_bf_ref[...]))

    @pl.when(valid == MOE_TILE)
    def _():
        for s in range(MOE_TILE // MOE_SUBTILE):
            sub_tile(s)

    @pl.when(jnp.logical_and(valid > 0, valid < MOE_TILE))
    def _():
        for s in range(MOE_TILE // MOE_SUBTILE):
            pl.when(valid > s * MOE_SUBTILE)(functools.partial(sub_tile, s))


def _moe_grouped(xs, tile_expert, n_tiles, tile_valid, order, w):
    rows = xs.shape[0]
    row = lambda i, te, nt, tv, od: (jnp.minimum(i, nt[0] - 1), 0)
    by_expert = lambda i, te, nt, tv, od: (te[i], 0, 0)
    return pl.pallas_call(
        _moe_grouped_kernel,
        out_shape=jax.ShapeDtypeStruct((rows, D_MODEL // 2), jnp.uint32),
        grid_spec=pltpu.PrefetchScalarGridSpec(
            num_scalar_prefetch=4,
            grid=(rows // MOE_TILE,),
            in_specs=[pl.BlockSpec((MOE_TILE, D_MODEL // 2), row),
                      pl.BlockSpec((1, D_MODEL, EXPERT_FF), by_expert),
                      pl.BlockSpec((1, D_MODEL, EXPERT_FF), by_expert),
                      pl.BlockSpec((1, EXPERT_FF, D_MODEL), by_expert)],
            out_specs=pl.BlockSpec((MOE_TILE, D_MODEL // 2), row),
            scratch_shapes=[pltpu.VMEM((D_MODEL, 2 * EXPERT_FF), BF16), pltpu.VMEM((EXPERT_FF, D_MODEL), BF16)],
        ),
        compiler_params=pltpu.CompilerParams(dimension_semantics=("arbitrary",),
                                             vmem_limit_bytes=VMEM_LIMIT),
        name="moe_grouped",
    )(tile_expert, n_tiles, tile_valid, order, xs, w["w_gate"], w["w_up"], w["w_down"])


def _moe_combine_kernel(h_ref, z1_ref, z2_ref, route_ref, y_ref):
    route = route_ref[...]
    g1, g2 = route[:, 2:3], route[:, 3:4]
    half = D_MODEL // 2
    a1, b1 = _unpack_bf16_pair(z1_ref[...])
    a2, b2 = _unpack_bf16_pair(z2_ref[...])
    y_ref[:, :half] = h_ref[:, :half] + g1 * a1 + g2 * a2
    y_ref[:, half:] = h_ref[:, half:] + g1 * b1 + g2 * b2


def _moe_combine(h, z, route, tm):
    t = h.shape[0]
    nb = t // tm
    tok = lambda i: (i, 0)
    return pl.pallas_call(
        _moe_combine_kernel,
        out_shape=jax.ShapeDtypeStruct((t, D_MODEL), F32),
        grid=(nb,),
        in_specs=[pl.BlockSpec((tm, D_MODEL), tok), pl.BlockSpec((tm, D_MODEL // 2), tok),
                  pl.BlockSpec((tm, D_MODEL // 2), lambda i: (i + nb, 0)), pl.BlockSpec((tm, LANES), tok)],
        out_specs=pl.BlockSpec((tm, D_MODEL), tok),
        compiler_params=pltpu.CompilerParams(dimension_semantics=("parallel",),
                                             vmem_limit_bytes=VMEM_LIMIT),
        name="moe_combine",
    )(h, z, z, route)


def _route_pos_kernel(routet_ref, cnt_ref, upper_ref, pos_ref):
    tm = routet_ref.shape[1]
    cnt = cnt_ref[...]
    padded = jnp.floor((cnt + float(MOE_TILE - 1)) * (1.0 / MOE_TILE)) * float(MOE_TILE)
    p_hi, p_mid, p_lo = _split3(padded)
    upper = upper_ref[...]
    starts = (_dot(p_hi, upper) + _dot(p_mid, upper) + _dot(p_lo, upper))[0:1, :]
    starts_col = jnp.broadcast_to(starts, (LANES, LANES)).T[EXPERT_LANE0:EXPERT_LANE0 + N_EXPERTS, 0:1]
    erow = lax.broadcasted_iota(jnp.int32, (N_EXPERTS, tm), 0).astype(F32)
    for k in range(2):
        e_k = routet_ref[k:k + 1, :]
        pos = (jnp.sum(jnp.where(erow == e_k, starts_col, 0.0), axis=0, keepdims=True)
               + routet_ref[4 + k:5 + k, :]).astype(jnp.int32)
        for r in range(tm // LANES):
            pos_ref[k, r:r + 1, :] = pos[:, r * LANES:(r + 1) * LANES]


def _route_positions(routet, counts, tm):
    t = routet.shape[1]
    idx = jnp.arange(LANES)
    upper = (idx[:, None] < idx[None, :]).astype(BF16)
    return pl.pallas_call(
        _route_pos_kernel,
        out_shape=jax.ShapeDtypeStruct((2, t // LANES, LANES), jnp.int32),
        grid=(t // tm,),
        in_specs=[pl.BlockSpec((8, tm), lambda i: (0, i)), pl.BlockSpec((8, LANES), lambda i: (0, 0)),
                  pl.BlockSpec((LANES, LANES), lambda i: (0, 0))],
        out_specs=pl.BlockSpec((2, tm // LANES, LANES), lambda i: (0, i, 0)),
        compiler_params=pltpu.CompilerParams(dimension_semantics=("parallel",),
                                             vmem_limit_bytes=VMEM_LIMIT),
        name="route_positions",
    )(routet, counts, upper)


def _moe_routed(h, t_packed, route, routet, counts, w, tm, run_first):
    t = h.shape[0]
    pos = _route_positions(routet, counts, min(32 * LANES, t))
    pos1 = pos[0].reshape(t)
    pos2 = pos[1].reshape(t)
    cnt = counts[0, EXPERT_LANE0:EXPERT_LANE0 + N_EXPERTS].astype(jnp.int32)
    padded = (cnt + MOE_TILE - 1) // MOE_TILE * MOE_TILE
    ends = jnp.cumsum(padded)
    n_rows = 2 * t + N_EXPERTS * MOE_TILE
    n_tiles = ends[N_EXPERTS - 1] // MOE_TILE
    tile_first = jnp.arange(n_rows // MOE_TILE, dtype=jnp.int32) * MOE_TILE
    tile_start = jnp.minimum(tile_first, ends[N_EXPERTS - 1] - MOE_TILE)
    tile_expert = jnp.sum((tile_start[:, None] >= ends[None, :]).astype(jnp.int32), axis=1)
    used_end = ends - padded + cnt
    own = tile_expert[:, None] == jnp.arange(N_EXPERTS, dtype=jnp.int32)[None, :]
    tile_valid = jnp.clip(jnp.sum(jnp.where(own, used_end[None, :], 0), axis=1) - tile_first, 0, MOE_TILE)
    xs = _sc_scatter_rows(t_packed, pos1, pos2, n_rows)
    order = lax.bitcast_convert_type(run_first.reshape(-1)[:1].astype(F32), jnp.int32)
    out = _moe_grouped(xs, tile_expert, n_tiles.reshape(1), tile_valid, order, w)
    z = _sc_gather_rows(out, pos.reshape(2 * t))
    return _moe_combine(h, z, route, tm)


def _pad_lanes(a, width=LANES):
    return jnp.pad(a, ((0, 0), (0, width - a.shape[1])))


def _prep_weights(norm1, w_in, q_norm, k_norm, conv_w, conv_b, dt_bias, a_log, d_skip, ssm_norm, w_out,
                  norm2, w_grp, b_grp, w_exp, b_exp, w_gate, w_up, w_down):
    w = {}
    w["norm1"] = norm1.reshape(1, D_MODEL)
    w["w_in"] = _pad_lanes(w_in, XBC_END + LANES).astype(BF16)
    w["qkn"] = jnp.concatenate([jnp.tile(q_norm, N_HEADS), jnp.tile(k_norm, KV_HEADS)]).reshape(1, QK_WIDTH)
    head_of_col = jnp.arange(QK_WIDTH) // HEAD_DIM
    red = (head_of_col[:, None] == jnp.arange(LANES)[None, :])
    w["red"] = red.astype(BF16)
    w["exp"] = red.T.astype(BF16)
    w["conv_w"] = conv_w
    w["conv_b"] = conv_b.reshape(1, CONV_DIM)
    w["dt_bias"] = _pad_lanes(dt_bias.reshape(1, SSM_HEADS))
    w["a_log"] = _pad_lanes(a_log.reshape(1, SSM_HEADS))
    w["d_skip"] = jnp.repeat(d_skip, SSM_HEAD_DIM).reshape(1, SSM_WIDTH)
    w["ssm_norm"] = ssm_norm.reshape(1, SSM_WIDTH)
    idx = jnp.arange(SSD_CHUNK)
    w["tri"] = (idx[None, :] <= idx[:, None]).astype(BF16)
    lane_head = jnp.arange(SSM_WIDTH) // SSM_HEAD_DIM
    w["expand"] = (jnp.arange(LANES)[:, None] == lane_head[None, :]).astype(BF16)
    w["w_out"] = w_out.astype(BF16)
    w["norm2"] = norm2.reshape(1, D_MODEL)
    wr = jnp.zeros((D_MODEL, LANES), F32)
    wr = wr.at[:, :N_EGROUPS].set(w_grp).at[:, EXPERT_LANE0:EXPERT_LANE0 + N_EXPERTS].set(w_exp)
    wr_hi = wr.astype(BF16)
    w["wrt"] = jnp.concatenate([wr_hi, (wr - wr_hi.astype(F32)).astype(BF16)], axis=1).T
    br = jnp.zeros((LANES, 1), F32)
    w["br_col"] = br.at[:N_EGROUPS, 0].set(b_grp).at[EXPERT_LANE0:EXPERT_LANE0 + N_EXPERTS, 0].set(b_exp)
    w["w_gate"], w["w_up"], w["w_down"] = w_gate, w_up, w_down
    return w


def _rope_tables(pos):
    inv = 1.0 / (ROPE_THETA ** (jnp.arange(0, HEAD_DIM, 2, dtype=F32) / HEAD_DIM))
    ang = pos.astype(F32)[:, None] * inv[None, :]
    cos, sin = jnp.cos(ang), jnp.sin(ang)
    reps = LANES // HEAD_DIM
    return (jnp.tile(jnp.concatenate([cos, cos], axis=-1), (1, reps)),
            jnp.tile(jnp.concatenate([-sin, sin], axis=-1), (1, reps)))


def _token_tile(t):
    for tm in (1024, 512, 256, 128, 64, 32, 16):
        if t % tm == 0:
            return tm
    raise ValueError(f"token count {t} is not a multiple of 16")


def kernel(x_prompt, x_sample, cache_win_k, cache_win_v, state_conv, state_ssm, norm1, w_in, q_norm, k_norm,
           sinks, conv_w, conv_b, dt_bias, a_log, d_skip, ssm_norm, w_out, norm2, w_grp, b_grp, w_exp, b_exp,
           w_gate, w_up, w_down):
    depth = norm1.shape[0]
    assert depth == 1, "single-layer stack"
    bp, lp, _ = x_prompt.shape
    bsn, ls, _ = x_sample.shape
    assert ls == 1 and lp % WINDOW == 0 and cache_win_k.shape[2] == WINDOW
    l = 0
    w = _prep_weights(norm1[l], w_in[l], q_norm[l], k_norm[l], conv_w[l], conv_b[l], dt_bias[l], a_log[l],
                      d_skip[l], ssm_norm[l], w_out[l], norm2[l], w_grp[l], b_grp[l], w_exp[l], b_exp[l],
                      w_gate[l], w_up[l], w_down[l])
    sink = sinks[l]

    tp = bp * lp
    xp = x_prompt.reshape(tp, D_MODEL)
    tm_p = _token_tile(lp)
    cos_p, sin_p = _rope_tables(jnp.arange(lp, dtype=jnp.int32))
    q, k, v, z, xbc, dt, k2, v2 = _inproj(xp, w, cos_p, sin_p, tm_p, lp // tm_p)
    attn = _attn_prompt(q, k2, v2, sink, bp, lp)
    ssm, st_p = _ssd_prompt(xbc, z, dt, w, bp, lp)
    h, t, route, routet, counts = _outproj_router(xp, attn, ssm, w, tm_p)
    k3 = k.reshape(bp, lp, KV_HEADS, HEAD_DIM)
    v3 = v.reshape(bp, lp, KV_HEADS, HEAD_DIM)
    win_k_p = k3[:, lp - WINDOW:][None]
    win_v_p = v3[:, lp - WINDOW:][None]
    conv_p = xbc.reshape(bp, lp, CONV_DIM)[:, lp - (CONV_K - 1):][None]
    ssm_p = st_p.reshape(1, bp, SSM_HEADS, SSM_HEAD_DIM, SSM_STATE)

    xs2 = x_sample.reshape(bsn, D_MODEL)
    tm_s = _token_tile(bsn)
    cos_s, sin_s = _rope_tables(jnp.full((tm_s,), PAST_LEN, jnp.int32))
    q_s, k_s, v_s, z_s, xbc_s, dt_s, _, _ = _inproj(xs2, w, cos_s, sin_s, tm_s, 1)
    q4 = q_s.reshape(bsn, KV_HEADS, N_HEADS // KV_HEADS, HEAD_DIM)
    zq = jnp.zeros_like(q4[:, 0])
    qx = jnp.concatenate([jnp.concatenate([q4[:, 0], zq], axis=-1),
                          jnp.concatenate([zq, q4[:, 1]], axis=-1)], axis=1)
    qx = jnp.pad(qx, ((0, 0), (0, BF16_ROWS - N_HEADS), (0, 0)))
    sink_x = jnp.pad(jnp.broadcast_to(sink[:, None], (N_HEADS, LANES)), ((0, BF16_ROWS - N_HEADS), (0, 0)))
    kc = cache_win_k[l].reshape(bsn, WINDOW, KV_WIDTH)
    vc = cache_win_v[l].reshape(bsn, WINDOW, KV_WIDTH)
    ko, vo, attn_s = _attn_sample(qx, kc, k_s.reshape(bsn, 1, KV_WIDTH), vc, v_s.reshape(bsn, 1, KV_WIDTH),
                               sink_x, 8)
    y_prompt = _moe_routed(h, t, route, routet, counts, w, tm_p, attn_s).reshape(bp, lp, D_MODEL)
    cprev_t = jnp.transpose(state_conv[l], (1, 0, 2))
    h0 = state_ssm[l].reshape(bsn, HEAD_PAIRS, LANES, SSM_STATE)
    ssm_s, cnew_t, h1 = _ssd_sample(xbc_s, z_s, dt_s, cprev_t, h0, w, 16)
    h_s, t_s, route_s, _, _ = _outproj_router(xs2, attn_s, ssm_s, w, tm_s)
    y_sample = _moe_dense(h_s, t_s, route_s, w).reshape(bsn, 1, D_MODEL)
    win_k_s = ko.reshape(1, bsn, WINDOW, KV_HEADS, HEAD_DIM)
    win_v_s = vo.reshape(1, bsn, WINDOW, KV_HEADS, HEAD_DIM)
    conv_s = jnp.transpose(cnew_t, (1, 0, 2))[None]
    ssm_s_state = h1.reshape(1, bsn, SSM_HEADS, SSM_HEAD_DIM, SSM_STATE)

    return (y_prompt, y_sample, win_k_p, win_v_p, conv_p, ssm_p, win_k_s, win_v_s, conv_s, ssm_s_state)
```

```python
import functools
import math

import jax
import jax.numpy as jnp
from jax import lax
from jax.experimental import pallas as pl
from jax.experimental.pallas import tpu as pltpu
from jax.experimental.pallas import tpu_sc as plsc

F32 = jnp.float32
BF16 = jnp.bfloat16

D_MODEL = 1024
HEAD_DIM = 64
N_HEADS = 8
KV_HEADS = 2
WINDOW = 128
ATTN_WIDTH = N_HEADS * HEAD_DIM
QK_WIDTH = ATTN_WIDTH + KV_HEADS * HEAD_DIM
KV_WIDTH = KV_HEADS * HEAD_DIM
ATTN_SCALE = HEAD_DIM ** -0.5
ROPE_THETA = 10000.0
SSM_WIDTH = 512
SSM_HEADS = 8
SSM_HEAD_DIM = 64
SSM_GROUPS = 2
SSM_STATE = 128
CONV_K = 4
CONV_HALO = 8
CONV_DIM = SSM_WIDTH + 2 * SSM_GROUPS * SSM_STATE
SSD_CHUNK = 128
N_EGROUPS = 4
EXP_PER_GROUP = 8
N_EXPERTS = 32
EXPERT_FF = 128
EPS = 1e-6
PAST_LEN = 16384

LANES = 128
BF16_ROWS = 16
HEAD_PAIRS = SSM_HEADS // 2
EXPERT_LANE0 = 32
VMEM_LIMIT = 56 * 1024 * 1024
MOE_TILE = 1024
MOE_SUBTILE = 128
ATTN_QBLOCKS = 8
SSD_CHUNKS_PER_STEP = 8
COUNT_BLOCK = 256
ROUTER_SUBTILE = 1024
INPROJ_SUBTILE = 512
SC_CORES = 2
SC_SUBCORES = 16
SC_WORKERS = SC_CORES * SC_SUBCORES
SC_CHUNK = 64

Q_END = ATTN_WIDTH
K_END = Q_END + KV_WIDTH
V_END = K_END + KV_WIDTH
Z_END = V_END + SSM_WIDTH
XBC_END = Z_END + CONV_DIM


def _dot(a, b):
    return jnp.dot(a, b, preferred_element_type=F32)


def _dot_nt(a, b):
    return lax.dot_general(a, b, (((1,), (1,)), ((), ())), preferred_element_type=F32)


def _split2(v):
    hi = v.astype(BF16)
    lo = (v - hi.astype(F32)).astype(BF16)
    return hi, lo


def _split3(v):
    hi = v.astype(BF16)
    r = v - hi.astype(F32)
    mid = r.astype(BF16)
    lo = (r - mid.astype(F32)).astype(BF16)
    return hi, mid, lo


def _silu(x):
    return x * jax.nn.sigmoid(x)


def _softplus(x):
    return jnp.maximum(x, 0.0) + jnp.log1p(jnp.exp(-jnp.abs(x)))


def _lane_bcast_pairs(v, n_pairs):
    r = v.shape[0]
    lo = lax.broadcasted_iota(jnp.int32, (r, LANES), 1) < HEAD_DIM
    slabs = []
    for j in range(n_pairs):
        a = jnp.broadcast_to(v[:, 2 * j:2 * j + 1], (r, LANES))
        b = jnp.broadcast_to(v[:, 2 * j + 1:2 * j + 2], (r, LANES))
        slabs.append(jnp.where(lo, a, b))
    return jnp.concatenate(slabs, axis=1)


def _causal_conv_silu(x_ext, convw_ref, convb_ref):
    halo = CONV_HALO
    x_raw = x_ext[halo:, :]
    conv = convb_ref[...] + x_raw * convw_ref[CONV_K - 1:CONV_K, :]
    for j in range(CONV_K - 1):
        shifted = pltpu.roll(x_ext, CONV_K - 1 - j, axis=0)[halo:, :]
        conv = conv + shifted * convw_ref[j:j + 1, :]
    return _silu(conv)


def _inproj_kernel(x_ref, n1_ref, win_ref, dtb_ref, qkn_ref,
                   cos_ref, sin_ref, red_ref, exp_ref,
                   q_ref, k_ref, v_ref, z_ref, xbc_ref, dt_ref, k2_ref, v2_ref):
    tm = x_ref.shape[0]
    sub = min(tm, INPROJ_SUBTILE)
    lane = lax.broadcasted_iota(jnp.int32, (sub, LANES), 1)
    first_half = (lane % HEAD_DIM) < (HEAD_DIM // 2)
    for s in range(tm // sub):
        rows = slice(s * sub, (s + 1) * sub)
        x = x_ref[rows, :]
        ms = jnp.mean(x * x, axis=-1, keepdims=True)
        xn = (x * lax.rsqrt(ms + EPS) * n1_ref[...]).astype(BF16)
        v = _dot(xn, win_ref[:, K_END:V_END])
        v_ref[rows, :] = v
        v2_ref[rows, :] = _pair_operands(v)
        z_ref[rows, :] = _dot(xn, win_ref[:, V_END:Z_END])
        xbc_ref[rows, :] = _dot(xn, win_ref[:, Z_END:XBC_END])
        dt_ref[rows, :] = _softplus(_dot(xn, win_ref[:, XBC_END:]) + dtb_ref[...])
        qk = _dot(xn, win_ref[:, :K_END])
        ss = _dot((qk * qk).astype(BF16), red_ref[...])
        inv = lax.rsqrt(ss * (1.0 / HEAD_DIM) + EPS)
        inv_hi, inv_lo = _split2(inv)
        inv_x = _dot(inv_hi, exp_ref[...]) + _dot(inv_lo, exp_ref[...])
        qkn = qk * inv_x * qkn_ref[...]
        cos = cos_ref[rows, :]
        sin = sin_ref[rows, :]
        for c in range(QK_WIDTH // LANES):
            xc = qkn[:, c * LANES:(c + 1) * LANES]
            partner = jnp.where(first_half,
                                pltpu.roll(xc, LANES - HEAD_DIM // 2, axis=1),
                                pltpu.roll(xc, HEAD_DIM // 2, axis=1))
            rot = xc * cos + partner * sin
            if c < ATTN_WIDTH // LANES:
                q_ref[rows, c * LANES:(c + 1) * LANES] = (rot * ATTN_SCALE).astype(BF16)
            else:
                k_ref[rows, :] = rot
                k2_ref[rows, :] = _pair_operands(rot)


def _inproj(x2d, w, cos_tab, sin_tab, tm, n_pos_blocks):
    t = x2d.shape[0]
    grid = (t // tm,)
    tok = lambda i: (i, 0)
    const = lambda i: (0, 0)
    pos = lambda i: (i % n_pos_blocks, 0)
    full = lambda a: pl.BlockSpec(a.shape, const)
    rows = lambda width, dtype: (jax.ShapeDtypeStruct((t, width), dtype), pl.BlockSpec((tm, width), tok))
    outs = [rows(ATTN_WIDTH, BF16), rows(KV_WIDTH, F32), rows(KV_WIDTH, F32), rows(SSM_WIDTH, F32)]
    operands = [x2d, w["norm1"], w["w_in"], w["dt_bias"], w["qkn"],
                cos_tab, sin_tab, w["red"], w["exp"]]
    in_specs = [
        pl.BlockSpec((tm, D_MODEL), tok),
        full(w["norm1"]), full(w["w_in"]), full(w["dt_bias"]), full(w["qkn"]),
        pl.BlockSpec((tm, LANES), pos), pl.BlockSpec((tm, LANES), pos),
        full(w["red"]), full(w["exp"]),
    ]
    outs += [rows(CONV_DIM, F32), rows(LANES, F32), rows(4 * LANES, BF16), rows(4 * LANES, BF16)]
    return pl.pallas_call(
        _inproj_kernel,
        out_shape=tuple(o[0] for o in outs),
        grid=grid,
        in_specs=in_specs,
        out_specs=tuple(o[1] for o in outs),
        compiler_params=pltpu.CompilerParams(dimension_semantics=("parallel",),
                                             vmem_limit_bytes=VMEM_LIMIT),
        name="inproj",
    )(*operands)


def _pair_operands(kv):
    lo = lax.broadcasted_iota(jnp.int32, kv.shape, 1) < HEAD_DIM
    swapped = pltpu.roll(kv, HEAD_DIM, axis=1)
    parts = [jnp.where(lo, kv, 0.0), jnp.where(lo, 0.0, swapped), jnp.where(lo, swapped, 0.0), jnp.where(lo, 0.0, kv)]
    return jnp.concatenate(parts, axis=1).astype(BF16)


def _pair_rhs(blk, g):
    return jnp.concatenate([blk[:, 2 * g * LANES:(2 * g + 1) * LANES],
                            blk[:, (2 * g + 1) * LANES:(2 * g + 2) * LANES]], axis=0)


def _attn_qblock(sink_ref, q_blk, k_prev, k_cur, v_prev, v_cur, seq_start, o_ref):
    blk = WINDOW
    qi = lax.broadcasted_iota(jnp.int32, (blk, 2 * blk), 0)
    kj = lax.broadcasted_iota(jnp.int32, (blk, 2 * blk), 1) % blk
    cur_ok = kj <= qi
    lo = lax.broadcasted_iota(jnp.int32, (blk, LANES), 1) < HEAD_DIM
    n_pairs = N_HEADS // KV_HEADS // 2
    for g in range(KV_HEADS):
        q_all = jnp.concatenate([q_blk[:, (g * n_pairs + r) * LANES:(g * n_pairs + r + 1) * LANES]
                                 for r in range(n_pairs)], axis=0)
        s_all = _dot_nt(q_all, jnp.concatenate([_pair_rhs(k_cur, g), _pair_rhs(k_prev, g)], axis=0))
        p_rows = []
        den_rows = []
        for r in range(n_pairs):
            pair = g * n_pairs + r
            s_cur = s_all[r * blk:(r + 1) * blk, :2 * blk]
            s_prev = s_all[r * blk:(r + 1) * blk, 2 * blk:]
            s = jnp.where(cur_ok, s_cur, s_prev)
            if seq_start is not None:
                s = jnp.where(jnp.logical_or(cur_ok, jnp.logical_not(seq_start)), s, -jnp.inf)
            ps = []
            dens = []
            for hh in range(2):
                sink = sink_ref[2 * pair + hh]
                sh = s[:, hh * blk:(hh + 1) * blk]
                m = jnp.maximum(jnp.max(sh, axis=-1, keepdims=True), sink)
                p = jnp.exp(sh - m)
                dens.append(jnp.sum(p, axis=-1, keepdims=True) + jnp.exp(sink - m))
                ps.append(p)
            p2 = jnp.concatenate(ps, axis=1)
            p_rows.append(jnp.concatenate([jnp.where(cur_ok, p2, 0.0), jnp.where(cur_ok, 0.0, p2)],
                                          axis=1).astype(BF16))
            den_rows.append(jnp.where(lo, dens[0], dens[1]))
        o_all = _dot(jnp.concatenate(p_rows, axis=0),
                     jnp.concatenate([_pair_rhs(v_cur, g), _pair_rhs(v_prev, g)], axis=0))
        for r in range(n_pairs):
            pair = g * n_pairs + r
            o2 = o_all[r * blk:(r + 1) * blk, :]
            o_ref[:, pair * LANES:(pair + 1) * LANES] = (o2 / den_rows[r]).astype(BF16)


def _attn_kernel(sink_ref, q_ref, kc_ref, kp_ref, vc_ref, vp_ref, o_ref):
    blk = WINDOW
    first_step = pl.program_id(1) == 0
    for u in range(q_ref.shape[0] // blk):
        rows = slice(u * blk, (u + 1) * blk)
        prev_rows = slice((u - 1) * blk, u * blk)
        k_prev = kp_ref[...] if u == 0 else kc_ref[prev_rows, :]
        v_prev = vp_ref[...] if u == 0 else vc_ref[prev_rows, :]
        _attn_qblock(sink_ref, q_ref[rows, :], k_prev, kc_ref[rows, :], v_prev, vc_ref[rows, :],
                     first_step if u == 0 else None, o_ref.at[rows, :])


def _attn_prompt(q, k, v, sinks, batch, seq):
    n_sub = ATTN_QBLOCKS if seq % (ATTN_QBLOCKS * WINDOW) == 0 else 1
    rows = n_sub * WINDOW
    nb = seq // rows
    cur = lambda b, j, s: (b * nb + j, 0)
    prev = lambda b, j, s: (jnp.maximum((b * nb + j) * n_sub - 1, 0), 0)
    return pl.pallas_call(
        _attn_kernel,
        out_shape=jax.ShapeDtypeStruct((batch * seq, ATTN_WIDTH), BF16),
        grid_spec=pltpu.PrefetchScalarGridSpec(
            num_scalar_prefetch=1,
            grid=(batch, nb),
            in_specs=[
                pl.BlockSpec((rows, ATTN_WIDTH), cur),
                pl.BlockSpec((rows, 4 * LANES), cur), pl.BlockSpec((WINDOW, 4 * LANES), prev),
                pl.BlockSpec((rows, 4 * LANES), cur), pl.BlockSpec((WINDOW, 4 * LANES), prev),
            ],
            out_specs=pl.BlockSpec((rows, ATTN_WIDTH), cur),
        ),
        compiler_params=pltpu.CompilerParams(dimension_semantics=("parallel", "parallel"),
                                             vmem_limit_bytes=VMEM_LIMIT),
        name="attn_prompt",
    )(sinks, q, k, k, v, v)


def _ssd_kernel(xbc_ref, z_ref, dt_ref, convw_ref, convb_ref, alog_ref, dskip_ref, nw_ref,
                tri_ref, expand_ref, y_ref, st_ref, buf_ref, state_ref):
    c = pl.program_id(1)
    cl = SSD_CHUNK
    n_sub = xbc_ref.shape[0] // cl
    halo = CONV_HALO

    @pl.when(c == 0)
    def _():
        buf_ref[...] = jnp.zeros(buf_ref.shape, F32)
        state_ref[...] = jnp.zeros(state_ref.shape, F32)

    lane = lax.broadcasted_iota(jnp.int32, (1, LANES), 1)
    a_neg = jnp.where(lane < SSM_HEADS, -jnp.exp(alog_ref[...]), 0.0)
    tri = tri_ref[...]
    for u in range(n_sub):
        rows = slice(u * cl, (u + 1) * cl)
        if u == 0:
            x_ext = jnp.concatenate([buf_ref[...], xbc_ref[rows, :]], axis=0)
        else:
            x_ext = xbc_ref[u * cl - halo:(u + 1) * cl, :]
        _ssd_chunk(x_ext, z_ref[rows, :], dt_ref[rows, :], a_neg, tri, convw_ref, convb_ref, dskip_ref, nw_ref,
                   expand_ref, y_ref.at[rows, :], state_ref)
    buf_ref[...] = xbc_ref[n_sub * cl - halo:n_sub * cl, :]

    @pl.when(c == pl.num_programs(1) - 1)
    def _():
        st_ref[0] = state_ref[...]


def _ssd_chunk(x_ext, z, dt, a_neg, tri, convw_ref, convb_ref, dskip_ref, nw_ref, expand_ref, y_ref, state_ref):
    cl = SSD_CHUNK
    act = _causal_conv_silu(x_ext, convw_ref, convb_ref)
    xs = act[:, :SSM_WIDTH]
    bm = act[:, SSM_WIDTH:SSM_WIDTH + SSM_GROUPS * SSM_STATE].astype(BF16)
    cm = act[:, SSM_WIDTH + SSM_GROUPS * SSM_STATE:].astype(BF16)

    dta = dt * a_neg
    p_hi, p_mid, p_lo = _split3(dta)
    a_col = _dot(tri, p_hi) + _dot(tri, p_mid) + _dot(tri, p_lo)
    a_last = a_col[cl - 1:cl, :]
    a_row = a_col.T
    per_head = jnp.concatenate([dt, jnp.exp(a_col), jnp.exp(a_last - a_col)], axis=0)
    ph_hi, ph_lo = _split2(per_head)
    per_lane = _dot(ph_hi, expand_ref[...]) + _dot(ph_lo, expand_ref[...])
    dt_x = per_lane[:cl]
    ecol_x = per_lane[cl:2 * cl]
    dte_x = per_lane[2 * cl:]
    e_last = jnp.exp(a_last)
    xdt = xs * dt_x

    li = lax.broadcasted_iota(jnp.int32, (cl, cl), 0)
    si = lax.broadcasted_iota(jnp.int32, (cl, cl), 1)
    causal = si <= li
    lo = lax.broadcasted_iota(jnp.int32, (cl, LANES), 1) < SSM_HEAD_DIM
    row_lo = lax.broadcasted_iota(jnp.int32, (LANES, SSM_STATE), 0) < SSM_HEAD_DIM

    ys = []
    for g in range(SSM_GROUPS):
        b_g = bm[:, g * SSM_STATE:(g + 1) * SSM_STATE]
        c_g = cm[:, g * SSM_STATE:(g + 1) * SSM_STATE]
        cb = _dot_nt(c_g, b_g)
        for r in range(HEAD_PAIRS // SSM_GROUPS):
            j = g * (HEAD_PAIRS // SSM_GROUPS) + r
            sl = slice(j * LANES, (j + 1) * LANES)
            xdt_p = xdt[:, sl]
            ms = []
            for hh in range(2):
                h = 2 * j + hh
                seg = a_col[:, h:h + 1] - a_row[h:h + 1, :]
                ms.append(cb * jnp.exp(jnp.where(causal, seg, -jnp.inf)))
            m2 = jnp.concatenate(ms, axis=1).astype(BF16)
            rhs = jnp.concatenate([jnp.where(lo, xdt_p, 0.0), jnp.where(lo, 0.0, xdt_p)],
                                  axis=0).astype(BF16)
            y_diag = _dot(m2, rhs)
            st = state_ref[j]
            y_off = _dot_nt(c_g, st.astype(BF16)) * ecol_x[:, sl]
            xdt_e = (xdt_p * dte_x[:, sl]).T.astype(BF16)
            d_a = e_last[:, 2 * j:2 * j + 1]
            d_b = e_last[:, 2 * j + 1:2 * j + 2]
            decay = jnp.where(row_lo, jnp.broadcast_to(d_a, row_lo.shape), jnp.broadcast_to(d_b, row_lo.shape))
            state_ref[j] = decay * st + _dot(xdt_e, b_g)
            ys.append(y_diag + y_off + dskip_ref[:, sl] * xs[:, sl])
    y = jnp.concatenate(ys, axis=1)
    gated = y * _silu(z)
    gw = SSM_WIDTH // SSM_GROUPS
    outs = []
    for g in range(SSM_GROUPS):
        gg = gated[:, g * gw:(g + 1) * gw]
        outs.append(gg * lax.rsqrt(jnp.mean(gg * gg, axis=-1, keepdims=True) + EPS))
    y_ref[...] = (jnp.concatenate(outs, axis=1) * nw_ref[...]).astype(BF16)


def _ssd_prompt(xbc, z, dt, w, batch, seq):
    n_sub = SSD_CHUNKS_PER_STEP if seq % (SSD_CHUNKS_PER_STEP * SSD_CHUNK) == 0 else 1
    rows = n_sub * SSD_CHUNK
    nc = seq // rows
    tok = lambda b, c: (b * nc + c, 0)
    const = lambda b, c: (0, 0)
    full = lambda a: pl.BlockSpec(a.shape, const)
    return pl.pallas_call(
        _ssd_kernel,
        out_shape=(jax.ShapeDtypeStruct((batch * seq, SSM_WIDTH), BF16),
                   jax.ShapeDtypeStruct((batch, HEAD_PAIRS, LANES, SSM_STATE), F32)),
        grid=(batch, nc),
        in_specs=[
            pl.BlockSpec((rows, CONV_DIM), tok), pl.BlockSpec((rows, SSM_WIDTH), tok),
            pl.BlockSpec((rows, LANES), tok),
            full(w["conv_w"]), full(w["conv_b"]), full(w["a_log"]),
            full(w["d_skip"]), full(w["ssm_norm"]), full(w["tri"]), full(w["expand"]),
        ],
        out_specs=(pl.BlockSpec((rows, SSM_WIDTH), tok),
                   pl.BlockSpec((1, HEAD_PAIRS, LANES, SSM_STATE), lambda b, c: (b, 0, 0, 0))),
        scratch_shapes=[pltpu.VMEM((CONV_HALO, CONV_DIM), F32),
                        pltpu.VMEM((HEAD_PAIRS, LANES, SSM_STATE), F32)],
        compiler_params=pltpu.CompilerParams(dimension_semantics=("parallel", "arbitrary"),
                                             vmem_limit_bytes=VMEM_LIMIT),
        name="ssd_prompt",
    )(xbc, z, dt, w["conv_w"], w["conv_b"], w["a_log"], w["d_skip"], w["ssm_norm"], w["tri"], w["expand"])


def _attn_sample_kernel(qx_ref, kc_ref, kn_ref, vc_ref, vn_ref, sink_ref, ko_ref, vo_ref, o_ref):
    bs = qx_ref.shape[0]
    w = kc_ref.shape[2]
    sink = sink_ref[...]
    lo = lax.broadcasted_iota(jnp.int32, (1, LANES), 1) < HEAD_DIM
    newest = lax.broadcasted_iota(jnp.int32, (1, w), 1) == w - 1
    for i in range(bs):
        k_t = jnp.where(newest, kn_ref[i], pltpu.roll(kc_ref[i], w - 1, axis=1))
        v_t = jnp.where(newest, vn_ref[i], pltpu.roll(vc_ref[i], w - 1, axis=1))
        ko_ref[i] = k_t
        vo_ref[i] = v_t
        s = _dot(qx_ref[i], k_t.astype(BF16))
        m = jnp.maximum(jnp.max(s, axis=-1, keepdims=True), sink)
        p = jnp.exp(s - m)
        den = jnp.sum(p, axis=-1, keepdims=True) + jnp.exp(sink - m)
        o = _dot_nt(p.astype(BF16), v_t.astype(BF16)) / den
        o_sw = pltpu.roll(o, HEAD_DIM, axis=1)
        for j in range(N_HEADS // 2):
            a, b = (o, o_sw) if j < N_HEADS // 4 else (o_sw, o)
            o_ref[i:i + 1, j * LANES:(j + 1) * LANES] = jnp.where(lo, a[2 * j:2 * j + 1], b[2 * j + 1:2 * j + 2])


def _attn_sample(qx, kc, kn, vc, vn, sink_x, bs):
    n, w = kc.shape[0], kc.shape[2]
    blk3 = lambda i: (i, 0, 0)
    return pl.pallas_call(
        _attn_sample_kernel,
        out_shape=(jax.ShapeDtypeStruct((n, KV_WIDTH, w), F32),
                   jax.ShapeDtypeStruct((n, KV_WIDTH, w), F32),
                   jax.ShapeDtypeStruct((n, ATTN_WIDTH), F32)),
        grid=(n // bs,),
        in_specs=[
            pl.BlockSpec((bs, BF16_ROWS, LANES), blk3),
            pl.BlockSpec((bs, KV_WIDTH, w), blk3), pl.BlockSpec((bs, KV_WIDTH, 1), blk3),
            pl.BlockSpec((bs, KV_WIDTH, w), blk3), pl.BlockSpec((bs, KV_WIDTH, 1), blk3),
            pl.BlockSpec(sink_x.shape, lambda i: (0, 0)),
        ],
        out_specs=(pl.BlockSpec((bs, KV_WIDTH, w), blk3), pl.BlockSpec((bs, KV_WIDTH, w), blk3),
                   pl.BlockSpec((bs, ATTN_WIDTH), lambda i: (i, 0))),
        compiler_params=pltpu.CompilerParams(dimension_semantics=("parallel",),
                                             vmem_limit_bytes=VMEM_LIMIT),
        name="attn_sample",
    )(qx, kc, kn, vc, vn, sink_x)


def _ssd_sample_kernel(xbc_ref, z_ref, dt_ref, cprev_ref, h0_ref, convw_ref, convb_ref, alog_ref,
                       dskip_ref, nw_ref, y_ref, cnew_ref, h1_ref):
    bs = xbc_ref.shape[0]
    x_raw = xbc_ref[...]
    conv = convb_ref[...] + x_raw * convw_ref[CONV_K - 1:CONV_K, :]
    for j in range(CONV_K - 1):
        conv = conv + cprev_ref[j] * convw_ref[j:j + 1, :]
    for j in range(CONV_K - 2):
        cnew_ref[j] = cprev_ref[j + 1]
    cnew_ref[CONV_K - 2] = x_raw
    act = _silu(conv)
    xs = act[:, :SSM_WIDTH]
    bm = act[:, SSM_WIDTH:SSM_WIDTH + SSM_GROUPS * SSM_STATE].astype(BF16)
    cm = act[:, SSM_WIDTH + SSM_GROUPS * SSM_STATE:].astype(BF16)
    lane = lax.broadcasted_iota(jnp.int32, (1, LANES), 1)
    a_neg = jnp.where(lane < SSM_HEADS, -jnp.exp(alog_ref[...]), 0.0)
    dt = dt_ref[...]
    dec = jnp.exp(dt * a_neg)
    xdt = xs * _lane_bcast_pairs(dt, HEAD_PAIRS)
    rowid = lax.broadcasted_iota(jnp.int32, (bs, LANES), 0)
    row_lo = lax.broadcasted_iota(jnp.int32, (LANES, SSM_STATE), 0) < SSM_HEAD_DIM
    ys = []
    for j in range(HEAD_PAIRS):
        g = j // (HEAD_PAIRS // SSM_GROUPS)
        sl = slice(j * LANES, (j + 1) * LANES)
        b_g = bm[:, g * SSM_STATE:(g + 1) * SSM_STATE]
        c_g = cm[:, g * SSM_STATE:(g + 1) * SSM_STATE]
        xdt_p = xdt[:, sl]
        y_p = jnp.zeros((bs, LANES), F32)
        for i in range(bs):
            xi = jnp.where(rowid == i, xdt_p, 0.0).T.astype(BF16)
            d_a = dec[i:i + 1, 2 * j:2 * j + 1]
            d_b = dec[i:i + 1, 2 * j + 1:2 * j + 2]
            decay = jnp.where(row_lo, jnp.broadcast_to(d_a, row_lo.shape), jnp.broadcast_to(d_b, row_lo.shape))
            new = decay * h0_ref[i, j] + _dot(xi, b_g)
            h1_ref[i, j] = new
            y_p = y_p + jnp.where(rowid == i, _dot_nt(c_g, new.astype(BF16)), 0.0)
        ys.append(y_p + dskip_ref[:, sl] * xs[:, sl])
    y = jnp.concatenate(ys, axis=1)
    gated = y * _silu(z_ref[...])
    gw = SSM_WIDTH // SSM_GROUPS
    outs = []
    for g in range(SSM_GROUPS):
        gg = gated[:, g * gw:(g + 1) * gw]
        outs.append(gg * lax.rsqrt(jnp.mean(gg * gg, axis=-1, keepdims=True) + EPS))
    y_ref[...] = (jnp.concatenate(outs, axis=1) * nw_ref[...]).astype(BF16)


def _ssd_sample(xbc, z, dt, cprev_t, h0, w, bs):
    n = xbc.shape[0]
    tok = lambda i: (i, 0)
    const = lambda i: (0, 0)
    full = lambda a: pl.BlockSpec(a.shape, const)
    return pl.pallas_call(
        _ssd_sample_kernel,
        out_shape=(jax.ShapeDtypeStruct((n, SSM_WIDTH), BF16),
                   jax.ShapeDtypeStruct((CONV_K - 1, n, CONV_DIM), F32),
                   jax.ShapeDtypeStruct((n, HEAD_PAIRS, LANES, SSM_STATE), F32)),
        grid=(n // bs,),
        in_specs=[
            pl.BlockSpec((bs, CONV_DIM), tok), pl.BlockSpec((bs, SSM_WIDTH), tok),
            pl.BlockSpec((bs, LANES), tok),
            pl.BlockSpec((CONV_K - 1, bs, CONV_DIM), lambda i: (0, i, 0)),
            pl.BlockSpec((bs, HEAD_PAIRS, LANES, SSM_STATE), lambda i: (i, 0, 0, 0)),
            full(w["conv_w"]), full(w["conv_b"]), full(w["a_log"]),
            full(w["d_skip"]), full(w["ssm_norm"]),
        ],
        out_specs=(pl.BlockSpec((bs, SSM_WIDTH), tok),
                   pl.BlockSpec((CONV_K - 1, bs, CONV_DIM), lambda i: (0, i, 0)),
                   pl.BlockSpec((bs, HEAD_PAIRS, LANES, SSM_STATE), lambda i: (i, 0, 0, 0))),
        compiler_params=pltpu.CompilerParams(dimension_semantics=("parallel",),
                                             vmem_limit_bytes=VMEM_LIMIT),
        name="ssd_sample",
    )(xbc, z, dt, cprev_t, h0, w["conv_w"], w["conv_b"], w["a_log"], w["d_skip"], w["ssm_norm"])


def _pack_bf16_pair(v):
    c = v.shape[1] // 2
    hi = lax.bitcast_convert_type(v[:, :c].astype(BF16).astype(F32), jnp.uint32)
    lo = lax.bitcast_convert_type(v[:, c:].astype(BF16).astype(F32), jnp.uint32)
    return hi | (lo >> 16)


def _unpack_bf16_pair(word):
    a = lax.bitcast_convert_type(word & jnp.uint32(0xFFFF0000), F32)
    b = lax.bitcast_convert_type(word << 16, F32)
    return a, b


def _outproj_router_kernel(x_ref, a_ref, s_ref, wo_ref, n2_ref, wr_ref, br_ref, tri_ref,
                           h_ref, t_ref, route_ref, routet_ref, cnt_ref, carry_ref):
    @pl.when(pl.program_id(0) == 0)
    def _():
        carry_ref[...] = jnp.zeros(carry_ref.shape, F32)

    tm = x_ref.shape[0]
    h_ref[...] = (x_ref[...] + _dot(a_ref[...].astype(BF16), wo_ref[:ATTN_WIDTH, :])
                  + _dot(s_ref[...].astype(BF16), wo_ref[ATTN_WIDTH:, :]))
    sub = min(tm, ROUTER_SUBTILE)
    carry = carry_ref[:, 0:1]
    for s in range(tm // sub):
        rows = slice(s * sub, (s + 1) * sub)
        carry = _route_rows(h_ref[rows, :], n2_ref, wr_ref, br_ref, tri_ref, carry, t_ref.at[rows, :],
                            route_ref.at[rows, :], routet_ref.at[:, rows])
    carry_ref[...] = jnp.broadcast_to(carry, carry_ref.shape)
    cpad = jnp.concatenate([jnp.zeros((EXPERT_LANE0, LANES), F32), jnp.broadcast_to(carry, (N_EXPERTS, LANES)),
                            jnp.zeros((LANES - EXPERT_LANE0 - N_EXPERTS, LANES), F32)], axis=0)
    cnt_ref[...] = cpad.T[0:cnt_ref.shape[0], :]


def _route_rows(h, n2_ref, wrt_ref, brc_ref, triu_ref, carry, t_ref, route_ref, routet_ref):
    ms = jnp.mean(h * h, axis=-1, keepdims=True)
    t = h * lax.rsqrt(ms + EPS) * n2_ref[...]
    t_ref[...] = _pack_bf16_pair(t)
    n = h.shape[0]
    a = _dot_nt(wrt_ref[...], t.astype(BF16))

    def logit_rows(r0, r1):
        return a[r0:r1] + a[LANES + r0:LANES + r1] + brc_ref[r0:r1, :]

    sl = EXP_PER_GROUP
    glog = logit_rows(0, sl)
    elog = logit_rows(EXPERT_LANE0, EXPERT_LANE0 + N_EXPERTS)
    row = lax.broadcasted_iota(jnp.int32, (sl, n), 0).astype(F32)
    big = float(sl)
    ninf = -jnp.inf
    gm = jnp.where(row < N_EGROUPS, glog, ninf)
    gmax = jnp.max(gm, axis=0, keepdims=True)
    g_top = 1.0 / jnp.sum(jnp.exp(gm - gmax), axis=0, keepdims=True)
    g_idx = jnp.min(jnp.where(gm == gmax, row, big), axis=0, keepdims=True)
    ml = elog[(N_EGROUPS - 1) * sl:]
    for g in range(N_EGROUPS - 2, -1, -1):
        ml = jnp.where(g_idx == float(g), elog[g * sl:(g + 1) * sl], ml)
    m1 = jnp.max(ml, axis=0, keepdims=True)
    i1 = jnp.min(jnp.where(ml == m1, row, big), axis=0, keepdims=True)
    ml2 = jnp.where(row == i1, ninf, ml)
    m2 = jnp.max(ml2, axis=0, keepdims=True)
    i2 = jnp.min(jnp.where(ml2 == m2, row, big), axis=0, keepdims=True)
    r = jnp.exp(m2 - m1)
    w1 = g_top / (1.0 + r)
    w2 = g_top * r / (1.0 + r)
    e1 = g_idx * float(sl) + i1
    e2 = g_idx * float(sl) + i2
    pick = jnp.where(jnp.logical_or(row == i1, row == i2), 1.0, 0.0)
    onehot = jnp.concatenate([jnp.where(g_idx == float(g), pick, 0.0) for g in range(N_EGROUPS)], axis=0)
    onehot_bf = onehot.astype(BF16)
    cb = triu_ref.shape[0]
    cums = []
    for blk in range(n // cb):
        c = _dot(onehot_bf[:, blk * cb:(blk + 1) * cb], triu_ref[...]) + carry
        carry = c[:, cb - 1:cb]
        cums.append(c)
    before = jnp.concatenate(cums, axis=1) - onehot
    erow = lax.broadcasted_iota(jnp.int32, (N_EXPERTS, n), 0).astype(F32)
    rank1 = jnp.sum(jnp.where(erow == e1, before, 0.0), axis=0, keepdims=True)
    rank2 = jnp.sum(jnp.where(erow == e2, before, 0.0), axis=0, keepdims=True)
    fields = jnp.concatenate([e1, e2, w1, w2, rank1, rank2], axis=0)
    routet_ref[0:fields.shape[0], :] = fields
    routet_ref[fields.shape[0]:, :] = jnp.zeros((routet_ref.shape[0] - fields.shape[0], n), F32)
    if n % LANES:
        fields = jnp.concatenate([fields, jnp.zeros((fields.shape[0], LANES - n % LANES), F32)], axis=1)
    pad = jnp.zeros((LANES - fields.shape[0], LANES), F32)
    for j in range(fields.shape[1] // LANES):
        blk_rows = jnp.concatenate([fields[:, j * LANES:(j + 1) * LANES], pad], axis=0)
        n_valid = min(LANES, n - j * LANES)
        route_ref[j * LANES:j * LANES + n_valid, :] = blk_rows.T[:n_valid, :]
    return carry


def _outproj_router(x2d, attn, ssm, w, tm):
    t = x2d.shape[0]
    tok = lambda i: (i, 0)
    const = lambda i: (0, 0)
    full = lambda a: pl.BlockSpec(a.shape, const)
    idx = jnp.arange(min(tm, COUNT_BLOCK))
    tri = (idx[:, None] <= idx[None, :]).astype(BF16)
    return pl.pallas_call(
        _outproj_router_kernel,
        out_shape=(jax.ShapeDtypeStruct((t, D_MODEL), F32), jax.ShapeDtypeStruct((t, D_MODEL // 2), jnp.uint32),
                   jax.ShapeDtypeStruct((t, LANES), F32), jax.ShapeDtypeStruct((8, t), F32),
                   jax.ShapeDtypeStruct((8, LANES), F32)),
        grid=(t // tm,),
        in_specs=[
            pl.BlockSpec((tm, D_MODEL), tok), pl.BlockSpec((tm, ATTN_WIDTH), tok),
            pl.BlockSpec((tm, SSM_WIDTH), tok),
            full(w["w_out"]), full(w["norm2"]), full(w["wrt"]),
            full(w["br_col"]), full(tri),
        ],
        out_specs=(pl.BlockSpec((tm, D_MODEL), tok), pl.BlockSpec((tm, D_MODEL // 2), tok),
                   pl.BlockSpec((tm, LANES), tok), pl.BlockSpec((8, tm), lambda i: (0, i)),
                   pl.BlockSpec((8, LANES), const)),
        scratch_shapes=[pltpu.VMEM((N_EXPERTS, LANES), F32)],
        compiler_params=pltpu.CompilerParams(dimension_semantics=("arbitrary",),
                                             vmem_limit_bytes=VMEM_LIMIT),
        name="outproj_router",
    )(x2d, attn, ssm, w["w_out"], w["norm2"], w["wrt"], w["br_col"], tri)


def _expert_hidden(t_a, t_b, w1):
    half = D_MODEL // 2
    gu = _dot(t_a, w1[:half]) + _dot(t_b, w1[half:])
    return _silu(gu[:, :EXPERT_FF]) * gu[:, EXPERT_FF:]


def _gate_up_bf16(wg_ref, wu_ref):
    return jnp.concatenate([wg_ref[0].astype(BF16), wu_ref[0].astype(BF16)], axis=1)


def _moe_dense_kernel(h_ref, t_ref, route_ref, wg_ref, wu_ref, wd_ref, y_ref):
    e = pl.program_id(0)

    @pl.when(e == 0)
    def _():
        y_ref[...] = h_ref[...]

    t_a, t_b = _unpack_bf16_pair(t_ref[...])
    route = route_ref[...]
    e1, e2, g1, g2 = route[:, 0:1], route[:, 1:2], route[:, 2:3], route[:, 3:4]
    e_f = e.astype(F32)
    hid = _expert_hidden(t_a.astype(BF16), t_b.astype(BF16), _gate_up_bf16(wg_ref, wu_ref))
    c_e = jnp.where(e1 == e_f, g1, 0.0) + jnp.where(e2 == e_f, g2, 0.0)
    y_ref[...] += _dot((hid * c_e).astype(BF16), wd_ref[0].astype(BF16))


def _moe_dense(h, t, route, w):
    n = h.shape[0]
    whole = lambda e: (0, 0)
    by_expert = lambda e: (e, 0, 0)
    return pl.pallas_call(
        _moe_dense_kernel,
        out_shape=jax.ShapeDtypeStruct((n, D_MODEL), F32),
        grid=(N_EXPERTS,),
        in_specs=[pl.BlockSpec((n, D_MODEL), whole), pl.BlockSpec((n, D_MODEL // 2), whole),
                  pl.BlockSpec((n, LANES), whole),
                  pl.BlockSpec((1, D_MODEL, EXPERT_FF), by_expert), pl.BlockSpec((1, D_MODEL, EXPERT_FF), by_expert),
                  pl.BlockSpec((1, EXPERT_FF, D_MODEL), by_expert)],
        out_specs=pl.BlockSpec((n, D_MODEL), whole),
        compiler_params=pltpu.CompilerParams(dimension_semantics=("arbitrary",),
                                             vmem_limit_bytes=VMEM_LIMIT),
        name="moe_dense",
    )(h, t, route, w["w_gate"], w["w_up"], w["w_down"])


def _sc_scatter_rows(src, pos1, pos2, n_out):
    t, width = src.shape
    rows_per_worker = t // SC_WORKERS
    n_chunks = rows_per_worker // SC_CHUNK
    assert t == SC_WORKERS * SC_CHUNK * n_chunks and n_chunks % 2 == 0
    mesh = plsc.VectorSubcoreMesh(core_axis_name="c", subcore_axis_name="s")

    @functools.partial(
        pl.kernel, mesh=mesh,
        out_type=jax.ShapeDtypeStruct((n_out, width), src.dtype),
        scratch_types=[pltpu.VMEM((2, SC_CHUNK), jnp.int32), pltpu.VMEM((2, SC_CHUNK), jnp.int32),
                       pltpu.VMEM((2, SC_CHUNK, width), src.dtype),
                       pltpu.SemaphoreType.DMA, pltpu.SemaphoreType.DMA,
                       pltpu.SemaphoreType.DMA, pltpu.SemaphoreType.DMA],
    )
    def scatter_kernel(src_hbm, p1_hbm, p2_hbm, out_hbm, i1_v, i2_v, rows_v, l0, l1, s0, s1):
        wid = lax.axis_index("s") * SC_CORES + lax.axis_index("c")
        base = wid * rows_per_worker
        lsem = (l0, l1)
        ssem = (s0, s1)

        def load_copy(c, slot):
            off = pl.multiple_of(base + c * SC_CHUNK, 8)
            return pltpu.make_async_copy(src_hbm.at[pl.ds(off, SC_CHUNK)], rows_v.at[slot], lsem[slot])

        def start_load(c, slot):
            off = pl.multiple_of(base + c * SC_CHUNK, 8)
            pltpu.sync_copy(p1_hbm.at[pl.ds(off, SC_CHUNK)], i1_v.at[slot])
            pltpu.sync_copy(p2_hbm.at[pl.ds(off, SC_CHUNK)], i2_v.at[slot])
            load_copy(c, slot).start()

        def scatter_copies(slot):
            return (pltpu.make_async_copy(rows_v.at[slot], out_hbm.at[i1_v.at[slot]], ssem[slot]),
                    pltpu.make_async_copy(rows_v.at[slot], out_hbm.at[i2_v.at[slot]], ssem[slot]))

        def start_scatter(slot):
            for cp in scatter_copies(slot):
                cp.start()

        def wait_scatter(slot):
            for cp in scatter_copies(slot):
                cp.wait()

        start_load(0, 0)

        @pl.loop(0, n_chunks, step=2)
        def _(c):
            @pl.when(c > 0)
            def _():
                wait_scatter(1)

            start_load(c + 1, 1)
            load_copy(c, 0).wait()
            start_scatter(0)
            load_copy(c + 1, 1).wait()
            wait_scatter(0)
            start_scatter(1)

            @pl.when(c + 2 < n_chunks)
            def _():
                start_load(c + 2, 0)

        wait_scatter(1)

    return scatter_kernel(src, pos1, pos2)


def _sc_gather_rows(table, idx):
    n, width = idx.shape[0], table.shape[1]
    rows_per_worker = n // SC_WORKERS
    n_chunks = rows_per_worker // SC_CHUNK
    assert n == SC_WORKERS * SC_CHUNK * n_chunks and n_chunks % 2 == 0
    mesh = plsc.VectorSubcoreMesh(core_axis_name="c", subcore_axis_name="s")

    @functools.partial(
        pl.kernel, mesh=mesh,
        out_type=jax.ShapeDtypeStruct((n, width), table.dtype),
        scratch_types=[pltpu.VMEM((2, SC_CHUNK), jnp.int32), pltpu.VMEM((2, SC_CHUNK, width), table.dtype),
                       pltpu.SemaphoreType.DMA, pltpu.SemaphoreType.DMA,
                       pltpu.SemaphoreType.DMA, pltpu.SemaphoreType.DMA],
    )
    def gather_kernel(table_hbm, idx_hbm, out_hbm, idx_v, rows_v, g0, g1, w0, w1):
        wid = lax.axis_index("s") * SC_CORES + lax.axis_index("c")
        base = wid * rows_per_worker
        gsem = (g0, g1)
        wsem = (w0, w1)

        def gather_copy(slot):
            return pltpu.make_async_copy(table_hbm.at[idx_v.at[slot]], rows_v.at[slot], gsem[slot])

        def write_copy(c, slot):
            off = pl.multiple_of(base + c * SC_CHUNK, 8)
            return pltpu.make_async_copy(rows_v.at[slot], out_hbm.at[pl.ds(off, SC_CHUNK)], wsem[slot])

        def start_gather(c, slot):
            off = pl.multiple_of(base + c * SC_CHUNK, 8)
            pltpu.sync_copy(idx_hbm.at[pl.ds(off, SC_CHUNK)], idx_v.at[slot])
            gather_copy(slot).start()

        start_gather(0, 0)

        @pl.loop(0, n_chunks, step=2)
        def _(c):
            @pl.when(c > 0)
            def _():
                write_copy(c - 1, 1).wait()

            start_gather(c + 1, 1)
            gather_copy(0).wait()
            write_copy(c, 0).start()
            gather_copy(1).wait()
            write_copy(c + 1, 1).start()
            write_copy(c, 0).wait()

            @pl.when(c + 2 < n_chunks)
            def _():
                start_gather(c + 2, 0)

        write_copy(n_chunks - 1, 1).wait()

    return gather_kernel(table, idx)


def _moe_grouped_kernel(te_ref, nt_ref, order_ref, x_ref, wg_ref, wu_ref, wd_ref, o_ref, w1_bf_ref, w2_bf_ref):
    del order_ref
    i = pl.program_id(0)

    @pl.when(jnp.logical_or(i == 0, te_ref[i] != te_ref[jnp.maximum(i - 1, 0)]))
    def _():
        w1_bf_ref[...] = _gate_up_bf16(wg_ref, wu_ref)
        w2_bf_ref[...] = wd_ref[0].astype(BF16)

    @pl.when(i < nt_ref[0])
    def _():
        for s in range(MOE_TILE // MOE_SUBTILE):
            rows = slice(s * MOE_SUBTILE, (s + 1) * MOE_SUBTILE)
            t_a, t_b = _unpack_bf16_pair(x_ref[rows, :])
            hid = _expert_hidden(t_a.astype(BF16), t_b.astype(BF16), w1_bf_ref[...])
            o_ref[rows, :] = _pack_bf16_pair(_dot(hid.astype(BF16), w2_bf_ref[...]))


def _moe_grouped(xs, tile_expert, n_tiles, order, w):
    rows = xs.shape[0]
    row = lambda i, te, nt, od: (jnp.minimum(i, nt[0] - 1), 0)
    by_expert = lambda i, te, nt, od: (te[i], 0, 0)
    return pl.pallas_call(
        _moe_grouped_kernel,
        out_shape=jax.ShapeDtypeStruct((rows, D_MODEL // 2), jnp.uint32),
        grid_spec=pltpu.PrefetchScalarGridSpec(
            num_scalar_prefetch=3,
            grid=(rows // MOE_TILE,),
            in_specs=[pl.BlockSpec((MOE_TILE, D_MODEL // 2), row),
                      pl.BlockSpec((1, D_MODEL, EXPERT_FF), by_expert),
                      pl.BlockSpec((1, D_MODEL, EXPERT_FF), by_expert),
                      pl.BlockSpec((1, EXPERT_FF, D_MODEL), by_expert)],
            out_specs=pl.BlockSpec((MOE_TILE, D_MODEL // 2), row),
            scratch_shapes=[pltpu.VMEM((D_MODEL, 2 * EXPERT_FF), BF16), pltpu.VMEM((EXPERT_FF, D_MODEL), BF16)],
        ),
        compiler_params=pltpu.CompilerParams(dimension_semantics=("arbitrary",),
                                             vmem_limit_bytes=VMEM_LIMIT),
        name="moe_grouped",
    )(tile_expert, n_tiles, order, xs, w["w_gate"], w["w_up"], w["w_down"])


def _moe_combine_kernel(h_ref, z1_ref, z2_ref, route_ref, y_ref):
    route = route_ref[...]
    g1, g2 = route[:, 2:3], route[:, 3:4]
    half = D_MODEL // 2
    a1, b1 = _unpack_bf16_pair(z1_ref[...])
    a2, b2 = _unpack_bf16_pair(z2_ref[...])
    y_ref[:, :half] = h_ref[:, :half] + g1 * a1 + g2 * a2
    y_ref[:, half:] = h_ref[:, half:] + g1 * b1 + g2 * b2


def _moe_combine(h, z, route, tm):
    t = h.shape[0]
    nb = t // tm
    tok = lambda i: (i, 0)
    return pl.pallas_call(
        _moe_combine_kernel,
        out_shape=jax.ShapeDtypeStruct((t, D_MODEL), F32),
        grid=(nb,),
        in_specs=[pl.BlockSpec((tm, D_MODEL), tok), pl.BlockSpec((tm, D_MODEL // 2), tok),
                  pl.BlockSpec((tm, D_MODEL // 2), lambda i: (i + nb, 0)), pl.BlockSpec((tm, LANES), tok)],
        out_specs=pl.BlockSpec((tm, D_MODEL), tok),
        compiler_params=pltpu.CompilerParams(dimension_semantics=("parallel",),
                                             vmem_limit_bytes=VMEM_LIMIT),
        name="moe_combine",
    )(h, z, z, route)


def _route_pos_kernel(routet_ref, cnt_ref, upper_ref, pos_ref):
    tm = routet_ref.shape[1]
    cnt = cnt_ref[...]
    padded = jnp.floor((cnt + float(MOE_TILE - 1)) * (1.0 / MOE_TILE)) * float(MOE_TILE)
    p_hi, p_mid, p_lo = _split3(padded)
    upper = upper_ref[...]
    starts = (_dot(p_hi, upper) + _dot(p_mid, upper) + _dot(p_lo, upper))[0:1, :]
    starts_col = jnp.broadcast_to(starts, (LANES, LANES)).T[EXPERT_LANE0:EXPERT_LANE0 + N_EXPERTS, 0:1]
    erow = lax.broadcasted_iota(jnp.int32, (N_EXPERTS, tm), 0).astype(F32)
    for k in range(2):
        e_k = routet_ref[k:k + 1, :]
        pos = (jnp.sum(jnp.where(erow == e_k, starts_col, 0.0), axis=0, keepdims=True)
               + routet_ref[4 + k:5 + k, :]).astype(jnp.int32)
        for r in range(tm // LANES):
            pos_ref[k, r:r + 1, :] = pos[:, r * LANES:(r + 1) * LANES]


def _route_positions(routet, counts, tm):
    t = routet.shape[1]
    idx = jnp.arange(LANES)
    upper = (idx[:, None] < idx[None, :]).astype(BF16)
    return pl.pallas_call(
        _route_pos_kernel,
        out_shape=jax.ShapeDtypeStruct((2, t // LANES, LANES), jnp.int32),
        grid=(t // tm,),
        in_specs=[pl.BlockSpec((8, tm), lambda i: (0, i)), pl.BlockSpec((8, LANES), lambda i: (0, 0)),
                  pl.BlockSpec((LANES, LANES), lambda i: (0, 0))],
        out_specs=pl.BlockSpec((2, tm // LANES, LANES), lambda i: (0, i, 0)),
        compiler_params=pltpu.CompilerParams(dimension_semantics=("parallel",),
                                             vmem_limit_bytes=VMEM_LIMIT),
        name="route_positions",
    )(routet, counts, upper)


def _moe_routed(h, t_packed, route, routet, counts, w, tm, run_first):
    t = h.shape[0]
    pos = _route_positions(routet, counts, min(32 * LANES, t))
    pos1 = pos[0].reshape(t)
    pos2 = pos[1].reshape(t)
    cnt = counts[0, EXPERT_LANE0:EXPERT_LANE0 + N_EXPERTS].astype(jnp.int32)
    padded = (cnt + MOE_TILE - 1) // MOE_TILE * MOE_TILE
    ends = jnp.cumsum(padded)
    n_rows = 2 * t + N_EXPERTS * MOE_TILE
    n_tiles = ends[N_EXPERTS - 1] // MOE_TILE
    tile_start = jnp.arange(n_rows // MOE_TILE, dtype=jnp.int32) * MOE_TILE
    tile_start = jnp.minimum(tile_start, ends[N_EXPERTS - 1] - MOE_TILE)
    tile_expert = jnp.sum((tile_start[:, None] >= ends[None, :]).astype(jnp.int32), axis=1)
    xs = _sc_scatter_rows(t_packed, pos1, pos2, n_rows)
    order = lax.bitcast_convert_type(run_first.reshape(-1)[:1].astype(F32), jnp.int32)
    out = _moe_grouped(xs, tile_expert, n_tiles.reshape(1), order, w)
    z = _sc_gather_rows(out, pos.reshape(2 * t))
    return _moe_combine(h, z, route, tm)


def _pad_lanes(a, width=LANES):
    return jnp.pad(a, ((0, 0), (0, width - a.shape[1])))


def _prep_weights(norm1, w_in, q_norm, k_norm, conv_w, conv_b, dt_bias, a_log, d_skip, ssm_norm, w_out,
                  norm2, w_grp, b_grp, w_exp, b_exp, w_gate, w_up, w_down):
    w = {}
    w["norm1"] = norm1.reshape(1, D_MODEL)
    w["w_in"] = _pad_lanes(w_in, XBC_END + LANES).astype(BF16)
    w["qkn"] = jnp.concatenate([jnp.tile(q_norm, N_HEADS), jnp.tile(k_norm, KV_HEADS)]).reshape(1, QK_WIDTH)
    head_of_col = jnp.arange(QK_WIDTH) // HEAD_DIM
    red = (head_of_col[:, None] == jnp.arange(LANES)[None, :])
    w["red"] = red.astype(BF16)
    w["exp"] = red.T.astype(BF16)
    w["conv_w"] = conv_w
    w["conv_b"] = conv_b.reshape(1, CONV_DIM)
    w["dt_bias"] = _pad_lanes(dt_bias.reshape(1, SSM_HEADS))
    w["a_log"] = _pad_lanes(a_log.reshape(1, SSM_HEADS))
    w["d_skip"] = jnp.repeat(d_skip, SSM_HEAD_DIM).reshape(1, SSM_WIDTH)
    w["ssm_norm"] = ssm_norm.reshape(1, SSM_WIDTH)
    idx = jnp.arange(SSD_CHUNK)
    w["tri"] = (idx[None, :] <= idx[:, None]).astype(BF16)
    lane_head = jnp.arange(SSM_WIDTH) // SSM_HEAD_DIM
    w["expand"] = (jnp.arange(LANES)[:, None] == lane_head[None, :]).astype(BF16)
    w["w_out"] = w_out.astype(BF16)
    w["norm2"] = norm2.reshape(1, D_MODEL)
    wr = jnp.zeros((D_MODEL, LANES), F32)
    wr = wr.at[:, :N_EGROUPS].set(w_grp).at[:, EXPERT_LANE0:EXPERT_LANE0 + N_EXPERTS].set(w_exp)
    wr_hi = wr.astype(BF16)
    w["wrt"] = jnp.concatenate([wr_hi, (wr - wr_hi.astype(F32)).astype(BF16)], axis=1).T
    br = jnp.zeros((LANES, 1), F32)
    w["br_col"] = br.at[:N_EGROUPS, 0].set(b_grp).at[EXPERT_LANE0:EXPERT_LANE0 + N_EXPERTS, 0].set(b_exp)
    w["w_gate"], w["w_up"], w["w_down"] = w_gate, w_up, w_down
    return w


def _rope_tables(pos):
    inv = 1.0 / (ROPE_THETA ** (jnp.arange(0, HEAD_DIM, 2, dtype=F32) / HEAD_DIM))
    ang = pos.astype(F32)[:, None] * inv[None, :]
    cos, sin = jnp.cos(ang), jnp.sin(ang)
    reps = LANES // HEAD_DIM
    return (jnp.tile(jnp.concatenate([cos, cos], axis=-1), (1, reps)),
            jnp.tile(jnp.concatenate([-sin, sin], axis=-1), (1, reps)))


def _token_tile(t):
    for tm in (1024, 512, 256, 128, 64, 32, 16):
        if t % tm == 0:
            return tm
    raise ValueError(f"token count {t} is not a multiple of 16")


def kernel(x_prompt, x_sample, cache_win_k, cache_win_v, state_conv, state_ssm, norm1, w_in, q_norm, k_norm,
           sinks, conv_w, conv_b, dt_bias, a_log, d_skip, ssm_norm, w_out, norm2, w_grp, b_grp, w_exp, b_exp,
           w_gate, w_up, w_down):
    depth = norm1.shape[0]
    assert depth == 1, "single-layer stack"
    bp, lp, _ = x_prompt.shape
    bsn, ls, _ = x_sample.shape
    assert ls == 1 and lp % WINDOW == 0 and cache_win_k.shape[2] == WINDOW
    l = 0
    w = _prep_weights(norm1[l], w_in[l], q_norm[l], k_norm[l], conv_w[l], conv_b[l], dt_bias[l], a_log[l],
                      d_skip[l], ssm_norm[l], w_out[l], norm2[l], w_grp[l], b_grp[l], w_exp[l], b_exp[l],
                      w_gate[l], w_up[l], w_down[l])
    sink = sinks[l]

    tp = bp * lp
    xp = x_prompt.reshape(tp, D_MODEL)
    tm_p = _token_tile(lp)
    cos_p, sin_p = _rope_tables(jnp.arange(lp, dtype=jnp.int32))
    q, k, v, z, xbc, dt, k2, v2 = _inproj(xp, w, cos_p, sin_p, tm_p, lp // tm_p)
    attn = _attn_prompt(q, k2, v2, sink, bp, lp)
    ssm, st_p = _ssd_prompt(xbc, z, dt, w, bp, lp)
    h, t, route, routet, counts = _outproj_router(xp, attn, ssm, w, tm_p)
    k3 = k.reshape(bp, lp, KV_HEADS, HEAD_DIM)
    v3 = v.reshape(bp, lp, KV_HEADS, HEAD_DIM)
    win_k_p = k3[:, lp - WINDOW:][None]
    win_v_p = v3[:, lp - WINDOW:][None]
    conv_p = xbc.reshape(bp, lp, CONV_DIM)[:, lp - (CONV_K - 1):][None]
    ssm_p = st_p.reshape(1, bp, SSM_HEADS, SSM_HEAD_DIM, SSM_STATE)

    xs2 = x_sample.reshape(bsn, D_MODEL)
    tm_s = _token_tile(bsn)
    cos_s, sin_s = _rope_tables(jnp.full((tm_s,), PAST_LEN, jnp.int32))
    q_s, k_s, v_s, z_s, xbc_s, dt_s, _, _ = _inproj(xs2, w, cos_s, sin_s, tm_s, 1)
    q4 = q_s.reshape(bsn, KV_HEADS, N_HEADS // KV_HEADS, HEAD_DIM)
    zq = jnp.zeros_like(q4[:, 0])
    qx = jnp.concatenate([jnp.concatenate([q4[:, 0], zq], axis=-1),
                          jnp.concatenate([zq, q4[:, 1]], axis=-1)], axis=1)
    qx = jnp.pad(qx, ((0, 0), (0, BF16_ROWS - N_HEADS), (0, 0)))
    sink_x = jnp.pad(jnp.broadcast_to(sink[:, None], (N_HEADS, LANES)), ((0, BF16_ROWS - N_HEADS), (0, 0)))
    kc = jnp.transpose(cache_win_k[l], (0, 2, 3, 1)).reshape(bsn, KV_WIDTH, WINDOW)
    vc = jnp.transpose(cache_win_v[l], (0, 2, 3, 1)).reshape(bsn, KV_WIDTH, WINDOW)
    ko, vo, attn_s = _attn_sample(qx, kc, k_s.reshape(bsn, KV_WIDTH, 1), vc, v_s.reshape(bsn, KV_WIDTH, 1),
                               sink_x, 8)
    y_prompt = _moe_routed(h, t, route, routet, counts, w, tm_p, attn_s).reshape(bp, lp, D_MODEL)
    cprev_t = jnp.transpose(state_conv[l], (1, 0, 2))
    h0 = state_ssm[l].reshape(bsn, HEAD_PAIRS, LANES, SSM_STATE)
    ssm_s, cnew_t, h1 = _ssd_sample(xbc_s, z_s, dt_s, cprev_t, h0, w, 16)
    h_s, t_s, route_s, _, _ = _outproj_router(xs2, attn_s, ssm_s, w, tm_s)
    y_sample = _moe_dense(h_s, t_s, route_s, w).reshape(bsn, 1, D_MODEL)
    win_k_s = jnp.transpose(ko.reshape(bsn, KV_HEADS, HEAD_DIM, WINDOW), (0, 3, 1, 2))[None]
    win_v_s = jnp.transpose(vo.reshape(bsn, KV_HEADS, HEAD_DIM, WINDOW), (0, 3, 1, 2))[None]
    conv_s = jnp.transpose(cnew_t, (1, 0, 2))[None]
    ssm_s_state = h1.reshape(1, bsn, SSM_HEADS, SSM_HEAD_DIM, SSM_STATE)

    return (y_prompt, y_sample, win_k_p, win_v_p, conv_p, ssm_p, win_k_s, win_v_s, conv_s, ssm_s_state)
```

```python
import functools
import math

import jax
import jax.numpy as jnp
from jax import lax
from jax.experimental import pallas as pl
from jax.experimental.pallas import tpu as pltpu
from jax.experimental.pallas import tpu_sc as plsc

F32 = jnp.float32
BF16 = jnp.bfloat16

D_MODEL = 1024
HEAD_DIM = 64
N_HEADS = 8
KV_HEADS = 2
WINDOW = 128
ATTN_WIDTH = N_HEADS * HEAD_DIM
QK_WIDTH = ATTN_WIDTH + KV_HEADS * HEAD_DIM
KV_WIDTH = KV_HEADS * HEAD_DIM
ATTN_SCALE = HEAD_DIM ** -0.5
ROPE_THETA = 10000.0
SSM_WIDTH = 512
SSM_HEADS = 8
SSM_HEAD_DIM = 64
SSM_GROUPS = 2
SSM_STATE = 128
CONV_K = 4
CONV_HALO = 8
CONV_DIM = SSM_WIDTH + 2 * SSM_GROUPS * SSM_STATE
SSD_CHUNK = 128
N_EGROUPS = 4
EXP_PER_GROUP = 8
N_EXPERTS = 32
EXPERT_FF = 128
EPS = 1e-6
PAST_LEN = 16384

LANES = 128
BF16_ROWS = 16
HEAD_PAIRS = SSM_HEADS // 2
EXPERT_LANE0 = 32
VMEM_LIMIT = 56 * 1024 * 1024
MOE_TILE = 1024
MOE_SUBTILE = 128
ATTN_QBLOCKS = 8
SSD_CHUNKS_PER_STEP = 8
COUNT_BLOCK = 256
ROUTER_SUBTILE = 1024
INPROJ_SUBTILE = 512
SC_CORES = 2
SC_SUBCORES = 16
SC_WORKERS = SC_CORES * SC_SUBCORES
SC_CHUNK = 64

Q_END = ATTN_WIDTH
K_END = Q_END + KV_WIDTH
V_END = K_END + KV_WIDTH
Z_END = V_END + SSM_WIDTH
XBC_END = Z_END + CONV_DIM


def _dot(a, b):
    return jnp.dot(a, b, preferred_element_type=F32)


def _dot_nt(a, b):
    return lax.dot_general(a, b, (((1,), (1,)), ((), ())), preferred_element_type=F32)


def _split2(v):
    hi = v.astype(BF16)
    lo = (v - hi.astype(F32)).astype(BF16)
    return hi, lo


def _split3(v):
    hi = v.astype(BF16)
    r = v - hi.astype(F32)
    mid = r.astype(BF16)
    lo = (r - mid.astype(F32)).astype(BF16)
    return hi, mid, lo


def _silu(x):
    return x * jax.nn.sigmoid(x)


def _softplus(x):
    return jnp.maximum(x, 0.0) + jnp.log1p(jnp.exp(-jnp.abs(x)))


def _lane_bcast_pairs(v, n_pairs):
    r = v.shape[0]
    lo = lax.broadcasted_iota(jnp.int32, (r, LANES), 1) < HEAD_DIM
    slabs = []
    for j in range(n_pairs):
        a = jnp.broadcast_to(v[:, 2 * j:2 * j + 1], (r, LANES))
        b = jnp.broadcast_to(v[:, 2 * j + 1:2 * j + 2], (r, LANES))
        slabs.append(jnp.where(lo, a, b))
    return jnp.concatenate(slabs, axis=1)


def _causal_conv_silu(x_ext, convw_ref, convb_ref):
    halo = CONV_HALO
    x_raw = x_ext[halo:, :]
    conv = convb_ref[...] + x_raw * convw_ref[CONV_K - 1:CONV_K, :]
    for j in range(CONV_K - 1):
        shifted = pltpu.roll(x_ext, CONV_K - 1 - j, axis=0)[halo:, :]
        conv = conv + shifted * convw_ref[j:j + 1, :]
    return _silu(conv)


def _inproj_kernel(x_ref, n1_ref, win_ref, dtb_ref, qkn_ref,
                   cos_ref, sin_ref, red_ref, exp_ref,
                   q_ref, k_ref, v_ref, z_ref, xbc_ref, dt_ref, k2_ref, v2_ref):
    tm = x_ref.shape[0]
    sub = min(tm, INPROJ_SUBTILE)
    lane = lax.broadcasted_iota(jnp.int32, (sub, LANES), 1)
    first_half = (lane % HEAD_DIM) < (HEAD_DIM // 2)
    for s in range(tm // sub):
        rows = slice(s * sub, (s + 1) * sub)
        x = x_ref[rows, :]
        ms = jnp.mean(x * x, axis=-1, keepdims=True)
        xn = (x * lax.rsqrt(ms + EPS) * n1_ref[...]).astype(BF16)
        v = _dot(xn, win_ref[:, K_END:V_END])
        v_ref[rows, :] = v
        v2_ref[rows, :] = _pair_operands(v)
        z_ref[rows, :] = _dot(xn, win_ref[:, V_END:Z_END])
        xbc_ref[rows, :] = _dot(xn, win_ref[:, Z_END:XBC_END])
        dt_ref[rows, :] = _softplus(_dot(xn, win_ref[:, XBC_END:]) + dtb_ref[...])
        qk = _dot(xn, win_ref[:, :K_END])
        ss = _dot((qk * qk).astype(BF16), red_ref[...])
        inv = lax.rsqrt(ss * (1.0 / HEAD_DIM) + EPS)
        inv_hi, inv_lo = _split2(inv)
        inv_x = _dot(inv_hi, exp_ref[...]) + _dot(inv_lo, exp_ref[...])
        qkn = qk * inv_x * qkn_ref[...]
        cos = cos_ref[rows, :]
        sin = sin_ref[rows, :]
        for c in range(QK_WIDTH // LANES):
            xc = qkn[:, c * LANES:(c + 1) * LANES]
            partner = jnp.where(first_half,
                                pltpu.roll(xc, LANES - HEAD_DIM // 2, axis=1),
                                pltpu.roll(xc, HEAD_DIM // 2, axis=1))
            rot = xc * cos + partner * sin
            if c < ATTN_WIDTH // LANES:
                q_ref[rows, c * LANES:(c + 1) * LANES] = (rot * ATTN_SCALE).astype(BF16)
            else:
                k_ref[rows, :] = rot
                k2_ref[rows, :] = _pair_operands(rot)


def _inproj(x2d, w, cos_tab, sin_tab, tm, n_pos_blocks):
    t = x2d.shape[0]
    grid = (t // tm,)
    tok = lambda i: (i, 0)
    const = lambda i: (0, 0)
    pos = lambda i: (i % n_pos_blocks, 0)
    full = lambda a: pl.BlockSpec(a.shape, const)
    rows = lambda width, dtype: (jax.ShapeDtypeStruct((t, width), dtype), pl.BlockSpec((tm, width), tok))
    outs = [rows(ATTN_WIDTH, BF16), rows(KV_WIDTH, F32), rows(KV_WIDTH, F32), rows(SSM_WIDTH, F32)]
    operands = [x2d, w["norm1"], w["w_in"], w["dt_bias"], w["qkn"],
                cos_tab, sin_tab, w["red"], w["exp"]]
    in_specs = [
        pl.BlockSpec((tm, D_MODEL), tok),
        full(w["norm1"]), full(w["w_in"]), full(w["dt_bias"]), full(w["qkn"]),
        pl.BlockSpec((tm, LANES), pos), pl.BlockSpec((tm, LANES), pos),
        full(w["red"]), full(w["exp"]),
    ]
    outs += [rows(CONV_DIM, F32), rows(LANES, F32), rows(4 * LANES, BF16), rows(4 * LANES, BF16)]
    return pl.pallas_call(
        _inproj_kernel,
        out_shape=tuple(o[0] for o in outs),
        grid=grid,
        in_specs=in_specs,
        out_specs=tuple(o[1] for o in outs),
        compiler_params=pltpu.CompilerParams(dimension_semantics=("parallel",),
                                             vmem_limit_bytes=VMEM_LIMIT),
        name="inproj",
    )(*operands)


def _pair_operands(kv):
    lo = lax.broadcasted_iota(jnp.int32, kv.shape, 1) < HEAD_DIM
    swapped = pltpu.roll(kv, HEAD_DIM, axis=1)
    parts = [jnp.where(lo, kv, 0.0), jnp.where(lo, 0.0, swapped), jnp.where(lo, swapped, 0.0), jnp.where(lo, 0.0, kv)]
    return jnp.concatenate(parts, axis=1).astype(BF16)


def _pair_rhs(blk, g):
    return jnp.concatenate([blk[:, 2 * g * LANES:(2 * g + 1) * LANES],
                            blk[:, (2 * g + 1) * LANES:(2 * g + 2) * LANES]], axis=0)


def _attn_qblock(sink_ref, q_blk, k_prev, k_cur, v_prev, v_cur, seq_start, o_ref):
    blk = WINDOW
    qi = lax.broadcasted_iota(jnp.int32, (blk, 2 * blk), 0)
    kj = lax.broadcasted_iota(jnp.int32, (blk, 2 * blk), 1) % blk
    cur_ok = kj <= qi
    lo = lax.broadcasted_iota(jnp.int32, (blk, LANES), 1) < HEAD_DIM
    n_pairs = N_HEADS // KV_HEADS // 2
    for g in range(KV_HEADS):
        q_all = jnp.concatenate([q_blk[:, (g * n_pairs + r) * LANES:(g * n_pairs + r + 1) * LANES]
                                 for r in range(n_pairs)], axis=0)
        s_all = _dot_nt(q_all, jnp.concatenate([_pair_rhs(k_cur, g), _pair_rhs(k_prev, g)], axis=0))
        p_rows = []
        den_rows = []
        for r in range(n_pairs):
            pair = g * n_pairs + r
            s_cur = s_all[r * blk:(r + 1) * blk, :2 * blk]
            s_prev = s_all[r * blk:(r + 1) * blk, 2 * blk:]
            s = jnp.where(cur_ok, s_cur, s_prev)
            if seq_start is not None:
                s = jnp.where(jnp.logical_or(cur_ok, jnp.logical_not(seq_start)), s, -jnp.inf)
            ps = []
            dens = []
            for hh in range(2):
                sink = sink_ref[2 * pair + hh]
                sh = s[:, hh * blk:(hh + 1) * blk]
                m = jnp.maximum(jnp.max(sh, axis=-1, keepdims=True), sink)
                p = jnp.exp(sh - m)
                dens.append(jnp.sum(p, axis=-1, keepdims=True) + jnp.exp(sink - m))
                ps.append(p)
            p2 = jnp.concatenate(ps, axis=1)
            p_rows.append(jnp.concatenate([jnp.where(cur_ok, p2, 0.0), jnp.where(cur_ok, 0.0, p2)],
                                          axis=1).astype(BF16))
            den_rows.append(jnp.where(lo, dens[0], dens[1]))
        o_all = _dot(jnp.concatenate(p_rows, axis=0),
                     jnp.concatenate([_pair_rhs(v_cur, g), _pair_rhs(v_prev, g)], axis=0))
        for r in range(n_pairs):
            pair = g * n_pairs + r
            o2 = o_all[r * blk:(r + 1) * blk, :]
            o_ref[:, pair * LANES:(pair + 1) * LANES] = (o2 / den_rows[r]).astype(BF16)


def _attn_kernel(sink_ref, q_ref, kc_ref, kp_ref, vc_ref, vp_ref, o_ref):
    blk = WINDOW
    first_step = pl.program_id(1) == 0
    for u in range(q_ref.shape[0] // blk):
        rows = slice(u * blk, (u + 1) * blk)
        prev_rows = slice((u - 1) * blk, u * blk)
        k_prev = kp_ref[...] if u == 0 else kc_ref[prev_rows, :]
        v_prev = vp_ref[...] if u == 0 else vc_ref[prev_rows, :]
        _attn_qblock(sink_ref, q_ref[rows, :], k_prev, kc_ref[rows, :], v_prev, vc_ref[rows, :],
                     first_step if u == 0 else None, o_ref.at[rows, :])


def _attn_prompt(q, k, v, sinks, batch, seq):
    n_sub = ATTN_QBLOCKS if seq % (ATTN_QBLOCKS * WINDOW) == 0 else 1
    rows = n_sub * WINDOW
    nb = seq // rows
    cur = lambda b, j, s: (b * nb + j, 0)
    prev = lambda b, j, s: (jnp.maximum((b * nb + j) * n_sub - 1, 0), 0)
    return pl.pallas_call(
        _attn_kernel,
        out_shape=jax.ShapeDtypeStruct((batch * seq, ATTN_WIDTH), BF16),
        grid_spec=pltpu.PrefetchScalarGridSpec(
            num_scalar_prefetch=1,
            grid=(batch, nb),
            in_specs=[
                pl.BlockSpec((rows, ATTN_WIDTH), cur),
                pl.BlockSpec((rows, 4 * LANES), cur), pl.BlockSpec((WINDOW, 4 * LANES), prev),
                pl.BlockSpec((rows, 4 * LANES), cur), pl.BlockSpec((WINDOW, 4 * LANES), prev),
            ],
            out_specs=pl.BlockSpec((rows, ATTN_WIDTH), cur),
        ),
        compiler_params=pltpu.CompilerParams(dimension_semantics=("parallel", "parallel"),
                                             vmem_limit_bytes=VMEM_LIMIT),
        name="attn_prompt",
    )(sinks, q, k, k, v, v)


def _ssd_kernel(xbc_ref, z_ref, dt_ref, convw_ref, convb_ref, alog_ref, dskip_ref, nw_ref,
                tri_ref, expand_ref, y_ref, st_ref, buf_ref, state_ref):
    c = pl.program_id(1)
    cl = SSD_CHUNK
    n_sub = xbc_ref.shape[0] // cl
    halo = CONV_HALO

    @pl.when(c == 0)
    def _():
        buf_ref[...] = jnp.zeros(buf_ref.shape, F32)
        state_ref[...] = jnp.zeros(state_ref.shape, F32)

    lane = lax.broadcasted_iota(jnp.int32, (1, LANES), 1)
    a_neg = jnp.where(lane < SSM_HEADS, -jnp.exp(alog_ref[...]), 0.0)
    tri = tri_ref[...]
    for u in range(n_sub):
        rows = slice(u * cl, (u + 1) * cl)
        if u == 0:
            x_ext = jnp.concatenate([buf_ref[...], xbc_ref[rows, :]], axis=0)
        else:
            x_ext = xbc_ref[u * cl - halo:(u + 1) * cl, :]
        _ssd_chunk(x_ext, z_ref[rows, :], dt_ref[rows, :], a_neg, tri, convw_ref, convb_ref, dskip_ref, nw_ref,
                   expand_ref, y_ref.at[rows, :], state_ref)
    buf_ref[...] = xbc_ref[n_sub * cl - halo:n_sub * cl, :]

    @pl.when(c == pl.num_programs(1) - 1)
    def _():
        st_ref[0] = state_ref[...]


def _ssd_chunk(x_ext, z, dt, a_neg, tri, convw_ref, convb_ref, dskip_ref, nw_ref, expand_ref, y_ref, state_ref):
    cl = SSD_CHUNK
    act = _causal_conv_silu(x_ext, convw_ref, convb_ref)
    xs = act[:, :SSM_WIDTH]
    bm = act[:, SSM_WIDTH:SSM_WIDTH + SSM_GROUPS * SSM_STATE].astype(BF16)
    cm = act[:, SSM_WIDTH + SSM_GROUPS * SSM_STATE:].astype(BF16)

    dta = dt * a_neg
    p_hi, p_mid, p_lo = _split3(dta)
    a_col = _dot(tri, p_hi) + _dot(tri, p_mid) + _dot(tri, p_lo)
    a_last = a_col[cl - 1:cl, :]
    a_row = a_col.T
    per_head = jnp.concatenate([dt, jnp.exp(a_col), jnp.exp(a_last - a_col)], axis=0)
    ph_hi, ph_lo = _split2(per_head)
    per_lane = _dot(ph_hi, expand_ref[...]) + _dot(ph_lo, expand_ref[...])
    dt_x = per_lane[:cl]
    ecol_x = per_lane[cl:2 * cl]
    dte_x = per_lane[2 * cl:]
    e_last = jnp.exp(a_last)
    xdt = xs * dt_x

    li = lax.broadcasted_iota(jnp.int32, (cl, cl), 0)
    si = lax.broadcasted_iota(jnp.int32, (cl, cl), 1)
    causal = si <= li
    lo = lax.broadcasted_iota(jnp.int32, (cl, LANES), 1) < SSM_HEAD_DIM
    row_lo = lax.broadcasted_iota(jnp.int32, (LANES, SSM_STATE), 0) < SSM_HEAD_DIM

    ys = []
    for g in range(SSM_GROUPS):
        b_g = bm[:, g * SSM_STATE:(g + 1) * SSM_STATE]
        c_g = cm[:, g * SSM_STATE:(g + 1) * SSM_STATE]
        cb = _dot_nt(c_g, b_g)
        for r in range(HEAD_PAIRS // SSM_GROUPS):
            j = g * (HEAD_PAIRS // SSM_GROUPS) + r
            sl = slice(j * LANES, (j + 1) * LANES)
            xdt_p = xdt[:, sl]
            ms = []
            for hh in range(2):
                h = 2 * j + hh
                seg = a_col[:, h:h + 1] - a_row[h:h + 1, :]
                ms.append(cb * jnp.exp(jnp.where(causal, seg, -jnp.inf)))
            m2 = jnp.concatenate(ms, axis=1).astype(BF16)
            rhs = jnp.concatenate([jnp.where(lo, xdt_p, 0.0), jnp.where(lo, 0.0, xdt_p)],
                                  axis=0).astype(BF16)
            y_diag = _dot(m2, rhs)
            st = state_ref[j]
            y_off = _dot_nt(c_g, st.astype(BF16)) * ecol_x[:, sl]
            xdt_e = (xdt_p * dte_x[:, sl]).T.astype(BF16)
            d_a = e_last[:, 2 * j:2 * j + 1]
            d_b = e_last[:, 2 * j + 1:2 * j + 2]
            decay = jnp.where(row_lo, jnp.broadcast_to(d_a, row_lo.shape), jnp.broadcast_to(d_b, row_lo.shape))
            state_ref[j] = decay * st + _dot(xdt_e, b_g)
            ys.append(y_diag + y_off + dskip_ref[:, sl] * xs[:, sl])
    y = jnp.concatenate(ys, axis=1)
    gated = y * _silu(z)
    gw = SSM_WIDTH // SSM_GROUPS
    outs = []
    for g in range(SSM_GROUPS):
        gg = gated[:, g * gw:(g + 1) * gw]
        outs.append(gg * lax.rsqrt(jnp.mean(gg * gg, axis=-1, keepdims=True) + EPS))
    y_ref[...] = (jnp.concatenate(outs, axis=1) * nw_ref[...]).astype(BF16)


def _ssd_prompt(xbc, z, dt, w, batch, seq):
    n_sub = SSD_CHUNKS_PER_STEP if seq % (SSD_CHUNKS_PER_STEP * SSD_CHUNK) == 0 else 1
    rows = n_sub * SSD_CHUNK
    nc = seq // rows
    tok = lambda b, c: (b * nc + c, 0)
    const = lambda b, c: (0, 0)
    full = lambda a: pl.BlockSpec(a.shape, const)
    return pl.pallas_call(
        _ssd_kernel,
        out_shape=(jax.ShapeDtypeStruct((batch * seq, SSM_WIDTH), BF16),
                   jax.ShapeDtypeStruct((batch, HEAD_PAIRS, LANES, SSM_STATE), F32)),
        grid=(batch, nc),
        in_specs=[
            pl.BlockSpec((rows, CONV_DIM), tok), pl.BlockSpec((rows, SSM_WIDTH), tok),
            pl.BlockSpec((rows, LANES), tok),
            full(w["conv_w"]), full(w["conv_b"]), full(w["a_log"]),
            full(w["d_skip"]), full(w["ssm_norm"]), full(w["tri"]), full(w["expand"]),
        ],
        out_specs=(pl.BlockSpec((rows, SSM_WIDTH), tok),
                   pl.BlockSpec((1, HEAD_PAIRS, LANES, SSM_STATE), lambda b, c: (b, 0, 0, 0))),
        scratch_shapes=[pltpu.VMEM((CONV_HALO, CONV_DIM), F32),
                        pltpu.VMEM((HEAD_PAIRS, LANES, SSM_STATE), F32)],
        compiler_params=pltpu.CompilerParams(dimension_semantics=("parallel", "arbitrary"),
                                             vmem_limit_bytes=VMEM_LIMIT),
        name="ssd_prompt",
    )(xbc, z, dt, w["conv_w"], w["conv_b"], w["a_log"], w["d_skip"], w["ssm_norm"], w["tri"], w["expand"])


def _attn_sample_kernel(qx_ref, kc_ref, kn_ref, vc_ref, vn_ref, sink_ref, ko_ref, vo_ref, o_ref):
    bs = qx_ref.shape[0]
    w = kc_ref.shape[2]
    sink = sink_ref[...]
    lo = lax.broadcasted_iota(jnp.int32, (1, LANES), 1) < HEAD_DIM
    newest = lax.broadcasted_iota(jnp.int32, (1, w), 1) == w - 1
    for i in range(bs):
        k_t = jnp.where(newest, kn_ref[i], pltpu.roll(kc_ref[i], w - 1, axis=1))
        v_t = jnp.where(newest, vn_ref[i], pltpu.roll(vc_ref[i], w - 1, axis=1))
        ko_ref[i] = k_t
        vo_ref[i] = v_t
        s = _dot(qx_ref[i], k_t.astype(BF16))
        m = jnp.maximum(jnp.max(s, axis=-1, keepdims=True), sink)
        p = jnp.exp(s - m)
        den = jnp.sum(p, axis=-1, keepdims=True) + jnp.exp(sink - m)
        o = _dot_nt(p.astype(BF16), v_t.astype(BF16)) / den
        o_sw = pltpu.roll(o, HEAD_DIM, axis=1)
        for j in range(N_HEADS // 2):
            a, b = (o, o_sw) if j < N_HEADS // 4 else (o_sw, o)
            o_ref[i:i + 1, j * LANES:(j + 1) * LANES] = jnp.where(lo, a[2 * j:2 * j + 1], b[2 * j + 1:2 * j + 2])


def _attn_sample(qx, kc, kn, vc, vn, sink_x, bs):
    n, w = kc.shape[0], kc.shape[2]
    blk3 = lambda i: (i, 0, 0)
    return pl.pallas_call(
        _attn_sample_kernel,
        out_shape=(jax.ShapeDtypeStruct((n, KV_WIDTH, w), F32),
                   jax.ShapeDtypeStruct((n, KV_WIDTH, w), F32),
                   jax.ShapeDtypeStruct((n, ATTN_WIDTH), F32)),
        grid=(n // bs,),
        in_specs=[
            pl.BlockSpec((bs, BF16_ROWS, LANES), blk3),
            pl.BlockSpec((bs, KV_WIDTH, w), blk3), pl.BlockSpec((bs, KV_WIDTH, 1), blk3),
            pl.BlockSpec((bs, KV_WIDTH, w), blk3), pl.BlockSpec((bs, KV_WIDTH, 1), blk3),
            pl.BlockSpec(sink_x.shape, lambda i: (0, 0)),
        ],
        out_specs=(pl.BlockSpec((bs, KV_WIDTH, w), blk3), pl.BlockSpec((bs, KV_WIDTH, w), blk3),
                   pl.BlockSpec((bs, ATTN_WIDTH), lambda i: (i, 0))),
        compiler_params=pltpu.CompilerParams(dimension_semantics=("parallel",),
                                             vmem_limit_bytes=VMEM_LIMIT),
        name="attn_sample",
    )(qx, kc, kn, vc, vn, sink_x)


def _ssd_sample_kernel(xbc_ref, z_ref, dt_ref, cprev_ref, h0_ref, convw_ref, convb_ref, alog_ref,
                       dskip_ref, nw_ref, y_ref, cnew_ref, h1_ref):
    bs = xbc_ref.shape[0]
    x_raw = xbc_ref[...]
    conv = convb_ref[...] + x_raw * convw_ref[CONV_K - 1:CONV_K, :]
    for j in range(CONV_K - 1):
        conv = conv + cprev_ref[j] * convw_ref[j:j + 1, :]
    for j in range(CONV_K - 2):
        cnew_ref[j] = cprev_ref[j + 1]
    cnew_ref[CONV_K - 2] = x_raw
    act = _silu(conv)
    xs = act[:, :SSM_WIDTH]
    bm = act[:, SSM_WIDTH:SSM_WIDTH + SSM_GROUPS * SSM_STATE].astype(BF16)
    cm = act[:, SSM_WIDTH + SSM_GROUPS * SSM_STATE:].astype(BF16)
    lane = lax.broadcasted_iota(jnp.int32, (1, LANES), 1)
    a_neg = jnp.where(lane < SSM_HEADS, -jnp.exp(alog_ref[...]), 0.0)
    dt = dt_ref[...]
    dec = jnp.exp(dt * a_neg)
    xdt = xs * _lane_bcast_pairs(dt, HEAD_PAIRS)
    rowid = lax.broadcasted_iota(jnp.int32, (bs, LANES), 0)
    row_lo = lax.broadcasted_iota(jnp.int32, (LANES, SSM_STATE), 0) < SSM_HEAD_DIM
    ys = []
    for j in range(HEAD_PAIRS):
        g = j // (HEAD_PAIRS // SSM_GROUPS)
        sl = slice(j * LANES, (j + 1) * LANES)
        b_g = bm[:, g * SSM_STATE:(g + 1) * SSM_STATE]
        c_g = cm[:, g * SSM_STATE:(g + 1) * SSM_STATE]
        xdt_p = xdt[:, sl]
        y_p = jnp.zeros((bs, LANES), F32)
        for i in range(bs):
            xi = jnp.where(rowid == i, xdt_p, 0.0).T.astype(BF16)
            d_a = dec[i:i + 1, 2 * j:2 * j + 1]
            d_b = dec[i:i + 1, 2 * j + 1:2 * j + 2]
            decay = jnp.where(row_lo, jnp.broadcast_to(d_a, row_lo.shape), jnp.broadcast_to(d_b, row_lo.shape))
            new = decay * h0_ref[i, j] + _dot(xi, b_g)
            h1_ref[i, j] = new
            y_p = y_p + jnp.where(rowid == i, _dot_nt(c_g, new.astype(BF16)), 0.0)
        ys.append(y_p + dskip_ref[:, sl] * xs[:, sl])
    y = jnp.concatenate(ys, axis=1)
    gated = y * _silu(z_ref[...])
    gw = SSM_WIDTH // SSM_GROUPS
    outs = []
    for g in range(SSM_GROUPS):
        gg = gated[:, g * gw:(g + 1) * gw]
        outs.append(gg * lax.rsqrt(jnp.mean(gg * gg, axis=-1, keepdims=True) + EPS))
    y_ref[...] = (jnp.concatenate(outs, axis=1) * nw_ref[...]).astype(BF16)


def _ssd_sample(xbc, z, dt, cprev_t, h0, w, bs):
    n = xbc.shape[0]
    tok = lambda i: (i, 0)
    const = lambda i: (0, 0)
    full = lambda a: pl.BlockSpec(a.shape, const)
    return pl.pallas_call(
        _ssd_sample_kernel,
        out_shape=(jax.ShapeDtypeStruct((n, SSM_WIDTH), BF16),
                   jax.ShapeDtypeStruct((CONV_K - 1, n, CONV_DIM), F32),
                   jax.ShapeDtypeStruct((n, HEAD_PAIRS, LANES, SSM_STATE), F32)),
        grid=(n // bs,),
        in_specs=[
            pl.BlockSpec((bs, CONV_DIM), tok), pl.BlockSpec((bs, SSM_WIDTH), tok),
            pl.BlockSpec((bs, LANES), tok),
            pl.BlockSpec((CONV_K - 1, bs, CONV_DIM), lambda i: (0, i, 0)),
            pl.BlockSpec((bs, HEAD_PAIRS, LANES, SSM_STATE), lambda i: (i, 0, 0, 0)),
            full(w["conv_w"]), full(w["conv_b"]), full(w["a_log"]),
            full(w["d_skip"]), full(w["ssm_norm"]),
        ],
        out_specs=(pl.BlockSpec((bs, SSM_WIDTH), tok),
                   pl.BlockSpec((CONV_K - 1, bs, CONV_DIM), lambda i: (0, i, 0)),
                   pl.BlockSpec((bs, HEAD_PAIRS, LANES, SSM_STATE), lambda i: (i, 0, 0, 0))),
        compiler_params=pltpu.CompilerParams(dimension_semantics=("parallel",),
                                             vmem_limit_bytes=VMEM_LIMIT),
        name="ssd_sample",
    )(xbc, z, dt, cprev_t, h0, w["conv_w"], w["conv_b"], w["a_log"], w["d_skip"], w["ssm_norm"])


def _pack_bf16_pair(v):
    c = v.shape[1] // 2
    hi = lax.bitcast_convert_type(v[:, :c].astype(BF16).astype(F32), jnp.uint32)
    lo = lax.bitcast_convert_type(v[:, c:].astype(BF16).astype(F32), jnp.uint32)
    return hi | (lo >> 16)


def _unpack_bf16_pair(word):
    a = lax.bitcast_convert_type(word & jnp.uint32(0xFFFF0000), F32)
    b = lax.bitcast_convert_type(word << 16, F32)
    return a, b


def _outproj_router_kernel(x_ref, a_ref, s_ref, wo_ref, n2_ref, wr_ref, br_ref, tri_ref,
                           h_ref, t_ref, route_ref, routet_ref, cnt_ref, carry_ref):
    @pl.when(pl.program_id(0) == 0)
    def _():
        carry_ref[...] = jnp.zeros(carry_ref.shape, F32)

    tm = x_ref.shape[0]
    h_ref[...] = (x_ref[...] + _dot(a_ref[...].astype(BF16), wo_ref[:ATTN_WIDTH, :])
                  + _dot(s_ref[...].astype(BF16), wo_ref[ATTN_WIDTH:, :]))
    sub = min(tm, ROUTER_SUBTILE)
    carry = carry_ref[:, 0:1]
    for s in range(tm // sub):
        rows = slice(s * sub, (s + 1) * sub)
        carry = _route_rows(h_ref[rows, :], n2_ref, wr_ref, br_ref, tri_ref, carry, t_ref.at[rows, :],
                            route_ref.at[rows, :], routet_ref.at[:, rows])
    carry_ref[...] = jnp.broadcast_to(carry, carry_ref.shape)
    cpad = jnp.concatenate([jnp.zeros((EXPERT_LANE0, LANES), F32), jnp.broadcast_to(carry, (N_EXPERTS, LANES)),
                            jnp.zeros((LANES - EXPERT_LANE0 - N_EXPERTS, LANES), F32)], axis=0)
    cnt_ref[...] = cpad.T[0:cnt_ref.shape[0], :]


def _route_rows(h, n2_ref, wrt_ref, brc_ref, triu_ref, carry, t_ref, route_ref, routet_ref):
    ms = jnp.mean(h * h, axis=-1, keepdims=True)
    t = h * lax.rsqrt(ms + EPS) * n2_ref[...]
    t_ref[...] = _pack_bf16_pair(t)
    n = h.shape[0]
    a = _dot_nt(wrt_ref[...], t.astype(BF16))

    def logit_rows(r0, r1):
        return a[r0:r1] + a[LANES + r0:LANES + r1] + brc_ref[r0:r1, :]

    sl = EXP_PER_GROUP
    glog = logit_rows(0, sl)
    elog = logit_rows(EXPERT_LANE0, EXPERT_LANE0 + N_EXPERTS)
    row = lax.broadcasted_iota(jnp.int32, (sl, n), 0).astype(F32)
    big = float(sl)
    ninf = -jnp.inf
    gm = jnp.where(row < N_EGROUPS, glog, ninf)
    gmax = jnp.max(gm, axis=0, keepdims=True)
    g_top = 1.0 / jnp.sum(jnp.exp(gm - gmax), axis=0, keepdims=True)
    g_idx = jnp.min(jnp.where(gm == gmax, row, big), axis=0, keepdims=True)
    ml = elog[(N_EGROUPS - 1) * sl:]
    for g in range(N_EGROUPS - 2, -1, -1):
        ml = jnp.where(g_idx == float(g), elog[g * sl:(g + 1) * sl], ml)
    m1 = jnp.max(ml, axis=0, keepdims=True)
    i1 = jnp.min(jnp.where(ml == m1, row, big), axis=0, keepdims=True)
    ml2 = jnp.where(row == i1, ninf, ml)
    m2 = jnp.max(ml2, axis=0, keepdims=True)
    i2 = jnp.min(jnp.where(ml2 == m2, row, big), axis=0, keepdims=True)
    r = jnp.exp(m2 - m1)
    w1 = g_top / (1.0 + r)
    w2 = g_top * r / (1.0 + r)
    e1 = g_idx * float(sl) + i1
    e2 = g_idx * float(sl) + i2
    pick = jnp.where(jnp.logical_or(row == i1, row == i2), 1.0, 0.0)
    onehot = jnp.concatenate([jnp.where(g_idx == float(g), pick, 0.0) for g in range(N_EGROUPS)], axis=0)
    onehot_bf = onehot.astype(BF16)
    cb = triu_ref.shape[0]
    cums = []
    for blk in range(n // cb):
        c = _dot(onehot_bf[:, blk * cb:(blk + 1) * cb], triu_ref[...]) + carry
        carry = c[:, cb - 1:cb]
        cums.append(c)
    before = jnp.concatenate(cums, axis=1) - onehot
    erow = lax.broadcasted_iota(jnp.int32, (N_EXPERTS, n), 0).astype(F32)
    rank1 = jnp.sum(jnp.where(erow == e1, before, 0.0), axis=0, keepdims=True)
    rank2 = jnp.sum(jnp.where(erow == e2, before, 0.0), axis=0, keepdims=True)
    fields = jnp.concatenate([e1, e2, w1, w2, rank1, rank2], axis=0)
    routet_ref[0:fields.shape[0], :] = fields
    routet_ref[fields.shape[0]:, :] = jnp.zeros((routet_ref.shape[0] - fields.shape[0], n), F32)
    if n % LANES:
        fields = jnp.concatenate([fields, jnp.zeros((fields.shape[0], LANES - n % LANES), F32)], axis=1)
    pad = jnp.zeros((LANES - fields.shape[0], LANES), F32)
    for j in range(fields.shape[1] // LANES):
        blk_rows = jnp.concatenate([fields[:, j * LANES:(j + 1) * LANES], pad], axis=0)
        n_valid = min(LANES, n - j * LANES)
        route_ref[j * LANES:j * LANES + n_valid, :] = blk_rows.T[:n_valid, :]
    return carry


def _outproj_router(x2d, attn, ssm, w, tm):
    t = x2d.shape[0]
    tok = lambda i: (i, 0)
    const = lambda i: (0, 0)
    full = lambda a: pl.BlockSpec(a.shape, const)
    idx = jnp.arange(min(tm, COUNT_BLOCK))
    tri = (idx[:, None] <= idx[None, :]).astype(BF16)
    return pl.pallas_call(
        _outproj_router_kernel,
        out_shape=(jax.ShapeDtypeStruct((t, D_MODEL), F32), jax.ShapeDtypeStruct((t, D_MODEL // 2), jnp.uint32),
                   jax.ShapeDtypeStruct((t, LANES), F32), jax.ShapeDtypeStruct((8, t), F32),
                   jax.ShapeDtypeStruct((8, LANES), F32)),
        grid=(t // tm,),
        in_specs=[
            pl.BlockSpec((tm, D_MODEL), tok), pl.BlockSpec((tm, ATTN_WIDTH), tok),
            pl.BlockSpec((tm, SSM_WIDTH), tok),
            full(w["w_out"]), full(w["norm2"]), full(w["wrt"]),
            full(w["br_col"]), full(tri),
        ],
        out_specs=(pl.BlockSpec((tm, D_MODEL), tok), pl.BlockSpec((tm, D_MODEL // 2), tok),
                   pl.BlockSpec((tm, LANES), tok), pl.BlockSpec((8, tm), lambda i: (0, i)),
                   pl.BlockSpec((8, LANES), const)),
        scratch_shapes=[pltpu.VMEM((N_EXPERTS, LANES), F32)],
        compiler_params=pltpu.CompilerParams(dimension_semantics=("arbitrary",),
                                             vmem_limit_bytes=VMEM_LIMIT),
        name="outproj_router",
    )(x2d, attn, ssm, w["w_out"], w["norm2"], w["wrt"], w["br_col"], tri)


def _expert_hidden(t_a, t_b, w1):
    half = D_MODEL // 2
    gu = _dot(t_a, w1[:half]) + _dot(t_b, w1[half:])
    return _silu(gu[:, :EXPERT_FF]) * gu[:, EXPERT_FF:]


def _gate_up_bf16(wg_ref, wu_ref):
    return jnp.concatenate([wg_ref[0].astype(BF16), wu_ref[0].astype(BF16)], axis=1)


def _moe_dense_kernel(h_ref, t_ref, route_ref, wg_ref, wu_ref, wd_ref, y_ref):
    e = pl.program_id(0)

    @pl.when(e == 0)
    def _():
        y_ref[...] = h_ref[...]

    t_a, t_b = _unpack_bf16_pair(t_ref[...])
    route = route_ref[...]
    e1, e2, g1, g2 = route[:, 0:1], route[:, 1:2], route[:, 2:3], route[:, 3:4]
    e_f = e.astype(F32)
    hid = _expert_hidden(t_a.astype(BF16), t_b.astype(BF16), _gate_up_bf16(wg_ref, wu_ref))
    c_e = jnp.where(e1 == e_f, g1, 0.0) + jnp.where(e2 == e_f, g2, 0.0)
    y_ref[...] += _dot((hid * c_e).astype(BF16), wd_ref[0].astype(BF16))


def _moe_dense(h, t, route, w):
    n = h.shape[0]
    whole = lambda e: (0, 0)
    by_expert = lambda e: (e, 0, 0)
    return pl.pallas_call(
        _moe_dense_kernel,
        out_shape=jax.ShapeDtypeStruct((n, D_MODEL), F32),
        grid=(N_EXPERTS,),
        in_specs=[pl.BlockSpec((n, D_MODEL), whole), pl.BlockSpec((n, D_MODEL // 2), whole),
                  pl.BlockSpec((n, LANES), whole),
                  pl.BlockSpec((1, D_MODEL, EXPERT_FF), by_expert), pl.BlockSpec((1, D_MODEL, EXPERT_FF), by_expert),
                  pl.BlockSpec((1, EXPERT_FF, D_MODEL), by_expert)],
        out_specs=pl.BlockSpec((n, D_MODEL), whole),
        compiler_params=pltpu.CompilerParams(dimension_semantics=("arbitrary",),
                                             vmem_limit_bytes=VMEM_LIMIT),
        name="moe_dense",
    )(h, t, route, w["w_gate"], w["w_up"], w["w_down"])


def _sc_scatter_rows(src, pos1, pos2, n_out):
    t, width = src.shape
    rows_per_worker = t // SC_WORKERS
    n_chunks = rows_per_worker // SC_CHUNK
    assert t == SC_WORKERS * SC_CHUNK * n_chunks and n_chunks % 2 == 0
    mesh = plsc.VectorSubcoreMesh(core_axis_name="c", subcore_axis_name="s")

    @functools.partial(
        pl.kernel, mesh=mesh,
        out_type=jax.ShapeDtypeStruct((n_out, width), src.dtype),
        scratch_types=[pltpu.VMEM((2, SC_CHUNK), jnp.int32), pltpu.VMEM((2, SC_CHUNK), jnp.int32),
                       pltpu.VMEM((2, SC_CHUNK, width), src.dtype),
                       pltpu.SemaphoreType.DMA, pltpu.SemaphoreType.DMA,
                       pltpu.SemaphoreType.DMA, pltpu.SemaphoreType.DMA],
    )
    def scatter_kernel(src_hbm, p1_hbm, p2_hbm, out_hbm, i1_v, i2_v, rows_v, l0, l1, s0, s1):
        wid = lax.axis_index("s") * SC_CORES + lax.axis_index("c")
        base = wid * rows_per_worker
        lsem = (l0, l1)
        ssem = (s0, s1)

        def load_copy(c, slot):
            off = pl.multiple_of(base + c * SC_CHUNK, 8)
            return pltpu.make_async_copy(src_hbm.at[pl.ds(off, SC_CHUNK)], rows_v.at[slot], lsem[slot])

        def start_load(c, slot):
            off = pl.multiple_of(base + c * SC_CHUNK, 8)
            pltpu.sync_copy(p1_hbm.at[pl.ds(off, SC_CHUNK)], i1_v.at[slot])
            pltpu.sync_copy(p2_hbm.at[pl.ds(off, SC_CHUNK)], i2_v.at[slot])
            load_copy(c, slot).start()

        def scatter_copies(slot):
            return (pltpu.make_async_copy(rows_v.at[slot], out_hbm.at[i1_v.at[slot]], ssem[slot]),
                    pltpu.make_async_copy(rows_v.at[slot], out_hbm.at[i2_v.at[slot]], ssem[slot]))

        def start_scatter(slot):
            for cp in scatter_copies(slot):
                cp.start()

        def wait_scatter(slot):
            for cp in scatter_copies(slot):
                cp.wait()

        start_load(0, 0)

        @pl.loop(0, n_chunks, step=2)
        def _(c):
            @pl.when(c > 0)
            def _():
                wait_scatter(1)

            start_load(c + 1, 1)
            load_copy(c, 0).wait()
            start_scatter(0)
            load_copy(c + 1, 1).wait()
            wait_scatter(0)
            start_scatter(1)

            @pl.when(c + 2 < n_chunks)
            def _():
                start_load(c + 2, 0)

        wait_scatter(1)

    return scatter_kernel(src, pos1, pos2)


def _sc_gather_rows(table, idx):
    n, width = idx.shape[0], table.shape[1]
    rows_per_worker = n // SC_WORKERS
    n_chunks = rows_per_worker // SC_CHUNK
    assert n == SC_WORKERS * SC_CHUNK * n_chunks and n_chunks % 2 == 0
    mesh = plsc.VectorSubcoreMesh(core_axis_name="c", subcore_axis_name="s")

    @functools.partial(
        pl.kernel, mesh=mesh,
        out_type=jax.ShapeDtypeStruct((n, width), table.dtype),
        scratch_types=[pltpu.VMEM((2, SC_CHUNK), jnp.int32), pltpu.VMEM((2, SC_CHUNK, width), table.dtype),
                       pltpu.SemaphoreType.DMA, pltpu.SemaphoreType.DMA,
                       pltpu.SemaphoreType.DMA, pltpu.SemaphoreType.DMA],
    )
    def gather_kernel(table_hbm, idx_hbm, out_hbm, idx_v, rows_v, g0, g1, w0, w1):
        wid = lax.axis_index("s") * SC_CORES + lax.axis_index("c")
        base = wid * rows_per_worker
        gsem = (g0, g1)
        wsem = (w0, w1)

        def gather_copy(slot):
            return pltpu.make_async_copy(table_hbm.at[idx_v.at[slot]], rows_v.at[slot], gsem[slot])

        def write_copy(c, slot):
            off = pl.multiple_of(base + c * SC_CHUNK, 8)
            return pltpu.make_async_copy(rows_v.at[slot], out_hbm.at[pl.ds(off, SC_CHUNK)], wsem[slot])

        def start_gather(c, slot):
            off = pl.multiple_of(base + c * SC_CHUNK, 8)
            pltpu.sync_copy(idx_hbm.at[pl.ds(off, SC_CHUNK)], idx_v.at[slot])
            gather_copy(slot).start()

        start_gather(0, 0)

        @pl.loop(0, n_chunks, step=2)
        def _(c):
            @pl.when(c > 0)
            def _():
                write_copy(c - 1, 1).wait()

            start_gather(c + 1, 1)
            gather_copy(0).wait()
            write_copy(c, 0).start()
            gather_copy(1).wait()
            write_copy(c + 1, 1).start()
            write_copy(c, 0).wait()

            @pl.when(c + 2 < n_chunks)
            def _():
                start_gather(c + 2, 0)

        write_copy(n_chunks - 1, 1).wait()

    return gather_kernel(table, idx)


def _moe_grouped_kernel(te_ref, nt_ref, order_ref, x_lo_ref, x_hi_ref, wg_ref, wu_ref, wd_ref, o_ref,
                        w1_bf_ref, w2_bf_ref):
    del order_ref
    i = pl.program_id(0)
    half = MOE_TILE // 2

    @pl.when(jnp.logical_or(i == 0, te_ref[i] != te_ref[jnp.maximum(i - 1, 0)]))
    def _():
        w1_bf_ref[...] = _gate_up_bf16(wg_ref, wu_ref)
        w2_bf_ref[...] = wd_ref[0].astype(BF16)

    @pl.when(i < nt_ref[0])
    def _():
        for s in range(MOE_TILE // MOE_SUBTILE):
            rows = slice(s * MOE_SUBTILE, (s + 1) * MOE_SUBTILE)
            x_ref = x_lo_ref if rows.start < half else x_hi_ref
            t_a, t_b = _unpack_bf16_pair(x_ref[rows.start % half:rows.start % half + MOE_SUBTILE, :])
            hid = _expert_hidden(t_a.astype(BF16), t_b.astype(BF16), w1_bf_ref[...])
            o_ref[rows, :] = _pack_bf16_pair(_dot(hid.astype(BF16), w2_bf_ref[...]))


def _moe_grouped(xs, tile_expert, n_tiles, order, w):
    rows = xs.shape[0]
    row = lambda i, te, nt, od: (jnp.minimum(i, nt[0] - 1), 0)
    row_lo = lambda i, te, nt, od: (2 * jnp.minimum(i, nt[0] - 1), 0)
    row_hi = lambda i, te, nt, od: (2 * jnp.minimum(i, nt[0] - 1) + 1, 0)
    by_expert = lambda i, te, nt, od: (te[i], 0, 0)
    return pl.pallas_call(
        _moe_grouped_kernel,
        out_shape=jax.ShapeDtypeStruct((rows, D_MODEL // 2), jnp.uint32),
        grid_spec=pltpu.PrefetchScalarGridSpec(
            num_scalar_prefetch=3,
            grid=(rows // MOE_TILE,),
            in_specs=[pl.BlockSpec((MOE_TILE // 2, D_MODEL // 2), row_lo),
                      pl.BlockSpec((MOE_TILE // 2, D_MODEL // 2), row_hi),
                      pl.BlockSpec((1, D_MODEL, EXPERT_FF), by_expert),
                      pl.BlockSpec((1, D_MODEL, EXPERT_FF), by_expert),
                      pl.BlockSpec((1, EXPERT_FF, D_MODEL), by_expert)],
            out_specs=pl.BlockSpec((MOE_TILE, D_MODEL // 2), row),
            scratch_shapes=[pltpu.VMEM((D_MODEL, 2 * EXPERT_FF), BF16), pltpu.VMEM((EXPERT_FF, D_MODEL), BF16)],
        ),
        compiler_params=pltpu.CompilerParams(dimension_semantics=("arbitrary",),
                                             vmem_limit_bytes=VMEM_LIMIT),
        name="moe_grouped",
    )(tile_expert, n_tiles, order, xs, xs, w["w_gate"], w["w_up"], w["w_down"])


def _moe_combine_kernel(h_ref, z1_ref, z2_ref, route_ref, y_ref):
    route = route_ref[...]
    g1, g2 = route[:, 2:3], route[:, 3:4]
    half = D_MODEL // 2
    a1, b1 = _unpack_bf16_pair(z1_ref[...])
    a2, b2 = _unpack_bf16_pair(z2_ref[...])
    y_ref[:, :half] = h_ref[:, :half] + g1 * a1 + g2 * a2
    y_ref[:, half:] = h_ref[:, half:] + g1 * b1 + g2 * b2


def _moe_combine(h, z, route, tm):
    t = h.shape[0]
    nb = t // tm
    tok = lambda i: (i, 0)
    return pl.pallas_call(
        _moe_combine_kernel,
        out_shape=jax.ShapeDtypeStruct((t, D_MODEL), F32),
        grid=(nb,),
        in_specs=[pl.BlockSpec((tm, D_MODEL), tok), pl.BlockSpec((tm, D_MODEL // 2), tok),
                  pl.BlockSpec((tm, D_MODEL // 2), lambda i: (i + nb, 0)), pl.BlockSpec((tm, LANES), tok)],
        out_specs=pl.BlockSpec((tm, D_MODEL), tok),
        compiler_params=pltpu.CompilerParams(dimension_semantics=("parallel",),
                                             vmem_limit_bytes=VMEM_LIMIT),
        name="moe_combine",
    )(h, z, z, route)


def _route_pos_kernel(routet_ref, cnt_ref, upper_ref, pos_ref):
    tm = routet_ref.shape[1]
    cnt = cnt_ref[...]
    padded = jnp.floor((cnt + float(MOE_TILE - 1)) * (1.0 / MOE_TILE)) * float(MOE_TILE)
    p_hi, p_mid, p_lo = _split3(padded)
    upper = upper_ref[...]
    starts = (_dot(p_hi, upper) + _dot(p_mid, upper) + _dot(p_lo, upper))[0:1, :]
    starts_col = jnp.broadcast_to(starts, (LANES, LANES)).T[EXPERT_LANE0:EXPERT_LANE0 + N_EXPERTS, 0:1]
    erow = lax.broadcasted_iota(jnp.int32, (N_EXPERTS, tm), 0).astype(F32)
    for k in range(2):
        e_k = routet_ref[k:k + 1, :]
        pos = (jnp.sum(jnp.where(erow == e_k, starts_col, 0.0), axis=0, keepdims=True)
               + routet_ref[4 + k:5 + k, :]).astype(jnp.int32)
        for r in range(tm // LANES):
            pos_ref[k, r:r + 1, :] = pos[:, r * LANES:(r + 1) * LANES]


def _route_positions(routet, counts, tm):
    t = routet.shape[1]
    idx = jnp.arange(LANES)
    upper = (idx[:, None] < idx[None, :]).astype(BF16)
    return pl.pallas_call(
        _route_pos_kernel,
        out_shape=jax.ShapeDtypeStruct((2, t // LANES, LANES), jnp.int32),
        grid=(t // tm,),
        in_specs=[pl.BlockSpec((8, tm), lambda i: (0, i)), pl.BlockSpec((8, LANES), lambda i: (0, 0)),
                  pl.BlockSpec((LANES, LANES), lambda i: (0, 0))],
        out_specs=pl.BlockSpec((2, tm // LANES, LANES), lambda i: (0, i, 0)),
        compiler_params=pltpu.CompilerParams(dimension_semantics=("parallel",),
                                             vmem_limit_bytes=VMEM_LIMIT),
        name="route_positions",
    )(routet, counts, upper)


def _moe_routed(h, t_packed, route, routet, counts, w, tm, run_first):
    t = h.shape[0]
    pos = _route_positions(routet, counts, min(32 * LANES, t))
    pos1 = pos[0].reshape(t)
    pos2 = pos[1].reshape(t)
    cnt = counts[0, EXPERT_LANE0:EXPERT_LANE0 + N_EXPERTS].astype(jnp.int32)
    padded = (cnt + MOE_TILE - 1) // MOE_TILE * MOE_TILE
    ends = jnp.cumsum(padded)
    n_rows = 2 * t + N_EXPERTS * MOE_TILE
    n_tiles = ends[N_EXPERTS - 1] // MOE_TILE
    tile_start = jnp.arange(n_rows // MOE_TILE, dtype=jnp.int32) * MOE_TILE
    tile_start = jnp.minimum(tile_start, ends[N_EXPERTS - 1] - MOE_TILE)
    tile_expert = jnp.sum((tile_start[:, None] >= ends[None, :]).astype(jnp.int32), axis=1)
    xs = _sc_scatter_rows(t_packed, pos1, pos2, n_rows)
    order = lax.bitcast_convert_type(run_first.reshape(-1)[:1].astype(F32), jnp.int32)
    out = _moe_grouped(xs, tile_expert, n_tiles.reshape(1), order, w)
    z = _sc_gather_rows(out, pos.reshape(2 * t))
    return _moe_combine(h, z, route, tm)


def _pad_lanes(a, width=LANES):
    return jnp.pad(a, ((0, 0), (0, width - a.shape[1])))


def _prep_weights(norm1, w_in, q_norm, k_norm, conv_w, conv_b, dt_bias, a_log, d_skip, ssm_norm, w_out,
                  norm2, w_grp, b_grp, w_exp, b_exp, w_gate, w_up, w_down):
    w = {}
    w["norm1"] = norm1.reshape(1, D_MODEL)
    w["w_in"] = _pad_lanes(w_in, XBC_END + LANES).astype(BF16)
    w["qkn"] = jnp.concatenate([jnp.tile(q_norm, N_HEADS), jnp.tile(k_norm, KV_HEADS)]).reshape(1, QK_WIDTH)
    head_of_col = jnp.arange(QK_WIDTH) // HEAD_DIM
    red = (head_of_col[:, None] == jnp.arange(LANES)[None, :])
    w["red"] = red.astype(BF16)
    w["exp"] = red.T.astype(BF16)
    w["conv_w"] = conv_w
    w["conv_b"] = conv_b.reshape(1, CONV_DIM)
    w["dt_bias"] = _pad_lanes(dt_bias.reshape(1, SSM_HEADS))
    w["a_log"] = _pad_lanes(a_log.reshape(1, SSM_HEADS))
    w["d_skip"] = jnp.repeat(d_skip, SSM_HEAD_DIM).reshape(1, SSM_WIDTH)
    w["ssm_norm"] = ssm_norm.reshape(1, SSM_WIDTH)
    idx = jnp.arange(SSD_CHUNK)
    w["tri"] = (idx[None, :] <= idx[:, None]).astype(BF16)
    lane_head = jnp.arange(SSM_WIDTH) // SSM_HEAD_DIM
    w["expand"] = (jnp.arange(LANES)[:, None] == lane_head[None, :]).astype(BF16)
    w["w_out"] = w_out.astype(BF16)
    w["norm2"] = norm2.reshape(1, D_MODEL)
    wr = jnp.zeros((D_MODEL, LANES), F32)
    wr = wr.at[:, :N_EGROUPS].set(w_grp).at[:, EXPERT_LANE0:EXPERT_LANE0 + N_EXPERTS].set(w_exp)
    wr_hi = wr.astype(BF16)
    w["wrt"] = jnp.concatenate([wr_hi, (wr - wr_hi.astype(F32)).astype(BF16)], axis=1).T
    br = jnp.zeros((LANES, 1), F32)
    w["br_col"] = br.at[:N_EGROUPS, 0].set(b_grp).at[EXPERT_LANE0:EXPERT_LANE0 + N_EXPERTS, 0].set(b_exp)
    w["w_gate"], w["w_up"], w["w_down"] = w_gate, w_up, w_down
    return w


def _rope_tables(pos):
    inv = 1.0 / (ROPE_THETA ** (jnp.arange(0, HEAD_DIM, 2, dtype=F32) / HEAD_DIM))
    ang = pos.astype(F32)[:, None] * inv[None, :]
    cos, sin = jnp.cos(ang), jnp.sin(ang)
    reps = LANES // HEAD_DIM
    return (jnp.tile(jnp.concatenate([cos, cos], axis=-1), (1, reps)),
            jnp.tile(jnp.concatenate([-sin, sin], axis=-1), (1, reps)))


def _token_tile(t):
    for tm in (1024, 512, 256, 128, 64, 32, 16):
        if t % tm == 0:
            return tm
    raise ValueError(f"token count {t} is not a multiple of 16")


def kernel(x_prompt, x_sample, cache_win_k, cache_win_v, state_conv, state_ssm, norm1, w_in, q_norm, k_norm,
           sinks, conv_w, conv_b, dt_bias, a_log, d_skip, ssm_norm, w_out, norm2, w_grp, b_grp, w_exp, b_exp,
           w_gate, w_up, w_down):
    depth = norm1.shape[0]
    assert depth == 1, "single-layer stack"
    bp, lp, _ = x_prompt.shape
    bsn, ls, _ = x_sample.shape
    assert ls == 1 and lp % WINDOW == 0 and cache_win_k.shape[2] == WINDOW
    l = 0
    w = _prep_weights(norm1[l], w_in[l], q_norm[l], k_norm[l], conv_w[l], conv_b[l], dt_bias[l], a_log[l],
                      d_skip[l], ssm_norm[l], w_out[l], norm2[l], w_grp[l], b_grp[l], w_exp[l], b_exp[l],
                      w_gate[l], w_up[l], w_down[l])
    sink = sinks[l]

    tp = bp * lp
    xp = x_prompt.reshape(tp, D_MODEL)
    tm_p = _token_tile(lp)
    cos_p, sin_p = _rope_tables(jnp.arange(lp, dtype=jnp.int32))
    q, k, v, z, xbc, dt, k2, v2 = _inproj(xp, w, cos_p, sin_p, tm_p, lp // tm_p)
    attn = _attn_prompt(q, k2, v2, sink, bp, lp)
    ssm, st_p = _ssd_prompt(xbc, z, dt, w, bp, lp)
    h, t, route, routet, counts = _outproj_router(xp, attn, ssm, w, tm_p)
    k3 = k.reshape(bp, lp, KV_HEADS, HEAD_DIM)
    v3 = v.reshape(bp, lp, KV_HEADS, HEAD_DIM)
    win_k_p = k3[:, lp - WINDOW:][None]
    win_v_p = v3[:, lp - WINDOW:][None]
    conv_p = xbc.reshape(bp, lp, CONV_DIM)[:, lp - (CONV_K - 1):][None]
    ssm_p = st_p.reshape(1, bp, SSM_HEADS, SSM_HEAD_DIM, SSM_STATE)

    xs2 = x_sample.reshape(bsn, D_MODEL)
    tm_s = _token_tile(bsn)
    cos_s, sin_s = _rope_tables(jnp.full((tm_s,), PAST_LEN, jnp.int32))
    q_s, k_s, v_s, z_s, xbc_s, dt_s, _, _ = _inproj(xs2, w, cos_s, sin_s, tm_s, 1)
    q4 = q_s.reshape(bsn, KV_HEADS, N_HEADS // KV_HEADS, HEAD_DIM)
    zq = jnp.zeros_like(q4[:, 0])
    qx = jnp.concatenate([jnp.concatenate([q4[:, 0], zq], axis=-1),
                          jnp.concatenate([zq, q4[:, 1]], axis=-1)], axis=1)
    qx = jnp.pad(qx, ((0, 0), (0, BF16_ROWS - N_HEADS), (0, 0)))
    sink_x = jnp.pad(jnp.broadcast_to(sink[:, None], (N_HEADS, LANES)), ((0, BF16_ROWS - N_HEADS), (0, 0)))
    kc = jnp.transpose(cache_win_k[l], (0, 2, 3, 1)).reshape(bsn, KV_WIDTH, WINDOW)
    vc = jnp.transpose(cache_win_v[l], (0, 2, 3, 1)).reshape(bsn, KV_WIDTH, WINDOW)
    ko, vo, attn_s = _attn_sample(qx, kc, k_s.reshape(bsn, KV_WIDTH, 1), vc, v_s.reshape(bsn, KV_WIDTH, 1),
                               sink_x, 8)
    y_prompt = _moe_routed(h, t, route, routet, counts, w, tm_p, attn_s).reshape(bp, lp, D_MODEL)
    cprev_t = jnp.transpose(state_conv[l], (1, 0, 2))
    h0 = state_ssm[l].reshape(bsn, HEAD_PAIRS, LANES, SSM_STATE)
    ssm_s, cnew_t, h1 = _ssd_sample(xbc_s, z_s, dt_s, cprev_t, h0, w, 16)
    h_s, t_s, route_s, _, _ = _outproj_router(xs2, attn_s, ssm_s, w, tm_s)
    y_sample = _moe_dense(h_s, t_s, route_s, w).reshape(bsn, 1, D_MODEL)
    win_k_s = jnp.transpose(ko.reshape(bsn, KV_HEADS, HEAD_DIM, WINDOW), (0, 3, 1, 2))[None]
    win_v_s = jnp.transpose(vo.reshape(bsn, KV_HEADS, HEAD_DIM, WINDOW), (0, 3, 1, 2))[None]
    conv_s = jnp.transpose(cnew_t, (1, 0, 2))[None]
    ssm_s_state = h1.reshape(1, bsn, SSM_HEADS, SSM_HEAD_DIM, SSM_STATE)

    return (y_prompt, y_sample, win_k_p, win_v_p, conv_p, ssm_p, win_k_s, win_v_s, conv_s, ssm_s_state)
```

```python
import functools
import math

import jax
import jax.numpy as jnp
from jax import lax
from jax.experimental import pallas as pl
from jax.experimental.pallas import tpu as pltpu
from jax.experimental.pallas import tpu_sc as plsc

F32 = jnp.float32
BF16 = jnp.bfloat16

D_MODEL = 1024
HEAD_DIM = 64
N_HEADS = 8
KV_HEADS = 2
WINDOW = 128
ATTN_WIDTH = N_HEADS * HEAD_DIM
QK_WIDTH = ATTN_WIDTH + KV_HEADS * HEAD_DIM
KV_WIDTH = KV_HEADS * HEAD_DIM
ATTN_SCALE = HEAD_DIM ** -0.5
ROPE_THETA = 10000.0
SSM_WIDTH = 512
SSM_HEADS = 8
SSM_HEAD_DIM = 64
SSM_GROUPS = 2
SSM_STATE = 128
CONV_K = 4
CONV_HALO = 8
CONV_DIM = SSM_WIDTH + 2 * SSM_GROUPS * SSM_STATE
SSD_CHUNK = 128
N_EGROUPS = 4
EXP_PER_GROUP = 8
N_EXPERTS = 32
EXPERT_FF = 128
EPS = 1e-6
PAST_LEN = 16384

LANES = 128
BF16_ROWS = 16
HEAD_PAIRS = SSM_HEADS // 2
EXPERT_LANE0 = 32
VMEM_LIMIT = 56 * 1024 * 1024
MOE_TILE = 1024
MOE_SUBTILE = 128
ATTN_QBLOCKS = 8
SSD_CHUNKS_PER_STEP = 8
COUNT_BLOCK = 256
X_SLOTS = 3
ROUTER_SUBTILE = 1024
INPROJ_SUBTILE = 512
SC_CORES = 2
SC_SUBCORES = 16
SC_WORKERS = SC_CORES * SC_SUBCORES
SC_CHUNK = 64

Q_END = ATTN_WIDTH
K_END = Q_END + KV_WIDTH
V_END = K_END + KV_WIDTH
Z_END = V_END + SSM_WIDTH
XBC_END = Z_END + CONV_DIM


def _dot(a, b):
    return jnp.dot(a, b, preferred_element_type=F32)


def _dot_nt(a, b):
    return lax.dot_general(a, b, (((1,), (1,)), ((), ())), preferred_element_type=F32)


def _split2(v):
    hi = v.astype(BF16)
    lo = (v - hi.astype(F32)).astype(BF16)
    return hi, lo


def _split3(v):
    hi = v.astype(BF16)
    r = v - hi.astype(F32)
    mid = r.astype(BF16)
    lo = (r - mid.astype(F32)).astype(BF16)
    return hi, mid, lo


def _silu(x):
    return x * jax.nn.sigmoid(x)


def _softplus(x):
    return jnp.maximum(x, 0.0) + jnp.log1p(jnp.exp(-jnp.abs(x)))


def _lane_bcast_pairs(v, n_pairs):
    r = v.shape[0]
    lo = lax.broadcasted_iota(jnp.int32, (r, LANES), 1) < HEAD_DIM
    slabs = []
    for j in range(n_pairs):
        a = jnp.broadcast_to(v[:, 2 * j:2 * j + 1], (r, LANES))
        b = jnp.broadcast_to(v[:, 2 * j + 1:2 * j + 2], (r, LANES))
        slabs.append(jnp.where(lo, a, b))
    return jnp.concatenate(slabs, axis=1)


def _causal_conv_silu(x_ext, convw_ref, convb_ref):
    halo = CONV_HALO
    x_raw = x_ext[halo:, :]
    conv = convb_ref[...] + x_raw * convw_ref[CONV_K - 1:CONV_K, :]
    for j in range(CONV_K - 1):
        shifted = pltpu.roll(x_ext, CONV_K - 1 - j, axis=0)[halo:, :]
        conv = conv + shifted * convw_ref[j:j + 1, :]
    return _silu(conv)


def _inproj_kernel(x_ref, n1_ref, win_ref, dtb_ref, qkn_ref,
                   cos_ref, sin_ref, red_ref, exp_ref,
                   q_ref, k_ref, v_ref, z_ref, xbc_ref, dt_ref, k2_ref, v2_ref):
    tm = x_ref.shape[0]
    sub = min(tm, INPROJ_SUBTILE)
    lane = lax.broadcasted_iota(jnp.int32, (sub, LANES), 1)
    first_half = (lane % HEAD_DIM) < (HEAD_DIM // 2)
    for s in range(tm // sub):
        rows = slice(s * sub, (s + 1) * sub)
        x = x_ref[rows, :]
        ms = jnp.mean(x * x, axis=-1, keepdims=True)
        xn = (x * lax.rsqrt(ms + EPS) * n1_ref[...]).astype(BF16)
        v = _dot(xn, win_ref[:, K_END:V_END])
        v_ref[rows, :] = v
        v2_ref[rows, :] = _pair_operands(v)
        z_ref[rows, :] = _dot(xn, win_ref[:, V_END:Z_END])
        xbc_ref[rows, :] = _dot(xn, win_ref[:, Z_END:XBC_END])
        dt_ref[rows, :] = _softplus(_dot(xn, win_ref[:, XBC_END:]) + dtb_ref[...])
        qk = _dot(xn, win_ref[:, :K_END])
        ss = _dot((qk * qk).astype(BF16), red_ref[...])
        inv = lax.rsqrt(ss * (1.0 / HEAD_DIM) + EPS)
        inv_hi, inv_lo = _split2(inv)
        inv_x = _dot(inv_hi, exp_ref[...]) + _dot(inv_lo, exp_ref[...])
        qkn = qk * inv_x * qkn_ref[...]
        cos = cos_ref[rows, :]
        sin = sin_ref[rows, :]
        for c in range(QK_WIDTH // LANES):
            xc = qkn[:, c * LANES:(c + 1) * LANES]
            partner = jnp.where(first_half,
                                pltpu.roll(xc, LANES - HEAD_DIM // 2, axis=1),
                                pltpu.roll(xc, HEAD_DIM // 2, axis=1))
            rot = xc * cos + partner * sin
            if c < ATTN_WIDTH // LANES:
                q_ref[rows, c * LANES:(c + 1) * LANES] = (rot * ATTN_SCALE).astype(BF16)
            else:
                k_ref[rows, :] = rot
                k2_ref[rows, :] = _pair_operands(rot)


def _inproj(x2d, w, cos_tab, sin_tab, tm, n_pos_blocks):
    t = x2d.shape[0]
    grid = (t // tm,)
    tok = lambda i: (i, 0)
    const = lambda i: (0, 0)
    pos = lambda i: (i % n_pos_blocks, 0)
    full = lambda a: pl.BlockSpec(a.shape, const)
    rows = lambda width, dtype: (jax.ShapeDtypeStruct((t, width), dtype), pl.BlockSpec((tm, width), tok))
    outs = [rows(ATTN_WIDTH, BF16), rows(KV_WIDTH, F32), rows(KV_WIDTH, F32), rows(SSM_WIDTH, F32)]
    operands = [x2d, w["norm1"], w["w_in"], w["dt_bias"], w["qkn"],
                cos_tab, sin_tab, w["red"], w["exp"]]
    in_specs = [
        pl.BlockSpec((tm, D_MODEL), tok),
        full(w["norm1"]), full(w["w_in"]), full(w["dt_bias"]), full(w["qkn"]),
        pl.BlockSpec((tm, LANES), pos), pl.BlockSpec((tm, LANES), pos),
        full(w["red"]), full(w["exp"]),
    ]
    outs += [rows(CONV_DIM, F32), rows(LANES, F32), rows(4 * LANES, BF16), rows(4 * LANES, BF16)]
    return pl.pallas_call(
        _inproj_kernel,
        out_shape=tuple(o[0] for o in outs),
        grid=grid,
        in_specs=in_specs,
        out_specs=tuple(o[1] for o in outs),
        compiler_params=pltpu.CompilerParams(dimension_semantics=("parallel",),
                                             vmem_limit_bytes=VMEM_LIMIT),
        name="inproj",
    )(*operands)


def _pair_operands(kv):
    lo = lax.broadcasted_iota(jnp.int32, kv.shape, 1) < HEAD_DIM
    swapped = pltpu.roll(kv, HEAD_DIM, axis=1)
    parts = [jnp.where(lo, kv, 0.0), jnp.where(lo, 0.0, swapped), jnp.where(lo, swapped, 0.0), jnp.where(lo, 0.0, kv)]
    return jnp.concatenate(parts, axis=1).astype(BF16)


def _pair_rhs(blk, g):
    return jnp.concatenate([blk[:, 2 * g * LANES:(2 * g + 1) * LANES],
                            blk[:, (2 * g + 1) * LANES:(2 * g + 2) * LANES]], axis=0)


def _attn_qblock(sink_ref, q_blk, k_prev, k_cur, v_prev, v_cur, seq_start, o_ref):
    blk = WINDOW
    qi = lax.broadcasted_iota(jnp.int32, (blk, 2 * blk), 0)
    kj = lax.broadcasted_iota(jnp.int32, (blk, 2 * blk), 1) % blk
    cur_ok = kj <= qi
    lo = lax.broadcasted_iota(jnp.int32, (blk, LANES), 1) < HEAD_DIM
    n_pairs = N_HEADS // KV_HEADS // 2
    for g in range(KV_HEADS):
        q_all = jnp.concatenate([q_blk[:, (g * n_pairs + r) * LANES:(g * n_pairs + r + 1) * LANES]
                                 for r in range(n_pairs)], axis=0)
        s_all = _dot_nt(q_all, jnp.concatenate([_pair_rhs(k_cur, g), _pair_rhs(k_prev, g)], axis=0))
        p_rows = []
        den_rows = []
        for r in range(n_pairs):
            pair = g * n_pairs + r
            s_cur = s_all[r * blk:(r + 1) * blk, :2 * blk]
            s_prev = s_all[r * blk:(r + 1) * blk, 2 * blk:]
            s = jnp.where(cur_ok, s_cur, s_prev)
            if seq_start is not None:
                s = jnp.where(jnp.logical_or(cur_ok, jnp.logical_not(seq_start)), s, -jnp.inf)
            ps = []
            dens = []
            for hh in range(2):
                sink = sink_ref[2 * pair + hh]
                sh = s[:, hh * blk:(hh + 1) * blk]
                m = jnp.maximum(jnp.max(sh, axis=-1, keepdims=True), sink)
                p = jnp.exp(sh - m)
                dens.append(jnp.sum(p, axis=-1, keepdims=True) + jnp.exp(sink - m))
                ps.append(p)
            p2 = jnp.concatenate(ps, axis=1)
            p_rows.append(jnp.concatenate([jnp.where(cur_ok, p2, 0.0), jnp.where(cur_ok, 0.0, p2)],
                                          axis=1).astype(BF16))
            den_rows.append(jnp.where(lo, dens[0], dens[1]))
        o_all = _dot(jnp.concatenate(p_rows, axis=0),
                     jnp.concatenate([_pair_rhs(v_cur, g), _pair_rhs(v_prev, g)], axis=0))
        for r in range(n_pairs):
            pair = g * n_pairs + r
            o2 = o_all[r * blk:(r + 1) * blk, :]
            o_ref[:, pair * LANES:(pair + 1) * LANES] = (o2 / den_rows[r]).astype(BF16)


def _attn_kernel(sink_ref, q_ref, kc_ref, kp_ref, vc_ref, vp_ref, o_ref):
    blk = WINDOW
    first_step = pl.program_id(1) == 0
    for u in range(q_ref.shape[0] // blk):
        rows = slice(u * blk, (u + 1) * blk)
        prev_rows = slice((u - 1) * blk, u * blk)
        k_prev = kp_ref[...] if u == 0 else kc_ref[prev_rows, :]
        v_prev = vp_ref[...] if u == 0 else vc_ref[prev_rows, :]
        _attn_qblock(sink_ref, q_ref[rows, :], k_prev, kc_ref[rows, :], v_prev, vc_ref[rows, :],
                     first_step if u == 0 else None, o_ref.at[rows, :])


def _attn_prompt(q, k, v, sinks, batch, seq):
    n_sub = ATTN_QBLOCKS if seq % (ATTN_QBLOCKS * WINDOW) == 0 else 1
    rows = n_sub * WINDOW
    nb = seq // rows
    cur = lambda b, j, s: (b * nb + j, 0)
    prev = lambda b, j, s: (jnp.maximum((b * nb + j) * n_sub - 1, 0), 0)
    return pl.pallas_call(
        _attn_kernel,
        out_shape=jax.ShapeDtypeStruct((batch * seq, ATTN_WIDTH), BF16),
        grid_spec=pltpu.PrefetchScalarGridSpec(
            num_scalar_prefetch=1,
            grid=(batch, nb),
            in_specs=[
                pl.BlockSpec((rows, ATTN_WIDTH), cur),
                pl.BlockSpec((rows, 4 * LANES), cur), pl.BlockSpec((WINDOW, 4 * LANES), prev),
                pl.BlockSpec((rows, 4 * LANES), cur), pl.BlockSpec((WINDOW, 4 * LANES), prev),
            ],
            out_specs=pl.BlockSpec((rows, ATTN_WIDTH), cur),
        ),
        compiler_params=pltpu.CompilerParams(dimension_semantics=("parallel", "parallel"),
                                             vmem_limit_bytes=VMEM_LIMIT),
        name="attn_prompt",
    )(sinks, q, k, k, v, v)


def _ssd_kernel(xbc_ref, z_ref, dt_ref, convw_ref, convb_ref, alog_ref, dskip_ref, nw_ref,
                tri_ref, expand_ref, y_ref, st_ref, buf_ref, state_ref):
    c = pl.program_id(1)
    cl = SSD_CHUNK
    n_sub = xbc_ref.shape[0] // cl
    halo = CONV_HALO

    @pl.when(c == 0)
    def _():
        buf_ref[...] = jnp.zeros(buf_ref.shape, F32)
        state_ref[...] = jnp.zeros(state_ref.shape, F32)

    lane = lax.broadcasted_iota(jnp.int32, (1, LANES), 1)
    a_neg = jnp.where(lane < SSM_HEADS, -jnp.exp(alog_ref[...]), 0.0)
    tri = tri_ref[...]
    for u in range(n_sub):
        rows = slice(u * cl, (u + 1) * cl)
        if u == 0:
            x_ext = jnp.concatenate([buf_ref[...], xbc_ref[rows, :]], axis=0)
        else:
            x_ext = xbc_ref[u * cl - halo:(u + 1) * cl, :]
        _ssd_chunk(x_ext, z_ref[rows, :], dt_ref[rows, :], a_neg, tri, convw_ref, convb_ref, dskip_ref, nw_ref,
                   expand_ref, y_ref.at[rows, :], state_ref)
    buf_ref[...] = xbc_ref[n_sub * cl - halo:n_sub * cl, :]

    @pl.when(c == pl.num_programs(1) - 1)
    def _():
        st_ref[0] = state_ref[...]


def _ssd_chunk(x_ext, z, dt, a_neg, tri, convw_ref, convb_ref, dskip_ref, nw_ref, expand_ref, y_ref, state_ref):
    cl = SSD_CHUNK
    act = _causal_conv_silu(x_ext, convw_ref, convb_ref)
    xs = act[:, :SSM_WIDTH]
    bm = act[:, SSM_WIDTH:SSM_WIDTH + SSM_GROUPS * SSM_STATE].astype(BF16)
    cm = act[:, SSM_WIDTH + SSM_GROUPS * SSM_STATE:].astype(BF16)

    dta = dt * a_neg
    p_hi, p_mid, p_lo = _split3(dta)
    a_col = _dot(tri, p_hi) + _dot(tri, p_mid) + _dot(tri, p_lo)
    a_last = a_col[cl - 1:cl, :]
    a_row = a_col.T
    per_head = jnp.concatenate([dt, jnp.exp(a_col), jnp.exp(a_last - a_col)], axis=0)
    ph_hi, ph_lo = _split2(per_head)
    per_lane = _dot(ph_hi, expand_ref[...]) + _dot(ph_lo, expand_ref[...])
    dt_x = per_lane[:cl]
    ecol_x = per_lane[cl:2 * cl]
    dte_x = per_lane[2 * cl:]
    e_last = jnp.exp(a_last)
    xdt = xs * dt_x

    li = lax.broadcasted_iota(jnp.int32, (cl, cl), 0)
    si = lax.broadcasted_iota(jnp.int32, (cl, cl), 1)
    causal = si <= li
    lo = lax.broadcasted_iota(jnp.int32, (cl, LANES), 1) < SSM_HEAD_DIM
    row_lo = lax.broadcasted_iota(jnp.int32, (LANES, SSM_STATE), 0) < SSM_HEAD_DIM

    ys = []
    for g in range(SSM_GROUPS):
        b_g = bm[:, g * SSM_STATE:(g + 1) * SSM_STATE]
        c_g = cm[:, g * SSM_STATE:(g + 1) * SSM_STATE]
        cb = _dot_nt(c_g, b_g)
        for r in range(HEAD_PAIRS // SSM_GROUPS):
            j = g * (HEAD_PAIRS // SSM_GROUPS) + r
            sl = slice(j * LANES, (j + 1) * LANES)
            xdt_p = xdt[:, sl]
            ms = []
            for hh in range(2):
                h = 2 * j + hh
                seg = a_col[:, h:h + 1] - a_row[h:h + 1, :]
                ms.append(cb * jnp.exp(jnp.where(causal, seg, -jnp.inf)))
            m2 = jnp.concatenate(ms, axis=1).astype(BF16)
            rhs = jnp.concatenate([jnp.where(lo, xdt_p, 0.0), jnp.where(lo, 0.0, xdt_p)],
                                  axis=0).astype(BF16)
            y_diag = _dot(m2, rhs)
            st = state_ref[j]
            y_off = _dot_nt(c_g, st.astype(BF16)) * ecol_x[:, sl]
            xdt_e = (xdt_p * dte_x[:, sl]).T.astype(BF16)
            d_a = e_last[:, 2 * j:2 * j + 1]
            d_b = e_last[:, 2 * j + 1:2 * j + 2]
            decay = jnp.where(row_lo, jnp.broadcast_to(d_a, row_lo.shape), jnp.broadcast_to(d_b, row_lo.shape))
            state_ref[j] = decay * st + _dot(xdt_e, b_g)
            ys.append(y_diag + y_off + dskip_ref[:, sl] * xs[:, sl])
    y = jnp.concatenate(ys, axis=1)
    gated = y * _silu(z)
    gw = SSM_WIDTH // SSM_GROUPS
    outs = []
    for g in range(SSM_GROUPS):
        gg = gated[:, g * gw:(g + 1) * gw]
        outs.append(gg * lax.rsqrt(jnp.mean(gg * gg, axis=-1, keepdims=True) + EPS))
    y_ref[...] = (jnp.concatenate(outs, axis=1) * nw_ref[...]).astype(BF16)


def _ssd_prompt(xbc, z, dt, w, batch, seq):
    n_sub = SSD_CHUNKS_PER_STEP if seq % (SSD_CHUNKS_PER_STEP * SSD_CHUNK) == 0 else 1
    rows = n_sub * SSD_CHUNK
    nc = seq // rows
    tok = lambda b, c: (b * nc + c, 0)
    const = lambda b, c: (0, 0)
    full = lambda a: pl.BlockSpec(a.shape, const)
    return pl.pallas_call(
        _ssd_kernel,
        out_shape=(jax.ShapeDtypeStruct((batch * seq, SSM_WIDTH), BF16),
                   jax.ShapeDtypeStruct((batch, HEAD_PAIRS, LANES, SSM_STATE), F32)),
        grid=(batch, nc),
        in_specs=[
            pl.BlockSpec((rows, CONV_DIM), tok), pl.BlockSpec((rows, SSM_WIDTH), tok),
            pl.BlockSpec((rows, LANES), tok),
            full(w["conv_w"]), full(w["conv_b"]), full(w["a_log"]),
            full(w["d_skip"]), full(w["ssm_norm"]), full(w["tri"]), full(w["expand"]),
        ],
        out_specs=(pl.BlockSpec((rows, SSM_WIDTH), tok),
                   pl.BlockSpec((1, HEAD_PAIRS, LANES, SSM_STATE), lambda b, c: (b, 0, 0, 0))),
        scratch_shapes=[pltpu.VMEM((CONV_HALO, CONV_DIM), F32),
                        pltpu.VMEM((HEAD_PAIRS, LANES, SSM_STATE), F32)],
        compiler_params=pltpu.CompilerParams(dimension_semantics=("parallel", "arbitrary"),
                                             vmem_limit_bytes=VMEM_LIMIT),
        name="ssd_prompt",
    )(xbc, z, dt, w["conv_w"], w["conv_b"], w["a_log"], w["d_skip"], w["ssm_norm"], w["tri"], w["expand"])


def _attn_sample_kernel(qx_ref, kc_ref, kn_ref, vc_ref, vn_ref, sink_ref, ko_ref, vo_ref, o_ref):
    bs = qx_ref.shape[0]
    w = kc_ref.shape[2]
    sink = sink_ref[...]
    lo = lax.broadcasted_iota(jnp.int32, (1, LANES), 1) < HEAD_DIM
    newest = lax.broadcasted_iota(jnp.int32, (1, w), 1) == w - 1
    for i in range(bs):
        k_t = jnp.where(newest, kn_ref[i], pltpu.roll(kc_ref[i], w - 1, axis=1))
        v_t = jnp.where(newest, vn_ref[i], pltpu.roll(vc_ref[i], w - 1, axis=1))
        ko_ref[i] = k_t
        vo_ref[i] = v_t
        s = _dot(qx_ref[i], k_t.astype(BF16))
        m = jnp.maximum(jnp.max(s, axis=-1, keepdims=True), sink)
        p = jnp.exp(s - m)
        den = jnp.sum(p, axis=-1, keepdims=True) + jnp.exp(sink - m)
        o = _dot_nt(p.astype(BF16), v_t.astype(BF16)) / den
        o_sw = pltpu.roll(o, HEAD_DIM, axis=1)
        for j in range(N_HEADS // 2):
            a, b = (o, o_sw) if j < N_HEADS // 4 else (o_sw, o)
            o_ref[i:i + 1, j * LANES:(j + 1) * LANES] = jnp.where(lo, a[2 * j:2 * j + 1], b[2 * j + 1:2 * j + 2])


def _attn_sample(qx, kc, kn, vc, vn, sink_x, bs):
    n, w = kc.shape[0], kc.shape[2]
    blk3 = lambda i: (i, 0, 0)
    return pl.pallas_call(
        _attn_sample_kernel,
        out_shape=(jax.ShapeDtypeStruct((n, KV_WIDTH, w), F32),
                   jax.ShapeDtypeStruct((n, KV_WIDTH, w), F32),
                   jax.ShapeDtypeStruct((n, ATTN_WIDTH), F32)),
        grid=(n // bs,),
        in_specs=[
            pl.BlockSpec((bs, BF16_ROWS, LANES), blk3),
            pl.BlockSpec((bs, KV_WIDTH, w), blk3), pl.BlockSpec((bs, KV_WIDTH, 1), blk3),
            pl.BlockSpec((bs, KV_WIDTH, w), blk3), pl.BlockSpec((bs, KV_WIDTH, 1), blk3),
            pl.BlockSpec(sink_x.shape, lambda i: (0, 0)),
        ],
        out_specs=(pl.BlockSpec((bs, KV_WIDTH, w), blk3), pl.BlockSpec((bs, KV_WIDTH, w), blk3),
                   pl.BlockSpec((bs, ATTN_WIDTH), lambda i: (i, 0))),
        compiler_params=pltpu.CompilerParams(dimension_semantics=("parallel",),
                                             vmem_limit_bytes=VMEM_LIMIT),
        name="attn_sample",
    )(qx, kc, kn, vc, vn, sink_x)


def _ssd_sample_kernel(xbc_ref, z_ref, dt_ref, cprev_ref, h0_ref, convw_ref, convb_ref, alog_ref,
                       dskip_ref, nw_ref, y_ref, cnew_ref, h1_ref):
    bs = xbc_ref.shape[0]
    x_raw = xbc_ref[...]
    conv = convb_ref[...] + x_raw * convw_ref[CONV_K - 1:CONV_K, :]
    for j in range(CONV_K - 1):
        conv = conv + cprev_ref[j] * convw_ref[j:j + 1, :]
    for j in range(CONV_K - 2):
        cnew_ref[j] = cprev_ref[j + 1]
    cnew_ref[CONV_K - 2] = x_raw
    act = _silu(conv)
    xs = act[:, :SSM_WIDTH]
    bm = act[:, SSM_WIDTH:SSM_WIDTH + SSM_GROUPS * SSM_STATE].astype(BF16)
    cm = act[:, SSM_WIDTH + SSM_GROUPS * SSM_STATE:].astype(BF16)
    lane = lax.broadcasted_iota(jnp.int32, (1, LANES), 1)
    a_neg = jnp.where(lane < SSM_HEADS, -jnp.exp(alog_ref[...]), 0.0)
    dt = dt_ref[...]
    dec = jnp.exp(dt * a_neg)
    xdt = xs * _lane_bcast_pairs(dt, HEAD_PAIRS)
    rowid = lax.broadcasted_iota(jnp.int32, (bs, LANES), 0)
    row_lo = lax.broadcasted_iota(jnp.int32, (LANES, SSM_STATE), 0) < SSM_HEAD_DIM
    ys = []
    for j in range(HEAD_PAIRS):
        g = j // (HEAD_PAIRS // SSM_GROUPS)
        sl = slice(j * LANES, (j + 1) * LANES)
        b_g = bm[:, g * SSM_STATE:(g + 1) * SSM_STATE]
        c_g = cm[:, g * SSM_STATE:(g + 1) * SSM_STATE]
        xdt_p = xdt[:, sl]
        y_p = jnp.zeros((bs, LANES), F32)
        for i in range(bs):
            xi = jnp.where(rowid == i, xdt_p, 0.0).T.astype(BF16)
            d_a = dec[i:i + 1, 2 * j:2 * j + 1]
            d_b = dec[i:i + 1, 2 * j + 1:2 * j + 2]
            decay = jnp.where(row_lo, jnp.broadcast_to(d_a, row_lo.shape), jnp.broadcast_to(d_b, row_lo.shape))
            new = decay * h0_ref[i, j] + _dot(xi, b_g)
            h1_ref[i, j] = new
            y_p = y_p + jnp.where(rowid == i, _dot_nt(c_g, new.astype(BF16)), 0.0)
        ys.append(y_p + dskip_ref[:, sl] * xs[:, sl])
    y = jnp.concatenate(ys, axis=1)
    gated = y * _silu(z_ref[...])
    gw = SSM_WIDTH // SSM_GROUPS
    outs = []
    for g in range(SSM_GROUPS):
        gg = gated[:, g * gw:(g + 1) * gw]
        outs.append(gg * lax.rsqrt(jnp.mean(gg * gg, axis=-1, keepdims=True) + EPS))
    y_ref[...] = (jnp.concatenate(outs, axis=1) * nw_ref[...]).astype(BF16)


def _ssd_sample(xbc, z, dt, cprev_t, h0, w, bs):
    n = xbc.shape[0]
    tok = lambda i: (i, 0)
    const = lambda i: (0, 0)
    full = lambda a: pl.BlockSpec(a.shape, const)
    return pl.pallas_call(
        _ssd_sample_kernel,
        out_shape=(jax.ShapeDtypeStruct((n, SSM_WIDTH), BF16),
                   jax.ShapeDtypeStruct((CONV_K - 1, n, CONV_DIM), F32),
                   jax.ShapeDtypeStruct((n, HEAD_PAIRS, LANES, SSM_STATE), F32)),
        grid=(n // bs,),
        in_specs=[
            pl.BlockSpec((bs, CONV_DIM), tok), pl.BlockSpec((bs, SSM_WIDTH), tok),
            pl.BlockSpec((bs, LANES), tok),
            pl.BlockSpec((CONV_K - 1, bs, CONV_DIM), lambda i: (0, i, 0)),
            pl.BlockSpec((bs, HEAD_PAIRS, LANES, SSM_STATE), lambda i: (i, 0, 0, 0)),
            full(w["conv_w"]), full(w["conv_b"]), full(w["a_log"]),
            full(w["d_skip"]), full(w["ssm_norm"]),
        ],
        out_specs=(pl.BlockSpec((bs, SSM_WIDTH), tok),
                   pl.BlockSpec((CONV_K - 1, bs, CONV_DIM), lambda i: (0, i, 0)),
                   pl.BlockSpec((bs, HEAD_PAIRS, LANES, SSM_STATE), lambda i: (i, 0, 0, 0))),
        compiler_params=pltpu.CompilerParams(dimension_semantics=("parallel",),
                                             vmem_limit_bytes=VMEM_LIMIT),
        name="ssd_sample",
    )(xbc, z, dt, cprev_t, h0, w["conv_w"], w["conv_b"], w["a_log"], w["d_skip"], w["ssm_norm"])


def _pack_bf16_pair(v):
    c = v.shape[1] // 2
    hi = lax.bitcast_convert_type(v[:, :c].astype(BF16).astype(F32), jnp.uint32)
    lo = lax.bitcast_convert_type(v[:, c:].astype(BF16).astype(F32), jnp.uint32)
    return hi | (lo >> 16)


def _unpack_bf16_pair(word):
    a = lax.bitcast_convert_type(word & jnp.uint32(0xFFFF0000), F32)
    b = lax.bitcast_convert_type(word << 16, F32)
    return a, b


def _outproj_router_kernel(x_hbm, a_ref, s_ref, wo_ref, n2_ref, wr_ref, br_ref, tri_ref,
                           h_ref, t_ref, route_ref, routet_ref, cnt_ref, carry_ref, xbuf_ref, xsem_ref):
    step = pl.program_id(0)
    n_steps = pl.num_programs(0)
    tm = a_ref.shape[0]

    def x_copy(st):
        slot = lax.rem(st, X_SLOTS)
        return pltpu.make_async_copy(x_hbm.at[pl.ds(pl.multiple_of(st * tm, 8), tm)], xbuf_ref.at[slot],
                                     xsem_ref.at[slot])

    @pl.when(step == 0)
    def _():
        carry_ref[...] = jnp.zeros(carry_ref.shape, F32)
        for st in range(X_SLOTS - 1):
            pl.when(st < n_steps)(lambda st=st: x_copy(st).start())

    @pl.when(step + (X_SLOTS - 1) < n_steps)
    def _():
        x_copy(step + (X_SLOTS - 1)).start()

    x_copy(step).wait()
    x_ref = xbuf_ref.at[lax.rem(step, X_SLOTS)]
    h_ref[...] = (x_ref[...] + _dot(a_ref[...].astype(BF16), wo_ref[:ATTN_WIDTH, :])
                  + _dot(s_ref[...].astype(BF16), wo_ref[ATTN_WIDTH:, :]))
    sub = min(tm, ROUTER_SUBTILE)
    carry = carry_ref[:, 0:1]
    for s in range(tm // sub):
        rows = slice(s * sub, (s + 1) * sub)
        carry = _route_rows(h_ref[rows, :], n2_ref, wr_ref, br_ref, tri_ref, carry, t_ref.at[rows, :],
                            route_ref.at[rows, :], routet_ref.at[:, rows])
    carry_ref[...] = jnp.broadcast_to(carry, carry_ref.shape)
    cpad = jnp.concatenate([jnp.zeros((EXPERT_LANE0, LANES), F32), jnp.broadcast_to(carry, (N_EXPERTS, LANES)),
                            jnp.zeros((LANES - EXPERT_LANE0 - N_EXPERTS, LANES), F32)], axis=0)
    cnt_ref[...] = cpad.T[0:cnt_ref.shape[0], :]


def _route_rows(h, n2_ref, wrt_ref, brc_ref, triu_ref, carry, t_ref, route_ref, routet_ref):
    ms = jnp.mean(h * h, axis=-1, keepdims=True)
    t = h * lax.rsqrt(ms + EPS) * n2_ref[...]
    t_ref[...] = _pack_bf16_pair(t)
    n = h.shape[0]
    a = _dot_nt(wrt_ref[...], t.astype(BF16))

    def logit_rows(r0, r1):
        return a[r0:r1] + a[LANES + r0:LANES + r1] + brc_ref[r0:r1, :]

    sl = EXP_PER_GROUP
    glog = logit_rows(0, sl)
    elog = logit_rows(EXPERT_LANE0, EXPERT_LANE0 + N_EXPERTS)
    row = lax.broadcasted_iota(jnp.int32, (sl, n), 0).astype(F32)
    big = float(sl)
    ninf = -jnp.inf
    gm = jnp.where(row < N_EGROUPS, glog, ninf)
    gmax = jnp.max(gm, axis=0, keepdims=True)
    g_top = 1.0 / jnp.sum(jnp.exp(gm - gmax), axis=0, keepdims=True)
    g_idx = jnp.min(jnp.where(gm == gmax, row, big), axis=0, keepdims=True)
    ml = elog[(N_EGROUPS - 1) * sl:]
    for g in range(N_EGROUPS - 2, -1, -1):
        ml = jnp.where(g_idx == float(g), elog[g * sl:(g + 1) * sl], ml)
    m1 = jnp.max(ml, axis=0, keepdims=True)
    i1 = jnp.min(jnp.where(ml == m1, row, big), axis=0, keepdims=True)
    ml2 = jnp.where(row == i1, ninf, ml)
    m2 = jnp.max(ml2, axis=0, keepdims=True)
    i2 = jnp.min(jnp.where(ml2 == m2, row, big), axis=0, keepdims=True)
    r = jnp.exp(m2 - m1)
    w1 = g_top / (1.0 + r)
    w2 = g_top * r / (1.0 + r)
    e1 = g_idx * float(sl) + i1
    e2 = g_idx * float(sl) + i2
    pick = jnp.where(jnp.logical_or(row == i1, row == i2), 1.0, 0.0)
    onehot = jnp.concatenate([jnp.where(g_idx == float(g), pick, 0.0) for g in range(N_EGROUPS)], axis=0)
    onehot_bf = onehot.astype(BF16)
    cb = triu_ref.shape[0]
    cums = []
    for blk in range(n // cb):
        c = _dot(onehot_bf[:, blk * cb:(blk + 1) * cb], triu_ref[...]) + carry
        carry = c[:, cb - 1:cb]
        cums.append(c)
    before = jnp.concatenate(cums, axis=1) - onehot
    erow = lax.broadcasted_iota(jnp.int32, (N_EXPERTS, n), 0).astype(F32)
    rank1 = jnp.sum(jnp.where(erow == e1, before, 0.0), axis=0, keepdims=True)
    rank2 = jnp.sum(jnp.where(erow == e2, before, 0.0), axis=0, keepdims=True)
    fields = jnp.concatenate([e1, e2, w1, w2, rank1, rank2], axis=0)
    routet_ref[0:fields.shape[0], :] = fields
    routet_ref[fields.shape[0]:, :] = jnp.zeros((routet_ref.shape[0] - fields.shape[0], n), F32)
    if n % LANES:
        fields = jnp.concatenate([fields, jnp.zeros((fields.shape[0], LANES - n % LANES), F32)], axis=1)
    pad = jnp.zeros((LANES - fields.shape[0], LANES), F32)
    for j in range(fields.shape[1] // LANES):
        blk_rows = jnp.concatenate([fields[:, j * LANES:(j + 1) * LANES], pad], axis=0)
        n_valid = min(LANES, n - j * LANES)
        route_ref[j * LANES:j * LANES + n_valid, :] = blk_rows.T[:n_valid, :]
    return carry


def _outproj_router(x2d, attn, ssm, w, tm):
    t = x2d.shape[0]
    tok = lambda i: (i, 0)
    const = lambda i: (0, 0)
    full = lambda a: pl.BlockSpec(a.shape, const)
    idx = jnp.arange(min(tm, COUNT_BLOCK))
    tri = (idx[:, None] <= idx[None, :]).astype(BF16)
    return pl.pallas_call(
        _outproj_router_kernel,
        out_shape=(jax.ShapeDtypeStruct((t, D_MODEL), F32), jax.ShapeDtypeStruct((t, D_MODEL // 2), jnp.uint32),
                   jax.ShapeDtypeStruct((t, LANES), F32), jax.ShapeDtypeStruct((8, t), F32),
                   jax.ShapeDtypeStruct((8, LANES), F32)),
        grid=(t // tm,),
        in_specs=[
            pl.BlockSpec(memory_space=pl.ANY), pl.BlockSpec((tm, ATTN_WIDTH), tok),
            pl.BlockSpec((tm, SSM_WIDTH), tok),
            full(w["w_out"]), full(w["norm2"]), full(w["wrt"]),
            full(w["br_col"]), full(tri),
        ],
        out_specs=(pl.BlockSpec((tm, D_MODEL), tok), pl.BlockSpec((tm, D_MODEL // 2), tok),
                   pl.BlockSpec((tm, LANES), tok), pl.BlockSpec((8, tm), lambda i: (0, i)),
                   pl.BlockSpec((8, LANES), const)),
        scratch_shapes=[pltpu.VMEM((N_EXPERTS, LANES), F32), pltpu.VMEM((X_SLOTS, tm, D_MODEL), F32),
                        pltpu.SemaphoreType.DMA((X_SLOTS,))],
        compiler_params=pltpu.CompilerParams(dimension_semantics=("arbitrary",),
                                             vmem_limit_bytes=VMEM_LIMIT),
        name="outproj_router",
    )(x2d, attn, ssm, w["w_out"], w["norm2"], w["wrt"], w["br_col"], tri)


def _expert_hidden(t_a, t_b, w1):
    half = D_MODEL // 2
    gu = _dot(t_a, w1[:half]) + _dot(t_b, w1[half:])
    return _silu(gu[:, :EXPERT_FF]) * gu[:, EXPERT_FF:]


def _gate_up_bf16(wg_ref, wu_ref):
    return jnp.concatenate([wg_ref[0].astype(BF16), wu_ref[0].astype(BF16)], axis=1)


def _moe_dense_kernel(h_ref, t_ref, route_ref, wg_ref, wu_ref, wd_ref, y_ref):
    e = pl.program_id(0)

    @pl.when(e == 0)
    def _():
        y_ref[...] = h_ref[...]

    t_a, t_b = _unpack_bf16_pair(t_ref[...])
    route = route_ref[...]
    e1, e2, g1, g2 = route[:, 0:1], route[:, 1:2], route[:, 2:3], route[:, 3:4]
    e_f = e.astype(F32)
    hid = _expert_hidden(t_a.astype(BF16), t_b.astype(BF16), _gate_up_bf16(wg_ref, wu_ref))
    c_e = jnp.where(e1 == e_f, g1, 0.0) + jnp.where(e2 == e_f, g2, 0.0)
    y_ref[...] += _dot((hid * c_e).astype(BF16), wd_ref[0].astype(BF16))


def _moe_dense(h, t, route, w):
    n = h.shape[0]
    whole = lambda e: (0, 0)
    by_expert = lambda e: (e, 0, 0)
    return pl.pallas_call(
        _moe_dense_kernel,
        out_shape=jax.ShapeDtypeStruct((n, D_MODEL), F32),
        grid=(N_EXPERTS,),
        in_specs=[pl.BlockSpec((n, D_MODEL), whole), pl.BlockSpec((n, D_MODEL // 2), whole),
                  pl.BlockSpec((n, LANES), whole),
                  pl.BlockSpec((1, D_MODEL, EXPERT_FF), by_expert), pl.BlockSpec((1, D_MODEL, EXPERT_FF), by_expert),
                  pl.BlockSpec((1, EXPERT_FF, D_MODEL), by_expert)],
        out_specs=pl.BlockSpec((n, D_MODEL), whole),
        compiler_params=pltpu.CompilerParams(dimension_semantics=("arbitrary",),
                                             vmem_limit_bytes=VMEM_LIMIT),
        name="moe_dense",
    )(h, t, route, w["w_gate"], w["w_up"], w["w_down"])


def _sc_scatter_rows(src, pos1, pos2, n_out):
    t, width = src.shape
    rows_per_worker = t // SC_WORKERS
    n_chunks = rows_per_worker // SC_CHUNK
    assert t == SC_WORKERS * SC_CHUNK * n_chunks and n_chunks % 2 == 0
    mesh = plsc.VectorSubcoreMesh(core_axis_name="c", subcore_axis_name="s")

    @functools.partial(
        pl.kernel, mesh=mesh,
        out_type=jax.ShapeDtypeStruct((n_out, width), src.dtype),
        scratch_types=[pltpu.VMEM((2, SC_CHUNK), jnp.int32), pltpu.VMEM((2, SC_CHUNK), jnp.int32),
                       pltpu.VMEM((2, SC_CHUNK, width), src.dtype),
                       pltpu.SemaphoreType.DMA, pltpu.SemaphoreType.DMA,
                       pltpu.SemaphoreType.DMA, pltpu.SemaphoreType.DMA],
    )
    def scatter_kernel(src_hbm, p1_hbm, p2_hbm, out_hbm, i1_v, i2_v, rows_v, l0, l1, s0, s1):
        wid = lax.axis_index("s") * SC_CORES + lax.axis_index("c")
        base = wid * rows_per_worker
        lsem = (l0, l1)
        ssem = (s0, s1)

        def load_copy(c, slot):
            off = pl.multiple_of(base + c * SC_CHUNK, 8)
            return pltpu.make_async_copy(src_hbm.at[pl.ds(off, SC_CHUNK)], rows_v.at[slot], lsem[slot])

        def start_load(c, slot):
            off = pl.multiple_of(base + c * SC_CHUNK, 8)
            pltpu.sync_copy(p1_hbm.at[pl.ds(off, SC_CHUNK)], i1_v.at[slot])
            pltpu.sync_copy(p2_hbm.at[pl.ds(off, SC_CHUNK)], i2_v.at[slot])
            load_copy(c, slot).start()

        def scatter_copies(slot):
            return (pltpu.make_async_copy(rows_v.at[slot], out_hbm.at[i1_v.at[slot]], ssem[slot]),
                    pltpu.make_async_copy(rows_v.at[slot], out_hbm.at[i2_v.at[slot]], ssem[slot]))

        def start_scatter(slot):
            for cp in scatter_copies(slot):
                cp.start()

        def wait_scatter(slot):
            for cp in scatter_copies(slot):
                cp.wait()

        start_load(0, 0)

        @pl.loop(0, n_chunks, step=2)
        def _(c):
            @pl.when(c > 0)
            def _():
                wait_scatter(1)

            start_load(c + 1, 1)
            load_copy(c, 0).wait()
            start_scatter(0)
            load_copy(c + 1, 1).wait()
            wait_scatter(0)
            start_scatter(1)

            @pl.when(c + 2 < n_chunks)
            def _():
                start_load(c + 2, 0)

        wait_scatter(1)

    return scatter_kernel(src, pos1, pos2)


def _sc_gather_rows(table, idx):
    n, width = idx.shape[0], table.shape[1]
    rows_per_worker = n // SC_WORKERS
    n_chunks = rows_per_worker // SC_CHUNK
    assert n == SC_WORKERS * SC_CHUNK * n_chunks and n_chunks % 2 == 0
    mesh = plsc.VectorSubcoreMesh(core_axis_name="c", subcore_axis_name="s")

    @functools.partial(
        pl.kernel, mesh=mesh,
        out_type=jax.ShapeDtypeStruct((n, width), table.dtype),
        scratch_types=[pltpu.VMEM((2, SC_CHUNK), jnp.int32), pltpu.VMEM((2, SC_CHUNK, width), table.dtype),
                       pltpu.SemaphoreType.DMA, pltpu.SemaphoreType.DMA,
                       pltpu.SemaphoreType.DMA, pltpu.SemaphoreType.DMA],
    )
    def gather_kernel(table_hbm, idx_hbm, out_hbm, idx_v, rows_v, g0, g1, w0, w1):
        wid = lax.axis_index("s") * SC_CORES + lax.axis_index("c")
        base = wid * rows_per_worker
        gsem = (g0, g1)
        wsem = (w0, w1)

        def gather_copy(slot):
            return pltpu.make_async_copy(table_hbm.at[idx_v.at[slot]], rows_v.at[slot], gsem[slot])

        def write_copy(c, slot):
            off = pl.multiple_of(base + c * SC_CHUNK, 8)
            return pltpu.make_async_copy(rows_v.at[slot], out_hbm.at[pl.ds(off, SC_CHUNK)], wsem[slot])

        def start_gather(c, slot):
            off = pl.multiple_of(base + c * SC_CHUNK, 8)
            pltpu.sync_copy(idx_hbm.at[pl.ds(off, SC_CHUNK)], idx_v.at[slot])
            gather_copy(slot).start()

        start_gather(0, 0)

        @pl.loop(0, n_chunks, step=2)
        def _(c):
            @pl.when(c > 0)
            def _():
                write_copy(c - 1, 1).wait()

            start_gather(c + 1, 1)
            gather_copy(0).wait()
            write_copy(c, 0).start()
            gather_copy(1).wait()
            write_copy(c + 1, 1).start()
            write_copy(c, 0).wait()

            @pl.when(c + 2 < n_chunks)
            def _():
                start_gather(c + 2, 0)

        write_copy(n_chunks - 1, 1).wait()

    return gather_kernel(table, idx)


def _moe_grouped_kernel(te_ref, nt_ref, order_ref, x_lo_ref, x_hi_ref, wg_ref, wu_ref, wd_ref, o_ref,
                        w1_bf_ref, w2_bf_ref):
    del order_ref
    i = pl.program_id(0)
    half = MOE_TILE // 2

    @pl.when(jnp.logical_or(i == 0, te_ref[i] != te_ref[jnp.maximum(i - 1, 0)]))
    def _():
        w1_bf_ref[...] = _gate_up_bf16(wg_ref, wu_ref)
        w2_bf_ref[...] = wd_ref[0].astype(BF16)

    @pl.when(i < nt_ref[0])
    def _():
        for s in range(MOE_TILE // MOE_SUBTILE):
            rows = slice(s * MOE_SUBTILE, (s + 1) * MOE_SUBTILE)
            x_ref = x_lo_ref if rows.start < half else x_hi_ref
            t_a, t_b = _unpack_bf16_pair(x_ref[rows.start % half:rows.start % half + MOE_SUBTILE, :])
            hid = _expert_hidden(t_a.astype(BF16), t_b.astype(BF16), w1_bf_ref[...])
            o_ref[rows, :] = _pack_bf16_pair(_dot(hid.astype(BF16), w2_bf_ref[...]))


def _moe_grouped(xs, tile_expert, n_tiles, order, w):
    rows = xs.shape[0]
    row = lambda i, te, nt, od: (jnp.minimum(i, nt[0] - 1), 0)
    row_lo = lambda i, te, nt, od: (2 * jnp.minimum(i, nt[0] - 1), 0)
    row_hi = lambda i, te, nt, od: (2 * jnp.minimum(i, nt[0] - 1) + 1, 0)
    by_expert = lambda i, te, nt, od: (te[i], 0, 0)
    return pl.pallas_call(
        _moe_grouped_kernel,
        out_shape=jax.ShapeDtypeStruct((rows, D_MODEL // 2), jnp.uint32),
        grid_spec=pltpu.PrefetchScalarGridSpec(
            num_scalar_prefetch=3,
            grid=(rows // MOE_TILE,),
            in_specs=[pl.BlockSpec((MOE_TILE // 2, D_MODEL // 2), row_lo),
                      pl.BlockSpec((MOE_TILE // 2, D_MODEL // 2), row_hi),
                      pl.BlockSpec((1, D_MODEL, EXPERT_FF), by_expert),
                      pl.BlockSpec((1, D_MODEL, EXPERT_FF), by_expert),
                      pl.BlockSpec((1, EXPERT_FF, D_MODEL), by_expert)],
            out_specs=pl.BlockSpec((MOE_TILE, D_MODEL // 2), row),
            scratch_shapes=[pltpu.VMEM((D_MODEL, 2 * EXPERT_FF), BF16), pltpu.VMEM((EXPERT_FF, D_MODEL), BF16)],
        ),
        compiler_params=pltpu.CompilerParams(dimension_semantics=("arbitrary",),
                                             vmem_limit_bytes=VMEM_LIMIT),
        name="moe_grouped",
    )(tile_expert, n_tiles, order, xs, xs, w["w_gate"], w["w_up"], w["w_down"])


def _moe_combine_kernel(h_ref, z1_ref, z2_ref, route_ref, y_ref):
    route = route_ref[...]
    g1, g2 = route[:, 2:3], route[:, 3:4]
    half = D_MODEL // 2
    a1, b1 = _unpack_bf16_pair(z1_ref[...])
    a2, b2 = _unpack_bf16_pair(z2_ref[...])
    y_ref[:, :half] = h_ref[:, :half] + g1 * a1 + g2 * a2
    y_ref[:, half:] = h_ref[:, half:] + g1 * b1 + g2 * b2


def _moe_combine(h, z, route, tm):
    t = h.shape[0]
    nb = t // tm
    tok = lambda i: (i, 0)
    return pl.pallas_call(
        _moe_combine_kernel,
        out_shape=jax.ShapeDtypeStruct((t, D_MODEL), F32),
        grid=(nb,),
        in_specs=[pl.BlockSpec((tm, D_MODEL), tok), pl.BlockSpec((tm, D_MODEL // 2), tok),
                  pl.BlockSpec((tm, D_MODEL // 2), lambda i: (i + nb, 0)), pl.BlockSpec((tm, LANES), tok)],
        out_specs=pl.BlockSpec((tm, D_MODEL), tok),
        compiler_params=pltpu.CompilerParams(dimension_semantics=("parallel",),
                                             vmem_limit_bytes=VMEM_LIMIT),
        name="moe_combine",
    )(h, z, z, route)


def _route_pos_kernel(routet_ref, cnt_ref, upper_ref, pos_ref):
    tm = routet_ref.shape[1]
    cnt = cnt_ref[...]
    padded = jnp.floor((cnt + float(MOE_TILE - 1)) * (1.0 / MOE_TILE)) * float(MOE_TILE)
    p_hi, p_mid, p_lo = _split3(padded)
    upper = upper_ref[...]
    starts = (_dot(p_hi, upper) + _dot(p_mid, upper) + _dot(p_lo, upper))[0:1, :]
    starts_col = jnp.broadcast_to(starts, (LANES, LANES)).T[EXPERT_LANE0:EXPERT_LANE0 + N_EXPERTS, 0:1]
    erow = lax.broadcasted_iota(jnp.int32, (N_EXPERTS, tm), 0).astype(F32)
    for k in range(2):
        e_k = routet_ref[k:k + 1, :]
        pos = (jnp.sum(jnp.where(erow == e_k, starts_col, 0.0), axis=0, keepdims=True)
               + routet_ref[4 + k:5 + k, :]).astype(jnp.int32)
        for r in range(tm // LANES):
            pos_ref[k, r:r + 1, :] = pos[:, r * LANES:(r + 1) * LANES]


def _route_positions(routet, counts, tm):
    t = routet.shape[1]
    idx = jnp.arange(LANES)
    upper = (idx[:, None] < idx[None, :]).astype(BF16)
    return pl.pallas_call(
        _route_pos_kernel,
        out_shape=jax.ShapeDtypeStruct((2, t // LANES, LANES), jnp.int32),
        grid=(t // tm,),
        in_specs=[pl.BlockSpec((8, tm), lambda i: (0, i)), pl.BlockSpec((8, LANES), lambda i: (0, 0)),
                  pl.BlockSpec((LANES, LANES), lambda i: (0, 0))],
        out_specs=pl.BlockSpec((2, tm // LANES, LANES), lambda i: (0, i, 0)),
        compiler_params=pltpu.CompilerParams(dimension_semantics=("parallel",),
                                             vmem_limit_bytes=VMEM_LIMIT),
        name="route_positions",
    )(routet, counts, upper)


def _moe_routed(h, t_packed, route, routet, counts, w, tm, run_first):
    t = h.shape[0]
    pos = _route_positions(routet, counts, min(32 * LANES, t))
    pos1 = pos[0].reshape(t)
    pos2 = pos[1].reshape(t)
    cnt = counts[0, EXPERT_LANE0:EXPERT_LANE0 + N_EXPERTS].astype(jnp.int32)
    padded = (cnt + MOE_TILE - 1) // MOE_TILE * MOE_TILE
    ends = jnp.cumsum(padded)
    n_rows = 2 * t + N_EXPERTS * MOE_TILE
    n_tiles = ends[N_EXPERTS - 1] // MOE_TILE
    tile_start = jnp.arange(n_rows // MOE_TILE, dtype=jnp.int32) * MOE_TILE
    tile_start = jnp.minimum(tile_start, ends[N_EXPERTS - 1] - MOE_TILE)
    tile_expert = jnp.sum((tile_start[:, None] >= ends[None, :]).astype(jnp.int32), axis=1)
    xs = _sc_scatter_rows(t_packed, pos1, pos2, n_rows)
    order = lax.bitcast_convert_type(run_first.reshape(-1)[:1].astype(F32), jnp.int32)
    out = _moe_grouped(xs, tile_expert, n_tiles.reshape(1), order, w)
    z = _sc_gather_rows(out, pos.reshape(2 * t))
    return _moe_combine(h, z, route, tm)


def _pad_lanes(a, width=LANES):
    return jnp.pad(a, ((0, 0), (0, width - a.shape[1])))


def _prep_weights(norm1, w_in, q_norm, k_norm, conv_w, conv_b, dt_bias, a_log, d_skip, ssm_norm, w_out,
                  norm2, w_grp, b_grp, w_exp, b_exp, w_gate, w_up, w_down):
    w = {}
    w["norm1"] = norm1.reshape(1, D_MODEL)
    w["w_in"] = _pad_lanes(w_in, XBC_END + LANES).astype(BF16)
    w["qkn"] = jnp.concatenate([jnp.tile(q_norm, N_HEADS), jnp.tile(k_norm, KV_HEADS)]).reshape(1, QK_WIDTH)
    head_of_col = jnp.arange(QK_WIDTH) // HEAD_DIM
    red = (head_of_col[:, None] == jnp.arange(LANES)[None, :])
    w["red"] = red.astype(BF16)
    w["exp"] = red.T.astype(BF16)
    w["conv_w"] = conv_w
    w["conv_b"] = conv_b.reshape(1, CONV_DIM)
    w["dt_bias"] = _pad_lanes(dt_bias.reshape(1, SSM_HEADS))
    w["a_log"] = _pad_lanes(a_log.reshape(1, SSM_HEADS))
    w["d_skip"] = jnp.repeat(d_skip, SSM_HEAD_DIM).reshape(1, SSM_WIDTH)
    w["ssm_norm"] = ssm_norm.reshape(1, SSM_WIDTH)
    idx = jnp.arange(SSD_CHUNK)
    w["tri"] = (idx[None, :] <= idx[:, None]).astype(BF16)
    lane_head = jnp.arange(SSM_WIDTH) // SSM_HEAD_DIM
    w["expand"] = (jnp.arange(LANES)[:, None] == lane_head[None, :]).astype(BF16)
    w["w_out"] = w_out.astype(BF16)
    w["norm2"] = norm2.reshape(1, D_MODEL)
    wr = jnp.zeros((D_MODEL, LANES), F32)
    wr = wr.at[:, :N_EGROUPS].set(w_grp).at[:, EXPERT_LANE0:EXPERT_LANE0 + N_EXPERTS].set(w_exp)
    wr_hi = wr.astype(BF16)
    w["wrt"] = jnp.concatenate([wr_hi, (wr - wr_hi.astype(F32)).astype(BF16)], axis=1).T
    br = jnp.zeros((LANES, 1), F32)
    w["br_col"] = br.at[:N_EGROUPS, 0].set(b_grp).at[EXPERT_LANE0:EXPERT_LANE0 + N_EXPERTS, 0].set(b_exp)
    w["w_gate"], w["w_up"], w["w_down"] = w_gate, w_up, w_down
    return w


def _rope_tables(pos):
    inv = 1.0 / (ROPE_THETA ** (jnp.arange(0, HEAD_DIM, 2, dtype=F32) / HEAD_DIM))
    ang = pos.astype(F32)[:, None] * inv[None, :]
    cos, sin = jnp.cos(ang), jnp.sin(ang)
    reps = LANES // HEAD_DIM
    return (jnp.tile(jnp.concatenate([cos, cos], axis=-1), (1, reps)),
            jnp.tile(jnp.concatenate([-sin, sin], axis=-1), (1, reps)))


def _token_tile(t):
    for tm in (1024, 512, 256, 128, 64, 32, 16):
        if t % tm == 0:
            return tm
    raise ValueError(f"token count {t} is not a multiple of 16")


def kernel(x_prompt, x_sample, cache_win_k, cache_win_v, state_conv, state_ssm, norm1, w_in, q_norm, k_norm,
           sinks, conv_w, conv_b, dt_bias, a_log, d_skip, ssm_norm, w_out, norm2, w_grp, b_grp, w_exp, b_exp,
           w_gate, w_up, w_down):
    depth = norm1.shape[0]
    assert depth == 1, "single-layer stack"
    bp, lp, _ = x_prompt.shape
    bsn, ls, _ = x_sample.shape
    assert ls == 1 and lp % WINDOW == 0 and cache_win_k.shape[2] == WINDOW
    l = 0
    w = _prep_weights(norm1[l], w_in[l], q_norm[l], k_norm[l], conv_w[l], conv_b[l], dt_bias[l], a_log[l],
                      d_skip[l], ssm_norm[l], w_out[l], norm2[l], w_grp[l], b_grp[l], w_exp[l], b_exp[l],
                      w_gate[l], w_up[l], w_down[l])
    sink = sinks[l]

    tp = bp * lp
    xp = x_prompt.reshape(tp, D_MODEL)
    tm_p = _token_tile(lp)
    cos_p, sin_p = _rope_tables(jnp.arange(lp, dtype=jnp.int32))
    q, k, v, z, xbc, dt, k2, v2 = _inproj(xp, w, cos_p, sin_p, tm_p, lp // tm_p)
    attn = _attn_prompt(q, k2, v2, sink, bp, lp)
    ssm, st_p = _ssd_prompt(xbc, z, dt, w, bp, lp)
    h, t, route, routet, counts = _outproj_router(xp, attn, ssm, w, tm_p)
    k3 = k.reshape(bp, lp, KV_HEADS, HEAD_DIM)
    v3 = v.reshape(bp, lp, KV_HEADS, HEAD_DIM)
    win_k_p = k3[:, lp - WINDOW:][None]
    win_v_p = v3[:, lp - WINDOW:][None]
    conv_p = xbc.reshape(bp, lp, CONV_DIM)[:, lp - (CONV_K - 1):][None]
    ssm_p = st_p.reshape(1, bp, SSM_HEADS, SSM_HEAD_DIM, SSM_STATE)

    xs2 = x_sample.reshape(bsn, D_MODEL)
    tm_s = _token_tile(bsn)
    cos_s, sin_s = _rope_tables(jnp.full((tm_s,), PAST_LEN, jnp.int32))
    q_s, k_s, v_s, z_s, xbc_s, dt_s, _, _ = _inproj(xs2, w, cos_s, sin_s, tm_s, 1)
    q4 = q_s.reshape(bsn, KV_HEADS, N_HEADS // KV_HEADS, HEAD_DIM)
    zq = jnp.zeros_like(q4[:, 0])
    qx = jnp.concatenate([jnp.concatenate([q4[:, 0], zq], axis=-1),
                          jnp.concatenate([zq, q4[:, 1]], axis=-1)], axis=1)
    qx = jnp.pad(qx, ((0, 0), (0, BF16_ROWS - N_HEADS), (0, 0)))
    sink_x = jnp.pad(jnp.broadcast_to(sink[:, None], (N_HEADS, LANES)), ((0, BF16_ROWS - N_HEADS), (0, 0)))
    kc = jnp.transpose(cache_win_k[l], (0, 2, 3, 1)).reshape(bsn, KV_WIDTH, WINDOW)
    vc = jnp.transpose(cache_win_v[l], (0, 2, 3, 1)).reshape(bsn, KV_WIDTH, WINDOW)
    ko, vo, attn_s = _attn_sample(qx, kc, k_s.reshape(bsn, KV_WIDTH, 1), vc, v_s.reshape(bsn, KV_WIDTH, 1),
                               sink_x, 8)
    y_prompt = _moe_routed(h, t, route, routet, counts, w, tm_p, attn_s).reshape(bp, lp, D_MODEL)
    cprev_t = jnp.transpose(state_conv[l], (1, 0, 2))
    h0 = state_ssm[l].reshape(bsn, HEAD_PAIRS, LANES, SSM_STATE)
    ssm_s, cnew_t, h1 = _ssd_sample(xbc_s, z_s, dt_s, cprev_t, h0, w, 16)
    h_s, t_s, route_s, _, _ = _outproj_router(xs2, attn_s, ssm_s, w, tm_s)
    y_sample = _moe_dense(h_s, t_s, route_s, w).reshape(bsn, 1, D_MODEL)
    win_k_s = jnp.transpose(ko.reshape(bsn, KV_HEADS, HEAD_DIM, WINDOW), (0, 3, 1, 2))[None]
    win_v_s = jnp.transpose(vo.reshape(bsn, KV_HEADS, HEAD_DIM, WINDOW), (0, 3, 1, 2))[None]
    conv_s = jnp.transpose(cnew_t, (1, 0, 2))[None]
    ssm_s_state = h1.reshape(1, bsn, SSM_HEADS, SSM_HEAD_DIM, SSM_STATE)

    return (y_prompt, y_sample, win_k_p, win_v_p, conv_p, ssm_p, win_k_s, win_v_s, conv_s, ssm_s_state)
```

```python
import functools
import math

import jax
import jax.numpy as jnp
from jax import lax
from jax.experimental import pallas as pl
from jax.experimental.pallas import tpu as pltpu
from jax.experimental.pallas import tpu_sc as plsc

F32 = jnp.float32
BF16 = jnp.bfloat16

D_MODEL = 1024
HEAD_DIM = 64
N_HEADS = 8
KV_HEADS = 2
WINDOW = 128
ATTN_WIDTH = N_HEADS * HEAD_DIM
QK_WIDTH = ATTN_WIDTH + KV_HEADS * HEAD_DIM
KV_WIDTH = KV_HEADS * HEAD_DIM
ATTN_SCALE = HEAD_DIM ** -0.5
ROPE_THETA = 10000.0
SSM_WIDTH = 512
SSM_HEADS = 8
SSM_HEAD_DIM = 64
SSM_GROUPS = 2
SSM_STATE = 128
CONV_K = 4
CONV_HALO = 8
CONV_DIM = SSM_WIDTH + 2 * SSM_GROUPS * SSM_STATE
SSD_CHUNK = 128
N_EGROUPS = 4
EXP_PER_GROUP = 8
N_EXPERTS = 32
EXPERT_FF = 128
EPS = 1e-6
PAST_LEN = 16384

LANES = 128
BF16_ROWS = 16
HEAD_PAIRS = SSM_HEADS // 2
EXPERT_LANE0 = 32
VMEM_LIMIT = 56 * 1024 * 1024
MOE_TILE = 1024
MOE_SUBTILE = 128
ATTN_QBLOCKS = 8
SSD_CHUNKS_PER_STEP = 8
COUNT_BLOCK = 256
X_SLOTS = 3
ROUTER_SUBTILE = 1024
INPROJ_SUBTILE = 512
SC_CORES = 2
SC_SUBCORES = 16
SC_WORKERS = SC_CORES * SC_SUBCORES
SC_CHUNK = 64

Q_END = ATTN_WIDTH
K_END = Q_END + KV_WIDTH
V_END = K_END + KV_WIDTH
Z_END = V_END + SSM_WIDTH
XBC_END = Z_END + CONV_DIM


def _dot(a, b):
    return jnp.dot(a, b, preferred_element_type=F32)


def _dot_nt(a, b):
    return lax.dot_general(a, b, (((1,), (1,)), ((), ())), preferred_element_type=F32)


def _split2(v):
    hi = v.astype(BF16)
    lo = (v - hi.astype(F32)).astype(BF16)
    return hi, lo


def _split3(v):
    hi = v.astype(BF16)
    r = v - hi.astype(F32)
    mid = r.astype(BF16)
    lo = (r - mid.astype(F32)).astype(BF16)
    return hi, mid, lo


def _silu(x):
    return x * jax.nn.sigmoid(x)


def _softplus(x):
    return jnp.maximum(x, 0.0) + jnp.log1p(jnp.exp(-jnp.abs(x)))


def _lane_bcast_pairs(v, n_pairs):
    r = v.shape[0]
    lo = lax.broadcasted_iota(jnp.int32, (r, LANES), 1) < HEAD_DIM
    slabs = []
    for j in range(n_pairs):
        a = jnp.broadcast_to(v[:, 2 * j:2 * j + 1], (r, LANES))
        b = jnp.broadcast_to(v[:, 2 * j + 1:2 * j + 2], (r, LANES))
        slabs.append(jnp.where(lo, a, b))
    return jnp.concatenate(slabs, axis=1)


def _causal_conv_silu(x_ext, convw_ref, convb_ref):
    halo = CONV_HALO
    x_raw = x_ext[halo:, :]
    conv = convb_ref[...] + x_raw * convw_ref[CONV_K - 1:CONV_K, :]
    for j in range(CONV_K - 1):
        shifted = pltpu.roll(x_ext, CONV_K - 1 - j, axis=0)[halo:, :]
        conv = conv + shifted * convw_ref[j:j + 1, :]
    return _silu(conv)


def _inproj_kernel(x_ref, n1_ref, win_ref, dtb_ref, qkn_ref,
                   cos_ref, sin_ref, red_ref, exp_ref,
                   q_ref, k_ref, v_ref, z_ref, xbc_ref, dt_ref, k2_ref, v2_ref):
    tm = x_ref.shape[0]
    sub = min(tm, INPROJ_SUBTILE)
    lane = lax.broadcasted_iota(jnp.int32, (sub, LANES), 1)
    first_half = (lane % HEAD_DIM) < (HEAD_DIM // 2)
    for s in range(tm // sub):
        rows = slice(s * sub, (s + 1) * sub)
        x = x_ref[rows, :]
        ms = jnp.mean(x * x, axis=-1, keepdims=True)
        xn = (x * lax.rsqrt(ms + EPS) * n1_ref[...]).astype(BF16)
        v = _dot(xn, win_ref[:, K_END:V_END])
        v_ref[rows, :] = v
        v2_ref[rows, :] = _pair_operands(v)
        z_ref[rows, :] = _dot(xn, win_ref[:, V_END:Z_END])
        xbc_ref[rows, :] = _dot(xn, win_ref[:, Z_END:XBC_END])
        dt_ref[rows, :] = _softplus(_dot(xn, win_ref[:, XBC_END:]) + dtb_ref[...])
        qk = _dot(xn, win_ref[:, :K_END])
        ss = _dot((qk * qk).astype(BF16), red_ref[...])
        inv = lax.rsqrt(ss * (1.0 / HEAD_DIM) + EPS)
        inv_hi, inv_lo = _split2(inv)
        inv_x = _dot(inv_hi, exp_ref[...]) + _dot(inv_lo, exp_ref[...])
        qkn = qk * inv_x * qkn_ref[...]
        cos = cos_ref[rows, :]
        sin = sin_ref[rows, :]
        for c in range(QK_WIDTH // LANES):
            xc = qkn[:, c * LANES:(c + 1) * LANES]
            partner = jnp.where(first_half,
                                pltpu.roll(xc, LANES - HEAD_DIM // 2, axis=1),
                                pltpu.roll(xc, HEAD_DIM // 2, axis=1))
            rot = xc * cos + partner * sin
            if c < ATTN_WIDTH // LANES:
                q_ref[rows, c * LANES:(c + 1) * LANES] = (rot * ATTN_SCALE).astype(BF16)
            else:
                k_ref[rows, :] = rot
                k2_ref[rows, :] = _pair_operands(rot)


def _inproj(x2d, w, cos_tab, sin_tab, tm, n_pos_blocks):
    t = x2d.shape[0]
    grid = (t // tm,)
    tok = lambda i: (i, 0)
    const = lambda i: (0, 0)
    pos = lambda i: (i % n_pos_blocks, 0)
    full = lambda a: pl.BlockSpec(a.shape, const)
    rows = lambda width, dtype: (jax.ShapeDtypeStruct((t, width), dtype), pl.BlockSpec((tm, width), tok))
    outs = [rows(ATTN_WIDTH, BF16), rows(KV_WIDTH, F32), rows(KV_WIDTH, F32), rows(SSM_WIDTH, F32)]
    operands = [x2d, w["norm1"], w["w_in"], w["dt_bias"], w["qkn"],
                cos_tab, sin_tab, w["red"], w["exp"]]
    in_specs = [
        pl.BlockSpec((tm, D_MODEL), tok),
        full(w["norm1"]), full(w["w_in"]), full(w["dt_bias"]), full(w["qkn"]),
        pl.BlockSpec((tm, LANES), pos), pl.BlockSpec((tm, LANES), pos),
        full(w["red"]), full(w["exp"]),
    ]
    outs += [rows(CONV_DIM, F32), rows(LANES, F32), rows(4 * LANES, BF16), rows(4 * LANES, BF16)]
    return pl.pallas_call(
        _inproj_kernel,
        out_shape=tuple(o[0] for o in outs),
        grid=grid,
        in_specs=in_specs,
        out_specs=tuple(o[1] for o in outs),
        compiler_params=pltpu.CompilerParams(dimension_semantics=("parallel",),
                                             vmem_limit_bytes=VMEM_LIMIT),
        name="inproj",
    )(*operands)


def _pair_operands(kv):
    lo = lax.broadcasted_iota(jnp.int32, kv.shape, 1) < HEAD_DIM
    swapped = pltpu.roll(kv, HEAD_DIM, axis=1)
    parts = [jnp.where(lo, kv, 0.0), jnp.where(lo, 0.0, swapped), jnp.where(lo, swapped, 0.0), jnp.where(lo, 0.0, kv)]
    return jnp.concatenate(parts, axis=1).astype(BF16)


def _pair_rhs(blk, g):
    return jnp.concatenate([blk[:, 2 * g * LANES:(2 * g + 1) * LANES],
                            blk[:, (2 * g + 1) * LANES:(2 * g + 2) * LANES]], axis=0)


def _attn_qblock(sink_ref, q_blk, k_prev, k_cur, v_prev, v_cur, seq_start, o_ref):
    blk = WINDOW
    qi = lax.broadcasted_iota(jnp.int32, (blk, 2 * blk), 0)
    kj = lax.broadcasted_iota(jnp.int32, (blk, 2 * blk), 1) % blk
    cur_ok = kj <= qi
    lo = lax.broadcasted_iota(jnp.int32, (blk, LANES), 1) < HEAD_DIM
    n_pairs = N_HEADS // KV_HEADS // 2
    for g in range(KV_HEADS):
        q_all = jnp.concatenate([q_blk[:, (g * n_pairs + r) * LANES:(g * n_pairs + r + 1) * LANES]
                                 for r in range(n_pairs)], axis=0)
        s_all = _dot_nt(q_all, jnp.concatenate([_pair_rhs(k_cur, g), _pair_rhs(k_prev, g)], axis=0))
        p_rows = []
        den_rows = []
        for r in range(n_pairs):
            pair = g * n_pairs + r
            s_cur = s_all[r * blk:(r + 1) * blk, :2 * blk]
            s_prev = s_all[r * blk:(r + 1) * blk, 2 * blk:]
            s = jnp.where(cur_ok, s_cur, s_prev)
            if seq_start is not None:
                s = jnp.where(jnp.logical_or(cur_ok, jnp.logical_not(seq_start)), s, -jnp.inf)
            ps = []
            dens = []
            for hh in range(2):
                sink = sink_ref[2 * pair + hh]
                sh = s[:, hh * blk:(hh + 1) * blk]
                m = jnp.maximum(jnp.max(sh, axis=-1, keepdims=True), sink)
                p = jnp.exp(sh - m)
                dens.append(jnp.sum(p, axis=-1, keepdims=True) + jnp.exp(sink - m))
                ps.append(p)
            p2 = jnp.concatenate(ps, axis=1)
            p_rows.append(jnp.concatenate([jnp.where(cur_ok, p2, 0.0), jnp.where(cur_ok, 0.0, p2)],
                                          axis=1).astype(BF16))
            den_rows.append(jnp.where(lo, dens[0], dens[1]))
        o_all = _dot(jnp.concatenate(p_rows, axis=0),
                     jnp.concatenate([_pair_rhs(v_cur, g), _pair_rhs(v_prev, g)], axis=0))
        for r in range(n_pairs):
            pair = g * n_pairs + r
            o2 = o_all[r * blk:(r + 1) * blk, :]
            o_ref[:, pair * LANES:(pair + 1) * LANES] = (o2 / den_rows[r]).astype(BF16)


def _attn_kernel(sink_ref, q_ref, kc_ref, kp_ref, vc_ref, vp_ref, o_ref):
    blk = WINDOW
    first_step = pl.program_id(1) == 0
    for u in range(q_ref.shape[0] // blk):
        rows = slice(u * blk, (u + 1) * blk)
        prev_rows = slice((u - 1) * blk, u * blk)
        k_prev = kp_ref[...] if u == 0 else kc_ref[prev_rows, :]
        v_prev = vp_ref[...] if u == 0 else vc_ref[prev_rows, :]
        _attn_qblock(sink_ref, q_ref[rows, :], k_prev, kc_ref[rows, :], v_prev, vc_ref[rows, :],
                     first_step if u == 0 else None, o_ref.at[rows, :])


def _attn_prompt(q, k, v, sinks, batch, seq):
    n_sub = ATTN_QBLOCKS if seq % (ATTN_QBLOCKS * WINDOW) == 0 else 1
    rows = n_sub * WINDOW
    nb = seq // rows
    cur = lambda b, j, s: (b * nb + j, 0)
    prev = lambda b, j, s: (jnp.maximum((b * nb + j) * n_sub - 1, 0), 0)
    return pl.pallas_call(
        _attn_kernel,
        out_shape=jax.ShapeDtypeStruct((batch * seq, ATTN_WIDTH), BF16),
        grid_spec=pltpu.PrefetchScalarGridSpec(
            num_scalar_prefetch=1,
            grid=(batch, nb),
            in_specs=[
                pl.BlockSpec((rows, ATTN_WIDTH), cur),
                pl.BlockSpec((rows, 4 * LANES), cur), pl.BlockSpec((WINDOW, 4 * LANES), prev),
                pl.BlockSpec((rows, 4 * LANES), cur), pl.BlockSpec((WINDOW, 4 * LANES), prev),
            ],
            out_specs=pl.BlockSpec((rows, ATTN_WIDTH), cur),
        ),
        compiler_params=pltpu.CompilerParams(dimension_semantics=("parallel", "parallel"),
                                             vmem_limit_bytes=VMEM_LIMIT),
        name="attn_prompt",
    )(sinks, q, k, k, v, v)


def _ssd_kernel(xbc_ref, z_ref, dt_ref, convw_ref, convb_ref, alog_ref, dskip_ref, nw_ref,
                tri_ref, expand_ref, y_ref, st_ref, buf_ref, state_ref):
    c = pl.program_id(1)
    cl = SSD_CHUNK
    n_sub = xbc_ref.shape[0] // cl
    halo = CONV_HALO

    @pl.when(c == 0)
    def _():
        buf_ref[...] = jnp.zeros(buf_ref.shape, F32)
        state_ref[...] = jnp.zeros(state_ref.shape, F32)

    lane = lax.broadcasted_iota(jnp.int32, (1, LANES), 1)
    a_neg = jnp.where(lane < SSM_HEADS, -jnp.exp(alog_ref[...]), 0.0)
    tri = tri_ref[...]
    for u in range(n_sub):
        rows = slice(u * cl, (u + 1) * cl)
        if u == 0:
            x_ext = jnp.concatenate([buf_ref[...], xbc_ref[rows, :]], axis=0)
        else:
            x_ext = xbc_ref[u * cl - halo:(u + 1) * cl, :]
        _ssd_chunk(x_ext, z_ref[rows, :], dt_ref[rows, :], a_neg, tri, convw_ref, convb_ref, dskip_ref, nw_ref,
                   expand_ref, y_ref.at[rows, :], state_ref)
    buf_ref[...] = xbc_ref[n_sub * cl - halo:n_sub * cl, :]

    @pl.when(c == pl.num_programs(1) - 1)
    def _():
        st_ref[0] = state_ref[...]


def _ssd_chunk(x_ext, z, dt, a_neg, tri, convw_ref, convb_ref, dskip_ref, nw_ref, expand_ref, y_ref, state_ref):
    cl = SSD_CHUNK
    act = _causal_conv_silu(x_ext, convw_ref, convb_ref)
    xs = act[:, :SSM_WIDTH]
    bm = act[:, SSM_WIDTH:SSM_WIDTH + SSM_GROUPS * SSM_STATE].astype(BF16)
    cm = act[:, SSM_WIDTH + SSM_GROUPS * SSM_STATE:].astype(BF16)

    dta = dt * a_neg
    p_hi, p_mid, p_lo = _split3(dta)
    a_col = _dot(tri, p_hi) + _dot(tri, p_mid) + _dot(tri, p_lo)
    a_last = a_col[cl - 1:cl, :]
    a_row = a_col.T
    per_head = jnp.concatenate([dt, jnp.exp(a_col), jnp.exp(a_last - a_col)], axis=0)
    ph_hi, ph_lo = _split2(per_head)
    per_lane = _dot(ph_hi, expand_ref[...]) + _dot(ph_lo, expand_ref[...])
    dt_x = per_lane[:cl]
    ecol_x = per_lane[cl:2 * cl]
    dte_x = per_lane[2 * cl:]
    e_last = jnp.exp(a_last)
    xdt = xs * dt_x

    li = lax.broadcasted_iota(jnp.int32, (cl, cl), 0)
    si = lax.broadcasted_iota(jnp.int32, (cl, cl), 1)
    causal = si <= li
    lo = lax.broadcasted_iota(jnp.int32, (cl, LANES), 1) < SSM_HEAD_DIM
    row_lo = lax.broadcasted_iota(jnp.int32, (LANES, SSM_STATE), 0) < SSM_HEAD_DIM

    ys = []
    for g in range(SSM_GROUPS):
        b_g = bm[:, g * SSM_STATE:(g + 1) * SSM_STATE]
        c_g = cm[:, g * SSM_STATE:(g + 1) * SSM_STATE]
        cb = _dot_nt(c_g, b_g)
        for r in range(HEAD_PAIRS // SSM_GROUPS):
            j = g * (HEAD_PAIRS // SSM_GROUPS) + r
            sl = slice(j * LANES, (j + 1) * LANES)
            xdt_p = xdt[:, sl]
            ms = []
            for hh in range(2):
                h = 2 * j + hh
                seg = a_col[:, h:h + 1] - a_row[h:h + 1, :]
                ms.append(cb * jnp.exp(jnp.where(causal, seg, -jnp.inf)))
            m2 = jnp.concatenate(ms, axis=1).astype(BF16)
            rhs = jnp.concatenate([jnp.where(lo, xdt_p, 0.0), jnp.where(lo, 0.0, xdt_p)],
                                  axis=0).astype(BF16)
            y_diag = _dot(m2, rhs)
            st = state_ref[j]
            y_off = _dot_nt(c_g, st.astype(BF16)) * ecol_x[:, sl]
            xdt_e = (xdt_p * dte_x[:, sl]).T.astype(BF16)
            d_a = e_last[:, 2 * j:2 * j + 1]
            d_b = e_last[:, 2 * j + 1:2 * j + 2]
            decay = jnp.where(row_lo, jnp.broadcast_to(d_a, row_lo.shape), jnp.broadcast_to(d_b, row_lo.shape))
            state_ref[j] = decay * st + _dot(xdt_e, b_g)
            ys.append(y_diag + y_off + dskip_ref[:, sl] * xs[:, sl])
    y = jnp.concatenate(ys, axis=1)
    gated = y * _silu(z)
    gw = SSM_WIDTH // SSM_GROUPS
    outs = []
    for g in range(SSM_GROUPS):
        gg = gated[:, g * gw:(g + 1) * gw]
        outs.append(gg * lax.rsqrt(jnp.mean(gg * gg, axis=-1, keepdims=True) + EPS))
    y_ref[...] = (jnp.concatenate(outs, axis=1) * nw_ref[...]).astype(BF16)


def _ssd_prompt(xbc, z, dt, w, batch, seq):
    n_sub = SSD_CHUNKS_PER_STEP if seq % (SSD_CHUNKS_PER_STEP * SSD_CHUNK) == 0 else 1
    rows = n_sub * SSD_CHUNK
    nc = seq // rows
    tok = lambda b, c: (b * nc + c, 0)
    const = lambda b, c: (0, 0)
    full = lambda a: pl.BlockSpec(a.shape, const)
    return pl.pallas_call(
        _ssd_kernel,
        out_shape=(jax.ShapeDtypeStruct((batch * seq, SSM_WIDTH), BF16),
                   jax.ShapeDtypeStruct((batch, HEAD_PAIRS, LANES, SSM_STATE), F32)),
        grid=(batch, nc),
        in_specs=[
            pl.BlockSpec((rows, CONV_DIM), tok), pl.BlockSpec((rows, SSM_WIDTH), tok),
            pl.BlockSpec((rows, LANES), tok),
            full(w["conv_w"]), full(w["conv_b"]), full(w["a_log"]),
            full(w["d_skip"]), full(w["ssm_norm"]), full(w["tri"]), full(w["expand"]),
        ],
        out_specs=(pl.BlockSpec((rows, SSM_WIDTH), tok),
                   pl.BlockSpec((1, HEAD_PAIRS, LANES, SSM_STATE), lambda b, c: (b, 0, 0, 0))),
        scratch_shapes=[pltpu.VMEM((CONV_HALO, CONV_DIM), F32),
                        pltpu.VMEM((HEAD_PAIRS, LANES, SSM_STATE), F32)],
        compiler_params=pltpu.CompilerParams(dimension_semantics=("parallel", "arbitrary"),
                                             vmem_limit_bytes=VMEM_LIMIT),
        name="ssd_prompt",
    )(xbc, z, dt, w["conv_w"], w["conv_b"], w["a_log"], w["d_skip"], w["ssm_norm"], w["tri"], w["expand"])


def _attn_sample_kernel(qx_ref, kc_ref, kn_ref, vc_ref, vn_ref, sink_ref, ko_ref, vo_ref, o_ref):
    bs = qx_ref.shape[0]
    w = kc_ref.shape[2]
    sink = sink_ref[...]
    lo = lax.broadcasted_iota(jnp.int32, (1, LANES), 1) < HEAD_DIM
    newest = lax.broadcasted_iota(jnp.int32, (1, w), 1) == w - 1
    for i in range(bs):
        k_t = jnp.where(newest, kn_ref[i], pltpu.roll(kc_ref[i], w - 1, axis=1))
        v_t = jnp.where(newest, vn_ref[i], pltpu.roll(vc_ref[i], w - 1, axis=1))
        ko_ref[i] = k_t
        vo_ref[i] = v_t
        s = _dot(qx_ref[i], k_t.astype(BF16))
        m = jnp.maximum(jnp.max(s, axis=-1, keepdims=True), sink)
        p = jnp.exp(s - m)
        den = jnp.sum(p, axis=-1, keepdims=True) + jnp.exp(sink - m)
        o = _dot_nt(p.astype(BF16), v_t.astype(BF16)) / den
        o_sw = pltpu.roll(o, HEAD_DIM, axis=1)
        for j in range(N_HEADS // 2):
            a, b = (o, o_sw) if j < N_HEADS // 4 else (o_sw, o)
            o_ref[i:i + 1, j * LANES:(j + 1) * LANES] = jnp.where(lo, a[2 * j:2 * j + 1], b[2 * j + 1:2 * j + 2])


def _attn_sample(qx, kc, kn, vc, vn, sink_x, bs):
    n, w = kc.shape[0], kc.shape[2]
    blk3 = lambda i: (i, 0, 0)
    return pl.pallas_call(
        _attn_sample_kernel,
        out_shape=(jax.ShapeDtypeStruct((n, KV_WIDTH, w), F32),
                   jax.ShapeDtypeStruct((n, KV_WIDTH, w), F32),
                   jax.ShapeDtypeStruct((n, ATTN_WIDTH), F32)),
        grid=(n // bs,),
        in_specs=[
            pl.BlockSpec((bs, BF16_ROWS, LANES), blk3),
            pl.BlockSpec((bs, KV_WIDTH, w), blk3), pl.BlockSpec((bs, KV_WIDTH, 1), blk3),
            pl.BlockSpec((bs, KV_WIDTH, w), blk3), pl.BlockSpec((bs, KV_WIDTH, 1), blk3),
            pl.BlockSpec(sink_x.shape, lambda i: (0, 0)),
        ],
        out_specs=(pl.BlockSpec((bs, KV_WIDTH, w), blk3), pl.BlockSpec((bs, KV_WIDTH, w), blk3),
                   pl.BlockSpec((bs, ATTN_WIDTH), lambda i: (i, 0))),
        compiler_params=pltpu.CompilerParams(dimension_semantics=("parallel",),
                                             vmem_limit_bytes=VMEM_LIMIT),
        name="attn_sample",
    )(qx, kc, kn, vc, vn, sink_x)


def _ssd_sample_kernel(xbc_ref, z_ref, dt_ref, cprev_ref, h0_ref, convw_ref, convb_ref, alog_ref,
                       dskip_ref, nw_ref, y_ref, cnew_ref, h1_ref):
    bs = xbc_ref.shape[0]
    x_raw = xbc_ref[...]
    conv = convb_ref[...] + x_raw * convw_ref[CONV_K - 1:CONV_K, :]
    for j in range(CONV_K - 1):
        conv = conv + cprev_ref[j] * convw_ref[j:j + 1, :]
    for j in range(CONV_K - 2):
        cnew_ref[j] = cprev_ref[j + 1]
    cnew_ref[CONV_K - 2] = x_raw
    act = _silu(conv)
    xs = act[:, :SSM_WIDTH]
    bm = act[:, SSM_WIDTH:SSM_WIDTH + SSM_GROUPS * SSM_STATE].astype(BF16)
    cm = act[:, SSM_WIDTH + SSM_GROUPS * SSM_STATE:].astype(BF16)
    lane = lax.broadcasted_iota(jnp.int32, (1, LANES), 1)
    a_neg = jnp.where(lane < SSM_HEADS, -jnp.exp(alog_ref[...]), 0.0)
    dt = dt_ref[...]
    dec = jnp.exp(dt * a_neg)
    xdt = xs * _lane_bcast_pairs(dt, HEAD_PAIRS)
    rowid = lax.broadcasted_iota(jnp.int32, (bs, LANES), 0)
    row_lo = lax.broadcasted_iota(jnp.int32, (LANES, SSM_STATE), 0) < SSM_HEAD_DIM
    ys = []
    for j in range(HEAD_PAIRS):
        g = j // (HEAD_PAIRS // SSM_GROUPS)
        sl = slice(j * LANES, (j + 1) * LANES)
        b_g = bm[:, g * SSM_STATE:(g + 1) * SSM_STATE]
        c_g = cm[:, g * SSM_STATE:(g + 1) * SSM_STATE]
        xdt_p = xdt[:, sl]
        y_p = jnp.zeros((bs, LANES), F32)
        for i in range(bs):
            xi = jnp.where(rowid == i, xdt_p, 0.0).T.astype(BF16)
            d_a = dec[i:i + 1, 2 * j:2 * j + 1]
            d_b = dec[i:i + 1, 2 * j + 1:2 * j + 2]
            decay = jnp.where(row_lo, jnp.broadcast_to(d_a, row_lo.shape), jnp.broadcast_to(d_b, row_lo.shape))
            new = decay * h0_ref[i, j] + _dot(xi, b_g)
            h1_ref[i, j] = new
            y_p = y_p + jnp.where(rowid == i, _dot_nt(c_g, new.astype(BF16)), 0.0)
        ys.append(y_p + dskip_ref[:, sl] * xs[:, sl])
    y = jnp.concatenate(ys, axis=1)
    gated = y * _silu(z_ref[...])
    gw = SSM_WIDTH // SSM_GROUPS
    outs = []
    for g in range(SSM_GROUPS):
        gg = gated[:, g * gw:(g + 1) * gw]
        outs.append(gg * lax.rsqrt(jnp.mean(gg * gg, axis=-1, keepdims=True) + EPS))
    y_ref[...] = (jnp.concatenate(outs, axis=1) * nw_ref[...]).astype(BF16)


def _ssd_sample(xbc, z, dt, cprev_t, h0, w, bs):
    n = xbc.shape[0]
    tok = lambda i: (i, 0)
    const = lambda i: (0, 0)
    full = lambda a: pl.BlockSpec(a.shape, const)
    return pl.pallas_call(
        _ssd_sample_kernel,
        out_shape=(jax.ShapeDtypeStruct((n, SSM_WIDTH), BF16),
                   jax.ShapeDtypeStruct((CONV_K - 1, n, CONV_DIM), F32),
                   jax.ShapeDtypeStruct((n, HEAD_PAIRS, LANES, SSM_STATE), F32)),
        grid=(n // bs,),
        in_specs=[
            pl.BlockSpec((bs, CONV_DIM), tok), pl.BlockSpec((bs, SSM_WIDTH), tok),
            pl.BlockSpec((bs, LANES), tok),
            pl.BlockSpec((CONV_K - 1, bs, CONV_DIM), lambda i: (0, i, 0)),
            pl.BlockSpec((bs, HEAD_PAIRS, LANES, SSM_STATE), lambda i: (i, 0, 0, 0)),
            full(w["conv_w"]), full(w["conv_b"]), full(w["a_log"]),
            full(w["d_skip"]), full(w["ssm_norm"]),
        ],
        out_specs=(pl.BlockSpec((bs, SSM_WIDTH), tok),
                   pl.BlockSpec((CONV_K - 1, bs, CONV_DIM), lambda i: (0, i, 0)),
                   pl.BlockSpec((bs, HEAD_PAIRS, LANES, SSM_STATE), lambda i: (i, 0, 0, 0))),
        compiler_params=pltpu.CompilerParams(dimension_semantics=("parallel",),
                                             vmem_limit_bytes=VMEM_LIMIT),
        name="ssd_sample",
    )(xbc, z, dt, cprev_t, h0, w["conv_w"], w["conv_b"], w["a_log"], w["d_skip"], w["ssm_norm"])


def _pack_bf16_pair(v):
    c = v.shape[1] // 2
    hi = lax.bitcast_convert_type(v[:, :c].astype(BF16).astype(F32), jnp.uint32)
    lo = lax.bitcast_convert_type(v[:, c:].astype(BF16).astype(F32), jnp.uint32)
    return hi | (lo >> 16)


def _unpack_bf16_pair(word):
    a = lax.bitcast_convert_type(word & jnp.uint32(0xFFFF0000), F32)
    b = lax.bitcast_convert_type(word << 16, F32)
    return a, b


def _outproj_router_kernel(x_hbm, a_ref, s_ref, wo_ref, n2_ref, wr_ref, br_ref, tri_ref,
                           h_ref, t_ref, route_ref, routet_ref, cnt_ref, carry_ref, xbuf_ref, xsem_ref):
    step = pl.program_id(0)
    n_steps = pl.num_programs(0)
    tm = a_ref.shape[0]

    def x_copy(st):
        slot = lax.rem(st, X_SLOTS)
        return pltpu.make_async_copy(x_hbm.at[pl.ds(pl.multiple_of(st * tm, 8), tm)], xbuf_ref.at[slot],
                                     xsem_ref.at[slot])

    @pl.when(step == 0)
    def _():
        carry_ref[...] = jnp.zeros(carry_ref.shape, F32)
        for st in range(X_SLOTS - 1):
            pl.when(st < n_steps)(lambda st=st: x_copy(st).start())

    @pl.when(step + (X_SLOTS - 1) < n_steps)
    def _():
        x_copy(step + (X_SLOTS - 1)).start()

    x_copy(step).wait()
    x_ref = xbuf_ref.at[lax.rem(step, X_SLOTS)]
    h_ref[...] = (x_ref[...] + _dot(a_ref[...].astype(BF16), wo_ref[:ATTN_WIDTH, :])
                  + _dot(s_ref[...].astype(BF16), wo_ref[ATTN_WIDTH:, :]))
    sub = min(tm, ROUTER_SUBTILE)
    carry = carry_ref[:, 0:1]
    for s in range(tm // sub):
        rows = slice(s * sub, (s + 1) * sub)
        carry = _route_rows(h_ref[rows, :], n2_ref, wr_ref, br_ref, tri_ref, carry, t_ref.at[rows, :],
                            route_ref.at[rows, :], routet_ref.at[:, rows])
    carry_ref[...] = jnp.broadcast_to(carry, carry_ref.shape)
    cpad = jnp.concatenate([jnp.zeros((EXPERT_LANE0, LANES), F32), jnp.broadcast_to(carry, (N_EXPERTS, LANES)),
                            jnp.zeros((LANES - EXPERT_LANE0 - N_EXPERTS, LANES), F32)], axis=0)
    cnt_ref[...] = cpad.T[0:cnt_ref.shape[0], :]


def _route_rows(h, n2_ref, wrt_ref, brc_ref, triu_ref, carry, t_ref, route_ref, routet_ref):
    ms = jnp.mean(h * h, axis=-1, keepdims=True)
    t = h * lax.rsqrt(ms + EPS) * n2_ref[...]
    t_ref[...] = _pack_bf16_pair(t)
    n = h.shape[0]
    a = _dot_nt(wrt_ref[...], t.astype(BF16))

    def logit_rows(r0, r1):
        return a[r0:r1] + a[LANES + r0:LANES + r1] + brc_ref[r0:r1, :]

    sl = EXP_PER_GROUP
    glog = logit_rows(0, sl)
    elog = logit_rows(EXPERT_LANE0, EXPERT_LANE0 + N_EXPERTS)
    row = lax.broadcasted_iota(jnp.int32, (sl, n), 0).astype(F32)
    big = float(sl)
    ninf = -jnp.inf
    gm = jnp.where(row < N_EGROUPS, glog, ninf)
    gmax = jnp.max(gm, axis=0, keepdims=True)
    g_top = 1.0 / jnp.sum(jnp.exp(gm - gmax), axis=0, keepdims=True)
    g_idx = jnp.min(jnp.where(gm == gmax, row, big), axis=0, keepdims=True)
    ml = elog[(N_EGROUPS - 1) * sl:]
    for g in range(N_EGROUPS - 2, -1, -1):
        ml = jnp.where(g_idx == float(g), elog[g * sl:(g + 1) * sl], ml)
    m1 = jnp.max(ml, axis=0, keepdims=True)
    i1 = jnp.min(jnp.where(ml == m1, row, big), axis=0, keepdims=True)
    ml2 = jnp.where(row == i1, ninf, ml)
    m2 = jnp.max(ml2, axis=0, keepdims=True)
    i2 = jnp.min(jnp.where(ml2 == m2, row, big), axis=0, keepdims=True)
    r = jnp.exp(m2 - m1)
    w1 = g_top / (1.0 + r)
    w2 = g_top * r / (1.0 + r)
    e1 = g_idx * float(sl) + i1
    e2 = g_idx * float(sl) + i2
    pick = jnp.where(jnp.logical_or(row == i1, row == i2), 1.0, 0.0)
    onehot = jnp.concatenate([jnp.where(g_idx == float(g), pick, 0.0) for g in range(N_EGROUPS)], axis=0)
    onehot_bf = onehot.astype(BF16)
    cb = triu_ref.shape[0]
    cums = []
    for blk in range(n // cb):
        c = _dot(onehot_bf[:, blk * cb:(blk + 1) * cb], triu_ref[...]) + carry
        carry = c[:, cb - 1:cb]
        cums.append(c)
    before = jnp.concatenate(cums, axis=1) - onehot
    erow = lax.broadcasted_iota(jnp.int32, (N_EXPERTS, n), 0).astype(F32)
    rank1 = jnp.sum(jnp.where(erow == e1, before, 0.0), axis=0, keepdims=True)
    rank2 = jnp.sum(jnp.where(erow == e2, before, 0.0), axis=0, keepdims=True)
    fields = jnp.concatenate([e1, e2, w1, w2, rank1, rank2], axis=0)
    routet_ref[0:fields.shape[0], :] = fields
    routet_ref[fields.shape[0]:, :] = jnp.zeros((routet_ref.shape[0] - fields.shape[0], n), F32)
    if n % LANES:
        fields = jnp.concatenate([fields, jnp.zeros((fields.shape[0], LANES - n % LANES), F32)], axis=1)
    pad = jnp.zeros((LANES - fields.shape[0], LANES), F32)
    for j in range(fields.shape[1] // LANES):
        blk_rows = jnp.concatenate([fields[:, j * LANES:(j + 1) * LANES], pad], axis=0)
        n_valid = min(LANES, n - j * LANES)
        route_ref[j * LANES:j * LANES + n_valid, :] = blk_rows.T[:n_valid, :]
    return carry


def _outproj_router(x2d, attn, ssm, w, tm):
    t = x2d.shape[0]
    tok = lambda i: (i, 0)
    const = lambda i: (0, 0)
    full = lambda a: pl.BlockSpec(a.shape, const)
    idx = jnp.arange(min(tm, COUNT_BLOCK))
    tri = (idx[:, None] <= idx[None, :]).astype(BF16)
    return pl.pallas_call(
        _outproj_router_kernel,
        out_shape=(jax.ShapeDtypeStruct((t, D_MODEL), F32), jax.ShapeDtypeStruct((t, D_MODEL // 2), jnp.uint32),
                   jax.ShapeDtypeStruct((t, LANES), F32), jax.ShapeDtypeStruct((8, t), F32),
                   jax.ShapeDtypeStruct((8, LANES), F32)),
        grid=(t // tm,),
        in_specs=[
            pl.BlockSpec(memory_space=pl.ANY), pl.BlockSpec((tm, ATTN_WIDTH), tok),
            pl.BlockSpec((tm, SSM_WIDTH), tok),
            full(w["w_out"]), full(w["norm2"]), full(w["wrt"]),
            full(w["br_col"]), full(tri),
        ],
        out_specs=(pl.BlockSpec((tm, D_MODEL), tok), pl.BlockSpec((tm, D_MODEL // 2), tok),
                   pl.BlockSpec((tm, LANES), tok), pl.BlockSpec((8, tm), lambda i: (0, i)),
                   pl.BlockSpec((8, LANES), const)),
        scratch_shapes=[pltpu.VMEM((N_EXPERTS, LANES), F32), pltpu.VMEM((X_SLOTS, tm, D_MODEL), F32),
                        pltpu.SemaphoreType.DMA((X_SLOTS,))],
        compiler_params=pltpu.CompilerParams(dimension_semantics=("arbitrary",),
                                             vmem_limit_bytes=VMEM_LIMIT),
        name="outproj_router",
    )(x2d, attn, ssm, w["w_out"], w["norm2"], w["wrt"], w["br_col"], tri)


def _expert_hidden(t_a, t_b, w1):
    half = D_MODEL // 2
    gu = _dot(t_a, w1[:half]) + _dot(t_b, w1[half:])
    return _silu(gu[:, :EXPERT_FF]) * gu[:, EXPERT_FF:]


def _gate_up_bf16(wg_ref, wu_ref):
    return jnp.concatenate([wg_ref[0].astype(BF16), wu_ref[0].astype(BF16)], axis=1)


def _moe_dense_kernel(h_ref, t_ref, route_ref, wg_ref, wu_ref, wd_ref, y_ref):
    e = pl.program_id(0)

    @pl.when(e == 0)
    def _():
        y_ref[...] = h_ref[...]

    t_a, t_b = _unpack_bf16_pair(t_ref[...])
    route = route_ref[...]
    e1, e2, g1, g2 = route[:, 0:1], route[:, 1:2], route[:, 2:3], route[:, 3:4]
    e_f = e.astype(F32)
    hid = _expert_hidden(t_a.astype(BF16), t_b.astype(BF16), _gate_up_bf16(wg_ref, wu_ref))
    c_e = jnp.where(e1 == e_f, g1, 0.0) + jnp.where(e2 == e_f, g2, 0.0)
    y_ref[...] += _dot((hid * c_e).astype(BF16), wd_ref[0].astype(BF16))


def _moe_dense(h, t, route, w):
    n = h.shape[0]
    whole = lambda e: (0, 0)
    by_expert = lambda e: (e, 0, 0)
    return pl.pallas_call(
        _moe_dense_kernel,
        out_shape=jax.ShapeDtypeStruct((n, D_MODEL), F32),
        grid=(N_EXPERTS,),
        in_specs=[pl.BlockSpec((n, D_MODEL), whole), pl.BlockSpec((n, D_MODEL // 2), whole),
                  pl.BlockSpec((n, LANES), whole),
                  pl.BlockSpec((1, D_MODEL, EXPERT_FF), by_expert), pl.BlockSpec((1, D_MODEL, EXPERT_FF), by_expert),
                  pl.BlockSpec((1, EXPERT_FF, D_MODEL), by_expert)],
        out_specs=pl.BlockSpec((n, D_MODEL), whole),
        compiler_params=pltpu.CompilerParams(dimension_semantics=("arbitrary",),
                                             vmem_limit_bytes=VMEM_LIMIT),
        name="moe_dense",
    )(h, t, route, w["w_gate"], w["w_up"], w["w_down"])


def _sc_scatter_rows(src, pos1, pos2, n_out):
    t, width = src.shape
    rows_per_worker = t // SC_WORKERS
    n_chunks = rows_per_worker // SC_CHUNK
    assert t == SC_WORKERS * SC_CHUNK * n_chunks and n_chunks % 2 == 0
    mesh = plsc.VectorSubcoreMesh(core_axis_name="c", subcore_axis_name="s")

    @functools.partial(
        pl.kernel, mesh=mesh,
        out_type=jax.ShapeDtypeStruct((n_out, width), src.dtype),
        scratch_types=[pltpu.VMEM((2, SC_CHUNK), jnp.int32), pltpu.VMEM((2, SC_CHUNK), jnp.int32),
                       pltpu.VMEM((2, SC_CHUNK, width), src.dtype),
                       pltpu.SemaphoreType.DMA, pltpu.SemaphoreType.DMA,
                       pltpu.SemaphoreType.DMA, pltpu.SemaphoreType.DMA],
    )
    def scatter_kernel(src_hbm, p1_hbm, p2_hbm, out_hbm, i1_v, i2_v, rows_v, l0, l1, s0, s1):
        wid = lax.axis_index("s") * SC_CORES + lax.axis_index("c")
        base = wid * rows_per_worker
        lsem = (l0, l1)
        ssem = (s0, s1)

        def load_copy(c, slot):
            off = pl.multiple_of(base + c * SC_CHUNK, 8)
            return pltpu.make_async_copy(src_hbm.at[pl.ds(off, SC_CHUNK)], rows_v.at[slot], lsem[slot])

        def start_load(c, slot):
            off = pl.multiple_of(base + c * SC_CHUNK, 8)
            pltpu.sync_copy(p1_hbm.at[pl.ds(off, SC_CHUNK)], i1_v.at[slot])
            pltpu.sync_copy(p2_hbm.at[pl.ds(off, SC_CHUNK)], i2_v.at[slot])
            load_copy(c, slot).start()

        def scatter_copies(slot):
            return (pltpu.make_async_copy(rows_v.at[slot], out_hbm.at[i1_v.at[slot]], ssem[slot]),
                    pltpu.make_async_copy(rows_v.at[slot], out_hbm.at[i2_v.at[slot]], ssem[slot]))

        def start_scatter(slot):
            for cp in scatter_copies(slot):
                cp.start()

        def wait_scatter(slot):
            for cp in scatter_copies(slot):
                cp.wait()

        start_load(0, 0)

        @pl.loop(0, n_chunks, step=2)
        def _(c):
            @pl.when(c > 0)
            def _():
                wait_scatter(1)

            start_load(c + 1, 1)
            load_copy(c, 0).wait()
            start_scatter(0)
            load_copy(c + 1, 1).wait()
            wait_scatter(0)
            start_scatter(1)

            @pl.when(c + 2 < n_chunks)
            def _():
                start_load(c + 2, 0)

        wait_scatter(1)

    return scatter_kernel(src, pos1, pos2)


def _sc_gather_rows(table, idx):
    n, width = idx.shape[0], table.shape[1]
    rows_per_worker = n // SC_WORKERS
    n_chunks = rows_per_worker // SC_CHUNK
    assert n == SC_WORKERS * SC_CHUNK * n_chunks and n_chunks % 2 == 0
    mesh = plsc.VectorSubcoreMesh(core_axis_name="c", subcore_axis_name="s")

    @functools.partial(
        pl.kernel, mesh=mesh,
        out_type=jax.ShapeDtypeStruct((n, width), table.dtype),
        scratch_types=[pltpu.VMEM((2, SC_CHUNK), jnp.int32), pltpu.VMEM((2, SC_CHUNK, width), table.dtype),
                       pltpu.SemaphoreType.DMA, pltpu.SemaphoreType.DMA,
                       pltpu.SemaphoreType.DMA, pltpu.SemaphoreType.DMA],
    )
    def gather_kernel(table_hbm, idx_hbm, out_hbm, idx_v, rows_v, g0, g1, w0, w1):
        wid = lax.axis_index("s") * SC_CORES + lax.axis_index("c")
        base = wid * rows_per_worker
        gsem = (g0, g1)
        wsem = (w0, w1)

        def gather_copy(slot):
            return pltpu.make_async_copy(table_hbm.at[idx_v.at[slot]], rows_v.at[slot], gsem[slot])

        def write_copy(c, slot):
            off = pl.multiple_of(base + c * SC_CHUNK, 8)
            return pltpu.make_async_copy(rows_v.at[slot], out_hbm.at[pl.ds(off, SC_CHUNK)], wsem[slot])

        def start_gather(c, slot):
            off = pl.multiple_of(base + c * SC_CHUNK, 8)
            pltpu.sync_copy(idx_hbm.at[pl.ds(off, SC_CHUNK)], idx_v.at[slot])
            gather_copy(slot).start()

        start_gather(0, 0)

        @pl.loop(0, n_chunks, step=2)
        def _(c):
            @pl.when(c > 0)
            def _():
                write_copy(c - 1, 1).wait()

            start_gather(c + 1, 1)
            gather_copy(0).wait()
            write_copy(c, 0).start()
            gather_copy(1).wait()
            write_copy(c + 1, 1).start()
            write_copy(c, 0).wait()

            @pl.when(c + 2 < n_chunks)
            def _():
                start_gather(c + 2, 0)

        write_copy(n_chunks - 1, 1).wait()

    return gather_kernel(table, idx)


def _moe_grouped_kernel(te_ref, nt_ref, order_ref, x_hbm, wg_ref, wu_ref, wd_ref, o_ref,
                        w1_bf_ref, w2_bf_ref, xbuf_ref, xsem_ref):
    del order_ref
    i = pl.program_id(0)
    n_used = nt_ref[0]

    def x_copy(tile):
        slot = lax.rem(tile, X_SLOTS)
        return pltpu.make_async_copy(x_hbm.at[pl.ds(pl.multiple_of(tile * MOE_TILE, 8), MOE_TILE)],
                                     xbuf_ref.at[slot], xsem_ref.at[slot])

    @pl.when(i == 0)
    def _():
        for tile in range(X_SLOTS - 1):
            pl.when(tile < n_used)(lambda tile=tile: x_copy(tile).start())

    @pl.when(i + (X_SLOTS - 1) < n_used)
    def _():
        x_copy(i + (X_SLOTS - 1)).start()

    @pl.when(jnp.logical_or(i == 0, te_ref[i] != te_ref[jnp.maximum(i - 1, 0)]))
    def _():
        w1_bf_ref[...] = _gate_up_bf16(wg_ref, wu_ref)
        w2_bf_ref[...] = wd_ref[0].astype(BF16)

    @pl.when(i < nt_ref[0])
    def _():
        x_copy(i).wait()
        x_ref = xbuf_ref.at[lax.rem(i, X_SLOTS)]
        for s in range(MOE_TILE // MOE_SUBTILE):
            rows = slice(s * MOE_SUBTILE, (s + 1) * MOE_SUBTILE)
            t_a, t_b = _unpack_bf16_pair(x_ref[rows, :])
            hid = _expert_hidden(t_a.astype(BF16), t_b.astype(BF16), w1_bf_ref[...])
            o_ref[rows, :] = _pack_bf16_pair(_dot(hid.astype(BF16), w2_bf_ref[...]))


def _moe_grouped(xs, tile_expert, n_tiles, order, w):
    rows = xs.shape[0]
    row = lambda i, te, nt, od: (jnp.minimum(i, nt[0] - 1), 0)
    by_expert = lambda i, te, nt, od: (te[i], 0, 0)
    return pl.pallas_call(
        _moe_grouped_kernel,
        out_shape=jax.ShapeDtypeStruct((rows, D_MODEL // 2), jnp.uint32),
        grid_spec=pltpu.PrefetchScalarGridSpec(
            num_scalar_prefetch=3,
            grid=(rows // MOE_TILE,),
            in_specs=[pl.BlockSpec(memory_space=pl.ANY),
                      pl.BlockSpec((1, D_MODEL, EXPERT_FF), by_expert),
                      pl.BlockSpec((1, D_MODEL, EXPERT_FF), by_expert),
                      pl.BlockSpec((1, EXPERT_FF, D_MODEL), by_expert)],
            out_specs=pl.BlockSpec((MOE_TILE, D_MODEL // 2), row),
            scratch_shapes=[pltpu.VMEM((D_MODEL, 2 * EXPERT_FF), BF16), pltpu.VMEM((EXPERT_FF, D_MODEL), BF16),
                            pltpu.VMEM((X_SLOTS, MOE_TILE, D_MODEL // 2), jnp.uint32),
                            pltpu.SemaphoreType.DMA((X_SLOTS,))],
        ),
        compiler_params=pltpu.CompilerParams(dimension_semantics=("arbitrary",),
                                             vmem_limit_bytes=VMEM_LIMIT),
        name="moe_grouped",
    )(tile_expert, n_tiles, order, xs, w["w_gate"], w["w_up"], w["w_down"])


def _moe_combine_kernel(h_ref, z1_ref, z2_ref, route_ref, y_ref):
    route = route_ref[...]
    g1, g2 = route[:, 2:3], route[:, 3:4]
    half = D_MODEL // 2
    a1, b1 = _unpack_bf16_pair(z1_ref[...])
    a2, b2 = _unpack_bf16_pair(z2_ref[...])
    y_ref[:, :half] = h_ref[:, :half] + g1 * a1 + g2 * a2
    y_ref[:, half:] = h_ref[:, half:] + g1 * b1 + g2 * b2


def _moe_combine(h, z, route, tm):
    t = h.shape[0]
    nb = t // tm
    tok = lambda i: (i, 0)
    return pl.pallas_call(
        _moe_combine_kernel,
        out_shape=jax.ShapeDtypeStruct((t, D_MODEL), F32),
        grid=(nb,),
        in_specs=[pl.BlockSpec((tm, D_MODEL), tok), pl.BlockSpec((tm, D_MODEL // 2), tok),
                  pl.BlockSpec((tm, D_MODEL // 2), lambda i: (i + nb, 0)), pl.BlockSpec((tm, LANES), tok)],
        out_specs=pl.BlockSpec((tm, D_MODEL), tok),
        compiler_params=pltpu.CompilerParams(dimension_semantics=("parallel",),
                                             vmem_limit_bytes=VMEM_LIMIT),
        name="moe_combine",
    )(h, z, z, route)


def _route_pos_kernel(routet_ref, cnt_ref, upper_ref, pos_ref):
    tm = routet_ref.shape[1]
    cnt = cnt_ref[...]
    padded = jnp.floor((cnt + float(MOE_TILE - 1)) * (1.0 / MOE_TILE)) * float(MOE_TILE)
    p_hi, p_mid, p_lo = _split3(padded)
    upper = upper_ref[...]
    starts = (_dot(p_hi, upper) + _dot(p_mid, upper) + _dot(p_lo, upper))[0:1, :]
    starts_col = jnp.broadcast_to(starts, (LANES, LANES)).T[EXPERT_LANE0:EXPERT_LANE0 + N_EXPERTS, 0:1]
    erow = lax.broadcasted_iota(jnp.int32, (N_EXPERTS, tm), 0).astype(F32)
    for k in range(2):
        e_k = routet_ref[k:k + 1, :]
        pos = (jnp.sum(jnp.where(erow == e_k, starts_col, 0.0), axis=0, keepdims=True)
               + routet_ref[4 + k:5 + k, :]).astype(jnp.int32)
        for r in range(tm // LANES):
            pos_ref[k, r:r + 1, :] = pos[:, r * LANES:(r + 1) * LANES]


def _route_positions(routet, counts, tm):
    t = routet.shape[1]
    idx = jnp.arange(LANES)
    upper = (idx[:, None] < idx[None, :]).astype(BF16)
    return pl.pallas_call(
        _route_pos_kernel,
        out_shape=jax.ShapeDtypeStruct((2, t // LANES, LANES), jnp.int32),
        grid=(t // tm,),
        in_specs=[pl.BlockSpec((8, tm), lambda i: (0, i)), pl.BlockSpec((8, LANES), lambda i: (0, 0)),
                  pl.BlockSpec((LANES, LANES), lambda i: (0, 0))],
        out_specs=pl.BlockSpec((2, tm // LANES, LANES), lambda i: (0, i, 0)),
        compiler_params=pltpu.CompilerParams(dimension_semantics=("parallel",),
                                             vmem_limit_bytes=VMEM_LIMIT),
        name="route_positions",
    )(routet, counts, upper)


def _moe_routed(h, t_packed, route, routet, counts, w, tm, run_first):
    t = h.shape[0]
    pos = _route_positions(routet, counts, min(32 * LANES, t))
    pos1 = pos[0].reshape(t)
    pos2 = pos[1].reshape(t)
    cnt = counts[0, EXPERT_LANE0:EXPERT_LANE0 + N_EXPERTS].astype(jnp.int32)
    padded = (cnt + MOE_TILE - 1) // MOE_TILE * MOE_TILE
    ends = jnp.cumsum(padded)
    n_rows = 2 * t + N_EXPERTS * MOE_TILE
    n_tiles = ends[N_EXPERTS - 1] // MOE_TILE
    tile_start = jnp.arange(n_rows // MOE_TILE, dtype=jnp.int32) * MOE_TILE
    tile_start = jnp.minimum(tile_start, ends[N_EXPERTS - 1] - MOE_TILE)
    tile_expert = jnp.sum((tile_start[:, None] >= ends[None, :]).astype(jnp.int32), axis=1)
    xs = _sc_scatter_rows(t_packed, pos1, pos2, n_rows)
    order = lax.bitcast_convert_type(run_first.reshape(-1)[:1].astype(F32), jnp.int32)
    out = _moe_grouped(xs, tile_expert, n_tiles.reshape(1), order, w)
    z = _sc_gather_rows(out, pos.reshape(2 * t))
    return _moe_combine(h, z, route, tm)


def _pad_lanes(a, width=LANES):
    return jnp.pad(a, ((0, 0), (0, width - a.shape[1])))


def _prep_weights(norm1, w_in, q_norm, k_norm, conv_w, conv_b, dt_bias, a_log, d_skip, ssm_norm, w_out,
                  norm2, w_grp, b_grp, w_exp, b_exp, w_gate, w_up, w_down):
    w = {}
    w["norm1"] = norm1.reshape(1, D_MODEL)
    w["w_in"] = _pad_lanes(w_in, XBC_END + LANES).astype(BF16)
    w["qkn"] = jnp.concatenate([jnp.tile(q_norm, N_HEADS), jnp.tile(k_norm, KV_HEADS)]).reshape(1, QK_WIDTH)
    head_of_col = jnp.arange(QK_WIDTH) // HEAD_DIM
    red = (head_of_col[:, None] == jnp.arange(LANES)[None, :])
    w["red"] = red.astype(BF16)
    w["exp"] = red.T.astype(BF16)
    w["conv_w"] = conv_w
    w["conv_b"] = conv_b.reshape(1, CONV_DIM)
    w["dt_bias"] = _pad_lanes(dt_bias.reshape(1, SSM_HEADS))
    w["a_log"] = _pad_lanes(a_log.reshape(1, SSM_HEADS))
    w["d_skip"] = jnp.repeat(d_skip, SSM_HEAD_DIM).reshape(1, SSM_WIDTH)
    w["ssm_norm"] = ssm_norm.reshape(1, SSM_WIDTH)
    idx = jnp.arange(SSD_CHUNK)
    w["tri"] = (idx[None, :] <= idx[:, None]).astype(BF16)
    lane_head = jnp.arange(SSM_WIDTH) // SSM_HEAD_DIM
    w["expand"] = (jnp.arange(LANES)[:, None] == lane_head[None, :]).astype(BF16)
    w["w_out"] = w_out.astype(BF16)
    w["norm2"] = norm2.reshape(1, D_MODEL)
    wr = jnp.zeros((D_MODEL, LANES), F32)
    wr = wr.at[:, :N_EGROUPS].set(w_grp).at[:, EXPERT_LANE0:EXPERT_LANE0 + N_EXPERTS].set(w_exp)
    wr_hi = wr.astype(BF16)
    w["wrt"] = jnp.concatenate([wr_hi, (wr - wr_hi.astype(F32)).astype(BF16)], axis=1).T
    br = jnp.zeros((LANES, 1), F32)
    w["br_col"] = br.at[:N_EGROUPS, 0].set(b_grp).at[EXPERT_LANE0:EXPERT_LANE0 + N_EXPERTS, 0].set(b_exp)
    w["w_gate"], w["w_up"], w["w_down"] = w_gate, w_up, w_down
    return w


def _rope_tables(pos):
    inv = 1.0 / (ROPE_THETA ** (jnp.arange(0, HEAD_DIM, 2, dtype=F32) / HEAD_DIM))
    ang = pos.astype(F32)[:, None] * inv[None, :]
    cos, sin = jnp.cos(ang), jnp.sin(ang)
    reps = LANES // HEAD_DIM
    return (jnp.tile(jnp.concatenate([cos, cos], axis=-1), (1, reps)),
            jnp.tile(jnp.concatenate([-sin, sin], axis=-1), (1, reps)))


def _token_tile(t):
    for tm in (1024, 512, 256, 128, 64, 32, 16):
        if t % tm == 0:
            return tm
    raise ValueError(f"token count {t} is not a multiple of 16")


def kernel(x_prompt, x_sample, cache_win_k, cache_win_v, state_conv, state_ssm, norm1, w_in, q_norm, k_norm,
           sinks, conv_w, conv_b, dt_bias, a_log, d_skip, ssm_norm, w_out, norm2, w_grp, b_grp, w_exp, b_exp,
           w_gate, w_up, w_down):
    depth = norm1.shape[0]
    assert depth == 1, "single-layer stack"
    bp, lp, _ = x_prompt.shape
    bsn, ls, _ = x_sample.shape
    assert ls == 1 and lp % WINDOW == 0 and cache_win_k.shape[2] == WINDOW
    l = 0
    w = _prep_weights(norm1[l], w_in[l], q_norm[l], k_norm[l], conv_w[l], conv_b[l], dt_bias[l], a_log[l],
                      d_skip[l], ssm_norm[l], w_out[l], norm2[l], w_grp[l], b_grp[l], w_exp[l], b_exp[l],
                      w_gate[l], w_up[l], w_down[l])
    sink = sinks[l]

    tp = bp * lp
    xp = x_prompt.reshape(tp, D_MODEL)
    tm_p = _token_tile(lp)
    cos_p, sin_p = _rope_tables(jnp.arange(lp, dtype=jnp.int32))
    q, k, v, z, xbc, dt, k2, v2 = _inproj(xp, w, cos_p, sin_p, tm_p, lp // tm_p)
    attn = _attn_prompt(q, k2, v2, sink, bp, lp)
    ssm, st_p = _ssd_prompt(xbc, z, dt, w, bp, lp)
    h, t, route, routet, counts = _outproj_router(xp, attn, ssm, w, tm_p)
    k3 = k.reshape(bp, lp, KV_HEADS, HEAD_DIM)
    v3 = v.reshape(bp, lp, KV_HEADS, HEAD_DIM)
    win_k_p = k3[:, lp - WINDOW:][None]
    win_v_p = v3[:, lp - WINDOW:][None]
    conv_p = xbc.reshape(bp, lp, CONV_DIM)[:, lp - (CONV_K - 1):][None]
    ssm_p = st_p.reshape(1, bp, SSM_HEADS, SSM_HEAD_DIM, SSM_STATE)

    xs2 = x_sample.reshape(bsn, D_MODEL)
    tm_s = _token_tile(bsn)
    cos_s, sin_s = _rope_tables(jnp.full((tm_s,), PAST_LEN, jnp.int32))
    q_s, k_s, v_s, z_s, xbc_s, dt_s, _, _ = _inproj(xs2, w, cos_s, sin_s, tm_s, 1)
    q4 = q_s.reshape(bsn, KV_HEADS, N_HEADS // KV_HEADS, HEAD_DIM)
    zq = jnp.zeros_like(q4[:, 0])
    qx = jnp.concatenate([jnp.concatenate([q4[:, 0], zq], axis=-1),
                          jnp.concatenate([zq, q4[:, 1]], axis=-1)], axis=1)
    qx = jnp.pad(qx, ((0, 0), (0, BF16_ROWS - N_HEADS), (0, 0)))
    sink_x = jnp.pad(jnp.broadcast_to(sink[:, None], (N_HEADS, LANES)), ((0, BF16_ROWS - N_HEADS), (0, 0)))
    kc = jnp.transpose(cache_win_k[l], (0, 2, 3, 1)).reshape(bsn, KV_WIDTH, WINDOW)
    vc = jnp.transpose(cache_win_v[l], (0, 2, 3, 1)).reshape(bsn, KV_WIDTH, WINDOW)
    ko, vo, attn_s = _attn_sample(qx, kc, k_s.reshape(bsn, KV_WIDTH, 1), vc, v_s.reshape(bsn, KV_WIDTH, 1),
                               sink_x, 8)
    y_prompt = _moe_routed(h, t, route, routet, counts, w, tm_p, attn_s).reshape(bp, lp, D_MODEL)
    cprev_t = jnp.transpose(state_conv[l], (1, 0, 2))
    h0 = state_ssm[l].reshape(bsn, HEAD_PAIRS, LANES, SSM_STATE)
    ssm_s, cnew_t, h1 = _ssd_sample(xbc_s, z_s, dt_s, cprev_t, h0, w, 16)
    h_s, t_s, route_s, _, _ = _outproj_router(xs2, attn_s, ssm_s, w, tm_s)
    y_sample = _moe_dense(h_s, t_s, route_s, w).reshape(bsn, 1, D_MODEL)
    win_k_s = jnp.transpose(ko.reshape(bsn, KV_HEADS, HEAD_DIM, WINDOW), (0, 3, 1, 2))[None]
    win_v_s = jnp.transpose(vo.reshape(bsn, KV_HEADS, HEAD_DIM, WINDOW), (0, 3, 1, 2))[None]
    conv_s = jnp.transpose(cnew_t, (1, 0, 2))[None]
    ssm_s_state = h1.reshape(1, bsn, SSM_HEADS, SSM_HEAD_DIM, SSM_STATE)

    return (y_prompt, y_sample, win_k_p, win_v_p, conv_p, ssm_p, win_k_s, win_v_s, conv_s, ssm_s_state)
```
